```python
import jax, jax.numpy as jnp
from jax import lax
import numpy as np

D_MODEL = 1024
BATCH = 8
SEQ = 8192
DEPTH = 4

CHUNK = 64
GM_BLOCK = 128
GM_HEADS = 4
GM_HEAD_DIM = 128
GM_DIM = GM_HEADS * GM_HEAD_DIM
HG_HEADS = 4
HG_DK = 128
HG_DV = 128
HG_FDIM = HG_HEADS * HG_DK
HG_IDIM = HG_HEADS * HG_DV
N_BRANCH = 2
D_FF = 2816
CONV_W = 3
EPS = 1e-6
TINY = 1e-30
LB_MAX = 0.999
IN_SIZES = (GM_DIM, GM_DIM, HG_FDIM, HG_FDIM, HG_IDIM, HG_IDIM, N_BRANCH * D_MODEL)
IN_COLS = 2 * GM_DIM + 2 * HG_FDIM + 2 * HG_IDIM + N_BRANCH * D_MODEL

kernel_name = "hybrid_gmlp_hgrn2_convffn_trunk"


def _split_cols(z, sizes):
    outs, start = [], 0
    for s in sizes:
        outs.append(z[..., start:start + s])
        start += s
    return outs


def _rmsnorm(x, g):
    xf = x.astype(jnp.float32)
    r = lax.rsqrt(jnp.mean(xf * xf, axis=-1, keepdims=True) + EPS)
    return (xf * r).astype(x.dtype) * g


def _layernorm(x, g, b):
    xf = x.astype(jnp.float32)
    mu = jnp.mean(xf, axis=-1, keepdims=True)
    var = jnp.mean(jnp.square(xf - mu), axis=-1, keepdims=True)
    return ((xf - mu) * lax.rsqrt(var + EPS)).astype(x.dtype) * g + b


def _gmlp_branch(u, v, ln_g, ln_b, ws, bs):
    B, S, _ = v.shape
    nb = S // GM_BLOCK
    v = _layernorm(v, ln_g, ln_b)
    vb = v.reshape(B, nb, GM_BLOCK, GM_HEADS, GM_HEAD_DIM)
    pos = jnp.arange(GM_BLOCK) // CHUNK
    mask = pos[:, None] >= pos[None, :]
    w = jnp.where(mask[None], ws, 0.0).astype(v.dtype)
    mixed = jnp.einsum('hij,bnjhd->bnihd', w, vb) + bs.T[None, None, :, :, None]
    return u * mixed.reshape(B, S, GM_DIM)


def _hgrn2_chunk_step(state, inp):
    q, k, g, v = inp
    C = q.shape[2]
    G = jnp.cumsum(g, axis=2)
    causal = jnp.tril(jnp.ones((C, C), dtype=bool))[None, None, :, :, None]
    diff = G[:, :, :, None, :] - G[:, :, None, :, :]
    decay = jnp.where(causal, jnp.exp(jnp.minimum(diff, 0.0)), 0.0)
    A = jnp.einsum('bhid,bhijd,bhjd->bhij', q, decay, k)
    o = jnp.einsum('bhij,bhje->bhie', A, v) + jnp.einsum('bhid,bhde->bhie', q * jnp.exp(G), state)
    G_last = G[:, :, -1:, :]
    state = jnp.exp(G_last[:, :, 0, :])[..., None] * state + \
        jnp.einsum('bhjd,bhje->bhde', k * jnp.exp(G_last - G), v)
    return state, o


def _hgrn2_branch(q, f, i, og, lb, norm_g):
    B, S, _ = q.shape
    nc = S // CHUNK
    out_dtype = i.dtype
    qf = jax.nn.silu(q.astype(jnp.float32))
    ff = f.astype(jnp.float32)
    lb = jnp.clip(lb.astype(jnp.float32), 0.0, LB_MAX)
    forget = lb + (1.0 - lb) * jax.nn.sigmoid(ff)
    g_log = jnp.log(jnp.maximum(forget, TINY))
    k = (1.0 - lb) * jax.nn.sigmoid(-ff)
    vf = i.astype(jnp.float32)

    def to_chunks(t, d):
        return t.reshape(B, nc, CHUNK, HG_HEADS, d).transpose(1, 0, 3, 2, 4)

    xs = (to_chunks(qf, HG_DK), to_chunks(k, HG_DK), to_chunks(g_log, HG_DK), to_chunks(vf, HG_DV))
    s0 = jnp.zeros((B, HG_HEADS, HG_DK, HG_DV), jnp.float32)
    _, o = lax.scan(_hgrn2_chunk_step, s0, xs)
    o = o.transpose(1, 0, 3, 2, 4).reshape(B, S, HG_HEADS, HG_DV).astype(out_dtype)
    o = _rmsnorm(o, norm_g) * jax.nn.silu(og.reshape(B, S, HG_HEADS, HG_DV))
    return o.reshape(B, S, HG_IDIM)


def _conv_ffn(h, w_up, conv_w, conv_b, w_down):
    S = h.shape[1]
    z = h @ w_up
    zp = jnp.pad(z, ((0, 0), (CONV_W - 1, 0), (0, 0)))
    zc = conv_b + sum(zp[:, t:t + S, :] * conv_w[t] for t in range(CONV_W))
    gate, val = zc[..., :D_FF], zc[..., D_FF:]
    return (jax.nn.silu(gate) * val) @ w_down


def _fwd_setup_inputs(seed: int = 0) -> dict:
    key = jax.random.key(seed)
    ks = jax.random.split(key, 20)
    f32 = jnp.float32

    def nrm(k, shape, scale):
        return jax.random.normal(k, shape, f32) * scale

    return {
        "x": nrm(ks[0], (BATCH, SEQ, D_MODEL), 1.0),
        "mix_norm": 1.0 + nrm(ks[1], (DEPTH, D_MODEL), 0.05),
        "w_in": nrm(ks[2], (DEPTH, D_MODEL, IN_COLS), D_MODEL ** -0.5),
        "gm_ln_g": 1.0 + nrm(ks[3], (DEPTH, GM_DIM), 0.05),
        "gm_ln_b": nrm(ks[4], (DEPTH, GM_DIM), 0.02),
        "gm_ws": nrm(ks[5], (DEPTH, GM_HEADS, GM_BLOCK, GM_BLOCK), GM_BLOCK ** -0.5),
        "gm_bs": 1.0 + nrm(ks[6], (DEPTH, GM_HEADS, GM_BLOCK), 0.1),
        "hg_lb_logits": nrm(ks[7], (DEPTH, HG_FDIM), 0.5),
        "hg_norm_g": 1.0 + nrm(ks[8], (DEPTH, HG_DV), 0.05),
        "w_br_gm": nrm(ks[9], (DEPTH, GM_DIM, D_MODEL), GM_DIM ** -0.5),
        "w_br_hg": nrm(ks[10], (DEPTH, HG_IDIM, D_MODEL), HG_IDIM ** -0.5),
        "w_out": nrm(ks[11], (DEPTH, D_MODEL, D_MODEL), D_MODEL ** -0.5),
        "ffn_norm": 1.0 + nrm(ks[12], (DEPTH, D_MODEL), 0.05),
        "w_up": nrm(ks[13], (DEPTH, D_MODEL, 2 * D_FF), D_MODEL ** -0.5),
        "conv_w": nrm(ks[14], (DEPTH, CONV_W, 2 * D_FF), CONV_W ** -0.5),
        "conv_b": nrm(ks[15], (DEPTH, 2 * D_FF), 0.02),
        "w_down": nrm(ks[16], (DEPTH, D_FF, D_MODEL), D_FF ** -0.5),
        "final_norm": 1.0 + nrm(ks[17], (D_MODEL,), 0.05),
    }


def _fwd_reference(x, mix_norm, w_in, gm_ln_g, gm_ln_b, gm_ws, gm_bs, hg_lb_logits, hg_norm_g,
              w_br_gm, w_br_hg, w_out, ffn_norm, w_up, conv_w, conv_b, w_down, final_norm):
    p = jax.nn.softmax(hg_lb_logits.astype(jnp.float32), axis=0)
    lower_bounds = jnp.cumsum(p, axis=0) - p[0:1]
    for l in range(DEPTH):
        h = _rmsnorm(x, mix_norm[l])
        z = h @ w_in[l]
        u, v, q, f, i, og, gates = _split_cols(z, IN_SIZES)
        a = _gmlp_branch(jax.nn.gelu(u, approximate=False), jax.nn.gelu(v, approximate=False),
                         gm_ln_g[l], gm_ln_b[l], gm_ws[l], gm_bs[l])
        b = _hgrn2_branch(q, f, i, og, lower_bounds[l], hg_norm_g[l])
        gate_a, gate_b = gates[..., :D_MODEL], gates[..., D_MODEL:]
        y = jax.nn.sigmoid(gate_a) * (a @ w_br_gm[l]) + jax.nn.sigmoid(gate_b) * (b @ w_br_hg[l])
        x = x + y @ w_out[l]
        h = _rmsnorm(x, ffn_norm[l])
        x = x + _conv_ffn(h, w_up[l], conv_w[l], conv_b[l], w_down[l])
    return _rmsnorm(x, final_norm)


import jax as _jax
import jax.numpy as _jnp

TWIN_FORMAT = 'train_step'
FWD_PARAMS = ['x', 'mix_norm', 'w_in', 'gm_ln_g', 'gm_ln_b', 'gm_ws', 'gm_bs', 'hg_lb_logits', 'hg_norm_g', 'w_br_gm', 'w_br_hg', 'w_out', 'ffn_norm', 'w_up', 'conv_w', 'conv_b', 'w_down', 'final_norm']
TWIN_WEIGHTS = ['mix_norm', 'w_in', 'gm_ln_g', 'gm_ln_b', 'gm_ws', 'gm_bs', 'hg_lb_logits', 'hg_norm_g', 'w_br_gm', 'w_br_hg', 'w_out', 'ffn_norm', 'w_up', 'conv_w', 'conv_b', 'w_down', 'final_norm']
TWIN_DIFF_INPUT = 'x'
TWIN_INPUTS = ['x', 'mix_norm', 'w_in', 'gm_ln_g', 'gm_ln_b', 'gm_ws', 'gm_bs', 'hg_lb_logits', 'hg_norm_g', 'w_br_gm', 'w_br_hg', 'w_out', 'ffn_norm', 'w_up', 'conv_w', 'conv_b', 'w_down', 'final_norm', 'loss_target', 'm_mix_norm', 'm_w_in', 'm_gm_ln_g', 'm_gm_ln_b', 'm_gm_ws', 'm_gm_bs', 'm_hg_lb_logits', 'm_hg_norm_g', 'm_w_br_gm', 'm_w_br_hg', 'm_w_out', 'm_ffn_norm', 'm_w_up', 'm_conv_w', 'm_conv_b', 'm_w_down', 'm_final_norm', 'v_mix_norm', 'v_w_in', 'v_gm_ln_g', 'v_gm_ln_b', 'v_gm_ws', 'v_gm_bs', 'v_hg_lb_logits', 'v_hg_norm_g', 'v_w_br_gm', 'v_w_br_hg', 'v_w_out', 'v_ffn_norm', 'v_w_up', 'v_conv_w', 'v_conv_b', 'v_w_down', 'v_final_norm']
TWIN_OUTPUTS = ['loss', 'grad_x', 'grad_mix_norm', 'grad_w_in', 'grad_gm_ln_g', 'grad_gm_ln_b', 'grad_gm_ws', 'grad_gm_bs', 'grad_hg_lb_logits', 'grad_hg_norm_g', 'grad_w_br_gm', 'grad_w_br_hg', 'grad_w_out', 'grad_ffn_norm', 'grad_w_up', 'grad_conv_w', 'grad_conv_b', 'grad_w_down', 'grad_final_norm', 'delta_mix_norm', 'delta_w_in', 'delta_gm_ln_g', 'delta_gm_ln_b', 'delta_gm_ws', 'delta_gm_bs', 'delta_hg_lb_logits', 'delta_hg_norm_g', 'delta_w_br_gm', 'delta_w_br_hg', 'delta_w_out', 'delta_ffn_norm', 'delta_w_up', 'delta_conv_w', 'delta_conv_b', 'delta_w_down', 'delta_final_norm', 'new_m_mix_norm', 'new_m_w_in', 'new_m_gm_ln_g', 'new_m_gm_ln_b', 'new_m_gm_ws', 'new_m_gm_bs', 'new_m_hg_lb_logits', 'new_m_hg_norm_g', 'new_m_w_br_gm', 'new_m_w_br_hg', 'new_m_w_out', 'new_m_ffn_norm', 'new_m_w_up', 'new_m_conv_w', 'new_m_conv_b', 'new_m_w_down', 'new_m_final_norm', 'new_v_mix_norm', 'new_v_w_in', 'new_v_gm_ln_g', 'new_v_gm_ln_b', 'new_v_gm_ws', 'new_v_gm_bs', 'new_v_hg_lb_logits', 'new_v_hg_norm_g', 'new_v_w_br_gm', 'new_v_w_br_hg', 'new_v_w_out', 'new_v_ffn_norm', 'new_v_w_up', 'new_v_conv_w', 'new_v_conv_b', 'new_v_w_down', 'new_v_final_norm']
TWIN_LEAF_KINDS = {'loss': 'loss', 'grad_x': 'grad_x', 'grad_mix_norm': 'grad_w', 'grad_w_in': 'grad_w', 'grad_gm_ln_g': 'grad_w', 'grad_gm_ln_b': 'grad_w', 'grad_gm_ws': 'grad_w', 'grad_gm_bs': 'grad_w', 'grad_hg_lb_logits': 'grad_w', 'grad_hg_norm_g': 'grad_w', 'grad_w_br_gm': 'grad_w', 'grad_w_br_hg': 'grad_w', 'grad_w_out': 'grad_w', 'grad_ffn_norm': 'grad_w', 'grad_w_up': 'grad_w', 'grad_conv_w': 'grad_w', 'grad_conv_b': 'grad_w', 'grad_w_down': 'grad_w', 'grad_final_norm': 'grad_w', 'delta_mix_norm': 'delta_w', 'delta_w_in': 'delta_w', 'delta_gm_ln_g': 'delta_w', 'delta_gm_ln_b': 'delta_w', 'delta_gm_ws': 'delta_w', 'delta_gm_bs': 'delta_w', 'delta_hg_lb_logits': 'delta_w', 'delta_hg_norm_g': 'delta_w', 'delta_w_br_gm': 'delta_w', 'delta_w_br_hg': 'delta_w', 'delta_w_out': 'delta_w', 'delta_ffn_norm': 'delta_w', 'delta_w_up': 'delta_w', 'delta_conv_w': 'delta_w', 'delta_conv_b': 'delta_w', 'delta_w_down': 'delta_w', 'delta_final_norm': 'delta_w', 'new_m_mix_norm': 'new_m', 'new_m_w_in': 'new_m', 'new_m_gm_ln_g': 'new_m', 'new_m_gm_ln_b': 'new_m', 'new_m_gm_ws': 'new_m', 'new_m_gm_bs': 'new_m', 'new_m_hg_lb_logits': 'new_m', 'new_m_hg_norm_g': 'new_m', 'new_m_w_br_gm': 'new_m', 'new_m_w_br_hg': 'new_m', 'new_m_w_out': 'new_m', 'new_m_ffn_norm': 'new_m', 'new_m_w_up': 'new_m', 'new_m_conv_w': 'new_m', 'new_m_conv_b': 'new_m', 'new_m_w_down': 'new_m', 'new_m_final_norm': 'new_m', 'new_v_mix_norm': 'new_v', 'new_v_w_in': 'new_v', 'new_v_gm_ln_g': 'new_v', 'new_v_gm_ln_b': 'new_v', 'new_v_gm_ws': 'new_v', 'new_v_gm_bs': 'new_v', 'new_v_hg_lb_logits': 'new_v', 'new_v_hg_norm_g': 'new_v', 'new_v_w_br_gm': 'new_v', 'new_v_w_br_hg': 'new_v', 'new_v_w_out': 'new_v', 'new_v_ffn_norm': 'new_v', 'new_v_w_up': 'new_v', 'new_v_conv_w': 'new_v', 'new_v_conv_b': 'new_v', 'new_v_w_down': 'new_v', 'new_v_final_norm': 'new_v'}


def _forward(args):
    return _fwd_reference(*[args[k] for k in FWD_PARAMS])


def _output_shape():
    def fwd():
        inp = _fwd_setup_inputs(0)
        return _fwd_reference(*[inp[k] for k in FWD_PARAMS])
    out = _jax.eval_shape(fwd)
    return out.shape, out.dtype

N_MICROBATCH = 1
ADAM_LR = 0.001
ADAM_B1 = 0.9
ADAM_B2 = 0.999
ADAM_EPS = 1e-08
ADAM_WD = 0.01
ADAM_STEP = 10
PER_EXAMPLE_BATCH_AXIS = {'x': 0, 'loss_target': 0}
SHARED_INPUTS = []
_WEIGHT_DTYPES = {'mix_norm': _jnp.float32, 'w_in': _jnp.float32, 'gm_ln_g': _jnp.float32, 'gm_ln_b': _jnp.float32, 'gm_ws': _jnp.float32, 'gm_bs': _jnp.float32, 'hg_lb_logits': _jnp.float32, 'hg_norm_g': _jnp.float32, 'w_br_gm': _jnp.float32, 'w_br_hg': _jnp.float32, 'w_out': _jnp.float32, 'ffn_norm': _jnp.float32, 'w_up': _jnp.float32, 'conv_w': _jnp.float32, 'conv_b': _jnp.float32, 'w_down': _jnp.float32, 'final_norm': _jnp.float32}
MOMENT_SCALE = {'mix_norm': 1.876792e-01, 'w_in': 8.405853e-02, 'gm_ln_g': 1.026215e-01, 'gm_ln_b': 9.802740e-02, 'gm_ws': 1.013742e-01, 'gm_bs': 1.248713e-01, 'hg_lb_logits': 6.791451e-03, 'hg_norm_g': 2.238221e-01, 'w_br_gm': 1.199634e-01, 'w_br_hg': 7.793303e-02, 'w_out': 1.428826e-01, 'ffn_norm': 1.793521e-01, 'w_up': 7.404855e-02, 'conv_w': 7.433834e-02, 'conv_b': 8.066627e-02, 'w_down': 1.218462e-01, 'final_norm': 6.421893e+01}


def _to_microbatches(a, axis):
    t = _jnp.moveaxis(a, axis, 0)
    t = t.reshape((N_MICROBATCH, t.shape[0] // N_MICROBATCH) + t.shape[1:])
    return _jnp.moveaxis(t, 1, axis + 1)


def setup_inputs(seed: int = 0) -> dict:
    inp = _fwd_setup_inputs(seed)
    key = _jax.random.fold_in(_jax.random.key(seed), 7919)
    shape, _ = _output_shape()
    out = dict(inp)
    out["loss_target"] = _jax.random.normal(_jax.random.fold_in(key, 0), shape, _jnp.float32)
    for i, name in enumerate(TWIN_WEIGHTS):
        w = inp[name].astype(_jnp.float32)
        if MOMENT_SCALE is None:
            s = _jnp.sqrt(_jnp.mean(_jnp.square(w)) + 1e-30)
        else:
            s = MOMENT_SCALE[name]
        km, kv = _jax.random.split(_jax.random.fold_in(key, i + 1))
        out[name] = w
        out["m_" + name] = s * _jax.random.normal(km, w.shape, _jnp.float32)
        out["v_" + name] = (s * s) * _jax.random.uniform(kv, w.shape, _jnp.float32, 0.5, 1.5)
    if N_MICROBATCH > 1:
        for name, axis in PER_EXAMPLE_BATCH_AXIS.items():
            out[name] = _to_microbatches(out[name], axis)
    return {'x': out['x'], 'mix_norm': out['mix_norm'], 'w_in': out['w_in'], 'gm_ln_g': out['gm_ln_g'], 'gm_ln_b': out['gm_ln_b'], 'gm_ws': out['gm_ws'], 'gm_bs': out['gm_bs'], 'hg_lb_logits': out['hg_lb_logits'], 'hg_norm_g': out['hg_norm_g'], 'w_br_gm': out['w_br_gm'], 'w_br_hg': out['w_br_hg'], 'w_out': out['w_out'], 'ffn_norm': out['ffn_norm'], 'w_up': out['w_up'], 'conv_w': out['conv_w'], 'conv_b': out['conv_b'], 'w_down': out['w_down'], 'final_norm': out['final_norm'], 'loss_target': out['loss_target'], 'm_mix_norm': out['m_mix_norm'], 'm_w_in': out['m_w_in'], 'm_gm_ln_g': out['m_gm_ln_g'], 'm_gm_ln_b': out['m_gm_ln_b'], 'm_gm_ws': out['m_gm_ws'], 'm_gm_bs': out['m_gm_bs'], 'm_hg_lb_logits': out['m_hg_lb_logits'], 'm_hg_norm_g': out['m_hg_norm_g'], 'm_w_br_gm': out['m_w_br_gm'], 'm_w_br_hg': out['m_w_br_hg'], 'm_w_out': out['m_w_out'], 'm_ffn_norm': out['m_ffn_norm'], 'm_w_up': out['m_w_up'], 'm_conv_w': out['m_conv_w'], 'm_conv_b': out['m_conv_b'], 'm_w_down': out['m_w_down'], 'm_final_norm': out['m_final_norm'], 'v_mix_norm': out['v_mix_norm'], 'v_w_in': out['v_w_in'], 'v_gm_ln_g': out['v_gm_ln_g'], 'v_gm_ln_b': out['v_gm_ln_b'], 'v_gm_ws': out['v_gm_ws'], 'v_gm_bs': out['v_gm_bs'], 'v_hg_lb_logits': out['v_hg_lb_logits'], 'v_hg_norm_g': out['v_hg_norm_g'], 'v_w_br_gm': out['v_w_br_gm'], 'v_w_br_hg': out['v_w_br_hg'], 'v_w_out': out['v_w_out'], 'v_ffn_norm': out['v_ffn_norm'], 'v_w_up': out['v_w_up'], 'v_conv_w': out['v_conv_w'], 'v_conv_b': out['v_conv_b'], 'v_w_down': out['v_w_down'], 'v_final_norm': out['v_final_norm']}


def _loss(weights, diff, rest, loss_target):
    with _jax.named_scope("forward"):
        args = {**rest, TWIN_DIFF_INPUT: diff, **{k: w.astype(_WEIGHT_DTYPES[k]) for k, w in weights.items()}}
        y = _forward(args)
    with _jax.named_scope("loss_head"):
        err = _jnp.square(y.astype(_jnp.float32) - loss_target)
        return 0.5 * _jnp.sum(_jnp.mean(err, axis=-1)) if err.ndim else 0.5 * err


def _adamw(w, g, m, v):
    m = ADAM_B1 * m + (1.0 - ADAM_B1) * g
    v = ADAM_B2 * v + (1.0 - ADAM_B2) * _jnp.square(g)
    m_hat = m / (1.0 - ADAM_B1 ** ADAM_STEP)
    v_hat = v / (1.0 - ADAM_B2 ** ADAM_STEP)
    delta = -ADAM_LR * (m_hat / (_jnp.sqrt(v_hat) + ADAM_EPS) + ADAM_WD * w)
    return delta, m, v


def reference(x, mix_norm, w_in, gm_ln_g, gm_ln_b, gm_ws, gm_bs, hg_lb_logits, hg_norm_g, w_br_gm, w_br_hg, w_out, ffn_norm, w_up, conv_w, conv_b, w_down, final_norm, loss_target, m_mix_norm, m_w_in, m_gm_ln_g, m_gm_ln_b, m_gm_ws, m_gm_bs, m_hg_lb_logits, m_hg_norm_g, m_w_br_gm, m_w_br_hg, m_w_out, m_ffn_norm, m_w_up, m_conv_w, m_conv_b, m_w_down, m_final_norm, v_mix_norm, v_w_in, v_gm_ln_g, v_gm_ln_b, v_gm_ws, v_gm_bs, v_hg_lb_logits, v_hg_norm_g, v_w_br_gm, v_w_br_hg, v_w_out, v_ffn_norm, v_w_up, v_conv_w, v_conv_b, v_w_down, v_final_norm):
    given = dict(x=x, mix_norm=mix_norm, w_in=w_in, gm_ln_g=gm_ln_g, gm_ln_b=gm_ln_b, gm_ws=gm_ws, gm_bs=gm_bs, hg_lb_logits=hg_lb_logits, hg_norm_g=hg_norm_g, w_br_gm=w_br_gm, w_br_hg=w_br_hg, w_out=w_out, ffn_norm=ffn_norm, w_up=w_up, conv_w=conv_w, conv_b=conv_b, w_down=w_down, final_norm=final_norm, loss_target=loss_target, m_mix_norm=m_mix_norm, m_w_in=m_w_in, m_gm_ln_g=m_gm_ln_g, m_gm_ln_b=m_gm_ln_b, m_gm_ws=m_gm_ws, m_gm_bs=m_gm_bs, m_hg_lb_logits=m_hg_lb_logits, m_hg_norm_g=m_hg_norm_g, m_w_br_gm=m_w_br_gm, m_w_br_hg=m_w_br_hg, m_w_out=m_w_out, m_ffn_norm=m_ffn_norm, m_w_up=m_w_up, m_conv_w=m_conv_w, m_conv_b=m_conv_b, m_w_down=m_w_down, m_final_norm=m_final_norm, v_mix_norm=v_mix_norm, v_w_in=v_w_in, v_gm_ln_g=v_gm_ln_g, v_gm_ln_b=v_gm_ln_b, v_gm_ws=v_gm_ws, v_gm_bs=v_gm_bs, v_hg_lb_logits=v_hg_lb_logits, v_hg_norm_g=v_hg_norm_g, v_w_br_gm=v_w_br_gm, v_w_br_hg=v_w_br_hg, v_w_out=v_w_out, v_ffn_norm=v_ffn_norm, v_w_up=v_w_up, v_conv_w=v_conv_w, v_conv_b=v_conv_b, v_w_down=v_w_down, v_final_norm=v_final_norm)
    weights = {n: given[n] for n in TWIN_WEIGHTS}
    shared = {n: given[n] for n in SHARED_INPUTS}
    per_example = {n: given[n] for n in ['x']}
    grad_fn = _jax.value_and_grad(_loss, argnums=(0, 1))

    def one_microbatch(ex, loss_target):
        ex = dict(ex)
        diff = ex.pop(TWIN_DIFF_INPUT)
        return grad_fn(weights, diff, {**shared, **ex}, loss_target)

    if N_MICROBATCH == 1:
        loss, (grad_w, grad_x) = one_microbatch(per_example, given["loss_target"])
    else:
        def body(carry, xs):
            loss_sum, grad_sum = carry
            l_k, (gw_k, gx_k) = one_microbatch(xs[0], xs[1])
            with _jax.named_scope("update"):
                return (loss_sum + l_k, _jax.tree.map(_jnp.add, grad_sum, gw_k)), gx_k

        init = (_jnp.zeros((), _jnp.float32), _jax.tree.map(_jnp.zeros_like, weights))
        (loss, grad_w), grad_x = _jax.lax.scan(body, init, (per_example, given["loss_target"]))
    with _jax.named_scope("update"):
        delta_w, new_m, new_v = {}, {}, {}
        for n in TWIN_WEIGHTS:
            delta_w[n], new_m[n], new_v[n] = _adamw(weights[n], grad_w[n], given["m_" + n], given["v_" + n])
    return (loss, grad_x, *[grad_w[n] for n in TWIN_WEIGHTS], *[delta_w[n] for n in TWIN_WEIGHTS],
            *[new_m[n] for n in TWIN_WEIGHTS], *[new_v[n] for n in TWIN_WEIGHTS])
```

```python
import functools
import math

import jax
import jax.numpy as jnp
from jax import lax
from jax.experimental import pallas as pl
from jax.experimental.pallas import tpu as pltpu

F32 = jnp.float32
BF16 = jnp.bfloat16
MM_DTYPE = BF16

N_HEADS = 4
HEAD_DIM = 128
BR_DIM = N_HEADS * HEAD_DIM
CHUNK = 64
GM_BLOCK = 128
DEPTH = 4
N_CHIPS = 4
EPS = 1e-6
TINY = 1e-30
LB_MAX = 0.999
ADAM_LR = 0.001
ADAM_B1 = 0.9
ADAM_B2 = 0.999
ADAM_EPS = 1e-08
ADAM_WD = 0.01
ADAM_STEP = 10
INV_SQRT2 = 1.0 / math.sqrt(2.0)
INV_SQRT_2PI = 1.0 / math.sqrt(2.0 * math.pi)

VMEM_LIMIT_BYTES = 56 * 1024 * 1024
TILE_BYTES = 2 * 1024 * 1024
ACC_BYTES = 6 * 1024 * 1024
MESH = pl.DeviceIdType.MESH
SDS = jax.ShapeDtypeStruct
ANY = pl.BlockSpec(memory_space=pl.ANY)


def _params(*sem):
    return pltpu.CompilerParams(dimension_semantics=sem or None, vmem_limit_bytes=VMEM_LIMIT_BYTES)


def _divisor(n, limit, mult):
    best = None
    for d in range(mult, min(n, limit) + 1, mult):
        if n % d == 0:
            best = d
    return best if best is not None else n


def _dot(a, b):
    return jnp.dot(a.astype(MM_DTYPE), b.astype(MM_DTYPE), preferred_element_type=F32)


def _dot_nt(a, b):
    return lax.dot_general(a.astype(MM_DTYPE), b.astype(MM_DTYPE), (((1,), (1,)), ((), ())), preferred_element_type=F32)


def _dot_tn(a, b):
    return lax.dot_general(a.astype(MM_DTYPE), b.astype(MM_DTYPE), (((0,), (0,)), ((), ())), preferred_element_type=F32)


def _dot_exact(a, b):
    return jnp.dot(a, b, preferred_element_type=F32, precision=lax.Precision.HIGHEST)


def _sigmoid(x):
    return 1.0 / (1.0 + jnp.exp(-x))


def _gelu(x):
    return 0.5 * x * (1.0 + lax.erf(x * INV_SQRT2))


def _gelu_grad(x):
    return 0.5 * (1.0 + lax.erf(x * INV_SQRT2)) + x * jnp.exp(-0.5 * x * x) * INV_SQRT_2PI


def _silu_grad(x, s):
    return s * (1.0 + x * (1.0 - s))


def _resident(shape, index_map):
    return pl.BlockSpec(shape, index_map, pipeline_mode=pl.Buffered(1))


def _lower_bounds(logits):
    def body(lg_ref, lb_ref):
        lg = lg_ref[...]
        mx = jnp.max(lg, axis=0, keepdims=True)
        e = jnp.exp(lg - mx)
        p = e / jnp.sum(e, axis=0, keepdims=True)
        run = p[0:1]
        rows = [run - p[0:1]]
        for l in range(1, DEPTH):
            run = run + p[l : l + 1]
            rows.append(run - p[0:1])
        raw = jnp.concatenate(rows, axis=0)
        lb_ref[...] = jnp.minimum(jnp.maximum(raw, 0.0), LB_MAX)

    return pl.pallas_call(body, name="lower_bounds", out_shape=SDS(logits.shape, F32))(logits)


def _lower_bounds_bwd(logits, dlb):
    def body(lg_ref, dlb_ref, dlg_ref):
        lg = lg_ref[...]
        mx = jnp.max(lg, axis=0, keepdims=True)
        e = jnp.exp(lg - mx)
        p = e / jnp.sum(e, axis=0, keepdims=True)
        run = p[0:1]
        rows = [run - p[0:1]]
        for l in range(1, DEPTH):
            run = run + p[l : l + 1]
            rows.append(run - p[0:1])
        raw = jnp.concatenate(rows, axis=0)
        lo = jnp.where(raw > 0.0, 1.0, jnp.where(raw == 0.0, 0.5, 0.0))
        clipped = jnp.maximum(raw, 0.0)
        hi = jnp.where(clipped < LB_MAX, 1.0, jnp.where(clipped == LB_MAX, 0.5, 0.0))
        draw = dlb_ref[...] * lo * hi
        total = jnp.sum(draw, axis=0, keepdims=True)
        dps = []
        tail = total
        for j in range(DEPTH):
            dps.append(tail - total if j == 0 else tail)
            tail = tail - draw[j : j + 1]
        dp = jnp.concatenate(dps, axis=0)
        dlg_ref[...] = p * (dp - jnp.sum(p * dp, axis=0, keepdims=True))

    return pl.pallas_call(body, name="lower_bounds_bwd", out_shape=SDS(logits.shape, F32))(logits, dlb)


def _norm_matmul(x, gain, w_stack, layer, name):
    s, d = x.shape
    n = w_stack.shape[2]
    tm = _divisor(s, 256, 8)

    def body(x_ref, g_ref, w_ref, z_ref, h_ref):
        xv = x_ref[...]
        r = lax.rsqrt(jnp.mean(xv * xv, axis=-1, keepdims=True) + EPS)
        h = ((xv * r) * g_ref[...]).astype(MM_DTYPE)
        h_ref[...] = h
        z_ref[...] = jnp.dot(h, w_ref[...], preferred_element_type=F32)

    return pl.pallas_call(
        body,
        name=name,
        grid=(s // tm,),
        in_specs=[
            pl.BlockSpec((tm, d), lambda i: (i, 0)),
            _resident((1, d), lambda i: (0, 0)),
            _resident((None, d, n), lambda i: (layer, 0, 0)),
        ],
        out_specs=[pl.BlockSpec((tm, n), lambda i: (i, 0)), pl.BlockSpec((tm, d), lambda i: (i, 0))],
        out_shape=[SDS((s, n), F32), SDS((s, d), MM_DTYPE)],
        compiler_params=_params("parallel"),
    )(x, gain, w_stack)


def _proj_norm_bwd(dz, w_stack, layer, x, gain, dres, name):
    s, n = dz.shape
    d = x.shape[1]
    tm = _divisor(s, 256, 16)

    def body(dz_ref, w_ref, x_ref, g_ref, dres_ref, dx_ref, dg_ref):
        @pl.when(pl.program_id(0) == 0)
        def _():
            dg_ref[...] = jnp.zeros_like(dg_ref)

        dh = _dot_nt(dz_ref[...], w_ref[...])
        xv = x_ref[...]
        r = lax.rsqrt(jnp.mean(xv * xv, axis=-1, keepdims=True) + EPS)
        xh = xv * r
        dg_ref[...] += jnp.sum(dh * xh, axis=0, keepdims=True)
        dxh = dh * g_ref[...]
        dx_ref[...] = dres_ref[...] + r * (dxh - xh * jnp.mean(dxh * xh, axis=-1, keepdims=True))

    return pl.pallas_call(
        body,
        name=name,
        grid=(s // tm,),
        in_specs=[
            pl.BlockSpec((tm, n), lambda i: (i, 0)),
            _resident((None, d, n), lambda i: (layer, 0, 0)),
            pl.BlockSpec((tm, d), lambda i: (i, 0)),
            _resident((1, d), lambda i: (0, 0)),
            pl.BlockSpec((tm, d), lambda i: (i, 0)),
        ],
        out_specs=[pl.BlockSpec((tm, d), lambda i: (i, 0)), pl.BlockSpec((1, d), lambda i: (0, 0))],
        out_shape=[SDS((s, d), F32), SDS((1, d), F32)],
        compiler_params=_params("arbitrary"),
    )(dz, w_stack, x, gain, dres)


def _matmul_tn(a, b, stack, layer, name):
    s, k = a.shape
    n = b.shape[1]
    tn = _divisor(n, max(128, ACC_BYTES // (4 * k)), 128)
    ts = _divisor(s, 512, 16)

    def body(a_ref, b_ref, *rest):
        out_ref = rest[-1]

        @pl.when(pl.program_id(1) == 0)
        def _():
            out_ref[...] = jnp.zeros_like(out_ref)

        out_ref[...] += _dot_tn(a_ref[...], b_ref[...])

    in_specs = [pl.BlockSpec((ts, k), lambda j, i: (i, 0)), pl.BlockSpec((ts, tn), lambda j, i: (i, j))]
    args = [a, b]
    aliases = {}
    if stack is not None:
        in_specs.append(ANY)
        args.append(stack)
        aliases = {2: 0}
    return pl.pallas_call(
        body,
        name=name,
        grid=(n // tn, s // ts),
        in_specs=in_specs,
        out_specs=pl.BlockSpec((None, k, tn), lambda j, i: (layer, 0, j)),
        out_shape=SDS((DEPTH, k, n), F32),
        input_output_aliases=aliases,
        compiler_params=_params("parallel", "arbitrary"),
    )(*args)


def _gm_mask():
    r = lax.broadcasted_iota(jnp.int32, (GM_BLOCK, GM_BLOCK), 0) // CHUNK
    c = lax.broadcasted_iota(jnp.int32, (GM_BLOCK, GM_BLOCK), 1) // CHUNK
    return r >= c


def _gm_normed(v, lng, lnb):
    vg = _gelu(v)
    mu = jnp.mean(vg, axis=-1, keepdims=True)
    xc = vg - mu
    rstd = lax.rsqrt(jnp.mean(xc * xc, axis=-1, keepdims=True) + EPS)
    xh = xc * rstd
    return xh, rstd, xh * lng + lnb


def _gmlp_fwd(z, lng, lnb, ws, bst, name):
    s = z.shape[0]
    tg = _divisor(s, 256, GM_BLOCK)

    def body(z_ref, lng_ref, lnb_ref, ws_ref, bst_ref, a_ref):
        mask = _gm_mask()
        ug = _gelu(z_ref[:, :BR_DIM])
        _, _, vn = _gm_normed(z_ref[:, BR_DIM:], lng_ref[...], lnb_ref[...])
        vn = vn.astype(MM_DTYPE)
        for h in range(N_HEADS):
            cs = slice(h * HEAD_DIM, (h + 1) * HEAD_DIM)
            wm = jnp.where(mask, ws_ref[h], 0.0).astype(MM_DTYPE)
            for blk in range(tg // GM_BLOCK):
                rs = slice(blk * GM_BLOCK, (blk + 1) * GM_BLOCK)
                mixed = _dot(wm, vn[rs, cs]) + bst_ref[:, h : h + 1]
                a_ref[rs, cs] = (ug[rs, cs] * mixed).astype(MM_DTYPE)

    return pl.pallas_call(
        body,
        name=name,
        grid=(s // tg,),
        in_specs=[
            pl.BlockSpec((tg, 2 * BR_DIM), lambda i: (i, 0)),
            _resident((1, BR_DIM), lambda i: (0, 0)),
            _resident((1, BR_DIM), lambda i: (0, 0)),
            _resident((N_HEADS, GM_BLOCK, GM_BLOCK), lambda i: (0, 0, 0)),
            _resident((GM_BLOCK, N_HEADS), lambda i: (0, 0)),
        ],
        out_specs=pl.BlockSpec((tg, BR_DIM), lambda i: (i, 0)),
        out_shape=SDS((s, BR_DIM), MM_DTYPE),
        compiler_params=_params("parallel"),
    )(z, lng, lnb, ws, bst)


def _gmlp_bwd(da, z, lng, lnb, ws, bst, name):
    s = z.shape[0]
    tg = _divisor(s, 256, GM_BLOCK)

    def body(da_ref, z_ref, lng_ref, lnb_ref, ws_ref, bst_ref, dz_ref, dlng_ref, dlnb_ref, dws_ref, dbst_ref):
        @pl.when(pl.program_id(0) == 0)
        def _():
            dlng_ref[...] = jnp.zeros_like(dlng_ref)
            dlnb_ref[...] = jnp.zeros_like(dlnb_ref)
            dws_ref[...] = jnp.zeros_like(dws_ref)
            dbst_ref[...] = jnp.zeros_like(dbst_ref)

        mask = _gm_mask()
        u = z_ref[:, :BR_DIM]
        v = z_ref[:, BR_DIM:]
        ug = _gelu(u)
        lng_v = lng_ref[...]
        xh, rstd, vn = _gm_normed(v, lng_v, lnb_ref[...])
        vnb = vn.astype(MM_DTYPE)
        da_v = da_ref[...]
        dmix = da_v * ug
        dmixb = dmix.astype(MM_DTYPE)
        dvn_cols = []
        for h in range(N_HEADS):
            cs = slice(h * HEAD_DIM, (h + 1) * HEAD_DIM)
            wm = jnp.where(mask, ws_ref[h], 0.0).astype(MM_DTYPE)
            dw = jnp.zeros((GM_BLOCK, GM_BLOCK), F32)
            dbias = jnp.zeros((GM_BLOCK, 1), F32)
            dvn_rows = []
            for blk in range(tg // GM_BLOCK):
                rs = slice(blk * GM_BLOCK, (blk + 1) * GM_BLOCK)
                mixed = _dot(wm, vnb[rs, cs]) + bst_ref[:, h : h + 1]
                dz_ref[rs, cs] = (da_v[rs, cs] * mixed * _gelu_grad(u[rs, cs])).astype(MM_DTYPE)
                dw = dw + _dot_nt(dmixb[rs, cs], vnb[rs, cs])
                dbias = dbias + jnp.sum(dmix[rs, cs], axis=1, keepdims=True)
                dvn_rows.append(_dot_tn(wm, dmixb[rs, cs]))
            dws_ref[h] += jnp.where(mask, dw, 0.0)
            dbst_ref[:, h : h + 1] += dbias
            dvn_cols.append(jnp.concatenate(dvn_rows, axis=0) if len(dvn_rows) > 1 else dvn_rows[0])
        dvn = jnp.concatenate(dvn_cols, axis=1)
        dlng_ref[...] += jnp.sum(dvn * xh, axis=0, keepdims=True)
        dlnb_ref[...] += jnp.sum(dvn, axis=0, keepdims=True)
        dxh = dvn * lng_v
        dvg = rstd * (dxh - jnp.mean(dxh, axis=-1, keepdims=True) - xh * jnp.mean(dxh * xh, axis=-1, keepdims=True))
        dz_ref[:, BR_DIM:] = (dvg * _gelu_grad(v)).astype(MM_DTYPE)

    return pl.pallas_call(
        body,
        name=name,
        grid=(s // tg,),
        in_specs=[
            pl.BlockSpec((tg, BR_DIM), lambda i: (i, 0)),
            pl.BlockSpec((tg, 2 * BR_DIM), lambda i: (i, 0)),
            _resident((1, BR_DIM), lambda i: (0, 0)),
            _resident((1, BR_DIM), lambda i: (0, 0)),
            _resident((N_HEADS, GM_BLOCK, GM_BLOCK), lambda i: (0, 0, 0)),
            _resident((GM_BLOCK, N_HEADS), lambda i: (0, 0)),
        ],
        out_specs=[
            pl.BlockSpec((tg, 2 * BR_DIM), lambda i: (i, 0)),
            pl.BlockSpec((1, BR_DIM), lambda i: (0, 0)),
            pl.BlockSpec((1, BR_DIM), lambda i: (0, 0)),
            pl.BlockSpec((N_HEADS, GM_BLOCK, GM_BLOCK), lambda i: (0, 0, 0)),
            pl.BlockSpec((GM_BLOCK, N_HEADS), lambda i: (0, 0)),
        ],
        out_shape=[
            SDS((s, 2 * BR_DIM), MM_DTYPE),
            SDS((1, BR_DIM), F32),
            SDS((1, BR_DIM), F32),
            SDS((N_HEADS, GM_BLOCK, GM_BLOCK), F32),
            SDS((GM_BLOCK, N_HEADS), F32),
        ],
        compiler_params=_params("arbitrary"),
    )(da, z, lng, lnb, ws, bst)


HG_ROWS = 256
MID = CHUNK // 2 - 1


def _tri(lower):
    r = lax.broadcasted_iota(jnp.int32, (CHUNK, CHUNK), 0)
    c = lax.broadcasted_iota(jnp.int32, (CHUNK, CHUNK), 1)
    return r >= c if lower else c >= r


def _hgrn_fwd(z, lb, ng, name):
    s = z.shape[0]
    rows_per = _divisor(s, HG_ROWS, CHUNK)
    n_sub = rows_per // CHUNK

    def body(zqf_ref, zio_ref, lb_ref, ng_ref, o_ref, b_ref, st_ref, state):
        @pl.when(pl.program_id(0) == 0)
        def _():
            state[...] = jnp.zeros_like(state)

        low = _tri(True)
        lowf = low.astype(F32)
        ng_v = ng_ref[...]

        def chunk(ci, carry):
            rows = pl.ds(pl.multiple_of(ci * CHUNK, CHUNK), CHUNK)
            for h in range(N_HEADS):
                cs = slice(h * HEAD_DIM, (h + 1) * HEAD_DIM)
                cs2 = slice(BR_DIM + h * HEAD_DIM, BR_DIM + (h + 1) * HEAD_DIM)
                zq = zqf_ref[rows, cs]
                zf = zqf_ref[rows, cs2]
                vi = zio_ref[rows, cs]
                zog = zio_ref[rows, cs2]
                lbh = lb_ref[:, cs]
                qf = zq * _sigmoid(zq)
                forget = lbh + (1.0 - lbh) * _sigmoid(zf)
                g = jnp.log(jnp.maximum(forget, TINY))
                k = (1.0 - lbh) * _sigmoid(-zf)
                gc = _dot_exact(lowf, g)
                gm = gc[MID : MID + 1]
                gl = gc[CHUNK - 1 : CHUNK]
                a = jnp.where(low, _dot_nt(qf * jnp.exp(gc - gm), k * jnp.exp(gm - gc)), 0.0)
                st = state[h]
                st_ref[ci, h] = st
                o = _dot(a, vi) + _dot_nt(qf * jnp.exp(gc), st)
                state[h] = st * jnp.exp(gl) + _dot_tn(vi, k * jnp.exp(gl - gc))
                r = lax.rsqrt(jnp.mean(o * o, axis=-1, keepdims=True) + EPS)
                o_ref[rows, cs] = o
                b_ref[rows, cs] = (((o * r) * ng_v) * (zog * _sigmoid(zog))).astype(MM_DTYPE)
            return carry

        lax.fori_loop(0, n_sub, chunk, 0)

    return pl.pallas_call(
        body,
        name=name,
        grid=(s // rows_per,),
        in_specs=[
            pl.BlockSpec((rows_per, 2 * BR_DIM), lambda i: (i, 1)),
            pl.BlockSpec((rows_per, 2 * BR_DIM), lambda i: (i, 2)),
            _resident((1, BR_DIM), lambda i: (0, 0)),
            _resident((1, HEAD_DIM), lambda i: (0, 0)),
        ],
        out_specs=[
            pl.BlockSpec((rows_per, BR_DIM), lambda i: (i, 0)),
            pl.BlockSpec((rows_per, BR_DIM), lambda i: (i, 0)),
            pl.BlockSpec((n_sub, N_HEADS, HEAD_DIM, HEAD_DIM), lambda i: (i, 0, 0, 0)),
        ],
        out_shape=[
            SDS((s, BR_DIM), F32),
            SDS((s, BR_DIM), MM_DTYPE),
            SDS((s // CHUNK, N_HEADS, HEAD_DIM, HEAD_DIM), F32),
        ],
        scratch_shapes=[pltpu.VMEM((N_HEADS, HEAD_DIM, HEAD_DIM), F32)],
        compiler_params=_params("arbitrary"),
    )(z, z, lb, ng)


def _hgrn_bwd(db, z, o, states, lb, ng, name):
    s = z.shape[0]
    rows_per = _divisor(s, HG_ROWS, CHUNK)
    n_sub = rows_per // CHUNK
    n_steps = s // rows_per

    def body(db_ref, zqf_ref, zio_ref, o_ref, st_ref, lb_ref, ng_ref, dz_ref, dlb_ref, dng_ref, dstate):
        @pl.when(pl.program_id(0) == 0)
        def _():
            dstate[...] = jnp.zeros_like(dstate)
            dlb_ref[...] = jnp.zeros_like(dlb_ref)
            dng_ref[...] = jnp.zeros_like(dng_ref)

        low = _tri(True)
        lowf = low.astype(F32)
        upf = _tri(False).astype(F32)
        ng_v = ng_ref[...]

        def chunk(cc, carry):
            ci = n_sub - 1 - cc
            rows = pl.ds(pl.multiple_of(ci * CHUNK, CHUNK), CHUNK)
            for h in range(N_HEADS):
                cs = slice(h * HEAD_DIM, (h + 1) * HEAD_DIM)
                cs2 = slice(BR_DIM + h * HEAD_DIM, BR_DIM + (h + 1) * HEAD_DIM)
                zq = zqf_ref[rows, cs]
                zf = zqf_ref[rows, cs2]
                vi = zio_ref[rows, cs]
                zog = zio_ref[rows, cs2]
                lbh = lb_ref[:, cs]
                ov = o_ref[rows, cs]
                dbv = db_ref[rows, cs]
                r = lax.rsqrt(jnp.mean(ov * ov, axis=-1, keepdims=True) + EPS)
                on = ov * r
                sgo = _sigmoid(zog)
                gate = zog * sgo
                dng_ref[...] += jnp.sum(dbv * gate * on, axis=0, keepdims=True)
                don = dbv * gate * ng_v
                dzog = dbv * (on * ng_v) * _silu_grad(zog, sgo)
                do = r * (don - on * jnp.mean(don * on, axis=-1, keepdims=True))
                sq = _sigmoid(zq)
                qf = zq * sq
                sg = _sigmoid(zf)
                sgn = _sigmoid(-zf)
                oml = 1.0 - lbh
                forget = lbh + oml * sg
                fm = jnp.maximum(forget, TINY)
                k = oml * sgn
                gc = _dot_exact(lowf, jnp.log(fm))
                gm = gc[MID : MID + 1]
                gl = gc[CHUNK - 1 : CHUNK]
                e_g = jnp.exp(gc)
                e_q = jnp.exp(gc - gm)
                e_k = jnp.exp(gm - gc)
                e_kd = jnp.exp(gl - gc)
                e_gl = jnp.exp(gl)
                qt = qf * e_q
                kt = k * e_k
                a = jnp.where(low, _dot_nt(qt, kt), 0.0)
                st = st_ref[ci, h]
                dst = dstate[h]
                dp = jnp.where(low, _dot_nt(do, vi), 0.0)
                dv = _dot_tn(a, do) + _dot_nt(k * e_kd, dst)
                dq = _dot(dp, kt) * e_q + e_g * _dot(do, st)
                dks = e_kd * _dot(vi, dst)
                dk = _dot_tn(dp, qt) * e_k + dks
                dstate[h] = _dot_tn(do, qf * e_g) + dst * e_gl
                dgc = qf * dq - k * dk
                extra = e_gl * jnp.sum(st * dst, axis=0, keepdims=True) + jnp.sum(k * dks, axis=0, keepdims=True)
                dg = _dot_exact(upf, dgc) + extra
                dforget = jnp.where(forget > TINY, dg / fm, 0.0)
                both = dforget - dk
                dlb_ref[:, cs] += jnp.sum(sgn * both, axis=0, keepdims=True)
                dz_ref[rows, cs] = (dq * _silu_grad(zq, sq)).astype(MM_DTYPE)
                dz_ref[rows, cs2] = (oml * sg * sgn * both).astype(MM_DTYPE)
                dz_ref[rows, 2 * BR_DIM + h * HEAD_DIM : 2 * BR_DIM + (h + 1) * HEAD_DIM] = dv.astype(MM_DTYPE)
                dz_ref[rows, 3 * BR_DIM + h * HEAD_DIM : 3 * BR_DIM + (h + 1) * HEAD_DIM] = dzog.astype(MM_DTYPE)
            return carry

        lax.fori_loop(0, n_sub, chunk, 0)

    rev = lambda i: n_steps - 1 - i
    return pl.pallas_call(
        body,
        name=name,
        grid=(n_steps,),
        in_specs=[
            pl.BlockSpec((rows_per, BR_DIM), lambda i: (rev(i), 0)),
            pl.BlockSpec((rows_per, 2 * BR_DIM), lambda i: (rev(i), 1)),
            pl.BlockSpec((rows_per, 2 * BR_DIM), lambda i: (rev(i), 2)),
            pl.BlockSpec((rows_per, BR_DIM), lambda i: (rev(i), 0)),
            pl.BlockSpec((n_sub, N_HEADS, HEAD_DIM, HEAD_DIM), lambda i: (rev(i), 0, 0, 0)),
            _resident((1, BR_DIM), lambda i: (0, 0)),
            _resident((1, HEAD_DIM), lambda i: (0, 0)),
        ],
        out_specs=[
            pl.BlockSpec((rows_per, 4 * BR_DIM), lambda i: (rev(i), 0)),
            pl.BlockSpec((1, BR_DIM), lambda i: (0, 0)),
            pl.BlockSpec((1, HEAD_DIM), lambda i: (0, 0)),
        ],
        out_shape=[SDS((s, 4 * BR_DIM), MM_DTYPE), SDS((1, BR_DIM), F32), SDS((1, HEAD_DIM), F32)],
        scratch_shapes=[pltpu.VMEM((N_HEADS, HEAD_DIM, HEAD_DIM), F32)],
        compiler_params=_params("arbitrary"),
    )(db, z, z, o, states, lb, ng)


def _mix_fwd(x, a, b, z, w_gm, w_hg, w_out, layer, name):
    s, d = x.shape
    tm = _divisor(s, 256, 16)
    g0 = 6 * BR_DIM // d

    def body(x_ref, a_ref, b_ref, ga_ref, gb_ref, wgm_ref, whg_ref, wout_ref, out_ref):
        y = _sigmoid(ga_ref[...]) * _dot(a_ref[...], wgm_ref[...]) + _sigmoid(gb_ref[...]) * _dot(b_ref[...], whg_ref[...])
        out_ref[...] = x_ref[...] + _dot(y, wout_ref[...])

    return pl.pallas_call(
        body,
        name=name,
        grid=(s // tm,),
        in_specs=[
            pl.BlockSpec((tm, d), lambda i: (i, 0)),
            pl.BlockSpec((tm, BR_DIM), lambda i: (i, 0)),
            pl.BlockSpec((tm, BR_DIM), lambda i: (i, 0)),
            pl.BlockSpec((tm, d), lambda i: (i, g0)),
            pl.BlockSpec((tm, d), lambda i: (i, g0 + 1)),
            _resident((None, BR_DIM, d), lambda i: (layer, 0, 0)),
            _resident((None, BR_DIM, d), lambda i: (layer, 0, 0)),
            _resident((None, d, d), lambda i: (layer, 0, 0)),
        ],
        out_specs=pl.BlockSpec((tm, d), lambda i: (i, 0)),
        out_shape=SDS((s, d), F32),
        compiler_params=_params("parallel"),
    )(x, a, b, z, z, w_gm, w_hg, w_out)


def _mix_bwd(dx, a, b, z, w_gm, w_hg, w_out, layer, name):
    s, d = dx.shape
    tm = _divisor(s, 256, 16)
    g0 = 6 * BR_DIM // d

    def body(dx_ref, a_ref, b_ref, ga_ref, gb_ref, wgm_ref, whg_ref, wout_ref, y_ref, dpa_ref, dpb_ref, da_ref, db_ref, dg_ref):
        dy = _dot_nt(dx_ref[...], wout_ref[...])
        pa = _dot(a_ref[...], wgm_ref[...])
        pb = _dot(b_ref[...], whg_ref[...])
        sa = _sigmoid(ga_ref[...])
        sb = _sigmoid(gb_ref[...])
        y_ref[...] = (sa * pa + sb * pb).astype(MM_DTYPE)
        dpa = (dy * sa).astype(MM_DTYPE)
        dpb = (dy * sb).astype(MM_DTYPE)
        dpa_ref[...] = dpa
        dpb_ref[...] = dpb
        da_ref[...] = _dot_nt(dpa, wgm_ref[...])
        db_ref[...] = _dot_nt(dpb, whg_ref[...])
        dg_ref[:, :d] = (dy * pa * sa * (1.0 - sa)).astype(MM_DTYPE)
        dg_ref[:, d:] = (dy * pb * sb * (1.0 - sb)).astype(MM_DTYPE)

    row = lambda w: pl.BlockSpec((tm, w), lambda i: (i, 0))
    return pl.pallas_call(
        body,
        name=name,
        grid=(s // tm,),
        in_specs=[
            row(d),
            row(BR_DIM),
            row(BR_DIM),
            pl.BlockSpec((tm, d), lambda i: (i, g0)),
            pl.BlockSpec((tm, d), lambda i: (i, g0 + 1)),
            _resident((None, BR_DIM, d), lambda i: (layer, 0, 0)),
            _resident((None, BR_DIM, d), lambda i: (layer, 0, 0)),
            _resident((None, d, d), lambda i: (layer, 0, 0)),
        ],
        out_specs=[row(d), row(d), row(d), row(BR_DIM), row(BR_DIM), row(2 * d)],
        out_shape=[
            SDS((s, d), MM_DTYPE),
            SDS((s, d), MM_DTYPE),
            SDS((s, d), MM_DTYPE),
            SDS((s, BR_DIM), F32),
            SDS((s, BR_DIM), F32),
            SDS((s, 2 * d), MM_DTYPE),
        ],
        compiler_params=_params("parallel"),
    )(dx, a, b, z, z, w_gm, w_hg, w_out)


HALO = 8


def _shift_down(v, prev, n):
    tm = v.shape[0]
    rolled = pltpu.roll(v, n, 0)
    rid = lax.broadcasted_iota(jnp.int32, v.shape, 0)
    out = rolled
    for r in range(n):
        out = jnp.where(rid == r, prev[HALO - n + r : HALO - n + r + 1], out)
    return out


def _shift_up(v, nxt, n):
    tm = v.shape[0]
    rolled = pltpu.roll(v, tm - n, 0)
    rid = lax.broadcasted_iota(jnp.int32, v.shape, 0)
    out = rolled
    for r in range(n):
        out = jnp.where(rid == tm - n + r, nxt[r : r + 1], out)
    return out


def _conv_cols(f):
    return _divisor(f, 1408, 128)


def _convffn_fwd(x, z2, cw, cb, w_down, layer, name):
    s, d = x.shape
    f = z2.shape[1] // 2
    tm = _divisor(s, 256, 16)
    cwid = _conv_cols(f)
    per8 = tm // HALO

    def body(x_ref, z_ref, prev_ref, cw_ref, cb_ref, wd_ref, out_ref):
        first = pl.program_id(0) == 0
        acc = x_ref[...]
        for c0 in range(0, f, cwid):
            halves = []
            for base in (c0, f + c0):
                cols = slice(base, base + cwid)
                zc = z_ref[:, cols]
                prev = jnp.where(first, 0.0, prev_ref[:, cols])
                conv = cb_ref[:, cols] + cw_ref[0:1, cols] * _shift_down(zc, prev, 2)
                conv = conv + cw_ref[1:2, cols] * _shift_down(zc, prev, 1) + cw_ref[2:3, cols] * zc
                halves.append(conv)
            gate, val = halves
            act = gate * _sigmoid(gate) * val
            acc = acc + _dot(act, wd_ref[c0 : c0 + cwid, :])
        out_ref[...] = acc

    return pl.pallas_call(
        body,
        name=name,
        grid=(s // tm,),
        in_specs=[
            pl.BlockSpec((tm, d), lambda i: (i, 0)),
            pl.BlockSpec((tm, 2 * f), lambda i: (i, 0)),
            pl.BlockSpec((HALO, 2 * f), lambda i: (jnp.maximum(i * per8 - 1, 0), 0)),
            _resident((None, 3, 2 * f), lambda i: (layer, 0, 0)),
            _resident((1, 2 * f), lambda i: (0, 0)),
            _resident((None, f, d), lambda i: (layer, 0, 0)),
        ],
        out_specs=pl.BlockSpec((tm, d), lambda i: (i, 0)),
        out_shape=SDS((s, d), F32),
        compiler_params=_params("arbitrary"),
    )(x, z2, z2, cw, cb, w_down)


def _convffn_bwd(dx, z2, cw, cb, w_down, layer, name):
    s, d = dx.shape
    f = z2.shape[1] // 2
    tm = _divisor(s, 256, 16)
    cwid = _conv_cols(f)
    per8 = tm // HALO
    n_steps = s // tm

    def body(dx_ref, z_ref, prev_ref, cw_ref, cb_ref, wd_ref, dz_ref, act_ref, dcw_ref, nxt):
        step = pl.program_id(0)

        @pl.when(step == 0)
        def _():
            nxt[...] = jnp.zeros_like(nxt)
            dcw_ref[...] = jnp.zeros_like(dcw_ref)

        first_tile = step == n_steps - 1
        dxv = dx_ref[...].astype(MM_DTYPE)
        for c0 in range(0, f, cwid):
            dact = _dot_nt(dxv, wd_ref[c0 : c0 + cwid, :])
            zs, convs = [], []
            for base in (c0, f + c0):
                cols = slice(base, base + cwid)
                zc = z_ref[:, cols]
                prev = jnp.where(first_tile, 0.0, prev_ref[:, cols])
                zm1 = _shift_down(zc, prev, 1)
                zm2 = _shift_down(zc, prev, 2)
                convs.append(cb_ref[:, cols] + cw_ref[0:1, cols] * zm2 + cw_ref[1:2, cols] * zm1 + cw_ref[2:3, cols] * zc)
                zs.append((zc, zm1, zm2))
            gate, val = convs
            sg = _sigmoid(gate)
            silu = gate * sg
            act_ref[:, c0 : c0 + cwid] = (silu * val).astype(MM_DTYPE)
            dconvs = (dact * val * _silu_grad(gate, sg), dact * silu)
            for base, dconv, (zc, zm1, zm2) in zip((c0, f + c0), dconvs, zs):
                cols = slice(base, base + cwid)
                dcw_ref[0:1, cols] += jnp.sum(dconv * zm2, axis=0, keepdims=True)
                dcw_ref[1:2, cols] += jnp.sum(dconv * zm1, axis=0, keepdims=True)
                dcw_ref[2:3, cols] += jnp.sum(dconv * zc, axis=0, keepdims=True)
                dcw_ref[3:4, cols] += jnp.sum(dconv, axis=0, keepdims=True)
                nx = nxt[:, cols]
                dz = cw_ref[2:3, cols] * dconv + cw_ref[1:2, cols] * _shift_up(dconv, nx, 1)
                dz = dz + cw_ref[0:1, cols] * _shift_up(dconv, nx, 2)
                dz_ref[:, cols] = dz.astype(MM_DTYPE)
                nxt[:, cols] = dconv[0:HALO]

    rev = lambda i: n_steps - 1 - i
    return pl.pallas_call(
        body,
        name=name,
        grid=(n_steps,),
        in_specs=[
            pl.BlockSpec((tm, d), lambda i: (rev(i), 0)),
            pl.BlockSpec((tm, 2 * f), lambda i: (rev(i), 0)),
            pl.BlockSpec((HALO, 2 * f), lambda i: (jnp.maximum(rev(i) * per8 - 1, 0), 0)),
            _resident((None, 3, 2 * f), lambda i: (layer, 0, 0)),
            _resident((1, 2 * f), lambda i: (0, 0)),
            _resident((None, f, d), lambda i: (layer, 0, 0)),
        ],
        out_specs=[
            pl.BlockSpec((tm, 2 * f), lambda i: (rev(i), 0)),
            pl.BlockSpec((tm, f), lambda i: (rev(i), 0)),
            pl.BlockSpec((HALO, 2 * f), lambda i: (0, 0)),
        ],
        out_shape=[SDS((s, 2 * f), MM_DTYPE), SDS((s, f), MM_DTYPE), SDS((HALO, 2 * f), F32)],
        scratch_shapes=[pltpu.VMEM((HALO, 2 * f), F32)],
        compiler_params=_params("arbitrary"),
    )(dx, z2, z2, cw, cb, w_down)


def _loss_head(x, gain, target):
    s, d = x.shape
    tm = _divisor(s, 256, 8)

    def body(x_ref, g_ref, t_ref, loss_ref, dx_ref, dg_ref):
        @pl.when(pl.program_id(0) == 0)
        def _():
            loss_ref[...] = jnp.zeros_like(loss_ref)
            dg_ref[...] = jnp.zeros_like(dg_ref)

        xv = x_ref[...]
        gv = g_ref[...]
        r = lax.rsqrt(jnp.mean(xv * xv, axis=-1, keepdims=True) + EPS)
        xh = xv * r
        err = xh * gv - t_ref[...]
        loss_ref[...] += 0.5 * jnp.sum(jnp.mean(err * err, axis=-1, keepdims=True))
        dout = err * (1.0 / d)
        dg_ref[...] += jnp.sum(dout * xh, axis=0, keepdims=True)
        dxh = dout * gv
        dx_ref[...] = r * (dxh - xh * jnp.mean(dxh * xh, axis=-1, keepdims=True))

    return pl.pallas_call(
        body,
        name="loss_head",
        grid=(s // tm,),
        in_specs=[
            pl.BlockSpec((tm, d), lambda i: (i, 0)),
            _resident((1, d), lambda i: (0, 0)),
            pl.BlockSpec((tm, d), lambda i: (i, 0)),
        ],
        out_specs=[
            pl.BlockSpec((8, 128), lambda i: (0, 0)),
            pl.BlockSpec((tm, d), lambda i: (i, 0)),
            pl.BlockSpec((1, d), lambda i: (0, 0)),
        ],
        out_shape=[SDS((8, 128), F32), SDS((s, d), F32), SDS((1, d), F32)],
        compiler_params=_params("arbitrary"),
    )(x, gain, target)


def _adamw(w, g, m, v, name):
    shape = w.shape
    cols = shape[-1]
    rows = max(1, math.prod(shape[:-1]))
    tr = _divisor(rows, max(8, TILE_BYTES // (4 * cols)), 8)
    flat = [t.reshape(rows, cols) for t in (w, g, m, v)]

    def body(w_ref, g_ref, m_ref, v_ref, d_ref, nm_ref, nv_ref):
        gv = g_ref[...]
        m2 = ADAM_B1 * m_ref[...] + (1.0 - ADAM_B1) * gv
        v2 = ADAM_B2 * v_ref[...] + (1.0 - ADAM_B2) * (gv * gv)
        m_hat = m2 / (1.0 - ADAM_B1**ADAM_STEP)
        v_hat = v2 / (1.0 - ADAM_B2**ADAM_STEP)
        d_ref[...] = -ADAM_LR * (m_hat / (jnp.sqrt(v_hat) + ADAM_EPS) + ADAM_WD * w_ref[...])
        nm_ref[...] = m2
        nv_ref[...] = v2

    spec = pl.BlockSpec((tr, cols), lambda i: (i, 0))
    outs = pl.pallas_call(
        body,
        name=name,
        grid=(rows // tr,),
        in_specs=[spec] * 4,
        out_specs=[spec] * 3,
        out_shape=[SDS((rows, cols), F32)] * 3,
        compiler_params=_params("parallel"),
    )(*flat)
    return tuple(t.reshape(shape) for t in outs)


def _position():
    return lax.axis_index("x"), lax.axis_index("y"), lax.axis_index("c")


def _other_chips(x, y):
    return [(1 - x, y), (x, 1 - y), (1 - x, 1 - y)]


def _remote(src, dst, send_sem, recv_sem, device):
    return pltpu.make_async_remote_copy(
        src_ref=src, dst_ref=dst, send_sem=send_sem, recv_sem=recv_sem, device_id=device, device_id_type=MESH
    )


def _shard_slab(ref, layers, chip_index, size, axis):
    if axis == 1:
        return ref.at[layers, pl.ds(chip_index * size, size), :]
    return ref.at[layers, :, pl.ds(chip_index * size, size)]


def _gather_weights(shards, axes):
    n = len(shards)
    half_layers = DEPTH // 2
    full_shapes = []
    for t, ax in zip(shards, axes):
        shp = list(t.shape)
        shp[ax] *= N_CHIPS
        full_shapes.append(SDS(tuple(shp), t.dtype))

    def body(*refs):
        ins, outs = refs[:n], refs[n : 2 * n]
        send_sems, recv_sems, local_sems = refs[2 * n :]
        x, y, c = _position()
        sibling = (x, y, 1 - c)
        chips = _other_chips(x, y)
        mine = pl.ds(c * half_layers, half_layers)
        theirs = pl.ds((1 - c) * half_layers, half_layers)
        every = pl.ds(0, DEPTH)

        def slab(k, chip, layers):
            return _shard_slab(outs[k], layers, 2 * chip[0] + chip[1], shards[k].shape[axes[k]], axes[k])

        started = []
        for k in range(n):
            local = pltpu.make_async_copy(ins[k], slab(k, (x, y), every), local_sems.at[k])
            local.start()
            started.append(local)
        sends = []
        for k in range(n):
            for j, chip in enumerate(chips):
                cp = _remote(ins[k].at[mine], slab(k, (x, y), mine), send_sems.at[k, j], recv_sems.at[k, j], (*chip, c))
                cp.start()
                sends.append(cp)
        for k in range(n):
            for j, chip in enumerate(chips):
                _remote(ins[k].at[mine], slab(k, chip, mine), send_sems.at[k, j], recv_sems.at[k, j], (*chip, c)).wait_recv()
                fwd = _remote(slab(k, chip, mine), slab(k, chip, mine), send_sems.at[k, 3 + j], recv_sems.at[k, 3 + j], sibling)
                fwd.start()
                sends.append(fwd)
        for k in range(n):
            for j, chip in enumerate(chips):
                _remote(ins[k].at[mine], slab(k, chip, theirs), send_sems.at[k, 3 + j], recv_sems.at[k, 3 + j], sibling).wait_recv()
        for cp in sends:
            cp.wait_send()
        for local in started:
            local.wait()

    return pl.pallas_call(
        body,
        name="gather_weights",
        in_specs=[ANY] * n,
        out_specs=[ANY] * n,
        out_shape=full_shapes,
        scratch_shapes=[pltpu.SemaphoreType.DMA((n, 6)), pltpu.SemaphoreType.DMA((n, 6)), pltpu.SemaphoreType.DMA((n,))],
    )(*shards)


def _pair_exchange(grads):
    n = len(grads)
    half_layers = DEPTH // 2

    def body(*refs):
        ins, outs = refs[:n], refs[n : 2 * n]
        send_sems, recv_sems = refs[2 * n :]
        x, y, c = _position()
        theirs = pl.ds((1 - c) * half_layers, half_layers)
        copies = []
        for k in range(n):
            cp = _remote(ins[k].at[theirs], outs[k], send_sems.at[k], recv_sems.at[k], (x, y, 1 - c))
            cp.start()
            copies.append(cp)
        for cp in copies:
            cp.wait()

    return pl.pallas_call(
        body,
        name="grad_pair_exchange",
        in_specs=[ANY] * n,
        out_specs=[ANY] * n,
        out_shape=[SDS((half_layers,) + g.shape[1:], g.dtype) for g in grads],
        scratch_shapes=[pltpu.SemaphoreType.DMA((n,)), pltpu.SemaphoreType.DMA((n,))],
    )(*grads)


def _chip_exchange(pairs, axes):
    n = len(pairs)
    out_shapes = []
    sizes = []
    for t, ax in zip(pairs, axes):
        shp = list(t.shape)
        shp[ax] //= N_CHIPS
        sizes.append(shp[ax])
        out_shapes.append(SDS((3,) + tuple(shp), t.dtype))

    def body(*refs):
        ins, outs = refs[:n], refs[n : 2 * n]
        send_sems, recv_sems = refs[2 * n :]
        x, y, c = _position()
        both = pl.ds(0, DEPTH // 2)
        copies = []
        for k in range(n):
            for j, chip in enumerate(_other_chips(x, y)):
                src = _shard_slab(ins[k], both, 2 * chip[0] + chip[1], sizes[k], axes[k])
                cp = _remote(src, outs[k].at[j], send_sems.at[k, j], recv_sems.at[k, j], (*chip, c))
                cp.start()
                copies.append(cp)
        for cp in copies:
            cp.wait()

    return pl.pallas_call(
        body,
        name="grad_chip_exchange",
        in_specs=[ANY] * n,
        out_specs=[ANY] * n,
        out_shape=out_shapes,
        scratch_shapes=[pltpu.SemaphoreType.DMA((n, 3)), pltpu.SemaphoreType.DMA((n, 3))],
    )(*pairs)


def _sibling_complete(halves):
    n = len(halves)
    half_layers = DEPTH // 2

    def body(*refs):
        ins, outs = refs[:n], refs[n : 2 * n]
        send_sems, recv_sems, local_sems = refs[2 * n :]
        x, y, c = _position()
        mine = pl.ds(c * half_layers, half_layers)
        theirs = pl.ds((1 - c) * half_layers, half_layers)
        locals_, copies = [], []
        for k in range(n):
            local = pltpu.make_async_copy(ins[k], outs[k].at[mine], local_sems.at[k])
            local.start()
            locals_.append(local)
            cp = _remote(ins[k], outs[k].at[mine], send_sems.at[k], recv_sems.at[k], (x, y, 1 - c))
            cp.start()
            copies.append(cp)
        for k in range(n):
            _remote(ins[k], outs[k].at[theirs], send_sems.at[k], recv_sems.at[k], (x, y, 1 - c)).wait_recv()
        for cp in copies:
            cp.wait_send()
        for local in locals_:
            local.wait()

    return pl.pallas_call(
        body,
        name="grad_sibling_complete",
        in_specs=[ANY] * n,
        out_specs=[ANY] * n,
        out_shape=[SDS((DEPTH,) + h.shape[1:], h.dtype) for h in halves],
        scratch_shapes=[pltpu.SemaphoreType.DMA((n,)), pltpu.SemaphoreType.DMA((n,)), pltpu.SemaphoreType.DMA((n,))],
    )(*halves)


def _gather_small(buf):
    rows, cols = buf.shape
    flips = [(fx, fy, fc) for fx in (0, 1) for fy in (0, 1) for fc in (0, 1)][1:]

    def body(in_ref, out_ref, send_sems, recv_sems, local_sem):
        x, y, c = _position()

        def peer(f):
            return (x + f[0] - 2 * x * f[0], y + f[1] - 2 * y * f[1], c + f[2] - 2 * c * f[2])

        me = 4 * x + 2 * y + c
        local = pltpu.make_async_copy(in_ref, out_ref.at[me], local_sem)
        local.start()
        copies = []
        for j, f in enumerate(flips):
            cp = _remote(in_ref, out_ref.at[me], send_sems.at[j], recv_sems.at[j], peer(f))
            cp.start()
            copies.append(cp)
        for j, f in enumerate(flips):
            px, py, pc = peer(f)
            _remote(in_ref, out_ref.at[4 * px + 2 * py + pc], send_sems.at[j], recv_sems.at[j], peer(f)).wait_recv()
        for cp in copies:
            cp.wait_send()
        local.wait()

    return pl.pallas_call(
        body,
        name="gather_small_grads",
        in_specs=[ANY],
        out_specs=ANY,
        out_shape=SDS((8, rows, cols), buf.dtype),
        scratch_shapes=[pltpu.SemaphoreType.DMA((7,)), pltpu.SemaphoreType.DMA((7,)), pltpu.SemaphoreType.DMA(())],
    )(buf)


def _pair_sum(idx, grad, recv, name):
    _, k, n = grad.shape
    tk = _divisor(k, max(16, TILE_BYTES // (4 * n)), 16)

    def body(idx_ref, g_ref, r_ref, out_ref):
        out_ref[...] = (g_ref[...] + r_ref[...]).astype(out_ref.dtype)

    return pl.pallas_call(
        body,
        name=name,
        grid_spec=pltpu.PrefetchScalarGridSpec(
            num_scalar_prefetch=1,
            grid=(DEPTH // 2, k // tk),
            in_specs=[
                pl.BlockSpec((None, tk, n), lambda l, i, idx_ref: (idx_ref[0] * (DEPTH // 2) + l, i, 0)),
                pl.BlockSpec((None, tk, n), lambda l, i, idx_ref: (l, i, 0)),
            ],
            out_specs=pl.BlockSpec((None, tk, n), lambda l, i, idx_ref: (l, i, 0)),
        ),
        out_shape=SDS((DEPTH // 2, k, n), BF16),
        compiler_params=_params("parallel", "parallel"),
    )(idx, grad, recv)


def _chip_sum(idx, pair, others, axis, name):
    _, _, ks, ns = others.shape
    tk = _divisor(ks, max(16, TILE_BYTES // (4 * ns)), 16)
    per = ks // tk

    def body(idx_ref, p_ref, o0_ref, o1_ref, o2_ref, out_ref):
        acc = p_ref[...].astype(F32) + o0_ref[...].astype(F32)
        out_ref[...] = (acc + o1_ref[...].astype(F32)) + o2_ref[...].astype(F32)

    if axis == 1:
        own = pl.BlockSpec((None, tk, ns), lambda l, i, idx_ref: (l, idx_ref[1] * per + i, 0))
    else:
        own = pl.BlockSpec((None, tk, ns), lambda l, i, idx_ref: (l, i, idx_ref[1]))
    other = lambda j: pl.BlockSpec((None, None, tk, ns), lambda l, i, idx_ref: (j, l, i, 0))
    return pl.pallas_call(
        body,
        name=name,
        grid_spec=pltpu.PrefetchScalarGridSpec(
            num_scalar_prefetch=1,
            grid=(DEPTH // 2, per),
            in_specs=[own, other(0), other(1), other(2)],
            out_specs=pl.BlockSpec((None, tk, ns), lambda l, i, idx_ref: (l, i, 0)),
        ),
        out_shape=SDS((DEPTH // 2, ks, ns), F32),
        compiler_params=_params("parallel", "parallel"),
    )(idx, pair, others, others, others)


def _sum_devices(gathered):
    _, rows, cols = gathered.shape

    def body(g_ref, out_ref):
        acc = g_ref[0]
        for dev in range(1, 8):
            acc = acc + g_ref[dev]
        out_ref[...] = acc

    return pl.pallas_call(body, name="sum_small_grads", out_shape=SDS((rows, cols), F32))(gathered)


PACK_TILE = 8 * 128


def _pack(arrays):
    parts = []
    for t in arrays:
        flat = t.reshape(-1)
        pad = (-flat.shape[0]) % PACK_TILE
        if pad:
            flat = jnp.concatenate([flat, jnp.zeros((pad,), flat.dtype)])
        parts.append(flat.reshape(-1, 128))
    return jnp.concatenate(parts, axis=0)


def _unpack(buf, shapes):
    out, row = [], 0
    for shp in shapes:
        size = math.prod(shp)
        rows = -(-size // PACK_TILE) * 8
        out.append(buf[row : row + rows].reshape(-1)[:size].reshape(shp))
        row += rows
    return out


def _local_step(x, target, lb, full, small):
    d = x.shape[1]
    saved = []
    for l in range(DEPTH):
        z, h = _norm_matmul(x, small["mix_norm"][l : l + 1], full["w_in"], l, f"in_proj_{l}")
        bst = small["gm_bs"][l].T
        a = _gmlp_fwd(z, small["gm_ln_g"][l : l + 1], small["gm_ln_b"][l : l + 1], small["gm_ws"][l], bst, f"gmlp_fwd_{l}")
        o, b, states = _hgrn_fwd(z, lb[l : l + 1], small["hg_norm_g"][l : l + 1], f"hgrn_fwd_{l}")
        x1 = _mix_fwd(x, a, b, z, full["w_br_gm"], full["w_br_hg"], full["w_out"], l, f"mix_fwd_{l}")
        z2, h2 = _norm_matmul(x1, small["ffn_norm"][l : l + 1], full["w_up"], l, f"up_proj_{l}")
        x2 = _convffn_fwd(x1, z2, full["conv_w"], small["conv_b"][l : l + 1], full["w_down"], l, f"convffn_fwd_{l}")
        saved.append((x, h, z, a, b, o, states, x1, h2, z2, bst))
        x = x2

    loss_tile, dx, d_final = _loss_head(x, small["final_norm"].reshape(1, d), target)

    big = {k: None for k in ("w_in", "w_br_gm", "w_br_hg", "w_out", "w_up", "w_down")}
    per_layer = [None] * DEPTH
    for l in reversed(range(DEPTH)):
        x0, h, z, a, b, o, states, x1, h2, z2, bst = saved[l]
        dz2, act, dconv = _convffn_bwd(dx, z2, full["conv_w"], small["conv_b"][l : l + 1], full["w_down"], l, f"convffn_bwd_{l}")
        big["w_down"] = _matmul_tn(act, dx, big["w_down"], l, f"dw_down_{l}")
        big["w_up"] = _matmul_tn(h2, dz2, big["w_up"], l, f"dw_up_{l}")
        dx1, d_ffn = _proj_norm_bwd(dz2, full["w_up"], l, x1, small["ffn_norm"][l : l + 1], dx, f"up_proj_bwd_{l}")
        y, dpa, dpb, da, db, dgates = _mix_bwd(dx1, a, b, z, full["w_br_gm"], full["w_br_hg"], full["w_out"], l, f"mix_bwd_{l}")
        big["w_out"] = _matmul_tn(y, dx1, big["w_out"], l, f"dw_out_{l}")
        big["w_br_gm"] = _matmul_tn(a, dpa, big["w_br_gm"], l, f"dw_br_gm_{l}")
        big["w_br_hg"] = _matmul_tn(b, dpb, big["w_br_hg"], l, f"dw_br_hg_{l}")
        dz_hg, d_lb, d_ng = _hgrn_bwd(db, z, o, states, lb[l : l + 1], small["hg_norm_g"][l : l + 1], f"hgrn_bwd_{l}")
        dz_gm, d_lng, d_lnb, d_ws, d_bst = _gmlp_bwd(
            da, z, small["gm_ln_g"][l : l + 1], small["gm_ln_b"][l : l + 1], small["gm_ws"][l], bst, f"gmlp_bwd_{l}"
        )
        dz = jnp.concatenate([dz_gm, dz_hg, dgates], axis=1)
        big["w_in"] = _matmul_tn(h, dz, big["w_in"], l, f"dw_in_{l}")
        dx, d_mix = _proj_norm_bwd(dz, full["w_in"], l, x0, small["mix_norm"][l : l + 1], dx1, f"in_proj_bwd_{l}")
        per_layer[l] = dict(
            mix_norm=d_mix[0], gm_ln_g=d_lng[0], gm_ln_b=d_lnb[0], gm_ws=d_ws, gm_bs=d_bst.T, lb=d_lb[0], hg_norm_g=d_ng[0],
            ffn_norm=d_ffn[0], conv_w=dconv[0:3], conv_b=dconv[3],
        )

    small_grads = {k: jnp.stack([per_layer[l][k] for l in range(DEPTH)]) for k in per_layer[0]}
    small_grads["hg_lb_logits"] = _lower_bounds_bwd(small["hg_lb_logits"], small_grads.pop("lb"))
    small_grads["final_norm"] = d_final[0]
    return loss_tile[0, 0], dx, big, small_grads


BIG = ("w_in", "w_br_gm", "w_br_hg", "w_out", "w_up", "w_down")
BIG_AXIS = {"w_in": 2, "w_br_gm": 2, "w_br_hg": 2, "w_out": 1, "w_up": 2, "w_down": 1}
REPLICATED = ("mix_norm", "gm_ln_g", "gm_ln_b", "gm_ws", "gm_bs", "hg_lb_logits", "hg_norm_g", "ffn_norm", "conv_b", "final_norm")
WEIGHTS = ("mix_norm", "w_in", "gm_ln_g", "gm_ln_b", "gm_ws", "gm_bs", "hg_lb_logits", "hg_norm_g", "w_br_gm", "w_br_hg", "w_out",
           "ffn_norm", "w_up", "conv_w", "conv_b", "w_down", "final_norm")


def kernel(x, mix_norm, w_in, gm_ln_g, gm_ln_b, gm_ws, gm_bs, hg_lb_logits, hg_norm_g, w_br_gm, w_br_hg, w_out, ffn_norm, w_up, conv_w, conv_b, w_down, final_norm, loss_target, m_mix_norm, m_w_in, m_gm_ln_g, m_gm_ln_b, m_gm_ws, m_gm_bs, m_hg_lb_logits, m_hg_norm_g, m_w_br_gm, m_w_br_hg, m_w_out, m_ffn_norm, m_w_up, m_conv_w, m_conv_b, m_w_down, m_final_norm, v_mix_norm, v_w_in, v_gm_ln_g, v_gm_ln_b, v_gm_ws, v_gm_bs, v_hg_lb_logits, v_hg_norm_g, v_w_br_gm, v_w_br_hg, v_w_out, v_ffn_norm, v_w_up, v_conv_w, v_conv_b, v_w_down, v_final_norm):
    w = dict(mix_norm=mix_norm, w_in=w_in, gm_ln_g=gm_ln_g, gm_ln_b=gm_ln_b, gm_ws=gm_ws, gm_bs=gm_bs, hg_lb_logits=hg_lb_logits,
             hg_norm_g=hg_norm_g, w_br_gm=w_br_gm, w_br_hg=w_br_hg, w_out=w_out, ffn_norm=ffn_norm, w_up=w_up, conv_w=conv_w,
             conv_b=conv_b, w_down=w_down, final_norm=final_norm)
    m = dict(mix_norm=m_mix_norm, w_in=m_w_in, gm_ln_g=m_gm_ln_g, gm_ln_b=m_gm_ln_b, gm_ws=m_gm_ws, gm_bs=m_gm_bs,
             hg_lb_logits=m_hg_lb_logits, hg_norm_g=m_hg_norm_g, w_br_gm=m_w_br_gm, w_br_hg=m_w_br_hg, w_out=m_w_out,
             ffn_norm=m_ffn_norm, w_up=m_w_up, conv_w=m_conv_w, conv_b=m_conv_b, w_down=m_w_down, final_norm=m_final_norm)
    v = dict(mix_norm=v_mix_norm, w_in=v_w_in, gm_ln_g=v_gm_ln_g, gm_ln_b=v_gm_ln_b, gm_ws=v_gm_ws, gm_bs=v_gm_bs,
             hg_lb_logits=v_hg_lb_logits, hg_norm_g=v_hg_norm_g, w_br_gm=v_w_br_gm, w_br_hg=v_w_br_hg, w_out=v_w_out,
             ffn_norm=v_ffn_norm, w_up=v_w_up, conv_w=v_conv_w, conv_b=v_conv_b, w_down=v_w_down, final_norm=v_final_norm)

    px, py, pc = _position()
    chip = 2 * px + py
    idx = jnp.stack([pc, chip]).astype(jnp.int32)

    shards = [w[k].astype(MM_DTYPE) for k in BIG] + [conv_w]
    gathered = _gather_weights(shards, [BIG_AXIS[k] for k in BIG] + [2])
    full = dict(zip(BIG + ("conv_w",), gathered))
    small = {k: w[k] for k in REPLICATED}

    lb = _lower_bounds(hg_lb_logits)
    loss_part, dx, big, small_grads = _local_step(x[0], loss_target[0], lb, full, small)
    loss = lax.psum(loss_part, ("x", "y", "c"))

    received = _pair_exchange([big[k] for k in BIG])
    pairs = [_pair_sum(idx, big[k], r, f"pair_sum_{k}") for k, r in zip(BIG, received)]
    others = _chip_exchange(pairs, [BIG_AXIS[k] for k in BIG])
    halves = [_chip_sum(idx, p, o, BIG_AXIS[k], f"chip_sum_{k}") for k, p, o in zip(BIG, pairs, others)]
    grads = dict(zip(BIG, _sibling_complete(halves)))

    small_names = list(REPLICATED) + ["conv_w"]
    packed = _pack([small_grads[k] for k in small_names])
    summed = _sum_devices(_gather_small(packed))
    for k, g in zip(small_names, _unpack(summed, [small_grads[k].shape for k in small_names])):
        grads[k] = g
    grads["conv_w"] = lax.dynamic_slice_in_dim(grads["conv_w"], chip * conv_w.shape[2], conv_w.shape[2], axis=2)

    delta, new_m, new_v = {}, {}, {}
    for k in WEIGHTS:
        delta[k], new_m[k], new_v[k] = _adamw(w[k], grads[k], m[k], v[k], f"adamw_{k}")

    return (loss, dx[None], *[grads[k] for k in WEIGHTS], *[delta[k] for k in WEIGHTS], *[new_m[k] for k in WEIGHTS],
            *[new_v[k] for k in WEIGHTS])
```

```python
import math

import jax
import jax.numpy as jnp
from jax import lax
from jax.experimental import pallas as pl
from jax.experimental.pallas import tpu as pltpu

F32 = jnp.float32
BF16 = jnp.bfloat16
MM_DTYPE = BF16

N_HEADS = 4
HEAD_DIM = 128
BR_DIM = N_HEADS * HEAD_DIM
CHUNK = 64
GM_BLOCK = 128
DEPTH = 4
N_CHIPS = 4
EPS = 1e-6
TINY = 1e-30
LB_MAX = 0.999
ADAM_LR = 0.001
ADAM_B1 = 0.9
ADAM_B2 = 0.999
ADAM_EPS = 1e-08
ADAM_WD = 0.01
ADAM_STEP = 10
INV_SQRT2 = 1.0 / math.sqrt(2.0)
INV_SQRT_2PI = 1.0 / math.sqrt(2.0 * math.pi)

VMEM_LIMIT_BYTES = 56 * 1024 * 1024
TILE_BYTES = 2 * 1024 * 1024
ACC_BYTES = 6 * 1024 * 1024
MESH = pl.DeviceIdType.MESH
SDS = jax.ShapeDtypeStruct
ANY = pl.BlockSpec(memory_space=pl.ANY)


def _params(*sem):
    return pltpu.CompilerParams(dimension_semantics=sem or None, vmem_limit_bytes=VMEM_LIMIT_BYTES)


def _divisor(n, limit, mult):
    best = None
    for d in range(mult, min(n, limit) + 1, mult):
        if n % d == 0:
            best = d
    return best if best is not None else n


def _dot(a, b):
    return jnp.dot(a.astype(MM_DTYPE), b.astype(MM_DTYPE), preferred_element_type=F32)


def _dot_nt(a, b):
    return lax.dot_general(a.astype(MM_DTYPE), b.astype(MM_DTYPE), (((1,), (1,)), ((), ())), preferred_element_type=F32)


def _dot_tn(a, b):
    return lax.dot_general(a.astype(MM_DTYPE), b.astype(MM_DTYPE), (((0,), (0,)), ((), ())), preferred_element_type=F32)


def _dot_exact(a, b):
    return jnp.dot(a, b, preferred_element_type=F32, precision=lax.Precision.HIGHEST)


def _sigmoid(x):
    return 1.0 / (1.0 + jnp.exp(-x))


def _gelu(x):
    return 0.5 * x * (1.0 + lax.erf(x * INV_SQRT2))


def _gelu_grad(x):
    return 0.5 * (1.0 + lax.erf(x * INV_SQRT2)) + x * jnp.exp(-0.5 * x * x) * INV_SQRT_2PI


def _silu_grad(x, s):
    return s * (1.0 + x * (1.0 - s))


def _resident(shape, index_map):
    return pl.BlockSpec(shape, index_map, pipeline_mode=pl.Buffered(1))


class _Rider:
    def __init__(self, inputs, out_shapes, scratch, start, finish, aliases=None):
        self.inputs = list(inputs)
        self.out_shapes = list(out_shapes)
        self.scratch = list(scratch)
        self.start = start
        self.finish = finish
        self.aliases = dict(aliases or {})


def _join(a, b):
    if a is None or b is None:
        return a if b is None else b
    na, oa, sa = len(a.inputs), len(a.out_shapes), len(a.scratch)

    def start(i, o, s):
        a.start(i[:na], o[:oa], s[:sa])
        b.start(i[na:], o[oa:], s[sa:])

    def finish(i, o, s):
        a.finish(i[:na], o[:oa], s[:sa])
        b.finish(i[na:], o[oa:], s[sa:])

    aliases = dict(a.aliases)
    aliases.update({na + k: oa + v for k, v in b.aliases.items()})
    return _Rider(a.inputs + b.inputs, a.out_shapes + b.out_shapes, a.scratch + b.scratch, start, finish, aliases)


def _hosted_call(body, rider, *, name, grid, in_specs, out_specs, out_shape, scratch_shapes, args, semantics):
    n_in, n_out, n_scr = len(in_specs), len(out_specs), len(scratch_shapes)
    if rider is None:
        outs = pl.pallas_call(
            body, name=name, grid=grid, in_specs=in_specs, out_specs=out_specs, out_shape=out_shape,
            scratch_shapes=scratch_shapes, compiler_params=_params(*semantics),
        )(*args)
        return list(outs), []
    ri, ro = len(rider.inputs), len(rider.out_shapes)

    def wrapped(*refs):
        p = 0
        hin = refs[p : p + n_in]
        p += n_in
        rin = refs[p : p + ri]
        p += ri
        hout = refs[p : p + n_out]
        p += n_out
        rout = refs[p : p + ro]
        p += ro
        hscr = refs[p : p + n_scr]
        rscr = refs[p + n_scr :]

        @pl.when(pl.program_id(0) == 0)
        def _():
            rider.start(rin, rout, rscr)

        body(*hin, *hout, *hscr)

        @pl.when(pl.program_id(0) == grid[0] - 1)
        def _():
            rider.finish(rin, rout, rscr)

    outs = pl.pallas_call(
        wrapped,
        name=name,
        grid=grid,
        in_specs=list(in_specs) + [ANY] * ri,
        out_specs=list(out_specs) + [ANY] * ro,
        out_shape=list(out_shape) + rider.out_shapes,
        scratch_shapes=list(scratch_shapes) + rider.scratch,
        input_output_aliases={n_in + k: n_out + v for k, v in rider.aliases.items()},
        compiler_params=_params(*semantics),
    )(*args, *rider.inputs)
    return list(outs[:n_out]), list(outs[n_out:])


def _run_rider(rider, name):
    ri, ro = len(rider.inputs), len(rider.out_shapes)

    def body(*refs):
        rin, rout, rscr = refs[:ri], refs[ri : ri + ro], refs[ri + ro :]
        rider.start(rin, rout, rscr)
        rider.finish(rin, rout, rscr)

    return list(
        pl.pallas_call(
            body,
            name=name,
            in_specs=[ANY] * ri,
            out_specs=[ANY] * ro,
            out_shape=rider.out_shapes,
            scratch_shapes=rider.scratch,
            input_output_aliases=rider.aliases,
        )(*rider.inputs)
    )


def _lower_bounds(logits):
    def body(lg_ref, lb_ref):
        lg = lg_ref[...]
        mx = jnp.max(lg, axis=0, keepdims=True)
        e = jnp.exp(lg - mx)
        p = e / jnp.sum(e, axis=0, keepdims=True)
        run = p[0:1]
        rows = [run - p[0:1]]
        for l in range(1, DEPTH):
            run = run + p[l : l + 1]
            rows.append(run - p[0:1])
        raw = jnp.concatenate(rows, axis=0)
        lb_ref[...] = jnp.minimum(jnp.maximum(raw, 0.0), LB_MAX)

    return pl.pallas_call(body, name="lower_bounds", out_shape=SDS(logits.shape, F32))(logits)


def _lower_bounds_bwd(logits, dlb):
    def body(lg_ref, dlb_ref, dlg_ref):
        lg = lg_ref[...]
        mx = jnp.max(lg, axis=0, keepdims=True)
        e = jnp.exp(lg - mx)
        p = e / jnp.sum(e, axis=0, keepdims=True)
        run = p[0:1]
        rows = [run - p[0:1]]
        for l in range(1, DEPTH):
            run = run + p[l : l + 1]
            rows.append(run - p[0:1])
        raw = jnp.concatenate(rows, axis=0)
        lo = jnp.where(raw > 0.0, 1.0, jnp.where(raw == 0.0, 0.5, 0.0))
        clipped = jnp.maximum(raw, 0.0)
        hi = jnp.where(clipped < LB_MAX, 1.0, jnp.where(clipped == LB_MAX, 0.5, 0.0))
        draw = dlb_ref[...] * lo * hi
        total = jnp.sum(draw, axis=0, keepdims=True)
        dps = []
        tail = total
        for j in range(DEPTH):
            dps.append(tail - total if j == 0 else tail)
            tail = tail - draw[j : j + 1]
        dp = jnp.concatenate(dps, axis=0)
        dlg_ref[...] = p * (dp - jnp.sum(p * dp, axis=0, keepdims=True))

    return pl.pallas_call(body, name="lower_bounds_bwd", out_shape=SDS(logits.shape, F32))(logits, dlb)


def _norm_matmul(x, gain, w, name):
    s, d = x.shape
    n = w.shape[1]
    tm = _divisor(s, 256, 8)

    def body(x_ref, g_ref, w_ref, z_ref, h_ref):
        xv = x_ref[...]
        r = lax.rsqrt(jnp.mean(xv * xv, axis=-1, keepdims=True) + EPS)
        h = ((xv * r) * g_ref[...]).astype(MM_DTYPE)
        h_ref[...] = h
        z_ref[...] = jnp.dot(h, w_ref[...], preferred_element_type=F32)

    return pl.pallas_call(
        body,
        name=name,
        grid=(s // tm,),
        in_specs=[
            pl.BlockSpec((tm, d), lambda i: (i, 0)),
            _resident((1, d), lambda i: (0, 0)),
            _resident((d, n), lambda i: (0, 0)),
        ],
        out_specs=[pl.BlockSpec((tm, n), lambda i: (i, 0)), pl.BlockSpec((tm, d), lambda i: (i, 0))],
        out_shape=[SDS((s, n), F32), SDS((s, d), MM_DTYPE)],
        compiler_params=_params("parallel"),
    )(x, gain, w)


def _proj_norm_bwd(dz_parts, w, x, gain, dres, name, rider=None):
    s, d = x.shape
    n = w.shape[1]
    tm = _divisor(s, 256, 16)
    widths = [p.shape[1] for p in dz_parts]
    offsets = [sum(widths[:k]) for k in range(len(widths))]
    n_parts = len(dz_parts)

    def body(*refs):
        dz_refs = refs[:n_parts]
        w_ref, x_ref, g_ref, dres_ref, dx_ref, dg_ref = refs[n_parts:]

        @pl.when(pl.program_id(0) == 0)
        def _():
            dg_ref[...] = jnp.zeros_like(dg_ref)

        dh = None
        for dz_ref, off, wd in zip(dz_refs, offsets, widths):
            part = _dot_nt(dz_ref[...], w_ref[:, off : off + wd])
            dh = part if dh is None else dh + part
        xv = x_ref[...]
        r = lax.rsqrt(jnp.mean(xv * xv, axis=-1, keepdims=True) + EPS)
        xh = xv * r
        dg_ref[...] += jnp.sum(dh * xh, axis=0, keepdims=True)
        dxh = dh * g_ref[...]
        dx_ref[...] = dres_ref[...] + r * (dxh - xh * jnp.mean(dxh * xh, axis=-1, keepdims=True))

    in_specs = [pl.BlockSpec((tm, wd), lambda i: (i, 0)) for wd in widths] + [
        _resident((d, n), lambda i: (0, 0)),
        pl.BlockSpec((tm, d), lambda i: (i, 0)),
        _resident((1, d), lambda i: (0, 0)),
        pl.BlockSpec((tm, d), lambda i: (i, 0)),
    ]
    return _hosted_call(
        body,
        rider,
        name=name,
        grid=(s // tm,),
        in_specs=in_specs,
        out_specs=[pl.BlockSpec((tm, d), lambda i: (i, 0)), pl.BlockSpec((1, d), lambda i: (0, 0))],
        out_shape=[SDS((s, d), F32), SDS((1, d), F32)],
        scratch_shapes=[],
        args=[*dz_parts, w, x, gain, dres],
        semantics=("arbitrary",),
    )


def _matmul_tn(a, b, stack, layer, name, n_total=None, col_off=0):
    s, k = a.shape
    n = b.shape[1]
    n_total = n if n_total is None else n_total
    tn = _divisor(math.gcd(n, col_off) if col_off else n, max(128, ACC_BYTES // (4 * k)), 128)
    ts = _divisor(s, 512, 16)
    off = col_off // tn

    def body(a_ref, b_ref, *rest):
        out_ref = rest[-1]

        @pl.when(pl.program_id(1) == 0)
        def _():
            out_ref[...] = jnp.zeros_like(out_ref)

        out_ref[...] += _dot_tn(a_ref[...], b_ref[...])

    in_specs = [pl.BlockSpec((ts, k), lambda j, i: (i, 0)), pl.BlockSpec((ts, tn), lambda j, i: (i, j))]
    args = [a, b]
    aliases = {}
    if stack is not None:
        in_specs.append(ANY)
        args.append(stack)
        aliases = {2: 0}
    return pl.pallas_call(
        body,
        name=name,
        grid=(n // tn, s // ts),
        in_specs=in_specs,
        out_specs=pl.BlockSpec((None, k, tn), lambda j, i: (layer, 0, off + j)),
        out_shape=SDS((DEPTH, k, n_total), F32),
        input_output_aliases=aliases,
        compiler_params=_params("parallel", "arbitrary"),
    )(*args)


def _gm_mask():
    r = lax.broadcasted_iota(jnp.int32, (GM_BLOCK, GM_BLOCK), 0) // CHUNK
    c = lax.broadcasted_iota(jnp.int32, (GM_BLOCK, GM_BLOCK), 1) // CHUNK
    return r >= c


def _gm_normed(v, lng, lnb):
    vg = _gelu(v)
    mu = jnp.mean(vg, axis=-1, keepdims=True)
    xc = vg - mu
    rstd = lax.rsqrt(jnp.mean(xc * xc, axis=-1, keepdims=True) + EPS)
    xh = xc * rstd
    return xh, rstd, xh * lng + lnb


def _gmlp_fwd(z, lng, lnb, ws, bst, name):
    s = z.shape[0]
    tg = _divisor(s, 256, GM_BLOCK)

    def body(z_ref, lng_ref, lnb_ref, ws_ref, bst_ref, a_ref):
        mask = _gm_mask()
        ug = _gelu(z_ref[:, :BR_DIM])
        _, _, vn = _gm_normed(z_ref[:, BR_DIM:], lng_ref[...], lnb_ref[...])
        vn = vn.astype(MM_DTYPE)
        for h in range(N_HEADS):
            cs = slice(h * HEAD_DIM, (h + 1) * HEAD_DIM)
            wm = jnp.where(mask, ws_ref[h], 0.0).astype(MM_DTYPE)
            for blk in range(tg // GM_BLOCK):
                rs = slice(blk * GM_BLOCK, (blk + 1) * GM_BLOCK)
                mixed = _dot(wm, vn[rs, cs]) + bst_ref[:, h : h + 1]
                a_ref[rs, cs] = (ug[rs, cs] * mixed).astype(MM_DTYPE)

    return pl.pallas_call(
        body,
        name=name,
        grid=(s // tg,),
        in_specs=[
            pl.BlockSpec((tg, 2 * BR_DIM), lambda i: (i, 0)),
            _resident((1, BR_DIM), lambda i: (0, 0)),
            _resident((1, BR_DIM), lambda i: (0, 0)),
            _resident((N_HEADS, GM_BLOCK, GM_BLOCK), lambda i: (0, 0, 0)),
            _resident((GM_BLOCK, N_HEADS), lambda i: (0, 0)),
        ],
        out_specs=pl.BlockSpec((tg, BR_DIM), lambda i: (i, 0)),
        out_shape=SDS((s, BR_DIM), MM_DTYPE),
        compiler_params=_params("parallel"),
    )(z, lng, lnb, ws, bst)


def _gmlp_bwd(da, z, lng, lnb, ws, bst, name):
    s = z.shape[0]
    tg = _divisor(s, 256, GM_BLOCK)

    def body(da_ref, z_ref, lng_ref, lnb_ref, ws_ref, bst_ref, dz_ref, dlng_ref, dlnb_ref, dws_ref, dbst_ref):
        @pl.when(pl.program_id(0) == 0)
        def _():
            dlng_ref[...] = jnp.zeros_like(dlng_ref)
            dlnb_ref[...] = jnp.zeros_like(dlnb_ref)
            dws_ref[...] = jnp.zeros_like(dws_ref)
            dbst_ref[...] = jnp.zeros_like(dbst_ref)

        mask = _gm_mask()
        u = z_ref[:, :BR_DIM]
        v = z_ref[:, BR_DIM:]
        ug = _gelu(u)
        lng_v = lng_ref[...]
        xh, rstd, vn = _gm_normed(v, lng_v, lnb_ref[...])
        vnb = vn.astype(MM_DTYPE)
        da_v = da_ref[...]
        dmix = da_v * ug
        dmixb = dmix.astype(MM_DTYPE)
        dvn_cols = []
        for h in range(N_HEADS):
            cs = slice(h * HEAD_DIM, (h + 1) * HEAD_DIM)
            wm = jnp.where(mask, ws_ref[h], 0.0).astype(MM_DTYPE)
            dw = jnp.zeros((GM_BLOCK, GM_BLOCK), F32)
            dbias = jnp.zeros((GM_BLOCK, 1), F32)
            dvn_rows = []
            for blk in range(tg // GM_BLOCK):
                rs = slice(blk * GM_BLOCK, (blk + 1) * GM_BLOCK)
                mixed = _dot(wm, vnb[rs, cs]) + bst_ref[:, h : h + 1]
                dz_ref[rs, cs] = (da_v[rs, cs] * mixed * _gelu_grad(u[rs, cs])).astype(MM_DTYPE)
                dw = dw + _dot_nt(dmixb[rs, cs], vnb[rs, cs])
                dbias = dbias + jnp.sum(dmix[rs, cs], axis=1, keepdims=True)
                dvn_rows.append(_dot_tn(wm, dmixb[rs, cs]))
            dws_ref[h] += jnp.where(mask, dw, 0.0)
            dbst_ref[:, h : h + 1] += dbias
            dvn_cols.append(jnp.concatenate(dvn_rows, axis=0) if len(dvn_rows) > 1 else dvn_rows[0])
        dvn = jnp.concatenate(dvn_cols, axis=1)
        dlng_ref[...] += jnp.sum(dvn * xh, axis=0, keepdims=True)
        dlnb_ref[...] += jnp.sum(dvn, axis=0, keepdims=True)
        dxh = dvn * lng_v
        dvg = rstd * (dxh - jnp.mean(dxh, axis=-1, keepdims=True) - xh * jnp.mean(dxh * xh, axis=-1, keepdims=True))
        dz_ref[:, BR_DIM:] = (dvg * _gelu_grad(v)).astype(MM_DTYPE)

    return pl.pallas_call(
        body,
        name=name,
        grid=(s // tg,),
        in_specs=[
            pl.BlockSpec((tg, BR_DIM), lambda i: (i, 0)),
            pl.BlockSpec((tg, 2 * BR_DIM), lambda i: (i, 0)),
            _resident((1, BR_DIM), lambda i: (0, 0)),
            _resident((1, BR_DIM), lambda i: (0, 0)),
            _resident((N_HEADS, GM_BLOCK, GM_BLOCK), lambda i: (0, 0, 0)),
            _resident((GM_BLOCK, N_HEADS), lambda i: (0, 0)),
        ],
        out_specs=[
            pl.BlockSpec((tg, 2 * BR_DIM), lambda i: (i, 0)),
            pl.BlockSpec((1, BR_DIM), lambda i: (0, 0)),
            pl.BlockSpec((1, BR_DIM), lambda i: (0, 0)),
            pl.BlockSpec((N_HEADS, GM_BLOCK, GM_BLOCK), lambda i: (0, 0, 0)),
            pl.BlockSpec((GM_BLOCK, N_HEADS), lambda i: (0, 0)),
        ],
        out_shape=[
            SDS((s, 2 * BR_DIM), MM_DTYPE),
            SDS((1, BR_DIM), F32),
            SDS((1, BR_DIM), F32),
            SDS((N_HEADS, GM_BLOCK, GM_BLOCK), F32),
            SDS((GM_BLOCK, N_HEADS), F32),
        ],
        compiler_params=_params("arbitrary"),
    )(da, z, lng, lnb, ws, bst)


HG_ROWS = 256
MID = CHUNK // 2 - 1


def _tri(lower):
    r = lax.broadcasted_iota(jnp.int32, (CHUNK, CHUNK), 0)
    c = lax.broadcasted_iota(jnp.int32, (CHUNK, CHUNK), 1)
    return r >= c if lower else c >= r


def _hgrn_fwd(z, lb, ng, name, rider=None):
    s = z.shape[0]
    rows_per = _divisor(s, HG_ROWS, CHUNK)
    n_sub = rows_per // CHUNK

    def body(zqf_ref, zio_ref, lb_ref, ng_ref, o_ref, b_ref, st_ref, state):
        @pl.when(pl.program_id(0) == 0)
        def _():
            state[...] = jnp.zeros_like(state)

        low = _tri(True)
        lowf = low.astype(F32)
        ng_v = ng_ref[...]

        def chunk(ci, carry):
            rows = pl.ds(pl.multiple_of(ci * CHUNK, CHUNK), CHUNK)
            for h in range(N_HEADS):
                cs = slice(h * HEAD_DIM, (h + 1) * HEAD_DIM)
                cs2 = slice(BR_DIM + h * HEAD_DIM, BR_DIM + (h + 1) * HEAD_DIM)
                zq = zqf_ref[rows, cs]
                zf = zqf_ref[rows, cs2]
                vi = zio_ref[rows, cs]
                zog = zio_ref[rows, cs2]
                lbh = lb_ref[:, cs]
                qf = zq * _sigmoid(zq)
                forget = lbh + (1.0 - lbh) * _sigmoid(zf)
                g = jnp.log(jnp.maximum(forget, TINY))
                k = (1.0 - lbh) * _sigmoid(-zf)
                gc = _dot_exact(lowf, g)
                gm = gc[MID : MID + 1]
                gl = gc[CHUNK - 1 : CHUNK]
                a = jnp.where(low, _dot_nt(qf * jnp.exp(gc - gm), k * jnp.exp(gm - gc)), 0.0)
                st = state[h]
                st_ref[ci, h] = st
                o = _dot(a, vi) + _dot_nt(qf * jnp.exp(gc), st)
                state[h] = st * jnp.exp(gl) + _dot_tn(vi, k * jnp.exp(gl - gc))
                r = lax.rsqrt(jnp.mean(o * o, axis=-1, keepdims=True) + EPS)
                o_ref[rows, cs] = o
                b_ref[rows, cs] = (((o * r) * ng_v) * (zog * _sigmoid(zog))).astype(MM_DTYPE)
            return carry

        lax.fori_loop(0, n_sub, chunk, 0)

    return _hosted_call(
        body,
        rider,
        name=name,
        grid=(s // rows_per,),
        in_specs=[
            pl.BlockSpec((rows_per, 2 * BR_DIM), lambda i: (i, 1)),
            pl.BlockSpec((rows_per, 2 * BR_DIM), lambda i: (i, 2)),
            _resident((1, BR_DIM), lambda i: (0, 0)),
            _resident((1, HEAD_DIM), lambda i: (0, 0)),
        ],
        out_specs=[
            pl.BlockSpec((rows_per, BR_DIM), lambda i: (i, 0)),
            pl.BlockSpec((rows_per, BR_DIM), lambda i: (i, 0)),
            pl.BlockSpec((n_sub, N_HEADS, HEAD_DIM, HEAD_DIM), lambda i: (i, 0, 0, 0)),
        ],
        out_shape=[
            SDS((s, BR_DIM), F32),
            SDS((s, BR_DIM), MM_DTYPE),
            SDS((s // CHUNK, N_HEADS, HEAD_DIM, HEAD_DIM), F32),
        ],
        scratch_shapes=[pltpu.VMEM((N_HEADS, HEAD_DIM, HEAD_DIM), F32)],
        args=[z, z, lb, ng],
        semantics=("arbitrary",),
    )


def _hgrn_bwd(db, z, o, states, lb, ng, name, rider=None):
    s = z.shape[0]
    rows_per = _divisor(s, HG_ROWS, CHUNK)
    n_sub = rows_per // CHUNK
    n_steps = s // rows_per

    def body(db_ref, zqf_ref, zio_ref, o_ref, st_ref, lb_ref, ng_ref, dz_ref, dlb_ref, dng_ref, dstate):
        @pl.when(pl.program_id(0) == 0)
        def _():
            dstate[...] = jnp.zeros_like(dstate)
            dlb_ref[...] = jnp.zeros_like(dlb_ref)
            dng_ref[...] = jnp.zeros_like(dng_ref)

        low = _tri(True)
        lowf = low.astype(F32)
        upf = _tri(False).astype(F32)
        ng_v = ng_ref[...]

        def chunk(cc, carry):
            ci = n_sub - 1 - cc
            rows = pl.ds(pl.multiple_of(ci * CHUNK, CHUNK), CHUNK)
            for h in range(N_HEADS):
                cs = slice(h * HEAD_DIM, (h + 1) * HEAD_DIM)
                cs2 = slice(BR_DIM + h * HEAD_DIM, BR_DIM + (h + 1) * HEAD_DIM)
                zq = zqf_ref[rows, cs]
                zf = zqf_ref[rows, cs2]
                vi = zio_ref[rows, cs]
                zog = zio_ref[rows, cs2]
                lbh = lb_ref[:, cs]
                ov = o_ref[rows, cs]
                dbv = db_ref[rows, cs]
                r = lax.rsqrt(jnp.mean(ov * ov, axis=-1, keepdims=True) + EPS)
                on = ov * r
                sgo = _sigmoid(zog)
                gate = zog * sgo
                dng_ref[...] += jnp.sum(dbv * gate * on, axis=0, keepdims=True)
                don = dbv * gate * ng_v
                dzog = dbv * (on * ng_v) * _silu_grad(zog, sgo)
                do = r * (don - on * jnp.mean(don * on, axis=-1, keepdims=True))
                sq = _sigmoid(zq)
                qf = zq * sq
                sg = _sigmoid(zf)
                sgn = _sigmoid(-zf)
                oml = 1.0 - lbh
                forget = lbh + oml * sg
                fm = jnp.maximum(forget, TINY)
                k = oml * sgn
                gc = _dot_exact(lowf, jnp.log(fm))
                gm = gc[MID : MID + 1]
                gl = gc[CHUNK - 1 : CHUNK]
                e_g = jnp.exp(gc)
                e_q = jnp.exp(gc - gm)
                e_k = jnp.exp(gm - gc)
                e_kd = jnp.exp(gl - gc)
                e_gl = jnp.exp(gl)
                qt = qf * e_q
                kt = k * e_k
                a = jnp.where(low, _dot_nt(qt, kt), 0.0)
                st = st_ref[ci, h]
                dst = dstate[h]
                dp = jnp.where(low, _dot_nt(do, vi), 0.0)
                dv = _dot_tn(a, do) + _dot_nt(k * e_kd, dst)
                dq = _dot(dp, kt) * e_q + e_g * _dot(do, st)
                dks = e_kd * _dot(vi, dst)
                dk = _dot_tn(dp, qt) * e_k + dks
                dstate[h] = _dot_tn(do, qf * e_g) + dst * e_gl
                dgc = qf * dq - k * dk
                extra = e_gl * jnp.sum(st * dst, axis=0, keepdims=True) + jnp.sum(k * dks, axis=0, keepdims=True)
                dg = _dot_exact(upf, dgc) + extra
                dforget = jnp.where(forget > TINY, dg / fm, 0.0)
                both = dforget - dk
                dlb_ref[:, cs] += jnp.sum(sgn * both, axis=0, keepdims=True)
                dz_ref[rows, cs] = (dq * _silu_grad(zq, sq)).astype(MM_DTYPE)
                dz_ref[rows, cs2] = (oml * sg * sgn * both).astype(MM_DTYPE)
                dz_ref[rows, 2 * BR_DIM + h * HEAD_DIM : 2 * BR_DIM + (h + 1) * HEAD_DIM] = dv.astype(MM_DTYPE)
                dz_ref[rows, 3 * BR_DIM + h * HEAD_DIM : 3 * BR_DIM + (h + 1) * HEAD_DIM] = dzog.astype(MM_DTYPE)
            return carry

        lax.fori_loop(0, n_sub, chunk, 0)

    rev = lambda i: n_steps - 1 - i
    return _hosted_call(
        body,
        rider,
        name=name,
        grid=(n_steps,),
        in_specs=[
            pl.BlockSpec((rows_per, BR_DIM), lambda i: (rev(i), 0)),
            pl.BlockSpec((rows_per, 2 * BR_DIM), lambda i: (rev(i), 1)),
            pl.BlockSpec((rows_per, 2 * BR_DIM), lambda i: (rev(i), 2)),
            pl.BlockSpec((rows_per, BR_DIM), lambda i: (rev(i), 0)),
            pl.BlockSpec((n_sub, N_HEADS, HEAD_DIM, HEAD_DIM), lambda i: (rev(i), 0, 0, 0)),
            _resident((1, BR_DIM), lambda i: (0, 0)),
            _resident((1, HEAD_DIM), lambda i: (0, 0)),
        ],
        out_specs=[
            pl.BlockSpec((rows_per, 4 * BR_DIM), lambda i: (rev(i), 0)),
            pl.BlockSpec((1, BR_DIM), lambda i: (0, 0)),
            pl.BlockSpec((1, HEAD_DIM), lambda i: (0, 0)),
        ],
        out_shape=[SDS((s, 4 * BR_DIM), MM_DTYPE), SDS((1, BR_DIM), F32), SDS((1, HEAD_DIM), F32)],
        scratch_shapes=[pltpu.VMEM((N_HEADS, HEAD_DIM, HEAD_DIM), F32)],
        args=[db, z, z, o, states, lb, ng],
        semantics=("arbitrary",),
    )


def _mix_fwd(x, a, b, z, w_gm, w_hg, w_out, name):
    s, d = x.shape
    tm = _divisor(s, 256, 16)
    g0 = 6 * BR_DIM // d

    def body(x_ref, a_ref, b_ref, ga_ref, gb_ref, wgm_ref, whg_ref, wout_ref, out_ref):
        y = _sigmoid(ga_ref[...]) * _dot(a_ref[...], wgm_ref[...]) + _sigmoid(gb_ref[...]) * _dot(b_ref[...], whg_ref[...])
        out_ref[...] = x_ref[...] + _dot(y, wout_ref[...])

    return pl.pallas_call(
        body,
        name=name,
        grid=(s // tm,),
        in_specs=[
            pl.BlockSpec((tm, d), lambda i: (i, 0)),
            pl.BlockSpec((tm, BR_DIM), lambda i: (i, 0)),
            pl.BlockSpec((tm, BR_DIM), lambda i: (i, 0)),
            pl.BlockSpec((tm, d), lambda i: (i, g0)),
            pl.BlockSpec((tm, d), lambda i: (i, g0 + 1)),
            _resident((BR_DIM, d), lambda i: (0, 0)),
            _resident((BR_DIM, d), lambda i: (0, 0)),
            _resident((d, d), lambda i: (0, 0)),
        ],
        out_specs=pl.BlockSpec((tm, d), lambda i: (i, 0)),
        out_shape=SDS((s, d), F32),
        compiler_params=_params("parallel"),
    )(x, a, b, z, z, w_gm, w_hg, w_out)


def _mix_bwd(dx, a, b, z, w_gm, w_hg, w_out, name):
    s, d = dx.shape
    tm = _divisor(s, 256, 16)
    g0 = 6 * BR_DIM // d

    def body(dx_ref, a_ref, b_ref, ga_ref, gb_ref, wgm_ref, whg_ref, wout_ref, y_ref, dpa_ref, dpb_ref, da_ref, db_ref, dg_ref):
        dy = _dot_nt(dx_ref[...], wout_ref[...])
        pa = _dot(a_ref[...], wgm_ref[...])
        pb = _dot(b_ref[...], whg_ref[...])
        sa = _sigmoid(ga_ref[...])
        sb = _sigmoid(gb_ref[...])
        y_ref[...] = (sa * pa + sb * pb).astype(MM_DTYPE)
        dpa = (dy * sa).astype(MM_DTYPE)
        dpb = (dy * sb).astype(MM_DTYPE)
        dpa_ref[...] = dpa
        dpb_ref[...] = dpb
        da_ref[...] = _dot_nt(dpa, wgm_ref[...])
        db_ref[...] = _dot_nt(dpb, whg_ref[...])
        dg_ref[:, :d] = (dy * pa * sa * (1.0 - sa)).astype(MM_DTYPE)
        dg_ref[:, d:] = (dy * pb * sb * (1.0 - sb)).astype(MM_DTYPE)

    row = lambda w: pl.BlockSpec((tm, w), lambda i: (i, 0))
    return pl.pallas_call(
        body,
        name=name,
        grid=(s // tm,),
        in_specs=[
            row(d),
            row(BR_DIM),
            row(BR_DIM),
            pl.BlockSpec((tm, d), lambda i: (i, g0)),
            pl.BlockSpec((tm, d), lambda i: (i, g0 + 1)),
            _resident((BR_DIM, d), lambda i: (0, 0)),
            _resident((BR_DIM, d), lambda i: (0, 0)),
            _resident((d, d), lambda i: (0, 0)),
        ],
        out_specs=[row(d), row(d), row(d), row(BR_DIM), row(BR_DIM), row(2 * d)],
        out_shape=[
            SDS((s, d), MM_DTYPE),
            SDS((s, d), MM_DTYPE),
            SDS((s, d), MM_DTYPE),
            SDS((s, BR_DIM), F32),
            SDS((s, BR_DIM), F32),
            SDS((s, 2 * d), MM_DTYPE),
        ],
        compiler_params=_params("parallel"),
    )(dx, a, b, z, z, w_gm, w_hg, w_out)


HALO = 8


def _shift_down(v, prev, n):
    rolled = pltpu.roll(v, n, 0)
    rid = lax.broadcasted_iota(jnp.int32, v.shape, 0)
    out = rolled
    for r in range(n):
        out = jnp.where(rid == r, prev[HALO - n + r : HALO - n + r + 1], out)
    return out


def _shift_up(v, nxt, n):
    tm = v.shape[0]
    rolled = pltpu.roll(v, tm - n, 0)
    rid = lax.broadcasted_iota(jnp.int32, v.shape, 0)
    out = rolled
    for r in range(n):
        out = jnp.where(rid == tm - n + r, nxt[r : r + 1], out)
    return out


def _conv_cols(f):
    return _divisor(f, 1408, 128)


def _convffn_fwd(x, z2, cw, cb, w_down, layer, name):
    s, d = x.shape
    f = z2.shape[1] // 2
    tm = _divisor(s, 256, 16)
    cwid = _conv_cols(f)
    per8 = tm // HALO

    def body(x_ref, z_ref, prev_ref, cw_ref, cb_ref, wd_ref, out_ref):
        first = pl.program_id(0) == 0
        acc = x_ref[...]
        for c0 in range(0, f, cwid):
            halves = []
            for base in (c0, f + c0):
                cols = slice(base, base + cwid)
                zc = z_ref[:, cols]
                prev = jnp.where(first, 0.0, prev_ref[:, cols])
                conv = cb_ref[:, cols] + cw_ref[0:1, cols] * _shift_down(zc, prev, 2)
                conv = conv + cw_ref[1:2, cols] * _shift_down(zc, prev, 1) + cw_ref[2:3, cols] * zc
                halves.append(conv)
            gate, val = halves
            act = gate * _sigmoid(gate) * val
            acc = acc + _dot(act, wd_ref[c0 : c0 + cwid, :])
        out_ref[...] = acc

    return pl.pallas_call(
        body,
        name=name,
        grid=(s // tm,),
        in_specs=[
            pl.BlockSpec((tm, d), lambda i: (i, 0)),
            pl.BlockSpec((tm, 2 * f), lambda i: (i, 0)),
            pl.BlockSpec((HALO, 2 * f), lambda i: (jnp.maximum(i * per8 - 1, 0), 0)),
            _resident((None, 3, 2 * f), lambda i: (layer, 0, 0)),
            _resident((1, 2 * f), lambda i: (0, 0)),
            _resident((f, d), lambda i: (0, 0)),
        ],
        out_specs=pl.BlockSpec((tm, d), lambda i: (i, 0)),
        out_shape=SDS((s, d), F32),
        compiler_params=_params("arbitrary"),
    )(x, z2, z2, cw, cb, w_down)


def _convffn_bwd(dx, z2, cw, cb, w_down, layer, name):
    s, d = dx.shape
    f = z2.shape[1] // 2
    tm = _divisor(s, 256, 16)
    cwid = _conv_cols(f)
    per8 = tm // HALO
    n_steps = s // tm

    def body(dx_ref, z_ref, prev_ref, cw_ref, cb_ref, wd_ref, dz_ref, act_ref, dcw_ref, nxt):
        step = pl.program_id(0)

        @pl.when(step == 0)
        def _():
            nxt[...] = jnp.zeros_like(nxt)
            dcw_ref[...] = jnp.zeros_like(dcw_ref)

        first_tile = step == n_steps - 1
        dxv = dx_ref[...].astype(MM_DTYPE)
        for c0 in range(0, f, cwid):
            dact = _dot_nt(dxv, wd_ref[c0 : c0 + cwid, :])
            zs, convs = [], []
            for base in (c0, f + c0):
                cols = slice(base, base + cwid)
                zc = z_ref[:, cols]
                prev = jnp.where(first_tile, 0.0, prev_ref[:, cols])
                zm1 = _shift_down(zc, prev, 1)
                zm2 = _shift_down(zc, prev, 2)
                convs.append(cb_ref[:, cols] + cw_ref[0:1, cols] * zm2 + cw_ref[1:2, cols] * zm1 + cw_ref[2:3, cols] * zc)
                zs.append((zc, zm1, zm2))
            gate, val = convs
            sg = _sigmoid(gate)
            silu = gate * sg
            act_ref[:, c0 : c0 + cwid] = (silu * val).astype(MM_DTYPE)
            dconvs = (dact * val * _silu_grad(gate, sg), dact * silu)
            for base, dconv, (zc, zm1, zm2) in zip((c0, f + c0), dconvs, zs):
                cols = slice(base, base + cwid)
                dcw_ref[0:1, cols] += jnp.sum(dconv * zm2, axis=0, keepdims=True)
                dcw_ref[1:2, cols] += jnp.sum(dconv * zm1, axis=0, keepdims=True)
                dcw_ref[2:3, cols] += jnp.sum(dconv * zc, axis=0, keepdims=True)
                dcw_ref[3:4, cols] += jnp.sum(dconv, axis=0, keepdims=True)
                nx = nxt[:, cols]
                dz = cw_ref[2:3, cols] * dconv + cw_ref[1:2, cols] * _shift_up(dconv, nx, 1)
                dz = dz + cw_ref[0:1, cols] * _shift_up(dconv, nx, 2)
                dz_ref[:, cols] = dz.astype(MM_DTYPE)
                nxt[:, cols] = dconv[0:HALO]

    rev = lambda i: n_steps - 1 - i
    return pl.pallas_call(
        body,
        name=name,
        grid=(n_steps,),
        in_specs=[
            pl.BlockSpec((tm, d), lambda i: (rev(i), 0)),
            pl.BlockSpec((tm, 2 * f), lambda i: (rev(i), 0)),
            pl.BlockSpec((HALO, 2 * f), lambda i: (jnp.maximum(rev(i) * per8 - 1, 0), 0)),
            _resident((None, 3, 2 * f), lambda i: (layer, 0, 0)),
            _resident((1, 2 * f), lambda i: (0, 0)),
            _resident((f, d), lambda i: (0, 0)),
        ],
        out_specs=[
            pl.BlockSpec((tm, 2 * f), lambda i: (rev(i), 0)),
            pl.BlockSpec((tm, f), lambda i: (rev(i), 0)),
            pl.BlockSpec((HALO, 2 * f), lambda i: (0, 0)),
        ],
        out_shape=[SDS((s, 2 * f), MM_DTYPE), SDS((s, f), MM_DTYPE), SDS((HALO, 2 * f), F32)],
        scratch_shapes=[pltpu.VMEM((HALO, 2 * f), F32)],
        compiler_params=_params("arbitrary"),
    )(dx, z2, z2, cw, cb, w_down)


def _loss_head(x, gain, target):
    s, d = x.shape
    tm = _divisor(s, 256, 8)

    def body(x_ref, g_ref, t_ref, loss_ref, dx_ref, dg_ref):
        @pl.when(pl.program_id(0) == 0)
        def _():
            loss_ref[...] = jnp.zeros_like(loss_ref)
            dg_ref[...] = jnp.zeros_like(dg_ref)

        xv = x_ref[...]
        gv = g_ref[...]
        r = lax.rsqrt(jnp.mean(xv * xv, axis=-1, keepdims=True) + EPS)
        xh = xv * r
        err = xh * gv - t_ref[...]
        loss_ref[...] += 0.5 * jnp.sum(jnp.mean(err * err, axis=-1, keepdims=True))
        dout = err * (1.0 / d)
        dg_ref[...] += jnp.sum(dout * xh, axis=0, keepdims=True)
        dxh = dout * gv
        dx_ref[...] = r * (dxh - xh * jnp.mean(dxh * xh, axis=-1, keepdims=True))

    return pl.pallas_call(
        body,
        name="loss_head",
        grid=(s // tm,),
        in_specs=[
            pl.BlockSpec((tm, d), lambda i: (i, 0)),
            _resident((1, d), lambda i: (0, 0)),
            pl.BlockSpec((tm, d), lambda i: (i, 0)),
        ],
        out_specs=[
            pl.BlockSpec((8, 128), lambda i: (0, 0)),
            pl.BlockSpec((tm, d), lambda i: (i, 0)),
            pl.BlockSpec((1, d), lambda i: (0, 0)),
        ],
        out_shape=[SDS((8, 128), F32), SDS((s, d), F32), SDS((1, d), F32)],
        compiler_params=_params("arbitrary"),
    )(x, gain, target)


def _adamw(w, g, m, v, name):
    shape = w.shape
    cols = shape[-1]
    rows = max(1, math.prod(shape[:-1]))
    tr = _divisor(rows, max(8, TILE_BYTES // (4 * cols)), 8)
    flat = [t.reshape(rows, cols) for t in (w, g, m, v)]

    def body(w_ref, g_ref, m_ref, v_ref, d_ref, nm_ref, nv_ref):
        gv = g_ref[...]
        m2 = ADAM_B1 * m_ref[...] + (1.0 - ADAM_B1) * gv
        v2 = ADAM_B2 * v_ref[...] + (1.0 - ADAM_B2) * (gv * gv)
        m_hat = m2 / (1.0 - ADAM_B1**ADAM_STEP)
        v_hat = v2 / (1.0 - ADAM_B2**ADAM_STEP)
        d_ref[...] = -ADAM_LR * (m_hat / (jnp.sqrt(v_hat) + ADAM_EPS) + ADAM_WD * w_ref[...])
        nm_ref[...] = m2
        nv_ref[...] = v2

    spec = pl.BlockSpec((tr, cols), lambda i: (i, 0))
    outs = pl.pallas_call(
        body,
        name=name,
        grid=(rows // tr,),
        in_specs=[spec] * 4,
        out_specs=[spec] * 3,
        out_shape=[SDS((rows, cols), F32)] * 3,
        compiler_params=_params("parallel"),
    )(*flat)
    return tuple(t.reshape(shape) for t in outs)


def _position():
    return lax.axis_index("x"), lax.axis_index("y"), lax.axis_index("c")


def _other_chips(x, y):
    return [(1 - x, y), (x, 1 - y), (1 - x, 1 - y)]


def _remote(src, dst, send_sem, recv_sem, device):
    return pltpu.make_async_remote_copy(
        src_ref=src, dst_ref=dst, send_sem=send_sem, recv_sem=recv_sem, device_id=device, device_id_type=MESH
    )


def _sub(ref, lead, shape, axis, chip=None, half=None):
    cut = [pl.ds(0, shape[0]), pl.ds(0, shape[1])]
    if chip is not None:
        size = shape[axis] // N_CHIPS
        cut[axis] = pl.ds(chip * size, size)
    if half is not None:
        size = shape[1 - axis] // 2
        cut[1 - axis] = pl.ds(half * size, size)
    return ref.at[(*lead, *cut)]


def _scaled(shape, axis, num, den=1):
    return tuple(d * num // den if i == axis else d for i, d in enumerate(shape))


def _gather_rider(shards, axes, layer, conv_w=None):
    n = len(shards)
    shard2d = [t.shape[1:] for t in shards]
    full2d = [_scaled(s2, ax, N_CHIPS) for s2, ax in zip(shard2d, axes)]
    out_shapes = [SDS(f2, t.dtype) for f2, t in zip(full2d, shards)]
    inputs = list(shards)
    scratch = [pltpu.SemaphoreType.DMA((n, 6)), pltpu.SemaphoreType.DMA((n, 6)), pltpu.SemaphoreType.DMA((n,))]
    if conv_w is not None:
        cshape = conv_w.shape
        inputs.append(conv_w)
        out_shapes.append(SDS(_scaled(cshape, 2, N_CHIPS), conv_w.dtype))
        scratch += [pltpu.SemaphoreType.DMA((3,)), pltpu.SemaphoreType.DMA((3,)), pltpu.SemaphoreType.DMA(())]

    def conv_slab(ref, chip):
        return ref.at[:, :, pl.ds((2 * chip[0] + chip[1]) * cshape[2], cshape[2])]

    def start(ins, outs, scr):
        send_sems, recv_sems, local_sems = scr[:3]
        x, y, c = _position()
        me = 2 * x + y
        for k in range(n):
            pltpu.make_async_copy(ins[k].at[layer], _sub(outs[k], (), full2d[k], axes[k], chip=me), local_sems.at[k]).start()
            for j, chip in enumerate(_other_chips(x, y)):
                _remote(
                    _sub(ins[k], (layer,), shard2d[k], axes[k], half=c),
                    _sub(outs[k], (), full2d[k], axes[k], chip=me, half=c),
                    send_sems.at[k, j], recv_sems.at[k, j], (*chip, c),
                ).start()
        if conv_w is not None:
            csend, crecv, clocal = scr[3:]
            pltpu.make_async_copy(ins[n], conv_slab(outs[n], (x, y)), clocal).start()
            for j, chip in enumerate(_other_chips(x, y)):
                _remote(ins[n], conv_slab(outs[n], (x, y)), csend.at[j], crecv.at[j], (*chip, c)).start()

    def finish(ins, outs, scr):
        send_sems, recv_sems, local_sems = scr[:3]
        x, y, c = _position()
        me = 2 * x + y
        sibling = (x, y, 1 - c)
        chips = _other_chips(x, y)
        forwards = []
        for k in range(n):
            src = _sub(ins[k], (layer,), shard2d[k], axes[k], half=c)
            for j, chip in enumerate(chips):
                landed = _sub(outs[k], (), full2d[k], axes[k], chip=2 * chip[0] + chip[1], half=c)
                _remote(src, landed, send_sems.at[k, j], recv_sems.at[k, j], (*chip, c)).wait_recv()
                fwd = _remote(landed, landed, send_sems.at[k, 3 + j], recv_sems.at[k, 3 + j], sibling)
                fwd.start()
                forwards.append(fwd)
        for k in range(n):
            src = _sub(ins[k], (layer,), shard2d[k], axes[k], half=c)
            for j, chip in enumerate(chips):
                theirs = _sub(outs[k], (), full2d[k], axes[k], chip=2 * chip[0] + chip[1], half=1 - c)
                _remote(src, theirs, send_sems.at[k, 3 + j], recv_sems.at[k, 3 + j], sibling).wait_recv()
        for fwd in forwards:
            fwd.wait_send()
        for k in range(n):
            src = _sub(ins[k], (layer,), shard2d[k], axes[k], half=c)
            mine = _sub(outs[k], (), full2d[k], axes[k], chip=me, half=c)
            for j, chip in enumerate(chips):
                _remote(src, mine, send_sems.at[k, j], recv_sems.at[k, j], (*chip, c)).wait_send()
            pltpu.make_async_copy(ins[k].at[layer], _sub(outs[k], (), full2d[k], axes[k], chip=me), local_sems.at[k]).wait()
        if conv_w is not None:
            csend, crecv, clocal = scr[3:]
            for j, chip in enumerate(chips):
                cp = _remote(ins[n], conv_slab(outs[n], chip), csend.at[j], crecv.at[j], (*chip, c))
                cp.wait_recv()
                cp.wait_send()
            pltpu.make_async_copy(ins[n], conv_slab(outs[n], (x, y)), clocal).wait()

    return _Rider(inputs, out_shapes, scratch, start, finish)


def _pair_rider(stacks, axes, layer):
    n = len(stacks)
    shapes = [t.shape[1:] for t in stacks]
    out_shapes = [SDS(_scaled(s2, 1 - ax, 1, 2), F32) for s2, ax in zip(shapes, axes)]
    scratch = [pltpu.SemaphoreType.DMA((n,)), pltpu.SemaphoreType.DMA((n,))]

    def copies(ins, outs, scr):
        x, y, c = _position()
        return [
            _remote(_sub(ins[k], (layer,), shapes[k], axes[k], half=1 - c), outs[k], scr[0].at[k], scr[1].at[k], (x, y, 1 - c))
            for k in range(n)
        ]

    def start(ins, outs, scr):
        for cp in copies(ins, outs, scr):
            cp.start()

    def finish(ins, outs, scr):
        for cp in copies(ins, outs, scr):
            cp.wait()

    return _Rider(stacks, out_shapes, scratch, start, finish)


def _chip_rider(pairs, axes):
    n = len(pairs)
    shapes = [t.shape for t in pairs]
    out_shapes = [SDS((3,) + _scaled(s2, ax, 1, N_CHIPS), t.dtype) for s2, ax, t in zip(shapes, axes, pairs)]
    scratch = [pltpu.SemaphoreType.DMA((n, 3)), pltpu.SemaphoreType.DMA((n, 3))]

    def copies(ins, outs, scr):
        x, y, c = _position()
        return [
            _remote(
                _sub(ins[k], (), shapes[k], axes[k], chip=2 * chip[0] + chip[1]), outs[k].at[j],
                scr[0].at[k, j], scr[1].at[k, j], (*chip, c),
            )
            for k in range(n)
            for j, chip in enumerate(_other_chips(x, y))
        ]

    def start(ins, outs, scr):
        for cp in copies(ins, outs, scr):
            cp.start()

    def finish(ins, outs, scr):
        for cp in copies(ins, outs, scr):
            cp.wait()

    return _Rider(pairs, out_shapes, scratch, start, finish)


def _complete_rider(halves, axes, layer, stacks):
    n = len(halves)
    shapes = [_scaled(t.shape, 1 - ax, 2) for t, ax in zip(halves, axes)]
    out_shapes = [SDS((DEPTH,) + s2, F32) for s2 in shapes]
    scratch = [pltpu.SemaphoreType.DMA((n,)), pltpu.SemaphoreType.DMA((n,)), pltpu.SemaphoreType.DMA((n,))]
    inputs = list(halves)
    aliases = {}
    if stacks is not None:
        inputs += list(stacks)
        aliases = {n + k: k for k in range(n)}

    def start(ins, outs, scr):
        x, y, c = _position()
        for k in range(n):
            mine = _sub(outs[k], (layer,), shapes[k], axes[k], half=c)
            pltpu.make_async_copy(ins[k], mine, scr[2].at[k]).start()
            _remote(ins[k], mine, scr[0].at[k], scr[1].at[k], (x, y, 1 - c)).start()

    def finish(ins, outs, scr):
        x, y, c = _position()
        for k in range(n):
            mine = _sub(outs[k], (layer,), shapes[k], axes[k], half=c)
            theirs = _sub(outs[k], (layer,), shapes[k], axes[k], half=1 - c)
            _remote(ins[k], theirs, scr[0].at[k], scr[1].at[k], (x, y, 1 - c)).wait_recv()
            _remote(ins[k], mine, scr[0].at[k], scr[1].at[k], (x, y, 1 - c)).wait_send()
            pltpu.make_async_copy(ins[k], mine, scr[2].at[k]).wait()

    return _Rider(inputs, out_shapes, scratch, start, finish, aliases)


def _gather_small(buf):
    rows, cols = buf.shape
    flips = [(fx, fy, fc) for fx in (0, 1) for fy in (0, 1) for fc in (0, 1)][1:]

    def body(in_ref, out_ref, send_sems, recv_sems, local_sem):
        x, y, c = _position()

        def peer(f):
            return (x + f[0] - 2 * x * f[0], y + f[1] - 2 * y * f[1], c + f[2] - 2 * c * f[2])

        me = 4 * x + 2 * y + c
        local = pltpu.make_async_copy(in_ref, out_ref.at[me], local_sem)
        local.start()
        copies = []
        for j, f in enumerate(flips):
            cp = _remote(in_ref, out_ref.at[me], send_sems.at[j], recv_sems.at[j], peer(f))
            cp.start()
            copies.append(cp)
        for j, f in enumerate(flips):
            px, py, pc = peer(f)
            _remote(in_ref, out_ref.at[4 * px + 2 * py + pc], send_sems.at[j], recv_sems.at[j], peer(f)).wait_recv()
        for cp in copies:
            cp.wait_send()
        local.wait()

    return pl.pallas_call(
        body,
        name="gather_small_grads",
        in_specs=[ANY],
        out_specs=ANY,
        out_shape=SDS((8, rows, cols), buf.dtype),
        scratch_shapes=[pltpu.SemaphoreType.DMA((7,)), pltpu.SemaphoreType.DMA((7,)), pltpu.SemaphoreType.DMA(())],
    )(buf)


def _pair_sum(idx, stack, layer, recv, axis, name):
    hk, hn = recv.shape
    tk = _divisor(hk, max(16, TILE_BYTES // (4 * hn)), 16)
    per = hk // tk

    def body(idx_ref, g_ref, r_ref, out_ref):
        out_ref[...] = (g_ref[...] + r_ref[...]).astype(out_ref.dtype)

    if axis == 1:
        mine = pl.BlockSpec((None, tk, hn), lambda i, idx_ref: (layer, idx_ref[0] * per + i, 0))
    else:
        mine = pl.BlockSpec((None, tk, hn), lambda i, idx_ref: (layer, i, idx_ref[0]))
    return pl.pallas_call(
        body,
        name=name,
        grid_spec=pltpu.PrefetchScalarGridSpec(
            num_scalar_prefetch=1,
            grid=(per,),
            in_specs=[mine, pl.BlockSpec((tk, hn), lambda i, idx_ref: (i, 0))],
            out_specs=pl.BlockSpec((tk, hn), lambda i, idx_ref: (i, 0)),
        ),
        out_shape=SDS((hk, hn), BF16),
        compiler_params=_params("parallel"),
    )(idx, stack, recv)


def _chip_sum(idx, pair, others, axis, name):
    _, sk, sn = others.shape
    tk = _divisor(sk, max(16, TILE_BYTES // (4 * sn)), 16)
    per = sk // tk

    def body(idx_ref, p_ref, o0_ref, o1_ref, o2_ref, out_ref):
        acc = p_ref[...].astype(F32) + o0_ref[...].astype(F32)
        out_ref[...] = (acc + o1_ref[...].astype(F32)) + o2_ref[...].astype(F32)

    if axis == 1:
        own = pl.BlockSpec((tk, sn), lambda i, idx_ref: (i, idx_ref[1]))
    else:
        own = pl.BlockSpec((tk, sn), lambda i, idx_ref: (idx_ref[1] * per + i, 0))
    other = lambda j: pl.BlockSpec((None, tk, sn), lambda i, idx_ref: (j, i, 0))
    return pl.pallas_call(
        body,
        name=name,
        grid_spec=pltpu.PrefetchScalarGridSpec(
            num_scalar_prefetch=1,
            grid=(per,),
            in_specs=[own, other(0), other(1), other(2)],
            out_specs=pl.BlockSpec((tk, sn), lambda i, idx_ref: (i, 0)),
        ),
        out_shape=SDS((sk, sn), F32),
        compiler_params=_params("parallel"),
    )(idx, pair, others, others, others)


def _sum_devices(gathered):
    _, rows, cols = gathered.shape

    def body(g_ref, out_ref):
        acc = g_ref[0]
        for dev in range(1, 8):
            acc = acc + g_ref[dev]
        out_ref[...] = acc

    return pl.pallas_call(body, name="sum_small_grads", out_shape=SDS((rows, cols), F32))(gathered)


PACK_TILE = 8 * 128


def _pack(arrays):
    parts = []
    for t in arrays:
        flat = t.reshape(-1)
        pad = (-flat.shape[0]) % PACK_TILE
        if pad:
            flat = jnp.concatenate([flat, jnp.zeros((pad,), flat.dtype)])
        parts.append(flat.reshape(-1, 128))
    return jnp.concatenate(parts, axis=0)


def _unpack(buf, shapes):
    out, row = [], 0
    for shp in shapes:
        size = math.prod(shp)
        rows = -(-size // PACK_TILE) * 8
        out.append(buf[row : row + rows].reshape(-1)[:size].reshape(shp))
        row += rows
    return out


BIG = ("w_in", "w_br_gm", "w_br_hg", "w_out", "w_up", "w_down")
BIG_AXIS = (1, 1, 1, 0, 1, 0)
REPLICATED = ("mix_norm", "gm_ln_g", "gm_ln_b", "gm_ws", "gm_bs", "hg_lb_logits", "hg_norm_g", "ffn_norm", "conv_b", "final_norm")
WEIGHTS = ("mix_norm", "w_in", "gm_ln_g", "gm_ln_b", "gm_ws", "gm_bs", "hg_lb_logits", "hg_norm_g", "w_br_gm", "w_br_hg", "w_out",
           "ffn_norm", "w_up", "conv_w", "conv_b", "w_down", "final_norm")


def kernel(x, mix_norm, w_in, gm_ln_g, gm_ln_b, gm_ws, gm_bs, hg_lb_logits, hg_norm_g, w_br_gm, w_br_hg, w_out, ffn_norm, w_up, conv_w, conv_b, w_down, final_norm, loss_target, m_mix_norm, m_w_in, m_gm_ln_g, m_gm_ln_b, m_gm_ws, m_gm_bs, m_hg_lb_logits, m_hg_norm_g, m_w_br_gm, m_w_br_hg, m_w_out, m_ffn_norm, m_w_up, m_conv_w, m_conv_b, m_w_down, m_final_norm, v_mix_norm, v_w_in, v_gm_ln_g, v_gm_ln_b, v_gm_ws, v_gm_bs, v_hg_lb_logits, v_hg_norm_g, v_w_br_gm, v_w_br_hg, v_w_out, v_ffn_norm, v_w_up, v_conv_w, v_conv_b, v_w_down, v_final_norm):
    w = dict(mix_norm=mix_norm, w_in=w_in, gm_ln_g=gm_ln_g, gm_ln_b=gm_ln_b, gm_ws=gm_ws, gm_bs=gm_bs, hg_lb_logits=hg_lb_logits,
             hg_norm_g=hg_norm_g, w_br_gm=w_br_gm, w_br_hg=w_br_hg, w_out=w_out, ffn_norm=ffn_norm, w_up=w_up, conv_w=conv_w,
             conv_b=conv_b, w_down=w_down, final_norm=final_norm)
    m = dict(mix_norm=m_mix_norm, w_in=m_w_in, gm_ln_g=m_gm_ln_g, gm_ln_b=m_gm_ln_b, gm_ws=m_gm_ws, gm_bs=m_gm_bs,
             hg_lb_logits=m_hg_lb_logits, hg_norm_g=m_hg_norm_g, w_br_gm=m_w_br_gm, w_br_hg=m_w_br_hg, w_out=m_w_out,
             ffn_norm=m_ffn_norm, w_up=m_w_up, conv_w=m_conv_w, conv_b=m_conv_b, w_down=m_w_down, final_norm=m_final_norm)
    v = dict(mix_norm=v_mix_norm, w_in=v_w_in, gm_ln_g=v_gm_ln_g, gm_ln_b=v_gm_ln_b, gm_ws=v_gm_ws, gm_bs=v_gm_bs,
             hg_lb_logits=v_hg_lb_logits, hg_norm_g=v_hg_norm_g, w_br_gm=v_w_br_gm, w_br_hg=v_w_br_hg, w_out=v_w_out,
             ffn_norm=v_ffn_norm, w_up=v_w_up, conv_w=v_conv_w, conv_b=v_conv_b, w_down=v_w_down, final_norm=v_final_norm)

    px, py, pc = _position()
    chip = 2 * px + py
    idx = jnp.stack([pc, chip]).astype(jnp.int32)
    axes = list(BIG_AXIS)
    d = x.shape[2]

    shards = [w[k].astype(MM_DTYPE) for k in BIG]
    got = _run_rider(_gather_rider(shards, axes, 0, conv_w=conv_w), "gather_weights_0")
    full = [dict(zip(BIG, got[: len(BIG)]))]
    conv_full = got[len(BIG)]
    lb = _lower_bounds(hg_lb_logits)
    xs = x[0]
    saved = []
    for l in range(DEPTH):
        fw = full[l]
        z, h = _norm_matmul(xs, mix_norm[l : l + 1], fw["w_in"], f"in_proj_{l}")
        bst = gm_bs[l].T
        a = _gmlp_fwd(z, gm_ln_g[l : l + 1], gm_ln_b[l : l + 1], gm_ws[l], bst, f"gmlp_fwd_{l}")
        rider = _gather_rider(shards, axes, l + 1) if l + 1 < DEPTH else None
        (o, b, states), got = _hgrn_fwd(z, lb[l : l + 1], hg_norm_g[l : l + 1], f"hgrn_fwd_{l}", rider)
        if rider is not None:
            full.append(dict(zip(BIG, got)))
        x1 = _mix_fwd(xs, a, b, z, fw["w_br_gm"], fw["w_br_hg"], fw["w_out"], f"mix_fwd_{l}")
        z2, h2 = _norm_matmul(x1, ffn_norm[l : l + 1], fw["w_up"], f"up_proj_{l}")
        x2 = _convffn_fwd(x1, z2, conv_full, conv_b[l : l + 1], fw["w_down"], l, f"convffn_fwd_{l}")
        saved.append((xs, h, z, a, b, o, states, x1, h2, z2, bst))
        xs = x2

    loss_tile, dx, d_final = _loss_head(xs, final_norm.reshape(1, d), loss_target[0])
    loss = lax.psum(loss_tile[0, 0], ("x", "y", "c"))

    big = {k: None for k in BIG}
    grads = None
    per_layer = [None] * DEPTH
    pairs = None
    halves = None

    for l in reversed(range(DEPTH)):
        x0, h, z, a, b, o, states, x1, h2, z2, bst = saved[l]
        fw = full[l]
        dz2, act, dconv = _convffn_bwd(dx, z2, conv_full, conv_b[l : l + 1], fw["w_down"], l, f"convffn_bwd_{l}")
        big["w_down"] = _matmul_tn(act, dx, big["w_down"], l, f"dw_down_{l}")
        big["w_up"] = _matmul_tn(h2, dz2, big["w_up"], l, f"dw_up_{l}")
        (dx1, d_ffn), _ = _proj_norm_bwd([dz2], fw["w_up"], x1, ffn_norm[l : l + 1], dx, f"up_proj_bwd_{l}")
        y, dpa, dpb, da, db, dgates = _mix_bwd(dx1, a, b, z, fw["w_br_gm"], fw["w_br_hg"], fw["w_out"], f"mix_bwd_{l}")
        big["w_out"] = _matmul_tn(y, dx1, big["w_out"], l, f"dw_out_{l}")
        big["w_br_gm"] = _matmul_tn(a, dpa, big["w_br_gm"], l, f"dw_br_gm_{l}")
        big["w_br_hg"] = _matmul_tn(b, dpb, big["w_br_hg"], l, f"dw_br_hg_{l}")
        rider = _chip_rider(pairs, axes) if pairs is not None else None
        (dz_hg, d_lb, d_ng), others = _hgrn_bwd(db, z, o, states, lb[l : l + 1], hg_norm_g[l : l + 1], f"hgrn_bwd_{l}", rider)
        if pairs is not None:
            halves = [_chip_sum(idx, p, ot, ax, f"chip_sum_{k}_{l + 1}") for k, p, ot, ax in zip(BIG, pairs, others, axes)]
        dz_gm, d_lng, d_lnb, d_ws, d_bst = _gmlp_bwd(da, z, gm_ln_g[l : l + 1], gm_ln_b[l : l + 1], gm_ws[l], bst, f"gmlp_bwd_{l}")
        n_in = fw["w_in"].shape[1]
        big["w_in"] = _matmul_tn(h, dz_gm, big["w_in"], l, f"dw_in_gm_{l}", n_total=n_in, col_off=0)
        big["w_in"] = _matmul_tn(h, dz_hg, big["w_in"], l, f"dw_in_hg_{l}", n_total=n_in, col_off=2 * BR_DIM)
        big["w_in"] = _matmul_tn(h, dgates, big["w_in"], l, f"dw_in_gate_{l}", n_total=n_in, col_off=6 * BR_DIM)
        rider = _pair_rider([big[k] for k in BIG], axes, l)
        if halves is not None:
            rider = _join(rider, _complete_rider(halves, axes, l + 1, grads))
        (dx, d_mix), got = _proj_norm_bwd([dz_gm, dz_hg, dgates], fw["w_in"], x0, mix_norm[l : l + 1], dx1, f"in_proj_bwd_{l}", rider)
        received = got[: len(BIG)]
        if halves is not None:
            grads = got[len(BIG) :]
        pairs = [_pair_sum(idx, big[k], l, r, ax, f"pair_sum_{k}_{l}") for k, r, ax in zip(BIG, received, axes)]
        per_layer[l] = dict(
            mix_norm=d_mix[0], gm_ln_g=d_lng[0], gm_ln_b=d_lnb[0], gm_ws=d_ws, gm_bs=d_bst.T, lb=d_lb[0], hg_norm_g=d_ng[0],
            ffn_norm=d_ffn[0], conv_w=dconv[0:3], conv_b=dconv[3],
        )

    others = _run_rider(_chip_rider(pairs, axes), "grad_chip_exchange_0")
    halves = [_chip_sum(idx, p, ot, ax, f"chip_sum_{k}_0") for k, p, ot, ax in zip(BIG, pairs, others, axes)]
    grads = _run_rider(_complete_rider(halves, axes, 0, grads), "grad_complete_0")
    grad = dict(zip(BIG, grads))

    small_grads = {k: jnp.stack([per_layer[l][k] for l in range(DEPTH)]) for k in per_layer[0]}
    small_grads["hg_lb_logits"] = _lower_bounds_bwd(hg_lb_logits, small_grads.pop("lb"))
    small_grads["final_norm"] = d_final[0]
    small_names = list(REPLICATED) + ["conv_w"]
    packed = _pack([small_grads[k] for k in small_names])
    summed = _sum_devices(_gather_small(packed))
    for k, g in zip(small_names, _unpack(summed, [small_grads[k].shape for k in small_names])):
        grad[k] = g
    grad["conv_w"] = lax.dynamic_slice_in_dim(grad["conv_w"], chip * conv_w.shape[2], conv_w.shape[2], axis=2)

    delta, new_m, new_v = {}, {}, {}
    for k in WEIGHTS:
        delta[k], new_m[k], new_v[k] = _adamw(w[k], grad[k], m[k], v[k], f"adamw_{k}")

    return (loss, dx[None], *[grad[k] for k in WEIGHTS], *[delta[k] for k in WEIGHTS], *[new_m[k] for k in WEIGHTS],
            *[new_v[k] for k in WEIGHTS])
```

```python
import math

import jax
import jax.numpy as jnp
from jax import lax
from jax.experimental import pallas as pl
from jax.experimental.pallas import tpu as pltpu

F32 = jnp.float32
BF16 = jnp.bfloat16
MM_DTYPE = BF16

N_HEADS = 4
HEAD_DIM = 128
BR_DIM = N_HEADS * HEAD_DIM
CHUNK = 64
GM_BLOCK = 128
DEPTH = 4
N_CHIPS = 4
EPS = 1e-6
TINY = 1e-30
LB_MAX = 0.999
ADAM_LR = 0.001
ADAM_B1 = 0.9
ADAM_B2 = 0.999
ADAM_EPS = 1e-08
ADAM_WD = 0.01
ADAM_STEP = 10
INV_SQRT2 = 1.0 / math.sqrt(2.0)
INV_SQRT_2PI = 1.0 / math.sqrt(2.0 * math.pi)

VMEM_LIMIT_BYTES = 56 * 1024 * 1024
TILE_BYTES = 2 * 1024 * 1024
ACC_BYTES = 6 * 1024 * 1024
MESH = pl.DeviceIdType.MESH
SDS = jax.ShapeDtypeStruct
ANY = pl.BlockSpec(memory_space=pl.ANY)


def _params(*sem):
    return pltpu.CompilerParams(dimension_semantics=sem or None, vmem_limit_bytes=VMEM_LIMIT_BYTES)


def _divisor(n, limit, mult):
    best = None
    for d in range(mult, min(n, limit) + 1, mult):
        if n % d == 0:
            best = d
    return best if best is not None else n


def _dot(a, b):
    return jnp.dot(a.astype(MM_DTYPE), b.astype(MM_DTYPE), preferred_element_type=F32)


def _dot_nt(a, b):
    return lax.dot_general(a.astype(MM_DTYPE), b.astype(MM_DTYPE), (((1,), (1,)), ((), ())), preferred_element_type=F32)


def _dot_tn(a, b):
    return lax.dot_general(a.astype(MM_DTYPE), b.astype(MM_DTYPE), (((0,), (0,)), ((), ())), preferred_element_type=F32)


def _dot_exact(a, b):
    return jnp.dot(a, b, preferred_element_type=F32, precision=lax.Precision.HIGHEST)


def _sigmoid(x):
    return 1.0 / (1.0 + jnp.exp(-x))


def _gelu(x):
    return 0.5 * x * (1.0 + lax.erf(x * INV_SQRT2))


def _gelu_grad(x):
    return 0.5 * (1.0 + lax.erf(x * INV_SQRT2)) + x * jnp.exp(-0.5 * x * x) * INV_SQRT_2PI


def _silu_grad(x, s):
    return s * (1.0 + x * (1.0 - s))


def _resident(shape, index_map):
    return pl.BlockSpec(shape, index_map, pipeline_mode=pl.Buffered(1))


class _Rider:
    def __init__(self, inputs, out_shapes, scratch, start, finish, aliases=None):
        self.inputs = list(inputs)
        self.out_shapes = list(out_shapes)
        self.scratch = list(scratch)
        self.start = start
        self.finish = finish
        self.aliases = dict(aliases or {})


def _join(a, b):
    if a is None or b is None:
        return a if b is None else b
    na, oa, sa = len(a.inputs), len(a.out_shapes), len(a.scratch)

    def start(i, o, s):
        a.start(i[:na], o[:oa], s[:sa])
        b.start(i[na:], o[oa:], s[sa:])

    def finish(i, o, s):
        a.finish(i[:na], o[:oa], s[:sa])
        b.finish(i[na:], o[oa:], s[sa:])

    aliases = dict(a.aliases)
    aliases.update({na + k: oa + v for k, v in b.aliases.items()})
    return _Rider(a.inputs + b.inputs, a.out_shapes + b.out_shapes, a.scratch + b.scratch, start, finish, aliases)


def _hosted_call(body, rider, *, name, grid, in_specs, out_specs, out_shape, scratch_shapes, args, semantics):
    n_in, n_out, n_scr = len(in_specs), len(out_specs), len(scratch_shapes)
    if rider is None:
        outs = pl.pallas_call(
            body, name=name, grid=grid, in_specs=in_specs, out_specs=out_specs, out_shape=out_shape,
            scratch_shapes=scratch_shapes, compiler_params=_params(*semantics),
        )(*args)
        return list(outs), []
    ri, ro = len(rider.inputs), len(rider.out_shapes)

    def wrapped(*refs):
        p = 0
        hin = refs[p : p + n_in]
        p += n_in
        rin = refs[p : p + ri]
        p += ri
        hout = refs[p : p + n_out]
        p += n_out
        rout = refs[p : p + ro]
        p += ro
        hscr = refs[p : p + n_scr]
        rscr = refs[p + n_scr :]

        @pl.when(pl.program_id(0) == 0)
        def _():
            rider.start(rin, rout, rscr)

        body(*hin, *hout, *hscr)

        @pl.when(pl.program_id(0) == grid[0] - 1)
        def _():
            rider.finish(rin, rout, rscr)

    outs = pl.pallas_call(
        wrapped,
        name=name,
        grid=grid,
        in_specs=list(in_specs) + [ANY] * ri,
        out_specs=list(out_specs) + [ANY] * ro,
        out_shape=list(out_shape) + rider.out_shapes,
        scratch_shapes=list(scratch_shapes) + rider.scratch,
        input_output_aliases={n_in + k: n_out + v for k, v in rider.aliases.items()},
        compiler_params=_params(*semantics),
    )(*args, *rider.inputs)
    return list(outs[:n_out]), list(outs[n_out:])


def _run_rider(rider, name):
    ri, ro = len(rider.inputs), len(rider.out_shapes)

    def body(*refs):
        rin, rout, rscr = refs[:ri], refs[ri : ri + ro], refs[ri + ro :]
        rider.start(rin, rout, rscr)
        rider.finish(rin, rout, rscr)

    return list(
        pl.pallas_call(
            body,
            name=name,
            in_specs=[ANY] * ri,
            out_specs=[ANY] * ro,
            out_shape=rider.out_shapes,
            scratch_shapes=rider.scratch,
            input_output_aliases=rider.aliases,
        )(*rider.inputs)
    )


def _lower_bounds(logits):
    def body(lg_ref, lb_ref):
        lg = lg_ref[...]
        mx = jnp.max(lg, axis=0, keepdims=True)
        e = jnp.exp(lg - mx)
        p = e / jnp.sum(e, axis=0, keepdims=True)
        run = p[0:1]
        rows = [run - p[0:1]]
        for l in range(1, DEPTH):
            run = run + p[l : l + 1]
            rows.append(run - p[0:1])
        raw = jnp.concatenate(rows, axis=0)
        lb_ref[...] = jnp.minimum(jnp.maximum(raw, 0.0), LB_MAX)

    return pl.pallas_call(body, name="lower_bounds", out_shape=SDS(logits.shape, F32))(logits)


def _lower_bounds_bwd(logits, dlb):
    def body(lg_ref, dlb_ref, dlg_ref):
        lg = lg_ref[...]
        mx = jnp.max(lg, axis=0, keepdims=True)
        e = jnp.exp(lg - mx)
        p = e / jnp.sum(e, axis=0, keepdims=True)
        run = p[0:1]
        rows = [run - p[0:1]]
        for l in range(1, DEPTH):
            run = run + p[l : l + 1]
            rows.append(run - p[0:1])
        raw = jnp.concatenate(rows, axis=0)
        lo = jnp.where(raw > 0.0, 1.0, jnp.where(raw == 0.0, 0.5, 0.0))
        clipped = jnp.maximum(raw, 0.0)
        hi = jnp.where(clipped < LB_MAX, 1.0, jnp.where(clipped == LB_MAX, 0.5, 0.0))
        draw = dlb_ref[...] * lo * hi
        total = jnp.sum(draw, axis=0, keepdims=True)
        dps = []
        tail = total
        for j in range(DEPTH):
            dps.append(tail - total if j == 0 else tail)
            tail = tail - draw[j : j + 1]
        dp = jnp.concatenate(dps, axis=0)
        dlg_ref[...] = p * (dp - jnp.sum(p * dp, axis=0, keepdims=True))

    return pl.pallas_call(body, name="lower_bounds_bwd", out_shape=SDS(logits.shape, F32))(logits, dlb)


def _norm_matmul(x, gain, w, name):
    s, d = x.shape
    n = w.shape[1]
    tm = _divisor(s, 256, 8)

    def body(x_ref, g_ref, w_ref, z_ref, h_ref):
        xv = x_ref[...]
        r = lax.rsqrt(jnp.mean(xv * xv, axis=-1, keepdims=True) + EPS)
        h = ((xv * r) * g_ref[...]).astype(MM_DTYPE)
        h_ref[...] = h
        z_ref[...] = jnp.dot(h, w_ref[...], preferred_element_type=F32)

    return pl.pallas_call(
        body,
        name=name,
        grid=(s // tm,),
        in_specs=[
            pl.BlockSpec((tm, d), lambda i: (i, 0)),
            _resident((1, d), lambda i: (0, 0)),
            _resident((d, n), lambda i: (0, 0)),
        ],
        out_specs=[pl.BlockSpec((tm, n), lambda i: (i, 0)), pl.BlockSpec((tm, d), lambda i: (i, 0))],
        out_shape=[SDS((s, n), F32), SDS((s, d), MM_DTYPE)],
        compiler_params=_params("parallel"),
    )(x, gain, w)


def _proj_norm_bwd(dz_parts, w, x, gain, dres, name, rider=None):
    s, d = x.shape
    n = w.shape[1]
    tm = _divisor(s, 256, 16)
    widths = [p.shape[1] for p in dz_parts]
    offsets = [sum(widths[:k]) for k in range(len(widths))]
    n_parts = len(dz_parts)

    def body(*refs):
        dz_refs = refs[:n_parts]
        w_ref, x_ref, g_ref, dres_ref, dx_ref, dg_ref = refs[n_parts:]

        @pl.when(pl.program_id(0) == 0)
        def _():
            dg_ref[...] = jnp.zeros_like(dg_ref)

        dh = None
        for dz_ref, off, wd in zip(dz_refs, offsets, widths):
            part = _dot_nt(dz_ref[...], w_ref[:, off : off + wd])
            dh = part if dh is None else dh + part
        xv = x_ref[...]
        r = lax.rsqrt(jnp.mean(xv * xv, axis=-1, keepdims=True) + EPS)
        xh = xv * r
        dg_ref[...] += jnp.sum(dh * xh, axis=0, keepdims=True)
        dxh = dh * g_ref[...]
        dx_ref[...] = dres_ref[...] + r * (dxh - xh * jnp.mean(dxh * xh, axis=-1, keepdims=True))

    in_specs = [pl.BlockSpec((tm, wd), lambda i: (i, 0)) for wd in widths] + [
        _resident((d, n), lambda i: (0, 0)),
        pl.BlockSpec((tm, d), lambda i: (i, 0)),
        _resident((1, d), lambda i: (0, 0)),
        pl.BlockSpec((tm, d), lambda i: (i, 0)),
    ]
    return _hosted_call(
        body,
        rider,
        name=name,
        grid=(s // tm,),
        in_specs=in_specs,
        out_specs=[pl.BlockSpec((tm, d), lambda i: (i, 0)), pl.BlockSpec((1, d), lambda i: (0, 0))],
        out_shape=[SDS((s, d), F32), SDS((1, d), F32)],
        scratch_shapes=[],
        args=[*dz_parts, w, x, gain, dres],
        semantics=("arbitrary",),
    )


def _matmul_tn(a, b, stack, layer, name, n_total=None, col_off=0):
    s, k = a.shape
    n = b.shape[1]
    n_total = n if n_total is None else n_total
    tn = _divisor(math.gcd(n, col_off) if col_off else n, max(128, ACC_BYTES // (4 * k)), 128)
    ts = _divisor(s, 512, 16)
    off = col_off // tn

    def body(a_ref, b_ref, *rest):
        out_ref = rest[-1]

        @pl.when(pl.program_id(1) == 0)
        def _():
            out_ref[...] = jnp.zeros_like(out_ref)

        out_ref[...] += _dot_tn(a_ref[...], b_ref[...])

    in_specs = [pl.BlockSpec((ts, k), lambda j, i: (i, 0)), pl.BlockSpec((ts, tn), lambda j, i: (i, j))]
    args = [a, b]
    aliases = {}
    if stack is not None:
        in_specs.append(ANY)
        args.append(stack)
        aliases = {2: 0}
    return pl.pallas_call(
        body,
        name=name,
        grid=(n // tn, s // ts),
        in_specs=in_specs,
        out_specs=pl.BlockSpec((None, k, tn), lambda j, i: (layer, 0, off + j)),
        out_shape=SDS((DEPTH, k, n_total), F32),
        input_output_aliases=aliases,
        compiler_params=_params("parallel", "arbitrary"),
    )(*args)


def _gm_mask():
    r = lax.broadcasted_iota(jnp.int32, (GM_BLOCK, GM_BLOCK), 0) // CHUNK
    c = lax.broadcasted_iota(jnp.int32, (GM_BLOCK, GM_BLOCK), 1) // CHUNK
    return r >= c


def _gm_normed(v, lng, lnb):
    vg = _gelu(v)
    mu = jnp.mean(vg, axis=-1, keepdims=True)
    xc = vg - mu
    rstd = lax.rsqrt(jnp.mean(xc * xc, axis=-1, keepdims=True) + EPS)
    xh = xc * rstd
    return xh, rstd, xh * lng + lnb


def _gmlp_fwd(z, lng, lnb, ws, bst, name):
    s = z.shape[0]
    tg = _divisor(s, 256, GM_BLOCK)

    def body(z_ref, lng_ref, lnb_ref, ws_ref, bst_ref, a_ref):
        mask = _gm_mask()
        ug = _gelu(z_ref[:, :BR_DIM])
        _, _, vn = _gm_normed(z_ref[:, BR_DIM:], lng_ref[...], lnb_ref[...])
        vn = vn.astype(MM_DTYPE)
        for h in range(N_HEADS):
            cs = slice(h * HEAD_DIM, (h + 1) * HEAD_DIM)
            wm = jnp.where(mask, ws_ref[h], 0.0).astype(MM_DTYPE)
            for blk in range(tg // GM_BLOCK):
                rs = slice(blk * GM_BLOCK, (blk + 1) * GM_BLOCK)
                mixed = _dot(wm, vn[rs, cs]) + bst_ref[:, h : h + 1]
                a_ref[rs, cs] = (ug[rs, cs] * mixed).astype(MM_DTYPE)

    return pl.pallas_call(
        body,
        name=name,
        grid=(s // tg,),
        in_specs=[
            pl.BlockSpec((tg, 2 * BR_DIM), lambda i: (i, 0)),
            _resident((1, BR_DIM), lambda i: (0, 0)),
            _resident((1, BR_DIM), lambda i: (0, 0)),
            _resident((N_HEADS, GM_BLOCK, GM_BLOCK), lambda i: (0, 0, 0)),
            _resident((GM_BLOCK, N_HEADS), lambda i: (0, 0)),
        ],
        out_specs=pl.BlockSpec((tg, BR_DIM), lambda i: (i, 0)),
        out_shape=SDS((s, BR_DIM), MM_DTYPE),
        compiler_params=_params("parallel"),
    )(z, lng, lnb, ws, bst)


def _gmlp_bwd(da, z, lng, lnb, ws, bst, name):
    s = z.shape[0]
    tg = _divisor(s, 256, GM_BLOCK)

    def body(da_ref, z_ref, lng_ref, lnb_ref, ws_ref, bst_ref, dz_ref, dlng_ref, dlnb_ref, dws_ref, dbst_ref):
        @pl.when(pl.program_id(0) == 0)
        def _():
            dlng_ref[...] = jnp.zeros_like(dlng_ref)
            dlnb_ref[...] = jnp.zeros_like(dlnb_ref)
            dws_ref[...] = jnp.zeros_like(dws_ref)
            dbst_ref[...] = jnp.zeros_like(dbst_ref)

        mask = _gm_mask()
        u = z_ref[:, :BR_DIM]
        v = z_ref[:, BR_DIM:]
        ug = _gelu(u)
        lng_v = lng_ref[...]
        xh, rstd, vn = _gm_normed(v, lng_v, lnb_ref[...])
        vnb = vn.astype(MM_DTYPE)
        da_v = da_ref[...]
        dmix = da_v * ug
        dmixb = dmix.astype(MM_DTYPE)
        dvn_cols = []
        for h in range(N_HEADS):
            cs = slice(h * HEAD_DIM, (h + 1) * HEAD_DIM)
            wm = jnp.where(mask, ws_ref[h], 0.0).astype(MM_DTYPE)
            dw = jnp.zeros((GM_BLOCK, GM_BLOCK), F32)
            dbias = jnp.zeros((GM_BLOCK, 1), F32)
            dvn_rows = []
            for blk in range(tg // GM_BLOCK):
                rs = slice(blk * GM_BLOCK, (blk + 1) * GM_BLOCK)
                mixed = _dot(wm, vnb[rs, cs]) + bst_ref[:, h : h + 1]
                dz_ref[rs, cs] = (da_v[rs, cs] * mixed * _gelu_grad(u[rs, cs])).astype(MM_DTYPE)
                dw = dw + _dot_nt(dmixb[rs, cs], vnb[rs, cs])
                dbias = dbias + jnp.sum(dmix[rs, cs], axis=1, keepdims=True)
                dvn_rows.append(_dot_tn(wm, dmixb[rs, cs]))
            dws_ref[h] += jnp.where(mask, dw, 0.0)
            dbst_ref[:, h : h + 1] += dbias
            dvn_cols.append(jnp.concatenate(dvn_rows, axis=0) if len(dvn_rows) > 1 else dvn_rows[0])
        dvn = jnp.concatenate(dvn_cols, axis=1)
        dlng_ref[...] += jnp.sum(dvn * xh, axis=0, keepdims=True)
        dlnb_ref[...] += jnp.sum(dvn, axis=0, keepdims=True)
        dxh = dvn * lng_v
        dvg = rstd * (dxh - jnp.mean(dxh, axis=-1, keepdims=True) - xh * jnp.mean(dxh * xh, axis=-1, keepdims=True))
        dz_ref[:, BR_DIM:] = (dvg * _gelu_grad(v)).astype(MM_DTYPE)

    return pl.pallas_call(
        body,
        name=name,
        grid=(s // tg,),
        in_specs=[
            pl.BlockSpec((tg, BR_DIM), lambda i: (i, 0)),
            pl.BlockSpec((tg, 2 * BR_DIM), lambda i: (i, 0)),
            _resident((1, BR_DIM), lambda i: (0, 0)),
            _resident((1, BR_DIM), lambda i: (0, 0)),
            _resident((N_HEADS, GM_BLOCK, GM_BLOCK), lambda i: (0, 0, 0)),
            _resident((GM_BLOCK, N_HEADS), lambda i: (0, 0)),
        ],
        out_specs=[
            pl.BlockSpec((tg, 2 * BR_DIM), lambda i: (i, 0)),
            pl.BlockSpec((1, BR_DIM), lambda i: (0, 0)),
            pl.BlockSpec((1, BR_DIM), lambda i: (0, 0)),
            pl.BlockSpec((N_HEADS, GM_BLOCK, GM_BLOCK), lambda i: (0, 0, 0)),
            pl.BlockSpec((GM_BLOCK, N_HEADS), lambda i: (0, 0)),
        ],
        out_shape=[
            SDS((s, 2 * BR_DIM), MM_DTYPE),
            SDS((1, BR_DIM), F32),
            SDS((1, BR_DIM), F32),
            SDS((N_HEADS, GM_BLOCK, GM_BLOCK), F32),
            SDS((GM_BLOCK, N_HEADS), F32),
        ],
        compiler_params=_params("arbitrary"),
    )(da, z, lng, lnb, ws, bst)


HG_ROWS = 256
MID = CHUNK // 2 - 1


def _tri(lower):
    r = lax.broadcasted_iota(jnp.int32, (CHUNK, CHUNK), 0)
    c = lax.broadcasted_iota(jnp.int32, (CHUNK, CHUNK), 1)
    return r >= c if lower else c >= r


def _hgrn_fwd(z, lb, ng, name, rider=None):
    s = z.shape[0]
    rows_per = _divisor(s, HG_ROWS, CHUNK)
    n_sub = rows_per // CHUNK

    def body(zqf_ref, zio_ref, lb_ref, ng_ref, o_ref, b_ref, st_ref, state):
        @pl.when(pl.program_id(0) == 0)
        def _():
            state[...] = jnp.zeros_like(state)

        low = _tri(True)
        lowf = low.astype(F32)
        ng_v = ng_ref[...]

        def chunk(ci, carry):
            rows = pl.ds(pl.multiple_of(ci * CHUNK, CHUNK), CHUNK)
            for h in range(N_HEADS):
                cs = slice(h * HEAD_DIM, (h + 1) * HEAD_DIM)
                cs2 = slice(BR_DIM + h * HEAD_DIM, BR_DIM + (h + 1) * HEAD_DIM)
                zq = zqf_ref[rows, cs]
                zf = zqf_ref[rows, cs2]
                vi = zio_ref[rows, cs]
                zog = zio_ref[rows, cs2]
                lbh = lb_ref[:, cs]
                qf = zq * _sigmoid(zq)
                forget = lbh + (1.0 - lbh) * _sigmoid(zf)
                g = jnp.log(jnp.maximum(forget, TINY))
                k = (1.0 - lbh) * _sigmoid(-zf)
                gc = _dot_exact(lowf, g)
                gm = gc[MID : MID + 1]
                gl = gc[CHUNK - 1 : CHUNK]
                a = jnp.where(low, _dot_nt(qf * jnp.exp(gc - gm), k * jnp.exp(gm - gc)), 0.0)
                st = state[h]
                st_ref[ci, h] = st
                o = _dot(a, vi) + _dot_nt(qf * jnp.exp(gc), st)
                state[h] = st * jnp.exp(gl) + _dot_tn(vi, k * jnp.exp(gl - gc))
                r = lax.rsqrt(jnp.mean(o * o, axis=-1, keepdims=True) + EPS)
                o_ref[rows, cs] = o
                b_ref[rows, cs] = (((o * r) * ng_v) * (zog * _sigmoid(zog))).astype(MM_DTYPE)
            return carry

        lax.fori_loop(0, n_sub, chunk, 0)

    return _hosted_call(
        body,
        rider,
        name=name,
        grid=(s // rows_per,),
        in_specs=[
            pl.BlockSpec((rows_per, 2 * BR_DIM), lambda i: (i, 1)),
            pl.BlockSpec((rows_per, 2 * BR_DIM), lambda i: (i, 2)),
            _resident((1, BR_DIM), lambda i: (0, 0)),
            _resident((1, HEAD_DIM), lambda i: (0, 0)),
        ],
        out_specs=[
            pl.BlockSpec((rows_per, BR_DIM), lambda i: (i, 0)),
            pl.BlockSpec((rows_per, BR_DIM), lambda i: (i, 0)),
            pl.BlockSpec((n_sub, N_HEADS, HEAD_DIM, HEAD_DIM), lambda i: (i, 0, 0, 0)),
        ],
        out_shape=[
            SDS((s, BR_DIM), F32),
            SDS((s, BR_DIM), MM_DTYPE),
            SDS((s // CHUNK, N_HEADS, HEAD_DIM, HEAD_DIM), F32),
        ],
        scratch_shapes=[pltpu.VMEM((N_HEADS, HEAD_DIM, HEAD_DIM), F32)],
        args=[z, z, lb, ng],
        semantics=("arbitrary",),
    )


def _hgrn_bwd(db, z, o, states, lb, ng, name, rider=None):
    s = z.shape[0]
    rows_per = _divisor(s, HG_ROWS, CHUNK)
    n_sub = rows_per // CHUNK
    n_steps = s // rows_per

    def body(db_ref, zqf_ref, zio_ref, o_ref, st_ref, lb_ref, ng_ref, dz_ref, dlb_ref, dng_ref, dstate):
        @pl.when(pl.program_id(0) == 0)
        def _():
            dstate[...] = jnp.zeros_like(dstate)
            dlb_ref[...] = jnp.zeros_like(dlb_ref)
            dng_ref[...] = jnp.zeros_like(dng_ref)

        low = _tri(True)
        lowf = low.astype(F32)
        upf = _tri(False).astype(F32)
        ng_v = ng_ref[...]

        def chunk(cc, carry):
            ci = n_sub - 1 - cc
            rows = pl.ds(pl.multiple_of(ci * CHUNK, CHUNK), CHUNK)
            for h in range(N_HEADS):
                cs = slice(h * HEAD_DIM, (h + 1) * HEAD_DIM)
                cs2 = slice(BR_DIM + h * HEAD_DIM, BR_DIM + (h + 1) * HEAD_DIM)
                zq = zqf_ref[rows, cs]
                zf = zqf_ref[rows, cs2]
                vi = zio_ref[rows, cs]
                zog = zio_ref[rows, cs2]
                lbh = lb_ref[:, cs]
                ov = o_ref[rows, cs]
                dbv = db_ref[rows, cs]
                r = lax.rsqrt(jnp.mean(ov * ov, axis=-1, keepdims=True) + EPS)
                on = ov * r
                sgo = _sigmoid(zog)
                gate = zog * sgo
                dng_ref[...] += jnp.sum(dbv * gate * on, axis=0, keepdims=True)
                don = dbv * gate * ng_v
                dzog = dbv * (on * ng_v) * _silu_grad(zog, sgo)
                do = r * (don - on * jnp.mean(don * on, axis=-1, keepdims=True))
                sq = _sigmoid(zq)
                qf = zq * sq
                sg = _sigmoid(zf)
                sgn = _sigmoid(-zf)
                oml = 1.0 - lbh
                forget = lbh + oml * sg
                fm = jnp.maximum(forget, TINY)
                k = oml * sgn
                gc = _dot_exact(lowf, jnp.log(fm))
                gm = gc[MID : MID + 1]
                gl = gc[CHUNK - 1 : CHUNK]
                e_g = jnp.exp(gc)
                e_q = jnp.exp(gc - gm)
                e_k = jnp.exp(gm - gc)
                e_kd = jnp.exp(gl - gc)
                e_gl = jnp.exp(gl)
                qt = qf * e_q
                kt = k * e_k
                a = jnp.where(low, _dot_nt(qt, kt), 0.0)
                st = st_ref[ci, h]
                dst = dstate[h]
                dp = jnp.where(low, _dot_nt(do, vi), 0.0)
                dv = _dot_tn(a, do) + _dot_nt(k * e_kd, dst)
                dq = _dot(dp, kt) * e_q + e_g * _dot(do, st)
                dks = e_kd * _dot(vi, dst)
                dk = _dot_tn(dp, qt) * e_k + dks
                dstate[h] = _dot_tn(do, qf * e_g) + dst * e_gl
                dgc = qf * dq - k * dk
                extra = e_gl * jnp.sum(st * dst, axis=0, keepdims=True) + jnp.sum(k * dks, axis=0, keepdims=True)
                dg = _dot_exact(upf, dgc) + extra
                dforget = jnp.where(forget > TINY, dg / fm, 0.0)
                both = dforget - dk
                dlb_ref[:, cs] += jnp.sum(sgn * both, axis=0, keepdims=True)
                dz_ref[rows, cs] = (dq * _silu_grad(zq, sq)).astype(MM_DTYPE)
                dz_ref[rows, cs2] = (oml * sg * sgn * both).astype(MM_DTYPE)
                dz_ref[rows, 2 * BR_DIM + h * HEAD_DIM : 2 * BR_DIM + (h + 1) * HEAD_DIM] = dv.astype(MM_DTYPE)
                dz_ref[rows, 3 * BR_DIM + h * HEAD_DIM : 3 * BR_DIM + (h + 1) * HEAD_DIM] = dzog.astype(MM_DTYPE)
            return carry

        lax.fori_loop(0, n_sub, chunk, 0)

    rev = lambda i: n_steps - 1 - i
    return _hosted_call(
        body,
        rider,
        name=name,
        grid=(n_steps,),
        in_specs=[
            pl.BlockSpec((rows_per, BR_DIM), lambda i: (rev(i), 0)),
            pl.BlockSpec((rows_per, 2 * BR_DIM), lambda i: (rev(i), 1)),
            pl.BlockSpec((rows_per, 2 * BR_DIM), lambda i: (rev(i), 2)),
            pl.BlockSpec((rows_per, BR_DIM), lambda i: (rev(i), 0)),
            pl.BlockSpec((n_sub, N_HEADS, HEAD_DIM, HEAD_DIM), lambda i: (rev(i), 0, 0, 0)),
            _resident((1, BR_DIM), lambda i: (0, 0)),
            _resident((1, HEAD_DIM), lambda i: (0, 0)),
        ],
        out_specs=[
            pl.BlockSpec((rows_per, 4 * BR_DIM), lambda i: (rev(i), 0)),
            pl.BlockSpec((1, BR_DIM), lambda i: (0, 0)),
            pl.BlockSpec((1, HEAD_DIM), lambda i: (0, 0)),
        ],
        out_shape=[SDS((s, 4 * BR_DIM), MM_DTYPE), SDS((1, BR_DIM), F32), SDS((1, HEAD_DIM), F32)],
        scratch_shapes=[pltpu.VMEM((N_HEADS, HEAD_DIM, HEAD_DIM), F32)],
        args=[db, z, z, o, states, lb, ng],
        semantics=("arbitrary",),
    )


def _mix_fwd(x, a, b, z, w_gm, w_hg, w_out, name):
    s, d = x.shape
    tm = _divisor(s, 256, 16)
    g0 = 6 * BR_DIM // d

    def body(x_ref, a_ref, b_ref, ga_ref, gb_ref, wgm_ref, whg_ref, wout_ref, out_ref):
        y = _sigmoid(ga_ref[...]) * _dot(a_ref[...], wgm_ref[...]) + _sigmoid(gb_ref[...]) * _dot(b_ref[...], whg_ref[...])
        out_ref[...] = x_ref[...] + _dot(y, wout_ref[...])

    return pl.pallas_call(
        body,
        name=name,
        grid=(s // tm,),
        in_specs=[
            pl.BlockSpec((tm, d), lambda i: (i, 0)),
            pl.BlockSpec((tm, BR_DIM), lambda i: (i, 0)),
            pl.BlockSpec((tm, BR_DIM), lambda i: (i, 0)),
            pl.BlockSpec((tm, d), lambda i: (i, g0)),
            pl.BlockSpec((tm, d), lambda i: (i, g0 + 1)),
            _resident((BR_DIM, d), lambda i: (0, 0)),
            _resident((BR_DIM, d), lambda i: (0, 0)),
            _resident((d, d), lambda i: (0, 0)),
        ],
        out_specs=pl.BlockSpec((tm, d), lambda i: (i, 0)),
        out_shape=SDS((s, d), F32),
        compiler_params=_params("parallel"),
    )(x, a, b, z, z, w_gm, w_hg, w_out)


def _mix_bwd(dx, a, b, z, w_gm, w_hg, w_out, name):
    s, d = dx.shape
    tm = _divisor(s, 256, 16)
    g0 = 6 * BR_DIM // d

    def body(dx_ref, a_ref, b_ref, ga_ref, gb_ref, wgm_ref, whg_ref, wout_ref, y_ref, dpa_ref, dpb_ref, da_ref, db_ref, dg_ref):
        dy = _dot_nt(dx_ref[...], wout_ref[...])
        pa = _dot(a_ref[...], wgm_ref[...])
        pb = _dot(b_ref[...], whg_ref[...])
        sa = _sigmoid(ga_ref[...])
        sb = _sigmoid(gb_ref[...])
        y_ref[...] = (sa * pa + sb * pb).astype(MM_DTYPE)
        dpa = (dy * sa).astype(MM_DTYPE)
        dpb = (dy * sb).astype(MM_DTYPE)
        dpa_ref[...] = dpa
        dpb_ref[...] = dpb
        da_ref[...] = _dot_nt(dpa, wgm_ref[...])
        db_ref[...] = _dot_nt(dpb, whg_ref[...])
        dg_ref[:, :d] = (dy * pa * sa * (1.0 - sa)).astype(MM_DTYPE)
        dg_ref[:, d:] = (dy * pb * sb * (1.0 - sb)).astype(MM_DTYPE)

    row = lambda w: pl.BlockSpec((tm, w), lambda i: (i, 0))
    return pl.pallas_call(
        body,
        name=name,
        grid=(s // tm,),
        in_specs=[
            row(d),
            row(BR_DIM),
            row(BR_DIM),
            pl.BlockSpec((tm, d), lambda i: (i, g0)),
            pl.BlockSpec((tm, d), lambda i: (i, g0 + 1)),
            _resident((BR_DIM, d), lambda i: (0, 0)),
            _resident((BR_DIM, d), lambda i: (0, 0)),
            _resident((d, d), lambda i: (0, 0)),
        ],
        out_specs=[row(d), row(d), row(d), row(BR_DIM), row(BR_DIM), row(2 * d)],
        out_shape=[
            SDS((s, d), MM_DTYPE),
            SDS((s, d), MM_DTYPE),
            SDS((s, d), MM_DTYPE),
            SDS((s, BR_DIM), F32),
            SDS((s, BR_DIM), F32),
            SDS((s, 2 * d), MM_DTYPE),
        ],
        compiler_params=_params("parallel"),
    )(dx, a, b, z, z, w_gm, w_hg, w_out)


HALO = 8


def _shift_down(v, prev, n):
    rolled = pltpu.roll(v, n, 0)
    rid = lax.broadcasted_iota(jnp.int32, v.shape, 0)
    out = rolled
    for r in range(n):
        out = jnp.where(rid == r, prev[HALO - n + r : HALO - n + r + 1], out)
    return out


def _shift_up(v, nxt, n):
    tm = v.shape[0]
    rolled = pltpu.roll(v, tm - n, 0)
    rid = lax.broadcasted_iota(jnp.int32, v.shape, 0)
    out = rolled
    for r in range(n):
        out = jnp.where(rid == tm - n + r, nxt[r : r + 1], out)
    return out


def _conv_cols(f):
    return _divisor(f, 1408, 128)


def _convffn_fwd(x, z2, cw, cb, w_down, layer, name):
    s, d = x.shape
    f = z2.shape[1] // 2
    tm = _divisor(s, 256, 16)
    cwid = _conv_cols(f)
    per8 = tm // HALO

    def body(x_ref, z_ref, prev_ref, cw_ref, cb_ref, wd_ref, out_ref):
        first = pl.program_id(0) == 0
        acc = x_ref[...]
        for c0 in range(0, f, cwid):
            halves = []
            for base in (c0, f + c0):
                cols = slice(base, base + cwid)
                zc = z_ref[:, cols]
                prev = jnp.where(first, 0.0, prev_ref[:, cols])
                conv = cb_ref[:, cols] + cw_ref[0:1, cols] * _shift_down(zc, prev, 2)
                conv = conv + cw_ref[1:2, cols] * _shift_down(zc, prev, 1) + cw_ref[2:3, cols] * zc
                halves.append(conv)
            gate, val = halves
            act = gate * _sigmoid(gate) * val
            acc = acc + _dot(act, wd_ref[c0 : c0 + cwid, :])
        out_ref[...] = acc

    return pl.pallas_call(
        body,
        name=name,
        grid=(s // tm,),
        in_specs=[
            pl.BlockSpec((tm, d), lambda i: (i, 0)),
            pl.BlockSpec((tm, 2 * f), lambda i: (i, 0)),
            pl.BlockSpec((HALO, 2 * f), lambda i: (jnp.maximum(i * per8 - 1, 0), 0)),
            _resident((None, 3, 2 * f), lambda i: (layer, 0, 0)),
            _resident((1, 2 * f), lambda i: (0, 0)),
            _resident((f, d), lambda i: (0, 0)),
        ],
        out_specs=pl.BlockSpec((tm, d), lambda i: (i, 0)),
        out_shape=SDS((s, d), F32),
        compiler_params=_params("arbitrary"),
    )(x, z2, z2, cw, cb, w_down)


def _convffn_bwd(dx, z2, cw, cb, w_down, layer, name):
    s, d = dx.shape
    f = z2.shape[1] // 2
    tm = _divisor(s, 256, 16)
    cwid = _conv_cols(f)
    per8 = tm // HALO
    n_steps = s // tm

    def body(dx_ref, z_ref, prev_ref, cw_ref, cb_ref, wd_ref, dz_ref, act_ref, dcw_ref, nxt):
        step = pl.program_id(0)

        @pl.when(step == 0)
        def _():
            nxt[...] = jnp.zeros_like(nxt)
            dcw_ref[...] = jnp.zeros_like(dcw_ref)

        first_tile = step == n_steps - 1
        dxv = dx_ref[...].astype(MM_DTYPE)
        for c0 in range(0, f, cwid):
            dact = _dot_nt(dxv, wd_ref[c0 : c0 + cwid, :])
            zs, convs = [], []
            for base in (c0, f + c0):
                cols = slice(base, base + cwid)
                zc = z_ref[:, cols]
                prev = jnp.where(first_tile, 0.0, prev_ref[:, cols])
                zm1 = _shift_down(zc, prev, 1)
                zm2 = _shift_down(zc, prev, 2)
                convs.append(cb_ref[:, cols] + cw_ref[0:1, cols] * zm2 + cw_ref[1:2, cols] * zm1 + cw_ref[2:3, cols] * zc)
                zs.append((zc, zm1, zm2))
            gate, val = convs
            sg = _sigmoid(gate)
            silu = gate * sg
            act_ref[:, c0 : c0 + cwid] = (silu * val).astype(MM_DTYPE)
            dconvs = (dact * val * _silu_grad(gate, sg), dact * silu)
            for base, dconv, (zc, zm1, zm2) in zip((c0, f + c0), dconvs, zs):
                cols = slice(base, base + cwid)
                dcw_ref[0:1, cols] += jnp.sum(dconv * zm2, axis=0, keepdims=True)
                dcw_ref[1:2, cols] += jnp.sum(dconv * zm1, axis=0, keepdims=True)
                dcw_ref[2:3, cols] += jnp.sum(dconv * zc, axis=0, keepdims=True)
                dcw_ref[3:4, cols] += jnp.sum(dconv, axis=0, keepdims=True)
                nx = nxt[:, cols]
                dz = cw_ref[2:3, cols] * dconv + cw_ref[1:2, cols] * _shift_up(dconv, nx, 1)
                dz = dz + cw_ref[0:1, cols] * _shift_up(dconv, nx, 2)
                dz_ref[:, cols] = dz.astype(MM_DTYPE)
                nxt[:, cols] = dconv[0:HALO]

    rev = lambda i: n_steps - 1 - i
    return pl.pallas_call(
        body,
        name=name,
        grid=(n_steps,),
        in_specs=[
            pl.BlockSpec((tm, d), lambda i: (rev(i), 0)),
            pl.BlockSpec((tm, 2 * f), lambda i: (rev(i), 0)),
            pl.BlockSpec((HALO, 2 * f), lambda i: (jnp.maximum(rev(i) * per8 - 1, 0), 0)),
            _resident((None, 3, 2 * f), lambda i: (layer, 0, 0)),
            _resident((1, 2 * f), lambda i: (0, 0)),
            _resident((f, d), lambda i: (0, 0)),
        ],
        out_specs=[
            pl.BlockSpec((tm, 2 * f), lambda i: (rev(i), 0)),
            pl.BlockSpec((tm, f), lambda i: (rev(i), 0)),
            pl.BlockSpec((HALO, 2 * f), lambda i: (0, 0)),
        ],
        out_shape=[SDS((s, 2 * f), MM_DTYPE), SDS((s, f), MM_DTYPE), SDS((HALO, 2 * f), F32)],
        scratch_shapes=[pltpu.VMEM((HALO, 2 * f), F32)],
        compiler_params=_params("arbitrary"),
    )(dx, z2, z2, cw, cb, w_down)


def _loss_head(x, gain, target):
    s, d = x.shape
    tm = _divisor(s, 256, 8)

    def body(x_ref, g_ref, t_ref, loss_ref, dx_ref, dg_ref):
        @pl.when(pl.program_id(0) == 0)
        def _():
            loss_ref[...] = jnp.zeros_like(loss_ref)
            dg_ref[...] = jnp.zeros_like(dg_ref)

        xv = x_ref[...]
        gv = g_ref[...]
        r = lax.rsqrt(jnp.mean(xv * xv, axis=-1, keepdims=True) + EPS)
        xh = xv * r
        err = xh * gv - t_ref[...]
        loss_ref[...] += 0.5 * jnp.sum(jnp.mean(err * err, axis=-1, keepdims=True))
        dout = err * (1.0 / d)
        dg_ref[...] += jnp.sum(dout * xh, axis=0, keepdims=True)
        dxh = dout * gv
        dx_ref[...] = r * (dxh - xh * jnp.mean(dxh * xh, axis=-1, keepdims=True))

    return pl.pallas_call(
        body,
        name="loss_head",
        grid=(s // tm,),
        in_specs=[
            pl.BlockSpec((tm, d), lambda i: (i, 0)),
            _resident((1, d), lambda i: (0, 0)),
            pl.BlockSpec((tm, d), lambda i: (i, 0)),
        ],
        out_specs=[
            pl.BlockSpec((8, 128), lambda i: (0, 0)),
            pl.BlockSpec((tm, d), lambda i: (i, 0)),
            pl.BlockSpec((1, d), lambda i: (0, 0)),
        ],
        out_shape=[SDS((8, 128), F32), SDS((s, d), F32), SDS((1, d), F32)],
        compiler_params=_params("arbitrary"),
    )(x, gain, target)


def _adamw(w, g, m, v, name):
    shape = w.shape
    cols = shape[-1]
    rows = max(1, math.prod(shape[:-1]))
    tr = _divisor(rows, max(8, TILE_BYTES // (4 * cols)), 8)
    flat = [t.reshape(rows, cols) for t in (w, g, m, v)]

    def body(w_ref, g_ref, m_ref, v_ref, d_ref, nm_ref, nv_ref):
        gv = g_ref[...]
        m2 = ADAM_B1 * m_ref[...] + (1.0 - ADAM_B1) * gv
        v2 = ADAM_B2 * v_ref[...] + (1.0 - ADAM_B2) * (gv * gv)
        m_hat = m2 / (1.0 - ADAM_B1**ADAM_STEP)
        v_hat = v2 / (1.0 - ADAM_B2**ADAM_STEP)
        d_ref[...] = -ADAM_LR * (m_hat / (jnp.sqrt(v_hat) + ADAM_EPS) + ADAM_WD * w_ref[...])
        nm_ref[...] = m2
        nv_ref[...] = v2

    spec = pl.BlockSpec((tr, cols), lambda i: (i, 0))
    outs = pl.pallas_call(
        body,
        name=name,
        grid=(rows // tr,),
        in_specs=[spec] * 4,
        out_specs=[spec] * 3,
        out_shape=[SDS((rows, cols), F32)] * 3,
        compiler_params=_params("parallel"),
    )(*flat)
    return tuple(t.reshape(shape) for t in outs)


def _position():
    return lax.axis_index("x"), lax.axis_index("y"), lax.axis_index("c")


def _other_chips(x, y):
    return [(1 - x, y), (x, 1 - y), (1 - x, 1 - y)]


def _remote(src, dst, send_sem, recv_sem, device):
    return pltpu.make_async_remote_copy(
        src_ref=src, dst_ref=dst, send_sem=send_sem, recv_sem=recv_sem, device_id=device, device_id_type=MESH
    )


def _sub(ref, lead, shape, axis, chip=None, half=None):
    cut = [pl.ds(0, shape[0]), pl.ds(0, shape[1])]
    if chip is not None:
        size = shape[axis] // N_CHIPS
        cut[axis] = pl.ds(chip * size, size)
    if half is not None:
        size = shape[1 - axis] // 2
        cut[1 - axis] = pl.ds(half * size, size)
    return ref.at[(*lead, *cut)]


def _scaled(shape, axis, num, den=1):
    return tuple(d * num // den if i == axis else d for i, d in enumerate(shape))


def _gather_rider(shards, axes, layer, conv_w=None):
    n = len(shards)
    shard2d = [t.shape[1:] for t in shards]
    full2d = [_scaled(s2, ax, N_CHIPS) for s2, ax in zip(shard2d, axes)]
    out_shapes = [SDS(f2, t.dtype) for f2, t in zip(full2d, shards)]
    inputs = list(shards)
    own = 6
    scratch = [pltpu.SemaphoreType.DMA((n, 7)), pltpu.SemaphoreType.DMA((n, 7))]
    if conv_w is not None:
        cshape = conv_w.shape
        inputs.append(conv_w)
        out_shapes.append(SDS(_scaled(cshape, 2, N_CHIPS), conv_w.dtype))
        scratch += [pltpu.SemaphoreType.DMA((4,)), pltpu.SemaphoreType.DMA((4,))]

    def conv_slab(ref, chip):
        return ref.at[:, :, pl.ds((2 * chip[0] + chip[1]) * cshape[2], cshape[2])]

    def own_copy(ins, outs, send_sems, recv_sems, k):
        x, y, c = _position()
        slab = _sub(outs[k], (), full2d[k], axes[k], chip=2 * x + y)
        return _remote(ins[k].at[layer], slab, send_sems.at[k, own], recv_sems.at[k, own], (x, y, 1 - c))

    def start(ins, outs, scr):
        send_sems, recv_sems = scr[:2]
        x, y, c = _position()
        me = 2 * x + y
        for k in range(n):
            own_copy(ins, outs, send_sems, recv_sems, k).start()
            for j, chip in enumerate(_other_chips(x, y)):
                _remote(
                    _sub(ins[k], (layer,), shard2d[k], axes[k], half=c),
                    _sub(outs[k], (), full2d[k], axes[k], chip=me, half=c),
                    send_sems.at[k, j], recv_sems.at[k, j], (*chip, c),
                ).start()
        if conv_w is not None:
            csend, crecv = scr[2:]
            _remote(ins[n], conv_slab(outs[n], (x, y)), csend.at[3], crecv.at[3], (x, y, 1 - c)).start()
            for j, chip in enumerate(_other_chips(x, y)):
                _remote(ins[n], conv_slab(outs[n], (x, y)), csend.at[j], crecv.at[j], (*chip, c)).start()

    def finish(ins, outs, scr):
        send_sems, recv_sems = scr[:2]
        x, y, c = _position()
        me = 2 * x + y
        sibling = (x, y, 1 - c)
        chips = _other_chips(x, y)
        forwards = []
        for k in range(n):
            src = _sub(ins[k], (layer,), shard2d[k], axes[k], half=c)
            for j, chip in enumerate(chips):
                landed = _sub(outs[k], (), full2d[k], axes[k], chip=2 * chip[0] + chip[1], half=c)
                _remote(src, landed, send_sems.at[k, j], recv_sems.at[k, j], (*chip, c)).wait_recv()
                fwd = _remote(landed, landed, send_sems.at[k, 3 + j], recv_sems.at[k, 3 + j], sibling)
                fwd.start()
                forwards.append(fwd)
        for k in range(n):
            src = _sub(ins[k], (layer,), shard2d[k], axes[k], half=c)
            for j, chip in enumerate(chips):
                theirs = _sub(outs[k], (), full2d[k], axes[k], chip=2 * chip[0] + chip[1], half=1 - c)
                _remote(src, theirs, send_sems.at[k, 3 + j], recv_sems.at[k, 3 + j], sibling).wait_recv()
        for fwd in forwards:
            fwd.wait_send()
        for k in range(n):
            src = _sub(ins[k], (layer,), shard2d[k], axes[k], half=c)
            mine = _sub(outs[k], (), full2d[k], axes[k], chip=me, half=c)
            for j, chip in enumerate(chips):
                _remote(src, mine, send_sems.at[k, j], recv_sems.at[k, j], (*chip, c)).wait_send()
            own_copy(ins, outs, send_sems, recv_sems, k).wait()
        if conv_w is not None:
            csend, crecv = scr[2:]
            for j, chip in enumerate(chips):
                cp = _remote(ins[n], conv_slab(outs[n], chip), csend.at[j], crecv.at[j], (*chip, c))
                cp.wait_recv()
                cp.wait_send()
            _remote(ins[n], conv_slab(outs[n], (x, y)), csend.at[3], crecv.at[3], sibling).wait()

    return _Rider(inputs, out_shapes, scratch, start, finish)


def _pair_rider(stacks, axes, layer):
    n = len(stacks)
    shapes = [t.shape[1:] for t in stacks]
    out_shapes = [SDS(_scaled(s2, 1 - ax, 1, 2), F32) for s2, ax in zip(shapes, axes)]
    scratch = [pltpu.SemaphoreType.DMA((n,)), pltpu.SemaphoreType.DMA((n,))]

    def copies(ins, outs, scr):
        x, y, c = _position()
        return [
            _remote(_sub(ins[k], (layer,), shapes[k], axes[k], half=1 - c), outs[k], scr[0].at[k], scr[1].at[k], (x, y, 1 - c))
            for k in range(n)
        ]

    def start(ins, outs, scr):
        for cp in copies(ins, outs, scr):
            cp.start()

    def finish(ins, outs, scr):
        for cp in copies(ins, outs, scr):
            cp.wait()

    return _Rider(stacks, out_shapes, scratch, start, finish)


def _chip_rider(pairs, axes):
    n = len(pairs)
    shapes = [t.shape for t in pairs]
    out_shapes = [SDS((3,) + _scaled(s2, ax, 1, N_CHIPS), t.dtype) for s2, ax, t in zip(shapes, axes, pairs)]
    scratch = [pltpu.SemaphoreType.DMA((n, 3)), pltpu.SemaphoreType.DMA((n, 3))]

    def copies(ins, outs, scr):
        x, y, c = _position()
        return [
            _remote(
                _sub(ins[k], (), shapes[k], axes[k], chip=2 * chip[0] + chip[1]), outs[k].at[j],
                scr[0].at[k, j], scr[1].at[k, j], (*chip, c),
            )
            for k in range(n)
            for j, chip in enumerate(_other_chips(x, y))
        ]

    def start(ins, outs, scr):
        for cp in copies(ins, outs, scr):
            cp.start()

    def finish(ins, outs, scr):
        for cp in copies(ins, outs, scr):
            cp.wait()

    return _Rider(pairs, out_shapes, scratch, start, finish)


def _complete_rider(stacks, axes, layer):
    n = len(stacks)
    shapes = [t.shape[1:] for t in stacks]
    scratch = [pltpu.SemaphoreType.DMA((n,)), pltpu.SemaphoreType.DMA((n,))]

    def start(ins, outs, scr):
        x, y, c = _position()
        for k in range(n):
            mine = _sub(outs[k], (layer,), shapes[k], axes[k], half=c)
            _remote(mine, mine, scr[0].at[k], scr[1].at[k], (x, y, 1 - c)).start()

    def finish(ins, outs, scr):
        x, y, c = _position()
        for k in range(n):
            mine = _sub(outs[k], (layer,), shapes[k], axes[k], half=c)
            theirs = _sub(outs[k], (layer,), shapes[k], axes[k], half=1 - c)
            _remote(mine, theirs, scr[0].at[k], scr[1].at[k], (x, y, 1 - c)).wait_recv()
            _remote(mine, mine, scr[0].at[k], scr[1].at[k], (x, y, 1 - c)).wait_send()

    return _Rider(stacks, [SDS(t.shape, t.dtype) for t in stacks], scratch, start, finish, {k: k for k in range(n)})


def _gather_small(buf):
    rows, cols = buf.shape
    flips = [(fx, fy, fc) for fx in (0, 1) for fy in (0, 1) for fc in (0, 1)][1:]

    def body(in_ref, out_ref, send_sems, recv_sems, local_sem):
        x, y, c = _position()

        def peer(f):
            return (x + f[0] - 2 * x * f[0], y + f[1] - 2 * y * f[1], c + f[2] - 2 * c * f[2])

        me = 4 * x + 2 * y + c
        local = pltpu.make_async_copy(in_ref, out_ref.at[me], local_sem)
        local.start()
        copies = []
        for j, f in enumerate(flips):
            cp = _remote(in_ref, out_ref.at[me], send_sems.at[j], recv_sems.at[j], peer(f))
            cp.start()
            copies.append(cp)
        for j, f in enumerate(flips):
            px, py, pc = peer(f)
            _remote(in_ref, out_ref.at[4 * px + 2 * py + pc], send_sems.at[j], recv_sems.at[j], peer(f)).wait_recv()
        for cp in copies:
            cp.wait_send()
        local.wait()

    return pl.pallas_call(
        body,
        name="gather_small_grads",
        in_specs=[ANY],
        out_specs=ANY,
        out_shape=SDS((8, rows, cols), buf.dtype),
        scratch_shapes=[pltpu.SemaphoreType.DMA((7,)), pltpu.SemaphoreType.DMA((7,)), pltpu.SemaphoreType.DMA(())],
    )(buf)


def _pair_sum(idx, stack, layer, recv, axis, name):
    hk, hn = recv.shape
    tk = _divisor(hk, max(16, TILE_BYTES // (4 * hn)), 16)
    per = hk // tk

    def body(idx_ref, g_ref, r_ref, out_ref):
        out_ref[...] = (g_ref[...] + r_ref[...]).astype(out_ref.dtype)

    if axis == 1:
        mine = pl.BlockSpec((None, tk, hn), lambda i, idx_ref: (layer, idx_ref[0] * per + i, 0))
    else:
        mine = pl.BlockSpec((None, tk, hn), lambda i, idx_ref: (layer, i, idx_ref[0]))
    return pl.pallas_call(
        body,
        name=name,
        grid_spec=pltpu.PrefetchScalarGridSpec(
            num_scalar_prefetch=1,
            grid=(per,),
            in_specs=[mine, pl.BlockSpec((tk, hn), lambda i, idx_ref: (i, 0))],
            out_specs=pl.BlockSpec((tk, hn), lambda i, idx_ref: (i, 0)),
        ),
        out_shape=SDS((hk, hn), BF16),
        compiler_params=_params("parallel"),
    )(idx, stack, recv)


def _chip_sum(idx, pair, others, axis, stack, layer, name):
    _, sk, sn = others.shape
    tk = _divisor(sk, max(16, TILE_BYTES // (4 * sn)), 16)
    per = sk // tk

    def body(idx_ref, p_ref, o0_ref, o1_ref, o2_ref, *rest):
        out_ref = rest[-1]
        acc = p_ref[...].astype(F32) + o0_ref[...].astype(F32)
        out_ref[...] = (acc + o1_ref[...].astype(F32)) + o2_ref[...].astype(F32)

    if axis == 1:
        own = pl.BlockSpec((tk, sn), lambda i, idx_ref: (i, idx_ref[1]))
        out = pl.BlockSpec((None, tk, sn), lambda i, idx_ref: (layer, idx_ref[0] * per + i, 0))
        shard = (2 * sk, sn)
    else:
        own = pl.BlockSpec((tk, sn), lambda i, idx_ref: (idx_ref[1] * per + i, 0))
        out = pl.BlockSpec((None, tk, sn), lambda i, idx_ref: (layer, i, idx_ref[0]))
        shard = (sk, 2 * sn)
    other = lambda j: pl.BlockSpec((None, tk, sn), lambda i, idx_ref: (j, i, 0))
    in_specs = [own, other(0), other(1), other(2)]
    args = [idx, pair, others, others, others]
    aliases = {}
    if stack is not None:
        in_specs.append(ANY)
        args.append(stack)
        aliases = {5: 0}
    return pl.pallas_call(
        body,
        name=name,
        grid_spec=pltpu.PrefetchScalarGridSpec(num_scalar_prefetch=1, grid=(per,), in_specs=in_specs, out_specs=out),
        out_shape=SDS((DEPTH,) + shard, F32),
        input_output_aliases=aliases,
        compiler_params=_params("parallel"),
    )(*args)


def _sum_devices(gathered):
    _, rows, cols = gathered.shape

    def body(g_ref, out_ref):
        acc = g_ref[0]
        for dev in range(1, 8):
            acc = acc + g_ref[dev]
        out_ref[...] = acc

    return pl.pallas_call(body, name="sum_small_grads", out_shape=SDS((rows, cols), F32))(gathered)


PACK_TILE = 8 * 128


def _pack(arrays):
    parts = []
    for t in arrays:
        flat = t.reshape(-1)
        pad = (-flat.shape[0]) % PACK_TILE
        if pad:
            flat = jnp.concatenate([flat, jnp.zeros((pad,), flat.dtype)])
        parts.append(flat.reshape(-1, 128))
    return jnp.concatenate(parts, axis=0)


def _unpack(buf, shapes):
    out, row = [], 0
    for shp in shapes:
        size = math.prod(shp)
        rows = -(-size // PACK_TILE) * 8
        out.append(buf[row : row + rows].reshape(-1)[:size].reshape(shp))
        row += rows
    return out


BIG = ("w_in", "w_br_gm", "w_br_hg", "w_out", "w_up", "w_down")
BIG_AXIS = (1, 1, 1, 0, 1, 0)
REPLICATED = ("mix_norm", "gm_ln_g", "gm_ln_b", "gm_ws", "gm_bs", "hg_lb_logits", "hg_norm_g", "ffn_norm", "conv_b", "final_norm")
WEIGHTS = ("mix_norm", "w_in", "gm_ln_g", "gm_ln_b", "gm_ws", "gm_bs", "hg_lb_logits", "hg_norm_g", "w_br_gm", "w_br_hg", "w_out",
           "ffn_norm", "w_up", "conv_w", "conv_b", "w_down", "final_norm")


def kernel(x, mix_norm, w_in, gm_ln_g, gm_ln_b, gm_ws, gm_bs, hg_lb_logits, hg_norm_g, w_br_gm, w_br_hg, w_out, ffn_norm, w_up, conv_w, conv_b, w_down, final_norm, loss_target, m_mix_norm, m_w_in, m_gm_ln_g, m_gm_ln_b, m_gm_ws, m_gm_bs, m_hg_lb_logits, m_hg_norm_g, m_w_br_gm, m_w_br_hg, m_w_out, m_ffn_norm, m_w_up, m_conv_w, m_conv_b, m_w_down, m_final_norm, v_mix_norm, v_w_in, v_gm_ln_g, v_gm_ln_b, v_gm_ws, v_gm_bs, v_hg_lb_logits, v_hg_norm_g, v_w_br_gm, v_w_br_hg, v_w_out, v_ffn_norm, v_w_up, v_conv_w, v_conv_b, v_w_down, v_final_norm):
    w = dict(mix_norm=mix_norm, w_in=w_in, gm_ln_g=gm_ln_g, gm_ln_b=gm_ln_b, gm_ws=gm_ws, gm_bs=gm_bs, hg_lb_logits=hg_lb_logits,
             hg_norm_g=hg_norm_g, w_br_gm=w_br_gm, w_br_hg=w_br_hg, w_out=w_out, ffn_norm=ffn_norm, w_up=w_up, conv_w=conv_w,
             conv_b=conv_b, w_down=w_down, final_norm=final_norm)
    m = dict(mix_norm=m_mix_norm, w_in=m_w_in, gm_ln_g=m_gm_ln_g, gm_ln_b=m_gm_ln_b, gm_ws=m_gm_ws, gm_bs=m_gm_bs,
             hg_lb_logits=m_hg_lb_logits, hg_norm_g=m_hg_norm_g, w_br_gm=m_w_br_gm, w_br_hg=m_w_br_hg, w_out=m_w_out,
             ffn_norm=m_ffn_norm, w_up=m_w_up, conv_w=m_conv_w, conv_b=m_conv_b, w_down=m_w_down, final_norm=m_final_norm)
    v = dict(mix_norm=v_mix_norm, w_in=v_w_in, gm_ln_g=v_gm_ln_g, gm_ln_b=v_gm_ln_b, gm_ws=v_gm_ws, gm_bs=v_gm_bs,
             hg_lb_logits=v_hg_lb_logits, hg_norm_g=v_hg_norm_g, w_br_gm=v_w_br_gm, w_br_hg=v_w_br_hg, w_out=v_w_out,
             ffn_norm=v_ffn_norm, w_up=v_w_up, conv_w=v_conv_w, conv_b=v_conv_b, w_down=v_w_down, final_norm=v_final_norm)

    px, py, pc = _position()
    chip = 2 * px + py
    idx = jnp.stack([pc, chip]).astype(jnp.int32)
    axes = list(BIG_AXIS)
    d = x.shape[2]

    shards = [w[k].astype(MM_DTYPE) for k in BIG]
    got = _run_rider(_gather_rider(shards, axes, 0, conv_w=conv_w), "gather_weights_0")
    full = [dict(zip(BIG, got[: len(BIG)]))]
    conv_full = got[len(BIG)]
    lb = _lower_bounds(hg_lb_logits)
    xs = x[0]
    saved = []
    for l in range(DEPTH):
        fw = full[l]
        z, h = _norm_matmul(xs, mix_norm[l : l + 1], fw["w_in"], f"in_proj_{l}")
        bst = gm_bs[l].T
        a = _gmlp_fwd(z, gm_ln_g[l : l + 1], gm_ln_b[l : l + 1], gm_ws[l], bst, f"gmlp_fwd_{l}")
        rider = _gather_rider(shards, axes, l + 1) if l + 1 < DEPTH else None
        (o, b, states), got = _hgrn_fwd(z, lb[l : l + 1], hg_norm_g[l : l + 1], f"hgrn_fwd_{l}", rider)
        if rider is not None:
            full.append(dict(zip(BIG, got)))
        x1 = _mix_fwd(xs, a, b, z, fw["w_br_gm"], fw["w_br_hg"], fw["w_out"], f"mix_fwd_{l}")
        z2, h2 = _norm_matmul(x1, ffn_norm[l : l + 1], fw["w_up"], f"up_proj_{l}")
        x2 = _convffn_fwd(x1, z2, conv_full, conv_b[l : l + 1], fw["w_down"], l, f"convffn_fwd_{l}")
        saved.append((xs, h, z, a, b, o, states, x1, h2, z2, bst))
        xs = x2

    loss_tile, dx, d_final = _loss_head(xs, final_norm.reshape(1, d), loss_target[0])
    loss = lax.psum(loss_tile[0, 0], ("x", "y", "c"))

    big = {k: None for k in BIG}
    grads = [None] * len(BIG)
    per_layer = [None] * DEPTH
    pairs = None
    summed_layer = None

    def chip_sums(others, layer):
        return [
            _chip_sum(idx, p, ot, ax, g, layer, f"chip_sum_{k}_{layer}")
            for k, p, ot, ax, g in zip(BIG, pairs, others, axes, grads)
        ]

    for l in reversed(range(DEPTH)):
        x0, h, z, a, b, o, states, x1, h2, z2, bst = saved[l]
        fw = full[l]
        dz2, act, dconv = _convffn_bwd(dx, z2, conv_full, conv_b[l : l + 1], fw["w_down"], l, f"convffn_bwd_{l}")
        big["w_down"] = _matmul_tn(act, dx, big["w_down"], l, f"dw_down_{l}")
        big["w_up"] = _matmul_tn(h2, dz2, big["w_up"], l, f"dw_up_{l}")
        (dx1, d_ffn), _ = _proj_norm_bwd([dz2], fw["w_up"], x1, ffn_norm[l : l + 1], dx, f"up_proj_bwd_{l}")
        y, dpa, dpb, da, db, dgates = _mix_bwd(dx1, a, b, z, fw["w_br_gm"], fw["w_br_hg"], fw["w_out"], f"mix_bwd_{l}")
        big["w_out"] = _matmul_tn(y, dx1, big["w_out"], l, f"dw_out_{l}")
        big["w_br_gm"] = _matmul_tn(a, dpa, big["w_br_gm"], l, f"dw_br_gm_{l}")
        big["w_br_hg"] = _matmul_tn(b, dpb, big["w_br_hg"], l, f"dw_br_hg_{l}")
        rider = _chip_rider(pairs, axes) if pairs is not None else None
        (dz_hg, d_lb, d_ng), others = _hgrn_bwd(db, z, o, states, lb[l : l + 1], hg_norm_g[l : l + 1], f"hgrn_bwd_{l}", rider)
        if pairs is not None:
            grads = chip_sums(others, l + 1)
            summed_layer = l + 1
        dz_gm, d_lng, d_lnb, d_ws, d_bst = _gmlp_bwd(da, z, gm_ln_g[l : l + 1], gm_ln_b[l : l + 1], gm_ws[l], bst, f"gmlp_bwd_{l}")
        n_in = fw["w_in"].shape[1]
        big["w_in"] = _matmul_tn(h, dz_gm, big["w_in"], l, f"dw_in_gm_{l}", n_total=n_in, col_off=0)
        big["w_in"] = _matmul_tn(h, dz_hg, big["w_in"], l, f"dw_in_hg_{l}", n_total=n_in, col_off=2 * BR_DIM)
        big["w_in"] = _matmul_tn(h, dgates, big["w_in"], l, f"dw_in_gate_{l}", n_total=n_in, col_off=6 * BR_DIM)
        rider = _pair_rider([big[k] for k in BIG], axes, l)
        if summed_layer is not None:
            rider = _join(rider, _complete_rider(grads, axes, summed_layer))
        (dx, d_mix), got = _proj_norm_bwd([dz_gm, dz_hg, dgates], fw["w_in"], x0, mix_norm[l : l + 1], dx1, f"in_proj_bwd_{l}", rider)
        received = got[: len(BIG)]
        if summed_layer is not None:
            grads = got[len(BIG) :]
        pairs = [_pair_sum(idx, big[k], l, r, ax, f"pair_sum_{k}_{l}") for k, r, ax in zip(BIG, received, axes)]
        per_layer[l] = dict(
            mix_norm=d_mix[0], gm_ln_g=d_lng[0], gm_ln_b=d_lnb[0], gm_ws=d_ws, gm_bs=d_bst.T, lb=d_lb[0], hg_norm_g=d_ng[0],
            ffn_norm=d_ffn[0], conv_w=dconv[0:3], conv_b=dconv[3],
        )

    others = _run_rider(_chip_rider(pairs, axes), "grad_chip_exchange_0")
    grads = _run_rider(_complete_rider(chip_sums(others, 0), axes, 0), "grad_complete_0")
    grad = dict(zip(BIG, grads))

    small_grads = {k: jnp.stack([per_layer[l][k] for l in range(DEPTH)]) for k in per_layer[0]}
    small_grads["hg_lb_logits"] = _lower_bounds_bwd(hg_lb_logits, small_grads.pop("lb"))
    small_grads["final_norm"] = d_final[0]
    small_names = list(REPLICATED) + ["conv_w"]
    packed = _pack([small_grads[k] for k in small_names])
    summed = _sum_devices(_gather_small(packed))
    for k, g in zip(small_names, _unpack(summed, [small_grads[k].shape for k in small_names])):
        grad[k] = g
    grad["conv_w"] = lax.dynamic_slice_in_dim(grad["conv_w"], chip * conv_w.shape[2], conv_w.shape[2], axis=2)

    delta, new_m, new_v = {}, {}, {}
    for k in WEIGHTS:
        delta[k], new_m[k], new_v[k] = _adamw(w[k], grad[k], m[k], v[k], f"adamw_{k}")

    return (loss, dx[None], *[grad[k] for k in WEIGHTS], *[delta[k] for k in WEIGHTS], *[new_m[k] for k in WEIGHTS],
            *[new_v[k] for k in WEIGHTS])
```

```python
import math

import jax
import jax.numpy as jnp
from jax import lax
from jax.experimental import pallas as pl
from jax.experimental.pallas import tpu as pltpu

F32 = jnp.float32
BF16 = jnp.bfloat16
MM_DTYPE = BF16

N_HEADS = 4
HEAD_DIM = 128
BR_DIM = N_HEADS * HEAD_DIM
CHUNK = 64
GM_BLOCK = 128
DEPTH = 4
N_CHIPS = 4
EPS = 1e-6
TINY = 1e-30
LB_MAX = 0.999
ADAM_LR = 0.001
ADAM_B1 = 0.9
ADAM_B2 = 0.999
ADAM_EPS = 1e-08
ADAM_WD = 0.01
ADAM_STEP = 10
INV_SQRT2 = 1.0 / math.sqrt(2.0)
INV_SQRT_2PI = 1.0 / math.sqrt(2.0 * math.pi)

VMEM_LIMIT_BYTES = 56 * 1024 * 1024
TILE_BYTES = 2 * 1024 * 1024
ACC_BYTES = 6 * 1024 * 1024
MESH = pl.DeviceIdType.MESH
SDS = jax.ShapeDtypeStruct
ANY = pl.BlockSpec(memory_space=pl.ANY)


def _params(*sem):
    return pltpu.CompilerParams(dimension_semantics=sem or None, vmem_limit_bytes=VMEM_LIMIT_BYTES)


def _divisor(n, limit, mult):
    best = None
    for d in range(mult, min(n, limit) + 1, mult):
        if n % d == 0:
            best = d
    return best if best is not None else n


def _dot(a, b):
    return jnp.dot(a.astype(MM_DTYPE), b.astype(MM_DTYPE), preferred_element_type=F32)


def _dot_nt(a, b):
    return lax.dot_general(a.astype(MM_DTYPE), b.astype(MM_DTYPE), (((1,), (1,)), ((), ())), preferred_element_type=F32)


def _dot_tn(a, b):
    return lax.dot_general(a.astype(MM_DTYPE), b.astype(MM_DTYPE), (((0,), (0,)), ((), ())), preferred_element_type=F32)


def _sigmoid(x):
    return 1.0 / (1.0 + jnp.exp(-x))


def _gelu(x):
    return 0.5 * x * (1.0 + lax.erf(x * INV_SQRT2))


def _gelu_grad(x):
    return 0.5 * (1.0 + lax.erf(x * INV_SQRT2)) + x * jnp.exp(-0.5 * x * x) * INV_SQRT_2PI


def _silu_grad(x, s):
    return s * (1.0 + x * (1.0 - s))


def _resident(shape, index_map):
    return pl.BlockSpec(shape, index_map, pipeline_mode=pl.Buffered(1))


class _Rider:
    def __init__(self, inputs, out_shapes, scratch, start, finish, aliases=None):
        self.inputs = list(inputs)
        self.out_shapes = list(out_shapes)
        self.scratch = list(scratch)
        self.start = start
        self.finish = finish
        self.aliases = dict(aliases or {})


def _join(a, b):
    if a is None or b is None:
        return a if b is None else b
    na, oa, sa = len(a.inputs), len(a.out_shapes), len(a.scratch)

    def start(i, o, s):
        a.start(i[:na], o[:oa], s[:sa])
        b.start(i[na:], o[oa:], s[sa:])

    def finish(i, o, s):
        a.finish(i[:na], o[:oa], s[:sa])
        b.finish(i[na:], o[oa:], s[sa:])

    aliases = dict(a.aliases)
    aliases.update({na + k: oa + v for k, v in b.aliases.items()})
    return _Rider(a.inputs + b.inputs, a.out_shapes + b.out_shapes, a.scratch + b.scratch, start, finish, aliases)


def _hosted_call(body, rider, *, name, grid, in_specs, out_specs, out_shape, scratch_shapes, args, semantics):
    n_in, n_out, n_scr = len(in_specs), len(out_specs), len(scratch_shapes)
    if rider is None:
        outs = pl.pallas_call(
            body, name=name, grid=grid, in_specs=in_specs, out_specs=out_specs, out_shape=out_shape,
            scratch_shapes=scratch_shapes, compiler_params=_params(*semantics),
        )(*args)
        return list(outs), []
    ri, ro = len(rider.inputs), len(rider.out_shapes)

    def wrapped(*refs):
        p = 0
        hin = refs[p : p + n_in]
        p += n_in
        rin = refs[p : p + ri]
        p += ri
        hout = refs[p : p + n_out]
        p += n_out
        rout = refs[p : p + ro]
        p += ro
        hscr = refs[p : p + n_scr]
        rscr = refs[p + n_scr :]

        @pl.when(pl.program_id(0) == 0)
        def _():
            rider.start(rin, rout, rscr)

        body(*hin, *hout, *hscr)

        @pl.when(pl.program_id(0) == grid[0] - 1)
        def _():
            rider.finish(rin, rout, rscr)

    outs = pl.pallas_call(
        wrapped,
        name=name,
        grid=grid,
        in_specs=list(in_specs) + [ANY] * ri,
        out_specs=list(out_specs) + [ANY] * ro,
        out_shape=list(out_shape) + rider.out_shapes,
        scratch_shapes=list(scratch_shapes) + rider.scratch,
        input_output_aliases={n_in + k: n_out + v for k, v in rider.aliases.items()},
        compiler_params=_params(*semantics),
    )(*args, *rider.inputs)
    return list(outs[:n_out]), list(outs[n_out:])


def _run_rider(rider, name):
    ri, ro = len(rider.inputs), len(rider.out_shapes)

    def body(*refs):
        rin, rout, rscr = refs[:ri], refs[ri : ri + ro], refs[ri + ro :]
        rider.start(rin, rout, rscr)
        rider.finish(rin, rout, rscr)

    return list(
        pl.pallas_call(
            body,
            name=name,
            in_specs=[ANY] * ri,
            out_specs=[ANY] * ro,
            out_shape=rider.out_shapes,
            scratch_shapes=rider.scratch,
            input_output_aliases=rider.aliases,
        )(*rider.inputs)
    )


def _lower_bounds(logits):
    def body(lg_ref, lb_ref):
        lg = lg_ref[...]
        mx = jnp.max(lg, axis=0, keepdims=True)
        e = jnp.exp(lg - mx)
        p = e / jnp.sum(e, axis=0, keepdims=True)
        run = p[0:1]
        rows = [run - p[0:1]]
        for l in range(1, DEPTH):
            run = run + p[l : l + 1]
            rows.append(run - p[0:1])
        raw = jnp.concatenate(rows, axis=0)
        lb_ref[...] = jnp.minimum(jnp.maximum(raw, 0.0), LB_MAX)

    return pl.pallas_call(body, name="lower_bounds", out_shape=SDS(logits.shape, F32))(logits)


def _lower_bounds_bwd(logits, dlb):
    def body(lg_ref, dlb_ref, dlg_ref):
        lg = lg_ref[...]
        mx = jnp.max(lg, axis=0, keepdims=True)
        e = jnp.exp(lg - mx)
        p = e / jnp.sum(e, axis=0, keepdims=True)
        run = p[0:1]
        rows = [run - p[0:1]]
        for l in range(1, DEPTH):
            run = run + p[l : l + 1]
            rows.append(run - p[0:1])
        raw = jnp.concatenate(rows, axis=0)
        lo = jnp.where(raw > 0.0, 1.0, jnp.where(raw == 0.0, 0.5, 0.0))
        clipped = jnp.maximum(raw, 0.0)
        hi = jnp.where(clipped < LB_MAX, 1.0, jnp.where(clipped == LB_MAX, 0.5, 0.0))
        draw = dlb_ref[...] * lo * hi
        total = jnp.sum(draw, axis=0, keepdims=True)
        dps = []
        tail = total
        for j in range(DEPTH):
            dps.append(tail - total if j == 0 else tail)
            tail = tail - draw[j : j + 1]
        dp = jnp.concatenate(dps, axis=0)
        dlg_ref[...] = p * (dp - jnp.sum(p * dp, axis=0, keepdims=True))

    return pl.pallas_call(body, name="lower_bounds_bwd", out_shape=SDS(logits.shape, F32))(logits, dlb)


def _norm_matmul(x, gain, w, name, rider=None):
    s, d = x.shape
    n = w.shape[1]
    tm = _divisor(s, 512, 16)

    def body(x_ref, g_ref, w_ref, z_ref, h_ref):
        xv = x_ref[...]
        r = lax.rsqrt(jnp.mean(xv * xv, axis=-1, keepdims=True) + EPS)
        h = ((xv * r) * g_ref[...]).astype(MM_DTYPE)
        h_ref[...] = h
        z_ref[...] = jnp.dot(h, w_ref[...], preferred_element_type=F32)

    return _hosted_call(
        body,
        rider,
        name=name,
        grid=(s // tm,),
        in_specs=[
            pl.BlockSpec((tm, d), lambda i: (i, 0)),
            _resident((1, d), lambda i: (0, 0)),
            _resident((d, n), lambda i: (0, 0)),
        ],
        out_specs=[pl.BlockSpec((tm, n), lambda i: (i, 0)), pl.BlockSpec((tm, d), lambda i: (i, 0))],
        out_shape=[SDS((s, n), F32), SDS((s, d), MM_DTYPE)],
        scratch_shapes=[],
        args=[x, gain, w],
        semantics=("arbitrary",),
    )


def _proj_norm_bwd(dz_parts, w, x, gain, dres, name, rider=None):
    s, d = x.shape
    n = w.shape[1]
    tm = _divisor(s, 512, 16)
    widths = [p.shape[1] for p in dz_parts]
    offsets = [sum(widths[:k]) for k in range(len(widths))]
    n_parts = len(dz_parts)

    def body(*refs):
        dz_refs = refs[:n_parts]
        w_ref, x_ref, g_ref, dres_ref, dx_ref, dg_ref = refs[n_parts:]

        @pl.when(pl.program_id(0) == 0)
        def _():
            dg_ref[...] = jnp.zeros_like(dg_ref)

        dh = None
        for dz_ref, off, wd in zip(dz_refs, offsets, widths):
            part = _dot_nt(dz_ref[...], w_ref[:, off : off + wd])
            dh = part if dh is None else dh + part
        xv = x_ref[...]
        r = lax.rsqrt(jnp.mean(xv * xv, axis=-1, keepdims=True) + EPS)
        xh = xv * r
        dg_ref[...] += jnp.sum(dh * xh, axis=0, keepdims=True)
        dxh = dh * g_ref[...]
        dx_ref[...] = dres_ref[...] + r * (dxh - xh * jnp.mean(dxh * xh, axis=-1, keepdims=True))

    in_specs = [pl.BlockSpec((tm, wd), lambda i: (i, 0)) for wd in widths] + [
        _resident((d, n), lambda i: (0, 0)),
        pl.BlockSpec((tm, d), lambda i: (i, 0)),
        _resident((1, d), lambda i: (0, 0)),
        pl.BlockSpec((tm, d), lambda i: (i, 0)),
    ]
    return _hosted_call(
        body,
        rider,
        name=name,
        grid=(s // tm,),
        in_specs=in_specs,
        out_specs=[pl.BlockSpec((tm, d), lambda i: (i, 0)), pl.BlockSpec((1, d), lambda i: (0, 0))],
        out_shape=[SDS((s, d), F32), SDS((1, d), F32)],
        scratch_shapes=[],
        args=[*dz_parts, w, x, gain, dres],
        semantics=("arbitrary",),
    )


def _matmul_tn(a, b, stack, layer, name, n_total=None, col_off=0):
    s, k = a.shape
    n = b.shape[1]
    n_total = n if n_total is None else n_total
    tn = _divisor(math.gcd(n, col_off) if col_off else n, max(128, ACC_BYTES // (4 * k)), 128)
    ts = _divisor(s, min(2048, max(512, ACC_BYTES // (2 * k))), 16)
    off = col_off // tn

    def body(a_ref, b_ref, *rest):
        out_ref = rest[-1]

        @pl.when(pl.program_id(1) == 0)
        def _():
            out_ref[...] = jnp.zeros_like(out_ref)

        out_ref[...] += _dot_tn(a_ref[...], b_ref[...])

    in_specs = [pl.BlockSpec((ts, k), lambda j, i: (i, 0)), pl.BlockSpec((ts, tn), lambda j, i: (i, j))]
    args = [a, b]
    aliases = {}
    if stack is not None:
        in_specs.append(ANY)
        args.append(stack)
        aliases = {2: 0}
    return pl.pallas_call(
        body,
        name=name,
        grid=(n // tn, s // ts),
        in_specs=in_specs,
        out_specs=pl.BlockSpec((None, k, tn), lambda j, i: (layer, 0, off + j)),
        out_shape=SDS((DEPTH, k, n_total), F32),
        input_output_aliases=aliases,
        compiler_params=_params("parallel", "arbitrary"),
    )(*args)


def _gm_mask():
    r = lax.broadcasted_iota(jnp.int32, (GM_BLOCK, GM_BLOCK), 0) // CHUNK
    c = lax.broadcasted_iota(jnp.int32, (GM_BLOCK, GM_BLOCK), 1) // CHUNK
    return r >= c


def _gm_normed(v, lng, lnb):
    vg = _gelu(v)
    mu = jnp.mean(vg, axis=-1, keepdims=True)
    xc = vg - mu
    rstd = lax.rsqrt(jnp.mean(xc * xc, axis=-1, keepdims=True) + EPS)
    xh = xc * rstd
    return xh, rstd, xh * lng + lnb


def _gmlp_fwd(z, lng, lnb, ws, bst, name):
    s = z.shape[0]
    tg = _divisor(s, 256, GM_BLOCK)

    def body(z_ref, lng_ref, lnb_ref, ws_ref, bst_ref, a_ref):
        mask = _gm_mask()
        ug = _gelu(z_ref[:, :BR_DIM])
        _, _, vn = _gm_normed(z_ref[:, BR_DIM:], lng_ref[...], lnb_ref[...])
        vn = vn.astype(MM_DTYPE)
        for h in range(N_HEADS):
            cs = slice(h * HEAD_DIM, (h + 1) * HEAD_DIM)
            wm = jnp.where(mask, ws_ref[h], 0.0).astype(MM_DTYPE)
            for blk in range(tg // GM_BLOCK):
                rs = slice(blk * GM_BLOCK, (blk + 1) * GM_BLOCK)
                mixed = _dot(wm, vn[rs, cs]) + bst_ref[:, h : h + 1]
                a_ref[rs, cs] = (ug[rs, cs] * mixed).astype(MM_DTYPE)

    return pl.pallas_call(
        body,
        name=name,
        grid=(s // tg,),
        in_specs=[
            pl.BlockSpec((tg, 2 * BR_DIM), lambda i: (i, 0)),
            _resident((1, BR_DIM), lambda i: (0, 0)),
            _resident((1, BR_DIM), lambda i: (0, 0)),
            _resident((N_HEADS, GM_BLOCK, GM_BLOCK), lambda i: (0, 0, 0)),
            _resident((GM_BLOCK, N_HEADS), lambda i: (0, 0)),
        ],
        out_specs=pl.BlockSpec((tg, BR_DIM), lambda i: (i, 0)),
        out_shape=SDS((s, BR_DIM), MM_DTYPE),
        compiler_params=_params("parallel"),
    )(z, lng, lnb, ws, bst)


def _gmlp_bwd(da, z, lng, lnb, ws, bst, name):
    s = z.shape[0]
    tg = _divisor(s, 256, GM_BLOCK)

    def body(da_ref, z_ref, lng_ref, lnb_ref, ws_ref, bst_ref, dz_ref, dlng_ref, dlnb_ref, dws_ref, dbst_ref):
        @pl.when(pl.program_id(0) == 0)
        def _():
            dlng_ref[...] = jnp.zeros_like(dlng_ref)
            dlnb_ref[...] = jnp.zeros_like(dlnb_ref)
            dws_ref[...] = jnp.zeros_like(dws_ref)
            dbst_ref[...] = jnp.zeros_like(dbst_ref)

        mask = _gm_mask()
        u = z_ref[:, :BR_DIM]
        v = z_ref[:, BR_DIM:]
        ug = _gelu(u)
        lng_v = lng_ref[...]
        xh, rstd, vn = _gm_normed(v, lng_v, lnb_ref[...])
        vnb = vn.astype(MM_DTYPE)
        da_v = da_ref[...]
        dmix = da_v * ug
        dmixb = dmix.astype(MM_DTYPE)
        dvn_cols = []
        for h in range(N_HEADS):
            cs = slice(h * HEAD_DIM, (h + 1) * HEAD_DIM)
            wm = jnp.where(mask, ws_ref[h], 0.0).astype(MM_DTYPE)
            dw = jnp.zeros((GM_BLOCK, GM_BLOCK), F32)
            dbias = jnp.zeros((GM_BLOCK, 1), F32)
            dvn_rows = []
            for blk in range(tg // GM_BLOCK):
                rs = slice(blk * GM_BLOCK, (blk + 1) * GM_BLOCK)
                mixed = _dot(wm, vnb[rs, cs]) + bst_ref[:, h : h + 1]
                dz_ref[rs, cs] = (da_v[rs, cs] * mixed * _gelu_grad(u[rs, cs])).astype(MM_DTYPE)
                dw = dw + _dot_nt(dmixb[rs, cs], vnb[rs, cs])
                dbias = dbias + jnp.sum(dmix[rs, cs], axis=1, keepdims=True)
                dvn_rows.append(_dot_tn(wm, dmixb[rs, cs]))
            dws_ref[h] += jnp.where(mask, dw, 0.0)
            dbst_ref[:, h : h + 1] += dbias
            dvn_cols.append(jnp.concatenate(dvn_rows, axis=0) if len(dvn_rows) > 1 else dvn_rows[0])
        dvn = jnp.concatenate(dvn_cols, axis=1)
        dlng_ref[...] += jnp.sum(dvn * xh, axis=0, keepdims=True)
        dlnb_ref[...] += jnp.sum(dvn, axis=0, keepdims=True)
        dxh = dvn * lng_v
        dvg = rstd * (dxh - jnp.mean(dxh, axis=-1, keepdims=True) - xh * jnp.mean(dxh * xh, axis=-1, keepdims=True))
        dz_ref[:, BR_DIM:] = (dvg * _gelu_grad(v)).astype(MM_DTYPE)

    return pl.pallas_call(
        body,
        name=name,
        grid=(s // tg,),
        in_specs=[
            pl.BlockSpec((tg, BR_DIM), lambda i: (i, 0)),
            pl.BlockSpec((tg, 2 * BR_DIM), lambda i: (i, 0)),
            _resident((1, BR_DIM), lambda i: (0, 0)),
            _resident((1, BR_DIM), lambda i: (0, 0)),
            _resident((N_HEADS, GM_BLOCK, GM_BLOCK), lambda i: (0, 0, 0)),
            _resident((GM_BLOCK, N_HEADS), lambda i: (0, 0)),
        ],
        out_specs=[
            pl.BlockSpec((tg, 2 * BR_DIM), lambda i: (i, 0)),
            pl.BlockSpec((1, BR_DIM), lambda i: (0, 0)),
            pl.BlockSpec((1, BR_DIM), lambda i: (0, 0)),
            pl.BlockSpec((N_HEADS, GM_BLOCK, GM_BLOCK), lambda i: (0, 0, 0)),
            pl.BlockSpec((GM_BLOCK, N_HEADS), lambda i: (0, 0)),
        ],
        out_shape=[
            SDS((s, 2 * BR_DIM), MM_DTYPE),
            SDS((1, BR_DIM), F32),
            SDS((1, BR_DIM), F32),
            SDS((N_HEADS, GM_BLOCK, GM_BLOCK), F32),
            SDS((GM_BLOCK, N_HEADS), F32),
        ],
        compiler_params=_params("arbitrary"),
    )(da, z, lng, lnb, ws, bst)


HG_ROWS = 256
MID = CHUNK // 2 - 1


def _tri(lower):
    r = lax.broadcasted_iota(jnp.int32, (CHUNK, CHUNK), 0)
    c = lax.broadcasted_iota(jnp.int32, (CHUNK, CHUNK), 1)
    return r >= c if lower else c >= r


def _scan_rows(x, reverse=False):
    n = x.shape[0]
    rid = lax.broadcasted_iota(jnp.int32, x.shape, 0)
    k = 1
    while k < n:
        if reverse:
            x = x + jnp.where(rid < n - k, pltpu.roll(x, n - k, 0), 0.0)
        else:
            x = x + jnp.where(rid >= k, pltpu.roll(x, k, 0), 0.0)
        k *= 2
    return x


def _hgrn_fwd(z, lb, ng, name, rider=None):
    s = z.shape[0]
    rows_per = _divisor(s, HG_ROWS, CHUNK)
    n_sub = rows_per // CHUNK

    def body(zqf_ref, zio_ref, lb_ref, ng_ref, o_ref, b_ref, st_ref, state):
        @pl.when(pl.program_id(0) == 0)
        def _():
            state[...] = jnp.zeros_like(state)

        low = _tri(True)
        ng_v = ng_ref[...]

        def chunk(ci, carry):
            rows = pl.ds(pl.multiple_of(ci * CHUNK, CHUNK), CHUNK)
            for h in range(N_HEADS):
                cs = slice(h * HEAD_DIM, (h + 1) * HEAD_DIM)
                cs2 = slice(BR_DIM + h * HEAD_DIM, BR_DIM + (h + 1) * HEAD_DIM)
                zq = zqf_ref[rows, cs]
                zf = zqf_ref[rows, cs2]
                vi = zio_ref[rows, cs]
                zog = zio_ref[rows, cs2]
                lbh = lb_ref[:, cs]
                qf = zq * _sigmoid(zq)
                forget = lbh + (1.0 - lbh) * _sigmoid(zf)
                g = jnp.log(jnp.maximum(forget, TINY))
                k = (1.0 - lbh) * _sigmoid(-zf)
                gc = _scan_rows(g)
                gm = gc[MID : MID + 1]
                gl = gc[CHUNK - 1 : CHUNK]
                a = jnp.where(low, _dot_nt(qf * jnp.exp(gc - gm), k * jnp.exp(gm - gc)), 0.0)
                st = state[h]
                st_ref[ci, h] = st
                o = _dot(a, vi) + _dot_nt(qf * jnp.exp(gc), st)
                state[h] = st * jnp.exp(gl) + _dot_tn(vi, k * jnp.exp(gl - gc))
                r = lax.rsqrt(jnp.mean(o * o, axis=-1, keepdims=True) + EPS)
                o_ref[rows, cs] = o
                b_ref[rows, cs] = (((o * r) * ng_v) * (zog * _sigmoid(zog))).astype(MM_DTYPE)
            return carry

        lax.fori_loop(0, n_sub, chunk, 0)

    return _hosted_call(
        body,
        rider,
        name=name,
        grid=(s // rows_per,),
        in_specs=[
            pl.BlockSpec((rows_per, 2 * BR_DIM), lambda i: (i, 1)),
            pl.BlockSpec((rows_per, 2 * BR_DIM), lambda i: (i, 2)),
            _resident((1, BR_DIM), lambda i: (0, 0)),
            _resident((1, HEAD_DIM), lambda i: (0, 0)),
        ],
        out_specs=[
            pl.BlockSpec((rows_per, BR_DIM), lambda i: (i, 0)),
            pl.BlockSpec((rows_per, BR_DIM), lambda i: (i, 0)),
            pl.BlockSpec((n_sub, N_HEADS, HEAD_DIM, HEAD_DIM), lambda i: (i, 0, 0, 0)),
        ],
        out_shape=[
            SDS((s, BR_DIM), F32),
            SDS((s, BR_DIM), MM_DTYPE),
            SDS((s // CHUNK, N_HEADS, HEAD_DIM, HEAD_DIM), F32),
        ],
        scratch_shapes=[pltpu.VMEM((N_HEADS, HEAD_DIM, HEAD_DIM), F32)],
        args=[z, z, lb, ng],
        semantics=("arbitrary",),
    )


def _hgrn_bwd(db, z, o, states, lb, ng, name, rider=None):
    s = z.shape[0]
    rows_per = _divisor(s, HG_ROWS, CHUNK)
    n_sub = rows_per // CHUNK
    n_steps = s // rows_per

    def body(db_ref, zqf_ref, zio_ref, o_ref, st_ref, lb_ref, ng_ref, dz_ref, dlb_ref, dng_ref, dstate):
        @pl.when(pl.program_id(0) == 0)
        def _():
            dstate[...] = jnp.zeros_like(dstate)
            dlb_ref[...] = jnp.zeros_like(dlb_ref)
            dng_ref[...] = jnp.zeros_like(dng_ref)

        low = _tri(True)
        ng_v = ng_ref[...]

        def chunk(cc, carry):
            ci = n_sub - 1 - cc
            rows = pl.ds(pl.multiple_of(ci * CHUNK, CHUNK), CHUNK)
            for h in range(N_HEADS):
                cs = slice(h * HEAD_DIM, (h + 1) * HEAD_DIM)
                cs2 = slice(BR_DIM + h * HEAD_DIM, BR_DIM + (h + 1) * HEAD_DIM)
                zq = zqf_ref[rows, cs]
                zf = zqf_ref[rows, cs2]
                vi = zio_ref[rows, cs]
                zog = zio_ref[rows, cs2]
                lbh = lb_ref[:, cs]
                ov = o_ref[rows, cs]
                dbv = db_ref[rows, cs]
                r = lax.rsqrt(jnp.mean(ov * ov, axis=-1, keepdims=True) + EPS)
                on = ov * r
                sgo = _sigmoid(zog)
                gate = zog * sgo
                dng_ref[...] += jnp.sum(dbv * gate * on, axis=0, keepdims=True)
                don = dbv * gate * ng_v
                dzog = dbv * (on * ng_v) * _silu_grad(zog, sgo)
                do = r * (don - on * jnp.mean(don * on, axis=-1, keepdims=True))
                sq = _sigmoid(zq)
                qf = zq * sq
                sg = _sigmoid(zf)
                sgn = _sigmoid(-zf)
                oml = 1.0 - lbh
                forget = lbh + oml * sg
                fm = jnp.maximum(forget, TINY)
                k = oml * sgn
                gc = _scan_rows(jnp.log(fm))
                gm = gc[MID : MID + 1]
                gl = gc[CHUNK - 1 : CHUNK]
                e_g = jnp.exp(gc)
                e_q = jnp.exp(gc - gm)
                e_k = jnp.exp(gm - gc)
                e_kd = jnp.exp(gl - gc)
                e_gl = jnp.exp(gl)
                qt = qf * e_q
                kt = k * e_k
                a = jnp.where(low, _dot_nt(qt, kt), 0.0)
                st = st_ref[ci, h]
                dst = dstate[h]
                dp = jnp.where(low, _dot_nt(do, vi), 0.0)
                dv = _dot_tn(a, do) + _dot_nt(k * e_kd, dst)
                dq = _dot(dp, kt) * e_q + e_g * _dot(do, st)
                dks = e_kd * _dot(vi, dst)
                dk = _dot_tn(dp, qt) * e_k + dks
                dstate[h] = _dot_tn(do, qf * e_g) + dst * e_gl
                dgc = qf * dq - k * dk
                extra = e_gl * jnp.sum(st * dst, axis=0, keepdims=True) + jnp.sum(k * dks, axis=0, keepdims=True)
                dg = _scan_rows(dgc, reverse=True) + extra
                dforget = jnp.where(forget > TINY, dg / fm, 0.0)
                both = dforget - dk
                dlb_ref[:, cs] += jnp.sum(sgn * both, axis=0, keepdims=True)
                dz_ref[rows, cs] = (dq * _silu_grad(zq, sq)).astype(MM_DTYPE)
                dz_ref[rows, cs2] = (oml * sg * sgn * both).astype(MM_DTYPE)
                dz_ref[rows, 2 * BR_DIM + h * HEAD_DIM : 2 * BR_DIM + (h + 1) * HEAD_DIM] = dv.astype(MM_DTYPE)
                dz_ref[rows, 3 * BR_DIM + h * HEAD_DIM : 3 * BR_DIM + (h + 1) * HEAD_DIM] = dzog.astype(MM_DTYPE)
            return carry

        lax.fori_loop(0, n_sub, chunk, 0)

    rev = lambda i: n_steps - 1 - i
    return _hosted_call(
        body,
        rider,
        name=name,
        grid=(n_steps,),
        in_specs=[
            pl.BlockSpec((rows_per, BR_DIM), lambda i: (rev(i), 0)),
            pl.BlockSpec((rows_per, 2 * BR_DIM), lambda i: (rev(i), 1)),
            pl.BlockSpec((rows_per, 2 * BR_DIM), lambda i: (rev(i), 2)),
            pl.BlockSpec((rows_per, BR_DIM), lambda i: (rev(i), 0)),
            pl.BlockSpec((n_sub, N_HEADS, HEAD_DIM, HEAD_DIM), lambda i: (rev(i), 0, 0, 0)),
            _resident((1, BR_DIM), lambda i: (0, 0)),
            _resident((1, HEAD_DIM), lambda i: (0, 0)),
        ],
        out_specs=[
            pl.BlockSpec((rows_per, 4 * BR_DIM), lambda i: (rev(i), 0)),
            pl.BlockSpec((1, BR_DIM), lambda i: (0, 0)),
            pl.BlockSpec((1, HEAD_DIM), lambda i: (0, 0)),
        ],
        out_shape=[SDS((s, 4 * BR_DIM), MM_DTYPE), SDS((1, BR_DIM), F32), SDS((1, HEAD_DIM), F32)],
        scratch_shapes=[pltpu.VMEM((N_HEADS, HEAD_DIM, HEAD_DIM), F32)],
        args=[db, z, z, o, states, lb, ng],
        semantics=("arbitrary",),
    )


def _mix_fwd(x, a, b, z, w_gm, w_hg, w_out, name):
    s, d = x.shape
    tm = _divisor(s, 256, 16)
    g0 = 6 * BR_DIM // d

    def body(x_ref, a_ref, b_ref, ga_ref, gb_ref, wgm_ref, whg_ref, wout_ref, out_ref):
        y = _sigmoid(ga_ref[...]) * _dot(a_ref[...], wgm_ref[...]) + _sigmoid(gb_ref[...]) * _dot(b_ref[...], whg_ref[...])
        out_ref[...] = x_ref[...] + _dot(y, wout_ref[...])

    return pl.pallas_call(
        body,
        name=name,
        grid=(s // tm,),
        in_specs=[
            pl.BlockSpec((tm, d), lambda i: (i, 0)),
            pl.BlockSpec((tm, BR_DIM), lambda i: (i, 0)),
            pl.BlockSpec((tm, BR_DIM), lambda i: (i, 0)),
            pl.BlockSpec((tm, d), lambda i: (i, g0)),
            pl.BlockSpec((tm, d), lambda i: (i, g0 + 1)),
            _resident((BR_DIM, d), lambda i: (0, 0)),
            _resident((BR_DIM, d), lambda i: (0, 0)),
            _resident((d, d), lambda i: (0, 0)),
        ],
        out_specs=pl.BlockSpec((tm, d), lambda i: (i, 0)),
        out_shape=SDS((s, d), F32),
        compiler_params=_params("parallel"),
    )(x, a, b, z, z, w_gm, w_hg, w_out)


def _mix_bwd(dx, a, b, z, w_gm, w_hg, w_out, name):
    s, d = dx.shape
    tm = _divisor(s, 256, 16)
    g0 = 6 * BR_DIM // d

    def body(dx_ref, a_ref, b_ref, ga_ref, gb_ref, wgm_ref, whg_ref, wout_ref, y_ref, dpa_ref, dpb_ref, da_ref, db_ref, dg_ref):
        dy = _dot_nt(dx_ref[...], wout_ref[...])
        pa = _dot(a_ref[...], wgm_ref[...])
        pb = _dot(b_ref[...], whg_ref[...])
        sa = _sigmoid(ga_ref[...])
        sb = _sigmoid(gb_ref[...])
        y_ref[...] = (sa * pa + sb * pb).astype(MM_DTYPE)
        dpa = (dy * sa).astype(MM_DTYPE)
        dpb = (dy * sb).astype(MM_DTYPE)
        dpa_ref[...] = dpa
        dpb_ref[...] = dpb
        da_ref[...] = _dot_nt(dpa, wgm_ref[...])
        db_ref[...] = _dot_nt(dpb, whg_ref[...])
        dg_ref[:, :d] = (dy * pa * sa * (1.0 - sa)).astype(MM_DTYPE)
        dg_ref[:, d:] = (dy * pb * sb * (1.0 - sb)).astype(MM_DTYPE)

    row = lambda w: pl.BlockSpec((tm, w), lambda i: (i, 0))
    return pl.pallas_call(
        body,
        name=name,
        grid=(s // tm,),
        in_specs=[
            row(d),
            row(BR_DIM),
            row(BR_DIM),
            pl.BlockSpec((tm, d), lambda i: (i, g0)),
            pl.BlockSpec((tm, d), lambda i: (i, g0 + 1)),
            _resident((BR_DIM, d), lambda i: (0, 0)),
            _resident((BR_DIM, d), lambda i: (0, 0)),
            _resident((d, d), lambda i: (0, 0)),
        ],
        out_specs=[row(d), row(d), row(d), row(BR_DIM), row(BR_DIM), row(2 * d)],
        out_shape=[
            SDS((s, d), MM_DTYPE),
            SDS((s, d), MM_DTYPE),
            SDS((s, d), MM_DTYPE),
            SDS((s, BR_DIM), F32),
            SDS((s, BR_DIM), F32),
            SDS((s, 2 * d), MM_DTYPE),
        ],
        compiler_params=_params("parallel"),
    )(dx, a, b, z, z, w_gm, w_hg, w_out)


HALO = 8


def _shift_down(v, prev, n):
    rolled = pltpu.roll(v, n, 0)
    rid = lax.broadcasted_iota(jnp.int32, v.shape, 0)
    out = rolled
    for r in range(n):
        out = jnp.where(rid == r, prev[HALO - n + r : HALO - n + r + 1], out)
    return out


def _shift_up(v, nxt, n):
    tm = v.shape[0]
    rolled = pltpu.roll(v, tm - n, 0)
    rid = lax.broadcasted_iota(jnp.int32, v.shape, 0)
    out = rolled
    for r in range(n):
        out = jnp.where(rid == tm - n + r, nxt[r : r + 1], out)
    return out


def _conv_cols(f):
    return _divisor(f, 1408, 128)


def _convffn_fwd(x, z2, cw, cb, w_down, layer, name):
    s, d = x.shape
    f = z2.shape[1] // 2
    tm = _divisor(s, 256, 16)
    cwid = _conv_cols(f)
    per8 = tm // HALO

    def body(x_ref, z_ref, prev_ref, cw_ref, cb_ref, wd_ref, out_ref):
        first = pl.program_id(0) == 0
        acc = x_ref[...]
        for c0 in range(0, f, cwid):
            halves = []
            for base in (c0, f + c0):
                cols = slice(base, base + cwid)
                zc = z_ref[:, cols]
                prev = jnp.where(first, 0.0, prev_ref[:, cols])
                conv = cb_ref[:, cols] + cw_ref[0:1, cols] * _shift_down(zc, prev, 2)
                conv = conv + cw_ref[1:2, cols] * _shift_down(zc, prev, 1) + cw_ref[2:3, cols] * zc
                halves.append(conv)
            gate, val = halves
            act = gate * _sigmoid(gate) * val
            acc = acc + _dot(act, wd_ref[c0 : c0 + cwid, :])
        out_ref[...] = acc

    return pl.pallas_call(
        body,
        name=name,
        grid=(s // tm,),
        in_specs=[
            pl.BlockSpec((tm, d), lambda i: (i, 0)),
            pl.BlockSpec((tm, 2 * f), lambda i: (i, 0)),
            pl.BlockSpec((HALO, 2 * f), lambda i: (jnp.maximum(i * per8 - 1, 0), 0)),
            _resident((None, 3, 2 * f), lambda i: (layer, 0, 0)),
            _resident((1, 2 * f), lambda i: (0, 0)),
            _resident((f, d), lambda i: (0, 0)),
        ],
        out_specs=pl.BlockSpec((tm, d), lambda i: (i, 0)),
        out_shape=SDS((s, d), F32),
        compiler_params=_params("arbitrary"),
    )(x, z2, z2, cw, cb, w_down)


def _convffn_bwd(dx, z2, cw, cb, w_down, layer, name):
    s, d = dx.shape
    f = z2.shape[1] // 2
    tm = _divisor(s, 256, 16)
    cwid = _conv_cols(f)
    per8 = tm // HALO
    n_steps = s // tm

    def body(dx_ref, z_ref, prev_ref, cw_ref, cb_ref, wd_ref, dz_ref, act_ref, dcw_ref, nxt):
        step = pl.program_id(0)

        @pl.when(step == 0)
        def _():
            nxt[...] = jnp.zeros_like(nxt)
            dcw_ref[...] = jnp.zeros_like(dcw_ref)

        first_tile = step == n_steps - 1
        dxv = dx_ref[...].astype(MM_DTYPE)
        for c0 in range(0, f, cwid):
            dact = _dot_nt(dxv, wd_ref[c0 : c0 + cwid, :])
            zs, convs = [], []
            for base in (c0, f + c0):
                cols = slice(base, base + cwid)
                zc = z_ref[:, cols]
                prev = jnp.where(first_tile, 0.0, prev_ref[:, cols])
                zm1 = _shift_down(zc, prev, 1)
                zm2 = _shift_down(zc, prev, 2)
                convs.append(cb_ref[:, cols] + cw_ref[0:1, cols] * zm2 + cw_ref[1:2, cols] * zm1 + cw_ref[2:3, cols] * zc)
                zs.append((zc, zm1, zm2))
            gate, val = convs
            sg = _sigmoid(gate)
            silu = gate * sg
            act_ref[:, c0 : c0 + cwid] = (silu * val).astype(MM_DTYPE)
            dconvs = (dact * val * _silu_grad(gate, sg), dact * silu)
            for base, dconv, (zc, zm1, zm2) in zip((c0, f + c0), dconvs, zs):
                cols = slice(base, base + cwid)
                dcw_ref[0:1, cols] += jnp.sum(dconv * zm2, axis=0, keepdims=True)
                dcw_ref[1:2, cols] += jnp.sum(dconv * zm1, axis=0, keepdims=True)
                dcw_ref[2:3, cols] += jnp.sum(dconv * zc, axis=0, keepdims=True)
                dcw_ref[3:4, cols] += jnp.sum(dconv, axis=0, keepdims=True)
                nx = nxt[:, cols]
                dz = cw_ref[2:3, cols] * dconv + cw_ref[1:2, cols] * _shift_up(dconv, nx, 1)
                dz = dz + cw_ref[0:1, cols] * _shift_up(dconv, nx, 2)
                dz_ref[:, cols] = dz.astype(MM_DTYPE)
                nxt[:, cols] = dconv[0:HALO]

    rev = lambda i: n_steps - 1 - i
    return pl.pallas_call(
        body,
        name=name,
        grid=(n_steps,),
        in_specs=[
            pl.BlockSpec((tm, d), lambda i: (rev(i), 0)),
            pl.BlockSpec((tm, 2 * f), lambda i: (rev(i), 0)),
            pl.BlockSpec((HALO, 2 * f), lambda i: (jnp.maximum(rev(i) * per8 - 1, 0), 0)),
            _resident((None, 3, 2 * f), lambda i: (layer, 0, 0)),
            _resident((1, 2 * f), lambda i: (0, 0)),
            _resident((f, d), lambda i: (0, 0)),
        ],
        out_specs=[
            pl.BlockSpec((tm, 2 * f), lambda i: (rev(i), 0)),
            pl.BlockSpec((tm, f), lambda i: (rev(i), 0)),
            pl.BlockSpec((HALO, 2 * f), lambda i: (0, 0)),
        ],
        out_shape=[SDS((s, 2 * f), MM_DTYPE), SDS((s, f), MM_DTYPE), SDS((HALO, 2 * f), F32)],
        scratch_shapes=[pltpu.VMEM((HALO, 2 * f), F32)],
        compiler_params=_params("arbitrary"),
    )(dx, z2, z2, cw, cb, w_down)


def _loss_head(x, gain, target):
    s, d = x.shape
    tm = _divisor(s, 256, 8)

    def body(x_ref, g_ref, t_ref, loss_ref, dx_ref, dg_ref):
        @pl.when(pl.program_id(0) == 0)
        def _():
            loss_ref[...] = jnp.zeros_like(loss_ref)
            dg_ref[...] = jnp.zeros_like(dg_ref)

        xv = x_ref[...]
        gv = g_ref[...]
        r = lax.rsqrt(jnp.mean(xv * xv, axis=-1, keepdims=True) + EPS)
        xh = xv * r
        err = xh * gv - t_ref[...]
        loss_ref[...] += 0.5 * jnp.sum(jnp.mean(err * err, axis=-1, keepdims=True))
        dout = err * (1.0 / d)
        dg_ref[...] += jnp.sum(dout * xh, axis=0, keepdims=True)
        dxh = dout * gv
        dx_ref[...] = r * (dxh - xh * jnp.mean(dxh * xh, axis=-1, keepdims=True))

    return pl.pallas_call(
        body,
        name="loss_head",
        grid=(s // tm,),
        in_specs=[
            pl.BlockSpec((tm, d), lambda i: (i, 0)),
            _resident((1, d), lambda i: (0, 0)),
            pl.BlockSpec((tm, d), lambda i: (i, 0)),
        ],
        out_specs=[
            pl.BlockSpec((8, 128), lambda i: (0, 0)),
            pl.BlockSpec((tm, d), lambda i: (i, 0)),
            pl.BlockSpec((1, d), lambda i: (0, 0)),
        ],
        out_shape=[SDS((8, 128), F32), SDS((s, d), F32), SDS((1, d), F32)],
        compiler_params=_params("arbitrary"),
    )(x, gain, target)


def _adamw(w, g, m, v, name):
    shape = w.shape
    cols = shape[-1]
    rows = max(1, math.prod(shape[:-1]))
    tr = _divisor(rows, max(8, TILE_BYTES // (4 * cols)), 8)
    flat = [t.reshape(rows, cols) for t in (w, g, m, v)]

    def body(w_ref, g_ref, m_ref, v_ref, d_ref, nm_ref, nv_ref):
        gv = g_ref[...]
        m2 = ADAM_B1 * m_ref[...] + (1.0 - ADAM_B1) * gv
        v2 = ADAM_B2 * v_ref[...] + (1.0 - ADAM_B2) * (gv * gv)
        m_hat = m2 / (1.0 - ADAM_B1**ADAM_STEP)
        v_hat = v2 / (1.0 - ADAM_B2**ADAM_STEP)
        d_ref[...] = -ADAM_LR * (m_hat / (jnp.sqrt(v_hat) + ADAM_EPS) + ADAM_WD * w_ref[...])
        nm_ref[...] = m2
        nv_ref[...] = v2

    spec = pl.BlockSpec((tr, cols), lambda i: (i, 0))
    outs = pl.pallas_call(
        body,
        name=name,
        grid=(rows // tr,),
        in_specs=[spec] * 4,
        out_specs=[spec] * 3,
        out_shape=[SDS((rows, cols), F32)] * 3,
        compiler_params=_params("parallel"),
    )(*flat)
    return tuple(t.reshape(shape) for t in outs)


def _position():
    return lax.axis_index("x"), lax.axis_index("y"), lax.axis_index("c")


def _other_chips(x, y):
    return [(1 - x, y), (x, 1 - y), (1 - x, 1 - y)]


def _remote(src, dst, send_sem, recv_sem, device):
    return pltpu.make_async_remote_copy(
        src_ref=src, dst_ref=dst, send_sem=send_sem, recv_sem=recv_sem, device_id=device, device_id_type=MESH
    )


def _sub(ref, lead, shape, axis, chip=None, half=None):
    cut = [pl.ds(0, shape[0]), pl.ds(0, shape[1])]
    if chip is not None:
        size = shape[axis] // N_CHIPS
        cut[axis] = pl.ds(chip * size, size)
    if half is not None:
        size = shape[1 - axis] // 2
        cut[1 - axis] = pl.ds(half * size, size)
    return ref.at[(*lead, *cut)]


def _scaled(shape, axis, num, den=1):
    return tuple(d * num // den if i == axis else d for i, d in enumerate(shape))


def _gather_rider(shards, axes, layer, conv_w=None):
    n = len(shards)
    shard2d = [t.shape[1:] for t in shards]
    full2d = [_scaled(s2, ax, N_CHIPS) for s2, ax in zip(shard2d, axes)]
    out_shapes = [SDS(f2, t.dtype) for f2, t in zip(full2d, shards)]
    inputs = list(shards)
    own = 6
    scratch = [pltpu.SemaphoreType.DMA((n, 7)), pltpu.SemaphoreType.DMA((n, 7))]
    if conv_w is not None:
        cshape = conv_w.shape
        inputs.append(conv_w)
        out_shapes.append(SDS(_scaled(cshape, 2, N_CHIPS), conv_w.dtype))
        scratch += [pltpu.SemaphoreType.DMA((4,)), pltpu.SemaphoreType.DMA((4,))]

    def conv_slab(ref, chip):
        return ref.at[:, :, pl.ds((2 * chip[0] + chip[1]) * cshape[2], cshape[2])]

    def own_copy(ins, outs, send_sems, recv_sems, k):
        x, y, c = _position()
        slab = _sub(outs[k], (), full2d[k], axes[k], chip=2 * x + y)
        return _remote(ins[k].at[layer], slab, send_sems.at[k, own], recv_sems.at[k, own], (x, y, 1 - c))

    def start(ins, outs, scr):
        send_sems, recv_sems = scr[:2]
        x, y, c = _position()
        me = 2 * x + y
        for k in range(n):
            own_copy(ins, outs, send_sems, recv_sems, k).start()
            for j, chip in enumerate(_other_chips(x, y)):
                _remote(
                    _sub(ins[k], (layer,), shard2d[k], axes[k], half=c),
                    _sub(outs[k], (), full2d[k], axes[k], chip=me, half=c),
                    send_sems.at[k, j], recv_sems.at[k, j], (*chip, c),
                ).start()
        if conv_w is not None:
            csend, crecv = scr[2:]
            _remote(ins[n], conv_slab(outs[n], (x, y)), csend.at[3], crecv.at[3], (x, y, 1 - c)).start()
            for j, chip in enumerate(_other_chips(x, y)):
                _remote(ins[n], conv_slab(outs[n], (x, y)), csend.at[j], crecv.at[j], (*chip, c)).start()

    def finish(ins, outs, scr):
        send_sems, recv_sems = scr[:2]
        x, y, c = _position()
        me = 2 * x + y
        sibling = (x, y, 1 - c)
        chips = _other_chips(x, y)
        forwards = []
        for k in range(n):
            src = _sub(ins[k], (layer,), shard2d[k], axes[k], half=c)
            for j, chip in enumerate(chips):
                landed = _sub(outs[k], (), full2d[k], axes[k], chip=2 * chip[0] + chip[1], half=c)
                _remote(src, landed, send_sems.at[k, j], recv_sems.at[k, j], (*chip, c)).wait_recv()
                fwd = _remote(landed, landed, send_sems.at[k, 3 + j], recv_sems.at[k, 3 + j], sibling)
                fwd.start()
                forwards.append(fwd)
        for k in range(n):
            src = _sub(ins[k], (layer,), shard2d[k], axes[k], half=c)
            for j, chip in enumerate(chips):
                theirs = _sub(outs[k], (), full2d[k], axes[k], chip=2 * chip[0] + chip[1], half=1 - c)
                _remote(src, theirs, send_sems.at[k, 3 + j], recv_sems.at[k, 3 + j], sibling).wait_recv()
        for fwd in forwards:
            fwd.wait_send()
        for k in range(n):
            src = _sub(ins[k], (layer,), shard2d[k], axes[k], half=c)
            mine = _sub(outs[k], (), full2d[k], axes[k], chip=me, half=c)
            for j, chip in enumerate(chips):
                _remote(src, mine, send_sems.at[k, j], recv_sems.at[k, j], (*chip, c)).wait_send()
            own_copy(ins, outs, send_sems, recv_sems, k).wait()
        if conv_w is not None:
            csend, crecv = scr[2:]
            for j, chip in enumerate(chips):
                cp = _remote(ins[n], conv_slab(outs[n], chip), csend.at[j], crecv.at[j], (*chip, c))
                cp.wait_recv()
                cp.wait_send()
            _remote(ins[n], conv_slab(outs[n], (x, y)), csend.at[3], crecv.at[3], sibling).wait()

    return _Rider(inputs, out_shapes, scratch, start, finish)


def _pair_rider(stacks, axes, layer):
    n = len(stacks)
    shapes = [t.shape[1:] for t in stacks]
    out_shapes = [SDS(_scaled(s2, 1 - ax, 1, 2), F32) for s2, ax in zip(shapes, axes)]
    scratch = [pltpu.SemaphoreType.DMA((n,)), pltpu.SemaphoreType.DMA((n,))]

    def copies(ins, outs, scr):
        x, y, c = _position()
        return [
            _remote(_sub(ins[k], (layer,), shapes[k], axes[k], half=1 - c), outs[k], scr[0].at[k], scr[1].at[k], (x, y, 1 - c))
            for k in range(n)
        ]

    def start(ins, outs, scr):
        for cp in copies(ins, outs, scr):
            cp.start()

    def finish(ins, outs, scr):
        for cp in copies(ins, outs, scr):
            cp.wait()

    return _Rider(stacks, out_shapes, scratch, start, finish)


def _chip_rider(pairs, axes):
    n = len(pairs)
    shapes = [t.shape for t in pairs]
    out_shapes = [SDS((3,) + _scaled(s2, ax, 1, N_CHIPS), t.dtype) for s2, ax, t in zip(shapes, axes, pairs)]
    scratch = [pltpu.SemaphoreType.DMA((n, 3)), pltpu.SemaphoreType.DMA((n, 3))]

    def copies(ins, outs, scr):
        x, y, c = _position()
        return [
            _remote(
                _sub(ins[k], (), shapes[k], axes[k], chip=2 * chip[0] + chip[1]), outs[k].at[j],
                scr[0].at[k, j], scr[1].at[k, j], (*chip, c),
            )
            for k in range(n)
            for j, chip in enumerate(_other_chips(x, y))
        ]

    def start(ins, outs, scr):
        for cp in copies(ins, outs, scr):
            cp.start()

    def finish(ins, outs, scr):
        for cp in copies(ins, outs, scr):
            cp.wait()

    return _Rider(pairs, out_shapes, scratch, start, finish)


def _complete_rider(stacks, axes, layer):
    n = len(stacks)
    shapes = [t.shape[1:] for t in stacks]
    scratch = [pltpu.SemaphoreType.DMA((n,)), pltpu.SemaphoreType.DMA((n,))]

    def start(ins, outs, scr):
        x, y, c = _position()
        for k in range(n):
            mine = _sub(outs[k], (layer,), shapes[k], axes[k], half=c)
            _remote(mine, mine, scr[0].at[k], scr[1].at[k], (x, y, 1 - c)).start()

    def finish(ins, outs, scr):
        x, y, c = _position()
        for k in range(n):
            mine = _sub(outs[k], (layer,), shapes[k], axes[k], half=c)
            theirs = _sub(outs[k], (layer,), shapes[k], axes[k], half=1 - c)
            _remote(mine, theirs, scr[0].at[k], scr[1].at[k], (x, y, 1 - c)).wait_recv()
            _remote(mine, mine, scr[0].at[k], scr[1].at[k], (x, y, 1 - c)).wait_send()

    return _Rider(stacks, [SDS(t.shape, t.dtype) for t in stacks], scratch, start, finish, {k: k for k in range(n)})


def _gather_small(buf):
    rows, cols = buf.shape
    flips = [(fx, fy, fc) for fx in (0, 1) for fy in (0, 1) for fc in (0, 1)][1:]

    def body(in_ref, out_ref, send_sems, recv_sems, local_sem):
        x, y, c = _position()

        def peer(f):
            return (x + f[0] - 2 * x * f[0], y + f[1] - 2 * y * f[1], c + f[2] - 2 * c * f[2])

        me = 4 * x + 2 * y + c
        local = pltpu.make_async_copy(in_ref, out_ref.at[me], local_sem)
        local.start()
        copies = []
        for j, f in enumerate(flips):
            cp = _remote(in_ref, out_ref.at[me], send_sems.at[j], recv_sems.at[j], peer(f))
            cp.start()
            copies.append(cp)
        for j, f in enumerate(flips):
            px, py, pc = peer(f)
            _remote(in_ref, out_ref.at[4 * px + 2 * py + pc], send_sems.at[j], recv_sems.at[j], peer(f)).wait_recv()
        for cp in copies:
            cp.wait_send()
        local.wait()

    return pl.pallas_call(
        body,
        name="gather_small_grads",
        in_specs=[ANY],
        out_specs=ANY,
        out_shape=SDS((8, rows, cols), buf.dtype),
        scratch_shapes=[pltpu.SemaphoreType.DMA((7,)), pltpu.SemaphoreType.DMA((7,)), pltpu.SemaphoreType.DMA(())],
    )(buf)


def _pair_sum(idx, stack, layer, recv, axis, name):
    hk, hn = recv.shape
    tk = _divisor(hk, max(16, TILE_BYTES // (4 * hn)), 16)
    per = hk // tk

    def body(idx_ref, g_ref, r_ref, out_ref):
        out_ref[...] = (g_ref[...] + r_ref[...]).astype(out_ref.dtype)

    if axis == 1:
        mine = pl.BlockSpec((None, tk, hn), lambda i, idx_ref: (layer, idx_ref[0] * per + i, 0))
    else:
        mine = pl.BlockSpec((None, tk, hn), lambda i, idx_ref: (layer, i, idx_ref[0]))
    return pl.pallas_call(
        body,
        name=name,
        grid_spec=pltpu.PrefetchScalarGridSpec(
            num_scalar_prefetch=1,
            grid=(per,),
            in_specs=[mine, pl.BlockSpec((tk, hn), lambda i, idx_ref: (i, 0))],
            out_specs=pl.BlockSpec((tk, hn), lambda i, idx_ref: (i, 0)),
        ),
        out_shape=SDS((hk, hn), BF16),
        compiler_params=_params("parallel"),
    )(idx, stack, recv)


def _chip_sum(idx, pair, others, axis, stack, layer, name):
    _, sk, sn = others.shape
    tk = _divisor(sk, max(16, TILE_BYTES // (4 * sn)), 16)
    per = sk // tk

    def body(idx_ref, p_ref, o0_ref, o1_ref, o2_ref, *rest):
        out_ref = rest[-1]
        acc = p_ref[...].astype(F32) + o0_ref[...].astype(F32)
        out_ref[...] = (acc + o1_ref[...].astype(F32)) + o2_ref[...].astype(F32)

    if axis == 1:
        own = pl.BlockSpec((tk, sn), lambda i, idx_ref: (i, idx_ref[1]))
        out = pl.BlockSpec((None, tk, sn), lambda i, idx_ref: (layer, idx_ref[0] * per + i, 0))
        shard = (2 * sk, sn)
    else:
        own = pl.BlockSpec((tk, sn), lambda i, idx_ref: (idx_ref[1] * per + i, 0))
        out = pl.BlockSpec((None, tk, sn), lambda i, idx_ref: (layer, i, idx_ref[0]))
        shard = (sk, 2 * sn)
    other = lambda j: pl.BlockSpec((None, tk, sn), lambda i, idx_ref: (j, i, 0))
    in_specs = [own, other(0), other(1), other(2)]
    args = [idx, pair, others, others, others]
    aliases = {}
    if stack is not None:
        in_specs.append(ANY)
        args.append(stack)
        aliases = {5: 0}
    return pl.pallas_call(
        body,
        name=name,
        grid_spec=pltpu.PrefetchScalarGridSpec(num_scalar_prefetch=1, grid=(per,), in_specs=in_specs, out_specs=out),
        out_shape=SDS((DEPTH,) + shard, F32),
        input_output_aliases=aliases,
        compiler_params=_params("parallel"),
    )(*args)


def _sum_devices(gathered):
    _, rows, cols = gathered.shape

    def body(g_ref, out_ref):
        acc = g_ref[0]
        for dev in range(1, 8):
            acc = acc + g_ref[dev]
        out_ref[...] = acc

    return pl.pallas_call(body, name="sum_small_grads", out_shape=SDS((rows, cols), F32))(gathered)


PACK_TILE = 8 * 128


def _pack(arrays):
    parts = []
    for t in arrays:
        flat = t.reshape(-1)
        pad = (-flat.shape[0]) % PACK_TILE
        if pad:
            flat = jnp.concatenate([flat, jnp.zeros((pad,), flat.dtype)])
        parts.append(flat.reshape(-1, 128))
    return jnp.concatenate(parts, axis=0)


def _unpack(buf, shapes):
    out, row = [], 0
    for shp in shapes:
        size = math.prod(shp)
        rows = -(-size // PACK_TILE) * 8
        out.append(buf[row : row + rows].reshape(-1)[:size].reshape(shp))
        row += rows
    return out


BIG = ("w_in", "w_br_gm", "w_br_hg", "w_out", "w_up", "w_down")
BIG_AXIS = (1, 1, 1, 0, 1, 0)
REPLICATED = ("mix_norm", "gm_ln_g", "gm_ln_b", "gm_ws", "gm_bs", "hg_lb_logits", "hg_norm_g", "ffn_norm", "conv_b", "final_norm")
WEIGHTS = ("mix_norm", "w_in", "gm_ln_g", "gm_ln_b", "gm_ws", "gm_bs", "hg_lb_logits", "hg_norm_g", "w_br_gm", "w_br_hg", "w_out",
           "ffn_norm", "w_up", "conv_w", "conv_b", "w_down", "final_norm")


def kernel(x, mix_norm, w_in, gm_ln_g, gm_ln_b, gm_ws, gm_bs, hg_lb_logits, hg_norm_g, w_br_gm, w_br_hg, w_out, ffn_norm, w_up, conv_w, conv_b, w_down, final_norm, loss_target, m_mix_norm, m_w_in, m_gm_ln_g, m_gm_ln_b, m_gm_ws, m_gm_bs, m_hg_lb_logits, m_hg_norm_g, m_w_br_gm, m_w_br_hg, m_w_out, m_ffn_norm, m_w_up, m_conv_w, m_conv_b, m_w_down, m_final_norm, v_mix_norm, v_w_in, v_gm_ln_g, v_gm_ln_b, v_gm_ws, v_gm_bs, v_hg_lb_logits, v_hg_norm_g, v_w_br_gm, v_w_br_hg, v_w_out, v_ffn_norm, v_w_up, v_conv_w, v_conv_b, v_w_down, v_final_norm):
    w = dict(mix_norm=mix_norm, w_in=w_in, gm_ln_g=gm_ln_g, gm_ln_b=gm_ln_b, gm_ws=gm_ws, gm_bs=gm_bs, hg_lb_logits=hg_lb_logits,
             hg_norm_g=hg_norm_g, w_br_gm=w_br_gm, w_br_hg=w_br_hg, w_out=w_out, ffn_norm=ffn_norm, w_up=w_up, conv_w=conv_w,
             conv_b=conv_b, w_down=w_down, final_norm=final_norm)
    m = dict(mix_norm=m_mix_norm, w_in=m_w_in, gm_ln_g=m_gm_ln_g, gm_ln_b=m_gm_ln_b, gm_ws=m_gm_ws, gm_bs=m_gm_bs,
             hg_lb_logits=m_hg_lb_logits, hg_norm_g=m_hg_norm_g, w_br_gm=m_w_br_gm, w_br_hg=m_w_br_hg, w_out=m_w_out,
             ffn_norm=m_ffn_norm, w_up=m_w_up, conv_w=m_conv_w, conv_b=m_conv_b, w_down=m_w_down, final_norm=m_final_norm)
    v = dict(mix_norm=v_mix_norm, w_in=v_w_in, gm_ln_g=v_gm_ln_g, gm_ln_b=v_gm_ln_b, gm_ws=v_gm_ws, gm_bs=v_gm_bs,
             hg_lb_logits=v_hg_lb_logits, hg_norm_g=v_hg_norm_g, w_br_gm=v_w_br_gm, w_br_hg=v_w_br_hg, w_out=v_w_out,
             ffn_norm=v_ffn_norm, w_up=v_w_up, conv_w=v_conv_w, conv_b=v_conv_b, w_down=v_w_down, final_norm=v_final_norm)

    px, py, pc = _position()
    chip = 2 * px + py
    idx = jnp.stack([pc, chip]).astype(jnp.int32)
    axes = list(BIG_AXIS)
    d = x.shape[2]

    shards = [w[k].astype(MM_DTYPE) for k in BIG]
    w_in_0, conv_full = _run_rider(_gather_rider(shards[:1], axes[:1], 0, conv_w=conv_w), "gather_weights_0")
    full = [dict(w_in=w_in_0)]
    lb = _lower_bounds(hg_lb_logits)
    xs = x[0]
    saved = []
    mixer = 4
    for l in range(DEPTH):
        fw = full[l]
        more = l + 1 < DEPTH
        rider = _gather_rider(shards[:mixer], axes[:mixer], l + 1) if more else None
        (z, h), got_mixer = _norm_matmul(xs, mix_norm[l : l + 1], fw["w_in"], f"in_proj_{l}", rider)
        bst = gm_bs[l].T
        a = _gmlp_fwd(z, gm_ln_g[l : l + 1], gm_ln_b[l : l + 1], gm_ws[l], bst, f"gmlp_fwd_{l}")
        rider = _gather_rider(shards[1:], axes[1:], 0) if l == 0 else None
        (o, b, states), got = _hgrn_fwd(z, lb[l : l + 1], hg_norm_g[l : l + 1], f"hgrn_fwd_{l}", rider)
        if l == 0:
            fw.update(zip(BIG[1:], got))
        x1 = _mix_fwd(xs, a, b, z, fw["w_br_gm"], fw["w_br_hg"], fw["w_out"], f"mix_fwd_{l}")
        rider = _gather_rider(shards[mixer:], axes[mixer:], l + 1) if more else None
        (z2, h2), got_ffn = _norm_matmul(x1, ffn_norm[l : l + 1], fw["w_up"], f"up_proj_{l}", rider)
        if more:
            full.append(dict(zip(BIG, got_mixer + got_ffn)))
        x2 = _convffn_fwd(x1, z2, conv_full, conv_b[l : l + 1], fw["w_down"], l, f"convffn_fwd_{l}")
        saved.append((xs, h, z, a, b, o, states, x1, h2, z2, bst))
        xs = x2

    loss_tile, dx, d_final = _loss_head(xs, final_norm.reshape(1, d), loss_target[0])
    loss = lax.psum(loss_tile[0, 0], ("x", "y", "c"))

    big = {k: None for k in BIG}
    grads = [None] * len(BIG)
    per_layer = [None] * DEPTH
    pairs = None
    summed_layer = None

    def chip_sums(others, layer):
        return [
            _chip_sum(idx, p, ot, ax, g, layer, f"chip_sum_{k}_{layer}")
            for k, p, ot, ax, g in zip(BIG, pairs, others, axes, grads)
        ]

    for l in reversed(range(DEPTH)):
        x0, h, z, a, b, o, states, x1, h2, z2, bst = saved[l]
        fw = full[l]
        dz2, act, dconv = _convffn_bwd(dx, z2, conv_full, conv_b[l : l + 1], fw["w_down"], l, f"convffn_bwd_{l}")
        big["w_down"] = _matmul_tn(act, dx, big["w_down"], l, f"dw_down_{l}")
        big["w_up"] = _matmul_tn(h2, dz2, big["w_up"], l, f"dw_up_{l}")
        (dx1, d_ffn), _ = _proj_norm_bwd([dz2], fw["w_up"], x1, ffn_norm[l : l + 1], dx, f"up_proj_bwd_{l}")
        y, dpa, dpb, da, db, dgates = _mix_bwd(dx1, a, b, z, fw["w_br_gm"], fw["w_br_hg"], fw["w_out"], f"mix_bwd_{l}")
        big["w_out"] = _matmul_tn(y, dx1, big["w_out"], l, f"dw_out_{l}")
        big["w_br_gm"] = _matmul_tn(a, dpa, big["w_br_gm"], l, f"dw_br_gm_{l}")
        big["w_br_hg"] = _matmul_tn(b, dpb, big["w_br_hg"], l, f"dw_br_hg_{l}")
        rider = _chip_rider(pairs, axes) if pairs is not None else None
        (dz_hg, d_lb, d_ng), others = _hgrn_bwd(db, z, o, states, lb[l : l + 1], hg_norm_g[l : l + 1], f"hgrn_bwd_{l}", rider)
        if pairs is not None:
            grads = chip_sums(others, l + 1)
            summed_layer = l + 1
        dz_gm, d_lng, d_lnb, d_ws, d_bst = _gmlp_bwd(da, z, gm_ln_g[l : l + 1], gm_ln_b[l : l + 1], gm_ws[l], bst, f"gmlp_bwd_{l}")
        n_in = fw["w_in"].shape[1]
        big["w_in"] = _matmul_tn(h, dz_gm, big["w_in"], l, f"dw_in_gm_{l}", n_total=n_in, col_off=0)
        big["w_in"] = _matmul_tn(h, dz_hg, big["w_in"], l, f"dw_in_hg_{l}", n_total=n_in, col_off=2 * BR_DIM)
        big["w_in"] = _matmul_tn(h, dgates, big["w_in"], l, f"dw_in_gate_{l}", n_total=n_in, col_off=6 * BR_DIM)
        rider = _pair_rider([big[k] for k in BIG], axes, l)
        if summed_layer is not None:
            rider = _join(rider, _complete_rider(grads, axes, summed_layer))
        (dx, d_mix), got = _proj_norm_bwd([dz_gm, dz_hg, dgates], fw["w_in"], x0, mix_norm[l : l + 1], dx1, f"in_proj_bwd_{l}", rider)
        received = got[: len(BIG)]
        if summed_layer is not None:
            grads = got[len(BIG) :]
        pairs = [_pair_sum(idx, big[k], l, r, ax, f"pair_sum_{k}_{l}") for k, r, ax in zip(BIG, received, axes)]
        per_layer[l] = dict(
            mix_norm=d_mix[0], gm_ln_g=d_lng[0], gm_ln_b=d_lnb[0], gm_ws=d_ws, gm_bs=d_bst.T, lb=d_lb[0], hg_norm_g=d_ng[0],
            ffn_norm=d_ffn[0], conv_w=dconv[0:3], conv_b=dconv[3],
        )

    others = _run_rider(_chip_rider(pairs, axes), "grad_chip_exchange_0")
    grads = _run_rider(_complete_rider(chip_sums(others, 0), axes, 0), "grad_complete_0")
    grad = dict(zip(BIG, grads))

    small_grads = {k: jnp.stack([per_layer[l][k] for l in range(DEPTH)]) for k in per_layer[0]}
    small_grads["hg_lb_logits"] = _lower_bounds_bwd(hg_lb_logits, small_grads.pop("lb"))
    small_grads["final_norm"] = d_final[0]
    small_names = list(REPLICATED) + ["conv_w"]
    packed = _pack([small_grads[k] for k in small_names])
    summed = _sum_devices(_gather_small(packed))
    for k, g in zip(small_names, _unpack(summed, [small_grads[k].shape for k in small_names])):
        grad[k] = g
    grad["conv_w"] = lax.dynamic_slice_in_dim(grad["conv_w"], chip * conv_w.shape[2], conv_w.shape[2], axis=2)

    delta, new_m, new_v = {}, {}, {}
    for k in WEIGHTS:
        delta[k], new_m[k], new_v[k] = _adamw(w[k], grad[k], m[k], v[k], f"adamw_{k}")

    return (loss, dx[None], *[grad[k] for k in WEIGHTS], *[delta[k] for k in WEIGHTS], *[new_m[k] for k in WEIGHTS],
            *[new_v[k] for k in WEIGHTS])
```

```python
import math

import jax
import jax.numpy as jnp
from jax import lax
from jax.experimental import pallas as pl
from jax.experimental.pallas import tpu as pltpu

F32 = jnp.float32
BF16 = jnp.bfloat16
MM_DTYPE = BF16

N_HEADS = 4
HEAD_DIM = 128
BR_DIM = N_HEADS * HEAD_DIM
CHUNK = 64
GM_BLOCK = 128
DEPTH = 4
N_CHIPS = 4
EPS = 1e-6
TINY = 1e-30
LB_MAX = 0.999
ADAM_LR = 0.001
ADAM_B1 = 0.9
ADAM_B2 = 0.999
ADAM_EPS = 1e-08
ADAM_WD = 0.01
ADAM_STEP = 10
INV_SQRT2 = 1.0 / math.sqrt(2.0)
INV_SQRT_2PI = 1.0 / math.sqrt(2.0 * math.pi)

VMEM_LIMIT_BYTES = 56 * 1024 * 1024
TILE_BYTES = 2 * 1024 * 1024
ACC_BYTES = 6 * 1024 * 1024
MESH = pl.DeviceIdType.MESH
SDS = jax.ShapeDtypeStruct
ANY = pl.BlockSpec(memory_space=pl.ANY)


def _params(*sem):
    return pltpu.CompilerParams(dimension_semantics=sem or None, vmem_limit_bytes=VMEM_LIMIT_BYTES)


def _divisor(n, limit, mult):
    best = None
    for d in range(mult, min(n, limit) + 1, mult):
        if n % d == 0:
            best = d
    return best if best is not None else n


def _dot(a, b):
    return jnp.dot(a.astype(MM_DTYPE), b.astype(MM_DTYPE), preferred_element_type=F32)


def _dot_nt(a, b):
    return lax.dot_general(a.astype(MM_DTYPE), b.astype(MM_DTYPE), (((1,), (1,)), ((), ())), preferred_element_type=F32)


def _dot_tn(a, b):
    return lax.dot_general(a.astype(MM_DTYPE), b.astype(MM_DTYPE), (((0,), (0,)), ((), ())), preferred_element_type=F32)


def _sigmoid(x):
    return 1.0 / (1.0 + jnp.exp(-x))


def _gelu(x):
    return 0.5 * x * (1.0 + lax.erf(x * INV_SQRT2))


def _gelu_grad(x):
    return 0.5 * (1.0 + lax.erf(x * INV_SQRT2)) + x * jnp.exp(-0.5 * x * x) * INV_SQRT_2PI


def _silu_grad(x, s):
    return s * (1.0 + x * (1.0 - s))


def _resident(shape, index_map):
    return pl.BlockSpec(shape, index_map, pipeline_mode=pl.Buffered(1))


class _Rider:
    def __init__(self, inputs, out_shapes, scratch, start, finish, aliases=None):
        self.inputs = list(inputs)
        self.out_shapes = list(out_shapes)
        self.scratch = list(scratch)
        self.start = start
        self.finish = finish
        self.aliases = dict(aliases or {})


def _join(a, b):
    if a is None or b is None:
        return a if b is None else b
    na, oa, sa = len(a.inputs), len(a.out_shapes), len(a.scratch)

    def start(i, o, s):
        a.start(i[:na], o[:oa], s[:sa])
        b.start(i[na:], o[oa:], s[sa:])

    def finish(i, o, s):
        a.finish(i[:na], o[:oa], s[:sa])
        b.finish(i[na:], o[oa:], s[sa:])

    aliases = dict(a.aliases)
    aliases.update({na + k: oa + v for k, v in b.aliases.items()})
    return _Rider(a.inputs + b.inputs, a.out_shapes + b.out_shapes, a.scratch + b.scratch, start, finish, aliases)


def _hosted_call(body, rider, *, name, grid, in_specs, out_specs, out_shape, scratch_shapes, args, semantics):
    n_in, n_out, n_scr = len(in_specs), len(out_specs), len(scratch_shapes)
    if rider is None:
        outs = pl.pallas_call(
            body, name=name, grid=grid, in_specs=in_specs, out_specs=out_specs, out_shape=out_shape,
            scratch_shapes=scratch_shapes, compiler_params=_params(*semantics),
        )(*args)
        return list(outs), []
    ri, ro = len(rider.inputs), len(rider.out_shapes)

    def wrapped(*refs):
        p = 0
        hin = refs[p : p + n_in]
        p += n_in
        rin = refs[p : p + ri]
        p += ri
        hout = refs[p : p + n_out]
        p += n_out
        rout = refs[p : p + ro]
        p += ro
        hscr = refs[p : p + n_scr]
        rscr = refs[p + n_scr :]

        @pl.when(pl.program_id(0) == 0)
        def _():
            rider.start(rin, rout, rscr)

        body(*hin, *hout, *hscr)

        @pl.when(pl.program_id(0) == grid[0] - 1)
        def _():
            rider.finish(rin, rout, rscr)

    outs = pl.pallas_call(
        wrapped,
        name=name,
        grid=grid,
        in_specs=list(in_specs) + [ANY] * ri,
        out_specs=list(out_specs) + [ANY] * ro,
        out_shape=list(out_shape) + rider.out_shapes,
        scratch_shapes=list(scratch_shapes) + rider.scratch,
        input_output_aliases={n_in + k: n_out + v for k, v in rider.aliases.items()},
        compiler_params=_params(*semantics),
    )(*args, *rider.inputs)
    return list(outs[:n_out]), list(outs[n_out:])


def _run_rider(rider, name):
    ri, ro = len(rider.inputs), len(rider.out_shapes)

    def body(*refs):
        rin, rout, rscr = refs[:ri], refs[ri : ri + ro], refs[ri + ro :]
        rider.start(rin, rout, rscr)
        rider.finish(rin, rout, rscr)

    return list(
        pl.pallas_call(
            body,
            name=name,
            in_specs=[ANY] * ri,
            out_specs=[ANY] * ro,
            out_shape=rider.out_shapes,
            scratch_shapes=rider.scratch,
            input_output_aliases=rider.aliases,
        )(*rider.inputs)
    )


def _lower_bounds(logits):
    def body(lg_ref, lb_ref):
        lg = lg_ref[...]
        mx = jnp.max(lg, axis=0, keepdims=True)
        e = jnp.exp(lg - mx)
        p = e / jnp.sum(e, axis=0, keepdims=True)
        run = p[0:1]
        rows = [run - p[0:1]]
        for l in range(1, DEPTH):
            run = run + p[l : l + 1]
            rows.append(run - p[0:1])
        raw = jnp.concatenate(rows, axis=0)
        lb_ref[...] = jnp.minimum(jnp.maximum(raw, 0.0), LB_MAX)

    return pl.pallas_call(body, name="lower_bounds", out_shape=SDS(logits.shape, F32))(logits)


def _lower_bounds_bwd(logits, dlb):
    def body(lg_ref, dlb_ref, dlg_ref):
        lg = lg_ref[...]
        mx = jnp.max(lg, axis=0, keepdims=True)
        e = jnp.exp(lg - mx)
        p = e / jnp.sum(e, axis=0, keepdims=True)
        run = p[0:1]
        rows = [run - p[0:1]]
        for l in range(1, DEPTH):
            run = run + p[l : l + 1]
            rows.append(run - p[0:1])
        raw = jnp.concatenate(rows, axis=0)
        lo = jnp.where(raw > 0.0, 1.0, jnp.where(raw == 0.0, 0.5, 0.0))
        clipped = jnp.maximum(raw, 0.0)
        hi = jnp.where(clipped < LB_MAX, 1.0, jnp.where(clipped == LB_MAX, 0.5, 0.0))
        draw = dlb_ref[...] * lo * hi
        total = jnp.sum(draw, axis=0, keepdims=True)
        dps = []
        tail = total
        for j in range(DEPTH):
            dps.append(tail - total if j == 0 else tail)
            tail = tail - draw[j : j + 1]
        dp = jnp.concatenate(dps, axis=0)
        dlg_ref[...] = p * (dp - jnp.sum(p * dp, axis=0, keepdims=True))

    return pl.pallas_call(body, name="lower_bounds_bwd", out_shape=SDS(logits.shape, F32))(logits, dlb)


def _norm_matmul(x, gain, w, name, rider=None):
    s, d = x.shape
    n = w.shape[1]
    tm = _divisor(s, 512, 16)

    def body(x_ref, g_ref, w_ref, z_ref, h_ref):
        xv = x_ref[...]
        r = lax.rsqrt(jnp.mean(xv * xv, axis=-1, keepdims=True) + EPS)
        h = ((xv * r) * g_ref[...]).astype(MM_DTYPE)
        h_ref[...] = h
        z_ref[...] = jnp.dot(h, w_ref[...], preferred_element_type=F32)

    return _hosted_call(
        body,
        rider,
        name=name,
        grid=(s // tm,),
        in_specs=[
            pl.BlockSpec((tm, d), lambda i: (i, 0)),
            _resident((1, d), lambda i: (0, 0)),
            _resident((d, n), lambda i: (0, 0)),
        ],
        out_specs=[pl.BlockSpec((tm, n), lambda i: (i, 0)), pl.BlockSpec((tm, d), lambda i: (i, 0))],
        out_shape=[SDS((s, n), F32), SDS((s, d), MM_DTYPE)],
        scratch_shapes=[],
        args=[x, gain, w],
        semantics=("arbitrary",),
    )


def _proj_norm_bwd(dz_parts, w, x, gain, dres, name, rider=None):
    s, d = x.shape
    n = w.shape[1]
    tm = _divisor(s, 512, 16)
    widths = [p.shape[1] for p in dz_parts]
    offsets = [sum(widths[:k]) for k in range(len(widths))]
    n_parts = len(dz_parts)

    def body(*refs):
        dz_refs = refs[:n_parts]
        w_ref, x_ref, g_ref, dres_ref, dx_ref, dg_ref = refs[n_parts:]

        @pl.when(pl.program_id(0) == 0)
        def _():
            dg_ref[...] = jnp.zeros_like(dg_ref)

        dh = None
        for dz_ref, off, wd in zip(dz_refs, offsets, widths):
            part = _dot_nt(dz_ref[...], w_ref[:, off : off + wd])
            dh = part if dh is None else dh + part
        xv = x_ref[...]
        r = lax.rsqrt(jnp.mean(xv * xv, axis=-1, keepdims=True) + EPS)
        xh = xv * r
        dg_ref[...] += jnp.sum(dh * xh, axis=0, keepdims=True)
        dxh = dh * g_ref[...]
        dx_ref[...] = dres_ref[...] + r * (dxh - xh * jnp.mean(dxh * xh, axis=-1, keepdims=True))

    in_specs = [pl.BlockSpec((tm, wd), lambda i: (i, 0)) for wd in widths] + [
        _resident((d, n), lambda i: (0, 0)),
        pl.BlockSpec((tm, d), lambda i: (i, 0)),
        _resident((1, d), lambda i: (0, 0)),
        pl.BlockSpec((tm, d), lambda i: (i, 0)),
    ]
    return _hosted_call(
        body,
        rider,
        name=name,
        grid=(s // tm,),
        in_specs=in_specs,
        out_specs=[pl.BlockSpec((tm, d), lambda i: (i, 0)), pl.BlockSpec((1, d), lambda i: (0, 0))],
        out_shape=[SDS((s, d), F32), SDS((1, d), F32)],
        scratch_shapes=[],
        args=[*dz_parts, w, x, gain, dres],
        semantics=("arbitrary",),
    )


def _matmul_tn(a, b, stack, layer, name, n_total=None, col_off=0):
    s, k = a.shape
    n = b.shape[1]
    n_total = n if n_total is None else n_total
    tn = _divisor(math.gcd(n, col_off) if col_off else n, max(128, ACC_BYTES // (4 * k)), 128)
    ts = _divisor(s, min(2048, max(512, ACC_BYTES // (2 * k))), 16)
    off = col_off // tn

    def body(a_ref, b_ref, *rest):
        out_ref = rest[-1]

        @pl.when(pl.program_id(1) == 0)
        def _():
            out_ref[...] = jnp.zeros_like(out_ref)

        out_ref[...] += _dot_tn(a_ref[...], b_ref[...])

    in_specs = [pl.BlockSpec((ts, k), lambda j, i: (i, 0)), pl.BlockSpec((ts, tn), lambda j, i: (i, j))]
    args = [a, b]
    aliases = {}
    if stack is not None:
        in_specs.append(ANY)
        args.append(stack)
        aliases = {2: 0}
    return pl.pallas_call(
        body,
        name=name,
        grid=(n // tn, s // ts),
        in_specs=in_specs,
        out_specs=pl.BlockSpec((None, k, tn), lambda j, i: (layer, 0, off + j)),
        out_shape=SDS((DEPTH, k, n_total), F32),
        input_output_aliases=aliases,
        compiler_params=_params("parallel", "arbitrary"),
    )(*args)


def _gm_mask():
    r = lax.broadcasted_iota(jnp.int32, (GM_BLOCK, GM_BLOCK), 0) // CHUNK
    c = lax.broadcasted_iota(jnp.int32, (GM_BLOCK, GM_BLOCK), 1) // CHUNK
    return r >= c


def _gm_normed(v, lng, lnb):
    vg = _gelu(v)
    mu = jnp.mean(vg, axis=-1, keepdims=True)
    xc = vg - mu
    rstd = lax.rsqrt(jnp.mean(xc * xc, axis=-1, keepdims=True) + EPS)
    xh = xc * rstd
    return xh, rstd, xh * lng + lnb


def _gmlp_fwd(z, lng, lnb, ws, bst, name):
    s = z.shape[0]
    tg = _divisor(s, 256, GM_BLOCK)

    def body(z_ref, lng_ref, lnb_ref, ws_ref, bst_ref, a_ref):
        mask = _gm_mask()
        ug = _gelu(z_ref[:, :BR_DIM])
        _, _, vn = _gm_normed(z_ref[:, BR_DIM:], lng_ref[...], lnb_ref[...])
        vn = vn.astype(MM_DTYPE)
        for h in range(N_HEADS):
            cs = slice(h * HEAD_DIM, (h + 1) * HEAD_DIM)
            wm = jnp.where(mask, ws_ref[h], 0.0).astype(MM_DTYPE)
            for blk in range(tg // GM_BLOCK):
                rs = slice(blk * GM_BLOCK, (blk + 1) * GM_BLOCK)
                mixed = _dot(wm, vn[rs, cs]) + bst_ref[:, h : h + 1]
                a_ref[rs, cs] = (ug[rs, cs] * mixed).astype(MM_DTYPE)

    return pl.pallas_call(
        body,
        name=name,
        grid=(s // tg,),
        in_specs=[
            pl.BlockSpec((tg, 2 * BR_DIM), lambda i: (i, 0)),
            _resident((1, BR_DIM), lambda i: (0, 0)),
            _resident((1, BR_DIM), lambda i: (0, 0)),
            _resident((N_HEADS, GM_BLOCK, GM_BLOCK), lambda i: (0, 0, 0)),
            _resident((GM_BLOCK, N_HEADS), lambda i: (0, 0)),
        ],
        out_specs=pl.BlockSpec((tg, BR_DIM), lambda i: (i, 0)),
        out_shape=SDS((s, BR_DIM), MM_DTYPE),
        compiler_params=_params("parallel"),
    )(z, lng, lnb, ws, bst)


def _gmlp_bwd(da, z, lng, lnb, ws, bst, name):
    s = z.shape[0]
    tg = _divisor(s, 256, GM_BLOCK)

    def body(da_ref, z_ref, lng_ref, lnb_ref, ws_ref, bst_ref, dz_ref, dlng_ref, dlnb_ref, dws_ref, dbst_ref):
        @pl.when(pl.program_id(0) == 0)
        def _():
            dlng_ref[...] = jnp.zeros_like(dlng_ref)
            dlnb_ref[...] = jnp.zeros_like(dlnb_ref)
            dws_ref[...] = jnp.zeros_like(dws_ref)
            dbst_ref[...] = jnp.zeros_like(dbst_ref)

        mask = _gm_mask()
        u = z_ref[:, :BR_DIM]
        v = z_ref[:, BR_DIM:]
        ug = _gelu(u)
        lng_v = lng_ref[...]
        xh, rstd, vn = _gm_normed(v, lng_v, lnb_ref[...])
        vnb = vn.astype(MM_DTYPE)
        da_v = da_ref[...]
        dmix = da_v * ug
        dmixb = dmix.astype(MM_DTYPE)
        dvn_cols = []
        for h in range(N_HEADS):
            cs = slice(h * HEAD_DIM, (h + 1) * HEAD_DIM)
            wm = jnp.where(mask, ws_ref[h], 0.0).astype(MM_DTYPE)
            dw = jnp.zeros((GM_BLOCK, GM_BLOCK), F32)
            dbias = jnp.zeros((GM_BLOCK, 1), F32)
            dvn_rows = []
            for blk in range(tg // GM_BLOCK):
                rs = slice(blk * GM_BLOCK, (blk + 1) * GM_BLOCK)
                mixed = _dot(wm, vnb[rs, cs]) + bst_ref[:, h : h + 1]
                dz_ref[rs, cs] = (da_v[rs, cs] * mixed * _gelu_grad(u[rs, cs])).astype(MM_DTYPE)
                dw = dw + _dot_nt(dmixb[rs, cs], vnb[rs, cs])
                dbias = dbias + jnp.sum(dmix[rs, cs], axis=1, keepdims=True)
                dvn_rows.append(_dot_tn(wm, dmixb[rs, cs]))
            dws_ref[h] += jnp.where(mask, dw, 0.0)
            dbst_ref[:, h : h + 1] += dbias
            dvn_cols.append(jnp.concatenate(dvn_rows, axis=0) if len(dvn_rows) > 1 else dvn_rows[0])
        dvn = jnp.concatenate(dvn_cols, axis=1)
        dlng_ref[...] += jnp.sum(dvn * xh, axis=0, keepdims=True)
        dlnb_ref[...] += jnp.sum(dvn, axis=0, keepdims=True)
        dxh = dvn * lng_v
        dvg = rstd * (dxh - jnp.mean(dxh, axis=-1, keepdims=True) - xh * jnp.mean(dxh * xh, axis=-1, keepdims=True))
        dz_ref[:, BR_DIM:] = (dvg * _gelu_grad(v)).astype(MM_DTYPE)

    return pl.pallas_call(
        body,
        name=name,
        grid=(s // tg,),
        in_specs=[
            pl.BlockSpec((tg, BR_DIM), lambda i: (i, 0)),
            pl.BlockSpec((tg, 2 * BR_DIM), lambda i: (i, 0)),
            _resident((1, BR_DIM), lambda i: (0, 0)),
            _resident((1, BR_DIM), lambda i: (0, 0)),
            _resident((N_HEADS, GM_BLOCK, GM_BLOCK), lambda i: (0, 0, 0)),
            _resident((GM_BLOCK, N_HEADS), lambda i: (0, 0)),
        ],
        out_specs=[
            pl.BlockSpec((tg, 2 * BR_DIM), lambda i: (i, 0)),
            pl.BlockSpec((1, BR_DIM), lambda i: (0, 0)),
            pl.BlockSpec((1, BR_DIM), lambda i: (0, 0)),
            pl.BlockSpec((N_HEADS, GM_BLOCK, GM_BLOCK), lambda i: (0, 0, 0)),
            pl.BlockSpec((GM_BLOCK, N_HEADS), lambda i: (0, 0)),
        ],
        out_shape=[
            SDS((s, 2 * BR_DIM), MM_DTYPE),
            SDS((1, BR_DIM), F32),
            SDS((1, BR_DIM), F32),
            SDS((N_HEADS, GM_BLOCK, GM_BLOCK), F32),
            SDS((GM_BLOCK, N_HEADS), F32),
        ],
        compiler_params=_params("arbitrary"),
    )(da, z, lng, lnb, ws, bst)


HG_ROWS = 256
HG_UNROLL = 4
MID = CHUNK // 2 - 1


def _tri(lower):
    r = lax.broadcasted_iota(jnp.int32, (CHUNK, CHUNK), 0)
    c = lax.broadcasted_iota(jnp.int32, (CHUNK, CHUNK), 1)
    return r >= c if lower else c >= r


def _scan_rows(x, reverse=False):
    n = x.shape[0]
    rid = lax.broadcasted_iota(jnp.int32, x.shape, 0)
    k = 1
    while k < n:
        if reverse:
            x = x + jnp.where(rid < n - k, pltpu.roll(x, n - k, 0), 0.0)
        else:
            x = x + jnp.where(rid >= k, pltpu.roll(x, k, 0), 0.0)
        k *= 2
    return x


def _hgrn_fwd(z, lb, ng, name, rider=None):
    s = z.shape[0]
    rows_per = _divisor(s, HG_ROWS, CHUNK)
    n_sub = rows_per // CHUNK

    def body(zqf_ref, zio_ref, lb_ref, ng_ref, o_ref, b_ref, st_ref, state):
        @pl.when(pl.program_id(0) == 0)
        def _():
            state[...] = jnp.zeros_like(state)

        low = _tri(True)
        ng_v = ng_ref[...]

        def chunk(ci, carry):
            rows = pl.ds(pl.multiple_of(ci * CHUNK, CHUNK), CHUNK)
            for h in range(N_HEADS):
                cs = slice(h * HEAD_DIM, (h + 1) * HEAD_DIM)
                cs2 = slice(BR_DIM + h * HEAD_DIM, BR_DIM + (h + 1) * HEAD_DIM)
                zq = zqf_ref[rows, cs]
                zf = zqf_ref[rows, cs2]
                vi = zio_ref[rows, cs]
                zog = zio_ref[rows, cs2]
                lbh = lb_ref[:, cs]
                qf = zq * _sigmoid(zq)
                forget = lbh + (1.0 - lbh) * _sigmoid(zf)
                g = jnp.log(jnp.maximum(forget, TINY))
                k = (1.0 - lbh) * _sigmoid(-zf)
                gc = _scan_rows(g)
                gm = gc[MID : MID + 1]
                gl = gc[CHUNK - 1 : CHUNK]
                a = jnp.where(low, _dot_nt(qf * jnp.exp(gc - gm), k * jnp.exp(gm - gc)), 0.0)
                st = state[h]
                st_ref[ci, h] = st
                o = _dot(a, vi) + _dot_nt(qf * jnp.exp(gc), st)
                state[h] = st * jnp.exp(gl) + _dot_tn(vi, k * jnp.exp(gl - gc))
                r = lax.rsqrt(jnp.mean(o * o, axis=-1, keepdims=True) + EPS)
                o_ref[rows, cs] = o
                b_ref[rows, cs] = (((o * r) * ng_v) * (zog * _sigmoid(zog))).astype(MM_DTYPE)
            return carry

        lax.fori_loop(0, n_sub, chunk, 0, unroll=math.gcd(n_sub, HG_UNROLL))

    return _hosted_call(
        body,
        rider,
        name=name,
        grid=(s // rows_per,),
        in_specs=[
            pl.BlockSpec((rows_per, 2 * BR_DIM), lambda i: (i, 1)),
            pl.BlockSpec((rows_per, 2 * BR_DIM), lambda i: (i, 2)),
            _resident((1, BR_DIM), lambda i: (0, 0)),
            _resident((1, HEAD_DIM), lambda i: (0, 0)),
        ],
        out_specs=[
            pl.BlockSpec((rows_per, BR_DIM), lambda i: (i, 0)),
            pl.BlockSpec((rows_per, BR_DIM), lambda i: (i, 0)),
            pl.BlockSpec((n_sub, N_HEADS, HEAD_DIM, HEAD_DIM), lambda i: (i, 0, 0, 0)),
        ],
        out_shape=[
            SDS((s, BR_DIM), F32),
            SDS((s, BR_DIM), MM_DTYPE),
            SDS((s // CHUNK, N_HEADS, HEAD_DIM, HEAD_DIM), F32),
        ],
        scratch_shapes=[pltpu.VMEM((N_HEADS, HEAD_DIM, HEAD_DIM), F32)],
        args=[z, z, lb, ng],
        semantics=("arbitrary",),
    )


def _hgrn_bwd(db, z, o, states, lb, ng, name, rider=None):
    s = z.shape[0]
    rows_per = _divisor(s, HG_ROWS, CHUNK)
    n_sub = rows_per // CHUNK
    n_steps = s // rows_per

    def body(db_ref, zqf_ref, zio_ref, o_ref, st_ref, lb_ref, ng_ref, dz_ref, dlb_ref, dng_ref, dstate):
        @pl.when(pl.program_id(0) == 0)
        def _():
            dstate[...] = jnp.zeros_like(dstate)
            dlb_ref[...] = jnp.zeros_like(dlb_ref)
            dng_ref[...] = jnp.zeros_like(dng_ref)

        low = _tri(True)
        ng_v = ng_ref[...]

        def chunk(cc, carry):
            ci = n_sub - 1 - cc
            rows = pl.ds(pl.multiple_of(ci * CHUNK, CHUNK), CHUNK)
            for h in range(N_HEADS):
                cs = slice(h * HEAD_DIM, (h + 1) * HEAD_DIM)
                cs2 = slice(BR_DIM + h * HEAD_DIM, BR_DIM + (h + 1) * HEAD_DIM)
                zq = zqf_ref[rows, cs]
                zf = zqf_ref[rows, cs2]
                vi = zio_ref[rows, cs]
                zog = zio_ref[rows, cs2]
                lbh = lb_ref[:, cs]
                ov = o_ref[rows, cs]
                dbv = db_ref[rows, cs]
                r = lax.rsqrt(jnp.mean(ov * ov, axis=-1, keepdims=True) + EPS)
                on = ov * r
                sgo = _sigmoid(zog)
                gate = zog * sgo
                dng_ref[...] += jnp.sum(dbv * gate * on, axis=0, keepdims=True)
                don = dbv * gate * ng_v
                dzog = dbv * (on * ng_v) * _silu_grad(zog, sgo)
                do = r * (don - on * jnp.mean(don * on, axis=-1, keepdims=True))
                sq = _sigmoid(zq)
                qf = zq * sq
                sg = _sigmoid(zf)
                sgn = _sigmoid(-zf)
                oml = 1.0 - lbh
                forget = lbh + oml * sg
                fm = jnp.maximum(forget, TINY)
                k = oml * sgn
                gc = _scan_rows(jnp.log(fm))
                gm = gc[MID : MID + 1]
                gl = gc[CHUNK - 1 : CHUNK]
                e_g = jnp.exp(gc)
                e_q = jnp.exp(gc - gm)
                e_k = jnp.exp(gm - gc)
                e_kd = jnp.exp(gl - gc)
                e_gl = jnp.exp(gl)
                qt = qf * e_q
                kt = k * e_k
                a = jnp.where(low, _dot_nt(qt, kt), 0.0)
                st = st_ref[ci, h]
                dst = dstate[h]
                dp = jnp.where(low, _dot_nt(do, vi), 0.0)
                dv = _dot_tn(a, do) + _dot_nt(k * e_kd, dst)
                dq = _dot(dp, kt) * e_q + e_g * _dot(do, st)
                dks = e_kd * _dot(vi, dst)
                dk = _dot_tn(dp, qt) * e_k + dks
                dstate[h] = _dot_tn(do, qf * e_g) + dst * e_gl
                dgc = qf * dq - k * dk
                extra = e_gl * jnp.sum(st * dst, axis=0, keepdims=True) + jnp.sum(k * dks, axis=0, keepdims=True)
                dg = _scan_rows(dgc, reverse=True) + extra
                dforget = jnp.where(forget > TINY, dg / fm, 0.0)
                both = dforget - dk
                dlb_ref[:, cs] += jnp.sum(sgn * both, axis=0, keepdims=True)
                dz_ref[rows, cs] = (dq * _silu_grad(zq, sq)).astype(MM_DTYPE)
                dz_ref[rows, cs2] = (oml * sg * sgn * both).astype(MM_DTYPE)
                dz_ref[rows, 2 * BR_DIM + h * HEAD_DIM : 2 * BR_DIM + (h + 1) * HEAD_DIM] = dv.astype(MM_DTYPE)
                dz_ref[rows, 3 * BR_DIM + h * HEAD_DIM : 3 * BR_DIM + (h + 1) * HEAD_DIM] = dzog.astype(MM_DTYPE)
            return carry

        lax.fori_loop(0, n_sub, chunk, 0, unroll=math.gcd(n_sub, HG_UNROLL))

    rev = lambda i: n_steps - 1 - i
    return _hosted_call(
        body,
        rider,
        name=name,
        grid=(n_steps,),
        in_specs=[
            pl.BlockSpec((rows_per, BR_DIM), lambda i: (rev(i), 0)),
            pl.BlockSpec((rows_per, 2 * BR_DIM), lambda i: (rev(i), 1)),
            pl.BlockSpec((rows_per, 2 * BR_DIM), lambda i: (rev(i), 2)),
            pl.BlockSpec((rows_per, BR_DIM), lambda i: (rev(i), 0)),
            pl.BlockSpec((n_sub, N_HEADS, HEAD_DIM, HEAD_DIM), lambda i: (rev(i), 0, 0, 0)),
            _resident((1, BR_DIM), lambda i: (0, 0)),
            _resident((1, HEAD_DIM), lambda i: (0, 0)),
        ],
        out_specs=[
            pl.BlockSpec((rows_per, 4 * BR_DIM), lambda i: (rev(i), 0)),
            pl.BlockSpec((1, BR_DIM), lambda i: (0, 0)),
            pl.BlockSpec((1, HEAD_DIM), lambda i: (0, 0)),
        ],
        out_shape=[SDS((s, 4 * BR_DIM), MM_DTYPE), SDS((1, BR_DIM), F32), SDS((1, HEAD_DIM), F32)],
        scratch_shapes=[pltpu.VMEM((N_HEADS, HEAD_DIM, HEAD_DIM), F32)],
        args=[db, z, z, o, states, lb, ng],
        semantics=("arbitrary",),
    )


def _mix_fwd(x, a, b, z, w_gm, w_hg, w_out, name):
    s, d = x.shape
    tm = _divisor(s, 256, 16)
    g0 = 6 * BR_DIM // d

    def body(x_ref, a_ref, b_ref, ga_ref, gb_ref, wgm_ref, whg_ref, wout_ref, out_ref):
        y = _sigmoid(ga_ref[...]) * _dot(a_ref[...], wgm_ref[...]) + _sigmoid(gb_ref[...]) * _dot(b_ref[...], whg_ref[...])
        out_ref[...] = x_ref[...] + _dot(y, wout_ref[...])

    return pl.pallas_call(
        body,
        name=name,
        grid=(s // tm,),
        in_specs=[
            pl.BlockSpec((tm, d), lambda i: (i, 0)),
            pl.BlockSpec((tm, BR_DIM), lambda i: (i, 0)),
            pl.BlockSpec((tm, BR_DIM), lambda i: (i, 0)),
            pl.BlockSpec((tm, d), lambda i: (i, g0)),
            pl.BlockSpec((tm, d), lambda i: (i, g0 + 1)),
            _resident((BR_DIM, d), lambda i: (0, 0)),
            _resident((BR_DIM, d), lambda i: (0, 0)),
            _resident((d, d), lambda i: (0, 0)),
        ],
        out_specs=pl.BlockSpec((tm, d), lambda i: (i, 0)),
        out_shape=SDS((s, d), F32),
        compiler_params=_params("parallel"),
    )(x, a, b, z, z, w_gm, w_hg, w_out)


def _mix_bwd(dx, a, b, z, w_gm, w_hg, w_out, name):
    s, d = dx.shape
    tm = _divisor(s, 256, 16)
    g0 = 6 * BR_DIM // d

    def body(dx_ref, a_ref, b_ref, ga_ref, gb_ref, wgm_ref, whg_ref, wout_ref, y_ref, dpa_ref, dpb_ref, da_ref, db_ref, dg_ref):
        dy = _dot_nt(dx_ref[...], wout_ref[...])
        pa = _dot(a_ref[...], wgm_ref[...])
        pb = _dot(b_ref[...], whg_ref[...])
        sa = _sigmoid(ga_ref[...])
        sb = _sigmoid(gb_ref[...])
        y_ref[...] = (sa * pa + sb * pb).astype(MM_DTYPE)
        dpa = (dy * sa).astype(MM_DTYPE)
        dpb = (dy * sb).astype(MM_DTYPE)
        dpa_ref[...] = dpa
        dpb_ref[...] = dpb
        da_ref[...] = _dot_nt(dpa, wgm_ref[...])
        db_ref[...] = _dot_nt(dpb, whg_ref[...])
        dg_ref[:, :d] = (dy * pa * sa * (1.0 - sa)).astype(MM_DTYPE)
        dg_ref[:, d:] = (dy * pb * sb * (1.0 - sb)).astype(MM_DTYPE)

    row = lambda w: pl.BlockSpec((tm, w), lambda i: (i, 0))
    return pl.pallas_call(
        body,
        name=name,
        grid=(s // tm,),
        in_specs=[
            row(d),
            row(BR_DIM),
            row(BR_DIM),
            pl.BlockSpec((tm, d), lambda i: (i, g0)),
            pl.BlockSpec((tm, d), lambda i: (i, g0 + 1)),
            _resident((BR_DIM, d), lambda i: (0, 0)),
            _resident((BR_DIM, d), lambda i: (0, 0)),
            _resident((d, d), lambda i: (0, 0)),
        ],
        out_specs=[row(d), row(d), row(d), row(BR_DIM), row(BR_DIM), row(2 * d)],
        out_shape=[
            SDS((s, d), MM_DTYPE),
            SDS((s, d), MM_DTYPE),
            SDS((s, d), MM_DTYPE),
            SDS((s, BR_DIM), F32),
            SDS((s, BR_DIM), F32),
            SDS((s, 2 * d), MM_DTYPE),
        ],
        compiler_params=_params("parallel"),
    )(dx, a, b, z, z, w_gm, w_hg, w_out)


HALO = 8


def _shift_down(v, prev, n):
    rolled = pltpu.roll(v, n, 0)
    rid = lax.broadcasted_iota(jnp.int32, v.shape, 0)
    out = rolled
    for r in range(n):
        out = jnp.where(rid == r, prev[HALO - n + r : HALO - n + r + 1], out)
    return out


def _shift_up(v, nxt, n):
    tm = v.shape[0]
    rolled = pltpu.roll(v, tm - n, 0)
    rid = lax.broadcasted_iota(jnp.int32, v.shape, 0)
    out = rolled
    for r in range(n):
        out = jnp.where(rid == tm - n + r, nxt[r : r + 1], out)
    return out


def _conv_cols(f):
    return _divisor(f, 256, 128)


def _convffn_fwd(x, z2, cw, cb, w_down, layer, name):
    s, d = x.shape
    f = z2.shape[1] // 2
    tm = _divisor(s, 256, 16)
    cwid = _conv_cols(f)
    per8 = tm // HALO

    def body(x_ref, z_ref, prev_ref, cw_ref, cb_ref, wd_ref, out_ref, conv_ref):
        first = pl.program_id(0) == 0
        acc = x_ref[...]
        for c0 in range(0, f, cwid):
            halves = []
            for base in (c0, f + c0):
                cols = slice(base, base + cwid)
                zc = z_ref[:, cols]
                prev = jnp.where(first, 0.0, prev_ref[:, cols])
                conv = cb_ref[:, cols] + cw_ref[0:1, cols] * _shift_down(zc, prev, 2)
                conv = conv + cw_ref[1:2, cols] * _shift_down(zc, prev, 1) + cw_ref[2:3, cols] * zc
                conv_ref[:, cols] = conv
                halves.append(conv)
            gate, val = halves
            act = gate * _sigmoid(gate) * val
            acc = acc + _dot(act, wd_ref[c0 : c0 + cwid, :])
        out_ref[...] = acc

    return pl.pallas_call(
        body,
        name=name,
        grid=(s // tm,),
        in_specs=[
            pl.BlockSpec((tm, d), lambda i: (i, 0)),
            pl.BlockSpec((tm, 2 * f), lambda i: (i, 0)),
            pl.BlockSpec((HALO, 2 * f), lambda i: (jnp.maximum(i * per8 - 1, 0), 0)),
            _resident((None, 3, 2 * f), lambda i: (layer, 0, 0)),
            _resident((1, 2 * f), lambda i: (0, 0)),
            _resident((f, d), lambda i: (0, 0)),
        ],
        out_specs=[pl.BlockSpec((tm, d), lambda i: (i, 0)), pl.BlockSpec((tm, 2 * f), lambda i: (i, 0))],
        out_shape=[SDS((s, d), F32), SDS((s, 2 * f), F32)],
        compiler_params=_params("arbitrary"),
    )(x, z2, z2, cw, cb, w_down)


def _convffn_bwd(dx, z2, conv, cw, w_down, layer, name):
    s, d = dx.shape
    f = z2.shape[1] // 2
    tm = _divisor(s, 256, 16)
    cwid = _conv_cols(f)
    n_steps = s // tm

    def body(dx_ref, z_ref, conv_ref, cw_ref, wd_ref, dz_ref, act_ref, dcw_ref, nxt):
        @pl.when(pl.program_id(0) == 0)
        def _():
            nxt[...] = jnp.zeros_like(nxt)
            dcw_ref[...] = jnp.zeros_like(dcw_ref)

        dxv = dx_ref[...].astype(MM_DTYPE)
        for c0 in range(0, f, cwid):
            dact = _dot_nt(dxv, wd_ref[c0 : c0 + cwid, :])
            gate = conv_ref[:, c0 : c0 + cwid]
            val = conv_ref[:, f + c0 : f + c0 + cwid]
            sg = _sigmoid(gate)
            silu = gate * sg
            act_ref[:, c0 : c0 + cwid] = (silu * val).astype(MM_DTYPE)
            dconvs = (dact * val * _silu_grad(gate, sg), dact * silu)
            for base, dconv in zip((c0, f + c0), dconvs):
                cols = slice(base, base + cwid)
                zc = z_ref[:, cols]
                nx = nxt[:, cols]
                up1 = _shift_up(dconv, nx, 1)
                up2 = _shift_up(dconv, nx, 2)
                dcw_ref[0:1, cols] += jnp.sum(up2 * zc, axis=0, keepdims=True)
                dcw_ref[1:2, cols] += jnp.sum(up1 * zc, axis=0, keepdims=True)
                dcw_ref[2:3, cols] += jnp.sum(dconv * zc, axis=0, keepdims=True)
                dcw_ref[3:4, cols] += jnp.sum(dconv, axis=0, keepdims=True)
                dz = cw_ref[2:3, cols] * dconv + cw_ref[1:2, cols] * up1 + cw_ref[0:1, cols] * up2
                dz_ref[:, cols] = dz.astype(MM_DTYPE)
                nxt[:, cols] = dconv[0:HALO]

    rev = lambda i: n_steps - 1 - i
    return pl.pallas_call(
        body,
        name=name,
        grid=(n_steps,),
        in_specs=[
            pl.BlockSpec((tm, d), lambda i: (rev(i), 0)),
            pl.BlockSpec((tm, 2 * f), lambda i: (rev(i), 0)),
            pl.BlockSpec((tm, 2 * f), lambda i: (rev(i), 0)),
            _resident((None, 3, 2 * f), lambda i: (layer, 0, 0)),
            _resident((f, d), lambda i: (0, 0)),
        ],
        out_specs=[
            pl.BlockSpec((tm, 2 * f), lambda i: (rev(i), 0)),
            pl.BlockSpec((tm, f), lambda i: (rev(i), 0)),
            pl.BlockSpec((HALO, 2 * f), lambda i: (0, 0)),
        ],
        out_shape=[SDS((s, 2 * f), MM_DTYPE), SDS((s, f), MM_DTYPE), SDS((HALO, 2 * f), F32)],
        scratch_shapes=[pltpu.VMEM((HALO, 2 * f), F32)],
        compiler_params=_params("arbitrary"),
    )(dx, z2, conv, cw, w_down)


def _loss_head(x, gain, target):
    s, d = x.shape
    tm = _divisor(s, 256, 8)

    def body(x_ref, g_ref, t_ref, loss_ref, dx_ref, dg_ref):
        @pl.when(pl.program_id(0) == 0)
        def _():
            loss_ref[...] = jnp.zeros_like(loss_ref)
            dg_ref[...] = jnp.zeros_like(dg_ref)

        xv = x_ref[...]
        gv = g_ref[...]
        r = lax.rsqrt(jnp.mean(xv * xv, axis=-1, keepdims=True) + EPS)
        xh = xv * r
        err = xh * gv - t_ref[...]
        loss_ref[...] += 0.5 * jnp.sum(jnp.mean(err * err, axis=-1, keepdims=True))
        dout = err * (1.0 / d)
        dg_ref[...] += jnp.sum(dout * xh, axis=0, keepdims=True)
        dxh = dout * gv
        dx_ref[...] = r * (dxh - xh * jnp.mean(dxh * xh, axis=-1, keepdims=True))

    return pl.pallas_call(
        body,
        name="loss_head",
        grid=(s // tm,),
        in_specs=[
            pl.BlockSpec((tm, d), lambda i: (i, 0)),
            _resident((1, d), lambda i: (0, 0)),
            pl.BlockSpec((tm, d), lambda i: (i, 0)),
        ],
        out_specs=[
            pl.BlockSpec((8, 128), lambda i: (0, 0)),
            pl.BlockSpec((tm, d), lambda i: (i, 0)),
            pl.BlockSpec((1, d), lambda i: (0, 0)),
        ],
        out_shape=[SDS((8, 128), F32), SDS((s, d), F32), SDS((1, d), F32)],
        compiler_params=_params("arbitrary"),
    )(x, gain, target)


def _adamw(w, g, m, v, name):
    shape = w.shape
    cols = shape[-1]
    rows = max(1, math.prod(shape[:-1]))
    tr = _divisor(rows, max(8, TILE_BYTES // (4 * cols)), 8)
    flat = [t.reshape(rows, cols) for t in (w, g, m, v)]

    def body(w_ref, g_ref, m_ref, v_ref, d_ref, nm_ref, nv_ref):
        gv = g_ref[...]
        m2 = ADAM_B1 * m_ref[...] + (1.0 - ADAM_B1) * gv
        v2 = ADAM_B2 * v_ref[...] + (1.0 - ADAM_B2) * (gv * gv)
        m_hat = m2 / (1.0 - ADAM_B1**ADAM_STEP)
        v_hat = v2 / (1.0 - ADAM_B2**ADAM_STEP)
        d_ref[...] = -ADAM_LR * (m_hat / (jnp.sqrt(v_hat) + ADAM_EPS) + ADAM_WD * w_ref[...])
        nm_ref[...] = m2
        nv_ref[...] = v2

    spec = pl.BlockSpec((tr, cols), lambda i: (i, 0))
    outs = pl.pallas_call(
        body,
        name=name,
        grid=(rows // tr,),
        in_specs=[spec] * 4,
        out_specs=[spec] * 3,
        out_shape=[SDS((rows, cols), F32)] * 3,
        compiler_params=_params("parallel"),
    )(*flat)
    return tuple(t.reshape(shape) for t in outs)


def _position():
    return lax.axis_index("x"), lax.axis_index("y"), lax.axis_index("c")


def _other_chips(x, y):
    return [(1 - x, y), (x, 1 - y), (1 - x, 1 - y)]


def _remote(src, dst, send_sem, recv_sem, device):
    return pltpu.make_async_remote_copy(
        src_ref=src, dst_ref=dst, send_sem=send_sem, recv_sem=recv_sem, device_id=device, device_id_type=MESH
    )


def _sub(ref, lead, shape, axis, chip=None, half=None):
    cut = [pl.ds(0, shape[0]), pl.ds(0, shape[1])]
    if chip is not None:
        size = shape[axis] // N_CHIPS
        cut[axis] = pl.ds(chip * size, size)
    if half is not None:
        size = shape[1 - axis] // 2
        cut[1 - axis] = pl.ds(half * size, size)
    return ref.at[(*lead, *cut)]


def _scaled(shape, axis, num, den=1):
    return tuple(d * num // den if i == axis else d for i, d in enumerate(shape))


def _gather_rider(shards, axes, layer, conv_w=None):
    n = len(shards)
    shard2d = [t.shape[1:] for t in shards]
    full2d = [_scaled(s2, ax, N_CHIPS) for s2, ax in zip(shard2d, axes)]
    out_shapes = [SDS(f2, t.dtype) for f2, t in zip(full2d, shards)]
    inputs = list(shards)
    own = 6
    scratch = [pltpu.SemaphoreType.DMA((n, 7)), pltpu.SemaphoreType.DMA((n, 7))]
    if conv_w is not None:
        cshape = conv_w.shape
        inputs.append(conv_w)
        out_shapes.append(SDS(_scaled(cshape, 2, N_CHIPS), conv_w.dtype))
        scratch += [pltpu.SemaphoreType.DMA((4,)), pltpu.SemaphoreType.DMA((4,))]

    def conv_slab(ref, chip):
        return ref.at[:, :, pl.ds((2 * chip[0] + chip[1]) * cshape[2], cshape[2])]

    def own_copy(ins, outs, send_sems, recv_sems, k):
        x, y, c = _position()
        slab = _sub(outs[k], (), full2d[k], axes[k], chip=2 * x + y)
        return _remote(ins[k].at[layer], slab, send_sems.at[k, own], recv_sems.at[k, own], (x, y, 1 - c))

    def start(ins, outs, scr):
        send_sems, recv_sems = scr[:2]
        x, y, c = _position()
        me = 2 * x + y
        for k in range(n):
            own_copy(ins, outs, send_sems, recv_sems, k).start()
            for j, chip in enumerate(_other_chips(x, y)):
                _remote(
                    _sub(ins[k], (layer,), shard2d[k], axes[k], half=c),
                    _sub(outs[k], (), full2d[k], axes[k], chip=me, half=c),
                    send_sems.at[k, j], recv_sems.at[k, j], (*chip, c),
                ).start()
        if conv_w is not None:
            csend, crecv = scr[2:]
            _remote(ins[n], conv_slab(outs[n], (x, y)), csend.at[3], crecv.at[3], (x, y, 1 - c)).start()
            for j, chip in enumerate(_other_chips(x, y)):
                _remote(ins[n], conv_slab(outs[n], (x, y)), csend.at[j], crecv.at[j], (*chip, c)).start()

    def finish(ins, outs, scr):
        send_sems, recv_sems = scr[:2]
        x, y, c = _position()
        me = 2 * x + y
        sibling = (x, y, 1 - c)
        chips = _other_chips(x, y)
        forwards = []
        for k in range(n):
            src = _sub(ins[k], (layer,), shard2d[k], axes[k], half=c)
            for j, chip in enumerate(chips):
                landed = _sub(outs[k], (), full2d[k], axes[k], chip=2 * chip[0] + chip[1], half=c)
                _remote(src, landed, send_sems.at[k, j], recv_sems.at[k, j], (*chip, c)).wait_recv()
                fwd = _remote(landed, landed, send_sems.at[k, 3 + j], recv_sems.at[k, 3 + j], sibling)
                fwd.start()
                forwards.append(fwd)
        for k in range(n):
            src = _sub(ins[k], (layer,), shard2d[k], axes[k], half=c)
            for j, chip in enumerate(chips):
                theirs = _sub(outs[k], (), full2d[k], axes[k], chip=2 * chip[0] + chip[1], half=1 - c)
                _remote(src, theirs, send_sems.at[k, 3 + j], recv_sems.at[k, 3 + j], sibling).wait_recv()
        for fwd in forwards:
            fwd.wait_send()
        for k in range(n):
            src = _sub(ins[k], (layer,), shard2d[k], axes[k], half=c)
            mine = _sub(outs[k], (), full2d[k], axes[k], chip=me, half=c)
            for j, chip in enumerate(chips):
                _remote(src, mine, send_sems.at[k, j], recv_sems.at[k, j], (*chip, c)).wait_send()
            own_copy(ins, outs, send_sems, recv_sems, k).wait()
        if conv_w is not None:
            csend, crecv = scr[2:]
            for j, chip in enumerate(chips):
                cp = _remote(ins[n], conv_slab(outs[n], chip), csend.at[j], crecv.at[j], (*chip, c))
                cp.wait_recv()
                cp.wait_send()
            _remote(ins[n], conv_slab(outs[n], (x, y)), csend.at[3], crecv.at[3], sibling).wait()

    return _Rider(inputs, out_shapes, scratch, start, finish)


def _pair_rider(stacks, axes, layer):
    n = len(stacks)
    shapes = [t.shape[1:] for t in stacks]
    out_shapes = [SDS(_scaled(s2, 1 - ax, 1, 2), F32) for s2, ax in zip(shapes, axes)]
    scratch = [pltpu.SemaphoreType.DMA((n,)), pltpu.SemaphoreType.DMA((n,))]

    def copies(ins, outs, scr):
        x, y, c = _position()
        return [
            _remote(_sub(ins[k], (layer,), shapes[k], axes[k], half=1 - c), outs[k], scr[0].at[k], scr[1].at[k], (x, y, 1 - c))
            for k in range(n)
        ]

    def start(ins, outs, scr):
        for cp in copies(ins, outs, scr):
            cp.start()

    def finish(ins, outs, scr):
        for cp in copies(ins, outs, scr):
            cp.wait()

    return _Rider(stacks, out_shapes, scratch, start, finish)


def _chip_rider(pairs, axes):
    n = len(pairs)
    shapes = [t.shape for t in pairs]
    out_shapes = [SDS((3,) + _scaled(s2, ax, 1, N_CHIPS), t.dtype) for s2, ax, t in zip(shapes, axes, pairs)]
    scratch = [pltpu.SemaphoreType.DMA((n, 3)), pltpu.SemaphoreType.DMA((n, 3))]

    def copies(ins, outs, scr):
        x, y, c = _position()
        return [
            _remote(
                _sub(ins[k], (), shapes[k], axes[k], chip=2 * chip[0] + chip[1]), outs[k].at[j],
                scr[0].at[k, j], scr[1].at[k, j], (*chip, c),
            )
            for k in range(n)
            for j, chip in enumerate(_other_chips(x, y))
        ]

    def start(ins, outs, scr):
        for cp in copies(ins, outs, scr):
            cp.start()

    def finish(ins, outs, scr):
        for cp in copies(ins, outs, scr):
            cp.wait()

    return _Rider(pairs, out_shapes, scratch, start, finish)


def _complete_rider(stacks, axes, layers):
    n = len(stacks)
    shapes = [t.shape[1:] for t in stacks]
    scratch = [pltpu.SemaphoreType.DMA((n,)), pltpu.SemaphoreType.DMA((n,))]

    def start(ins, outs, scr):
        x, y, c = _position()
        for k in range(n):
            mine = _sub(outs[k], (layers[k],), shapes[k], axes[k], half=c)
            _remote(mine, mine, scr[0].at[k], scr[1].at[k], (x, y, 1 - c)).start()

    def finish(ins, outs, scr):
        x, y, c = _position()
        for k in range(n):
            mine = _sub(outs[k], (layers[k],), shapes[k], axes[k], half=c)
            theirs = _sub(outs[k], (layers[k],), shapes[k], axes[k], half=1 - c)
            _remote(mine, theirs, scr[0].at[k], scr[1].at[k], (x, y, 1 - c)).wait_recv()
            _remote(mine, mine, scr[0].at[k], scr[1].at[k], (x, y, 1 - c)).wait_send()

    return _Rider(stacks, [SDS(t.shape, t.dtype) for t in stacks], scratch, start, finish, {k: k for k in range(n)})


def _gather_small(buf):
    rows, cols = buf.shape
    flips = [(fx, fy, fc) for fx in (0, 1) for fy in (0, 1) for fc in (0, 1)][1:]

    def body(in_ref, out_ref, send_sems, recv_sems, local_sem):
        x, y, c = _position()

        def peer(f):
            return (x + f[0] - 2 * x * f[0], y + f[1] - 2 * y * f[1], c + f[2] - 2 * c * f[2])

        me = 4 * x + 2 * y + c
        local = pltpu.make_async_copy(in_ref, out_ref.at[me], local_sem)
        local.start()
        copies = []
        for j, f in enumerate(flips):
            cp = _remote(in_ref, out_ref.at[me], send_sems.at[j], recv_sems.at[j], peer(f))
            cp.start()
            copies.append(cp)
        for j, f in enumerate(flips):
            px, py, pc = peer(f)
            _remote(in_ref, out_ref.at[4 * px + 2 * py + pc], send_sems.at[j], recv_sems.at[j], peer(f)).wait_recv()
        for cp in copies:
            cp.wait_send()
        local.wait()

    return pl.pallas_call(
        body,
        name="gather_small_grads",
        in_specs=[ANY],
        out_specs=ANY,
        out_shape=SDS((8, rows, cols), buf.dtype),
        scratch_shapes=[pltpu.SemaphoreType.DMA((7,)), pltpu.SemaphoreType.DMA((7,)), pltpu.SemaphoreType.DMA(())],
    )(buf)


def _pair_sum(idx, stack, layer, recv, axis, name):
    hk, hn = recv.shape
    tk = _divisor(hk, max(16, TILE_BYTES // (4 * hn)), 16)
    per = hk // tk

    def body(idx_ref, g_ref, r_ref, out_ref):
        out_ref[...] = (g_ref[...] + r_ref[...]).astype(out_ref.dtype)

    if axis == 1:
        mine = pl.BlockSpec((None, tk, hn), lambda i, idx_ref: (layer, idx_ref[0] * per + i, 0))
    else:
        mine = pl.BlockSpec((None, tk, hn), lambda i, idx_ref: (layer, i, idx_ref[0]))
    return pl.pallas_call(
        body,
        name=name,
        grid_spec=pltpu.PrefetchScalarGridSpec(
            num_scalar_prefetch=1,
            grid=(per,),
            in_specs=[mine, pl.BlockSpec((tk, hn), lambda i, idx_ref: (i, 0))],
            out_specs=pl.BlockSpec((tk, hn), lambda i, idx_ref: (i, 0)),
        ),
        out_shape=SDS((hk, hn), BF16),
        compiler_params=_params("parallel"),
    )(idx, stack, recv)


def _chip_sum(idx, pair, others, axis, stack, layer, name):
    _, sk, sn = others.shape
    tk = _divisor(sk, max(16, TILE_BYTES // (4 * sn)), 16)
    per = sk // tk

    def body(idx_ref, p_ref, o0_ref, o1_ref, o2_ref, *rest):
        out_ref = rest[-1]
        acc = p_ref[...].astype(F32) + o0_ref[...].astype(F32)
        out_ref[...] = (acc + o1_ref[...].astype(F32)) + o2_ref[...].astype(F32)

    if axis == 1:
        own = pl.BlockSpec((tk, sn), lambda i, idx_ref: (i, idx_ref[1]))
        out = pl.BlockSpec((None, tk, sn), lambda i, idx_ref: (layer, idx_ref[0] * per + i, 0))
        shard = (2 * sk, sn)
    else:
        own = pl.BlockSpec((tk, sn), lambda i, idx_ref: (idx_ref[1] * per + i, 0))
        out = pl.BlockSpec((None, tk, sn), lambda i, idx_ref: (layer, i, idx_ref[0]))
        shard = (sk, 2 * sn)
    other = lambda j: pl.BlockSpec((None, tk, sn), lambda i, idx_ref: (j, i, 0))
    in_specs = [own, other(0), other(1), other(2)]
    args = [idx, pair, others, others, others]
    aliases = {}
    if stack is not None:
        in_specs.append(ANY)
        args.append(stack)
        aliases = {5: 0}
    return pl.pallas_call(
        body,
        name=name,
        grid_spec=pltpu.PrefetchScalarGridSpec(num_scalar_prefetch=1, grid=(per,), in_specs=in_specs, out_specs=out),
        out_shape=SDS((DEPTH,) + shard, F32),
        input_output_aliases=aliases,
        compiler_params=_params("parallel"),
    )(*args)


def _sum_devices(gathered):
    _, rows, cols = gathered.shape

    def body(g_ref, out_ref):
        acc = g_ref[0]
        for dev in range(1, 8):
            acc = acc + g_ref[dev]
        out_ref[...] = acc

    return pl.pallas_call(body, name="sum_small_grads", out_shape=SDS((rows, cols), F32))(gathered)


PACK_TILE = 8 * 128


def _pack(arrays):
    parts = []
    for t in arrays:
        flat = t.reshape(-1)
        pad = (-flat.shape[0]) % PACK_TILE
        if pad:
            flat = jnp.concatenate([flat, jnp.zeros((pad,), flat.dtype)])
        parts.append(flat.reshape(-1, 128))
    return jnp.concatenate(parts, axis=0)


def _unpack(buf, shapes):
    out, row = [], 0
    for shp in shapes:
        size = math.prod(shp)
        rows = -(-size // PACK_TILE) * 8
        out.append(buf[row : row + rows].reshape(-1)[:size].reshape(shp))
        row += rows
    return out


BIG = ("w_in", "w_br_gm", "w_br_hg", "w_out", "w_up", "w_down")
BIG_AXIS = (1, 1, 1, 0, 1, 0)
REPLICATED = ("mix_norm", "gm_ln_g", "gm_ln_b", "gm_ws", "gm_bs", "hg_lb_logits", "hg_norm_g", "ffn_norm", "conv_b", "final_norm")
WEIGHTS = ("mix_norm", "w_in", "gm_ln_g", "gm_ln_b", "gm_ws", "gm_bs", "hg_lb_logits", "hg_norm_g", "w_br_gm", "w_br_hg", "w_out",
           "ffn_norm", "w_up", "conv_w", "conv_b", "w_down", "final_norm")


def kernel(x, mix_norm, w_in, gm_ln_g, gm_ln_b, gm_ws, gm_bs, hg_lb_logits, hg_norm_g, w_br_gm, w_br_hg, w_out, ffn_norm, w_up, conv_w, conv_b, w_down, final_norm, loss_target, m_mix_norm, m_w_in, m_gm_ln_g, m_gm_ln_b, m_gm_ws, m_gm_bs, m_hg_lb_logits, m_hg_norm_g, m_w_br_gm, m_w_br_hg, m_w_out, m_ffn_norm, m_w_up, m_conv_w, m_conv_b, m_w_down, m_final_norm, v_mix_norm, v_w_in, v_gm_ln_g, v_gm_ln_b, v_gm_ws, v_gm_bs, v_hg_lb_logits, v_hg_norm_g, v_w_br_gm, v_w_br_hg, v_w_out, v_ffn_norm, v_w_up, v_conv_w, v_conv_b, v_w_down, v_final_norm):
    w = dict(mix_norm=mix_norm, w_in=w_in, gm_ln_g=gm_ln_g, gm_ln_b=gm_ln_b, gm_ws=gm_ws, gm_bs=gm_bs, hg_lb_logits=hg_lb_logits,
             hg_norm_g=hg_norm_g, w_br_gm=w_br_gm, w_br_hg=w_br_hg, w_out=w_out, ffn_norm=ffn_norm, w_up=w_up, conv_w=conv_w,
             conv_b=conv_b, w_down=w_down, final_norm=final_norm)
    m = dict(mix_norm=m_mix_norm, w_in=m_w_in, gm_ln_g=m_gm_ln_g, gm_ln_b=m_gm_ln_b, gm_ws=m_gm_ws, gm_bs=m_gm_bs,
             hg_lb_logits=m_hg_lb_logits, hg_norm_g=m_hg_norm_g, w_br_gm=m_w_br_gm, w_br_hg=m_w_br_hg, w_out=m_w_out,
             ffn_norm=m_ffn_norm, w_up=m_w_up, conv_w=m_conv_w, conv_b=m_conv_b, w_down=m_w_down, final_norm=m_final_norm)
    v = dict(mix_norm=v_mix_norm, w_in=v_w_in, gm_ln_g=v_gm_ln_g, gm_ln_b=v_gm_ln_b, gm_ws=v_gm_ws, gm_bs=v_gm_bs,
             hg_lb_logits=v_hg_lb_logits, hg_norm_g=v_hg_norm_g, w_br_gm=v_w_br_gm, w_br_hg=v_w_br_hg, w_out=v_w_out,
             ffn_norm=v_ffn_norm, w_up=v_w_up, conv_w=v_conv_w, conv_b=v_conv_b, w_down=v_w_down, final_norm=v_final_norm)

    px, py, pc = _position()
    chip = 2 * px + py
    idx = jnp.stack([pc, chip]).astype(jnp.int32)
    axes = list(BIG_AXIS)
    d = x.shape[2]

    shards = [w[k].astype(MM_DTYPE) for k in BIG]
    w_in_0, conv_full = _run_rider(_gather_rider(shards[:1], axes[:1], 0, conv_w=conv_w), "gather_weights_0")
    full = [dict(w_in=w_in_0)]
    lb = _lower_bounds(hg_lb_logits)
    xs = x[0]
    saved = []
    mixer = 4
    for l in range(DEPTH):
        fw = full[l]
        more = l + 1 < DEPTH
        rider = _gather_rider(shards[:mixer], axes[:mixer], l + 1) if more else None
        (z, h), got_mixer = _norm_matmul(xs, mix_norm[l : l + 1], fw["w_in"], f"in_proj_{l}", rider)
        bst = gm_bs[l].T
        a = _gmlp_fwd(z, gm_ln_g[l : l + 1], gm_ln_b[l : l + 1], gm_ws[l], bst, f"gmlp_fwd_{l}")
        rider = _gather_rider(shards[1:], axes[1:], 0) if l == 0 else None
        (o, b, states), got = _hgrn_fwd(z, lb[l : l + 1], hg_norm_g[l : l + 1], f"hgrn_fwd_{l}", rider)
        if l == 0:
            fw.update(zip(BIG[1:], got))
        x1 = _mix_fwd(xs, a, b, z, fw["w_br_gm"], fw["w_br_hg"], fw["w_out"], f"mix_fwd_{l}")
        rider = _gather_rider(shards[mixer:], axes[mixer:], l + 1) if more else None
        (z2, h2), got_ffn = _norm_matmul(x1, ffn_norm[l : l + 1], fw["w_up"], f"up_proj_{l}", rider)
        if more:
            full.append(dict(zip(BIG, got_mixer + got_ffn)))
        x2, conv = _convffn_fwd(x1, z2, conv_full, conv_b[l : l + 1], fw["w_down"], l, f"convffn_fwd_{l}")
        saved.append((xs, h, z, a, b, o, states, x1, h2, z2, conv, bst))
        xs = x2

    loss_tile, dx, d_final = _loss_head(xs, final_norm.reshape(1, d), loss_target[0])
    loss = lax.psum(loss_tile[0, 0], ("x", "y", "c"))

    big = {k: None for k in BIG}
    grads = [None] * len(BIG)
    per_layer = [None] * DEPTH
    mix_ids, ffn_ids = list(range(mixer)), list(range(mixer, len(BIG)))
    mix_axes, ffn_axes = axes[:mixer], axes[mixer:]
    mix_pairs = None
    mix_summed = None

    def pair_sums(ids, received, layer):
        return [_pair_sum(idx, big[BIG[k]], layer, r, axes[k], f"pair_sum_{BIG[k]}_{layer}") for k, r in zip(ids, received)]

    def chip_sums(ids, pairs, others, layer):
        for k, p, ot in zip(ids, pairs, others):
            grads[k] = _chip_sum(idx, p, ot, axes[k], grads[k], layer, f"chip_sum_{BIG[k]}_{layer}")

    for l in reversed(range(DEPTH)):
        x0, h, z, a, b, o, states, x1, h2, z2, conv, bst = saved[l]
        fw = full[l]
        dz2, act, dconv = _convffn_bwd(dx, z2, conv, conv_full, fw["w_down"], l, f"convffn_bwd_{l}")
        big["w_down"] = _matmul_tn(act, dx, big["w_down"], l, f"dw_down_{l}")
        big["w_up"] = _matmul_tn(h2, dz2, big["w_up"], l, f"dw_up_{l}")
        rider = _pair_rider([big[BIG[k]] for k in ffn_ids], ffn_axes, l)
        if mix_pairs is not None:
            rider = _join(rider, _chip_rider(mix_pairs, mix_axes))
        (dx1, d_ffn), got = _proj_norm_bwd([dz2], fw["w_up"], x1, ffn_norm[l : l + 1], dx, f"up_proj_bwd_{l}", rider)
        if mix_pairs is not None:
            chip_sums(mix_ids, mix_pairs, got[len(ffn_ids) :], l + 1)
            mix_summed = l + 1
        ffn_pairs = pair_sums(ffn_ids, got[: len(ffn_ids)], l)
        y, dpa, dpb, da, db, dgates = _mix_bwd(dx1, a, b, z, fw["w_br_gm"], fw["w_br_hg"], fw["w_out"], f"mix_bwd_{l}")
        big["w_out"] = _matmul_tn(y, dx1, big["w_out"], l, f"dw_out_{l}")
        big["w_br_gm"] = _matmul_tn(a, dpa, big["w_br_gm"], l, f"dw_br_gm_{l}")
        big["w_br_hg"] = _matmul_tn(b, dpb, big["w_br_hg"], l, f"dw_br_hg_{l}")
        (dz_hg, d_lb, d_ng), others = _hgrn_bwd(
            db, z, o, states, lb[l : l + 1], hg_norm_g[l : l + 1], f"hgrn_bwd_{l}", _chip_rider(ffn_pairs, ffn_axes)
        )
        chip_sums(ffn_ids, ffn_pairs, others, l)
        dz_gm, d_lng, d_lnb, d_ws, d_bst = _gmlp_bwd(da, z, gm_ln_g[l : l + 1], gm_ln_b[l : l + 1], gm_ws[l], bst, f"gmlp_bwd_{l}")
        n_in = fw["w_in"].shape[1]
        big["w_in"] = _matmul_tn(h, dz_gm, big["w_in"], l, f"dw_in_gm_{l}", n_total=n_in, col_off=0)
        big["w_in"] = _matmul_tn(h, dz_hg, big["w_in"], l, f"dw_in_hg_{l}", n_total=n_in, col_off=2 * BR_DIM)
        big["w_in"] = _matmul_tn(h, dgates, big["w_in"], l, f"dw_in_gate_{l}", n_total=n_in, col_off=6 * BR_DIM)
        done_ids = ffn_ids + (mix_ids if mix_summed is not None else [])
        done_layers = [l] * len(ffn_ids) + ([mix_summed] * len(mix_ids) if mix_summed is not None else [])
        rider = _join(
            _pair_rider([big[BIG[k]] for k in mix_ids], mix_axes, l),
            _complete_rider([grads[k] for k in done_ids], [axes[k] for k in done_ids], done_layers),
        )
        (dx, d_mix), got = _proj_norm_bwd([dz_gm, dz_hg, dgates], fw["w_in"], x0, mix_norm[l : l + 1], dx1, f"in_proj_bwd_{l}", rider)
        for k, g in zip(done_ids, got[len(mix_ids) :]):
            grads[k] = g
        mix_pairs = pair_sums(mix_ids, got[: len(mix_ids)], l)
        per_layer[l] = dict(
            mix_norm=d_mix[0], gm_ln_g=d_lng[0], gm_ln_b=d_lnb[0], gm_ws=d_ws, gm_bs=d_bst.T, lb=d_lb[0], hg_norm_g=d_ng[0],
            ffn_norm=d_ffn[0], conv_w=dconv[0:3], conv_b=dconv[3],
        )

    others = _run_rider(_chip_rider(mix_pairs, mix_axes), "grad_chip_exchange_0")
    chip_sums(mix_ids, mix_pairs, others, 0)
    done = _run_rider(_complete_rider([grads[k] for k in mix_ids], mix_axes, [0] * len(mix_ids)), "grad_complete_0")
    for k, g in zip(mix_ids, done):
        grads[k] = g
    grad = dict(zip(BIG, grads))

    small_grads = {k: jnp.stack([per_layer[l][k] for l in range(DEPTH)]) for k in per_layer[0]}
    small_grads["hg_lb_logits"] = _lower_bounds_bwd(hg_lb_logits, small_grads.pop("lb"))
    small_grads["final_norm"] = d_final[0]
    small_names = list(REPLICATED) + ["conv_w"]
    packed = _pack([small_grads[k] for k in small_names])
    summed = _sum_devices(_gather_small(packed))
    for k, g in zip(small_names, _unpack(summed, [small_grads[k].shape for k in small_names])):
        grad[k] = g
    grad["conv_w"] = lax.dynamic_slice_in_dim(grad["conv_w"], chip * conv_w.shape[2], conv_w.shape[2], axis=2)

    delta, new_m, new_v = {}, {}, {}
    for k in WEIGHTS:
        delta[k], new_m[k], new_v[k] = _adamw(w[k], grad[k], m[k], v[k], f"adamw_{k}")

    return (loss, dx[None], *[grad[k] for k in WEIGHTS], *[delta[k] for k in WEIGHTS], *[new_m[k] for k in WEIGHTS],
            *[new_v[k] for k in WEIGHTS])
```

```python
import math

import jax
import jax.numpy as jnp
from jax import lax
from jax.experimental import pallas as pl
from jax.experimental.pallas import tpu as pltpu

F32 = jnp.float32
BF16 = jnp.bfloat16
MM_DTYPE = BF16

N_HEADS = 4
HEAD_DIM = 128
BR_DIM = N_HEADS * HEAD_DIM
CHUNK = 64
GM_BLOCK = 128
DEPTH = 4
N_CHIPS = 4
EPS = 1e-6
TINY = 1e-30
LB_MAX = 0.999
ADAM_LR = 0.001
ADAM_B1 = 0.9
ADAM_B2 = 0.999
ADAM_EPS = 1e-08
ADAM_WD = 0.01
ADAM_STEP = 10
INV_SQRT2 = 1.0 / math.sqrt(2.0)
INV_SQRT_2PI = 1.0 / math.sqrt(2.0 * math.pi)

VMEM_LIMIT_BYTES = 56 * 1024 * 1024
TILE_BYTES = 2 * 1024 * 1024
ACC_BYTES = 6 * 1024 * 1024
MESH = pl.DeviceIdType.MESH
SDS = jax.ShapeDtypeStruct
ANY = pl.BlockSpec(memory_space=pl.ANY)


def _params(*sem):
    return pltpu.CompilerParams(dimension_semantics=sem or None, vmem_limit_bytes=VMEM_LIMIT_BYTES)


def _divisor(n, limit, mult):
    best = None
    for d in range(mult, min(n, limit) + 1, mult):
        if n % d == 0:
            best = d
    return best if best is not None else n


def _dot(a, b):
    return jnp.dot(a.astype(MM_DTYPE), b.astype(MM_DTYPE), preferred_element_type=F32)


def _dot_nt(a, b):
    return lax.dot_general(a.astype(MM_DTYPE), b.astype(MM_DTYPE), (((1,), (1,)), ((), ())), preferred_element_type=F32)


def _dot_tn(a, b):
    return lax.dot_general(a.astype(MM_DTYPE), b.astype(MM_DTYPE), (((0,), (0,)), ((), ())), preferred_element_type=F32)


def _sigmoid(x):
    return 1.0 / (1.0 + jnp.exp(-x))


def _gelu(x):
    return 0.5 * x * (1.0 + lax.erf(x * INV_SQRT2))


def _gelu_grad(x):
    return 0.5 * (1.0 + lax.erf(x * INV_SQRT2)) + x * jnp.exp(-0.5 * x * x) * INV_SQRT_2PI


def _silu_grad(x, s):
    return s * (1.0 + x * (1.0 - s))


def _resident(shape, index_map):
    return pl.BlockSpec(shape, index_map, pipeline_mode=pl.Buffered(1))


class _Rider:
    def __init__(self, inputs, out_shapes, scratch, start, finish, aliases=None):
        self.inputs = list(inputs)
        self.out_shapes = list(out_shapes)
        self.scratch = list(scratch)
        self.start = start
        self.finish = finish
        self.aliases = dict(aliases or {})


def _join(a, b):
    if a is None or b is None:
        return a if b is None else b
    na, oa, sa = len(a.inputs), len(a.out_shapes), len(a.scratch)

    def start(i, o, s):
        a.start(i[:na], o[:oa], s[:sa])
        b.start(i[na:], o[oa:], s[sa:])

    def finish(i, o, s):
        a.finish(i[:na], o[:oa], s[:sa])
        b.finish(i[na:], o[oa:], s[sa:])

    aliases = dict(a.aliases)
    aliases.update({na + k: oa + v for k, v in b.aliases.items()})
    return _Rider(a.inputs + b.inputs, a.out_shapes + b.out_shapes, a.scratch + b.scratch, start, finish, aliases)


def _hosted_call(body, rider, *, name, grid, in_specs, out_specs, out_shape, scratch_shapes, args, semantics):
    n_in, n_out, n_scr = len(in_specs), len(out_specs), len(scratch_shapes)
    if rider is None:
        outs = pl.pallas_call(
            body, name=name, grid=grid, in_specs=in_specs, out_specs=out_specs, out_shape=out_shape,
            scratch_shapes=scratch_shapes, compiler_params=_params(*semantics),
        )(*args)
        return list(outs), []
    ri, ro = len(rider.inputs), len(rider.out_shapes)

    def wrapped(*refs):
        p = 0
        hin = refs[p : p + n_in]
        p += n_in
        rin = refs[p : p + ri]
        p += ri
        hout = refs[p : p + n_out]
        p += n_out
        rout = refs[p : p + ro]
        p += ro
        hscr = refs[p : p + n_scr]
        rscr = refs[p + n_scr :]

        @pl.when(pl.program_id(0) == 0)
        def _():
            rider.start(rin, rout, rscr)

        body(*hin, *hout, *hscr)

        @pl.when(pl.program_id(0) == grid[0] - 1)
        def _():
            rider.finish(rin, rout, rscr)

    outs = pl.pallas_call(
        wrapped,
        name=name,
        grid=grid,
        in_specs=list(in_specs) + [ANY] * ri,
        out_specs=list(out_specs) + [ANY] * ro,
        out_shape=list(out_shape) + rider.out_shapes,
        scratch_shapes=list(scratch_shapes) + rider.scratch,
        input_output_aliases={n_in + k: n_out + v for k, v in rider.aliases.items()},
        compiler_params=_params(*semantics),
    )(*args, *rider.inputs)
    return list(outs[:n_out]), list(outs[n_out:])


def _run_rider(rider, name):
    ri, ro = len(rider.inputs), len(rider.out_shapes)

    def body(*refs):
        rin, rout, rscr = refs[:ri], refs[ri : ri + ro], refs[ri + ro :]
        rider.start(rin, rout, rscr)
        rider.finish(rin, rout, rscr)

    return list(
        pl.pallas_call(
            body,
            name=name,
            in_specs=[ANY] * ri,
            out_specs=[ANY] * ro,
            out_shape=rider.out_shapes,
            scratch_shapes=rider.scratch,
            input_output_aliases=rider.aliases,
        )(*rider.inputs)
    )


def _lower_bounds(logits):
    def body(lg_ref, lb_ref):
        lg = lg_ref[...]
        mx = jnp.max(lg, axis=0, keepdims=True)
        e = jnp.exp(lg - mx)
        p = e / jnp.sum(e, axis=0, keepdims=True)
        run = p[0:1]
        rows = [run - p[0:1]]
        for l in range(1, DEPTH):
            run = run + p[l : l + 1]
            rows.append(run - p[0:1])
        raw = jnp.concatenate(rows, axis=0)
        lb_ref[...] = jnp.minimum(jnp.maximum(raw, 0.0), LB_MAX)

    return pl.pallas_call(body, name="lower_bounds", out_shape=SDS(logits.shape, F32))(logits)


def _lower_bounds_bwd(logits, dlb):
    def body(lg_ref, dlb_ref, dlg_ref):
        lg = lg_ref[...]
        mx = jnp.max(lg, axis=0, keepdims=True)
        e = jnp.exp(lg - mx)
        p = e / jnp.sum(e, axis=0, keepdims=True)
        run = p[0:1]
        rows = [run - p[0:1]]
        for l in range(1, DEPTH):
            run = run + p[l : l + 1]
            rows.append(run - p[0:1])
        raw = jnp.concatenate(rows, axis=0)
        lo = jnp.where(raw > 0.0, 1.0, jnp.where(raw == 0.0, 0.5, 0.0))
        clipped = jnp.maximum(raw, 0.0)
        hi = jnp.where(clipped < LB_MAX, 1.0, jnp.where(clipped == LB_MAX, 0.5, 0.0))
        draw = dlb_ref[...] * lo * hi
        total = jnp.sum(draw, axis=0, keepdims=True)
        dps = []
        tail = total
        for j in range(DEPTH):
            dps.append(tail - total if j == 0 else tail)
            tail = tail - draw[j : j + 1]
        dp = jnp.concatenate(dps, axis=0)
        dlg_ref[...] = p * (dp - jnp.sum(p * dp, axis=0, keepdims=True))

    return pl.pallas_call(body, name="lower_bounds_bwd", out_shape=SDS(logits.shape, F32))(logits, dlb)


def _norm_matmul(x, gain, w, name, rider=None):
    s, d = x.shape
    n = w.shape[1]
    tm = _divisor(s, 512, 16)

    def body(x_ref, g_ref, w_ref, z_ref, h_ref):
        xv = x_ref[...]
        r = lax.rsqrt(jnp.mean(xv * xv, axis=-1, keepdims=True) + EPS)
        h = ((xv * r) * g_ref[...]).astype(MM_DTYPE)
        h_ref[...] = h
        z_ref[...] = jnp.dot(h, w_ref[...], preferred_element_type=F32)

    return _hosted_call(
        body,
        rider,
        name=name,
        grid=(s // tm,),
        in_specs=[
            pl.BlockSpec((tm, d), lambda i: (i, 0)),
            _resident((1, d), lambda i: (0, 0)),
            _resident((d, n), lambda i: (0, 0)),
        ],
        out_specs=[pl.BlockSpec((tm, n), lambda i: (i, 0)), pl.BlockSpec((tm, d), lambda i: (i, 0))],
        out_shape=[SDS((s, n), F32), SDS((s, d), MM_DTYPE)],
        scratch_shapes=[],
        args=[x, gain, w],
        semantics=("arbitrary",),
    )


def _proj_norm_bwd(dz_parts, w, x, gain, dres, name, rider=None):
    s, d = x.shape
    n = w.shape[1]
    tm = _divisor(s, 512, 16)
    widths = [p.shape[1] for p in dz_parts]
    offsets = [sum(widths[:k]) for k in range(len(widths))]
    n_parts = len(dz_parts)

    def body(*refs):
        dz_refs = refs[:n_parts]
        w_ref, x_ref, g_ref, dres_ref, dx_ref, dg_ref = refs[n_parts:]

        @pl.when(pl.program_id(0) == 0)
        def _():
            dg_ref[...] = jnp.zeros_like(dg_ref)

        dh = None
        for dz_ref, off, wd in zip(dz_refs, offsets, widths):
            part = _dot_nt(dz_ref[...], w_ref[:, off : off + wd])
            dh = part if dh is None else dh + part
        xv = x_ref[...]
        r = lax.rsqrt(jnp.mean(xv * xv, axis=-1, keepdims=True) + EPS)
        xh = xv * r
        dg_ref[...] += jnp.sum(dh * xh, axis=0, keepdims=True)
        dxh = dh * g_ref[...]
        dx_ref[...] = dres_ref[...] + r * (dxh - xh * jnp.mean(dxh * xh, axis=-1, keepdims=True))

    in_specs = [pl.BlockSpec((tm, wd), lambda i: (i, 0)) for wd in widths] + [
        _resident((d, n), lambda i: (0, 0)),
        pl.BlockSpec((tm, d), lambda i: (i, 0)),
        _resident((1, d), lambda i: (0, 0)),
        pl.BlockSpec((tm, d), lambda i: (i, 0)),
    ]
    return _hosted_call(
        body,
        rider,
        name=name,
        grid=(s // tm,),
        in_specs=in_specs,
        out_specs=[pl.BlockSpec((tm, d), lambda i: (i, 0)), pl.BlockSpec((1, d), lambda i: (0, 0))],
        out_shape=[SDS((s, d), F32), SDS((1, d), F32)],
        scratch_shapes=[],
        args=[*dz_parts, w, x, gain, dres],
        semantics=("arbitrary",),
    )


def _matmul_tn(a, b, stack, layer, name, n_total=None, col_off=0):
    s, k = a.shape
    n = b.shape[1]
    n_total = n if n_total is None else n_total
    tn = _divisor(math.gcd(n, col_off) if col_off else n, max(128, ACC_BYTES // (4 * k)), 128)
    ts = _divisor(s, min(2048, max(512, ACC_BYTES // (2 * k))), 16)
    off = col_off // tn

    def body(a_ref, b_ref, *rest):
        out_ref = rest[-1]

        @pl.when(pl.program_id(1) == 0)
        def _():
            out_ref[...] = jnp.zeros_like(out_ref)

        out_ref[...] += _dot_tn(a_ref[...], b_ref[...])

    in_specs = [pl.BlockSpec((ts, k), lambda j, i: (i, 0)), pl.BlockSpec((ts, tn), lambda j, i: (i, j))]
    args = [a, b]
    aliases = {}
    if stack is not None:
        in_specs.append(ANY)
        args.append(stack)
        aliases = {2: 0}
    return pl.pallas_call(
        body,
        name=name,
        grid=(n // tn, s // ts),
        in_specs=in_specs,
        out_specs=pl.BlockSpec((None, k, tn), lambda j, i: (layer, 0, off + j)),
        out_shape=SDS((DEPTH, k, n_total), F32),
        input_output_aliases=aliases,
        compiler_params=_params("parallel", "arbitrary"),
    )(*args)


def _gm_mask():
    r = lax.broadcasted_iota(jnp.int32, (GM_BLOCK, GM_BLOCK), 0) // CHUNK
    c = lax.broadcasted_iota(jnp.int32, (GM_BLOCK, GM_BLOCK), 1) // CHUNK
    return r >= c


def _gm_normed(v, lng, lnb):
    vg = _gelu(v)
    mu = jnp.mean(vg, axis=-1, keepdims=True)
    xc = vg - mu
    rstd = lax.rsqrt(jnp.mean(xc * xc, axis=-1, keepdims=True) + EPS)
    xh = xc * rstd
    return xh, rstd, xh * lng + lnb


def _gmlp_fwd(z, lng, lnb, ws, bst, name):
    s = z.shape[0]
    tg = _divisor(s, 256, GM_BLOCK)

    def body(z_ref, lng_ref, lnb_ref, ws_ref, bst_ref, a_ref):
        mask = _gm_mask()
        ug = _gelu(z_ref[:, :BR_DIM])
        _, _, vn = _gm_normed(z_ref[:, BR_DIM:], lng_ref[...], lnb_ref[...])
        vn = vn.astype(MM_DTYPE)
        for h in range(N_HEADS):
            cs = slice(h * HEAD_DIM, (h + 1) * HEAD_DIM)
            wm = jnp.where(mask, ws_ref[h], 0.0).astype(MM_DTYPE)
            for blk in range(tg // GM_BLOCK):
                rs = slice(blk * GM_BLOCK, (blk + 1) * GM_BLOCK)
                mixed = _dot(wm, vn[rs, cs]) + bst_ref[:, h : h + 1]
                a_ref[rs, cs] = (ug[rs, cs] * mixed).astype(MM_DTYPE)

    return pl.pallas_call(
        body,
        name=name,
        grid=(s // tg,),
        in_specs=[
            pl.BlockSpec((tg, 2 * BR_DIM), lambda i: (i, 0)),
            _resident((1, BR_DIM), lambda i: (0, 0)),
            _resident((1, BR_DIM), lambda i: (0, 0)),
            _resident((N_HEADS, GM_BLOCK, GM_BLOCK), lambda i: (0, 0, 0)),
            _resident((GM_BLOCK, N_HEADS), lambda i: (0, 0)),
        ],
        out_specs=pl.BlockSpec((tg, BR_DIM), lambda i: (i, 0)),
        out_shape=SDS((s, BR_DIM), MM_DTYPE),
        compiler_params=_params("parallel"),
    )(z, lng, lnb, ws, bst)


def _gmlp_bwd(da, z, lng, lnb, ws, bst, name):
    s = z.shape[0]
    tg = _divisor(s, 256, GM_BLOCK)

    def body(da_ref, z_ref, lng_ref, lnb_ref, ws_ref, bst_ref, dz_ref, dlng_ref, dlnb_ref, dws_ref, dbst_ref):
        @pl.when(pl.program_id(0) == 0)
        def _():
            dlng_ref[...] = jnp.zeros_like(dlng_ref)
            dlnb_ref[...] = jnp.zeros_like(dlnb_ref)
            dws_ref[...] = jnp.zeros_like(dws_ref)
            dbst_ref[...] = jnp.zeros_like(dbst_ref)

        mask = _gm_mask()
        u = z_ref[:, :BR_DIM]
        v = z_ref[:, BR_DIM:]
        ug = _gelu(u)
        lng_v = lng_ref[...]
        xh, rstd, vn = _gm_normed(v, lng_v, lnb_ref[...])
        vnb = vn.astype(MM_DTYPE)
        da_v = da_ref[...]
        dmix = da_v * ug
        dmixb = dmix.astype(MM_DTYPE)
        dvn_cols = []
        for h in range(N_HEADS):
            cs = slice(h * HEAD_DIM, (h + 1) * HEAD_DIM)
            wm = jnp.where(mask, ws_ref[h], 0.0).astype(MM_DTYPE)
            dw = jnp.zeros((GM_BLOCK, GM_BLOCK), F32)
            dbias = jnp.zeros((GM_BLOCK, 1), F32)
            dvn_rows = []
            for blk in range(tg // GM_BLOCK):
                rs = slice(blk * GM_BLOCK, (blk + 1) * GM_BLOCK)
                mixed = _dot(wm, vnb[rs, cs]) + bst_ref[:, h : h + 1]
                dz_ref[rs, cs] = (da_v[rs, cs] * mixed * _gelu_grad(u[rs, cs])).astype(MM_DTYPE)
                dw = dw + _dot_nt(dmixb[rs, cs], vnb[rs, cs])
                dbias = dbias + jnp.sum(dmix[rs, cs], axis=1, keepdims=True)
                dvn_rows.append(_dot_tn(wm, dmixb[rs, cs]))
            dws_ref[h] += jnp.where(mask, dw, 0.0)
            dbst_ref[:, h : h + 1] += dbias
            dvn_cols.append(jnp.concatenate(dvn_rows, axis=0) if len(dvn_rows) > 1 else dvn_rows[0])
        dvn = jnp.concatenate(dvn_cols, axis=1)
        dlng_ref[...] += jnp.sum(dvn * xh, axis=0, keepdims=True)
        dlnb_ref[...] += jnp.sum(dvn, axis=0, keepdims=True)
        dxh = dvn * lng_v
        dvg = rstd * (dxh - jnp.mean(dxh, axis=-1, keepdims=True) - xh * jnp.mean(dxh * xh, axis=-1, keepdims=True))
        dz_ref[:, BR_DIM:] = (dvg * _gelu_grad(v)).astype(MM_DTYPE)

    return pl.pallas_call(
        body,
        name=name,
        grid=(s // tg,),
        in_specs=[
            pl.BlockSpec((tg, BR_DIM), lambda i: (i, 0)),
            pl.BlockSpec((tg, 2 * BR_DIM), lambda i: (i, 0)),
            _resident((1, BR_DIM), lambda i: (0, 0)),
            _resident((1, BR_DIM), lambda i: (0, 0)),
            _resident((N_HEADS, GM_BLOCK, GM_BLOCK), lambda i: (0, 0, 0)),
            _resident((GM_BLOCK, N_HEADS), lambda i: (0, 0)),
        ],
        out_specs=[
            pl.BlockSpec((tg, 2 * BR_DIM), lambda i: (i, 0)),
            pl.BlockSpec((1, BR_DIM), lambda i: (0, 0)),
            pl.BlockSpec((1, BR_DIM), lambda i: (0, 0)),
            pl.BlockSpec((N_HEADS, GM_BLOCK, GM_BLOCK), lambda i: (0, 0, 0)),
            pl.BlockSpec((GM_BLOCK, N_HEADS), lambda i: (0, 0)),
        ],
        out_shape=[
            SDS((s, 2 * BR_DIM), MM_DTYPE),
            SDS((1, BR_DIM), F32),
            SDS((1, BR_DIM), F32),
            SDS((N_HEADS, GM_BLOCK, GM_BLOCK), F32),
            SDS((GM_BLOCK, N_HEADS), F32),
        ],
        compiler_params=_params("arbitrary"),
    )(da, z, lng, lnb, ws, bst)


HG_ROWS = 256
HG_UNROLL = 4
MID = CHUNK // 2 - 1


def _tri(lower):
    r = lax.broadcasted_iota(jnp.int32, (CHUNK, CHUNK), 0)
    c = lax.broadcasted_iota(jnp.int32, (CHUNK, CHUNK), 1)
    return r >= c if lower else c >= r


def _scan_rows(x, reverse=False):
    n = x.shape[0]
    rid = lax.broadcasted_iota(jnp.int32, x.shape, 0)
    k = 1
    while k < n:
        if reverse:
            x = x + jnp.where(rid < n - k, pltpu.roll(x, n - k, 0), 0.0)
        else:
            x = x + jnp.where(rid >= k, pltpu.roll(x, k, 0), 0.0)
        k *= 2
    return x


def _hgrn_fwd(z, lb, ng, name, rider=None):
    s = z.shape[0]
    rows_per = _divisor(s, HG_ROWS, CHUNK)
    n_sub = rows_per // CHUNK

    def body(zqf_ref, zio_ref, lb_ref, ng_ref, o_ref, b_ref, st_ref, state):
        @pl.when(pl.program_id(0) == 0)
        def _():
            state[...] = jnp.zeros_like(state)

        low = _tri(True)
        ng_v = ng_ref[...]

        def chunk(ci, carry):
            rows = pl.ds(pl.multiple_of(ci * CHUNK, CHUNK), CHUNK)
            for h in range(N_HEADS):
                cs = slice(h * HEAD_DIM, (h + 1) * HEAD_DIM)
                cs2 = slice(BR_DIM + h * HEAD_DIM, BR_DIM + (h + 1) * HEAD_DIM)
                zq = zqf_ref[rows, cs]
                zf = zqf_ref[rows, cs2]
                vi = zio_ref[rows, cs]
                zog = zio_ref[rows, cs2]
                lbh = lb_ref[:, cs]
                qf = zq * _sigmoid(zq)
                forget = lbh + (1.0 - lbh) * _sigmoid(zf)
                g = jnp.log(jnp.maximum(forget, TINY))
                k = (1.0 - lbh) * _sigmoid(-zf)
                gc = _scan_rows(g)
                gm = gc[MID : MID + 1]
                gl = gc[CHUNK - 1 : CHUNK]
                a = jnp.where(low, _dot_nt(qf * jnp.exp(gc - gm), k * jnp.exp(gm - gc)), 0.0)
                st = state[h]
                st_ref[ci, h] = st
                o = _dot(a, vi) + _dot_nt(qf * jnp.exp(gc), st)
                state[h] = st * jnp.exp(gl) + _dot_tn(vi, k * jnp.exp(gl - gc))
                r = lax.rsqrt(jnp.mean(o * o, axis=-1, keepdims=True) + EPS)
                o_ref[rows, cs] = o
                b_ref[rows, cs] = (((o * r) * ng_v) * (zog * _sigmoid(zog))).astype(MM_DTYPE)
            return carry

        lax.fori_loop(0, n_sub, chunk, 0, unroll=math.gcd(n_sub, HG_UNROLL))

    return _hosted_call(
        body,
        rider,
        name=name,
        grid=(s // rows_per,),
        in_specs=[
            pl.BlockSpec((rows_per, 2 * BR_DIM), lambda i: (i, 1)),
            pl.BlockSpec((rows_per, 2 * BR_DIM), lambda i: (i, 2)),
            _resident((1, BR_DIM), lambda i: (0, 0)),
            _resident((1, HEAD_DIM), lambda i: (0, 0)),
        ],
        out_specs=[
            pl.BlockSpec((rows_per, BR_DIM), lambda i: (i, 0)),
            pl.BlockSpec((rows_per, BR_DIM), lambda i: (i, 0)),
            pl.BlockSpec((n_sub, N_HEADS, HEAD_DIM, HEAD_DIM), lambda i: (i, 0, 0, 0)),
        ],
        out_shape=[
            SDS((s, BR_DIM), F32),
            SDS((s, BR_DIM), MM_DTYPE),
            SDS((s // CHUNK, N_HEADS, HEAD_DIM, HEAD_DIM), F32),
        ],
        scratch_shapes=[pltpu.VMEM((N_HEADS, HEAD_DIM, HEAD_DIM), F32)],
        args=[z, z, lb, ng],
        semantics=("arbitrary",),
    )


def _hgrn_bwd(db, z, o, states, lb, ng, name, rider=None):
    s = z.shape[0]
    rows_per = _divisor(s, HG_ROWS, CHUNK)
    n_sub = rows_per // CHUNK
    n_steps = s // rows_per

    def body(db_ref, zqf_ref, zio_ref, o_ref, st_ref, lb_ref, ng_ref, dz_ref, dlb_ref, dng_ref, dstate):
        @pl.when(pl.program_id(0) == 0)
        def _():
            dstate[...] = jnp.zeros_like(dstate)
            dlb_ref[...] = jnp.zeros_like(dlb_ref)
            dng_ref[...] = jnp.zeros_like(dng_ref)

        low = _tri(True)
        ng_v = ng_ref[...]

        def chunk(cc, carry):
            ci = n_sub - 1 - cc
            rows = pl.ds(pl.multiple_of(ci * CHUNK, CHUNK), CHUNK)
            for h in range(N_HEADS):
                cs = slice(h * HEAD_DIM, (h + 1) * HEAD_DIM)
                cs2 = slice(BR_DIM + h * HEAD_DIM, BR_DIM + (h + 1) * HEAD_DIM)
                zq = zqf_ref[rows, cs]
                zf = zqf_ref[rows, cs2]
                vi = zio_ref[rows, cs]
                zog = zio_ref[rows, cs2]
                lbh = lb_ref[:, cs]
                ov = o_ref[rows, cs]
                dbv = db_ref[rows, cs]
                r = lax.rsqrt(jnp.mean(ov * ov, axis=-1, keepdims=True) + EPS)
                on = ov * r
                sgo = _sigmoid(zog)
                gate = zog * sgo
                dng_ref[...] += jnp.sum(dbv * gate * on, axis=0, keepdims=True)
                don = dbv * gate * ng_v
                dzog = dbv * (on * ng_v) * _silu_grad(zog, sgo)
                do = r * (don - on * jnp.mean(don * on, axis=-1, keepdims=True))
                sq = _sigmoid(zq)
                qf = zq * sq
                sg = _sigmoid(zf)
                sgn = _sigmoid(-zf)
                oml = 1.0 - lbh
                forget = lbh + oml * sg
                fm = jnp.maximum(forget, TINY)
                k = oml * sgn
                gc = _scan_rows(jnp.log(fm))
                gm = gc[MID : MID + 1]
                gl = gc[CHUNK - 1 : CHUNK]
                e_g = jnp.exp(gc)
                e_q = jnp.exp(gc - gm)
                e_k = jnp.exp(gm - gc)
                e_kd = jnp.exp(gl - gc)
                e_gl = jnp.exp(gl)
                qt = qf * e_q
                kt = k * e_k
                a = jnp.where(low, _dot_nt(qt, kt), 0.0)
                st = st_ref[ci, h]
                dst = dstate[h]
                dp = jnp.where(low, _dot_nt(do, vi), 0.0)
                dv = _dot_tn(a, do) + _dot_nt(k * e_kd, dst)
                dq = _dot(dp, kt) * e_q + e_g * _dot(do, st)
                dks = e_kd * _dot(vi, dst)
                dk = _dot_tn(dp, qt) * e_k + dks
                dstate[h] = _dot_tn(do, qf * e_g) + dst * e_gl
                dgc = qf * dq - k * dk
                extra = e_gl * jnp.sum(st * dst, axis=0, keepdims=True) + jnp.sum(k * dks, axis=0, keepdims=True)
                dg = _scan_rows(dgc, reverse=True) + extra
                dforget = jnp.where(forget > TINY, dg / fm, 0.0)
                both = dforget - dk
                dlb_ref[:, cs] += jnp.sum(sgn * both, axis=0, keepdims=True)
                dz_ref[rows, cs] = (dq * _silu_grad(zq, sq)).astype(MM_DTYPE)
                dz_ref[rows, cs2] = (oml * sg * sgn * both).astype(MM_DTYPE)
                dz_ref[rows, 2 * BR_DIM + h * HEAD_DIM : 2 * BR_DIM + (h + 1) * HEAD_DIM] = dv.astype(MM_DTYPE)
                dz_ref[rows, 3 * BR_DIM + h * HEAD_DIM : 3 * BR_DIM + (h + 1) * HEAD_DIM] = dzog.astype(MM_DTYPE)
            return carry

        lax.fori_loop(0, n_sub, chunk, 0, unroll=math.gcd(n_sub, HG_UNROLL))

    rev = lambda i: n_steps - 1 - i
    return _hosted_call(
        body,
        rider,
        name=name,
        grid=(n_steps,),
        in_specs=[
            pl.BlockSpec((rows_per, BR_DIM), lambda i: (rev(i), 0)),
            pl.BlockSpec((rows_per, 2 * BR_DIM), lambda i: (rev(i), 1)),
            pl.BlockSpec((rows_per, 2 * BR_DIM), lambda i: (rev(i), 2)),
            pl.BlockSpec((rows_per, BR_DIM), lambda i: (rev(i), 0)),
            pl.BlockSpec((n_sub, N_HEADS, HEAD_DIM, HEAD_DIM), lambda i: (rev(i), 0, 0, 0)),
            _resident((1, BR_DIM), lambda i: (0, 0)),
            _resident((1, HEAD_DIM), lambda i: (0, 0)),
        ],
        out_specs=[
            pl.BlockSpec((rows_per, 4 * BR_DIM), lambda i: (rev(i), 0)),
            pl.BlockSpec((1, BR_DIM), lambda i: (0, 0)),
            pl.BlockSpec((1, HEAD_DIM), lambda i: (0, 0)),
        ],
        out_shape=[SDS((s, 4 * BR_DIM), MM_DTYPE), SDS((1, BR_DIM), F32), SDS((1, HEAD_DIM), F32)],
        scratch_shapes=[pltpu.VMEM((N_HEADS, HEAD_DIM, HEAD_DIM), F32)],
        args=[db, z, z, o, states, lb, ng],
        semantics=("arbitrary",),
    )


def _mix_fwd(x, a, b, z, w_gm, w_hg, w_out, name):
    s, d = x.shape
    tm = _divisor(s, 256, 16)
    g0 = 6 * BR_DIM // d

    def body(x_ref, a_ref, b_ref, ga_ref, gb_ref, wgm_ref, whg_ref, wout_ref, out_ref):
        y = _sigmoid(ga_ref[...]) * _dot(a_ref[...], wgm_ref[...]) + _sigmoid(gb_ref[...]) * _dot(b_ref[...], whg_ref[...])
        out_ref[...] = x_ref[...] + _dot(y, wout_ref[...])

    return pl.pallas_call(
        body,
        name=name,
        grid=(s // tm,),
        in_specs=[
            pl.BlockSpec((tm, d), lambda i: (i, 0)),
            pl.BlockSpec((tm, BR_DIM), lambda i: (i, 0)),
            pl.BlockSpec((tm, BR_DIM), lambda i: (i, 0)),
            pl.BlockSpec((tm, d), lambda i: (i, g0)),
            pl.BlockSpec((tm, d), lambda i: (i, g0 + 1)),
            _resident((BR_DIM, d), lambda i: (0, 0)),
            _resident((BR_DIM, d), lambda i: (0, 0)),
            _resident((d, d), lambda i: (0, 0)),
        ],
        out_specs=pl.BlockSpec((tm, d), lambda i: (i, 0)),
        out_shape=SDS((s, d), F32),
        compiler_params=_params("parallel"),
    )(x, a, b, z, z, w_gm, w_hg, w_out)


def _mix_bwd(dx, a, b, z, w_gm, w_hg, w_out, name, rider=None):
    s, d = dx.shape
    tm = _divisor(s, 256, 16)
    g0 = 6 * BR_DIM // d

    def body(dx_ref, a_ref, b_ref, ga_ref, gb_ref, wgm_ref, whg_ref, wout_ref, y_ref, dpa_ref, dpb_ref, da_ref, db_ref, dg_ref):
        dy = _dot_nt(dx_ref[...], wout_ref[...])
        pa = _dot(a_ref[...], wgm_ref[...])
        pb = _dot(b_ref[...], whg_ref[...])
        sa = _sigmoid(ga_ref[...])
        sb = _sigmoid(gb_ref[...])
        y_ref[...] = (sa * pa + sb * pb).astype(MM_DTYPE)
        dpa = (dy * sa).astype(MM_DTYPE)
        dpb = (dy * sb).astype(MM_DTYPE)
        dpa_ref[...] = dpa
        dpb_ref[...] = dpb
        da_ref[...] = _dot_nt(dpa, wgm_ref[...])
        db_ref[...] = _dot_nt(dpb, whg_ref[...])
        dg_ref[:, :d] = (dy * pa * sa * (1.0 - sa)).astype(MM_DTYPE)
        dg_ref[:, d:] = (dy * pb * sb * (1.0 - sb)).astype(MM_DTYPE)

    row = lambda w: pl.BlockSpec((tm, w), lambda i: (i, 0))
    return _hosted_call(
        body,
        rider,
        name=name,
        grid=(s // tm,),
        in_specs=[
            row(d),
            row(BR_DIM),
            row(BR_DIM),
            pl.BlockSpec((tm, d), lambda i: (i, g0)),
            pl.BlockSpec((tm, d), lambda i: (i, g0 + 1)),
            _resident((BR_DIM, d), lambda i: (0, 0)),
            _resident((BR_DIM, d), lambda i: (0, 0)),
            _resident((d, d), lambda i: (0, 0)),
        ],
        out_specs=[row(d), row(d), row(d), row(BR_DIM), row(BR_DIM), row(2 * d)],
        out_shape=[
            SDS((s, d), MM_DTYPE),
            SDS((s, d), MM_DTYPE),
            SDS((s, d), MM_DTYPE),
            SDS((s, BR_DIM), F32),
            SDS((s, BR_DIM), F32),
            SDS((s, 2 * d), MM_DTYPE),
        ],
        scratch_shapes=[],
        args=[dx, a, b, z, z, w_gm, w_hg, w_out],
        semantics=("arbitrary",),
    )


HALO = 8


def _shift_down(v, prev, n):
    rolled = pltpu.roll(v, n, 0)
    rid = lax.broadcasted_iota(jnp.int32, v.shape, 0)
    out = rolled
    for r in range(n):
        out = jnp.where(rid == r, prev[HALO - n + r : HALO - n + r + 1], out)
    return out


def _shift_up(v, nxt, n):
    tm = v.shape[0]
    rolled = pltpu.roll(v, tm - n, 0)
    rid = lax.broadcasted_iota(jnp.int32, v.shape, 0)
    out = rolled
    for r in range(n):
        out = jnp.where(rid == tm - n + r, nxt[r : r + 1], out)
    return out


def _conv_cols(f):
    return _divisor(f, 256, 128)


def _convffn_fwd(x, z2, cw, cb, w_down, layer, name):
    s, d = x.shape
    f = z2.shape[1] // 2
    tm = _divisor(s, 256, 16)
    cwid = _conv_cols(f)
    per8 = tm // HALO

    def body(x_ref, z_ref, prev_ref, cw_ref, cb_ref, wd_ref, out_ref, conv_ref):
        first = pl.program_id(0) == 0
        acc = x_ref[...]
        for c0 in range(0, f, cwid):
            halves = []
            for base in (c0, f + c0):
                cols = slice(base, base + cwid)
                zc = z_ref[:, cols]
                prev = jnp.where(first, 0.0, prev_ref[:, cols])
                conv = cb_ref[:, cols] + cw_ref[0:1, cols] * _shift_down(zc, prev, 2)
                conv = conv + cw_ref[1:2, cols] * _shift_down(zc, prev, 1) + cw_ref[2:3, cols] * zc
                conv_ref[:, cols] = conv.astype(MM_DTYPE)
                halves.append(conv)
            gate, val = halves
            act = gate * _sigmoid(gate) * val
            acc = acc + _dot(act, wd_ref[c0 : c0 + cwid, :])
        out_ref[...] = acc

    return pl.pallas_call(
        body,
        name=name,
        grid=(s // tm,),
        in_specs=[
            pl.BlockSpec((tm, d), lambda i: (i, 0)),
            pl.BlockSpec((tm, 2 * f), lambda i: (i, 0)),
            pl.BlockSpec((HALO, 2 * f), lambda i: (jnp.maximum(i * per8 - 1, 0), 0)),
            _resident((None, 3, 2 * f), lambda i: (layer, 0, 0)),
            _resident((1, 2 * f), lambda i: (0, 0)),
            _resident((f, d), lambda i: (0, 0)),
        ],
        out_specs=[pl.BlockSpec((tm, d), lambda i: (i, 0)), pl.BlockSpec((tm, 2 * f), lambda i: (i, 0))],
        out_shape=[SDS((s, d), F32), SDS((s, 2 * f), MM_DTYPE)],
        compiler_params=_params("arbitrary"),
    )(x, z2, z2, cw, cb, w_down)


def _convffn_bwd(dx, z2, conv, cw, w_down, layer, name):
    s, d = dx.shape
    f = z2.shape[1] // 2
    tm = _divisor(s, 256, 16)
    cwid = _conv_cols(f)
    n_steps = s // tm

    def body(dx_ref, z_ref, conv_ref, cw_ref, wd_ref, dz_ref, act_ref, dcw_ref, nxt):
        @pl.when(pl.program_id(0) == 0)
        def _():
            nxt[...] = jnp.zeros_like(nxt)
            dcw_ref[...] = jnp.zeros_like(dcw_ref)

        dxv = dx_ref[...].astype(MM_DTYPE)
        for c0 in range(0, f, cwid):
            dact = _dot_nt(dxv, wd_ref[c0 : c0 + cwid, :])
            gate = conv_ref[:, c0 : c0 + cwid].astype(F32)
            val = conv_ref[:, f + c0 : f + c0 + cwid].astype(F32)
            sg = _sigmoid(gate)
            silu = gate * sg
            act_ref[:, c0 : c0 + cwid] = (silu * val).astype(MM_DTYPE)
            dconvs = (dact * val * _silu_grad(gate, sg), dact * silu)
            for base, dconv in zip((c0, f + c0), dconvs):
                cols = slice(base, base + cwid)
                zc = z_ref[:, cols]
                nx = nxt[:, cols]
                up1 = _shift_up(dconv, nx, 1)
                up2 = _shift_up(dconv, nx, 2)
                dcw_ref[0:1, cols] += jnp.sum(up2 * zc, axis=0, keepdims=True)
                dcw_ref[1:2, cols] += jnp.sum(up1 * zc, axis=0, keepdims=True)
                dcw_ref[2:3, cols] += jnp.sum(dconv * zc, axis=0, keepdims=True)
                dcw_ref[3:4, cols] += jnp.sum(dconv, axis=0, keepdims=True)
                dz = cw_ref[2:3, cols] * dconv + cw_ref[1:2, cols] * up1 + cw_ref[0:1, cols] * up2
                dz_ref[:, cols] = dz.astype(MM_DTYPE)
                nxt[:, cols] = dconv[0:HALO]

    rev = lambda i: n_steps - 1 - i
    return pl.pallas_call(
        body,
        name=name,
        grid=(n_steps,),
        in_specs=[
            pl.BlockSpec((tm, d), lambda i: (rev(i), 0)),
            pl.BlockSpec((tm, 2 * f), lambda i: (rev(i), 0)),
            pl.BlockSpec((tm, 2 * f), lambda i: (rev(i), 0)),
            _resident((None, 3, 2 * f), lambda i: (layer, 0, 0)),
            _resident((f, d), lambda i: (0, 0)),
        ],
        out_specs=[
            pl.BlockSpec((tm, 2 * f), lambda i: (rev(i), 0)),
            pl.BlockSpec((tm, f), lambda i: (rev(i), 0)),
            pl.BlockSpec((HALO, 2 * f), lambda i: (0, 0)),
        ],
        out_shape=[SDS((s, 2 * f), MM_DTYPE), SDS((s, f), MM_DTYPE), SDS((HALO, 2 * f), F32)],
        scratch_shapes=[pltpu.VMEM((HALO, 2 * f), F32)],
        compiler_params=_params("arbitrary"),
    )(dx, z2, conv, cw, w_down)


def _loss_head(x, gain, target):
    s, d = x.shape
    tm = _divisor(s, 256, 8)

    def body(x_ref, g_ref, t_ref, loss_ref, dx_ref, dg_ref):
        @pl.when(pl.program_id(0) == 0)
        def _():
            loss_ref[...] = jnp.zeros_like(loss_ref)
            dg_ref[...] = jnp.zeros_like(dg_ref)

        xv = x_ref[...]
        gv = g_ref[...]
        r = lax.rsqrt(jnp.mean(xv * xv, axis=-1, keepdims=True) + EPS)
        xh = xv * r
        err = xh * gv - t_ref[...]
        loss_ref[...] += 0.5 * jnp.sum(jnp.mean(err * err, axis=-1, keepdims=True))
        dout = err * (1.0 / d)
        dg_ref[...] += jnp.sum(dout * xh, axis=0, keepdims=True)
        dxh = dout * gv
        dx_ref[...] = r * (dxh - xh * jnp.mean(dxh * xh, axis=-1, keepdims=True))

    return pl.pallas_call(
        body,
        name="loss_head",
        grid=(s // tm,),
        in_specs=[
            pl.BlockSpec((tm, d), lambda i: (i, 0)),
            _resident((1, d), lambda i: (0, 0)),
            pl.BlockSpec((tm, d), lambda i: (i, 0)),
        ],
        out_specs=[
            pl.BlockSpec((8, 128), lambda i: (0, 0)),
            pl.BlockSpec((tm, d), lambda i: (i, 0)),
            pl.BlockSpec((1, d), lambda i: (0, 0)),
        ],
        out_shape=[SDS((8, 128), F32), SDS((s, d), F32), SDS((1, d), F32)],
        compiler_params=_params("arbitrary"),
    )(x, gain, target)


def _adamw(w, g, m, v, name):
    shape = w.shape
    cols = shape[-1]
    rows = max(1, math.prod(shape[:-1]))
    tr = _divisor(rows, max(8, TILE_BYTES // (4 * cols)), 8)
    flat = [t.reshape(rows, cols) for t in (w, g, m, v)]

    def body(w_ref, g_ref, m_ref, v_ref, d_ref, nm_ref, nv_ref):
        gv = g_ref[...]
        m2 = ADAM_B1 * m_ref[...] + (1.0 - ADAM_B1) * gv
        v2 = ADAM_B2 * v_ref[...] + (1.0 - ADAM_B2) * (gv * gv)
        m_hat = m2 / (1.0 - ADAM_B1**ADAM_STEP)
        v_hat = v2 / (1.0 - ADAM_B2**ADAM_STEP)
        d_ref[...] = -ADAM_LR * (m_hat / (jnp.sqrt(v_hat) + ADAM_EPS) + ADAM_WD * w_ref[...])
        nm_ref[...] = m2
        nv_ref[...] = v2

    spec = pl.BlockSpec((tr, cols), lambda i: (i, 0))
    outs = pl.pallas_call(
        body,
        name=name,
        grid=(rows // tr,),
        in_specs=[spec] * 4,
        out_specs=[spec] * 3,
        out_shape=[SDS((rows, cols), F32)] * 3,
        compiler_params=_params("parallel"),
    )(*flat)
    return tuple(t.reshape(shape) for t in outs)


def _position():
    return lax.axis_index("x"), lax.axis_index("y"), lax.axis_index("c")


def _other_chips(x, y):
    return [(1 - x, y), (x, 1 - y), (1 - x, 1 - y)]


def _remote(src, dst, send_sem, recv_sem, device):
    return pltpu.make_async_remote_copy(
        src_ref=src, dst_ref=dst, send_sem=send_sem, recv_sem=recv_sem, device_id=device, device_id_type=MESH
    )


def _sub(ref, lead, shape, axis, chip=None, half=None):
    cut = [pl.ds(0, shape[0]), pl.ds(0, shape[1])]
    if chip is not None:
        size = shape[axis] // N_CHIPS
        cut[axis] = pl.ds(chip * size, size)
    if half is not None:
        size = shape[1 - axis] // 2
        cut[1 - axis] = pl.ds(half * size, size)
    return ref.at[(*lead, *cut)]


def _scaled(shape, axis, num, den=1):
    return tuple(d * num // den if i == axis else d for i, d in enumerate(shape))


def _gather_rider(shards, axes, layer, conv_w=None):
    n = len(shards)
    shard2d = [t.shape[1:] for t in shards]
    full2d = [_scaled(s2, ax, N_CHIPS) for s2, ax in zip(shard2d, axes)]
    out_shapes = [SDS(f2, t.dtype) for f2, t in zip(full2d, shards)]
    inputs = list(shards)
    own = 6
    scratch = [pltpu.SemaphoreType.DMA((n, 7)), pltpu.SemaphoreType.DMA((n, 7))]
    if conv_w is not None:
        cshape = conv_w.shape
        inputs.append(conv_w)
        out_shapes.append(SDS(_scaled(cshape, 2, N_CHIPS), conv_w.dtype))
        scratch += [pltpu.SemaphoreType.DMA((4,)), pltpu.SemaphoreType.DMA((4,))]

    def conv_slab(ref, chip):
        return ref.at[:, :, pl.ds((2 * chip[0] + chip[1]) * cshape[2], cshape[2])]

    def own_copy(ins, outs, send_sems, recv_sems, k):
        x, y, c = _position()
        slab = _sub(outs[k], (), full2d[k], axes[k], chip=2 * x + y)
        return _remote(ins[k].at[layer], slab, send_sems.at[k, own], recv_sems.at[k, own], (x, y, 1 - c))

    def start(ins, outs, scr):
        send_sems, recv_sems = scr[:2]
        x, y, c = _position()
        me = 2 * x + y
        for k in range(n):
            own_copy(ins, outs, send_sems, recv_sems, k).start()
            for j, chip in enumerate(_other_chips(x, y)):
                _remote(
                    _sub(ins[k], (layer,), shard2d[k], axes[k], half=c),
                    _sub(outs[k], (), full2d[k], axes[k], chip=me, half=c),
                    send_sems.at[k, j], recv_sems.at[k, j], (*chip, c),
                ).start()
        if conv_w is not None:
            csend, crecv = scr[2:]
            _remote(ins[n], conv_slab(outs[n], (x, y)), csend.at[3], crecv.at[3], (x, y, 1 - c)).start()
            for j, chip in enumerate(_other_chips(x, y)):
                _remote(ins[n], conv_slab(outs[n], (x, y)), csend.at[j], crecv.at[j], (*chip, c)).start()

    def finish(ins, outs, scr):
        send_sems, recv_sems = scr[:2]
        x, y, c = _position()
        me = 2 * x + y
        sibling = (x, y, 1 - c)
        chips = _other_chips(x, y)
        forwards = []
        for k in range(n):
            src = _sub(ins[k], (layer,), shard2d[k], axes[k], half=c)
            for j, chip in enumerate(chips):
                landed = _sub(outs[k], (), full2d[k], axes[k], chip=2 * chip[0] + chip[1], half=c)
                _remote(src, landed, send_sems.at[k, j], recv_sems.at[k, j], (*chip, c)).wait_recv()
                fwd = _remote(landed, landed, send_sems.at[k, 3 + j], recv_sems.at[k, 3 + j], sibling)
                fwd.start()
                forwards.append(fwd)
        for k in range(n):
            src = _sub(ins[k], (layer,), shard2d[k], axes[k], half=c)
            for j, chip in enumerate(chips):
                theirs = _sub(outs[k], (), full2d[k], axes[k], chip=2 * chip[0] + chip[1], half=1 - c)
                _remote(src, theirs, send_sems.at[k, 3 + j], recv_sems.at[k, 3 + j], sibling).wait_recv()
        for fwd in forwards:
            fwd.wait_send()
        for k in range(n):
            src = _sub(ins[k], (layer,), shard2d[k], axes[k], half=c)
            mine = _sub(outs[k], (), full2d[k], axes[k], chip=me, half=c)
            for j, chip in enumerate(chips):
                _remote(src, mine, send_sems.at[k, j], recv_sems.at[k, j], (*chip, c)).wait_send()
            own_copy(ins, outs, send_sems, recv_sems, k).wait()
        if conv_w is not None:
            csend, crecv = scr[2:]
            for j, chip in enumerate(chips):
                cp = _remote(ins[n], conv_slab(outs[n], chip), csend.at[j], crecv.at[j], (*chip, c))
                cp.wait_recv()
                cp.wait_send()
            _remote(ins[n], conv_slab(outs[n], (x, y)), csend.at[3], crecv.at[3], sibling).wait()

    return _Rider(inputs, out_shapes, scratch, start, finish)


def _pair_rider(stacks, axes, layer):
    n = len(stacks)
    shapes = [t.shape[1:] for t in stacks]
    out_shapes = [SDS(_scaled(s2, 1 - ax, 1, 2), F32) for s2, ax in zip(shapes, axes)]
    scratch = [pltpu.SemaphoreType.DMA((n,)), pltpu.SemaphoreType.DMA((n,))]

    def copies(ins, outs, scr):
        x, y, c = _position()
        return [
            _remote(_sub(ins[k], (layer,), shapes[k], axes[k], half=1 - c), outs[k], scr[0].at[k], scr[1].at[k], (x, y, 1 - c))
            for k in range(n)
        ]

    def start(ins, outs, scr):
        for cp in copies(ins, outs, scr):
            cp.start()

    def finish(ins, outs, scr):
        for cp in copies(ins, outs, scr):
            cp.wait()

    return _Rider(stacks, out_shapes, scratch, start, finish)


def _chip_rider(pairs, axes):
    n = len(pairs)
    shapes = [t.shape for t in pairs]
    out_shapes = [SDS((3,) + _scaled(s2, ax, 1, N_CHIPS), t.dtype) for s2, ax, t in zip(shapes, axes, pairs)]
    scratch = [pltpu.SemaphoreType.DMA((n, 3)), pltpu.SemaphoreType.DMA((n, 3))]

    def copies(ins, outs, scr):
        x, y, c = _position()
        return [
            _remote(
                _sub(ins[k], (), shapes[k], axes[k], chip=2 * chip[0] + chip[1]), outs[k].at[j],
                scr[0].at[k, j], scr[1].at[k, j], (*chip, c),
            )
            for k in range(n)
            for j, chip in enumerate(_other_chips(x, y))
        ]

    def start(ins, outs, scr):
        for cp in copies(ins, outs, scr):
            cp.start()

    def finish(ins, outs, scr):
        for cp in copies(ins, outs, scr):
            cp.wait()

    return _Rider(pairs, out_shapes, scratch, start, finish)


def _complete_rider(stacks, axes, layers):
    n = len(stacks)
    shapes = [t.shape[1:] for t in stacks]
    scratch = [pltpu.SemaphoreType.DMA((n,)), pltpu.SemaphoreType.DMA((n,))]

    def start(ins, outs, scr):
        x, y, c = _position()
        for k in range(n):
            mine = _sub(outs[k], (layers[k],), shapes[k], axes[k], half=c)
            _remote(mine, mine, scr[0].at[k], scr[1].at[k], (x, y, 1 - c)).start()

    def finish(ins, outs, scr):
        x, y, c = _position()
        for k in range(n):
            mine = _sub(outs[k], (layers[k],), shapes[k], axes[k], half=c)
            theirs = _sub(outs[k], (layers[k],), shapes[k], axes[k], half=1 - c)
            _remote(mine, theirs, scr[0].at[k], scr[1].at[k], (x, y, 1 - c)).wait_recv()
            _remote(mine, mine, scr[0].at[k], scr[1].at[k], (x, y, 1 - c)).wait_send()

    return _Rider(stacks, [SDS(t.shape, t.dtype) for t in stacks], scratch, start, finish, {k: k for k in range(n)})


def _small_rider(buf):
    rows, cols = buf.shape
    flips = [(fx, fy, fc) for fx in (0, 1) for fy in (0, 1) for fc in (0, 1)][1:]

    def peers():
        x, y, c = _position()
        return [(x + f[0] - 2 * x * f[0], y + f[1] - 2 * y * f[1], c + f[2] - 2 * c * f[2]) for f in flips]

    def start(ins, outs, scr):
        x, y, c = _position()
        mine = outs[0].at[4 * x + 2 * y + c]
        pltpu.make_async_copy(ins[0], mine, scr[2]).start()
        for j, peer in enumerate(peers()):
            _remote(ins[0], mine, scr[0].at[j], scr[1].at[j], peer).start()

    def finish(ins, outs, scr):
        x, y, c = _position()
        mine = outs[0].at[4 * x + 2 * y + c]
        for j, peer in enumerate(peers()):
            _remote(ins[0], outs[0].at[4 * peer[0] + 2 * peer[1] + peer[2]], scr[0].at[j], scr[1].at[j], peer).wait_recv()
        for j, peer in enumerate(peers()):
            _remote(ins[0], mine, scr[0].at[j], scr[1].at[j], peer).wait_send()
        pltpu.make_async_copy(ins[0], mine, scr[2]).wait()

    scratch = [pltpu.SemaphoreType.DMA((7,)), pltpu.SemaphoreType.DMA((7,)), pltpu.SemaphoreType.DMA(())]
    return _Rider([buf], [SDS((8, rows, cols), buf.dtype)], scratch, start, finish)


def _pair_sum(idx, stack, layer, recv, axis, name):
    hk, hn = recv.shape
    tk = _divisor(hk, max(16, TILE_BYTES // (4 * hn)), 16)
    per = hk // tk

    def body(idx_ref, g_ref, r_ref, out_ref):
        out_ref[...] = (g_ref[...] + r_ref[...]).astype(out_ref.dtype)

    if axis == 1:
        mine = pl.BlockSpec((None, tk, hn), lambda i, idx_ref: (layer, idx_ref[0] * per + i, 0))
    else:
        mine = pl.BlockSpec((None, tk, hn), lambda i, idx_ref: (layer, i, idx_ref[0]))
    return pl.pallas_call(
        body,
        name=name,
        grid_spec=pltpu.PrefetchScalarGridSpec(
            num_scalar_prefetch=1,
            grid=(per,),
            in_specs=[mine, pl.BlockSpec((tk, hn), lambda i, idx_ref: (i, 0))],
            out_specs=pl.BlockSpec((tk, hn), lambda i, idx_ref: (i, 0)),
        ),
        out_shape=SDS((hk, hn), BF16),
        compiler_params=_params("parallel"),
    )(idx, stack, recv)


def _chip_sum(idx, pair, others, axis, stack, layer, name):
    _, sk, sn = others.shape
    tk = _divisor(sk, max(16, TILE_BYTES // (4 * sn)), 16)
    per = sk // tk

    def body(idx_ref, p_ref, o0_ref, o1_ref, o2_ref, *rest):
        out_ref = rest[-1]
        acc = p_ref[...].astype(F32) + o0_ref[...].astype(F32)
        out_ref[...] = (acc + o1_ref[...].astype(F32)) + o2_ref[...].astype(F32)

    if axis == 1:
        own = pl.BlockSpec((tk, sn), lambda i, idx_ref: (i, idx_ref[1]))
        out = pl.BlockSpec((None, tk, sn), lambda i, idx_ref: (layer, idx_ref[0] * per + i, 0))
        shard = (2 * sk, sn)
    else:
        own = pl.BlockSpec((tk, sn), lambda i, idx_ref: (idx_ref[1] * per + i, 0))
        out = pl.BlockSpec((None, tk, sn), lambda i, idx_ref: (layer, i, idx_ref[0]))
        shard = (sk, 2 * sn)
    other = lambda j: pl.BlockSpec((None, tk, sn), lambda i, idx_ref: (j, i, 0))
    in_specs = [own, other(0), other(1), other(2)]
    args = [idx, pair, others, others, others]
    aliases = {}
    if stack is not None:
        in_specs.append(ANY)
        args.append(stack)
        aliases = {5: 0}
    return pl.pallas_call(
        body,
        name=name,
        grid_spec=pltpu.PrefetchScalarGridSpec(num_scalar_prefetch=1, grid=(per,), in_specs=in_specs, out_specs=out),
        out_shape=SDS((DEPTH,) + shard, F32),
        input_output_aliases=aliases,
        compiler_params=_params("parallel"),
    )(*args)


def _sum_devices(gathered, name):
    _, rows, cols = gathered.shape

    def body(g_ref, out_ref):
        acc = g_ref[0]
        for dev in range(1, 8):
            acc = acc + g_ref[dev]
        out_ref[...] = acc

    return pl.pallas_call(body, name=name, out_shape=SDS((rows, cols), F32))(gathered)


PACK_TILE = 8 * 128


def _pack(arrays):
    parts = []
    for t in arrays:
        flat = t.reshape(-1)
        pad = (-flat.shape[0]) % PACK_TILE
        if pad:
            flat = jnp.concatenate([flat, jnp.zeros((pad,), flat.dtype)])
        parts.append(flat.reshape(-1, 128))
    return jnp.concatenate(parts, axis=0)


def _unpack(buf, shapes):
    out, row = [], 0
    for shp in shapes:
        size = math.prod(shp)
        rows = -(-size // PACK_TILE) * 8
        out.append(buf[row : row + rows].reshape(-1)[:size].reshape(shp))
        row += rows
    return out


BIG = ("w_in", "w_br_gm", "w_br_hg", "w_out", "w_up", "w_down")
BIG_AXIS = (1, 1, 1, 0, 1, 0)
LAYER_SMALL = ("mix_norm", "gm_ln_g", "gm_ln_b", "gm_ws", "gm_bs", "lb", "hg_norm_g", "ffn_norm", "conv_w", "conv_b")
WEIGHTS = ("mix_norm", "w_in", "gm_ln_g", "gm_ln_b", "gm_ws", "gm_bs", "hg_lb_logits", "hg_norm_g", "w_br_gm", "w_br_hg", "w_out",
           "ffn_norm", "w_up", "conv_w", "conv_b", "w_down", "final_norm")


def kernel(x, mix_norm, w_in, gm_ln_g, gm_ln_b, gm_ws, gm_bs, hg_lb_logits, hg_norm_g, w_br_gm, w_br_hg, w_out, ffn_norm, w_up, conv_w, conv_b, w_down, final_norm, loss_target, m_mix_norm, m_w_in, m_gm_ln_g, m_gm_ln_b, m_gm_ws, m_gm_bs, m_hg_lb_logits, m_hg_norm_g, m_w_br_gm, m_w_br_hg, m_w_out, m_ffn_norm, m_w_up, m_conv_w, m_conv_b, m_w_down, m_final_norm, v_mix_norm, v_w_in, v_gm_ln_g, v_gm_ln_b, v_gm_ws, v_gm_bs, v_hg_lb_logits, v_hg_norm_g, v_w_br_gm, v_w_br_hg, v_w_out, v_ffn_norm, v_w_up, v_conv_w, v_conv_b, v_w_down, v_final_norm):
    w = dict(mix_norm=mix_norm, w_in=w_in, gm_ln_g=gm_ln_g, gm_ln_b=gm_ln_b, gm_ws=gm_ws, gm_bs=gm_bs, hg_lb_logits=hg_lb_logits,
             hg_norm_g=hg_norm_g, w_br_gm=w_br_gm, w_br_hg=w_br_hg, w_out=w_out, ffn_norm=ffn_norm, w_up=w_up, conv_w=conv_w,
             conv_b=conv_b, w_down=w_down, final_norm=final_norm)
    m = dict(mix_norm=m_mix_norm, w_in=m_w_in, gm_ln_g=m_gm_ln_g, gm_ln_b=m_gm_ln_b, gm_ws=m_gm_ws, gm_bs=m_gm_bs,
             hg_lb_logits=m_hg_lb_logits, hg_norm_g=m_hg_norm_g, w_br_gm=m_w_br_gm, w_br_hg=m_w_br_hg, w_out=m_w_out,
             ffn_norm=m_ffn_norm, w_up=m_w_up, conv_w=m_conv_w, conv_b=m_conv_b, w_down=m_w_down, final_norm=m_final_norm)
    v = dict(mix_norm=v_mix_norm, w_in=v_w_in, gm_ln_g=v_gm_ln_g, gm_ln_b=v_gm_ln_b, gm_ws=v_gm_ws, gm_bs=v_gm_bs,
             hg_lb_logits=v_hg_lb_logits, hg_norm_g=v_hg_norm_g, w_br_gm=v_w_br_gm, w_br_hg=v_w_br_hg, w_out=v_w_out,
             ffn_norm=v_ffn_norm, w_up=v_w_up, conv_w=v_conv_w, conv_b=v_conv_b, w_down=v_w_down, final_norm=v_final_norm)

    px, py, pc = _position()
    chip = 2 * px + py
    idx = jnp.stack([pc, chip]).astype(jnp.int32)
    axes = list(BIG_AXIS)
    d = x.shape[2]

    shards = [w[k].astype(MM_DTYPE) for k in BIG]
    w_in_0, conv_full = _run_rider(_gather_rider(shards[:1], axes[:1], 0, conv_w=conv_w), "gather_weights_0")
    full = [dict(w_in=w_in_0)]
    lb = _lower_bounds(hg_lb_logits)
    xs = x[0]
    saved = []
    mixer = 4
    for l in range(DEPTH):
        fw = full[l]
        more = l + 1 < DEPTH
        rider = _gather_rider(shards[:mixer], axes[:mixer], l + 1) if more else None
        (z, h), got_mixer = _norm_matmul(xs, mix_norm[l : l + 1], fw["w_in"], f"in_proj_{l}", rider)
        bst = gm_bs[l].T
        a = _gmlp_fwd(z, gm_ln_g[l : l + 1], gm_ln_b[l : l + 1], gm_ws[l], bst, f"gmlp_fwd_{l}")
        rider = _gather_rider(shards[1:], axes[1:], 0) if l == 0 else None
        (o, b, states), got = _hgrn_fwd(z, lb[l : l + 1], hg_norm_g[l : l + 1], f"hgrn_fwd_{l}", rider)
        if l == 0:
            fw.update(zip(BIG[1:], got))
        x1 = _mix_fwd(xs, a, b, z, fw["w_br_gm"], fw["w_br_hg"], fw["w_out"], f"mix_fwd_{l}")
        rider = _gather_rider(shards[mixer:], axes[mixer:], l + 1) if more else None
        (z2, h2), got_ffn = _norm_matmul(x1, ffn_norm[l : l + 1], fw["w_up"], f"up_proj_{l}", rider)
        if more:
            full.append(dict(zip(BIG, got_mixer + got_ffn)))
        x2, conv = _convffn_fwd(x1, z2, conv_full, conv_b[l : l + 1], fw["w_down"], l, f"convffn_fwd_{l}")
        saved.append((xs, h, z, a, b, o, states, x1, h2, z2, conv, bst))
        xs = x2

    loss_tile, dx, d_final = _loss_head(xs, final_norm.reshape(1, d), loss_target[0])
    loss = lax.psum(loss_tile[0, 0], ("x", "y", "c"))

    big = {k: None for k in BIG}
    grads = [None] * len(BIG)
    per_layer = [None] * DEPTH
    mix_ids, ffn_ids = list(range(mixer)), list(range(mixer, len(BIG)))
    mix_axes, ffn_axes = axes[:mixer], axes[mixer:]
    mix_pairs = None
    mix_summed = None

    def pair_sums(ids, received, layer):
        return [_pair_sum(idx, big[BIG[k]], layer, r, axes[k], f"pair_sum_{BIG[k]}_{layer}") for k, r in zip(ids, received)]

    def chip_sums(ids, pairs, others, layer):
        for k, p, ot in zip(ids, pairs, others):
            grads[k] = _chip_sum(idx, p, ot, axes[k], grads[k], layer, f"chip_sum_{BIG[k]}_{layer}")

    for l in reversed(range(DEPTH)):
        x0, h, z, a, b, o, states, x1, h2, z2, conv, bst = saved[l]
        fw = full[l]
        dz2, act, dconv = _convffn_bwd(dx, z2, conv, conv_full, fw["w_down"], l, f"convffn_bwd_{l}")
        big["w_down"] = _matmul_tn(act, dx, big["w_down"], l, f"dw_down_{l}")
        big["w_up"] = _matmul_tn(h2, dz2, big["w_up"], l, f"dw_up_{l}")
        rider = _pair_rider([big[BIG[k]] for k in ffn_ids], ffn_axes, l)
        if mix_pairs is not None:
            rider = _join(rider, _chip_rider(mix_pairs, mix_axes))
        (dx1, d_ffn), got = _proj_norm_bwd([dz2], fw["w_up"], x1, ffn_norm[l : l + 1], dx, f"up_proj_bwd_{l}", rider)
        if mix_pairs is not None:
            chip_sums(mix_ids, mix_pairs, got[len(ffn_ids) :], l + 1)
            mix_summed = l + 1
        ffn_pairs = pair_sums(ffn_ids, got[: len(ffn_ids)], l)
        rider = None
        if l == 0:
            upper = [_pack([per_layer[j][k] for k in LAYER_SMALL]) for j in range(1, DEPTH)] + [_pack([d_final[0]])]
            rider = _small_rider(jnp.concatenate(upper, axis=0))
        (y, dpa, dpb, da, db, dgates), gathered_upper = _mix_bwd(
            dx1, a, b, z, fw["w_br_gm"], fw["w_br_hg"], fw["w_out"], f"mix_bwd_{l}", rider
        )
        big["w_out"] = _matmul_tn(y, dx1, big["w_out"], l, f"dw_out_{l}")
        big["w_br_gm"] = _matmul_tn(a, dpa, big["w_br_gm"], l, f"dw_br_gm_{l}")
        big["w_br_hg"] = _matmul_tn(b, dpb, big["w_br_hg"], l, f"dw_br_hg_{l}")
        (dz_hg, d_lb, d_ng), others = _hgrn_bwd(
            db, z, o, states, lb[l : l + 1], hg_norm_g[l : l + 1], f"hgrn_bwd_{l}", _chip_rider(ffn_pairs, ffn_axes)
        )
        chip_sums(ffn_ids, ffn_pairs, others, l)
        dz_gm, d_lng, d_lnb, d_ws, d_bst = _gmlp_bwd(da, z, gm_ln_g[l : l + 1], gm_ln_b[l : l + 1], gm_ws[l], bst, f"gmlp_bwd_{l}")
        n_in = fw["w_in"].shape[1]
        big["w_in"] = _matmul_tn(h, dz_gm, big["w_in"], l, f"dw_in_gm_{l}", n_total=n_in, col_off=0)
        big["w_in"] = _matmul_tn(h, dz_hg, big["w_in"], l, f"dw_in_hg_{l}", n_total=n_in, col_off=2 * BR_DIM)
        big["w_in"] = _matmul_tn(h, dgates, big["w_in"], l, f"dw_in_gate_{l}", n_total=n_in, col_off=6 * BR_DIM)
        done_ids = ffn_ids + (mix_ids if mix_summed is not None else [])
        done_layers = [l] * len(ffn_ids) + ([mix_summed] * len(mix_ids) if mix_summed is not None else [])
        rider = _join(
            _pair_rider([big[BIG[k]] for k in mix_ids], mix_axes, l),
            _complete_rider([grads[k] for k in done_ids], [axes[k] for k in done_ids], done_layers),
        )
        (dx, d_mix), got = _proj_norm_bwd([dz_gm, dz_hg, dgates], fw["w_in"], x0, mix_norm[l : l + 1], dx1, f"in_proj_bwd_{l}", rider)
        for k, g in zip(done_ids, got[len(mix_ids) :]):
            grads[k] = g
        mix_pairs = pair_sums(mix_ids, got[: len(mix_ids)], l)
        per_layer[l] = dict(
            mix_norm=d_mix[0], gm_ln_g=d_lng[0], gm_ln_b=d_lnb[0], gm_ws=d_ws, gm_bs=d_bst.T, lb=d_lb[0], hg_norm_g=d_ng[0],
            ffn_norm=d_ffn[0], conv_w=dconv[0:3], conv_b=dconv[3],
        )

    got = _run_rider(
        _join(_chip_rider(mix_pairs, mix_axes), _small_rider(_pack([per_layer[0][k] for k in LAYER_SMALL]))), "grad_exchange_0"
    )
    chip_sums(mix_ids, mix_pairs, got[: len(mix_ids)], 0)
    done = _run_rider(_complete_rider([grads[k] for k in mix_ids], mix_axes, [0] * len(mix_ids)), "grad_complete_0")
    for k, g in zip(mix_ids, done):
        grads[k] = g
    grad = dict(zip(BIG, grads))

    layer_shapes = [per_layer[0][k].shape for k in LAYER_SMALL]
    summed_upper = _unpack(_sum_devices(gathered_upper[0], "sum_small_upper"), layer_shapes * (DEPTH - 1) + [d_final[0].shape])
    summed_0 = _unpack(_sum_devices(got[len(mix_ids)], "sum_small_0"), layer_shapes)
    by_layer = [summed_0] + [summed_upper[j * len(LAYER_SMALL) : (j + 1) * len(LAYER_SMALL)] for j in range(DEPTH - 1)]
    small = {k: jnp.stack([by_layer[j][i] for j in range(DEPTH)]) for i, k in enumerate(LAYER_SMALL)}
    for k in LAYER_SMALL:
        if k != "lb":
            grad[k] = small[k]
    grad["hg_lb_logits"] = _lower_bounds_bwd(hg_lb_logits, small["lb"])
    grad["final_norm"] = summed_upper[-1]
    grad["conv_w"] = lax.dynamic_slice_in_dim(grad["conv_w"], chip * conv_w.shape[2], conv_w.shape[2], axis=2)

    delta, new_m, new_v = {}, {}, {}
    for k in WEIGHTS:
        delta[k], new_m[k], new_v[k] = _adamw(w[k], grad[k], m[k], v[k], f"adamw_{k}")

    return (loss, dx[None], *[grad[k] for k in WEIGHTS], *[delta[k] for k in WEIGHTS], *[new_m[k] for k in WEIGHTS],
            *[new_v[k] for k in WEIGHTS])
```

```python
import math

import jax
import jax.numpy as jnp
from jax import lax
from jax.experimental import pallas as pl
from jax.experimental.pallas import tpu as pltpu

F32 = jnp.float32
BF16 = jnp.bfloat16
MM_DTYPE = BF16

N_HEADS = 4
HEAD_DIM = 128
BR_DIM = N_HEADS * HEAD_DIM
CHUNK = 64
GM_BLOCK = 128
DEPTH = 4
N_CHIPS = 4
EPS = 1e-6
TINY = 1e-30
LB_MAX = 0.999
ADAM_LR = 0.001
ADAM_B1 = 0.9
ADAM_B2 = 0.999
ADAM_EPS = 1e-08
ADAM_WD = 0.01
ADAM_STEP = 10
INV_SQRT2 = 1.0 / math.sqrt(2.0)
INV_SQRT_2PI = 1.0 / math.sqrt(2.0 * math.pi)

VMEM_LIMIT_BYTES = 56 * 1024 * 1024
TILE_BYTES = 2 * 1024 * 1024
ACC_BYTES = 6 * 1024 * 1024
MESH = pl.DeviceIdType.MESH
SDS = jax.ShapeDtypeStruct
ANY = pl.BlockSpec(memory_space=pl.ANY)


def _params(*sem):
    return pltpu.CompilerParams(dimension_semantics=sem or None, vmem_limit_bytes=VMEM_LIMIT_BYTES)


def _divisor(n, limit, mult):
    best = None
    for d in range(mult, min(n, limit) + 1, mult):
        if n % d == 0:
            best = d
    return best if best is not None else n


def _dot(a, b):
    return jnp.dot(a.astype(MM_DTYPE), b.astype(MM_DTYPE), preferred_element_type=F32)


def _dot_nt(a, b):
    return lax.dot_general(a.astype(MM_DTYPE), b.astype(MM_DTYPE), (((1,), (1,)), ((), ())), preferred_element_type=F32)


def _dot_tn(a, b):
    return lax.dot_general(a.astype(MM_DTYPE), b.astype(MM_DTYPE), (((0,), (0,)), ((), ())), preferred_element_type=F32)


def _sigmoid(x):
    return 1.0 / (1.0 + jnp.exp(-x))


def _gelu(x):
    return 0.5 * x * (1.0 + lax.erf(x * INV_SQRT2))


def _gelu_grad(x):
    return 0.5 * (1.0 + lax.erf(x * INV_SQRT2)) + x * jnp.exp(-0.5 * x * x) * INV_SQRT_2PI


def _silu_grad(x, s):
    return s * (1.0 + x * (1.0 - s))


def _resident(shape, index_map):
    return pl.BlockSpec(shape, index_map, pipeline_mode=pl.Buffered(1))


class _Rider:
    def __init__(self, inputs, out_shapes, scratch, start, finish, aliases=None):
        self.inputs = list(inputs)
        self.out_shapes = list(out_shapes)
        self.scratch = list(scratch)
        self.start = start
        self.finish = finish
        self.aliases = dict(aliases or {})


def _join(a, b):
    if a is None or b is None:
        return a if b is None else b
    na, oa, sa = len(a.inputs), len(a.out_shapes), len(a.scratch)

    def start(i, o, s):
        a.start(i[:na], o[:oa], s[:sa])
        b.start(i[na:], o[oa:], s[sa:])

    def finish(i, o, s):
        a.finish(i[:na], o[:oa], s[:sa])
        b.finish(i[na:], o[oa:], s[sa:])

    aliases = dict(a.aliases)
    aliases.update({na + k: oa + v for k, v in b.aliases.items()})
    return _Rider(a.inputs + b.inputs, a.out_shapes + b.out_shapes, a.scratch + b.scratch, start, finish, aliases)


def _hosted_call(body, rider, *, name, grid, in_specs, out_specs, out_shape, scratch_shapes, args, semantics):
    n_in, n_out, n_scr = len(in_specs), len(out_specs), len(scratch_shapes)
    if rider is None:
        outs = pl.pallas_call(
            body, name=name, grid=grid, in_specs=in_specs, out_specs=out_specs, out_shape=out_shape,
            scratch_shapes=scratch_shapes, compiler_params=_params(*semantics),
        )(*args)
        return list(outs), []
    ri, ro = len(rider.inputs), len(rider.out_shapes)

    def wrapped(*refs):
        p = 0
        hin = refs[p : p + n_in]
        p += n_in
        rin = refs[p : p + ri]
        p += ri
        hout = refs[p : p + n_out]
        p += n_out
        rout = refs[p : p + ro]
        p += ro
        hscr = refs[p : p + n_scr]
        rscr = refs[p + n_scr :]

        @pl.when(pl.program_id(0) == 0)
        def _():
            rider.start(rin, rout, rscr)

        body(*hin, *hout, *hscr)

        @pl.when(pl.program_id(0) == grid[0] - 1)
        def _():
            rider.finish(rin, rout, rscr)

    outs = pl.pallas_call(
        wrapped,
        name=name,
        grid=grid,
        in_specs=list(in_specs) + [ANY] * ri,
        out_specs=list(out_specs) + [ANY] * ro,
        out_shape=list(out_shape) + rider.out_shapes,
        scratch_shapes=list(scratch_shapes) + rider.scratch,
        input_output_aliases={n_in + k: n_out + v for k, v in rider.aliases.items()},
        compiler_params=_params(*semantics),
    )(*args, *rider.inputs)
    return list(outs[:n_out]), list(outs[n_out:])


def _run_rider(rider, name):
    ri, ro = len(rider.inputs), len(rider.out_shapes)

    def body(*refs):
        rin, rout, rscr = refs[:ri], refs[ri : ri + ro], refs[ri + ro :]
        rider.start(rin, rout, rscr)
        rider.finish(rin, rout, rscr)

    return list(
        pl.pallas_call(
            body,
            name=name,
            in_specs=[ANY] * ri,
            out_specs=[ANY] * ro,
            out_shape=rider.out_shapes,
            scratch_shapes=rider.scratch,
            input_output_aliases=rider.aliases,
        )(*rider.inputs)
    )


def _lower_bounds(logits):
    def body(lg_ref, lb_ref):
        lg = lg_ref[...]
        mx = jnp.max(lg, axis=0, keepdims=True)
        e = jnp.exp(lg - mx)
        p = e / jnp.sum(e, axis=0, keepdims=True)
        run = p[0:1]
        rows = [run - p[0:1]]
        for l in range(1, DEPTH):
            run = run + p[l : l + 1]
            rows.append(run - p[0:1])
        raw = jnp.concatenate(rows, axis=0)
        lb_ref[...] = jnp.minimum(jnp.maximum(raw, 0.0), LB_MAX)

    return pl.pallas_call(body, name="lower_bounds", out_shape=SDS(logits.shape, F32))(logits)


def _lower_bounds_bwd(logits, dlb):
    def body(lg_ref, dlb_ref, dlg_ref):
        lg = lg_ref[...]
        mx = jnp.max(lg, axis=0, keepdims=True)
        e = jnp.exp(lg - mx)
        p = e / jnp.sum(e, axis=0, keepdims=True)
        run = p[0:1]
        rows = [run - p[0:1]]
        for l in range(1, DEPTH):
            run = run + p[l : l + 1]
            rows.append(run - p[0:1])
        raw = jnp.concatenate(rows, axis=0)
        lo = jnp.where(raw > 0.0, 1.0, jnp.where(raw == 0.0, 0.5, 0.0))
        clipped = jnp.maximum(raw, 0.0)
        hi = jnp.where(clipped < LB_MAX, 1.0, jnp.where(clipped == LB_MAX, 0.5, 0.0))
        draw = dlb_ref[...] * lo * hi
        total = jnp.sum(draw, axis=0, keepdims=True)
        dps = []
        tail = total
        for j in range(DEPTH):
            dps.append(tail - total if j == 0 else tail)
            tail = tail - draw[j : j + 1]
        dp = jnp.concatenate(dps, axis=0)
        dlg_ref[...] = p * (dp - jnp.sum(p * dp, axis=0, keepdims=True))

    return pl.pallas_call(body, name="lower_bounds_bwd", out_shape=SDS(logits.shape, F32))(logits, dlb)


def _norm_matmul(x, gain, w, name, rider=None):
    s, d = x.shape
    n = w.shape[1]
    tm = _divisor(s, 512, 16)

    def body(x_ref, g_ref, w_ref, z_ref, h_ref):
        xv = x_ref[...]
        r = lax.rsqrt(jnp.mean(xv * xv, axis=-1, keepdims=True) + EPS)
        h = ((xv * r) * g_ref[...]).astype(MM_DTYPE)
        h_ref[...] = h
        z_ref[...] = jnp.dot(h, w_ref[...], preferred_element_type=F32)

    return _hosted_call(
        body,
        rider,
        name=name,
        grid=(s // tm,),
        in_specs=[
            pl.BlockSpec((tm, d), lambda i: (i, 0)),
            _resident((1, d), lambda i: (0, 0)),
            _resident((d, n), lambda i: (0, 0)),
        ],
        out_specs=[pl.BlockSpec((tm, n), lambda i: (i, 0)), pl.BlockSpec((tm, d), lambda i: (i, 0))],
        out_shape=[SDS((s, n), F32), SDS((s, d), MM_DTYPE)],
        scratch_shapes=[],
        args=[x, gain, w],
        semantics=("arbitrary",),
    )


def _proj_norm_bwd(dz_parts, w, x, gain, dres, name, rider=None):
    s, d = x.shape
    n = w.shape[1]
    tm = _divisor(s, 512, 16)
    widths = [p.shape[1] for p in dz_parts]
    offsets = [sum(widths[:k]) for k in range(len(widths))]
    n_parts = len(dz_parts)

    def body(*refs):
        dz_refs = refs[:n_parts]
        w_ref, x_ref, g_ref, dres_ref, dx_ref, dg_ref = refs[n_parts:]

        @pl.when(pl.program_id(0) == 0)
        def _():
            dg_ref[...] = jnp.zeros_like(dg_ref)

        dh = None
        for dz_ref, off, wd in zip(dz_refs, offsets, widths):
            part = _dot_nt(dz_ref[...], w_ref[:, off : off + wd])
            dh = part if dh is None else dh + part
        xv = x_ref[...]
        r = lax.rsqrt(jnp.mean(xv * xv, axis=-1, keepdims=True) + EPS)
        xh = xv * r
        dg_ref[...] += jnp.sum(dh * xh, axis=0, keepdims=True)
        dxh = dh * g_ref[...]
        dx_ref[...] = dres_ref[...] + r * (dxh - xh * jnp.mean(dxh * xh, axis=-1, keepdims=True))

    in_specs = [pl.BlockSpec((tm, wd), lambda i: (i, 0)) for wd in widths] + [
        _resident((d, n), lambda i: (0, 0)),
        pl.BlockSpec((tm, d), lambda i: (i, 0)),
        _resident((1, d), lambda i: (0, 0)),
        pl.BlockSpec((tm, d), lambda i: (i, 0)),
    ]
    return _hosted_call(
        body,
        rider,
        name=name,
        grid=(s // tm,),
        in_specs=in_specs,
        out_specs=[pl.BlockSpec((tm, d), lambda i: (i, 0)), pl.BlockSpec((1, d), lambda i: (0, 0))],
        out_shape=[SDS((s, d), F32), SDS((1, d), F32)],
        scratch_shapes=[],
        args=[*dz_parts, w, x, gain, dres],
        semantics=("arbitrary",),
    )


def _matmul_tn(a, b, stack, layer, name, n_total=None, col_off=0):
    s, k = a.shape
    n = b.shape[1]
    n_total = n if n_total is None else n_total
    tn = _divisor(math.gcd(n, col_off) if col_off else n, max(128, ACC_BYTES // (4 * k)), 128)
    ts = _divisor(s, min(2048, max(512, ACC_BYTES // (2 * k))), 16)
    off = col_off // tn

    def body(a_ref, b_ref, *rest):
        out_ref = rest[-1]

        @pl.when(pl.program_id(1) == 0)
        def _():
            out_ref[...] = jnp.zeros_like(out_ref)

        out_ref[...] += _dot_tn(a_ref[...], b_ref[...])

    in_specs = [pl.BlockSpec((ts, k), lambda j, i: (i, 0)), pl.BlockSpec((ts, tn), lambda j, i: (i, j))]
    args = [a, b]
    aliases = {}
    if stack is not None:
        in_specs.append(ANY)
        args.append(stack)
        aliases = {2: 0}
    return pl.pallas_call(
        body,
        name=name,
        grid=(n // tn, s // ts),
        in_specs=in_specs,
        out_specs=pl.BlockSpec((None, k, tn), lambda j, i: (layer, 0, off + j)),
        out_shape=SDS((DEPTH, k, n_total), F32),
        input_output_aliases=aliases,
        compiler_params=_params("parallel", "arbitrary"),
    )(*args)


def _gm_mask():
    r = lax.broadcasted_iota(jnp.int32, (GM_BLOCK, GM_BLOCK), 0) // CHUNK
    c = lax.broadcasted_iota(jnp.int32, (GM_BLOCK, GM_BLOCK), 1) // CHUNK
    return r >= c


def _gm_normed(v, lng, lnb):
    vg = _gelu(v)
    mu = jnp.mean(vg, axis=-1, keepdims=True)
    xc = vg - mu
    rstd = lax.rsqrt(jnp.mean(xc * xc, axis=-1, keepdims=True) + EPS)
    xh = xc * rstd
    return xh, rstd, xh * lng + lnb


def _gmlp_fwd(z, lng, lnb, ws, bst, name):
    s = z.shape[0]
    tg = _divisor(s, 256, GM_BLOCK)

    def body(z_ref, lng_ref, lnb_ref, ws_ref, bst_ref, a_ref):
        mask = _gm_mask()
        ug = _gelu(z_ref[:, :BR_DIM])
        _, _, vn = _gm_normed(z_ref[:, BR_DIM:], lng_ref[...], lnb_ref[...])
        vn = vn.astype(MM_DTYPE)
        for h in range(N_HEADS):
            cs = slice(h * HEAD_DIM, (h + 1) * HEAD_DIM)
            wm = jnp.where(mask, ws_ref[h], 0.0).astype(MM_DTYPE)
            for blk in range(tg // GM_BLOCK):
                rs = slice(blk * GM_BLOCK, (blk + 1) * GM_BLOCK)
                mixed = _dot(wm, vn[rs, cs]) + bst_ref[:, h : h + 1]
                a_ref[rs, cs] = (ug[rs, cs] * mixed).astype(MM_DTYPE)

    return pl.pallas_call(
        body,
        name=name,
        grid=(s // tg,),
        in_specs=[
            pl.BlockSpec((tg, 2 * BR_DIM), lambda i: (i, 0)),
            _resident((1, BR_DIM), lambda i: (0, 0)),
            _resident((1, BR_DIM), lambda i: (0, 0)),
            _resident((N_HEADS, GM_BLOCK, GM_BLOCK), lambda i: (0, 0, 0)),
            _resident((GM_BLOCK, N_HEADS), lambda i: (0, 0)),
        ],
        out_specs=pl.BlockSpec((tg, BR_DIM), lambda i: (i, 0)),
        out_shape=SDS((s, BR_DIM), MM_DTYPE),
        compiler_params=_params("parallel"),
    )(z, lng, lnb, ws, bst)


def _gmlp_bwd(da, z, lng, lnb, ws, bst, name):
    s = z.shape[0]
    tg = _divisor(s, 256, GM_BLOCK)

    def body(da_ref, z_ref, lng_ref, lnb_ref, ws_ref, bst_ref, dz_ref, dlng_ref, dlnb_ref, dws_ref, dbst_ref):
        @pl.when(pl.program_id(0) == 0)
        def _():
            dlng_ref[...] = jnp.zeros_like(dlng_ref)
            dlnb_ref[...] = jnp.zeros_like(dlnb_ref)
            dws_ref[...] = jnp.zeros_like(dws_ref)
            dbst_ref[...] = jnp.zeros_like(dbst_ref)

        mask = _gm_mask()
        u = z_ref[:, :BR_DIM]
        v = z_ref[:, BR_DIM:]
        ug = _gelu(u)
        lng_v = lng_ref[...]
        xh, rstd, vn = _gm_normed(v, lng_v, lnb_ref[...])
        vnb = vn.astype(MM_DTYPE)
        da_v = da_ref[...]
        dmix = da_v * ug
        dmixb = dmix.astype(MM_DTYPE)
        dvn_cols = []
        for h in range(N_HEADS):
            cs = slice(h * HEAD_DIM, (h + 1) * HEAD_DIM)
            wm = jnp.where(mask, ws_ref[h], 0.0).astype(MM_DTYPE)
            dw = jnp.zeros((GM_BLOCK, GM_BLOCK), F32)
            dbias = jnp.zeros((GM_BLOCK, 1), F32)
            dvn_rows = []
            for blk in range(tg // GM_BLOCK):
                rs = slice(blk * GM_BLOCK, (blk + 1) * GM_BLOCK)
                mixed = _dot(wm, vnb[rs, cs]) + bst_ref[:, h : h + 1]
                dz_ref[rs, cs] = (da_v[rs, cs] * mixed * _gelu_grad(u[rs, cs])).astype(MM_DTYPE)
                dw = dw + _dot_nt(dmixb[rs, cs], vnb[rs, cs])
                dbias = dbias + jnp.sum(dmix[rs, cs], axis=1, keepdims=True)
                dvn_rows.append(_dot_tn(wm, dmixb[rs, cs]))
            dws_ref[h] += jnp.where(mask, dw, 0.0)
            dbst_ref[:, h : h + 1] += dbias
            dvn_cols.append(jnp.concatenate(dvn_rows, axis=0) if len(dvn_rows) > 1 else dvn_rows[0])
        dvn = jnp.concatenate(dvn_cols, axis=1)
        dlng_ref[...] += jnp.sum(dvn * xh, axis=0, keepdims=True)
        dlnb_ref[...] += jnp.sum(dvn, axis=0, keepdims=True)
        dxh = dvn * lng_v
        dvg = rstd * (dxh - jnp.mean(dxh, axis=-1, keepdims=True) - xh * jnp.mean(dxh * xh, axis=-1, keepdims=True))
        dz_ref[:, BR_DIM:] = (dvg * _gelu_grad(v)).astype(MM_DTYPE)

    return pl.pallas_call(
        body,
        name=name,
        grid=(s // tg,),
        in_specs=[
            pl.BlockSpec((tg, BR_DIM), lambda i: (i, 0)),
            pl.BlockSpec((tg, 2 * BR_DIM), lambda i: (i, 0)),
            _resident((1, BR_DIM), lambda i: (0, 0)),
            _resident((1, BR_DIM), lambda i: (0, 0)),
            _resident((N_HEADS, GM_BLOCK, GM_BLOCK), lambda i: (0, 0, 0)),
            _resident((GM_BLOCK, N_HEADS), lambda i: (0, 0)),
        ],
        out_specs=[
            pl.BlockSpec((tg, 2 * BR_DIM), lambda i: (i, 0)),
            pl.BlockSpec((1, BR_DIM), lambda i: (0, 0)),
            pl.BlockSpec((1, BR_DIM), lambda i: (0, 0)),
            pl.BlockSpec((N_HEADS, GM_BLOCK, GM_BLOCK), lambda i: (0, 0, 0)),
            pl.BlockSpec((GM_BLOCK, N_HEADS), lambda i: (0, 0)),
        ],
        out_shape=[
            SDS((s, 2 * BR_DIM), MM_DTYPE),
            SDS((1, BR_DIM), F32),
            SDS((1, BR_DIM), F32),
            SDS((N_HEADS, GM_BLOCK, GM_BLOCK), F32),
            SDS((GM_BLOCK, N_HEADS), F32),
        ],
        compiler_params=_params("arbitrary"),
    )(da, z, lng, lnb, ws, bst)


HG_ROWS = 256
HG_UNROLL = 4
MID = CHUNK // 2 - 1


def _tri(lower):
    r = lax.broadcasted_iota(jnp.int32, (CHUNK, CHUNK), 0)
    c = lax.broadcasted_iota(jnp.int32, (CHUNK, CHUNK), 1)
    return r >= c if lower else c >= r


def _scan_rows(x, reverse=False):
    n = x.shape[0]
    rid = lax.broadcasted_iota(jnp.int32, x.shape, 0)
    k = 1
    while k < n:
        if reverse:
            x = x + jnp.where(rid < n - k, pltpu.roll(x, n - k, 0), 0.0)
        else:
            x = x + jnp.where(rid >= k, pltpu.roll(x, k, 0), 0.0)
        k *= 2
    return x


def _hgrn_fwd(z, lb, ng, name, rider=None):
    s = z.shape[0]
    rows_per = _divisor(s, HG_ROWS, CHUNK)
    n_sub = rows_per // CHUNK

    def body(zqf_ref, zio_ref, lb_ref, ng_ref, o_ref, b_ref, st_ref, state):
        @pl.when(pl.program_id(0) == 0)
        def _():
            state[...] = jnp.zeros_like(state)

        low = _tri(True)
        ng_v = ng_ref[...]

        def chunk(ci, carry):
            rows = pl.ds(pl.multiple_of(ci * CHUNK, CHUNK), CHUNK)
            for h in range(N_HEADS):
                cs = slice(h * HEAD_DIM, (h + 1) * HEAD_DIM)
                cs2 = slice(BR_DIM + h * HEAD_DIM, BR_DIM + (h + 1) * HEAD_DIM)
                zq = zqf_ref[rows, cs]
                zf = zqf_ref[rows, cs2]
                vi = zio_ref[rows, cs]
                zog = zio_ref[rows, cs2]
                lbh = lb_ref[:, cs]
                qf = zq * _sigmoid(zq)
                forget = lbh + (1.0 - lbh) * _sigmoid(zf)
                g = jnp.log(jnp.maximum(forget, TINY))
                k = (1.0 - lbh) * _sigmoid(-zf)
                gc = _scan_rows(g)
                gm = gc[MID : MID + 1]
                gl = gc[CHUNK - 1 : CHUNK]
                a = jnp.where(low, _dot_nt(qf * jnp.exp(gc - gm), k * jnp.exp(gm - gc)), 0.0)
                st = state[h]
                st_ref[ci, h] = st
                o = _dot(a, vi) + _dot_nt(qf * jnp.exp(gc), st)
                state[h] = st * jnp.exp(gl) + _dot_tn(vi, k * jnp.exp(gl - gc))
                r = lax.rsqrt(jnp.mean(o * o, axis=-1, keepdims=True) + EPS)
                o_ref[rows, cs] = o
                b_ref[rows, cs] = (((o * r) * ng_v) * (zog * _sigmoid(zog))).astype(MM_DTYPE)
            return carry

        lax.fori_loop(0, n_sub, chunk, 0, unroll=math.gcd(n_sub, HG_UNROLL))

    return _hosted_call(
        body,
        rider,
        name=name,
        grid=(s // rows_per,),
        in_specs=[
            pl.BlockSpec((rows_per, 2 * BR_DIM), lambda i: (i, 1)),
            pl.BlockSpec((rows_per, 2 * BR_DIM), lambda i: (i, 2)),
            _resident((1, BR_DIM), lambda i: (0, 0)),
            _resident((1, HEAD_DIM), lambda i: (0, 0)),
        ],
        out_specs=[
            pl.BlockSpec((rows_per, BR_DIM), lambda i: (i, 0)),
            pl.BlockSpec((rows_per, BR_DIM), lambda i: (i, 0)),
            pl.BlockSpec((n_sub, N_HEADS, HEAD_DIM, HEAD_DIM), lambda i: (i, 0, 0, 0)),
        ],
        out_shape=[
            SDS((s, BR_DIM), F32),
            SDS((s, BR_DIM), MM_DTYPE),
            SDS((s // CHUNK, N_HEADS, HEAD_DIM, HEAD_DIM), F32),
        ],
        scratch_shapes=[pltpu.VMEM((N_HEADS, HEAD_DIM, HEAD_DIM), F32)],
        args=[z, z, lb, ng],
        semantics=("arbitrary",),
    )


def _hgrn_bwd(db, z, o, states, lb, ng, name, rider=None):
    s = z.shape[0]
    rows_per = _divisor(s, HG_ROWS, CHUNK)
    n_sub = rows_per // CHUNK
    n_steps = s // rows_per

    def body(db_ref, zqf_ref, zio_ref, o_ref, st_ref, lb_ref, ng_ref, dz_ref, dlb_ref, dng_ref, dstate):
        @pl.when(pl.program_id(0) == 0)
        def _():
            dstate[...] = jnp.zeros_like(dstate)
            dlb_ref[...] = jnp.zeros_like(dlb_ref)
            dng_ref[...] = jnp.zeros_like(dng_ref)

        low = _tri(True)
        ng_v = ng_ref[...]

        def chunk(cc, carry):
            ci = n_sub - 1 - cc
            rows = pl.ds(pl.multiple_of(ci * CHUNK, CHUNK), CHUNK)
            for h in range(N_HEADS):
                cs = slice(h * HEAD_DIM, (h + 1) * HEAD_DIM)
                cs2 = slice(BR_DIM + h * HEAD_DIM, BR_DIM + (h + 1) * HEAD_DIM)
                zq = zqf_ref[rows, cs]
                zf = zqf_ref[rows, cs2]
                vi = zio_ref[rows, cs]
                zog = zio_ref[rows, cs2]
                lbh = lb_ref[:, cs]
                ov = o_ref[rows, cs]
                dbv = db_ref[rows, cs]
                r = lax.rsqrt(jnp.mean(ov * ov, axis=-1, keepdims=True) + EPS)
                on = ov * r
                sgo = _sigmoid(zog)
                gate = zog * sgo
                dng_ref[...] += jnp.sum(dbv * gate * on, axis=0, keepdims=True)
                don = dbv * gate * ng_v
                dzog = dbv * (on * ng_v) * _silu_grad(zog, sgo)
                do = r * (don - on * jnp.mean(don * on, axis=-1, keepdims=True))
                sq = _sigmoid(zq)
                qf = zq * sq
                sg = _sigmoid(zf)
                sgn = _sigmoid(-zf)
                oml = 1.0 - lbh
                forget = lbh + oml * sg
                fm = jnp.maximum(forget, TINY)
                k = oml * sgn
                gc = _scan_rows(jnp.log(fm))
                gm = gc[MID : MID + 1]
                gl = gc[CHUNK - 1 : CHUNK]
                e_g = jnp.exp(gc)
                e_q = jnp.exp(gc - gm)
                e_k = jnp.exp(gm - gc)
                e_kd = jnp.exp(gl - gc)
                e_gl = jnp.exp(gl)
                qt = qf * e_q
                kt = k * e_k
                a = jnp.where(low, _dot_nt(qt, kt), 0.0)
                st = st_ref[ci, h]
                dst = dstate[h]
                dp = jnp.where(low, _dot_nt(do, vi), 0.0)
                dv = _dot_tn(a, do) + _dot_nt(k * e_kd, dst)
                dq = _dot(dp, kt) * e_q + e_g * _dot(do, st)
                dks = e_kd * _dot(vi, dst)
                dk = _dot_tn(dp, qt) * e_k + dks
                dstate[h] = _dot_tn(do, qf * e_g) + dst * e_gl
                dgc = qf * dq - k * dk
                extra = e_gl * jnp.sum(st * dst, axis=0, keepdims=True) + jnp.sum(k * dks, axis=0, keepdims=True)
                dg = _scan_rows(dgc, reverse=True) + extra
                dforget = jnp.where(forget > TINY, dg / fm, 0.0)
                both = dforget - dk
                dlb_ref[:, cs] += jnp.sum(sgn * both, axis=0, keepdims=True)
                dz_ref[rows, cs] = (dq * _silu_grad(zq, sq)).astype(MM_DTYPE)
                dz_ref[rows, cs2] = (oml * sg * sgn * both).astype(MM_DTYPE)
                dz_ref[rows, 2 * BR_DIM + h * HEAD_DIM : 2 * BR_DIM + (h + 1) * HEAD_DIM] = dv.astype(MM_DTYPE)
                dz_ref[rows, 3 * BR_DIM + h * HEAD_DIM : 3 * BR_DIM + (h + 1) * HEAD_DIM] = dzog.astype(MM_DTYPE)
            return carry

        lax.fori_loop(0, n_sub, chunk, 0, unroll=math.gcd(n_sub, HG_UNROLL))

    rev = lambda i: n_steps - 1 - i
    return _hosted_call(
        body,
        rider,
        name=name,
        grid=(n_steps,),
        in_specs=[
            pl.BlockSpec((rows_per, BR_DIM), lambda i: (rev(i), 0)),
            pl.BlockSpec((rows_per, 2 * BR_DIM), lambda i: (rev(i), 1)),
            pl.BlockSpec((rows_per, 2 * BR_DIM), lambda i: (rev(i), 2)),
            pl.BlockSpec((rows_per, BR_DIM), lambda i: (rev(i), 0)),
            pl.BlockSpec((n_sub, N_HEADS, HEAD_DIM, HEAD_DIM), lambda i: (rev(i), 0, 0, 0)),
            _resident((1, BR_DIM), lambda i: (0, 0)),
            _resident((1, HEAD_DIM), lambda i: (0, 0)),
        ],
        out_specs=[
            pl.BlockSpec((rows_per, 4 * BR_DIM), lambda i: (rev(i), 0)),
            pl.BlockSpec((1, BR_DIM), lambda i: (0, 0)),
            pl.BlockSpec((1, HEAD_DIM), lambda i: (0, 0)),
        ],
        out_shape=[SDS((s, 4 * BR_DIM), MM_DTYPE), SDS((1, BR_DIM), F32), SDS((1, HEAD_DIM), F32)],
        scratch_shapes=[pltpu.VMEM((N_HEADS, HEAD_DIM, HEAD_DIM), F32)],
        args=[db, z, z, o, states, lb, ng],
        semantics=("arbitrary",),
    )


def _mix_fwd(x, a, b, z, w_gm, w_hg, w_out, name):
    s, d = x.shape
    tm = _divisor(s, 256, 16)
    g0 = 6 * BR_DIM // d

    def body(x_ref, a_ref, b_ref, ga_ref, gb_ref, wgm_ref, whg_ref, wout_ref, out_ref):
        y = _sigmoid(ga_ref[...]) * _dot(a_ref[...], wgm_ref[...]) + _sigmoid(gb_ref[...]) * _dot(b_ref[...], whg_ref[...])
        out_ref[...] = x_ref[...] + _dot(y, wout_ref[...])

    return pl.pallas_call(
        body,
        name=name,
        grid=(s // tm,),
        in_specs=[
            pl.BlockSpec((tm, d), lambda i: (i, 0)),
            pl.BlockSpec((tm, BR_DIM), lambda i: (i, 0)),
            pl.BlockSpec((tm, BR_DIM), lambda i: (i, 0)),
            pl.BlockSpec((tm, d), lambda i: (i, g0)),
            pl.BlockSpec((tm, d), lambda i: (i, g0 + 1)),
            _resident((BR_DIM, d), lambda i: (0, 0)),
            _resident((BR_DIM, d), lambda i: (0, 0)),
            _resident((d, d), lambda i: (0, 0)),
        ],
        out_specs=pl.BlockSpec((tm, d), lambda i: (i, 0)),
        out_shape=SDS((s, d), F32),
        compiler_params=_params("parallel"),
    )(x, a, b, z, z, w_gm, w_hg, w_out)


def _mix_bwd(dx, a, b, z, w_gm, w_hg, w_out, name, rider=None):
    s, d = dx.shape
    tm = _divisor(s, 256, 16)
    g0 = 6 * BR_DIM // d

    def body(dx_ref, a_ref, b_ref, ga_ref, gb_ref, wgm_ref, whg_ref, wout_ref, y_ref, dpa_ref, dpb_ref, da_ref, db_ref, dg_ref):
        dy = _dot_nt(dx_ref[...], wout_ref[...])
        pa = _dot(a_ref[...], wgm_ref[...])
        pb = _dot(b_ref[...], whg_ref[...])
        sa = _sigmoid(ga_ref[...])
        sb = _sigmoid(gb_ref[...])
        y_ref[...] = (sa * pa + sb * pb).astype(MM_DTYPE)
        dpa = (dy * sa).astype(MM_DTYPE)
        dpb = (dy * sb).astype(MM_DTYPE)
        dpa_ref[...] = dpa
        dpb_ref[...] = dpb
        da_ref[...] = _dot_nt(dpa, wgm_ref[...])
        db_ref[...] = _dot_nt(dpb, whg_ref[...])
        dg_ref[:, :d] = (dy * pa * sa * (1.0 - sa)).astype(MM_DTYPE)
        dg_ref[:, d:] = (dy * pb * sb * (1.0 - sb)).astype(MM_DTYPE)

    row = lambda w: pl.BlockSpec((tm, w), lambda i: (i, 0))
    return _hosted_call(
        body,
        rider,
        name=name,
        grid=(s // tm,),
        in_specs=[
            row(d),
            row(BR_DIM),
            row(BR_DIM),
            pl.BlockSpec((tm, d), lambda i: (i, g0)),
            pl.BlockSpec((tm, d), lambda i: (i, g0 + 1)),
            _resident((BR_DIM, d), lambda i: (0, 0)),
            _resident((BR_DIM, d), lambda i: (0, 0)),
            _resident((d, d), lambda i: (0, 0)),
        ],
        out_specs=[row(d), row(d), row(d), row(BR_DIM), row(BR_DIM), row(2 * d)],
        out_shape=[
            SDS((s, d), MM_DTYPE),
            SDS((s, d), MM_DTYPE),
            SDS((s, d), MM_DTYPE),
            SDS((s, BR_DIM), F32),
            SDS((s, BR_DIM), F32),
            SDS((s, 2 * d), MM_DTYPE),
        ],
        scratch_shapes=[],
        args=[dx, a, b, z, z, w_gm, w_hg, w_out],
        semantics=("arbitrary",),
    )


HALO = 8


def _shift_down(v, prev, n):
    rolled = pltpu.roll(v, n, 0)
    rid = lax.broadcasted_iota(jnp.int32, v.shape, 0)
    out = rolled
    for r in range(n):
        out = jnp.where(rid == r, prev[HALO - n + r : HALO - n + r + 1], out)
    return out


def _shift_up(v, nxt, n):
    tm = v.shape[0]
    rolled = pltpu.roll(v, tm - n, 0)
    rid = lax.broadcasted_iota(jnp.int32, v.shape, 0)
    out = rolled
    for r in range(n):
        out = jnp.where(rid == tm - n + r, nxt[r : r + 1], out)
    return out


def _conv_cols(f):
    return _divisor(f, 256, 128)


def _convffn_fwd(x, z2, cw, cb, w_down, layer, name):
    s, d = x.shape
    f = z2.shape[1] // 2
    tm = _divisor(s, 256, 16)
    cwid = _conv_cols(f)
    per8 = tm // HALO

    def body(x_ref, z_ref, prev_ref, cw_ref, cb_ref, wd_ref, out_ref, conv_ref):
        first = pl.program_id(0) == 0
        acc = x_ref[...]
        for c0 in range(0, f, cwid):
            halves = []
            for base in (c0, f + c0):
                cols = slice(base, base + cwid)
                zc = z_ref[:, cols]
                prev = jnp.where(first, 0.0, prev_ref[:, cols])
                conv = cb_ref[:, cols] + cw_ref[0:1, cols] * _shift_down(zc, prev, 2)
                conv = conv + cw_ref[1:2, cols] * _shift_down(zc, prev, 1) + cw_ref[2:3, cols] * zc
                conv_ref[:, cols] = conv.astype(MM_DTYPE)
                halves.append(conv)
            gate, val = halves
            act = gate * _sigmoid(gate) * val
            acc = acc + _dot(act, wd_ref[c0 : c0 + cwid, :])
        out_ref[...] = acc

    return pl.pallas_call(
        body,
        name=name,
        grid=(s // tm,),
        in_specs=[
            pl.BlockSpec((tm, d), lambda i: (i, 0)),
            pl.BlockSpec((tm, 2 * f), lambda i: (i, 0)),
            pl.BlockSpec((HALO, 2 * f), lambda i: (jnp.maximum(i * per8 - 1, 0), 0)),
            _resident((None, 3, 2 * f), lambda i: (layer, 0, 0)),
            _resident((1, 2 * f), lambda i: (0, 0)),
            _resident((f, d), lambda i: (0, 0)),
        ],
        out_specs=[pl.BlockSpec((tm, d), lambda i: (i, 0)), pl.BlockSpec((tm, 2 * f), lambda i: (i, 0))],
        out_shape=[SDS((s, d), F32), SDS((s, 2 * f), MM_DTYPE)],
        compiler_params=_params("arbitrary"),
    )(x, z2, z2, cw, cb, w_down)


def _convffn_bwd(dx, z2, conv, cw, w_down, layer, name):
    s, d = dx.shape
    f = z2.shape[1] // 2
    tm = _divisor(s, 256, 16)
    cwid = _conv_cols(f)
    n_steps = s // tm

    def body(dx_ref, z_ref, conv_ref, cw_ref, wd_ref, dz_ref, act_ref, dcw_ref, nxt):
        @pl.when(pl.program_id(0) == 0)
        def _():
            nxt[...] = jnp.zeros_like(nxt)
            dcw_ref[...] = jnp.zeros_like(dcw_ref)

        dxv = dx_ref[...].astype(MM_DTYPE)
        for c0 in range(0, f, cwid):
            dact = _dot_nt(dxv, wd_ref[c0 : c0 + cwid, :])
            gate = conv_ref[:, c0 : c0 + cwid].astype(F32)
            val = conv_ref[:, f + c0 : f + c0 + cwid].astype(F32)
            sg = _sigmoid(gate)
            silu = gate * sg
            act_ref[:, c0 : c0 + cwid] = (silu * val).astype(MM_DTYPE)
            dconvs = (dact * val * _silu_grad(gate, sg), dact * silu)
            for base, dconv in zip((c0, f + c0), dconvs):
                cols = slice(base, base + cwid)
                zc = z_ref[:, cols]
                nx = nxt[:, cols]
                up1 = _shift_up(dconv, nx, 1)
                up2 = _shift_up(dconv, nx, 2)
                dcw_ref[0:1, cols] += jnp.sum(up2 * zc, axis=0, keepdims=True)
                dcw_ref[1:2, cols] += jnp.sum(up1 * zc, axis=0, keepdims=True)
                dcw_ref[2:3, cols] += jnp.sum(dconv * zc, axis=0, keepdims=True)
                dcw_ref[3:4, cols] += jnp.sum(dconv, axis=0, keepdims=True)
                dz = cw_ref[2:3, cols] * dconv + cw_ref[1:2, cols] * up1 + cw_ref[0:1, cols] * up2
                dz_ref[:, cols] = dz.astype(MM_DTYPE)
                nxt[:, cols] = dconv[0:HALO]

    rev = lambda i: n_steps - 1 - i
    return pl.pallas_call(
        body,
        name=name,
        grid=(n_steps,),
        in_specs=[
            pl.BlockSpec((tm, d), lambda i: (rev(i), 0)),
            pl.BlockSpec((tm, 2 * f), lambda i: (rev(i), 0)),
            pl.BlockSpec((tm, 2 * f), lambda i: (rev(i), 0)),
            _resident((None, 3, 2 * f), lambda i: (layer, 0, 0)),
            _resident((f, d), lambda i: (0, 0)),
        ],
        out_specs=[
            pl.BlockSpec((tm, 2 * f), lambda i: (rev(i), 0)),
            pl.BlockSpec((tm, f), lambda i: (rev(i), 0)),
            pl.BlockSpec((HALO, 2 * f), lambda i: (0, 0)),
        ],
        out_shape=[SDS((s, 2 * f), MM_DTYPE), SDS((s, f), MM_DTYPE), SDS((HALO, 2 * f), F32)],
        scratch_shapes=[pltpu.VMEM((HALO, 2 * f), F32)],
        compiler_params=_params("arbitrary"),
    )(dx, z2, conv, cw, w_down)


def _loss_head(x, gain, target):
    s, d = x.shape
    tm = _divisor(s, 256, 8)

    def body(x_ref, g_ref, t_ref, loss_ref, dx_ref, dg_ref):
        @pl.when(pl.program_id(0) == 0)
        def _():
            loss_ref[...] = jnp.zeros_like(loss_ref)
            dg_ref[...] = jnp.zeros_like(dg_ref)

        xv = x_ref[...]
        gv = g_ref[...]
        r = lax.rsqrt(jnp.mean(xv * xv, axis=-1, keepdims=True) + EPS)
        xh = xv * r
        err = xh * gv - t_ref[...]
        loss_ref[...] += 0.5 * jnp.sum(jnp.mean(err * err, axis=-1, keepdims=True))
        dout = err * (1.0 / d)
        dg_ref[...] += jnp.sum(dout * xh, axis=0, keepdims=True)
        dxh = dout * gv
        dx_ref[...] = r * (dxh - xh * jnp.mean(dxh * xh, axis=-1, keepdims=True))

    return pl.pallas_call(
        body,
        name="loss_head",
        grid=(s // tm,),
        in_specs=[
            pl.BlockSpec((tm, d), lambda i: (i, 0)),
            _resident((1, d), lambda i: (0, 0)),
            pl.BlockSpec((tm, d), lambda i: (i, 0)),
        ],
        out_specs=[
            pl.BlockSpec((8, 128), lambda i: (0, 0)),
            pl.BlockSpec((tm, d), lambda i: (i, 0)),
            pl.BlockSpec((1, d), lambda i: (0, 0)),
        ],
        out_shape=[SDS((8, 128), F32), SDS((s, d), F32), SDS((1, d), F32)],
        compiler_params=_params("arbitrary"),
    )(x, gain, target)


def _adamw(w, g, m, v, name):
    shape = w.shape
    cols = shape[-1]
    rows = max(1, math.prod(shape[:-1]))
    tr = _divisor(rows, max(8, TILE_BYTES // (4 * cols)), 8)
    flat = [t.reshape(rows, cols) for t in (w, g, m, v)]

    def body(w_ref, g_ref, m_ref, v_ref, d_ref, nm_ref, nv_ref):
        gv = g_ref[...]
        m2 = ADAM_B1 * m_ref[...] + (1.0 - ADAM_B1) * gv
        v2 = ADAM_B2 * v_ref[...] + (1.0 - ADAM_B2) * (gv * gv)
        m_hat = m2 / (1.0 - ADAM_B1**ADAM_STEP)
        v_hat = v2 / (1.0 - ADAM_B2**ADAM_STEP)
        d_ref[...] = -ADAM_LR * (m_hat / (jnp.sqrt(v_hat) + ADAM_EPS) + ADAM_WD * w_ref[...])
        nm_ref[...] = m2
        nv_ref[...] = v2

    spec = pl.BlockSpec((tr, cols), lambda i: (i, 0))
    outs = pl.pallas_call(
        body,
        name=name,
        grid=(rows // tr,),
        in_specs=[spec] * 4,
        out_specs=[spec] * 3,
        out_shape=[SDS((rows, cols), F32)] * 3,
        compiler_params=_params("parallel"),
    )(*flat)
    return tuple(t.reshape(shape) for t in outs)


def _position():
    return lax.axis_index("x"), lax.axis_index("y"), lax.axis_index("c")


def _other_chips(x, y):
    return [(1 - x, y), (x, 1 - y), (1 - x, 1 - y)]


def _remote(src, dst, send_sem, recv_sem, device):
    return pltpu.make_async_remote_copy(
        src_ref=src, dst_ref=dst, send_sem=send_sem, recv_sem=recv_sem, device_id=device, device_id_type=MESH
    )


def _sub(ref, lead, shape, axis, chip=None, half=None):
    cut = [pl.ds(0, shape[0]), pl.ds(0, shape[1])]
    if chip is not None:
        size = shape[axis] // N_CHIPS
        cut[axis] = pl.ds(chip * size, size)
    if half is not None:
        size = shape[1 - axis] // 2
        cut[1 - axis] = pl.ds(half * size, size)
    return ref.at[(*lead, *cut)]


def _scaled(shape, axis, num, den=1):
    return tuple(d * num // den if i == axis else d for i, d in enumerate(shape))


def _gather_rider(shards, axes, layer, conv_w=None):
    n = len(shards)
    shard2d = [t.shape[1:] for t in shards]
    full2d = [_scaled(s2, ax, N_CHIPS) for s2, ax in zip(shard2d, axes)]
    out_shapes = [SDS(f2, t.dtype) for f2, t in zip(full2d, shards)]
    inputs = list(shards)
    own = 6
    scratch = [pltpu.SemaphoreType.DMA((n, 7)), pltpu.SemaphoreType.DMA((n, 7))]
    if conv_w is not None:
        cshape = conv_w.shape
        inputs.append(conv_w)
        out_shapes.append(SDS(_scaled(cshape, 2, N_CHIPS), conv_w.dtype))
        scratch += [pltpu.SemaphoreType.DMA((4,)), pltpu.SemaphoreType.DMA((4,))]

    def conv_slab(ref, chip):
        return ref.at[:, :, pl.ds((2 * chip[0] + chip[1]) * cshape[2], cshape[2])]

    def own_copy(ins, outs, send_sems, recv_sems, k):
        x, y, c = _position()
        slab = _sub(outs[k], (), full2d[k], axes[k], chip=2 * x + y)
        return _remote(ins[k].at[layer], slab, send_sems.at[k, own], recv_sems.at[k, own], (x, y, 1 - c))

    def start(ins, outs, scr):
        send_sems, recv_sems = scr[:2]
        x, y, c = _position()
        me = 2 * x + y
        for k in range(n):
            own_copy(ins, outs, send_sems, recv_sems, k).start()
            for j, chip in enumerate(_other_chips(x, y)):
                _remote(
                    _sub(ins[k], (layer,), shard2d[k], axes[k], half=c),
                    _sub(outs[k], (), full2d[k], axes[k], chip=me, half=c),
                    send_sems.at[k, j], recv_sems.at[k, j], (*chip, c),
                ).start()
        if conv_w is not None:
            csend, crecv = scr[2:]
            _remote(ins[n], conv_slab(outs[n], (x, y)), csend.at[3], crecv.at[3], (x, y, 1 - c)).start()
            for j, chip in enumerate(_other_chips(x, y)):
                _remote(ins[n], conv_slab(outs[n], (x, y)), csend.at[j], crecv.at[j], (*chip, c)).start()

    def finish(ins, outs, scr):
        send_sems, recv_sems = scr[:2]
        x, y, c = _position()
        me = 2 * x + y
        sibling = (x, y, 1 - c)
        chips = _other_chips(x, y)
        forwards = []
        for k in range(n):
            src = _sub(ins[k], (layer,), shard2d[k], axes[k], half=c)
            for j, chip in enumerate(chips):
                landed = _sub(outs[k], (), full2d[k], axes[k], chip=2 * chip[0] + chip[1], half=c)
                _remote(src, landed, send_sems.at[k, j], recv_sems.at[k, j], (*chip, c)).wait_recv()
                fwd = _remote(landed, landed, send_sems.at[k, 3 + j], recv_sems.at[k, 3 + j], sibling)
                fwd.start()
                forwards.append(fwd)
        for k in range(n):
            src = _sub(ins[k], (layer,), shard2d[k], axes[k], half=c)
            for j, chip in enumerate(chips):
                theirs = _sub(outs[k], (), full2d[k], axes[k], chip=2 * chip[0] + chip[1], half=1 - c)
                _remote(src, theirs, send_sems.at[k, 3 + j], recv_sems.at[k, 3 + j], sibling).wait_recv()
        for fwd in forwards:
            fwd.wait_send()
        for k in range(n):
            src = _sub(ins[k], (layer,), shard2d[k], axes[k], half=c)
            mine = _sub(outs[k], (), full2d[k], axes[k], chip=me, half=c)
            for j, chip in enumerate(chips):
                _remote(src, mine, send_sems.at[k, j], recv_sems.at[k, j], (*chip, c)).wait_send()
            own_copy(ins, outs, send_sems, recv_sems, k).wait()
        if conv_w is not None:
            csend, crecv = scr[2:]
            for j, chip in enumerate(chips):
                cp = _remote(ins[n], conv_slab(outs[n], chip), csend.at[j], crecv.at[j], (*chip, c))
                cp.wait_recv()
                cp.wait_send()
            _remote(ins[n], conv_slab(outs[n], (x, y)), csend.at[3], crecv.at[3], sibling).wait()

    return _Rider(inputs, out_shapes, scratch, start, finish)


def _pair_rider(stacks, axes, layer):
    n = len(stacks)
    shapes = [t.shape[1:] for t in stacks]
    out_shapes = [SDS(_scaled(s2, 1 - ax, 1, 2), F32) for s2, ax in zip(shapes, axes)]
    scratch = [pltpu.SemaphoreType.DMA((n,)), pltpu.SemaphoreType.DMA((n,))]

    def copies(ins, outs, scr):
        x, y, c = _position()
        return [
            _remote(_sub(ins[k], (layer,), shapes[k], axes[k], half=1 - c), outs[k], scr[0].at[k], scr[1].at[k], (x, y, 1 - c))
            for k in range(n)
        ]

    def start(ins, outs, scr):
        for cp in copies(ins, outs, scr):
            cp.start()

    def finish(ins, outs, scr):
        for cp in copies(ins, outs, scr):
            cp.wait()

    return _Rider(stacks, out_shapes, scratch, start, finish)


def _chip_rider(pairs, axes):
    n = len(pairs)
    shapes = [t.shape for t in pairs]
    out_shapes = [SDS((3,) + _scaled(s2, ax, 1, N_CHIPS), t.dtype) for s2, ax, t in zip(shapes, axes, pairs)]
    scratch = [pltpu.SemaphoreType.DMA((n, 3)), pltpu.SemaphoreType.DMA((n, 3))]

    def copies(ins, outs, scr):
        x, y, c = _position()
        return [
            _remote(
                _sub(ins[k], (), shapes[k], axes[k], chip=2 * chip[0] + chip[1]), outs[k].at[j],
                scr[0].at[k, j], scr[1].at[k, j], (*chip, c),
            )
            for k in range(n)
            for j, chip in enumerate(_other_chips(x, y))
        ]

    def start(ins, outs, scr):
        for cp in copies(ins, outs, scr):
            cp.start()

    def finish(ins, outs, scr):
        for cp in copies(ins, outs, scr):
            cp.wait()

    return _Rider(pairs, out_shapes, scratch, start, finish)


def _complete_rider(stacks, axes, layers):
    n = len(stacks)
    shapes = [t.shape[1:] for t in stacks]
    scratch = [pltpu.SemaphoreType.DMA((n,)), pltpu.SemaphoreType.DMA((n,))]

    def start(ins, outs, scr):
        x, y, c = _position()
        for k in range(n):
            mine = _sub(outs[k], (layers[k],), shapes[k], axes[k], half=c)
            _remote(mine, mine, scr[0].at[k], scr[1].at[k], (x, y, 1 - c)).start()

    def finish(ins, outs, scr):
        x, y, c = _position()
        for k in range(n):
            mine = _sub(outs[k], (layers[k],), shapes[k], axes[k], half=c)
            theirs = _sub(outs[k], (layers[k],), shapes[k], axes[k], half=1 - c)
            _remote(mine, theirs, scr[0].at[k], scr[1].at[k], (x, y, 1 - c)).wait_recv()
            _remote(mine, mine, scr[0].at[k], scr[1].at[k], (x, y, 1 - c)).wait_send()

    return _Rider(stacks, [SDS(t.shape, t.dtype) for t in stacks], scratch, start, finish, {k: k for k in range(n)})


def _small_rider(buf):
    rows, cols = buf.shape
    flips = [(fx, fy, fc) for fx in (0, 1) for fy in (0, 1) for fc in (0, 1)][1:]

    def peers():
        x, y, c = _position()
        return [(x + f[0] - 2 * x * f[0], y + f[1] - 2 * y * f[1], c + f[2] - 2 * c * f[2]) for f in flips]

    def start(ins, outs, scr):
        x, y, c = _position()
        mine = outs[0].at[4 * x + 2 * y + c]
        pltpu.make_async_copy(ins[0], mine, scr[2]).start()
        for j, peer in enumerate(peers()):
            _remote(ins[0], mine, scr[0].at[j], scr[1].at[j], peer).start()

    def finish(ins, outs, scr):
        x, y, c = _position()
        mine = outs[0].at[4 * x + 2 * y + c]
        for j, peer in enumerate(peers()):
            _remote(ins[0], outs[0].at[4 * peer[0] + 2 * peer[1] + peer[2]], scr[0].at[j], scr[1].at[j], peer).wait_recv()
        for j, peer in enumerate(peers()):
            _remote(ins[0], mine, scr[0].at[j], scr[1].at[j], peer).wait_send()
        pltpu.make_async_copy(ins[0], mine, scr[2]).wait()

    scratch = [pltpu.SemaphoreType.DMA((7,)), pltpu.SemaphoreType.DMA((7,)), pltpu.SemaphoreType.DMA(())]
    return _Rider([buf], [SDS((8, rows, cols), buf.dtype)], scratch, start, finish)


def _my_core():
    return lax.axis_index("c")


def _my_chip():
    return 2 * lax.axis_index("x") + lax.axis_index("y")


def _pair_sum(stack, layer, recv, axis, name):
    hk, hn = recv.shape
    tk = _divisor(hk, max(16, TILE_BYTES // (4 * hn)), 16)
    per = hk // tk

    def body(g_ref, r_ref, out_ref):
        out_ref[...] = (g_ref[...] + r_ref[...]).astype(out_ref.dtype)

    if axis == 1:
        mine = pl.BlockSpec((None, tk, hn), lambda i: (layer, _my_core() * per + i, 0))
    else:
        mine = pl.BlockSpec((None, tk, hn), lambda i: (layer, i, _my_core()))
    return pl.pallas_call(
        body,
        name=name,
        grid=(per,),
        in_specs=[mine, pl.BlockSpec((tk, hn), lambda i: (i, 0))],
        out_specs=pl.BlockSpec((tk, hn), lambda i: (i, 0)),
        out_shape=SDS((hk, hn), BF16),
        compiler_params=_params("parallel"),
    )(stack, recv)


def _chip_sum(pair, others, axis, stack, layer, name):
    _, sk, sn = others.shape
    tk = _divisor(sk, max(16, TILE_BYTES // (4 * sn)), 16)
    per = sk // tk

    def body(p_ref, o0_ref, o1_ref, o2_ref, *rest):
        out_ref = rest[-1]
        acc = p_ref[...].astype(F32) + o0_ref[...].astype(F32)
        out_ref[...] = (acc + o1_ref[...].astype(F32)) + o2_ref[...].astype(F32)

    if axis == 1:
        own = pl.BlockSpec((tk, sn), lambda i: (i, _my_chip()))
        out = pl.BlockSpec((None, tk, sn), lambda i: (layer, _my_core() * per + i, 0))
        shard = (2 * sk, sn)
    else:
        own = pl.BlockSpec((tk, sn), lambda i: (_my_chip() * per + i, 0))
        out = pl.BlockSpec((None, tk, sn), lambda i: (layer, i, _my_core()))
        shard = (sk, 2 * sn)
    other = lambda j: pl.BlockSpec((None, tk, sn), lambda i: (j, i, 0))
    in_specs = [own, other(0), other(1), other(2)]
    args = [pair, others, others, others]
    aliases = {}
    if stack is not None:
        in_specs.append(ANY)
        args.append(stack)
        aliases = {4: 0}
    return pl.pallas_call(
        body,
        name=name,
        grid=(per,),
        in_specs=in_specs,
        out_specs=out,
        out_shape=SDS((DEPTH,) + shard, F32),
        input_output_aliases=aliases,
        compiler_params=_params("parallel"),
    )(*args)


def _sum_devices(gathered, name):
    _, rows, cols = gathered.shape

    def body(g_ref, out_ref):
        acc = g_ref[0]
        for dev in range(1, 8):
            acc = acc + g_ref[dev]
        out_ref[...] = acc

    return pl.pallas_call(body, name=name, out_shape=SDS((rows, cols), F32))(gathered)


PACK_TILE = 8 * 128


def _pack(arrays):
    parts = []
    for t in arrays:
        flat = t.reshape(-1)
        pad = (-flat.shape[0]) % PACK_TILE
        if pad:
            flat = jnp.concatenate([flat, jnp.zeros((pad,), flat.dtype)])
        parts.append(flat.reshape(-1, 128))
    return jnp.concatenate(parts, axis=0)


def _unpack(buf, shapes):
    out, row = [], 0
    for shp in shapes:
        size = math.prod(shp)
        rows = -(-size // PACK_TILE) * 8
        out.append(buf[row : row + rows].reshape(-1)[:size].reshape(shp))
        row += rows
    return out


BIG = ("w_in", "w_br_gm", "w_br_hg", "w_out", "w_up", "w_down")
BIG_AXIS = (1, 1, 1, 0, 1, 0)
LAYER_SMALL = ("mix_norm", "gm_ln_g", "gm_ln_b", "gm_ws", "gm_bs", "lb", "hg_norm_g", "ffn_norm", "conv_w", "conv_b")
WEIGHTS = ("mix_norm", "w_in", "gm_ln_g", "gm_ln_b", "gm_ws", "gm_bs", "hg_lb_logits", "hg_norm_g", "w_br_gm", "w_br_hg", "w_out",
           "ffn_norm", "w_up", "conv_w", "conv_b", "w_down", "final_norm")


def kernel(x, mix_norm, w_in, gm_ln_g, gm_ln_b, gm_ws, gm_bs, hg_lb_logits, hg_norm_g, w_br_gm, w_br_hg, w_out, ffn_norm, w_up, conv_w, conv_b, w_down, final_norm, loss_target, m_mix_norm, m_w_in, m_gm_ln_g, m_gm_ln_b, m_gm_ws, m_gm_bs, m_hg_lb_logits, m_hg_norm_g, m_w_br_gm, m_w_br_hg, m_w_out, m_ffn_norm, m_w_up, m_conv_w, m_conv_b, m_w_down, m_final_norm, v_mix_norm, v_w_in, v_gm_ln_g, v_gm_ln_b, v_gm_ws, v_gm_bs, v_hg_lb_logits, v_hg_norm_g, v_w_br_gm, v_w_br_hg, v_w_out, v_ffn_norm, v_w_up, v_conv_w, v_conv_b, v_w_down, v_final_norm):
    w = dict(mix_norm=mix_norm, w_in=w_in, gm_ln_g=gm_ln_g, gm_ln_b=gm_ln_b, gm_ws=gm_ws, gm_bs=gm_bs, hg_lb_logits=hg_lb_logits,
             hg_norm_g=hg_norm_g, w_br_gm=w_br_gm, w_br_hg=w_br_hg, w_out=w_out, ffn_norm=ffn_norm, w_up=w_up, conv_w=conv_w,
             conv_b=conv_b, w_down=w_down, final_norm=final_norm)
    m = dict(mix_norm=m_mix_norm, w_in=m_w_in, gm_ln_g=m_gm_ln_g, gm_ln_b=m_gm_ln_b, gm_ws=m_gm_ws, gm_bs=m_gm_bs,
             hg_lb_logits=m_hg_lb_logits, hg_norm_g=m_hg_norm_g, w_br_gm=m_w_br_gm, w_br_hg=m_w_br_hg, w_out=m_w_out,
             ffn_norm=m_ffn_norm, w_up=m_w_up, conv_w=m_conv_w, conv_b=m_conv_b, w_down=m_w_down, final_norm=m_final_norm)
    v = dict(mix_norm=v_mix_norm, w_in=v_w_in, gm_ln_g=v_gm_ln_g, gm_ln_b=v_gm_ln_b, gm_ws=v_gm_ws, gm_bs=v_gm_bs,
             hg_lb_logits=v_hg_lb_logits, hg_norm_g=v_hg_norm_g, w_br_gm=v_w_br_gm, w_br_hg=v_w_br_hg, w_out=v_w_out,
             ffn_norm=v_ffn_norm, w_up=v_w_up, conv_w=v_conv_w, conv_b=v_conv_b, w_down=v_w_down, final_norm=v_final_norm)

    axes = list(BIG_AXIS)
    d = x.shape[2]

    shards = [w[k].astype(MM_DTYPE) for k in BIG]
    w_in_0, conv_full = _run_rider(_gather_rider(shards[:1], axes[:1], 0, conv_w=conv_w), "gather_weights_0")
    full = [dict(w_in=w_in_0)]
    lb = _lower_bounds(hg_lb_logits)
    xs = x[0]
    saved = []
    mixer = 4
    for l in range(DEPTH):
        fw = full[l]
        (z, h), got = _norm_matmul(
            xs, mix_norm[l : l + 1], fw["w_in"], f"in_proj_{l}", _gather_rider(shards[mixer:], axes[mixer:], l)
        )
        fw.update(zip(BIG[mixer:], got))
        bst = gm_bs[l].T
        a = _gmlp_fwd(z, gm_ln_g[l : l + 1], gm_ln_b[l : l + 1], gm_ws[l], bst, f"gmlp_fwd_{l}")
        rider = _gather_rider(shards[1:mixer], axes[1:mixer], 0) if l == 0 else None
        (o, b, states), got = _hgrn_fwd(z, lb[l : l + 1], hg_norm_g[l : l + 1], f"hgrn_fwd_{l}", rider)
        if l == 0:
            fw.update(zip(BIG[1:mixer], got))
        x1 = _mix_fwd(xs, a, b, z, fw["w_br_gm"], fw["w_br_hg"], fw["w_out"], f"mix_fwd_{l}")
        rider = _gather_rider(shards[:mixer], axes[:mixer], l + 1) if l + 1 < DEPTH else None
        (z2, h2), got = _norm_matmul(x1, ffn_norm[l : l + 1], fw["w_up"], f"up_proj_{l}", rider)
        if l + 1 < DEPTH:
            full.append(dict(zip(BIG[:mixer], got)))
        x2, conv = _convffn_fwd(x1, z2, conv_full, conv_b[l : l + 1], fw["w_down"], l, f"convffn_fwd_{l}")
        saved.append((xs, h, z, a, b, o, states, x1, h2, z2, conv, bst))
        xs = x2

    loss_tile, dx, d_final = _loss_head(xs, final_norm.reshape(1, d), loss_target[0])
    loss = lax.psum(loss_tile[0, 0], ("x", "y", "c"))

    big = {k: None for k in BIG}
    grads = [None] * len(BIG)
    per_layer = [None] * DEPTH
    mix_ids, ffn_ids = list(range(mixer)), list(range(mixer, len(BIG)))
    mix_axes, ffn_axes = axes[:mixer], axes[mixer:]
    mix_pairs = None
    mix_summed = None

    def pair_sums(ids, received, layer):
        return [_pair_sum(big[BIG[k]], layer, r, axes[k], f"pair_sum_{BIG[k]}_{layer}") for k, r in zip(ids, received)]

    def chip_sums(ids, pairs, others, layer):
        for k, p, ot in zip(ids, pairs, others):
            grads[k] = _chip_sum(p, ot, axes[k], grads[k], layer, f"chip_sum_{BIG[k]}_{layer}")

    for l in reversed(range(DEPTH)):
        x0, h, z, a, b, o, states, x1, h2, z2, conv, bst = saved[l]
        fw = full[l]
        dz2, act, dconv = _convffn_bwd(dx, z2, conv, conv_full, fw["w_down"], l, f"convffn_bwd_{l}")
        big["w_down"] = _matmul_tn(act, dx, big["w_down"], l, f"dw_down_{l}")
        big["w_up"] = _matmul_tn(h2, dz2, big["w_up"], l, f"dw_up_{l}")
        rider = _pair_rider([big[BIG[k]] for k in ffn_ids], ffn_axes, l)
        if mix_pairs is not None:
            rider = _join(rider, _chip_rider(mix_pairs, mix_axes))
        (dx1, d_ffn), got = _proj_norm_bwd([dz2], fw["w_up"], x1, ffn_norm[l : l + 1], dx, f"up_proj_bwd_{l}", rider)
        if mix_pairs is not None:
            chip_sums(mix_ids, mix_pairs, got[len(ffn_ids) :], l + 1)
            mix_summed = l + 1
        ffn_pairs = pair_sums(ffn_ids, got[: len(ffn_ids)], l)
        rider = None
        if l == 0:
            upper = [_pack([per_layer[j][k] for k in LAYER_SMALL]) for j in range(1, DEPTH)] + [_pack([d_final[0]])]
            rider = _small_rider(jnp.concatenate(upper, axis=0))
        (y, dpa, dpb, da, db, dgates), gathered_upper = _mix_bwd(
            dx1, a, b, z, fw["w_br_gm"], fw["w_br_hg"], fw["w_out"], f"mix_bwd_{l}", rider
        )
        big["w_out"] = _matmul_tn(y, dx1, big["w_out"], l, f"dw_out_{l}")
        big["w_br_gm"] = _matmul_tn(a, dpa, big["w_br_gm"], l, f"dw_br_gm_{l}")
        big["w_br_hg"] = _matmul_tn(b, dpb, big["w_br_hg"], l, f"dw_br_hg_{l}")
        (dz_hg, d_lb, d_ng), others = _hgrn_bwd(
            db, z, o, states, lb[l : l + 1], hg_norm_g[l : l + 1], f"hgrn_bwd_{l}", _chip_rider(ffn_pairs, ffn_axes)
        )
        chip_sums(ffn_ids, ffn_pairs, others, l)
        dz_gm, d_lng, d_lnb, d_ws, d_bst = _gmlp_bwd(da, z, gm_ln_g[l : l + 1], gm_ln_b[l : l + 1], gm_ws[l], bst, f"gmlp_bwd_{l}")
        n_in = fw["w_in"].shape[1]
        big["w_in"] = _matmul_tn(h, dz_gm, big["w_in"], l, f"dw_in_gm_{l}", n_total=n_in, col_off=0)
        big["w_in"] = _matmul_tn(h, dz_hg, big["w_in"], l, f"dw_in_hg_{l}", n_total=n_in, col_off=2 * BR_DIM)
        big["w_in"] = _matmul_tn(h, dgates, big["w_in"], l, f"dw_in_gate_{l}", n_total=n_in, col_off=6 * BR_DIM)
        done_ids = ffn_ids + (mix_ids if mix_summed is not None else [])
        done_layers = [l] * len(ffn_ids) + ([mix_summed] * len(mix_ids) if mix_summed is not None else [])
        rider = _join(
            _pair_rider([big[BIG[k]] for k in mix_ids], mix_axes, l),
            _complete_rider([grads[k] for k in done_ids], [axes[k] for k in done_ids], done_layers),
        )
        (dx, d_mix), got = _proj_norm_bwd([dz_gm, dz_hg, dgates], fw["w_in"], x0, mix_norm[l : l + 1], dx1, f"in_proj_bwd_{l}", rider)
        for k, g in zip(done_ids, got[len(mix_ids) :]):
            grads[k] = g
        mix_pairs = pair_sums(mix_ids, got[: len(mix_ids)], l)
        per_layer[l] = dict(
            mix_norm=d_mix[0], gm_ln_g=d_lng[0], gm_ln_b=d_lnb[0], gm_ws=d_ws, gm_bs=d_bst.T, lb=d_lb[0], hg_norm_g=d_ng[0],
            ffn_norm=d_ffn[0], conv_w=dconv[0:3], conv_b=dconv[3],
        )

    got = _run_rider(
        _join(_chip_rider(mix_pairs, mix_axes), _small_rider(_pack([per_layer[0][k] for k in LAYER_SMALL]))), "grad_exchange_0"
    )
    chip_sums(mix_ids, mix_pairs, got[: len(mix_ids)], 0)
    done = _run_rider(_complete_rider([grads[k] for k in mix_ids], mix_axes, [0] * len(mix_ids)), "grad_complete_0")
    for k, g in zip(mix_ids, done):
        grads[k] = g
    grad = dict(zip(BIG, grads))

    layer_shapes = [per_layer[0][k].shape for k in LAYER_SMALL]
    summed_upper = _unpack(_sum_devices(gathered_upper[0], "sum_small_upper"), layer_shapes * (DEPTH - 1) + [d_final[0].shape])
    summed_0 = _unpack(_sum_devices(got[len(mix_ids)], "sum_small_0"), layer_shapes)
    by_layer = [summed_0] + [summed_upper[j * len(LAYER_SMALL) : (j + 1) * len(LAYER_SMALL)] for j in range(DEPTH - 1)]
    small = {k: jnp.stack([by_layer[j][i] for j in range(DEPTH)]) for i, k in enumerate(LAYER_SMALL)}
    for k in LAYER_SMALL:
        if k != "lb":
            grad[k] = small[k]
    grad["hg_lb_logits"] = _lower_bounds_bwd(hg_lb_logits, small["lb"])
    grad["final_norm"] = summed_upper[-1]
    grad["conv_w"] = lax.dynamic_slice_in_dim(grad["conv_w"], _my_chip() * conv_w.shape[2], conv_w.shape[2], axis=2)

    delta, new_m, new_v = {}, {}, {}
    for k in WEIGHTS:
        delta[k], new_m[k], new_v[k] = _adamw(w[k], grad[k], m[k], v[k], f"adamw_{k}")

    return (loss, dx[None], *[grad[k] for k in WEIGHTS], *[delta[k] for k in WEIGHTS], *[new_m[k] for k in WEIGHTS],
            *[new_v[k] for k in WEIGHTS])
```

```python
import math

import jax
import jax.numpy as jnp
from jax import lax
from jax.experimental import pallas as pl
from jax.experimental.pallas import tpu as pltpu

F32 = jnp.float32
BF16 = jnp.bfloat16
MM_DTYPE = BF16

N_HEADS = 4
HEAD_DIM = 128
BR_DIM = N_HEADS * HEAD_DIM
CHUNK = 64
GM_BLOCK = 128
DEPTH = 4
N_CHIPS = 4
EPS = 1e-6
TINY = 1e-30
LB_MAX = 0.999
ADAM_LR = 0.001
ADAM_B1 = 0.9
ADAM_B2 = 0.999
ADAM_EPS = 1e-08
ADAM_WD = 0.01
ADAM_STEP = 10
INV_SQRT2 = 1.0 / math.sqrt(2.0)
INV_SQRT_2PI = 1.0 / math.sqrt(2.0 * math.pi)

VMEM_LIMIT_BYTES = 56 * 1024 * 1024
TILE_BYTES = 2 * 1024 * 1024
ACC_BYTES = 6 * 1024 * 1024
MESH = pl.DeviceIdType.MESH
SDS = jax.ShapeDtypeStruct
ANY = pl.BlockSpec(memory_space=pl.ANY)


def _params(*sem):
    return pltpu.CompilerParams(dimension_semantics=sem or None, vmem_limit_bytes=VMEM_LIMIT_BYTES)


def _divisor(n, limit, mult):
    best = None
    for d in range(mult, min(n, limit) + 1, mult):
        if n % d == 0:
            best = d
    return best if best is not None else n


def _dot(a, b):
    return jnp.dot(a.astype(MM_DTYPE), b.astype(MM_DTYPE), preferred_element_type=F32)


def _dot_nt(a, b):
    return lax.dot_general(a.astype(MM_DTYPE), b.astype(MM_DTYPE), (((1,), (1,)), ((), ())), preferred_element_type=F32)


def _dot_tn(a, b):
    return lax.dot_general(a.astype(MM_DTYPE), b.astype(MM_DTYPE), (((0,), (0,)), ((), ())), preferred_element_type=F32)


def _sigmoid(x):
    return 1.0 / (1.0 + jnp.exp(-x))


def _gelu(x):
    return 0.5 * x * (1.0 + lax.erf(x * INV_SQRT2))


def _gelu_grad(x):
    return 0.5 * (1.0 + lax.erf(x * INV_SQRT2)) + x * jnp.exp(-0.5 * x * x) * INV_SQRT_2PI


def _silu_grad(x, s):
    return s * (1.0 + x * (1.0 - s))


def _resident(shape, index_map):
    return pl.BlockSpec(shape, index_map, pipeline_mode=pl.Buffered(1))


class _Rider:
    def __init__(self, inputs, out_shapes, scratch, start, finish, aliases=None):
        self.inputs = list(inputs)
        self.out_shapes = list(out_shapes)
        self.scratch = list(scratch)
        self.start = start
        self.finish = finish
        self.aliases = dict(aliases or {})


def _join(a, b):
    if a is None or b is None:
        return a if b is None else b
    na, oa, sa = len(a.inputs), len(a.out_shapes), len(a.scratch)

    def start(i, o, s):
        a.start(i[:na], o[:oa], s[:sa])
        b.start(i[na:], o[oa:], s[sa:])

    def finish(i, o, s):
        a.finish(i[:na], o[:oa], s[:sa])
        b.finish(i[na:], o[oa:], s[sa:])

    aliases = dict(a.aliases)
    aliases.update({na + k: oa + v for k, v in b.aliases.items()})
    return _Rider(a.inputs + b.inputs, a.out_shapes + b.out_shapes, a.scratch + b.scratch, start, finish, aliases)


def _hosted_call(body, rider, *, name, grid, in_specs, out_specs, out_shape, scratch_shapes, args, semantics):
    n_in, n_out, n_scr = len(in_specs), len(out_specs), len(scratch_shapes)
    if rider is None:
        outs = pl.pallas_call(
            body, name=name, grid=grid, in_specs=in_specs, out_specs=out_specs, out_shape=out_shape,
            scratch_shapes=scratch_shapes, compiler_params=_params(*semantics),
        )(*args)
        return list(outs), []
    ri, ro = len(rider.inputs), len(rider.out_shapes)

    def wrapped(*refs):
        p = 0
        hin = refs[p : p + n_in]
        p += n_in
        rin = refs[p : p + ri]
        p += ri
        hout = refs[p : p + n_out]
        p += n_out
        rout = refs[p : p + ro]
        p += ro
        hscr = refs[p : p + n_scr]
        rscr = refs[p + n_scr :]

        @pl.when(pl.program_id(0) == 0)
        def _():
            rider.start(rin, rout, rscr)

        body(*hin, *hout, *hscr)

        @pl.when(pl.program_id(0) == grid[0] - 1)
        def _():
            rider.finish(rin, rout, rscr)

    outs = pl.pallas_call(
        wrapped,
        name=name,
        grid=grid,
        in_specs=list(in_specs) + [ANY] * ri,
        out_specs=list(out_specs) + [ANY] * ro,
        out_shape=list(out_shape) + rider.out_shapes,
        scratch_shapes=list(scratch_shapes) + rider.scratch,
        input_output_aliases={n_in + k: n_out + v for k, v in rider.aliases.items()},
        compiler_params=_params(*semantics),
    )(*args, *rider.inputs)
    return list(outs[:n_out]), list(outs[n_out:])


def _run_rider(rider, name):
    ri, ro = len(rider.inputs), len(rider.out_shapes)

    def body(*refs):
        rin, rout, rscr = refs[:ri], refs[ri : ri + ro], refs[ri + ro :]
        rider.start(rin, rout, rscr)
        rider.finish(rin, rout, rscr)

    return list(
        pl.pallas_call(
            body,
            name=name,
            in_specs=[ANY] * ri,
            out_specs=[ANY] * ro,
            out_shape=rider.out_shapes,
            scratch_shapes=rider.scratch,
            input_output_aliases=rider.aliases,
        )(*rider.inputs)
    )


def _lower_bounds(logits):
    def body(lg_ref, lb_ref):
        lg = lg_ref[...]
        mx = jnp.max(lg, axis=0, keepdims=True)
        e = jnp.exp(lg - mx)
        p = e / jnp.sum(e, axis=0, keepdims=True)
        run = p[0:1]
        rows = [run - p[0:1]]
        for l in range(1, DEPTH):
            run = run + p[l : l + 1]
            rows.append(run - p[0:1])
        raw = jnp.concatenate(rows, axis=0)
        lb_ref[...] = jnp.minimum(jnp.maximum(raw, 0.0), LB_MAX)

    return pl.pallas_call(body, name="lower_bounds", out_shape=SDS(logits.shape, F32))(logits)


def _lower_bounds_bwd(logits, dlb):
    def body(lg_ref, dlb_ref, dlg_ref):
        lg = lg_ref[...]
        mx = jnp.max(lg, axis=0, keepdims=True)
        e = jnp.exp(lg - mx)
        p = e / jnp.sum(e, axis=0, keepdims=True)
        run = p[0:1]
        rows = [run - p[0:1]]
        for l in range(1, DEPTH):
            run = run + p[l : l + 1]
            rows.append(run - p[0:1])
        raw = jnp.concatenate(rows, axis=0)
        lo = jnp.where(raw > 0.0, 1.0, jnp.where(raw == 0.0, 0.5, 0.0))
        clipped = jnp.maximum(raw, 0.0)
        hi = jnp.where(clipped < LB_MAX, 1.0, jnp.where(clipped == LB_MAX, 0.5, 0.0))
        draw = dlb_ref[...] * lo * hi
        total = jnp.sum(draw, axis=0, keepdims=True)
        dps = []
        tail = total
        for j in range(DEPTH):
            dps.append(tail - total if j == 0 else tail)
            tail = tail - draw[j : j + 1]
        dp = jnp.concatenate(dps, axis=0)
        dlg_ref[...] = p * (dp - jnp.sum(p * dp, axis=0, keepdims=True))

    return pl.pallas_call(body, name="lower_bounds_bwd", out_shape=SDS(logits.shape, F32))(logits, dlb)


def _norm_matmul(x, gain, w, name, rider=None):
    s, d = x.shape
    n = w.shape[1]
    tm = _divisor(s, 512, 16)

    def body(x_ref, g_ref, w_ref, z_ref, h_ref):
        xv = x_ref[...]
        r = lax.rsqrt(jnp.mean(xv * xv, axis=-1, keepdims=True) + EPS)
        h = ((xv * r) * g_ref[...]).astype(MM_DTYPE)
        h_ref[...] = h
        z_ref[...] = jnp.dot(h, w_ref[...], preferred_element_type=F32)

    return _hosted_call(
        body,
        rider,
        name=name,
        grid=(s // tm,),
        in_specs=[
            pl.BlockSpec((tm, d), lambda i: (i, 0)),
            _resident((1, d), lambda i: (0, 0)),
            _resident((d, n), lambda i: (0, 0)),
        ],
        out_specs=[pl.BlockSpec((tm, n), lambda i: (i, 0)), pl.BlockSpec((tm, d), lambda i: (i, 0))],
        out_shape=[SDS((s, n), F32), SDS((s, d), MM_DTYPE)],
        scratch_shapes=[],
        args=[x, gain, w],
        semantics=("arbitrary",),
    )


def _proj_norm_bwd(dz_parts, w, x, gain, dres, name, rider=None):
    s, d = x.shape
    n = w.shape[1]
    tm = _divisor(s, 512, 16)
    widths = [p.shape[1] for p in dz_parts]
    offsets = [sum(widths[:k]) for k in range(len(widths))]
    n_parts = len(dz_parts)

    def body(*refs):
        dz_refs = refs[:n_parts]
        w_ref, x_ref, g_ref, dres_ref, dx_ref, dg_ref = refs[n_parts:]

        @pl.when(pl.program_id(0) == 0)
        def _():
            dg_ref[...] = jnp.zeros_like(dg_ref)

        dh = None
        for dz_ref, off, wd in zip(dz_refs, offsets, widths):
            part = _dot_nt(dz_ref[...], w_ref[:, off : off + wd])
            dh = part if dh is None else dh + part
        xv = x_ref[...]
        r = lax.rsqrt(jnp.mean(xv * xv, axis=-1, keepdims=True) + EPS)
        xh = xv * r
        dg_ref[...] += jnp.sum(dh * xh, axis=0, keepdims=True)
        dxh = dh * g_ref[...]
        dx_ref[...] = dres_ref[...] + r * (dxh - xh * jnp.mean(dxh * xh, axis=-1, keepdims=True))

    in_specs = [pl.BlockSpec((tm, wd), lambda i: (i, 0)) for wd in widths] + [
        _resident((d, n), lambda i: (0, 0)),
        pl.BlockSpec((tm, d), lambda i: (i, 0)),
        _resident((1, d), lambda i: (0, 0)),
        pl.BlockSpec((tm, d), lambda i: (i, 0)),
    ]
    return _hosted_call(
        body,
        rider,
        name=name,
        grid=(s // tm,),
        in_specs=in_specs,
        out_specs=[pl.BlockSpec((tm, d), lambda i: (i, 0)), pl.BlockSpec((1, d), lambda i: (0, 0))],
        out_shape=[SDS((s, d), F32), SDS((1, d), F32)],
        scratch_shapes=[],
        args=[*dz_parts, w, x, gain, dres],
        semantics=("arbitrary",),
    )


def _matmul_tn(a, b, stack, layer, name, n_total=None, col_off=0):
    s, k = a.shape
    n = b.shape[1]
    n_total = n if n_total is None else n_total
    tn = _divisor(math.gcd(n, col_off) if col_off else n, max(128, ACC_BYTES // (4 * k)), 128)
    ts = _divisor(s, min(2048, max(512, ACC_BYTES // (2 * k))), 16)
    off = col_off // tn

    def body(a_ref, b_ref, *rest):
        out_ref = rest[-1]

        @pl.when(pl.program_id(1) == 0)
        def _():
            out_ref[...] = jnp.zeros_like(out_ref)

        out_ref[...] += _dot_tn(a_ref[...], b_ref[...])

    in_specs = [pl.BlockSpec((ts, k), lambda j, i: (i, 0)), pl.BlockSpec((ts, tn), lambda j, i: (i, j))]
    args = [a, b]
    aliases = {}
    if stack is not None:
        in_specs.append(ANY)
        args.append(stack)
        aliases = {2: 0}
    return pl.pallas_call(
        body,
        name=name,
        grid=(n // tn, s // ts),
        in_specs=in_specs,
        out_specs=pl.BlockSpec((None, k, tn), lambda j, i: (layer, 0, off + j)),
        out_shape=SDS((DEPTH, k, n_total), F32),
        input_output_aliases=aliases,
        compiler_params=_params("parallel", "arbitrary"),
    )(*args)


def _gm_mask():
    r = lax.broadcasted_iota(jnp.int32, (GM_BLOCK, GM_BLOCK), 0) // CHUNK
    c = lax.broadcasted_iota(jnp.int32, (GM_BLOCK, GM_BLOCK), 1) // CHUNK
    return r >= c


def _gm_normed(v, lng, lnb):
    vg = _gelu(v)
    mu = jnp.mean(vg, axis=-1, keepdims=True)
    xc = vg - mu
    rstd = lax.rsqrt(jnp.mean(xc * xc, axis=-1, keepdims=True) + EPS)
    xh = xc * rstd
    return xh, rstd, xh * lng + lnb


def _gmlp_fwd(z, lng, lnb, ws, bst, name):
    s = z.shape[0]
    tg = _divisor(s, 256, GM_BLOCK)

    def body(z_ref, lng_ref, lnb_ref, ws_ref, bst_ref, a_ref):
        mask = _gm_mask()
        ug = _gelu(z_ref[:, :BR_DIM])
        _, _, vn = _gm_normed(z_ref[:, BR_DIM:], lng_ref[...], lnb_ref[...])
        vn = vn.astype(MM_DTYPE)
        for h in range(N_HEADS):
            cs = slice(h * HEAD_DIM, (h + 1) * HEAD_DIM)
            wm = jnp.where(mask, ws_ref[h], 0.0).astype(MM_DTYPE)
            for blk in range(tg // GM_BLOCK):
                rs = slice(blk * GM_BLOCK, (blk + 1) * GM_BLOCK)
                mixed = _dot(wm, vn[rs, cs]) + bst_ref[:, h : h + 1]
                a_ref[rs, cs] = (ug[rs, cs] * mixed).astype(MM_DTYPE)

    return pl.pallas_call(
        body,
        name=name,
        grid=(s // tg,),
        in_specs=[
            pl.BlockSpec((tg, 2 * BR_DIM), lambda i: (i, 0)),
            _resident((1, BR_DIM), lambda i: (0, 0)),
            _resident((1, BR_DIM), lambda i: (0, 0)),
            _resident((N_HEADS, GM_BLOCK, GM_BLOCK), lambda i: (0, 0, 0)),
            _resident((GM_BLOCK, N_HEADS), lambda i: (0, 0)),
        ],
        out_specs=pl.BlockSpec((tg, BR_DIM), lambda i: (i, 0)),
        out_shape=SDS((s, BR_DIM), MM_DTYPE),
        compiler_params=_params("parallel"),
    )(z, lng, lnb, ws, bst)


def _gmlp_bwd(da, z, lng, lnb, ws, bst, name):
    s = z.shape[0]
    tg = _divisor(s, 256, GM_BLOCK)

    def body(da_ref, z_ref, lng_ref, lnb_ref, ws_ref, bst_ref, dz_ref, dlng_ref, dlnb_ref, dws_ref, dbst_ref):
        @pl.when(pl.program_id(0) == 0)
        def _():
            dlng_ref[...] = jnp.zeros_like(dlng_ref)
            dlnb_ref[...] = jnp.zeros_like(dlnb_ref)
            dws_ref[...] = jnp.zeros_like(dws_ref)
            dbst_ref[...] = jnp.zeros_like(dbst_ref)

        mask = _gm_mask()
        u = z_ref[:, :BR_DIM]
        v = z_ref[:, BR_DIM:]
        ug = _gelu(u)
        lng_v = lng_ref[...]
        xh, rstd, vn = _gm_normed(v, lng_v, lnb_ref[...])
        vnb = vn.astype(MM_DTYPE)
        da_v = da_ref[...]
        dmix = da_v * ug
        dmixb = dmix.astype(MM_DTYPE)
        dvn_cols = []
        for h in range(N_HEADS):
            cs = slice(h * HEAD_DIM, (h + 1) * HEAD_DIM)
            wm = jnp.where(mask, ws_ref[h], 0.0).astype(MM_DTYPE)
            dw = jnp.zeros((GM_BLOCK, GM_BLOCK), F32)
            dbias = jnp.zeros((GM_BLOCK, 1), F32)
            dvn_rows = []
            for blk in range(tg // GM_BLOCK):
                rs = slice(blk * GM_BLOCK, (blk + 1) * GM_BLOCK)
                mixed = _dot(wm, vnb[rs, cs]) + bst_ref[:, h : h + 1]
                dz_ref[rs, cs] = (da_v[rs, cs] * mixed * _gelu_grad(u[rs, cs])).astype(MM_DTYPE)
                dw = dw + _dot_nt(dmixb[rs, cs], vnb[rs, cs])
                dbias = dbias + jnp.sum(dmix[rs, cs], axis=1, keepdims=True)
                dvn_rows.append(_dot_tn(wm, dmixb[rs, cs]))
            dws_ref[h] += jnp.where(mask, dw, 0.0)
            dbst_ref[:, h : h + 1] += dbias
            dvn_cols.append(jnp.concatenate(dvn_rows, axis=0) if len(dvn_rows) > 1 else dvn_rows[0])
        dvn = jnp.concatenate(dvn_cols, axis=1)
        dlng_ref[...] += jnp.sum(dvn * xh, axis=0, keepdims=True)
        dlnb_ref[...] += jnp.sum(dvn, axis=0, keepdims=True)
        dxh = dvn * lng_v
        dvg = rstd * (dxh - jnp.mean(dxh, axis=-1, keepdims=True) - xh * jnp.mean(dxh * xh, axis=-1, keepdims=True))
        dz_ref[:, BR_DIM:] = (dvg * _gelu_grad(v)).astype(MM_DTYPE)

    return pl.pallas_call(
        body,
        name=name,
        grid=(s // tg,),
        in_specs=[
            pl.BlockSpec((tg, BR_DIM), lambda i: (i, 0)),
            pl.BlockSpec((tg, 2 * BR_DIM), lambda i: (i, 0)),
            _resident((1, BR_DIM), lambda i: (0, 0)),
            _resident((1, BR_DIM), lambda i: (0, 0)),
            _resident((N_HEADS, GM_BLOCK, GM_BLOCK), lambda i: (0, 0, 0)),
            _resident((GM_BLOCK, N_HEADS), lambda i: (0, 0)),
        ],
        out_specs=[
            pl.BlockSpec((tg, 2 * BR_DIM), lambda i: (i, 0)),
            pl.BlockSpec((1, BR_DIM), lambda i: (0, 0)),
            pl.BlockSpec((1, BR_DIM), lambda i: (0, 0)),
            pl.BlockSpec((N_HEADS, GM_BLOCK, GM_BLOCK), lambda i: (0, 0, 0)),
            pl.BlockSpec((GM_BLOCK, N_HEADS), lambda i: (0, 0)),
        ],
        out_shape=[
            SDS((s, 2 * BR_DIM), MM_DTYPE),
            SDS((1, BR_DIM), F32),
            SDS((1, BR_DIM), F32),
            SDS((N_HEADS, GM_BLOCK, GM_BLOCK), F32),
            SDS((GM_BLOCK, N_HEADS), F32),
        ],
        compiler_params=_params("arbitrary"),
    )(da, z, lng, lnb, ws, bst)


HG_ROWS = 256
HG_UNROLL = 4
MID = CHUNK // 2 - 1


def _tri(lower):
    r = lax.broadcasted_iota(jnp.int32, (CHUNK, CHUNK), 0)
    c = lax.broadcasted_iota(jnp.int32, (CHUNK, CHUNK), 1)
    return r >= c if lower else c >= r


def _scan_rows(x, reverse=False):
    n = x.shape[0]
    rid = lax.broadcasted_iota(jnp.int32, x.shape, 0)
    k = 1
    while k < n:
        if reverse:
            x = x + jnp.where(rid < n - k, pltpu.roll(x, n - k, 0), 0.0)
        else:
            x = x + jnp.where(rid >= k, pltpu.roll(x, k, 0), 0.0)
        k *= 2
    return x


def _hgrn_fwd(z, lb, ng, name, rider=None):
    s = z.shape[0]
    rows_per = _divisor(s, HG_ROWS, CHUNK)
    n_sub = rows_per // CHUNK

    def body(zqf_ref, zio_ref, lb_ref, ng_ref, o_ref, b_ref, st_ref, state):
        @pl.when(pl.program_id(0) == 0)
        def _():
            state[...] = jnp.zeros_like(state)

        low = _tri(True)
        ng_v = ng_ref[...]

        def chunk(ci, carry):
            rows = pl.ds(pl.multiple_of(ci * CHUNK, CHUNK), CHUNK)
            for h in range(N_HEADS):
                cs = slice(h * HEAD_DIM, (h + 1) * HEAD_DIM)
                cs2 = slice(BR_DIM + h * HEAD_DIM, BR_DIM + (h + 1) * HEAD_DIM)
                zq = zqf_ref[rows, cs]
                zf = zqf_ref[rows, cs2]
                vi = zio_ref[rows, cs]
                zog = zio_ref[rows, cs2]
                lbh = lb_ref[:, cs]
                qf = zq * _sigmoid(zq)
                forget = lbh + (1.0 - lbh) * _sigmoid(zf)
                g = jnp.log(jnp.maximum(forget, TINY))
                k = (1.0 - lbh) * _sigmoid(-zf)
                gc = _scan_rows(g)
                gm = gc[MID : MID + 1]
                gl = gc[CHUNK - 1 : CHUNK]
                a = jnp.where(low, _dot_nt(qf * jnp.exp(gc - gm), k * jnp.exp(gm - gc)), 0.0)
                st = state[h]
                st_ref[ci, h] = st
                o = _dot(a, vi) + _dot_nt(qf * jnp.exp(gc), st)
                state[h] = st * jnp.exp(gl) + _dot_tn(vi, k * jnp.exp(gl - gc))
                r = lax.rsqrt(jnp.mean(o * o, axis=-1, keepdims=True) + EPS)
                o_ref[rows, cs] = o
                b_ref[rows, cs] = (((o * r) * ng_v) * (zog * _sigmoid(zog))).astype(MM_DTYPE)
            return carry

        lax.fori_loop(0, n_sub, chunk, 0, unroll=math.gcd(n_sub, HG_UNROLL))

    return _hosted_call(
        body,
        rider,
        name=name,
        grid=(s // rows_per,),
        in_specs=[
            pl.BlockSpec((rows_per, 2 * BR_DIM), lambda i: (i, 1)),
            pl.BlockSpec((rows_per, 2 * BR_DIM), lambda i: (i, 2)),
            _resident((1, BR_DIM), lambda i: (0, 0)),
            _resident((1, HEAD_DIM), lambda i: (0, 0)),
        ],
        out_specs=[
            pl.BlockSpec((rows_per, BR_DIM), lambda i: (i, 0)),
            pl.BlockSpec((rows_per, BR_DIM), lambda i: (i, 0)),
            pl.BlockSpec((n_sub, N_HEADS, HEAD_DIM, HEAD_DIM), lambda i: (i, 0, 0, 0)),
        ],
        out_shape=[
            SDS((s, BR_DIM), F32),
            SDS((s, BR_DIM), MM_DTYPE),
            SDS((s // CHUNK, N_HEADS, HEAD_DIM, HEAD_DIM), F32),
        ],
        scratch_shapes=[pltpu.VMEM((N_HEADS, HEAD_DIM, HEAD_DIM), F32)],
        args=[z, z, lb, ng],
        semantics=("arbitrary",),
    )


def _hgrn_bwd(db, z, o, states, lb, ng, name, rider=None):
    s = z.shape[0]
    rows_per = _divisor(s, HG_ROWS, CHUNK)
    n_sub = rows_per // CHUNK
    n_steps = s // rows_per

    def body(db_ref, zqf_ref, zio_ref, o_ref, st_ref, lb_ref, ng_ref, dz_ref, dlb_ref, dng_ref, dstate):
        @pl.when(pl.program_id(0) == 0)
        def _():
            dstate[...] = jnp.zeros_like(dstate)
            dlb_ref[...] = jnp.zeros_like(dlb_ref)
            dng_ref[...] = jnp.zeros_like(dng_ref)

        low = _tri(True)
        ng_v = ng_ref[...]

        def chunk(cc, carry):
            ci = n_sub - 1 - cc
            rows = pl.ds(pl.multiple_of(ci * CHUNK, CHUNK), CHUNK)
            for h in range(N_HEADS):
                cs = slice(h * HEAD_DIM, (h + 1) * HEAD_DIM)
                cs2 = slice(BR_DIM + h * HEAD_DIM, BR_DIM + (h + 1) * HEAD_DIM)
                zq = zqf_ref[rows, cs]
                zf = zqf_ref[rows, cs2]
                vi = zio_ref[rows, cs]
                zog = zio_ref[rows, cs2]
                lbh = lb_ref[:, cs]
                ov = o_ref[rows, cs]
                dbv = db_ref[rows, cs]
                r = lax.rsqrt(jnp.mean(ov * ov, axis=-1, keepdims=True) + EPS)
                on = ov * r
                sgo = _sigmoid(zog)
                gate = zog * sgo
                dng_ref[...] += jnp.sum(dbv * gate * on, axis=0, keepdims=True)
                don = dbv * gate * ng_v
                dzog = dbv * (on * ng_v) * _silu_grad(zog, sgo)
                do = r * (don - on * jnp.mean(don * on, axis=-1, keepdims=True))
                sq = _sigmoid(zq)
                qf = zq * sq
                sg = _sigmoid(zf)
                sgn = _sigmoid(-zf)
                oml = 1.0 - lbh
                forget = lbh + oml * sg
                fm = jnp.maximum(forget, TINY)
                k = oml * sgn
                gc = _scan_rows(jnp.log(fm))
                gm = gc[MID : MID + 1]
                gl = gc[CHUNK - 1 : CHUNK]
                e_g = jnp.exp(gc)
                e_q = jnp.exp(gc - gm)
                e_k = jnp.exp(gm - gc)
                e_kd = jnp.exp(gl - gc)
                e_gl = jnp.exp(gl)
                qt = qf * e_q
                kt = k * e_k
                a = jnp.where(low, _dot_nt(qt, kt), 0.0)
                st = st_ref[ci, h]
                dst = dstate[h]
                dp = jnp.where(low, _dot_nt(do, vi), 0.0)
                dv = _dot_tn(a, do) + _dot_nt(k * e_kd, dst)
                dq = _dot(dp, kt) * e_q + e_g * _dot(do, st)
                dks = e_kd * _dot(vi, dst)
                dk = _dot_tn(dp, qt) * e_k + dks
                dstate[h] = _dot_tn(do, qf * e_g) + dst * e_gl
                dgc = qf * dq - k * dk
                extra = e_gl * jnp.sum(st * dst, axis=0, keepdims=True) + jnp.sum(k * dks, axis=0, keepdims=True)
                dg = _scan_rows(dgc, reverse=True) + extra
                dforget = jnp.where(forget > TINY, dg / fm, 0.0)
                both = dforget - dk
                dlb_ref[:, cs] += jnp.sum(sgn * both, axis=0, keepdims=True)
                dz_ref[rows, cs] = (dq * _silu_grad(zq, sq)).astype(MM_DTYPE)
                dz_ref[rows, cs2] = (oml * sg * sgn * both).astype(MM_DTYPE)
                dz_ref[rows, 2 * BR_DIM + h * HEAD_DIM : 2 * BR_DIM + (h + 1) * HEAD_DIM] = dv.astype(MM_DTYPE)
                dz_ref[rows, 3 * BR_DIM + h * HEAD_DIM : 3 * BR_DIM + (h + 1) * HEAD_DIM] = dzog.astype(MM_DTYPE)
            return carry

        lax.fori_loop(0, n_sub, chunk, 0, unroll=math.gcd(n_sub, HG_UNROLL))

    rev = lambda i: n_steps - 1 - i
    return _hosted_call(
        body,
        rider,
        name=name,
        grid=(n_steps,),
        in_specs=[
            pl.BlockSpec((rows_per, BR_DIM), lambda i: (rev(i), 0)),
            pl.BlockSpec((rows_per, 2 * BR_DIM), lambda i: (rev(i), 1)),
            pl.BlockSpec((rows_per, 2 * BR_DIM), lambda i: (rev(i), 2)),
            pl.BlockSpec((rows_per, BR_DIM), lambda i: (rev(i), 0)),
            pl.BlockSpec((n_sub, N_HEADS, HEAD_DIM, HEAD_DIM), lambda i: (rev(i), 0, 0, 0)),
            _resident((1, BR_DIM), lambda i: (0, 0)),
            _resident((1, HEAD_DIM), lambda i: (0, 0)),
        ],
        out_specs=[
            pl.BlockSpec((rows_per, 4 * BR_DIM), lambda i: (rev(i), 0)),
            pl.BlockSpec((1, BR_DIM), lambda i: (0, 0)),
            pl.BlockSpec((1, HEAD_DIM), lambda i: (0, 0)),
        ],
        out_shape=[SDS((s, 4 * BR_DIM), MM_DTYPE), SDS((1, BR_DIM), F32), SDS((1, HEAD_DIM), F32)],
        scratch_shapes=[pltpu.VMEM((N_HEADS, HEAD_DIM, HEAD_DIM), F32)],
        args=[db, z, z, o, states, lb, ng],
        semantics=("arbitrary",),
    )


def _mix_fwd(x, a, b, z, w_gm, w_hg, w_out, name, rider=None):
    s, d = x.shape
    tm = _divisor(s, 256, 16)
    g0 = 6 * BR_DIM // d

    def body(x_ref, a_ref, b_ref, ga_ref, gb_ref, wgm_ref, whg_ref, wout_ref, out_ref):
        y = _sigmoid(ga_ref[...]) * _dot(a_ref[...], wgm_ref[...]) + _sigmoid(gb_ref[...]) * _dot(b_ref[...], whg_ref[...])
        out_ref[...] = x_ref[...] + _dot(y, wout_ref[...])

    return _hosted_call(
        body,
        rider,
        name=name,
        grid=(s // tm,),
        in_specs=[
            pl.BlockSpec((tm, d), lambda i: (i, 0)),
            pl.BlockSpec((tm, BR_DIM), lambda i: (i, 0)),
            pl.BlockSpec((tm, BR_DIM), lambda i: (i, 0)),
            pl.BlockSpec((tm, d), lambda i: (i, g0)),
            pl.BlockSpec((tm, d), lambda i: (i, g0 + 1)),
            _resident((BR_DIM, d), lambda i: (0, 0)),
            _resident((BR_DIM, d), lambda i: (0, 0)),
            _resident((d, d), lambda i: (0, 0)),
        ],
        out_specs=[pl.BlockSpec((tm, d), lambda i: (i, 0))],
        out_shape=[SDS((s, d), F32)],
        scratch_shapes=[],
        args=[x, a, b, z, z, w_gm, w_hg, w_out],
        semantics=("arbitrary",),
    )


def _mix_bwd(dx, a, b, z, w_gm, w_hg, w_out, name, rider=None):
    s, d = dx.shape
    tm = _divisor(s, 256, 16)
    g0 = 6 * BR_DIM // d

    def body(dx_ref, a_ref, b_ref, ga_ref, gb_ref, wgm_ref, whg_ref, wout_ref, y_ref, dpa_ref, dpb_ref, da_ref, db_ref, dg_ref):
        dy = _dot_nt(dx_ref[...], wout_ref[...])
        pa = _dot(a_ref[...], wgm_ref[...])
        pb = _dot(b_ref[...], whg_ref[...])
        sa = _sigmoid(ga_ref[...])
        sb = _sigmoid(gb_ref[...])
        y_ref[...] = (sa * pa + sb * pb).astype(MM_DTYPE)
        dpa = (dy * sa).astype(MM_DTYPE)
        dpb = (dy * sb).astype(MM_DTYPE)
        dpa_ref[...] = dpa
        dpb_ref[...] = dpb
        da_ref[...] = _dot_nt(dpa, wgm_ref[...])
        db_ref[...] = _dot_nt(dpb, whg_ref[...])
        dg_ref[:, :d] = (dy * pa * sa * (1.0 - sa)).astype(MM_DTYPE)
        dg_ref[:, d:] = (dy * pb * sb * (1.0 - sb)).astype(MM_DTYPE)

    row = lambda w: pl.BlockSpec((tm, w), lambda i: (i, 0))
    return _hosted_call(
        body,
        rider,
        name=name,
        grid=(s // tm,),
        in_specs=[
            row(d),
            row(BR_DIM),
            row(BR_DIM),
            pl.BlockSpec((tm, d), lambda i: (i, g0)),
            pl.BlockSpec((tm, d), lambda i: (i, g0 + 1)),
            _resident((BR_DIM, d), lambda i: (0, 0)),
            _resident((BR_DIM, d), lambda i: (0, 0)),
            _resident((d, d), lambda i: (0, 0)),
        ],
        out_specs=[row(d), row(d), row(d), row(BR_DIM), row(BR_DIM), row(2 * d)],
        out_shape=[
            SDS((s, d), MM_DTYPE),
            SDS((s, d), MM_DTYPE),
            SDS((s, d), MM_DTYPE),
            SDS((s, BR_DIM), F32),
            SDS((s, BR_DIM), F32),
            SDS((s, 2 * d), MM_DTYPE),
        ],
        scratch_shapes=[],
        args=[dx, a, b, z, z, w_gm, w_hg, w_out],
        semantics=("arbitrary",),
    )


HALO = 8
FFN_ROWS = 256


def _shift_down(v, prev, n):
    rolled = pltpu.roll(v, n, 0)
    rid = lax.broadcasted_iota(jnp.int32, v.shape, 0)
    out = rolled
    for r in range(n):
        out = jnp.where(rid == r, prev[HALO - n + r : HALO - n + r + 1], out)
    return out


def _shift_up(v, nxt, n):
    tm = v.shape[0]
    rolled = pltpu.roll(v, tm - n, 0)
    rid = lax.broadcasted_iota(jnp.int32, v.shape, 0)
    out = rolled
    for r in range(n):
        out = jnp.where(rid == tm - n + r, nxt[r : r + 1], out)
    return out


def _conv_cols(f):
    return _divisor(f, 256, 128)


def _ffn_fwd(x, gain, w_up, cw, cb, w_down, layer, name, rider=None):
    s, d = x.shape
    f = w_down.shape[0]
    tm = _divisor(s, FFN_ROWS, 16)
    cwid = _conv_cols(f)

    def body(x_ref, g_ref, wu_ref, cw_ref, cb_ref, wd_ref, out_ref, z_ref, h_ref, conv_ref, halo):
        @pl.when(pl.program_id(0) == 0)
        def _():
            halo[...] = jnp.zeros_like(halo)

        xv = x_ref[...]
        r = lax.rsqrt(jnp.mean(xv * xv, axis=-1, keepdims=True) + EPS)
        h = ((xv * r) * g_ref[...]).astype(MM_DTYPE)
        h_ref[...] = h
        acc = xv
        starts = list(range(0, f, cwid))

        def project(c0):
            return [jnp.dot(h, wu_ref[:, base : base + cwid], preferred_element_type=F32) for base in (c0, f + c0)]

        ahead = project(starts[0])
        for n, c0 in enumerate(starts):
            halves = []
            current = ahead
            if n + 1 < len(starts):
                ahead = project(starts[n + 1])
            for base, zc in zip((c0, f + c0), current):
                cols = slice(base, base + cwid)
                z_ref[:, cols] = zc.astype(MM_DTYPE)
                prev = halo[:, cols]
                conv = cb_ref[:, cols] + cw_ref[0:1, cols] * _shift_down(zc, prev, 2)
                conv = conv + cw_ref[1:2, cols] * _shift_down(zc, prev, 1) + cw_ref[2:3, cols] * zc
                halo[:, cols] = zc[tm - HALO : tm]
                conv_ref[:, cols] = conv.astype(MM_DTYPE)
                halves.append(conv)
            gate, val = halves
            act = gate * _sigmoid(gate) * val
            acc = acc + _dot(act, wd_ref[c0 : c0 + cwid, :])
        out_ref[...] = acc

    row = lambda w: pl.BlockSpec((tm, w), lambda i: (i, 0))
    return _hosted_call(
        body,
        rider,
        name=name,
        grid=(s // tm,),
        in_specs=[
            row(d),
            _resident((1, d), lambda i: (0, 0)),
            _resident((d, 2 * f), lambda i: (0, 0)),
            _resident((None, 3, 2 * f), lambda i: (layer, 0, 0)),
            _resident((1, 2 * f), lambda i: (0, 0)),
            _resident((f, d), lambda i: (0, 0)),
        ],
        out_specs=[row(d), row(2 * f), row(d), row(2 * f)],
        out_shape=[SDS((s, d), F32), SDS((s, 2 * f), MM_DTYPE), SDS((s, d), MM_DTYPE), SDS((s, 2 * f), MM_DTYPE)],
        scratch_shapes=[pltpu.VMEM((HALO, 2 * f), F32)],
        args=[x, gain, w_up, cw, cb, w_down],
        semantics=("arbitrary",),
    )


def _ffn_bwd(dx, z2, conv, cw, w_down, w_up, x, gain, layer, name):
    s, d = dx.shape
    f = z2.shape[1] // 2
    tm = _divisor(s, 256, 16)
    cwid = _conv_cols(f)
    n_steps = s // tm

    def body(dx_ref, z_ref, conv_ref, cw_ref, wd_ref, wu_ref, x_ref, g_ref, dz_ref, act_ref, dcw_ref, dx1_ref, dg_ref, nxt):
        @pl.when(pl.program_id(0) == 0)
        def _():
            nxt[...] = jnp.zeros_like(nxt)
            dcw_ref[...] = jnp.zeros_like(dcw_ref)
            dg_ref[...] = jnp.zeros_like(dg_ref)

        dxf = dx_ref[...]
        dxv = dxf.astype(MM_DTYPE)
        starts = list(range(0, f, cwid))
        ahead = _dot_nt(dxv, wd_ref[0:cwid, :])
        dh = jnp.zeros((tm, d), F32)
        pending = []
        for n, c0 in enumerate(starts):
            dact = ahead
            if n + 1 < len(starts):
                ahead = _dot_nt(dxv, wd_ref[starts[n + 1] : starts[n + 1] + cwid, :])
            for dz_prev, cols_prev in pending:
                dh = dh + _dot_nt(dz_prev, wu_ref[:, cols_prev])
            pending = []
            gate = conv_ref[:, c0 : c0 + cwid].astype(F32)
            val = conv_ref[:, f + c0 : f + c0 + cwid].astype(F32)
            sg = _sigmoid(gate)
            silu = gate * sg
            act_ref[:, c0 : c0 + cwid] = (silu * val).astype(MM_DTYPE)
            dconvs = (dact * val * _silu_grad(gate, sg), dact * silu)
            for base, dconv in zip((c0, f + c0), dconvs):
                cols = slice(base, base + cwid)
                zc = z_ref[:, cols].astype(F32)
                nx = nxt[:, cols]
                up1 = _shift_up(dconv, nx, 1)
                up2 = _shift_up(dconv, nx, 2)
                dcw_ref[0:1, cols] += jnp.sum(up2 * zc, axis=0, keepdims=True)
                dcw_ref[1:2, cols] += jnp.sum(up1 * zc, axis=0, keepdims=True)
                dcw_ref[2:3, cols] += jnp.sum(dconv * zc, axis=0, keepdims=True)
                dcw_ref[3:4, cols] += jnp.sum(dconv, axis=0, keepdims=True)
                dz = (cw_ref[2:3, cols] * dconv + cw_ref[1:2, cols] * up1 + cw_ref[0:1, cols] * up2).astype(MM_DTYPE)
                dz_ref[:, cols] = dz
                nxt[:, cols] = dconv[0:HALO]
                pending.append((dz, cols))
        for dz_prev, cols_prev in pending:
            dh = dh + _dot_nt(dz_prev, wu_ref[:, cols_prev])
        xv = x_ref[...]
        r = lax.rsqrt(jnp.mean(xv * xv, axis=-1, keepdims=True) + EPS)
        xh = xv * r
        dg_ref[...] += jnp.sum(dh * xh, axis=0, keepdims=True)
        dxh = dh * g_ref[...]
        dx1_ref[...] = dxf + r * (dxh - xh * jnp.mean(dxh * xh, axis=-1, keepdims=True))

    rev = lambda i: n_steps - 1 - i
    row = lambda w: pl.BlockSpec((tm, w), lambda i: (rev(i), 0))
    return pl.pallas_call(
        body,
        name=name,
        grid=(n_steps,),
        in_specs=[
            row(d),
            row(2 * f),
            row(2 * f),
            _resident((None, 3, 2 * f), lambda i: (layer, 0, 0)),
            _resident((f, d), lambda i: (0, 0)),
            _resident((d, 2 * f), lambda i: (0, 0)),
            row(d),
            _resident((1, d), lambda i: (0, 0)),
        ],
        out_specs=[
            row(2 * f),
            row(f),
            pl.BlockSpec((HALO, 2 * f), lambda i: (0, 0)),
            row(d),
            pl.BlockSpec((1, d), lambda i: (0, 0)),
        ],
        out_shape=[SDS((s, 2 * f), MM_DTYPE), SDS((s, f), MM_DTYPE), SDS((HALO, 2 * f), F32), SDS((s, d), F32), SDS((1, d), F32)],
        scratch_shapes=[pltpu.VMEM((HALO, 2 * f), F32)],
        compiler_params=_params("arbitrary"),
    )(dx, z2, conv, cw, w_down, w_up, x, gain)


def _loss_head(x, gain, target):
    s, d = x.shape
    tm = _divisor(s, 256, 8)

    def body(x_ref, g_ref, t_ref, loss_ref, dx_ref, dg_ref):
        @pl.when(pl.program_id(0) == 0)
        def _():
            loss_ref[...] = jnp.zeros_like(loss_ref)
            dg_ref[...] = jnp.zeros_like(dg_ref)

        xv = x_ref[...]
        gv = g_ref[...]
        r = lax.rsqrt(jnp.mean(xv * xv, axis=-1, keepdims=True) + EPS)
        xh = xv * r
        err = xh * gv - t_ref[...]
        loss_ref[...] += 0.5 * jnp.sum(jnp.mean(err * err, axis=-1, keepdims=True))
        dout = err * (1.0 / d)
        dg_ref[...] += jnp.sum(dout * xh, axis=0, keepdims=True)
        dxh = dout * gv
        dx_ref[...] = r * (dxh - xh * jnp.mean(dxh * xh, axis=-1, keepdims=True))

    return pl.pallas_call(
        body,
        name="loss_head",
        grid=(s // tm,),
        in_specs=[
            pl.BlockSpec((tm, d), lambda i: (i, 0)),
            _resident((1, d), lambda i: (0, 0)),
            pl.BlockSpec((tm, d), lambda i: (i, 0)),
        ],
        out_specs=[
            pl.BlockSpec((8, 128), lambda i: (0, 0)),
            pl.BlockSpec((tm, d), lambda i: (i, 0)),
            pl.BlockSpec((1, d), lambda i: (0, 0)),
        ],
        out_shape=[SDS((8, 128), F32), SDS((s, d), F32), SDS((1, d), F32)],
        compiler_params=_params("arbitrary"),
    )(x, gain, target)


def _adamw(w, g, m, v, name):
    shape = w.shape
    cols = shape[-1]
    rows = max(1, math.prod(shape[:-1]))
    tr = _divisor(rows, max(8, TILE_BYTES // (4 * cols)), 8)
    flat = [t.reshape(rows, cols) for t in (w, g, m, v)]

    def body(w_ref, g_ref, m_ref, v_ref, d_ref, nm_ref, nv_ref):
        gv = g_ref[...]
        m2 = ADAM_B1 * m_ref[...] + (1.0 - ADAM_B1) * gv
        v2 = ADAM_B2 * v_ref[...] + (1.0 - ADAM_B2) * (gv * gv)
        m_hat = m2 / (1.0 - ADAM_B1**ADAM_STEP)
        v_hat = v2 / (1.0 - ADAM_B2**ADAM_STEP)
        d_ref[...] = -ADAM_LR * (m_hat / (jnp.sqrt(v_hat) + ADAM_EPS) + ADAM_WD * w_ref[...])
        nm_ref[...] = m2
        nv_ref[...] = v2

    spec = pl.BlockSpec((tr, cols), lambda i: (i, 0))
    outs = pl.pallas_call(
        body,
        name=name,
        grid=(rows // tr,),
        in_specs=[spec] * 4,
        out_specs=[spec] * 3,
        out_shape=[SDS((rows, cols), F32)] * 3,
        compiler_params=_params("parallel"),
    )(*flat)
    return tuple(t.reshape(shape) for t in outs)


def _position():
    return lax.axis_index("x"), lax.axis_index("y"), lax.axis_index("c")


def _other_chips(x, y):
    return [(1 - x, y), (x, 1 - y), (1 - x, 1 - y)]


def _remote(src, dst, send_sem, recv_sem, device):
    return pltpu.make_async_remote_copy(
        src_ref=src, dst_ref=dst, send_sem=send_sem, recv_sem=recv_sem, device_id=device, device_id_type=MESH
    )


def _sub(ref, lead, shape, axis, chip=None, half=None):
    cut = [pl.ds(0, shape[0]), pl.ds(0, shape[1])]
    if chip is not None:
        size = shape[axis] // N_CHIPS
        cut[axis] = pl.ds(chip * size, size)
    if half is not None:
        size = shape[1 - axis] // 2
        cut[1 - axis] = pl.ds(half * size, size)
    return ref.at[(*lead, *cut)]


def _scaled(shape, axis, num, den=1):
    return tuple(d * num // den if i == axis else d for i, d in enumerate(shape))


def _gather_rider(shards, axes, layer, conv_w=None):
    n = len(shards)
    shard2d = [t.shape[1:] for t in shards]
    full2d = [_scaled(s2, ax, N_CHIPS) for s2, ax in zip(shard2d, axes)]
    out_shapes = [SDS(f2, t.dtype) for f2, t in zip(full2d, shards)]
    inputs = list(shards)
    own = 6
    scratch = [pltpu.SemaphoreType.DMA((n, 7)), pltpu.SemaphoreType.DMA((n, 7))]
    if conv_w is not None:
        cshape = conv_w.shape
        inputs.append(conv_w)
        out_shapes.append(SDS(_scaled(cshape, 2, N_CHIPS), conv_w.dtype))
        scratch += [pltpu.SemaphoreType.DMA((4,)), pltpu.SemaphoreType.DMA((4,))]

    def conv_slab(ref, chip):
        return ref.at[:, :, pl.ds((2 * chip[0] + chip[1]) * cshape[2], cshape[2])]

    def own_copy(ins, outs, send_sems, recv_sems, k):
        x, y, c = _position()
        slab = _sub(outs[k], (), full2d[k], axes[k], chip=2 * x + y)
        return _remote(ins[k].at[layer], slab, send_sems.at[k, own], recv_sems.at[k, own], (x, y, 1 - c))

    def start(ins, outs, scr):
        send_sems, recv_sems = scr[:2]
        x, y, c = _position()
        me = 2 * x + y
        for k in range(n):
            own_copy(ins, outs, send_sems, recv_sems, k).start()
            for j, chip in enumerate(_other_chips(x, y)):
                _remote(
                    _sub(ins[k], (layer,), shard2d[k], axes[k], half=c),
                    _sub(outs[k], (), full2d[k], axes[k], chip=me, half=c),
                    send_sems.at[k, j], recv_sems.at[k, j], (*chip, c),
                ).start()
        if conv_w is not None:
            csend, crecv = scr[2:]
            _remote(ins[n], conv_slab(outs[n], (x, y)), csend.at[3], crecv.at[3], (x, y, 1 - c)).start()
            for j, chip in enumerate(_other_chips(x, y)):
                _remote(ins[n], conv_slab(outs[n], (x, y)), csend.at[j], crecv.at[j], (*chip, c)).start()

    def finish(ins, outs, scr):
        send_sems, recv_sems = scr[:2]
        x, y, c = _position()
        me = 2 * x + y
        sibling = (x, y, 1 - c)
        chips = _other_chips(x, y)
        forwards = []
        for k in range(n):
            src = _sub(ins[k], (layer,), shard2d[k], axes[k], half=c)
            for j, chip in enumerate(chips):
                landed = _sub(outs[k], (), full2d[k], axes[k], chip=2 * chip[0] + chip[1], half=c)
                _remote(src, landed, send_sems.at[k, j], recv_sems.at[k, j], (*chip, c)).wait_recv()
                fwd = _remote(landed, landed, send_sems.at[k, 3 + j], recv_sems.at[k, 3 + j], sibling)
                fwd.start()
                forwards.append(fwd)
        for k in range(n):
            src = _sub(ins[k], (layer,), shard2d[k], axes[k], half=c)
            for j, chip in enumerate(chips):
                theirs = _sub(outs[k], (), full2d[k], axes[k], chip=2 * chip[0] + chip[1], half=1 - c)
                _remote(src, theirs, send_sems.at[k, 3 + j], recv_sems.at[k, 3 + j], sibling).wait_recv()
        for fwd in forwards:
            fwd.wait_send()
        for k in range(n):
            src = _sub(ins[k], (layer,), shard2d[k], axes[k], half=c)
            mine = _sub(outs[k], (), full2d[k], axes[k], chip=me, half=c)
            for j, chip in enumerate(chips):
                _remote(src, mine, send_sems.at[k, j], recv_sems.at[k, j], (*chip, c)).wait_send()
            own_copy(ins, outs, send_sems, recv_sems, k).wait()
        if conv_w is not None:
            csend, crecv = scr[2:]
            for j, chip in enumerate(chips):
                cp = _remote(ins[n], conv_slab(outs[n], chip), csend.at[j], crecv.at[j], (*chip, c))
                cp.wait_recv()
                cp.wait_send()
            _remote(ins[n], conv_slab(outs[n], (x, y)), csend.at[3], crecv.at[3], sibling).wait()

    return _Rider(inputs, out_shapes, scratch, start, finish)


def _pair_rider(stacks, axes, layer):
    n = len(stacks)
    shapes = [t.shape[1:] for t in stacks]
    out_shapes = [SDS(_scaled(s2, 1 - ax, 1, 2), F32) for s2, ax in zip(shapes, axes)]
    scratch = [pltpu.SemaphoreType.DMA((n,)), pltpu.SemaphoreType.DMA((n,))]

    def copies(ins, outs, scr):
        x, y, c = _position()
        return [
            _remote(_sub(ins[k], (layer,), shapes[k], axes[k], half=1 - c), outs[k], scr[0].at[k], scr[1].at[k], (x, y, 1 - c))
            for k in range(n)
        ]

    def start(ins, outs, scr):
        for cp in copies(ins, outs, scr):
            cp.start()

    def finish(ins, outs, scr):
        for cp in copies(ins, outs, scr):
            cp.wait()

    return _Rider(stacks, out_shapes, scratch, start, finish)


def _chip_rider(pairs, axes):
    n = len(pairs)
    shapes = [t.shape for t in pairs]
    out_shapes = [SDS((3,) + _scaled(s2, ax, 1, N_CHIPS), t.dtype) for s2, ax, t in zip(shapes, axes, pairs)]
    scratch = [pltpu.SemaphoreType.DMA((n, 3)), pltpu.SemaphoreType.DMA((n, 3))]

    def copies(ins, outs, scr):
        x, y, c = _position()
        return [
            _remote(
                _sub(ins[k], (), shapes[k], axes[k], chip=2 * chip[0] + chip[1]), outs[k].at[j],
                scr[0].at[k, j], scr[1].at[k, j], (*chip, c),
            )
            for k in range(n)
            for j, chip in enumerate(_other_chips(x, y))
        ]

    def start(ins, outs, scr):
        for cp in copies(ins, outs, scr):
            cp.start()

    def finish(ins, outs, scr):
        for cp in copies(ins, outs, scr):
            cp.wait()

    return _Rider(pairs, out_shapes, scratch, start, finish)


def _complete_rider(stacks, axes, layers):
    n = len(stacks)
    shapes = [t.shape[1:] for t in stacks]
    scratch = [pltpu.SemaphoreType.DMA((n,)), pltpu.SemaphoreType.DMA((n,))]

    def start(ins, outs, scr):
        x, y, c = _position()
        for k in range(n):
            mine = _sub(outs[k], (layers[k],), shapes[k], axes[k], half=c)
            _remote(mine, mine, scr[0].at[k], scr[1].at[k], (x, y, 1 - c)).start()

    def finish(ins, outs, scr):
        x, y, c = _position()
        for k in range(n):
            mine = _sub(outs[k], (layers[k],), shapes[k], axes[k], half=c)
            theirs = _sub(outs[k], (layers[k],), shapes[k], axes[k], half=1 - c)
            _remote(mine, theirs, scr[0].at[k], scr[1].at[k], (x, y, 1 - c)).wait_recv()
            _remote(mine, mine, scr[0].at[k], scr[1].at[k], (x, y, 1 - c)).wait_send()

    return _Rider(stacks, [SDS(t.shape, t.dtype) for t in stacks], scratch, start, finish, {k: k for k in range(n)})


def _small_rider(buf):
    rows, cols = buf.shape
    flips = [(fx, fy, fc) for fx in (0, 1) for fy in (0, 1) for fc in (0, 1)][1:]

    def peers():
        x, y, c = _position()
        return [(x + f[0] - 2 * x * f[0], y + f[1] - 2 * y * f[1], c + f[2] - 2 * c * f[2]) for f in flips]

    def start(ins, outs, scr):
        x, y, c = _position()
        mine = outs[0].at[4 * x + 2 * y + c]
        pltpu.make_async_copy(ins[0], mine, scr[2]).start()
        for j, peer in enumerate(peers()):
            _remote(ins[0], mine, scr[0].at[j], scr[1].at[j], peer).start()

    def finish(ins, outs, scr):
        x, y, c = _position()
        mine = outs[0].at[4 * x + 2 * y + c]
        for j, peer in enumerate(peers()):
            _remote(ins[0], outs[0].at[4 * peer[0] + 2 * peer[1] + peer[2]], scr[0].at[j], scr[1].at[j], peer).wait_recv()
        for j, peer in enumerate(peers()):
            _remote(ins[0], mine, scr[0].at[j], scr[1].at[j], peer).wait_send()
        pltpu.make_async_copy(ins[0], mine, scr[2]).wait()

    scratch = [pltpu.SemaphoreType.DMA((7,)), pltpu.SemaphoreType.DMA((7,)), pltpu.SemaphoreType.DMA(())]
    return _Rider([buf], [SDS((8, rows, cols), buf.dtype)], scratch, start, finish)


def _my_core():
    return lax.axis_index("c")


def _my_chip():
    return 2 * lax.axis_index("x") + lax.axis_index("y")


def _pair_sum(stack, layer, recv, axis, name):
    hk, hn = recv.shape
    tk = _divisor(hk, max(16, TILE_BYTES // (4 * hn)), 16)
    per = hk // tk

    def body(g_ref, r_ref, out_ref):
        out_ref[...] = (g_ref[...] + r_ref[...]).astype(out_ref.dtype)

    if axis == 1:
        mine = pl.BlockSpec((None, tk, hn), lambda i: (layer, _my_core() * per + i, 0))
    else:
        mine = pl.BlockSpec((None, tk, hn), lambda i: (layer, i, _my_core()))
    return pl.pallas_call(
        body,
        name=name,
        grid=(per,),
        in_specs=[mine, pl.BlockSpec((tk, hn), lambda i: (i, 0))],
        out_specs=pl.BlockSpec((tk, hn), lambda i: (i, 0)),
        out_shape=SDS((hk, hn), BF16),
        compiler_params=_params("parallel"),
    )(stack, recv)


def _chip_sum(pair, others, axis, stack, layer, name):
    _, sk, sn = others.shape
    tk = _divisor(sk, max(16, TILE_BYTES // (4 * sn)), 16)
    per = sk // tk

    def body(p_ref, o0_ref, o1_ref, o2_ref, *rest):
        out_ref = rest[-1]
        acc = p_ref[...].astype(F32) + o0_ref[...].astype(F32)
        out_ref[...] = (acc + o1_ref[...].astype(F32)) + o2_ref[...].astype(F32)

    if axis == 1:
        own = pl.BlockSpec((tk, sn), lambda i: (i, _my_chip()))
        out = pl.BlockSpec((None, tk, sn), lambda i: (layer, _my_core() * per + i, 0))
        shard = (2 * sk, sn)
    else:
        own = pl.BlockSpec((tk, sn), lambda i: (_my_chip() * per + i, 0))
        out = pl.BlockSpec((None, tk, sn), lambda i: (layer, i, _my_core()))
        shard = (sk, 2 * sn)
    other = lambda j: pl.BlockSpec((None, tk, sn), lambda i: (j, i, 0))
    in_specs = [own, other(0), other(1), other(2)]
    args = [pair, others, others, others]
    aliases = {}
    if stack is not None:
        in_specs.append(ANY)
        args.append(stack)
        aliases = {4: 0}
    return pl.pallas_call(
        body,
        name=name,
        grid=(per,),
        in_specs=in_specs,
        out_specs=out,
        out_shape=SDS((DEPTH,) + shard, F32),
        input_output_aliases=aliases,
        compiler_params=_params("parallel"),
    )(*args)


def _sum_devices(gathered, name):
    _, rows, cols = gathered.shape

    def body(g_ref, out_ref):
        acc = g_ref[0]
        for dev in range(1, 8):
            acc = acc + g_ref[dev]
        out_ref[...] = acc

    return pl.pallas_call(body, name=name, out_shape=SDS((rows, cols), F32))(gathered)


PACK_TILE = 8 * 128


def _pack(arrays):
    parts = []
    for t in arrays:
        flat = t.reshape(-1)
        pad = (-flat.shape[0]) % PACK_TILE
        if pad:
            flat = jnp.concatenate([flat, jnp.zeros((pad,), flat.dtype)])
        parts.append(flat.reshape(-1, 128))
    return jnp.concatenate(parts, axis=0)


def _unpack(buf, shapes):
    out, row = [], 0
    for shp in shapes:
        size = math.prod(shp)
        rows = -(-size // PACK_TILE) * 8
        out.append(buf[row : row + rows].reshape(-1)[:size].reshape(shp))
        row += rows
    return out


BIG = ("w_in", "w_br_gm", "w_br_hg", "w_out", "w_up", "w_down")
BIG_AXIS = (1, 1, 1, 0, 1, 0)
LAYER_SMALL = ("mix_norm", "gm_ln_g", "gm_ln_b", "gm_ws", "gm_bs", "lb", "hg_norm_g", "ffn_norm", "conv_w", "conv_b")
WEIGHTS = ("mix_norm", "w_in", "gm_ln_g", "gm_ln_b", "gm_ws", "gm_bs", "hg_lb_logits", "hg_norm_g", "w_br_gm", "w_br_hg", "w_out",
           "ffn_norm", "w_up", "conv_w", "conv_b", "w_down", "final_norm")


def kernel(x, mix_norm, w_in, gm_ln_g, gm_ln_b, gm_ws, gm_bs, hg_lb_logits, hg_norm_g, w_br_gm, w_br_hg, w_out, ffn_norm, w_up, conv_w, conv_b, w_down, final_norm, loss_target, m_mix_norm, m_w_in, m_gm_ln_g, m_gm_ln_b, m_gm_ws, m_gm_bs, m_hg_lb_logits, m_hg_norm_g, m_w_br_gm, m_w_br_hg, m_w_out, m_ffn_norm, m_w_up, m_conv_w, m_conv_b, m_w_down, m_final_norm, v_mix_norm, v_w_in, v_gm_ln_g, v_gm_ln_b, v_gm_ws, v_gm_bs, v_hg_lb_logits, v_hg_norm_g, v_w_br_gm, v_w_br_hg, v_w_out, v_ffn_norm, v_w_up, v_conv_w, v_conv_b, v_w_down, v_final_norm):
    w = dict(mix_norm=mix_norm, w_in=w_in, gm_ln_g=gm_ln_g, gm_ln_b=gm_ln_b, gm_ws=gm_ws, gm_bs=gm_bs, hg_lb_logits=hg_lb_logits,
             hg_norm_g=hg_norm_g, w_br_gm=w_br_gm, w_br_hg=w_br_hg, w_out=w_out, ffn_norm=ffn_norm, w_up=w_up, conv_w=conv_w,
             conv_b=conv_b, w_down=w_down, final_norm=final_norm)
    m = dict(mix_norm=m_mix_norm, w_in=m_w_in, gm_ln_g=m_gm_ln_g, gm_ln_b=m_gm_ln_b, gm_ws=m_gm_ws, gm_bs=m_gm_bs,
             hg_lb_logits=m_hg_lb_logits, hg_norm_g=m_hg_norm_g, w_br_gm=m_w_br_gm, w_br_hg=m_w_br_hg, w_out=m_w_out,
             ffn_norm=m_ffn_norm, w_up=m_w_up, conv_w=m_conv_w, conv_b=m_conv_b, w_down=m_w_down, final_norm=m_final_norm)
    v = dict(mix_norm=v_mix_norm, w_in=v_w_in, gm_ln_g=v_gm_ln_g, gm_ln_b=v_gm_ln_b, gm_ws=v_gm_ws, gm_bs=v_gm_bs,
             hg_lb_logits=v_hg_lb_logits, hg_norm_g=v_hg_norm_g, w_br_gm=v_w_br_gm, w_br_hg=v_w_br_hg, w_out=v_w_out,
             ffn_norm=v_ffn_norm, w_up=v_w_up, conv_w=v_conv_w, conv_b=v_conv_b, w_down=v_w_down, final_norm=v_final_norm)

    axes = list(BIG_AXIS)
    d = x.shape[2]

    shards = [w[k].astype(MM_DTYPE) for k in BIG]
    w_in_0, conv_full = _run_rider(_gather_rider(shards[:1], axes[:1], 0, conv_w=conv_w), "gather_weights_0")
    full = [dict(w_in=w_in_0)]
    lb = _lower_bounds(hg_lb_logits)
    xs = x[0]
    saved = []
    mixer = 4
    for l in range(DEPTH):
        fw = full[l]
        more = l + 1 < DEPTH
        (z, h), got = _norm_matmul(
            xs, mix_norm[l : l + 1], fw["w_in"], f"in_proj_{l}", _gather_rider(shards[mixer:], axes[mixer:], l)
        )
        fw.update(zip(BIG[mixer:], got))
        bst = gm_bs[l].T
        a = _gmlp_fwd(z, gm_ln_g[l : l + 1], gm_ln_b[l : l + 1], gm_ws[l], bst, f"gmlp_fwd_{l}")
        rider = _gather_rider(shards[:1], axes[:1], l + 1) if more else None
        if l == 0:
            rider = _join(rider, _gather_rider(shards[1:mixer], axes[1:mixer], 0))
        (o, b, states), got = _hgrn_fwd(z, lb[l : l + 1], hg_norm_g[l : l + 1], f"hgrn_fwd_{l}", rider)
        if more:
            full.append(dict(w_in=got[0]))
        if l == 0:
            fw.update(zip(BIG[1:mixer], got[1:]))
        rider = _gather_rider(shards[1:mixer], axes[1:mixer], l + 1) if more else None
        (x1,), got = _mix_fwd(xs, a, b, z, fw["w_br_gm"], fw["w_br_hg"], fw["w_out"], f"mix_fwd_{l}", rider)
        if more:
            full[l + 1].update(zip(BIG[1:mixer], got))
        (x2, z2, h2, conv), _ = _ffn_fwd(
            x1, ffn_norm[l : l + 1], fw["w_up"], conv_full, conv_b[l : l + 1], fw["w_down"], l, f"ffn_fwd_{l}"
        )
        saved.append((xs, h, z, a, b, o, states, x1, h2, z2, conv, bst))
        xs = x2

    loss_tile, dx, d_final = _loss_head(xs, final_norm.reshape(1, d), loss_target[0])
    loss = lax.psum(loss_tile[0, 0], ("x", "y", "c"))

    big = {k: None for k in BIG}
    grads = [None] * len(BIG)
    per_layer = [None] * DEPTH
    mix_ids, ffn_ids = list(range(mixer)), list(range(mixer, len(BIG)))
    mix_axes, ffn_axes = axes[:mixer], axes[mixer:]
    mix_pairs = None
    mix_summed = None

    def pair_sums(ids, received, layer):
        return [_pair_sum(big[BIG[k]], layer, r, axes[k], f"pair_sum_{BIG[k]}_{layer}") for k, r in zip(ids, received)]

    def chip_sums(ids, pairs, others, layer):
        for k, p, ot in zip(ids, pairs, others):
            grads[k] = _chip_sum(p, ot, axes[k], grads[k], layer, f"chip_sum_{BIG[k]}_{layer}")

    for l in reversed(range(DEPTH)):
        x0, h, z, a, b, o, states, x1, h2, z2, conv, bst = saved[l]
        fw = full[l]
        dz2, act, dconv, dx1, d_ffn = _ffn_bwd(
            dx, z2, conv, conv_full, fw["w_down"], fw["w_up"], x1, ffn_norm[l : l + 1], l, f"ffn_bwd_{l}"
        )
        big["w_down"] = _matmul_tn(act, dx, big["w_down"], l, f"dw_down_{l}")
        big["w_up"] = _matmul_tn(h2, dz2, big["w_up"], l, f"dw_up_{l}")
        rider = _pair_rider([big[BIG[k]] for k in ffn_ids], ffn_axes, l)
        if mix_pairs is not None:
            rider = _join(rider, _chip_rider(mix_pairs, mix_axes))
        (y, dpa, dpb, da, db, dgates), got = _mix_bwd(
            dx1, a, b, z, fw["w_br_gm"], fw["w_br_hg"], fw["w_out"], f"mix_bwd_{l}", rider
        )
        if mix_pairs is not None:
            chip_sums(mix_ids, mix_pairs, got[len(ffn_ids) :], l + 1)
            mix_summed = l + 1
        ffn_pairs = pair_sums(ffn_ids, got[: len(ffn_ids)], l)
        big["w_out"] = _matmul_tn(y, dx1, big["w_out"], l, f"dw_out_{l}")
        big["w_br_gm"] = _matmul_tn(a, dpa, big["w_br_gm"], l, f"dw_br_gm_{l}")
        big["w_br_hg"] = _matmul_tn(b, dpb, big["w_br_hg"], l, f"dw_br_hg_{l}")
        (dz_hg, d_lb, d_ng), others = _hgrn_bwd(
            db, z, o, states, lb[l : l + 1], hg_norm_g[l : l + 1], f"hgrn_bwd_{l}", _chip_rider(ffn_pairs, ffn_axes)
        )
        chip_sums(ffn_ids, ffn_pairs, others, l)
        dz_gm, d_lng, d_lnb, d_ws, d_bst = _gmlp_bwd(da, z, gm_ln_g[l : l + 1], gm_ln_b[l : l + 1], gm_ws[l], bst, f"gmlp_bwd_{l}")
        n_in = fw["w_in"].shape[1]
        big["w_in"] = _matmul_tn(h, dz_gm, big["w_in"], l, f"dw_in_gm_{l}", n_total=n_in, col_off=0)
        big["w_in"] = _matmul_tn(h, dz_hg, big["w_in"], l, f"dw_in_hg_{l}", n_total=n_in, col_off=2 * BR_DIM)
        big["w_in"] = _matmul_tn(h, dgates, big["w_in"], l, f"dw_in_gate_{l}", n_total=n_in, col_off=6 * BR_DIM)
        done_ids = ffn_ids + (mix_ids if mix_summed is not None else [])
        done_layers = [l] * len(ffn_ids) + ([mix_summed] * len(mix_ids) if mix_summed is not None else [])
        rider = _join(
            _pair_rider([big[BIG[k]] for k in mix_ids], mix_axes, l),
            _complete_rider([grads[k] for k in done_ids], [axes[k] for k in done_ids], done_layers),
        )
        if l == 0:
            upper = [_pack([per_layer[j][k] for k in LAYER_SMALL]) for j in range(1, DEPTH)] + [_pack([d_final[0]])]
            rider = _join(rider, _small_rider(jnp.concatenate(upper, axis=0)))
        (dx, d_mix), got = _proj_norm_bwd([dz_gm, dz_hg, dgates], fw["w_in"], x0, mix_norm[l : l + 1], dx1, f"in_proj_bwd_{l}", rider)
        for k, g in zip(done_ids, got[len(mix_ids) : len(mix_ids) + len(done_ids)]):
            grads[k] = g
        gathered_upper = got[len(mix_ids) + len(done_ids) :]
        mix_pairs = pair_sums(mix_ids, got[: len(mix_ids)], l)
        per_layer[l] = dict(
            mix_norm=d_mix[0], gm_ln_g=d_lng[0], gm_ln_b=d_lnb[0], gm_ws=d_ws, gm_bs=d_bst.T, lb=d_lb[0], hg_norm_g=d_ng[0],
            ffn_norm=d_ffn[0], conv_w=dconv[0:3], conv_b=dconv[3],
        )

    got = _run_rider(
        _join(_chip_rider(mix_pairs, mix_axes), _small_rider(_pack([per_layer[0][k] for k in LAYER_SMALL]))), "grad_exchange_0"
    )
    chip_sums(mix_ids, mix_pairs, got[: len(mix_ids)], 0)
    done = _run_rider(_complete_rider([grads[k] for k in mix_ids], mix_axes, [0] * len(mix_ids)), "grad_complete_0")
    for k, g in zip(mix_ids, done):
        grads[k] = g
    grad = dict(zip(BIG, grads))

    layer_shapes = [per_layer[0][k].shape for k in LAYER_SMALL]
    summed_upper = _unpack(_sum_devices(gathered_upper[0], "sum_small_upper"), layer_shapes * (DEPTH - 1) + [d_final[0].shape])
    summed_0 = _unpack(_sum_devices(got[len(mix_ids)], "sum_small_0"), layer_shapes)
    by_layer = [summed_0] + [summed_upper[j * len(LAYER_SMALL) : (j + 1) * len(LAYER_SMALL)] for j in range(DEPTH - 1)]
    small = {k: jnp.stack([by_layer[j][i] for j in range(DEPTH)]) for i, k in enumerate(LAYER_SMALL)}
    for k in LAYER_SMALL:
        if k != "lb":
            grad[k] = small[k]
    grad["hg_lb_logits"] = _lower_bounds_bwd(hg_lb_logits, small["lb"])
    grad["final_norm"] = summed_upper[-1]
    grad["conv_w"] = lax.dynamic_slice_in_dim(grad["conv_w"], _my_chip() * conv_w.shape[2], conv_w.shape[2], axis=2)

    delta, new_m, new_v = {}, {}, {}
    for k in WEIGHTS:
        delta[k], new_m[k], new_v[k] = _adamw(w[k], grad[k], m[k], v[k], f"adamw_{k}")

    return (loss, dx[None], *[grad[k] for k in WEIGHTS], *[delta[k] for k in WEIGHTS], *[new_m[k] for k in WEIGHTS],
            *[new_v[k] for k in WEIGHTS])
```

```python
import math

import jax
import jax.numpy as jnp
from jax import lax
from jax.experimental import pallas as pl
from jax.experimental.pallas import tpu as pltpu

F32 = jnp.float32
BF16 = jnp.bfloat16
MM_DTYPE = BF16

N_HEADS = 4
HEAD_DIM = 128
BR_DIM = N_HEADS * HEAD_DIM
CHUNK = 64
GM_BLOCK = 128
DEPTH = 4
N_CHIPS = 4
EPS = 1e-6
TINY = 1e-30
LB_MAX = 0.999
ADAM_LR = 0.001
ADAM_B1 = 0.9
ADAM_B2 = 0.999
ADAM_EPS = 1e-08
ADAM_WD = 0.01
ADAM_STEP = 10
INV_SQRT2 = 1.0 / math.sqrt(2.0)
INV_SQRT_2PI = 1.0 / math.sqrt(2.0 * math.pi)

VMEM_LIMIT_BYTES = 56 * 1024 * 1024
TILE_BYTES = 2 * 1024 * 1024
ACC_BYTES = 6 * 1024 * 1024
MESH = pl.DeviceIdType.MESH
SDS = jax.ShapeDtypeStruct
ANY = pl.BlockSpec(memory_space=pl.ANY)


def _params(*sem):
    return pltpu.CompilerParams(dimension_semantics=sem or None, vmem_limit_bytes=VMEM_LIMIT_BYTES)


def _divisor(n, limit, mult):
    best = None
    for d in range(mult, min(n, limit) + 1, mult):
        if n % d == 0:
            best = d
    return best if best is not None else n


def _dot(a, b):
    return jnp.dot(a.astype(MM_DTYPE), b.astype(MM_DTYPE), preferred_element_type=F32)


def _dot_nt(a, b):
    return lax.dot_general(a.astype(MM_DTYPE), b.astype(MM_DTYPE), (((1,), (1,)), ((), ())), preferred_element_type=F32)


def _dot_tn(a, b):
    return lax.dot_general(a.astype(MM_DTYPE), b.astype(MM_DTYPE), (((0,), (0,)), ((), ())), preferred_element_type=F32)


def _sigmoid(x):
    return 1.0 / (1.0 + jnp.exp(-x))


def _gelu(x):
    return 0.5 * x * (1.0 + lax.erf(x * INV_SQRT2))


def _gelu_grad(x):
    return 0.5 * (1.0 + lax.erf(x * INV_SQRT2)) + x * jnp.exp(-0.5 * x * x) * INV_SQRT_2PI


def _silu_grad(x, s):
    return s * (1.0 + x * (1.0 - s))


def _resident(shape, index_map):
    return pl.BlockSpec(shape, index_map, pipeline_mode=pl.Buffered(1))


class _Rider:
    def __init__(self, inputs, out_shapes, scratch, start, finish, aliases=None):
        self.inputs = list(inputs)
        self.out_shapes = list(out_shapes)
        self.scratch = list(scratch)
        self.start = start
        self.finish = finish
        self.aliases = dict(aliases or {})


def _join(a, b):
    if a is None or b is None:
        return a if b is None else b
    na, oa, sa = len(a.inputs), len(a.out_shapes), len(a.scratch)

    def start(i, o, s):
        a.start(i[:na], o[:oa], s[:sa])
        b.start(i[na:], o[oa:], s[sa:])

    def finish(i, o, s):
        a.finish(i[:na], o[:oa], s[:sa])
        b.finish(i[na:], o[oa:], s[sa:])

    aliases = dict(a.aliases)
    aliases.update({na + k: oa + v for k, v in b.aliases.items()})
    return _Rider(a.inputs + b.inputs, a.out_shapes + b.out_shapes, a.scratch + b.scratch, start, finish, aliases)


def _hosted_call(body, rider, *, name, grid, in_specs, out_specs, out_shape, scratch_shapes, args, semantics):
    n_in, n_out, n_scr = len(in_specs), len(out_specs), len(scratch_shapes)
    if rider is None:
        outs = pl.pallas_call(
            body, name=name, grid=grid, in_specs=in_specs, out_specs=out_specs, out_shape=out_shape,
            scratch_shapes=scratch_shapes, compiler_params=_params(*semantics),
        )(*args)
        return list(outs), []
    ri, ro = len(rider.inputs), len(rider.out_shapes)

    def wrapped(*refs):
        p = 0
        hin = refs[p : p + n_in]
        p += n_in
        rin = refs[p : p + ri]
        p += ri
        hout = refs[p : p + n_out]
        p += n_out
        rout = refs[p : p + ro]
        p += ro
        hscr = refs[p : p + n_scr]
        rscr = refs[p + n_scr :]

        @pl.when(pl.program_id(0) == 0)
        def _():
            rider.start(rin, rout, rscr)

        body(*hin, *hout, *hscr)

        @pl.when(pl.program_id(0) == grid[0] - 1)
        def _():
            rider.finish(rin, rout, rscr)

    outs = pl.pallas_call(
        wrapped,
        name=name,
        grid=grid,
        in_specs=list(in_specs) + [ANY] * ri,
        out_specs=list(out_specs) + [ANY] * ro,
        out_shape=list(out_shape) + rider.out_shapes,
        scratch_shapes=list(scratch_shapes) + rider.scratch,
        input_output_aliases={n_in + k: n_out + v for k, v in rider.aliases.items()},
        compiler_params=_params(*semantics),
    )(*args, *rider.inputs)
    return list(outs[:n_out]), list(outs[n_out:])


def _run_rider(rider, name):
    ri, ro = len(rider.inputs), len(rider.out_shapes)

    def body(*refs):
        rin, rout, rscr = refs[:ri], refs[ri : ri + ro], refs[ri + ro :]
        rider.start(rin, rout, rscr)
        rider.finish(rin, rout, rscr)

    return list(
        pl.pallas_call(
            body,
            name=name,
            in_specs=[ANY] * ri,
            out_specs=[ANY] * ro,
            out_shape=rider.out_shapes,
            scratch_shapes=rider.scratch,
            input_output_aliases=rider.aliases,
        )(*rider.inputs)
    )


def _lower_bounds(logits):
    def body(lg_ref, lb_ref):
        lg = lg_ref[...]
        mx = jnp.max(lg, axis=0, keepdims=True)
        e = jnp.exp(lg - mx)
        p = e / jnp.sum(e, axis=0, keepdims=True)
        run = p[0:1]
        rows = [run - p[0:1]]
        for l in range(1, DEPTH):
            run = run + p[l : l + 1]
            rows.append(run - p[0:1])
        raw = jnp.concatenate(rows, axis=0)
        lb_ref[...] = jnp.minimum(jnp.maximum(raw, 0.0), LB_MAX)

    return pl.pallas_call(body, name="lower_bounds", out_shape=SDS(logits.shape, F32))(logits)


def _lower_bounds_bwd(logits, dlb):
    def body(lg_ref, dlb_ref, dlg_ref):
        lg = lg_ref[...]
        mx = jnp.max(lg, axis=0, keepdims=True)
        e = jnp.exp(lg - mx)
        p = e / jnp.sum(e, axis=0, keepdims=True)
        run = p[0:1]
        rows = [run - p[0:1]]
        for l in range(1, DEPTH):
            run = run + p[l : l + 1]
            rows.append(run - p[0:1])
        raw = jnp.concatenate(rows, axis=0)
        lo = jnp.where(raw > 0.0, 1.0, jnp.where(raw == 0.0, 0.5, 0.0))
        clipped = jnp.maximum(raw, 0.0)
        hi = jnp.where(clipped < LB_MAX, 1.0, jnp.where(clipped == LB_MAX, 0.5, 0.0))
        draw = dlb_ref[...] * lo * hi
        total = jnp.sum(draw, axis=0, keepdims=True)
        dps = []
        tail = total
        for j in range(DEPTH):
            dps.append(tail - total if j == 0 else tail)
            tail = tail - draw[j : j + 1]
        dp = jnp.concatenate(dps, axis=0)
        dlg_ref[...] = p * (dp - jnp.sum(p * dp, axis=0, keepdims=True))

    return pl.pallas_call(body, name="lower_bounds_bwd", out_shape=SDS(logits.shape, F32))(logits, dlb)


def _in_proj(x, gain, w, lng, lnb, ws, bst, name, rider=None):
    s, d = x.shape
    n = w.shape[1]
    tm = _divisor(s, 512, GM_BLOCK)

    def body(x_ref, g_ref, w_ref, lng_ref, lnb_ref, ws_ref, bst_ref, z_ref, h_ref, a_ref):
        xv = x_ref[...]
        r = lax.rsqrt(jnp.mean(xv * xv, axis=-1, keepdims=True) + EPS)
        h = ((xv * r) * g_ref[...]).astype(MM_DTYPE)
        h_ref[...] = h
        uv = jnp.dot(h, w_ref[:, : 2 * BR_DIM], preferred_element_type=F32)
        z_ref[:, : 2 * BR_DIM] = uv
        z_ref[:, 2 * BR_DIM :] = jnp.dot(h, w_ref[:, 2 * BR_DIM :], preferred_element_type=F32)
        _gmlp_fwd_tile(uv[:, :BR_DIM], uv[:, BR_DIM:], lng_ref, lnb_ref, ws_ref, bst_ref, a_ref)

    row = lambda wd: pl.BlockSpec((tm, wd), lambda i: (i, 0))
    return _hosted_call(
        body,
        rider,
        name=name,
        grid=(s // tm,),
        in_specs=[row(d), _resident((1, d), lambda i: (0, 0)), _resident((d, n), lambda i: (0, 0))] + _gmlp_param_specs(),
        out_specs=[row(n), row(d), row(BR_DIM)],
        out_shape=[SDS((s, n), F32), SDS((s, d), MM_DTYPE), SDS((s, BR_DIM), MM_DTYPE)],
        scratch_shapes=[],
        args=[x, gain, w, lng, lnb, ws, bst],
        semantics=("arbitrary",),
    )


def _in_proj_bwd(dz_parts, w, x, gain, dres, name, rider=None):
    s, d = x.shape
    n = w.shape[1]
    tm = _divisor(s, 512, 16)
    widths = [p.shape[1] for p in dz_parts]
    offsets = [sum(widths[:k]) for k in range(len(widths))]
    n_parts = len(dz_parts)

    def body(*refs):
        dz_refs = refs[:n_parts]
        w_ref, x_ref, g_ref, dres_ref, dx_ref, dg_ref = refs[n_parts:]

        @pl.when(pl.program_id(0) == 0)
        def _():
            dg_ref[...] = jnp.zeros_like(dg_ref)

        dh = None
        for dz_ref, off, wd in zip(dz_refs, offsets, widths):
            part = _dot_nt(dz_ref[...], w_ref[:, off : off + wd])
            dh = part if dh is None else dh + part
        xv = x_ref[...]
        r = lax.rsqrt(jnp.mean(xv * xv, axis=-1, keepdims=True) + EPS)
        xh = xv * r
        dg_ref[...] += jnp.sum(dh * xh, axis=0, keepdims=True)
        dxh = dh * g_ref[...]
        dx_ref[...] = dres_ref[...] + r * (dxh - xh * jnp.mean(dxh * xh, axis=-1, keepdims=True))

    in_specs = [pl.BlockSpec((tm, wd), lambda i: (i, 0)) for wd in widths] + [
        _resident((d, n), lambda i: (0, 0)),
        pl.BlockSpec((tm, d), lambda i: (i, 0)),
        _resident((1, d), lambda i: (0, 0)),
        pl.BlockSpec((tm, d), lambda i: (i, 0)),
    ]
    return _hosted_call(
        body,
        rider,
        name=name,
        grid=(s // tm,),
        in_specs=in_specs,
        out_specs=[pl.BlockSpec((tm, d), lambda i: (i, 0)), pl.BlockSpec((1, d), lambda i: (0, 0))],
        out_shape=[SDS((s, d), F32), SDS((1, d), F32)],
        scratch_shapes=[],
        args=[*dz_parts, w, x, gain, dres],
        semantics=("arbitrary",),
    )


def _matmul_tn(a, b, stack, layer, name, n_total=None, col_off=0):
    s, k = a.shape
    n = b.shape[1]
    n_total = n if n_total is None else n_total
    tn = _divisor(math.gcd(n, col_off) if col_off else n, max(128, ACC_BYTES // (4 * k)), 128)
    ts = _divisor(s, min(2048, max(512, ACC_BYTES // (2 * k))), 16)
    off = col_off // tn

    def body(a_ref, b_ref, *rest):
        out_ref = rest[-1]

        @pl.when(pl.program_id(1) == 0)
        def _():
            out_ref[...] = jnp.zeros_like(out_ref)

        out_ref[...] += _dot_tn(a_ref[...], b_ref[...])

    in_specs = [pl.BlockSpec((ts, k), lambda j, i: (i, 0)), pl.BlockSpec((ts, tn), lambda j, i: (i, j))]
    args = [a, b]
    aliases = {}
    if stack is not None:
        in_specs.append(ANY)
        args.append(stack)
        aliases = {2: 0}
    return pl.pallas_call(
        body,
        name=name,
        grid=(n // tn, s // ts),
        in_specs=in_specs,
        out_specs=pl.BlockSpec((None, k, tn), lambda j, i: (layer, 0, off + j)),
        out_shape=SDS((DEPTH, k, n_total), F32),
        input_output_aliases=aliases,
        compiler_params=_params("parallel", "arbitrary"),
    )(*args)


def _gm_mask():
    r = lax.broadcasted_iota(jnp.int32, (GM_BLOCK, GM_BLOCK), 0) // CHUNK
    c = lax.broadcasted_iota(jnp.int32, (GM_BLOCK, GM_BLOCK), 1) // CHUNK
    return r >= c


def _gm_normed(v, lng, lnb):
    vg = _gelu(v)
    mu = jnp.mean(vg, axis=-1, keepdims=True)
    xc = vg - mu
    rstd = lax.rsqrt(jnp.mean(xc * xc, axis=-1, keepdims=True) + EPS)
    xh = xc * rstd
    return xh, rstd, xh * lng + lnb


def _gmlp_param_specs():
    return [
        _resident((1, BR_DIM), lambda i: (0, 0)),
        _resident((1, BR_DIM), lambda i: (0, 0)),
        _resident((N_HEADS, GM_BLOCK, GM_BLOCK), lambda i: (0, 0, 0)),
        _resident((GM_BLOCK, N_HEADS), lambda i: (0, 0)),
    ]


def _gmlp_fwd_tile(u, v, lng_ref, lnb_ref, ws_ref, bst_ref, a_ref):
    mask = _gm_mask()
    ug = _gelu(u)
    _, _, vn = _gm_normed(v, lng_ref[...], lnb_ref[...])
    vn = vn.astype(MM_DTYPE)
    for h in range(N_HEADS):
        cs = slice(h * HEAD_DIM, (h + 1) * HEAD_DIM)
        wm = jnp.where(mask, ws_ref[h], 0.0).astype(MM_DTYPE)
        for blk in range(u.shape[0] // GM_BLOCK):
            rs = slice(blk * GM_BLOCK, (blk + 1) * GM_BLOCK)
            mixed = _dot(wm, vn[rs, cs]) + bst_ref[:, h : h + 1]
            a_ref[rs, cs] = (ug[rs, cs] * mixed).astype(MM_DTYPE)


def _gmlp_bwd_tile(da_v, u, v, lng_ref, lnb_ref, ws_ref, bst_ref, dz_ref, dlng_ref, dlnb_ref, dws_ref, dbst_ref):
    mask = _gm_mask()
    ug = _gelu(u)
    lng_v = lng_ref[...]
    xh, rstd, vn = _gm_normed(v, lng_v, lnb_ref[...])
    vnb = vn.astype(MM_DTYPE)
    dmix = da_v * ug
    dmixb = dmix.astype(MM_DTYPE)
    dvn_cols = []
    for h in range(N_HEADS):
        cs = slice(h * HEAD_DIM, (h + 1) * HEAD_DIM)
        wm = jnp.where(mask, ws_ref[h], 0.0).astype(MM_DTYPE)
        dw = jnp.zeros((GM_BLOCK, GM_BLOCK), F32)
        dbias = jnp.zeros((GM_BLOCK, 1), F32)
        dvn_rows = []
        for blk in range(u.shape[0] // GM_BLOCK):
            rs = slice(blk * GM_BLOCK, (blk + 1) * GM_BLOCK)
            mixed = _dot(wm, vnb[rs, cs]) + bst_ref[:, h : h + 1]
            dz_ref[rs, cs] = (da_v[rs, cs] * mixed * _gelu_grad(u[rs, cs])).astype(MM_DTYPE)
            dw = dw + _dot_nt(dmixb[rs, cs], vnb[rs, cs])
            dbias = dbias + jnp.sum(dmix[rs, cs], axis=1, keepdims=True)
            dvn_rows.append(_dot_tn(wm, dmixb[rs, cs]))
        dws_ref[h] += jnp.where(mask, dw, 0.0)
        dbst_ref[:, h : h + 1] += dbias
        dvn_cols.append(jnp.concatenate(dvn_rows, axis=0) if len(dvn_rows) > 1 else dvn_rows[0])
    dvn = jnp.concatenate(dvn_cols, axis=1)
    dlng_ref[...] += jnp.sum(dvn * xh, axis=0, keepdims=True)
    dlnb_ref[...] += jnp.sum(dvn, axis=0, keepdims=True)
    dxh = dvn * lng_v
    dvg = rstd * (dxh - jnp.mean(dxh, axis=-1, keepdims=True) - xh * jnp.mean(dxh * xh, axis=-1, keepdims=True))
    dz_ref[:, BR_DIM:] = (dvg * _gelu_grad(v)).astype(MM_DTYPE)


def _gmlp_bwd(da, z, lng, lnb, ws, bst, name):
    s = z.shape[0]
    tg = _divisor(s, 256, GM_BLOCK)

    def body(da_ref, z_ref, lng_ref, lnb_ref, ws_ref, bst_ref, dz_ref, dlng_ref, dlnb_ref, dws_ref, dbst_ref):
        @pl.when(pl.program_id(0) == 0)
        def _():
            for ref in (dlng_ref, dlnb_ref, dws_ref, dbst_ref):
                ref[...] = jnp.zeros_like(ref)

        _gmlp_bwd_tile(da_ref[...], z_ref[:, :BR_DIM], z_ref[:, BR_DIM:], lng_ref, lnb_ref, ws_ref, bst_ref,
                       dz_ref, dlng_ref, dlnb_ref, dws_ref, dbst_ref)

    fixed = lambda shape: pl.BlockSpec(shape, lambda i: tuple(0 for _ in shape))
    return pl.pallas_call(
        body,
        name=name,
        grid=(s // tg,),
        in_specs=[pl.BlockSpec((tg, BR_DIM), lambda i: (i, 0)), pl.BlockSpec((tg, 2 * BR_DIM), lambda i: (i, 0))] + _gmlp_param_specs(),
        out_specs=[
            pl.BlockSpec((tg, 2 * BR_DIM), lambda i: (i, 0)),
            fixed((1, BR_DIM)),
            fixed((1, BR_DIM)),
            fixed((N_HEADS, GM_BLOCK, GM_BLOCK)),
            fixed((GM_BLOCK, N_HEADS)),
        ],
        out_shape=[
            SDS((s, 2 * BR_DIM), MM_DTYPE),
            SDS((1, BR_DIM), F32),
            SDS((1, BR_DIM), F32),
            SDS((N_HEADS, GM_BLOCK, GM_BLOCK), F32),
            SDS((GM_BLOCK, N_HEADS), F32),
        ],
        compiler_params=_params("arbitrary"),
    )(da, z, lng, lnb, ws, bst)


HG_ROWS = 256
HG_UNROLL = 4
MID = CHUNK // 2 - 1


def _tri(lower):
    r = lax.broadcasted_iota(jnp.int32, (CHUNK, CHUNK), 0)
    c = lax.broadcasted_iota(jnp.int32, (CHUNK, CHUNK), 1)
    return r >= c if lower else c >= r


def _scan_rows(x, reverse=False):
    n = x.shape[0]
    rid = lax.broadcasted_iota(jnp.int32, x.shape, 0)
    k = 1
    while k < n:
        if reverse:
            x = x + jnp.where(rid < n - k, pltpu.roll(x, n - k, 0), 0.0)
        else:
            x = x + jnp.where(rid >= k, pltpu.roll(x, k, 0), 0.0)
        k *= 2
    return x


def _hgrn_fwd(z, lb, ng, name, rider=None):
    s = z.shape[0]
    rows_per = _divisor(s, HG_ROWS, CHUNK)
    n_sub = rows_per // CHUNK

    def body(zqf_ref, zio_ref, lb_ref, ng_ref, o_ref, b_ref, st_ref, state):
        @pl.when(pl.program_id(0) == 0)
        def _():
            state[...] = jnp.zeros_like(state)

        low = _tri(True)
        ng_v = ng_ref[...]

        def chunk(ci, carry):
            rows = pl.ds(pl.multiple_of(ci * CHUNK, CHUNK), CHUNK)
            for h in range(N_HEADS):
                cs = slice(h * HEAD_DIM, (h + 1) * HEAD_DIM)
                cs2 = slice(BR_DIM + h * HEAD_DIM, BR_DIM + (h + 1) * HEAD_DIM)
                zq = zqf_ref[rows, cs]
                zf = zqf_ref[rows, cs2]
                vi = zio_ref[rows, cs]
                zog = zio_ref[rows, cs2]
                lbh = lb_ref[:, cs]
                qf = zq * _sigmoid(zq)
                forget = lbh + (1.0 - lbh) * _sigmoid(zf)
                g = jnp.log(jnp.maximum(forget, TINY))
                k = (1.0 - lbh) * _sigmoid(-zf)
                gc = _scan_rows(g)
                gm = gc[MID : MID + 1]
                gl = gc[CHUNK - 1 : CHUNK]
                a = jnp.where(low, _dot_nt(qf * jnp.exp(gc - gm), k * jnp.exp(gm - gc)), 0.0)
                st = state[h]
                st_ref[ci, h] = st
                o = _dot(a, vi) + _dot_nt(qf * jnp.exp(gc), st)
                state[h] = st * jnp.exp(gl) + _dot_tn(vi, k * jnp.exp(gl - gc))
                r = lax.rsqrt(jnp.mean(o * o, axis=-1, keepdims=True) + EPS)
                o_ref[rows, cs] = o
                b_ref[rows, cs] = (((o * r) * ng_v) * (zog * _sigmoid(zog))).astype(MM_DTYPE)
            return carry

        lax.fori_loop(0, n_sub, chunk, 0, unroll=math.gcd(n_sub, HG_UNROLL))

    return _hosted_call(
        body,
        rider,
        name=name,
        grid=(s // rows_per,),
        in_specs=[
            pl.BlockSpec((rows_per, 2 * BR_DIM), lambda i: (i, 1)),
            pl.BlockSpec((rows_per, 2 * BR_DIM), lambda i: (i, 2)),
            _resident((1, BR_DIM), lambda i: (0, 0)),
            _resident((1, HEAD_DIM), lambda i: (0, 0)),
        ],
        out_specs=[
            pl.BlockSpec((rows_per, BR_DIM), lambda i: (i, 0)),
            pl.BlockSpec((rows_per, BR_DIM), lambda i: (i, 0)),
            pl.BlockSpec((n_sub, N_HEADS, HEAD_DIM, HEAD_DIM), lambda i: (i, 0, 0, 0)),
        ],
        out_shape=[
            SDS((s, BR_DIM), F32),
            SDS((s, BR_DIM), MM_DTYPE),
            SDS((s // CHUNK, N_HEADS, HEAD_DIM, HEAD_DIM), F32),
        ],
        scratch_shapes=[pltpu.VMEM((N_HEADS, HEAD_DIM, HEAD_DIM), F32)],
        args=[z, z, lb, ng],
        semantics=("arbitrary",),
    )


def _hgrn_bwd(db, z, o, states, lb, ng, name, rider=None):
    s = z.shape[0]
    rows_per = _divisor(s, HG_ROWS, CHUNK)
    n_sub = rows_per // CHUNK
    n_steps = s // rows_per

    def body(db_ref, zqf_ref, zio_ref, o_ref, st_ref, lb_ref, ng_ref, dz_ref, dlb_ref, dng_ref, dstate):
        @pl.when(pl.program_id(0) == 0)
        def _():
            dstate[...] = jnp.zeros_like(dstate)
            dlb_ref[...] = jnp.zeros_like(dlb_ref)
            dng_ref[...] = jnp.zeros_like(dng_ref)

        low = _tri(True)
        ng_v = ng_ref[...]

        def chunk(cc, carry):
            ci = n_sub - 1 - cc
            rows = pl.ds(pl.multiple_of(ci * CHUNK, CHUNK), CHUNK)
            for h in range(N_HEADS):
                cs = slice(h * HEAD_DIM, (h + 1) * HEAD_DIM)
                cs2 = slice(BR_DIM + h * HEAD_DIM, BR_DIM + (h + 1) * HEAD_DIM)
                zq = zqf_ref[rows, cs]
                zf = zqf_ref[rows, cs2]
                vi = zio_ref[rows, cs]
                zog = zio_ref[rows, cs2]
                lbh = lb_ref[:, cs]
                ov = o_ref[rows, cs]
                dbv = db_ref[rows, cs]
                r = lax.rsqrt(jnp.mean(ov * ov, axis=-1, keepdims=True) + EPS)
                on = ov * r
                sgo = _sigmoid(zog)
                gate = zog * sgo
                dng_ref[...] += jnp.sum(dbv * gate * on, axis=0, keepdims=True)
                don = dbv * gate * ng_v
                dzog = dbv * (on * ng_v) * _silu_grad(zog, sgo)
                do = r * (don - on * jnp.mean(don * on, axis=-1, keepdims=True))
                sq = _sigmoid(zq)
                qf = zq * sq
                sg = _sigmoid(zf)
                sgn = _sigmoid(-zf)
                oml = 1.0 - lbh
                forget = lbh + oml * sg
                fm = jnp.maximum(forget, TINY)
                k = oml * sgn
                gc = _scan_rows(jnp.log(fm))
                gm = gc[MID : MID + 1]
                gl = gc[CHUNK - 1 : CHUNK]
                e_g = jnp.exp(gc)
                e_q = jnp.exp(gc - gm)
                e_k = jnp.exp(gm - gc)
                e_kd = jnp.exp(gl - gc)
                e_gl = jnp.exp(gl)
                qt = qf * e_q
                kt = k * e_k
                a = jnp.where(low, _dot_nt(qt, kt), 0.0)
                st = st_ref[ci, h]
                dst = dstate[h]
                dp = jnp.where(low, _dot_nt(do, vi), 0.0)
                dv = _dot_tn(a, do) + _dot_nt(k * e_kd, dst)
                dq = _dot(dp, kt) * e_q + e_g * _dot(do, st)
                dks = e_kd * _dot(vi, dst)
                dk = _dot_tn(dp, qt) * e_k + dks
                dstate[h] = _dot_tn(do, qf * e_g) + dst * e_gl
                dgc = qf * dq - k * dk
                extra = e_gl * jnp.sum(st * dst, axis=0, keepdims=True) + jnp.sum(k * dks, axis=0, keepdims=True)
                dg = _scan_rows(dgc, reverse=True) + extra
                dforget = jnp.where(forget > TINY, dg / fm, 0.0)
                both = dforget - dk
                dlb_ref[:, cs] += jnp.sum(sgn * both, axis=0, keepdims=True)
                dz_ref[rows, cs] = (dq * _silu_grad(zq, sq)).astype(MM_DTYPE)
                dz_ref[rows, cs2] = (oml * sg * sgn * both).astype(MM_DTYPE)
                dz_ref[rows, 2 * BR_DIM + h * HEAD_DIM : 2 * BR_DIM + (h + 1) * HEAD_DIM] = dv.astype(MM_DTYPE)
                dz_ref[rows, 3 * BR_DIM + h * HEAD_DIM : 3 * BR_DIM + (h + 1) * HEAD_DIM] = dzog.astype(MM_DTYPE)
            return carry

        lax.fori_loop(0, n_sub, chunk, 0, unroll=math.gcd(n_sub, HG_UNROLL))

    rev = lambda i: n_steps - 1 - i
    return _hosted_call(
        body,
        rider,
        name=name,
        grid=(n_steps,),
        in_specs=[
            pl.BlockSpec((rows_per, BR_DIM), lambda i: (rev(i), 0)),
            pl.BlockSpec((rows_per, 2 * BR_DIM), lambda i: (rev(i), 1)),
            pl.BlockSpec((rows_per, 2 * BR_DIM), lambda i: (rev(i), 2)),
            pl.BlockSpec((rows_per, BR_DIM), lambda i: (rev(i), 0)),
            pl.BlockSpec((n_sub, N_HEADS, HEAD_DIM, HEAD_DIM), lambda i: (rev(i), 0, 0, 0)),
            _resident((1, BR_DIM), lambda i: (0, 0)),
            _resident((1, HEAD_DIM), lambda i: (0, 0)),
        ],
        out_specs=[
            pl.BlockSpec((rows_per, 4 * BR_DIM), lambda i: (rev(i), 0)),
            pl.BlockSpec((1, BR_DIM), lambda i: (0, 0)),
            pl.BlockSpec((1, HEAD_DIM), lambda i: (0, 0)),
        ],
        out_shape=[SDS((s, 4 * BR_DIM), MM_DTYPE), SDS((1, BR_DIM), F32), SDS((1, HEAD_DIM), F32)],
        scratch_shapes=[pltpu.VMEM((N_HEADS, HEAD_DIM, HEAD_DIM), F32)],
        args=[db, z, z, o, states, lb, ng],
        semantics=("arbitrary",),
    )


def _mix_fwd(x, a, b, z, w_gm, w_hg, w_out, name, rider=None):
    s, d = x.shape
    tm = _divisor(s, 256, 16)
    g0 = 6 * BR_DIM // d

    def body(x_ref, a_ref, b_ref, ga_ref, gb_ref, wgm_ref, whg_ref, wout_ref, out_ref):
        y = _sigmoid(ga_ref[...]) * _dot(a_ref[...], wgm_ref[...]) + _sigmoid(gb_ref[...]) * _dot(b_ref[...], whg_ref[...])
        out_ref[...] = x_ref[...] + _dot(y, wout_ref[...])

    return _hosted_call(
        body,
        rider,
        name=name,
        grid=(s // tm,),
        in_specs=[
            pl.BlockSpec((tm, d), lambda i: (i, 0)),
            pl.BlockSpec((tm, BR_DIM), lambda i: (i, 0)),
            pl.BlockSpec((tm, BR_DIM), lambda i: (i, 0)),
            pl.BlockSpec((tm, d), lambda i: (i, g0)),
            pl.BlockSpec((tm, d), lambda i: (i, g0 + 1)),
            _resident((BR_DIM, d), lambda i: (0, 0)),
            _resident((BR_DIM, d), lambda i: (0, 0)),
            _resident((d, d), lambda i: (0, 0)),
        ],
        out_specs=[pl.BlockSpec((tm, d), lambda i: (i, 0))],
        out_shape=[SDS((s, d), F32)],
        scratch_shapes=[],
        args=[x, a, b, z, z, w_gm, w_hg, w_out],
        semantics=("arbitrary",),
    )


def _mix_bwd(dx, a, b, z, w_gm, w_hg, w_out, name, rider=None):
    s, d = dx.shape
    tm = _divisor(s, 256, 16)
    g0 = 6 * BR_DIM // d

    def body(dx_ref, a_ref, b_ref, ga_ref, gb_ref, wgm_ref, whg_ref, wout_ref, y_ref, dpa_ref, dpb_ref, da_ref, db_ref, dg_ref):
        dy = _dot_nt(dx_ref[...], wout_ref[...])
        pa = _dot(a_ref[...], wgm_ref[...])
        pb = _dot(b_ref[...], whg_ref[...])
        sa = _sigmoid(ga_ref[...])
        sb = _sigmoid(gb_ref[...])
        y_ref[...] = (sa * pa + sb * pb).astype(MM_DTYPE)
        dpa = (dy * sa).astype(MM_DTYPE)
        dpb = (dy * sb).astype(MM_DTYPE)
        dpa_ref[...] = dpa
        dpb_ref[...] = dpb
        da_ref[...] = _dot_nt(dpa, wgm_ref[...])
        db_ref[...] = _dot_nt(dpb, whg_ref[...])
        dg_ref[:, :d] = (dy * pa * sa * (1.0 - sa)).astype(MM_DTYPE)
        dg_ref[:, d:] = (dy * pb * sb * (1.0 - sb)).astype(MM_DTYPE)

    row = lambda w: pl.BlockSpec((tm, w), lambda i: (i, 0))
    return _hosted_call(
        body,
        rider,
        name=name,
        grid=(s // tm,),
        in_specs=[
            row(d),
            row(BR_DIM),
            row(BR_DIM),
            pl.BlockSpec((tm, d), lambda i: (i, g0)),
            pl.BlockSpec((tm, d), lambda i: (i, g0 + 1)),
            _resident((BR_DIM, d), lambda i: (0, 0)),
            _resident((BR_DIM, d), lambda i: (0, 0)),
            _resident((d, d), lambda i: (0, 0)),
        ],
        out_specs=[row(d), row(d), row(d), row(BR_DIM), row(BR_DIM), row(2 * d)],
        out_shape=[
            SDS((s, d), MM_DTYPE),
            SDS((s, d), MM_DTYPE),
            SDS((s, d), MM_DTYPE),
            SDS((s, BR_DIM), F32),
            SDS((s, BR_DIM), F32),
            SDS((s, 2 * d), MM_DTYPE),
        ],
        scratch_shapes=[],
        args=[dx, a, b, z, z, w_gm, w_hg, w_out],
        semantics=("arbitrary",),
    )


HALO = 8
FFN_ROWS = 256


def _shift_down(v, prev, n):
    rolled = pltpu.roll(v, n, 0)
    rid = lax.broadcasted_iota(jnp.int32, v.shape, 0)
    out = rolled
    for r in range(n):
        out = jnp.where(rid == r, prev[HALO - n + r : HALO - n + r + 1], out)
    return out


def _shift_up(v, nxt, n):
    tm = v.shape[0]
    rolled = pltpu.roll(v, tm - n, 0)
    rid = lax.broadcasted_iota(jnp.int32, v.shape, 0)
    out = rolled
    for r in range(n):
        out = jnp.where(rid == tm - n + r, nxt[r : r + 1], out)
    return out


def _conv_cols(f):
    return _divisor(f, 256, 128)


def _ffn_fwd(x, gain, w_up, cw, cb, w_down, layer, name, rider=None):
    s, d = x.shape
    f = w_down.shape[0]
    tm = _divisor(s, FFN_ROWS, 16)
    cwid = _conv_cols(f)

    def body(x_ref, g_ref, wu_ref, cw_ref, cb_ref, wd_ref, out_ref, z_ref, h_ref, conv_ref, halo):
        @pl.when(pl.program_id(0) == 0)
        def _():
            halo[...] = jnp.zeros_like(halo)

        xv = x_ref[...]
        r = lax.rsqrt(jnp.mean(xv * xv, axis=-1, keepdims=True) + EPS)
        h = ((xv * r) * g_ref[...]).astype(MM_DTYPE)
        h_ref[...] = h
        acc = xv
        starts = list(range(0, f, cwid))

        def project(c0):
            return [jnp.dot(h, wu_ref[:, base : base + cwid], preferred_element_type=F32) for base in (c0, f + c0)]

        ahead = project(starts[0])
        for n, c0 in enumerate(starts):
            halves = []
            current = ahead
            if n + 1 < len(starts):
                ahead = project(starts[n + 1])
            for base, zc in zip((c0, f + c0), current):
                cols = slice(base, base + cwid)
                z_ref[:, cols] = zc.astype(MM_DTYPE)
                prev = halo[:, cols]
                conv = cb_ref[:, cols] + cw_ref[0:1, cols] * _shift_down(zc, prev, 2)
                conv = conv + cw_ref[1:2, cols] * _shift_down(zc, prev, 1) + cw_ref[2:3, cols] * zc
                halo[:, cols] = zc[tm - HALO : tm]
                conv_ref[:, cols] = conv.astype(MM_DTYPE)
                halves.append(conv)
            gate, val = halves
            act = gate * _sigmoid(gate) * val
            acc = acc + _dot(act, wd_ref[c0 : c0 + cwid, :])
        out_ref[...] = acc

    row = lambda w: pl.BlockSpec((tm, w), lambda i: (i, 0))
    return _hosted_call(
        body,
        rider,
        name=name,
        grid=(s // tm,),
        in_specs=[
            row(d),
            _resident((1, d), lambda i: (0, 0)),
            _resident((d, 2 * f), lambda i: (0, 0)),
            _resident((None, 3, 2 * f), lambda i: (layer, 0, 0)),
            _resident((1, 2 * f), lambda i: (0, 0)),
            _resident((f, d), lambda i: (0, 0)),
        ],
        out_specs=[row(d), row(2 * f), row(d), row(2 * f)],
        out_shape=[SDS((s, d), F32), SDS((s, 2 * f), MM_DTYPE), SDS((s, d), MM_DTYPE), SDS((s, 2 * f), MM_DTYPE)],
        scratch_shapes=[pltpu.VMEM((HALO, 2 * f), F32)],
        args=[x, gain, w_up, cw, cb, w_down],
        semantics=("arbitrary",),
    )


def _ffn_bwd(dx, z2, conv, cw, w_down, w_up, x, gain, layer, name):
    s, d = dx.shape
    f = z2.shape[1] // 2
    tm = _divisor(s, 256, 16)
    cwid = _conv_cols(f)
    n_steps = s // tm

    def body(dx_ref, z_ref, conv_ref, cw_ref, wd_ref, wu_ref, x_ref, g_ref, dz_ref, act_ref, dcw_ref, dx1_ref, dg_ref, nxt):
        @pl.when(pl.program_id(0) == 0)
        def _():
            nxt[...] = jnp.zeros_like(nxt)
            dcw_ref[...] = jnp.zeros_like(dcw_ref)
            dg_ref[...] = jnp.zeros_like(dg_ref)

        dxf = dx_ref[...]
        dxv = dxf.astype(MM_DTYPE)
        starts = list(range(0, f, cwid))
        ahead = _dot_nt(dxv, wd_ref[0:cwid, :])
        dh = jnp.zeros((tm, d), F32)
        pending = []
        for n, c0 in enumerate(starts):
            dact = ahead
            if n + 1 < len(starts):
                ahead = _dot_nt(dxv, wd_ref[starts[n + 1] : starts[n + 1] + cwid, :])
            for dz_prev, cols_prev in pending:
                dh = dh + _dot_nt(dz_prev, wu_ref[:, cols_prev])
            pending = []
            gate = conv_ref[:, c0 : c0 + cwid].astype(F32)
            val = conv_ref[:, f + c0 : f + c0 + cwid].astype(F32)
            sg = _sigmoid(gate)
            silu = gate * sg
            act_ref[:, c0 : c0 + cwid] = (silu * val).astype(MM_DTYPE)
            dconvs = (dact * val * _silu_grad(gate, sg), dact * silu)
            for base, dconv in zip((c0, f + c0), dconvs):
                cols = slice(base, base + cwid)
                zc = z_ref[:, cols].astype(F32)
                nx = nxt[:, cols]
                up1 = _shift_up(dconv, nx, 1)
                up2 = _shift_up(dconv, nx, 2)
                dcw_ref[0:1, cols] += jnp.sum(up2 * zc, axis=0, keepdims=True)
                dcw_ref[1:2, cols] += jnp.sum(up1 * zc, axis=0, keepdims=True)
                dcw_ref[2:3, cols] += jnp.sum(dconv * zc, axis=0, keepdims=True)
                dcw_ref[3:4, cols] += jnp.sum(dconv, axis=0, keepdims=True)
                dz = (cw_ref[2:3, cols] * dconv + cw_ref[1:2, cols] * up1 + cw_ref[0:1, cols] * up2).astype(MM_DTYPE)
                dz_ref[:, cols] = dz
                nxt[:, cols] = dconv[0:HALO]
                pending.append((dz, cols))
        for dz_prev, cols_prev in pending:
            dh = dh + _dot_nt(dz_prev, wu_ref[:, cols_prev])
        xv = x_ref[...]
        r = lax.rsqrt(jnp.mean(xv * xv, axis=-1, keepdims=True) + EPS)
        xh = xv * r
        dg_ref[...] += jnp.sum(dh * xh, axis=0, keepdims=True)
        dxh = dh * g_ref[...]
        dx1_ref[...] = dxf + r * (dxh - xh * jnp.mean(dxh * xh, axis=-1, keepdims=True))

    rev = lambda i: n_steps - 1 - i
    row = lambda w: pl.BlockSpec((tm, w), lambda i: (rev(i), 0))
    return pl.pallas_call(
        body,
        name=name,
        grid=(n_steps,),
        in_specs=[
            row(d),
            row(2 * f),
            row(2 * f),
            _resident((None, 3, 2 * f), lambda i: (layer, 0, 0)),
            _resident((f, d), lambda i: (0, 0)),
            _resident((d, 2 * f), lambda i: (0, 0)),
            row(d),
            _resident((1, d), lambda i: (0, 0)),
        ],
        out_specs=[
            row(2 * f),
            row(f),
            pl.BlockSpec((HALO, 2 * f), lambda i: (0, 0)),
            row(d),
            pl.BlockSpec((1, d), lambda i: (0, 0)),
        ],
        out_shape=[SDS((s, 2 * f), MM_DTYPE), SDS((s, f), MM_DTYPE), SDS((HALO, 2 * f), F32), SDS((s, d), F32), SDS((1, d), F32)],
        scratch_shapes=[pltpu.VMEM((HALO, 2 * f), F32)],
        compiler_params=_params("arbitrary"),
    )(dx, z2, conv, cw, w_down, w_up, x, gain)


def _loss_head(x, gain, target):
    s, d = x.shape
    tm = _divisor(s, 256, 8)

    def body(x_ref, g_ref, t_ref, loss_ref, dx_ref, dg_ref):
        @pl.when(pl.program_id(0) == 0)
        def _():
            loss_ref[...] = jnp.zeros_like(loss_ref)
            dg_ref[...] = jnp.zeros_like(dg_ref)

        xv = x_ref[...]
        gv = g_ref[...]
        r = lax.rsqrt(jnp.mean(xv * xv, axis=-1, keepdims=True) + EPS)
        xh = xv * r
        err = xh * gv - t_ref[...]
        loss_ref[...] += 0.5 * jnp.sum(jnp.mean(err * err, axis=-1, keepdims=True))
        dout = err * (1.0 / d)
        dg_ref[...] += jnp.sum(dout * xh, axis=0, keepdims=True)
        dxh = dout * gv
        dx_ref[...] = r * (dxh - xh * jnp.mean(dxh * xh, axis=-1, keepdims=True))

    return pl.pallas_call(
        body,
        name="loss_head",
        grid=(s // tm,),
        in_specs=[
            pl.BlockSpec((tm, d), lambda i: (i, 0)),
            _resident((1, d), lambda i: (0, 0)),
            pl.BlockSpec((tm, d), lambda i: (i, 0)),
        ],
        out_specs=[
            pl.BlockSpec((8, 128), lambda i: (0, 0)),
            pl.BlockSpec((tm, d), lambda i: (i, 0)),
            pl.BlockSpec((1, d), lambda i: (0, 0)),
        ],
        out_shape=[SDS((8, 128), F32), SDS((s, d), F32), SDS((1, d), F32)],
        compiler_params=_params("arbitrary"),
    )(x, gain, target)


def _adamw(w, g, m, v, name):
    shape = w.shape
    cols = shape[-1]
    rows = max(1, math.prod(shape[:-1]))
    tr = _divisor(rows, max(8, TILE_BYTES // (4 * cols)), 8)
    flat = [t.reshape(rows, cols) for t in (w, g, m, v)]

    def body(w_ref, g_ref, m_ref, v_ref, d_ref, nm_ref, nv_ref):
        gv = g_ref[...]
        m2 = ADAM_B1 * m_ref[...] + (1.0 - ADAM_B1) * gv
        v2 = ADAM_B2 * v_ref[...] + (1.0 - ADAM_B2) * (gv * gv)
        m_hat = m2 / (1.0 - ADAM_B1**ADAM_STEP)
        v_hat = v2 / (1.0 - ADAM_B2**ADAM_STEP)
        d_ref[...] = -ADAM_LR * (m_hat / (jnp.sqrt(v_hat) + ADAM_EPS) + ADAM_WD * w_ref[...])
        nm_ref[...] = m2
        nv_ref[...] = v2

    spec = pl.BlockSpec((tr, cols), lambda i: (i, 0))
    outs = pl.pallas_call(
        body,
        name=name,
        grid=(rows // tr,),
        in_specs=[spec] * 4,
        out_specs=[spec] * 3,
        out_shape=[SDS((rows, cols), F32)] * 3,
        compiler_params=_params("parallel"),
    )(*flat)
    return tuple(t.reshape(shape) for t in outs)


def _position():
    return lax.axis_index("x"), lax.axis_index("y"), lax.axis_index("c")


def _other_chips(x, y):
    return [(1 - x, y), (x, 1 - y), (1 - x, 1 - y)]


def _remote(src, dst, send_sem, recv_sem, device):
    return pltpu.make_async_remote_copy(
        src_ref=src, dst_ref=dst, send_sem=send_sem, recv_sem=recv_sem, device_id=device, device_id_type=MESH
    )


def _sub(ref, lead, shape, axis, chip=None, half=None):
    cut = [pl.ds(0, shape[0]), pl.ds(0, shape[1])]
    if chip is not None:
        size = shape[axis] // N_CHIPS
        cut[axis] = pl.ds(chip * size, size)
    if half is not None:
        size = shape[1 - axis] // 2
        cut[1 - axis] = pl.ds(half * size, size)
    return ref.at[(*lead, *cut)]


def _scaled(shape, axis, num, den=1):
    return tuple(d * num // den if i == axis else d for i, d in enumerate(shape))


def _gather_rider(shards, axes, layer, conv_w=None):
    n = len(shards)
    shard2d = [t.shape[1:] for t in shards]
    full2d = [_scaled(s2, ax, N_CHIPS) for s2, ax in zip(shard2d, axes)]
    out_shapes = [SDS(f2, t.dtype) for f2, t in zip(full2d, shards)]
    inputs = list(shards)
    own = 6
    scratch = [pltpu.SemaphoreType.DMA((n, 7)), pltpu.SemaphoreType.DMA((n, 7))]
    if conv_w is not None:
        cshape = conv_w.shape
        inputs.append(conv_w)
        out_shapes.append(SDS(_scaled(cshape, 2, N_CHIPS), conv_w.dtype))
        scratch += [pltpu.SemaphoreType.DMA((4,)), pltpu.SemaphoreType.DMA((4,))]

    def conv_slab(ref, chip):
        return ref.at[:, :, pl.ds((2 * chip[0] + chip[1]) * cshape[2], cshape[2])]

    def own_copy(ins, outs, send_sems, recv_sems, k):
        x, y, c = _position()
        slab = _sub(outs[k], (), full2d[k], axes[k], chip=2 * x + y)
        return _remote(ins[k].at[layer], slab, send_sems.at[k, own], recv_sems.at[k, own], (x, y, 1 - c))

    def start(ins, outs, scr):
        send_sems, recv_sems = scr[:2]
        x, y, c = _position()
        me = 2 * x + y
        for k in range(n):
            own_copy(ins, outs, send_sems, recv_sems, k).start()
            for j, chip in enumerate(_other_chips(x, y)):
                _remote(
                    _sub(ins[k], (layer,), shard2d[k], axes[k], half=c),
                    _sub(outs[k], (), full2d[k], axes[k], chip=me, half=c),
                    send_sems.at[k, j], recv_sems.at[k, j], (*chip, c),
                ).start()
        if conv_w is not None:
            csend, crecv = scr[2:]
            _remote(ins[n], conv_slab(outs[n], (x, y)), csend.at[3], crecv.at[3], (x, y, 1 - c)).start()
            for j, chip in enumerate(_other_chips(x, y)):
                _remote(ins[n], conv_slab(outs[n], (x, y)), csend.at[j], crecv.at[j], (*chip, c)).start()

    def finish(ins, outs, scr):
        send_sems, recv_sems = scr[:2]
        x, y, c = _position()
        me = 2 * x + y
        sibling = (x, y, 1 - c)
        chips = _other_chips(x, y)
        forwards = []
        for k in range(n):
            src = _sub(ins[k], (layer,), shard2d[k], axes[k], half=c)
            for j, chip in enumerate(chips):
                landed = _sub(outs[k], (), full2d[k], axes[k], chip=2 * chip[0] + chip[1], half=c)
                _remote(src, landed, send_sems.at[k, j], recv_sems.at[k, j], (*chip, c)).wait_recv()
                fwd = _remote(landed, landed, send_sems.at[k, 3 + j], recv_sems.at[k, 3 + j], sibling)
                fwd.start()
                forwards.append(fwd)
        for k in range(n):
            src = _sub(ins[k], (layer,), shard2d[k], axes[k], half=c)
            for j, chip in enumerate(chips):
                theirs = _sub(outs[k], (), full2d[k], axes[k], chip=2 * chip[0] + chip[1], half=1 - c)
                _remote(src, theirs, send_sems.at[k, 3 + j], recv_sems.at[k, 3 + j], sibling).wait_recv()
        for fwd in forwards:
            fwd.wait_send()
        for k in range(n):
            src = _sub(ins[k], (layer,), shard2d[k], axes[k], half=c)
            mine = _sub(outs[k], (), full2d[k], axes[k], chip=me, half=c)
            for j, chip in enumerate(chips):
                _remote(src, mine, send_sems.at[k, j], recv_sems.at[k, j], (*chip, c)).wait_send()
            own_copy(ins, outs, send_sems, recv_sems, k).wait()
        if conv_w is not None:
            csend, crecv = scr[2:]
            for j, chip in enumerate(chips):
                cp = _remote(ins[n], conv_slab(outs[n], chip), csend.at[j], crecv.at[j], (*chip, c))
                cp.wait_recv()
                cp.wait_send()
            _remote(ins[n], conv_slab(outs[n], (x, y)), csend.at[3], crecv.at[3], sibling).wait()

    return _Rider(inputs, out_shapes, scratch, start, finish)


def _pair_rider(stacks, axes, layer):
    n = len(stacks)
    shapes = [t.shape[1:] for t in stacks]
    out_shapes = [SDS(_scaled(s2, 1 - ax, 1, 2), F32) for s2, ax in zip(shapes, axes)]
    scratch = [pltpu.SemaphoreType.DMA((n,)), pltpu.SemaphoreType.DMA((n,))]

    def copies(ins, outs, scr):
        x, y, c = _position()
        return [
            _remote(_sub(ins[k], (layer,), shapes[k], axes[k], half=1 - c), outs[k], scr[0].at[k], scr[1].at[k], (x, y, 1 - c))
            for k in range(n)
        ]

    def start(ins, outs, scr):
        for cp in copies(ins, outs, scr):
            cp.start()

    def finish(ins, outs, scr):
        for cp in copies(ins, outs, scr):
            cp.wait()

    return _Rider(stacks, out_shapes, scratch, start, finish)


def _chip_rider(pairs, axes):
    n = len(pairs)
    shapes = [t.shape for t in pairs]
    out_shapes = [SDS((3,) + _scaled(s2, ax, 1, N_CHIPS), t.dtype) for s2, ax, t in zip(shapes, axes, pairs)]
    scratch = [pltpu.SemaphoreType.DMA((n, 3)), pltpu.SemaphoreType.DMA((n, 3))]

    def copies(ins, outs, scr):
        x, y, c = _position()
        return [
            _remote(
                _sub(ins[k], (), shapes[k], axes[k], chip=2 * chip[0] + chip[1]), outs[k].at[j],
                scr[0].at[k, j], scr[1].at[k, j], (*chip, c),
            )
            for k in range(n)
            for j, chip in enumerate(_other_chips(x, y))
        ]

    def start(ins, outs, scr):
        for cp in copies(ins, outs, scr):
            cp.start()

    def finish(ins, outs, scr):
        for cp in copies(ins, outs, scr):
            cp.wait()

    return _Rider(pairs, out_shapes, scratch, start, finish)


def _complete_rider(stacks, axes, layers):
    n = len(stacks)
    shapes = [t.shape[1:] for t in stacks]
    scratch = [pltpu.SemaphoreType.DMA((n,)), pltpu.SemaphoreType.DMA((n,))]

    def start(ins, outs, scr):
        x, y, c = _position()
        for k in range(n):
            mine = _sub(outs[k], (layers[k],), shapes[k], axes[k], half=c)
            _remote(mine, mine, scr[0].at[k], scr[1].at[k], (x, y, 1 - c)).start()

    def finish(ins, outs, scr):
        x, y, c = _position()
        for k in range(n):
            mine = _sub(outs[k], (layers[k],), shapes[k], axes[k], half=c)
            theirs = _sub(outs[k], (layers[k],), shapes[k], axes[k], half=1 - c)
            _remote(mine, theirs, scr[0].at[k], scr[1].at[k], (x, y, 1 - c)).wait_recv()
            _remote(mine, mine, scr[0].at[k], scr[1].at[k], (x, y, 1 - c)).wait_send()

    return _Rider(stacks, [SDS(t.shape, t.dtype) for t in stacks], scratch, start, finish, {k: k for k in range(n)})


def _small_rider(buf):
    rows, cols = buf.shape
    flips = [(fx, fy, fc) for fx in (0, 1) for fy in (0, 1) for fc in (0, 1)][1:]

    def peers():
        x, y, c = _position()
        return [(x + f[0] - 2 * x * f[0], y + f[1] - 2 * y * f[1], c + f[2] - 2 * c * f[2]) for f in flips]

    def start(ins, outs, scr):
        x, y, c = _position()
        mine = outs[0].at[4 * x + 2 * y + c]
        pltpu.make_async_copy(ins[0], mine, scr[2]).start()
        for j, peer in enumerate(peers()):
            _remote(ins[0], mine, scr[0].at[j], scr[1].at[j], peer).start()

    def finish(ins, outs, scr):
        x, y, c = _position()
        mine = outs[0].at[4 * x + 2 * y + c]
        for j, peer in enumerate(peers()):
            _remote(ins[0], outs[0].at[4 * peer[0] + 2 * peer[1] + peer[2]], scr[0].at[j], scr[1].at[j], peer).wait_recv()
        for j, peer in enumerate(peers()):
            _remote(ins[0], mine, scr[0].at[j], scr[1].at[j], peer).wait_send()
        pltpu.make_async_copy(ins[0], mine, scr[2]).wait()

    scratch = [pltpu.SemaphoreType.DMA((7,)), pltpu.SemaphoreType.DMA((7,)), pltpu.SemaphoreType.DMA(())]
    return _Rider([buf], [SDS((8, rows, cols), buf.dtype)], scratch, start, finish)


def _my_core():
    return lax.axis_index("c")


def _my_chip():
    return 2 * lax.axis_index("x") + lax.axis_index("y")


def _pair_sum(stack, layer, recv, axis, name):
    hk, hn = recv.shape
    tk = _divisor(hk, max(16, TILE_BYTES // (4 * hn)), 16)
    per = hk // tk

    def body(g_ref, r_ref, out_ref):
        out_ref[...] = (g_ref[...] + r_ref[...]).astype(out_ref.dtype)

    if axis == 1:
        mine = pl.BlockSpec((None, tk, hn), lambda i: (layer, _my_core() * per + i, 0))
    else:
        mine = pl.BlockSpec((None, tk, hn), lambda i: (layer, i, _my_core()))
    return pl.pallas_call(
        body,
        name=name,
        grid=(per,),
        in_specs=[mine, pl.BlockSpec((tk, hn), lambda i: (i, 0))],
        out_specs=pl.BlockSpec((tk, hn), lambda i: (i, 0)),
        out_shape=SDS((hk, hn), BF16),
        compiler_params=_params("parallel"),
    )(stack, recv)


def _chip_sum(pair, others, axis, stack, layer, name):
    _, sk, sn = others.shape
    tk = _divisor(sk, max(16, TILE_BYTES // (4 * sn)), 16)
    per = sk // tk

    def body(p_ref, o0_ref, o1_ref, o2_ref, *rest):
        out_ref = rest[-1]
        acc = p_ref[...].astype(F32) + o0_ref[...].astype(F32)
        out_ref[...] = (acc + o1_ref[...].astype(F32)) + o2_ref[...].astype(F32)

    if axis == 1:
        own = pl.BlockSpec((tk, sn), lambda i: (i, _my_chip()))
        out = pl.BlockSpec((None, tk, sn), lambda i: (layer, _my_core() * per + i, 0))
        shard = (2 * sk, sn)
    else:
        own = pl.BlockSpec((tk, sn), lambda i: (_my_chip() * per + i, 0))
        out = pl.BlockSpec((None, tk, sn), lambda i: (layer, i, _my_core()))
        shard = (sk, 2 * sn)
    other = lambda j: pl.BlockSpec((None, tk, sn), lambda i: (j, i, 0))
    in_specs = [own, other(0), other(1), other(2)]
    args = [pair, others, others, others]
    aliases = {}
    if stack is not None:
        in_specs.append(ANY)
        args.append(stack)
        aliases = {4: 0}
    return pl.pallas_call(
        body,
        name=name,
        grid=(per,),
        in_specs=in_specs,
        out_specs=out,
        out_shape=SDS((DEPTH,) + shard, F32),
        input_output_aliases=aliases,
        compiler_params=_params("parallel"),
    )(*args)


def _sum_devices(gathered, name):
    _, rows, cols = gathered.shape

    def body(g_ref, out_ref):
        acc = g_ref[0]
        for dev in range(1, 8):
            acc = acc + g_ref[dev]
        out_ref[...] = acc

    return pl.pallas_call(body, name=name, out_shape=SDS((rows, cols), F32))(gathered)


PACK_TILE = 8 * 128


def _pack(arrays):
    parts = []
    for t in arrays:
        flat = t.reshape(-1)
        pad = (-flat.shape[0]) % PACK_TILE
        if pad:
            flat = jnp.concatenate([flat, jnp.zeros((pad,), flat.dtype)])
        parts.append(flat.reshape(-1, 128))
    return jnp.concatenate(parts, axis=0)


def _unpack(buf, shapes):
    out, row = [], 0
    for shp in shapes:
        size = math.prod(shp)
        rows = -(-size // PACK_TILE) * 8
        out.append(buf[row : row + rows].reshape(-1)[:size].reshape(shp))
        row += rows
    return out


BIG = ("w_in", "w_br_gm", "w_br_hg", "w_out", "w_up", "w_down")
BIG_AXIS = (1, 1, 1, 0, 1, 0)
LAYER_SMALL = ("mix_norm", "gm_ln_g", "gm_ln_b", "gm_ws", "gm_bs", "lb", "hg_norm_g", "ffn_norm", "conv_w", "conv_b")
WEIGHTS = ("mix_norm", "w_in", "gm_ln_g", "gm_ln_b", "gm_ws", "gm_bs", "hg_lb_logits", "hg_norm_g", "w_br_gm", "w_br_hg", "w_out",
           "ffn_norm", "w_up", "conv_w", "conv_b", "w_down", "final_norm")


def kernel(x, mix_norm, w_in, gm_ln_g, gm_ln_b, gm_ws, gm_bs, hg_lb_logits, hg_norm_g, w_br_gm, w_br_hg, w_out, ffn_norm, w_up, conv_w, conv_b, w_down, final_norm, loss_target, m_mix_norm, m_w_in, m_gm_ln_g, m_gm_ln_b, m_gm_ws, m_gm_bs, m_hg_lb_logits, m_hg_norm_g, m_w_br_gm, m_w_br_hg, m_w_out, m_ffn_norm, m_w_up, m_conv_w, m_conv_b, m_w_down, m_final_norm, v_mix_norm, v_w_in, v_gm_ln_g, v_gm_ln_b, v_gm_ws, v_gm_bs, v_hg_lb_logits, v_hg_norm_g, v_w_br_gm, v_w_br_hg, v_w_out, v_ffn_norm, v_w_up, v_conv_w, v_conv_b, v_w_down, v_final_norm):
    w = dict(mix_norm=mix_norm, w_in=w_in, gm_ln_g=gm_ln_g, gm_ln_b=gm_ln_b, gm_ws=gm_ws, gm_bs=gm_bs, hg_lb_logits=hg_lb_logits,
             hg_norm_g=hg_norm_g, w_br_gm=w_br_gm, w_br_hg=w_br_hg, w_out=w_out, ffn_norm=ffn_norm, w_up=w_up, conv_w=conv_w,
             conv_b=conv_b, w_down=w_down, final_norm=final_norm)
    m = dict(mix_norm=m_mix_norm, w_in=m_w_in, gm_ln_g=m_gm_ln_g, gm_ln_b=m_gm_ln_b, gm_ws=m_gm_ws, gm_bs=m_gm_bs,
             hg_lb_logits=m_hg_lb_logits, hg_norm_g=m_hg_norm_g, w_br_gm=m_w_br_gm, w_br_hg=m_w_br_hg, w_out=m_w_out,
             ffn_norm=m_ffn_norm, w_up=m_w_up, conv_w=m_conv_w, conv_b=m_conv_b, w_down=m_w_down, final_norm=m_final_norm)
    v = dict(mix_norm=v_mix_norm, w_in=v_w_in, gm_ln_g=v_gm_ln_g, gm_ln_b=v_gm_ln_b, gm_ws=v_gm_ws, gm_bs=v_gm_bs,
             hg_lb_logits=v_hg_lb_logits, hg_norm_g=v_hg_norm_g, w_br_gm=v_w_br_gm, w_br_hg=v_w_br_hg, w_out=v_w_out,
             ffn_norm=v_ffn_norm, w_up=v_w_up, conv_w=v_conv_w, conv_b=v_conv_b, w_down=v_w_down, final_norm=v_final_norm)

    axes = list(BIG_AXIS)
    d = x.shape[2]

    shards = [w[k].astype(MM_DTYPE) for k in BIG]
    w_in_0, conv_full = _run_rider(_gather_rider(shards[:1], axes[:1], 0, conv_w=conv_w), "gather_weights_0")
    full = [dict(w_in=w_in_0)]
    lb = _lower_bounds(hg_lb_logits)
    xs = x[0]
    saved = []
    mixer = 4
    for l in range(DEPTH):
        fw = full[l]
        more = l + 1 < DEPTH
        bst = gm_bs[l].T
        (z, h, a), got = _in_proj(
            xs, mix_norm[l : l + 1], fw["w_in"], gm_ln_g[l : l + 1], gm_ln_b[l : l + 1], gm_ws[l], bst, f"in_proj_{l}",
            _gather_rider(shards[mixer:], axes[mixer:], l),
        )
        fw.update(zip(BIG[mixer:], got))
        rider = _gather_rider(shards[:1], axes[:1], l + 1) if more else None
        if l == 0:
            rider = _join(rider, _gather_rider(shards[1:mixer], axes[1:mixer], 0))
        (o, b, states), got = _hgrn_fwd(z, lb[l : l + 1], hg_norm_g[l : l + 1], f"hgrn_fwd_{l}", rider)
        if more:
            full.append(dict(w_in=got[0]))
        if l == 0:
            fw.update(zip(BIG[1:mixer], got[1:]))
        rider = _gather_rider(shards[1:mixer], axes[1:mixer], l + 1) if more else None
        (x1,), got = _mix_fwd(xs, a, b, z, fw["w_br_gm"], fw["w_br_hg"], fw["w_out"], f"mix_fwd_{l}", rider)
        if more:
            full[l + 1].update(zip(BIG[1:mixer], got))
        (x2, z2, h2, conv), _ = _ffn_fwd(
            x1, ffn_norm[l : l + 1], fw["w_up"], conv_full, conv_b[l : l + 1], fw["w_down"], l, f"ffn_fwd_{l}"
        )
        saved.append((xs, h, z, a, b, o, states, x1, h2, z2, conv, bst))
        xs = x2

    loss_tile, dx, d_final = _loss_head(xs, final_norm.reshape(1, d), loss_target[0])
    loss = lax.psum(loss_tile[0, 0], ("x", "y", "c"))

    big = {k: None for k in BIG}
    grads = [None] * len(BIG)
    per_layer = [None] * DEPTH
    mix_ids, ffn_ids = list(range(mixer)), list(range(mixer, len(BIG)))
    mix_axes, ffn_axes = axes[:mixer], axes[mixer:]
    mix_pairs = None
    mix_summed = None

    def pair_sums(ids, received, layer):
        return [_pair_sum(big[BIG[k]], layer, r, axes[k], f"pair_sum_{BIG[k]}_{layer}") for k, r in zip(ids, received)]

    def chip_sums(ids, pairs, others, layer):
        for k, p, ot in zip(ids, pairs, others):
            grads[k] = _chip_sum(p, ot, axes[k], grads[k], layer, f"chip_sum_{BIG[k]}_{layer}")

    for l in reversed(range(DEPTH)):
        x0, h, z, a, b, o, states, x1, h2, z2, conv, bst = saved[l]
        fw = full[l]
        dz2, act, dconv, dx1, d_ffn = _ffn_bwd(
            dx, z2, conv, conv_full, fw["w_down"], fw["w_up"], x1, ffn_norm[l : l + 1], l, f"ffn_bwd_{l}"
        )
        big["w_down"] = _matmul_tn(act, dx, big["w_down"], l, f"dw_down_{l}")
        big["w_up"] = _matmul_tn(h2, dz2, big["w_up"], l, f"dw_up_{l}")
        rider = _pair_rider([big[BIG[k]] for k in ffn_ids], ffn_axes, l)
        if mix_pairs is not None:
            rider = _join(rider, _chip_rider(mix_pairs, mix_axes))
        (y, dpa, dpb, da, db, dgates), got = _mix_bwd(
            dx1, a, b, z, fw["w_br_gm"], fw["w_br_hg"], fw["w_out"], f"mix_bwd_{l}", rider
        )
        if mix_pairs is not None:
            chip_sums(mix_ids, mix_pairs, got[len(ffn_ids) :], l + 1)
            mix_summed = l + 1
        ffn_pairs = pair_sums(ffn_ids, got[: len(ffn_ids)], l)
        big["w_out"] = _matmul_tn(y, dx1, big["w_out"], l, f"dw_out_{l}")
        big["w_br_gm"] = _matmul_tn(a, dpa, big["w_br_gm"], l, f"dw_br_gm_{l}")
        big["w_br_hg"] = _matmul_tn(b, dpb, big["w_br_hg"], l, f"dw_br_hg_{l}")
        (dz_hg, d_lb, d_ng), others = _hgrn_bwd(
            db, z, o, states, lb[l : l + 1], hg_norm_g[l : l + 1], f"hgrn_bwd_{l}", _chip_rider(ffn_pairs, ffn_axes)
        )
        chip_sums(ffn_ids, ffn_pairs, others, l)
        dz_gm, d_lng, d_lnb, d_ws, d_bst = _gmlp_bwd(da, z, gm_ln_g[l : l + 1], gm_ln_b[l : l + 1], gm_ws[l], bst, f"gmlp_bwd_{l}")
        n_in = fw["w_in"].shape[1]
        big["w_in"] = _matmul_tn(h, dz_gm, big["w_in"], l, f"dw_in_gm_{l}", n_total=n_in, col_off=0)
        big["w_in"] = _matmul_tn(h, dz_hg, big["w_in"], l, f"dw_in_hg_{l}", n_total=n_in, col_off=2 * BR_DIM)
        big["w_in"] = _matmul_tn(h, dgates, big["w_in"], l, f"dw_in_gate_{l}", n_total=n_in, col_off=6 * BR_DIM)
        done_ids = ffn_ids + (mix_ids if mix_summed is not None else [])
        done_layers = [l] * len(ffn_ids) + ([mix_summed] * len(mix_ids) if mix_summed is not None else [])
        rider = _join(
            _pair_rider([big[BIG[k]] for k in mix_ids], mix_axes, l),
            _complete_rider([grads[k] for k in done_ids], [axes[k] for k in done_ids], done_layers),
        )
        if l == 0:
            upper = [_pack([per_layer[j][k] for k in LAYER_SMALL]) for j in range(1, DEPTH)] + [_pack([d_final[0]])]
            rider = _join(rider, _small_rider(jnp.concatenate(upper, axis=0)))
        (dx, d_mix), got = _in_proj_bwd([dz_gm, dz_hg, dgates], fw["w_in"], x0, mix_norm[l : l + 1], dx1, f"in_proj_bwd_{l}", rider)
        for k, g in zip(done_ids, got[len(mix_ids) : len(mix_ids) + len(done_ids)]):
            grads[k] = g
        gathered_upper = got[len(mix_ids) + len(done_ids) :]
        mix_pairs = pair_sums(mix_ids, got[: len(mix_ids)], l)
        per_layer[l] = dict(
            mix_norm=d_mix[0], gm_ln_g=d_lng[0], gm_ln_b=d_lnb[0], gm_ws=d_ws, gm_bs=d_bst.T, lb=d_lb[0], hg_norm_g=d_ng[0],
            ffn_norm=d_ffn[0], conv_w=dconv[0:3], conv_b=dconv[3],
        )

    got = _run_rider(
        _join(_chip_rider(mix_pairs, mix_axes), _small_rider(_pack([per_layer[0][k] for k in LAYER_SMALL]))), "grad_exchange_0"
    )
    chip_sums(mix_ids, mix_pairs, got[: len(mix_ids)], 0)
    done = _run_rider(_complete_rider([grads[k] for k in mix_ids], mix_axes, [0] * len(mix_ids)), "grad_complete_0")
    for k, g in zip(mix_ids, done):
        grads[k] = g
    grad = dict(zip(BIG, grads))

    layer_shapes = [per_layer[0][k].shape for k in LAYER_SMALL]
    summed_upper = _unpack(_sum_devices(gathered_upper[0], "sum_small_upper"), layer_shapes * (DEPTH - 1) + [d_final[0].shape])
    summed_0 = _unpack(_sum_devices(got[len(mix_ids)], "sum_small_0"), layer_shapes)
    by_layer = [summed_0] + [summed_upper[j * len(LAYER_SMALL) : (j + 1) * len(LAYER_SMALL)] for j in range(DEPTH - 1)]
    small = {k: jnp.stack([by_layer[j][i] for j in range(DEPTH)]) for i, k in enumerate(LAYER_SMALL)}
    for k in LAYER_SMALL:
        if k != "lb":
            grad[k] = small[k]
    grad["hg_lb_logits"] = _lower_bounds_bwd(hg_lb_logits, small["lb"])
    grad["final_norm"] = summed_upper[-1]
    grad["conv_w"] = lax.dynamic_slice_in_dim(grad["conv_w"], _my_chip() * conv_w.shape[2], conv_w.shape[2], axis=2)

    delta, new_m, new_v = {}, {}, {}
    for k in WEIGHTS:
        delta[k], new_m[k], new_v[k] = _adamw(w[k], grad[k], m[k], v[k], f"adamw_{k}")

    return (loss, dx[None], *[grad[k] for k in WEIGHTS], *[delta[k] for k in WEIGHTS], *[new_m[k] for k in WEIGHTS],
            *[new_v[k] for k in WEIGHTS])
```

```python
import math

import jax
import jax.numpy as jnp
from jax import lax
from jax.experimental import pallas as pl
from jax.experimental.pallas import tpu as pltpu

F32 = jnp.float32
BF16 = jnp.bfloat16
MM_DTYPE = BF16

N_HEADS = 4
HEAD_DIM = 128
BR_DIM = N_HEADS * HEAD_DIM
CHUNK = 64
GM_BLOCK = 128
DEPTH = 4
N_CHIPS = 4
EPS = 1e-6
TINY = 1e-30
LB_MAX = 0.999
ADAM_LR = 0.001
ADAM_B1 = 0.9
ADAM_B2 = 0.999
ADAM_EPS = 1e-08
ADAM_WD = 0.01
ADAM_STEP = 10
INV_SQRT2 = 1.0 / math.sqrt(2.0)
INV_SQRT_2PI = 1.0 / math.sqrt(2.0 * math.pi)

VMEM_LIMIT_BYTES = 56 * 1024 * 1024
TILE_BYTES = 2 * 1024 * 1024
ACC_BYTES = 6 * 1024 * 1024
MESH = pl.DeviceIdType.MESH
SDS = jax.ShapeDtypeStruct
ANY = pl.BlockSpec(memory_space=pl.ANY)


def _params(*sem):
    return pltpu.CompilerParams(dimension_semantics=sem or None, vmem_limit_bytes=VMEM_LIMIT_BYTES)


def _divisor(n, limit, mult):
    best = None
    for d in range(mult, min(n, limit) + 1, mult):
        if n % d == 0:
            best = d
    return best if best is not None else n


def _dot(a, b):
    return jnp.dot(a.astype(MM_DTYPE), b.astype(MM_DTYPE), preferred_element_type=F32)


def _dot_nt(a, b):
    return lax.dot_general(a.astype(MM_DTYPE), b.astype(MM_DTYPE), (((1,), (1,)), ((), ())), preferred_element_type=F32)


def _dot_tn(a, b):
    return lax.dot_general(a.astype(MM_DTYPE), b.astype(MM_DTYPE), (((0,), (0,)), ((), ())), preferred_element_type=F32)


def _sigmoid(x):
    return 1.0 / (1.0 + jnp.exp(-x))


def _gelu(x):
    return 0.5 * x * (1.0 + lax.erf(x * INV_SQRT2))


def _gelu_grad(x):
    return 0.5 * (1.0 + lax.erf(x * INV_SQRT2)) + x * jnp.exp(-0.5 * x * x) * INV_SQRT_2PI


def _silu_grad(x, s):
    return s * (1.0 + x * (1.0 - s))


def _resident(shape, index_map):
    return pl.BlockSpec(shape, index_map, pipeline_mode=pl.Buffered(1))


class _Rider:
    def __init__(self, inputs, out_shapes, scratch, start, finish, aliases=None):
        self.inputs = list(inputs)
        self.out_shapes = list(out_shapes)
        self.scratch = list(scratch)
        self.start = start
        self.finish = finish
        self.aliases = dict(aliases or {})


def _join(a, b):
    if a is None or b is None:
        return a if b is None else b
    na, oa, sa = len(a.inputs), len(a.out_shapes), len(a.scratch)

    def start(i, o, s):
        a.start(i[:na], o[:oa], s[:sa])
        b.start(i[na:], o[oa:], s[sa:])

    def finish(i, o, s):
        a.finish(i[:na], o[:oa], s[:sa])
        b.finish(i[na:], o[oa:], s[sa:])

    aliases = dict(a.aliases)
    aliases.update({na + k: oa + v for k, v in b.aliases.items()})
    return _Rider(a.inputs + b.inputs, a.out_shapes + b.out_shapes, a.scratch + b.scratch, start, finish, aliases)


def _hosted_call(body, rider, *, name, grid, in_specs, out_specs, out_shape, scratch_shapes, args, semantics):
    n_in, n_out, n_scr = len(in_specs), len(out_specs), len(scratch_shapes)
    if rider is None:
        outs = pl.pallas_call(
            body, name=name, grid=grid, in_specs=in_specs, out_specs=out_specs, out_shape=out_shape,
            scratch_shapes=scratch_shapes, compiler_params=_params(*semantics),
        )(*args)
        return list(outs), []
    ri, ro = len(rider.inputs), len(rider.out_shapes)

    def wrapped(*refs):
        p = 0
        hin = refs[p : p + n_in]
        p += n_in
        rin = refs[p : p + ri]
        p += ri
        hout = refs[p : p + n_out]
        p += n_out
        rout = refs[p : p + ro]
        p += ro
        hscr = refs[p : p + n_scr]
        rscr = refs[p + n_scr :]

        @pl.when(pl.program_id(0) == 0)
        def _():
            rider.start(rin, rout, rscr)

        body(*hin, *hout, *hscr)

        @pl.when(pl.program_id(0) == grid[0] - 1)
        def _():
            rider.finish(rin, rout, rscr)

    outs = pl.pallas_call(
        wrapped,
        name=name,
        grid=grid,
        in_specs=list(in_specs) + [ANY] * ri,
        out_specs=list(out_specs) + [ANY] * ro,
        out_shape=list(out_shape) + rider.out_shapes,
        scratch_shapes=list(scratch_shapes) + rider.scratch,
        input_output_aliases={n_in + k: n_out + v for k, v in rider.aliases.items()},
        compiler_params=_params(*semantics),
    )(*args, *rider.inputs)
    return list(outs[:n_out]), list(outs[n_out:])


def _run_rider(rider, name):
    ri, ro = len(rider.inputs), len(rider.out_shapes)

    def body(*refs):
        rin, rout, rscr = refs[:ri], refs[ri : ri + ro], refs[ri + ro :]
        rider.start(rin, rout, rscr)
        rider.finish(rin, rout, rscr)

    return list(
        pl.pallas_call(
            body,
            name=name,
            in_specs=[ANY] * ri,
            out_specs=[ANY] * ro,
            out_shape=rider.out_shapes,
            scratch_shapes=rider.scratch,
            input_output_aliases=rider.aliases,
        )(*rider.inputs)
    )


def _lower_bounds(logits):
    def body(lg_ref, lb_ref):
        lg = lg_ref[...]
        mx = jnp.max(lg, axis=0, keepdims=True)
        e = jnp.exp(lg - mx)
        p = e / jnp.sum(e, axis=0, keepdims=True)
        run = p[0:1]
        rows = [run - p[0:1]]
        for l in range(1, DEPTH):
            run = run + p[l : l + 1]
            rows.append(run - p[0:1])
        raw = jnp.concatenate(rows, axis=0)
        lb_ref[...] = jnp.minimum(jnp.maximum(raw, 0.0), LB_MAX)

    return pl.pallas_call(body, name="lower_bounds", out_shape=SDS(logits.shape, F32))(logits)


def _lower_bounds_bwd(logits, dlb):
    def body(lg_ref, dlb_ref, dlg_ref):
        lg = lg_ref[...]
        mx = jnp.max(lg, axis=0, keepdims=True)
        e = jnp.exp(lg - mx)
        p = e / jnp.sum(e, axis=0, keepdims=True)
        run = p[0:1]
        rows = [run - p[0:1]]
        for l in range(1, DEPTH):
            run = run + p[l : l + 1]
            rows.append(run - p[0:1])
        raw = jnp.concatenate(rows, axis=0)
        lo = jnp.where(raw > 0.0, 1.0, jnp.where(raw == 0.0, 0.5, 0.0))
        clipped = jnp.maximum(raw, 0.0)
        hi = jnp.where(clipped < LB_MAX, 1.0, jnp.where(clipped == LB_MAX, 0.5, 0.0))
        draw = dlb_ref[...] * lo * hi
        total = jnp.sum(draw, axis=0, keepdims=True)
        dps = []
        tail = total
        for j in range(DEPTH):
            dps.append(tail - total if j == 0 else tail)
            tail = tail - draw[j : j + 1]
        dp = jnp.concatenate(dps, axis=0)
        dlg_ref[...] = p * (dp - jnp.sum(p * dp, axis=0, keepdims=True))

    return pl.pallas_call(body, name="lower_bounds_bwd", out_shape=SDS(logits.shape, F32))(logits, dlb)


def _in_proj(x, gain, w, lng, lnb, ws, bst, name, rider=None):
    s, d = x.shape
    n = w.shape[1]
    tm = _divisor(s, 512, GM_BLOCK)

    def body(x_ref, g_ref, w_ref, lng_ref, lnb_ref, ws_ref, bst_ref, z_ref, h_ref, a_ref):
        xv = x_ref[...]
        r = lax.rsqrt(jnp.mean(xv * xv, axis=-1, keepdims=True) + EPS)
        h = ((xv * r) * g_ref[...]).astype(MM_DTYPE)
        h_ref[...] = h
        uv = jnp.dot(h, w_ref[:, : 2 * BR_DIM], preferred_element_type=F32)
        z_ref[:, : 2 * BR_DIM] = uv
        z_ref[:, 2 * BR_DIM :] = jnp.dot(h, w_ref[:, 2 * BR_DIM :], preferred_element_type=F32)
        _gmlp_fwd_tile(uv[:, :BR_DIM], uv[:, BR_DIM:], lng_ref, lnb_ref, ws_ref, bst_ref, a_ref)

    row = lambda wd: pl.BlockSpec((tm, wd), lambda i: (i, 0))
    return _hosted_call(
        body,
        rider,
        name=name,
        grid=(s // tm,),
        in_specs=[row(d), _resident((1, d), lambda i: (0, 0)), _resident((d, n), lambda i: (0, 0))] + _gmlp_param_specs(),
        out_specs=[row(n), row(d), row(BR_DIM)],
        out_shape=[SDS((s, n), F32), SDS((s, d), MM_DTYPE), SDS((s, BR_DIM), MM_DTYPE)],
        scratch_shapes=[],
        args=[x, gain, w, lng, lnb, ws, bst],
        semantics=("arbitrary",),
    )


def _in_proj_bwd(dz_parts, w, x, gain, dres, name, rider=None):
    s, d = x.shape
    n = w.shape[1]
    tm = _divisor(s, 512, 16)
    widths = [p.shape[1] for p in dz_parts]
    offsets = [sum(widths[:k]) for k in range(len(widths))]
    n_parts = len(dz_parts)

    def body(*refs):
        dz_refs = refs[:n_parts]
        w_ref, x_ref, g_ref, dres_ref, dx_ref, dg_ref = refs[n_parts:]

        @pl.when(pl.program_id(0) == 0)
        def _():
            dg_ref[...] = jnp.zeros_like(dg_ref)

        dh = None
        for dz_ref, off, wd in zip(dz_refs, offsets, widths):
            part = _dot_nt(dz_ref[...], w_ref[:, off : off + wd])
            dh = part if dh is None else dh + part
        xv = x_ref[...]
        r = lax.rsqrt(jnp.mean(xv * xv, axis=-1, keepdims=True) + EPS)
        xh = xv * r
        dg_ref[...] += jnp.sum(dh * xh, axis=0, keepdims=True)
        dxh = dh * g_ref[...]
        dx_ref[...] = dres_ref[...] + r * (dxh - xh * jnp.mean(dxh * xh, axis=-1, keepdims=True))

    in_specs = [pl.BlockSpec((tm, wd), lambda i: (i, 0)) for wd in widths] + [
        _resident((d, n), lambda i: (0, 0)),
        pl.BlockSpec((tm, d), lambda i: (i, 0)),
        _resident((1, d), lambda i: (0, 0)),
        pl.BlockSpec((tm, d), lambda i: (i, 0)),
    ]
    return _hosted_call(
        body,
        rider,
        name=name,
        grid=(s // tm,),
        in_specs=in_specs,
        out_specs=[pl.BlockSpec((tm, d), lambda i: (i, 0)), pl.BlockSpec((1, d), lambda i: (0, 0))],
        out_shape=[SDS((s, d), F32), SDS((1, d), F32)],
        scratch_shapes=[],
        args=[*dz_parts, w, x, gain, dres],
        semantics=("arbitrary",),
    )


def _matmul_tn(a, b, stack, layer, name, n_total=None, col_off=0):
    s, k = a.shape
    n = b.shape[1]
    n_total = n if n_total is None else n_total
    tn = _divisor(math.gcd(n, col_off) if col_off else n, max(128, ACC_BYTES // (4 * k)), 128)
    ts = _divisor(s, min(2048, max(512, ACC_BYTES // (2 * k))), 16)
    off = col_off // tn

    def body(a_ref, b_ref, *rest):
        out_ref = rest[-1]

        @pl.when(pl.program_id(1) == 0)
        def _():
            out_ref[...] = jnp.zeros_like(out_ref)

        out_ref[...] += _dot_tn(a_ref[...], b_ref[...])

    in_specs = [pl.BlockSpec((ts, k), lambda j, i: (i, 0)), pl.BlockSpec((ts, tn), lambda j, i: (i, j))]
    args = [a, b]
    aliases = {}
    if stack is not None:
        in_specs.append(ANY)
        args.append(stack)
        aliases = {2: 0}
    return pl.pallas_call(
        body,
        name=name,
        grid=(n // tn, s // ts),
        in_specs=in_specs,
        out_specs=pl.BlockSpec((None, k, tn), lambda j, i: (layer, 0, off + j)),
        out_shape=SDS((DEPTH, k, n_total), F32),
        input_output_aliases=aliases,
        compiler_params=_params("parallel", "arbitrary"),
    )(*args)


def _gm_mask():
    r = lax.broadcasted_iota(jnp.int32, (GM_BLOCK, GM_BLOCK), 0) // CHUNK
    c = lax.broadcasted_iota(jnp.int32, (GM_BLOCK, GM_BLOCK), 1) // CHUNK
    return r >= c


def _gm_normed(v, lng, lnb):
    vg = _gelu(v)
    mu = jnp.mean(vg, axis=-1, keepdims=True)
    xc = vg - mu
    rstd = lax.rsqrt(jnp.mean(xc * xc, axis=-1, keepdims=True) + EPS)
    xh = xc * rstd
    return xh, rstd, xh * lng + lnb


def _gmlp_param_specs():
    return [
        _resident((1, BR_DIM), lambda i: (0, 0)),
        _resident((1, BR_DIM), lambda i: (0, 0)),
        _resident((N_HEADS, GM_BLOCK, GM_BLOCK), lambda i: (0, 0, 0)),
        _resident((GM_BLOCK, N_HEADS), lambda i: (0, 0)),
    ]


def _gmlp_fwd_tile(u, v, lng_ref, lnb_ref, ws_ref, bst_ref, a_ref):
    mask = _gm_mask()
    ug = _gelu(u)
    _, _, vn = _gm_normed(v, lng_ref[...], lnb_ref[...])
    vn = vn.astype(MM_DTYPE)
    for h in range(N_HEADS):
        cs = slice(h * HEAD_DIM, (h + 1) * HEAD_DIM)
        wm = jnp.where(mask, ws_ref[h], 0.0).astype(MM_DTYPE)
        for blk in range(u.shape[0] // GM_BLOCK):
            rs = slice(blk * GM_BLOCK, (blk + 1) * GM_BLOCK)
            mixed = _dot(wm, vn[rs, cs]) + bst_ref[:, h : h + 1]
            a_ref[rs, cs] = (ug[rs, cs] * mixed).astype(MM_DTYPE)


def _gmlp_bwd_tile(da_v, u, v, lng_ref, lnb_ref, ws_ref, bst_ref, dz_ref, dlng_ref, dlnb_ref, dws_ref, dbst_ref):
    mask = _gm_mask()
    ug = _gelu(u)
    lng_v = lng_ref[...]
    xh, rstd, vn = _gm_normed(v, lng_v, lnb_ref[...])
    vnb = vn.astype(MM_DTYPE)
    dmix = da_v * ug
    dmixb = dmix.astype(MM_DTYPE)
    dvn_cols = []
    for h in range(N_HEADS):
        cs = slice(h * HEAD_DIM, (h + 1) * HEAD_DIM)
        wm = jnp.where(mask, ws_ref[h], 0.0).astype(MM_DTYPE)
        dw = jnp.zeros((GM_BLOCK, GM_BLOCK), F32)
        dbias = jnp.zeros((GM_BLOCK, 1), F32)
        dvn_rows = []
        for blk in range(u.shape[0] // GM_BLOCK):
            rs = slice(blk * GM_BLOCK, (blk + 1) * GM_BLOCK)
            mixed = _dot(wm, vnb[rs, cs]) + bst_ref[:, h : h + 1]
            dz_ref[rs, cs] = (da_v[rs, cs] * mixed * _gelu_grad(u[rs, cs])).astype(MM_DTYPE)
            dw = dw + _dot_nt(dmixb[rs, cs], vnb[rs, cs])
            dbias = dbias + jnp.sum(dmix[rs, cs], axis=1, keepdims=True)
            dvn_rows.append(_dot_tn(wm, dmixb[rs, cs]))
        dws_ref[h] += jnp.where(mask, dw, 0.0)
        dbst_ref[:, h : h + 1] += dbias
        dvn_cols.append(jnp.concatenate(dvn_rows, axis=0) if len(dvn_rows) > 1 else dvn_rows[0])
    dvn = jnp.concatenate(dvn_cols, axis=1)
    dlng_ref[...] += jnp.sum(dvn * xh, axis=0, keepdims=True)
    dlnb_ref[...] += jnp.sum(dvn, axis=0, keepdims=True)
    dxh = dvn * lng_v
    dvg = rstd * (dxh - jnp.mean(dxh, axis=-1, keepdims=True) - xh * jnp.mean(dxh * xh, axis=-1, keepdims=True))
    dz_ref[:, BR_DIM:] = (dvg * _gelu_grad(v)).astype(MM_DTYPE)


def _gmlp_bwd(da, z, lng, lnb, ws, bst, name):
    s = z.shape[0]
    tg = _divisor(s, 256, GM_BLOCK)

    def body(da_ref, z_ref, lng_ref, lnb_ref, ws_ref, bst_ref, dz_ref, dlng_ref, dlnb_ref, dws_ref, dbst_ref):
        @pl.when(pl.program_id(0) == 0)
        def _():
            for ref in (dlng_ref, dlnb_ref, dws_ref, dbst_ref):
                ref[...] = jnp.zeros_like(ref)

        _gmlp_bwd_tile(da_ref[...], z_ref[:, :BR_DIM], z_ref[:, BR_DIM:], lng_ref, lnb_ref, ws_ref, bst_ref,
                       dz_ref, dlng_ref, dlnb_ref, dws_ref, dbst_ref)

    fixed = lambda shape: pl.BlockSpec(shape, lambda i: tuple(0 for _ in shape))
    return pl.pallas_call(
        body,
        name=name,
        grid=(s // tg,),
        in_specs=[pl.BlockSpec((tg, BR_DIM), lambda i: (i, 0)), pl.BlockSpec((tg, 2 * BR_DIM), lambda i: (i, 0))] + _gmlp_param_specs(),
        out_specs=[
            pl.BlockSpec((tg, 2 * BR_DIM), lambda i: (i, 0)),
            fixed((1, BR_DIM)),
            fixed((1, BR_DIM)),
            fixed((N_HEADS, GM_BLOCK, GM_BLOCK)),
            fixed((GM_BLOCK, N_HEADS)),
        ],
        out_shape=[
            SDS((s, 2 * BR_DIM), MM_DTYPE),
            SDS((1, BR_DIM), F32),
            SDS((1, BR_DIM), F32),
            SDS((N_HEADS, GM_BLOCK, GM_BLOCK), F32),
            SDS((GM_BLOCK, N_HEADS), F32),
        ],
        compiler_params=_params("arbitrary"),
    )(da, z, lng, lnb, ws, bst)


HG_ROWS = 256
HG_UNROLL = 4
MID = CHUNK // 2 - 1


def _tri(lower):
    r = lax.broadcasted_iota(jnp.int32, (CHUNK, CHUNK), 0)
    c = lax.broadcasted_iota(jnp.int32, (CHUNK, CHUNK), 1)
    return r >= c if lower else c >= r


def _scan_rows(x, reverse=False):
    n = x.shape[0]
    rid = lax.broadcasted_iota(jnp.int32, x.shape, 0)
    k = 1
    while k < n:
        if reverse:
            x = x + jnp.where(rid < n - k, pltpu.roll(x, n - k, 0), 0.0)
        else:
            x = x + jnp.where(rid >= k, pltpu.roll(x, k, 0), 0.0)
        k *= 2
    return x


def _hgrn_fwd(z, lb, ng, name, rider=None):
    s = z.shape[0]
    rows_per = _divisor(s, HG_ROWS, CHUNK)
    n_sub = rows_per // CHUNK

    def body(zqf_ref, zio_ref, lb_ref, ng_ref, o_ref, b_ref, st_ref, state):
        @pl.when(pl.program_id(0) == 0)
        def _():
            state[...] = jnp.zeros_like(state)

        low = _tri(True)
        ng_v = ng_ref[...]

        def chunk(ci, carry):
            rows = pl.ds(pl.multiple_of(ci * CHUNK, CHUNK), CHUNK)
            for h in range(N_HEADS):
                cs = slice(h * HEAD_DIM, (h + 1) * HEAD_DIM)
                cs2 = slice(BR_DIM + h * HEAD_DIM, BR_DIM + (h + 1) * HEAD_DIM)
                zq = zqf_ref[rows, cs]
                zf = zqf_ref[rows, cs2]
                vi = zio_ref[rows, cs]
                zog = zio_ref[rows, cs2]
                lbh = lb_ref[:, cs]
                qf = zq * _sigmoid(zq)
                forget = lbh + (1.0 - lbh) * _sigmoid(zf)
                g = jnp.log(jnp.maximum(forget, TINY))
                k = (1.0 - lbh) * _sigmoid(-zf)
                gc = _scan_rows(g)
                gm = gc[MID : MID + 1]
                gl = gc[CHUNK - 1 : CHUNK]
                a = jnp.where(low, _dot_nt(qf * jnp.exp(gc - gm), k * jnp.exp(gm - gc)), 0.0)
                st = state[h]
                st_ref[ci, h] = st
                o = _dot(a, vi) + _dot_nt(qf * jnp.exp(gc), st)
                state[h] = st * jnp.exp(gl) + _dot_tn(vi, k * jnp.exp(gl - gc))
                r = lax.rsqrt(jnp.mean(o * o, axis=-1, keepdims=True) + EPS)
                o_ref[rows, cs] = o
                b_ref[rows, cs] = (((o * r) * ng_v) * (zog * _sigmoid(zog))).astype(MM_DTYPE)
            return carry

        lax.fori_loop(0, n_sub, chunk, 0, unroll=math.gcd(n_sub, HG_UNROLL))

    return _hosted_call(
        body,
        rider,
        name=name,
        grid=(s // rows_per,),
        in_specs=[
            pl.BlockSpec((rows_per, 2 * BR_DIM), lambda i: (i, 1)),
            pl.BlockSpec((rows_per, 2 * BR_DIM), lambda i: (i, 2)),
            _resident((1, BR_DIM), lambda i: (0, 0)),
            _resident((1, HEAD_DIM), lambda i: (0, 0)),
        ],
        out_specs=[
            pl.BlockSpec((rows_per, BR_DIM), lambda i: (i, 0)),
            pl.BlockSpec((rows_per, BR_DIM), lambda i: (i, 0)),
            pl.BlockSpec((n_sub, N_HEADS, HEAD_DIM, HEAD_DIM), lambda i: (i, 0, 0, 0)),
        ],
        out_shape=[
            SDS((s, BR_DIM), F32),
            SDS((s, BR_DIM), MM_DTYPE),
            SDS((s // CHUNK, N_HEADS, HEAD_DIM, HEAD_DIM), F32),
        ],
        scratch_shapes=[pltpu.VMEM((N_HEADS, HEAD_DIM, HEAD_DIM), F32)],
        args=[z, z, lb, ng],
        semantics=("arbitrary",),
    )


def _hgrn_bwd(db, z, o, states, lb, ng, name, rider=None):
    s = z.shape[0]
    rows_per = _divisor(s, HG_ROWS, CHUNK)
    n_sub = rows_per // CHUNK
    n_steps = s // rows_per

    def body(db_ref, zqf_ref, zio_ref, o_ref, st_ref, lb_ref, ng_ref, dz_ref, dlb_ref, dng_ref, dstate):
        @pl.when(pl.program_id(0) == 0)
        def _():
            dstate[...] = jnp.zeros_like(dstate)
            dlb_ref[...] = jnp.zeros_like(dlb_ref)
            dng_ref[...] = jnp.zeros_like(dng_ref)

        low = _tri(True)
        ng_v = ng_ref[...]

        def chunk(cc, carry):
            ci = n_sub - 1 - cc
            rows = pl.ds(pl.multiple_of(ci * CHUNK, CHUNK), CHUNK)
            for h in range(N_HEADS):
                cs = slice(h * HEAD_DIM, (h + 1) * HEAD_DIM)
                cs2 = slice(BR_DIM + h * HEAD_DIM, BR_DIM + (h + 1) * HEAD_DIM)
                zq = zqf_ref[rows, cs]
                zf = zqf_ref[rows, cs2]
                vi = zio_ref[rows, cs]
                zog = zio_ref[rows, cs2]
                lbh = lb_ref[:, cs]
                ov = o_ref[rows, cs]
                dbv = db_ref[rows, cs]
                r = lax.rsqrt(jnp.mean(ov * ov, axis=-1, keepdims=True) + EPS)
                on = ov * r
                sgo = _sigmoid(zog)
                gate = zog * sgo
                dng_ref[...] += jnp.sum(dbv * gate * on, axis=0, keepdims=True)
                don = dbv * gate * ng_v
                dzog = dbv * (on * ng_v) * _silu_grad(zog, sgo)
                do = r * (don - on * jnp.mean(don * on, axis=-1, keepdims=True))
                sq = _sigmoid(zq)
                qf = zq * sq
                sg = _sigmoid(zf)
                sgn = _sigmoid(-zf)
                oml = 1.0 - lbh
                forget = lbh + oml * sg
                fm = jnp.maximum(forget, TINY)
                k = oml * sgn
                gc = _scan_rows(jnp.log(fm))
                gm = gc[MID : MID + 1]
                gl = gc[CHUNK - 1 : CHUNK]
                e_g = jnp.exp(gc)
                e_q = jnp.exp(gc - gm)
                e_k = jnp.exp(gm - gc)
                e_kd = jnp.exp(gl - gc)
                e_gl = jnp.exp(gl)
                qt = qf * e_q
                kt = k * e_k
                a = jnp.where(low, _dot_nt(qt, kt), 0.0)
                st = st_ref[ci, h]
                dst = dstate[h]
                dp = jnp.where(low, _dot_nt(do, vi), 0.0)
                dv = _dot_tn(a, do) + _dot_nt(k * e_kd, dst)
                dq = _dot(dp, kt) * e_q + e_g * _dot(do, st)
                dks = e_kd * _dot(vi, dst)
                dk = _dot_tn(dp, qt) * e_k + dks
                dstate[h] = _dot_tn(do, qf * e_g) + dst * e_gl
                dgc = qf * dq - k * dk
                extra = e_gl * jnp.sum(st * dst, axis=0, keepdims=True) + jnp.sum(k * dks, axis=0, keepdims=True)
                dg = _scan_rows(dgc, reverse=True) + extra
                dforget = jnp.where(forget > TINY, dg / fm, 0.0)
                both = dforget - dk
                dlb_ref[:, cs] += jnp.sum(sgn * both, axis=0, keepdims=True)
                dz_ref[rows, cs] = (dq * _silu_grad(zq, sq)).astype(MM_DTYPE)
                dz_ref[rows, cs2] = (oml * sg * sgn * both).astype(MM_DTYPE)
                dz_ref[rows, 2 * BR_DIM + h * HEAD_DIM : 2 * BR_DIM + (h + 1) * HEAD_DIM] = dv.astype(MM_DTYPE)
                dz_ref[rows, 3 * BR_DIM + h * HEAD_DIM : 3 * BR_DIM + (h + 1) * HEAD_DIM] = dzog.astype(MM_DTYPE)
            return carry

        lax.fori_loop(0, n_sub, chunk, 0, unroll=math.gcd(n_sub, HG_UNROLL))

    rev = lambda i: n_steps - 1 - i
    return _hosted_call(
        body,
        rider,
        name=name,
        grid=(n_steps,),
        in_specs=[
            pl.BlockSpec((rows_per, BR_DIM), lambda i: (rev(i), 0)),
            pl.BlockSpec((rows_per, 2 * BR_DIM), lambda i: (rev(i), 1)),
            pl.BlockSpec((rows_per, 2 * BR_DIM), lambda i: (rev(i), 2)),
            pl.BlockSpec((rows_per, BR_DIM), lambda i: (rev(i), 0)),
            pl.BlockSpec((n_sub, N_HEADS, HEAD_DIM, HEAD_DIM), lambda i: (rev(i), 0, 0, 0)),
            _resident((1, BR_DIM), lambda i: (0, 0)),
            _resident((1, HEAD_DIM), lambda i: (0, 0)),
        ],
        out_specs=[
            pl.BlockSpec((rows_per, 4 * BR_DIM), lambda i: (rev(i), 0)),
            pl.BlockSpec((1, BR_DIM), lambda i: (0, 0)),
            pl.BlockSpec((1, HEAD_DIM), lambda i: (0, 0)),
        ],
        out_shape=[SDS((s, 4 * BR_DIM), MM_DTYPE), SDS((1, BR_DIM), F32), SDS((1, HEAD_DIM), F32)],
        scratch_shapes=[pltpu.VMEM((N_HEADS, HEAD_DIM, HEAD_DIM), F32)],
        args=[db, z, z, o, states, lb, ng],
        semantics=("arbitrary",),
    )


def _mix_fwd(x, a, b, z, w_gm, w_hg, w_out, name, rider=None):
    s, d = x.shape
    tm = _divisor(s, 256, 16)
    g0 = 6 * BR_DIM // d

    def body(x_ref, a_ref, b_ref, ga_ref, gb_ref, wgm_ref, whg_ref, wout_ref, out_ref):
        y = _sigmoid(ga_ref[...]) * _dot(a_ref[...], wgm_ref[...]) + _sigmoid(gb_ref[...]) * _dot(b_ref[...], whg_ref[...])
        out_ref[...] = x_ref[...] + _dot(y, wout_ref[...])

    return _hosted_call(
        body,
        rider,
        name=name,
        grid=(s // tm,),
        in_specs=[
            pl.BlockSpec((tm, d), lambda i: (i, 0)),
            pl.BlockSpec((tm, BR_DIM), lambda i: (i, 0)),
            pl.BlockSpec((tm, BR_DIM), lambda i: (i, 0)),
            pl.BlockSpec((tm, d), lambda i: (i, g0)),
            pl.BlockSpec((tm, d), lambda i: (i, g0 + 1)),
            _resident((BR_DIM, d), lambda i: (0, 0)),
            _resident((BR_DIM, d), lambda i: (0, 0)),
            _resident((d, d), lambda i: (0, 0)),
        ],
        out_specs=[pl.BlockSpec((tm, d), lambda i: (i, 0))],
        out_shape=[SDS((s, d), F32)],
        scratch_shapes=[],
        args=[x, a, b, z, z, w_gm, w_hg, w_out],
        semantics=("arbitrary",),
    )


def _mix_bwd(dx, a, b, z, w_gm, w_hg, w_out, name, rider=None):
    s, d = dx.shape
    tm = _divisor(s, 256, 16)
    g0 = 6 * BR_DIM // d

    def body(dx_ref, a_ref, b_ref, ga_ref, gb_ref, wgm_ref, whg_ref, wout_ref, y_ref, dpa_ref, dpb_ref, da_ref, db_ref, dg_ref):
        dy = _dot_nt(dx_ref[...], wout_ref[...])
        pa = _dot(a_ref[...], wgm_ref[...])
        pb = _dot(b_ref[...], whg_ref[...])
        sa = _sigmoid(ga_ref[...])
        sb = _sigmoid(gb_ref[...])
        y_ref[...] = (sa * pa + sb * pb).astype(MM_DTYPE)
        dpa = (dy * sa).astype(MM_DTYPE)
        dpb = (dy * sb).astype(MM_DTYPE)
        dpa_ref[...] = dpa
        dpb_ref[...] = dpb
        da_ref[...] = _dot_nt(dpa, wgm_ref[...])
        db_ref[...] = _dot_nt(dpb, whg_ref[...])
        dg_ref[:, :d] = (dy * pa * sa * (1.0 - sa)).astype(MM_DTYPE)
        dg_ref[:, d:] = (dy * pb * sb * (1.0 - sb)).astype(MM_DTYPE)

    row = lambda w: pl.BlockSpec((tm, w), lambda i: (i, 0))
    return _hosted_call(
        body,
        rider,
        name=name,
        grid=(s // tm,),
        in_specs=[
            row(d),
            row(BR_DIM),
            row(BR_DIM),
            pl.BlockSpec((tm, d), lambda i: (i, g0)),
            pl.BlockSpec((tm, d), lambda i: (i, g0 + 1)),
            _resident((BR_DIM, d), lambda i: (0, 0)),
            _resident((BR_DIM, d), lambda i: (0, 0)),
            _resident((d, d), lambda i: (0, 0)),
        ],
        out_specs=[row(d), row(d), row(d), row(BR_DIM), row(BR_DIM), row(2 * d)],
        out_shape=[
            SDS((s, d), MM_DTYPE),
            SDS((s, d), MM_DTYPE),
            SDS((s, d), MM_DTYPE),
            SDS((s, BR_DIM), F32),
            SDS((s, BR_DIM), F32),
            SDS((s, 2 * d), MM_DTYPE),
        ],
        scratch_shapes=[],
        args=[dx, a, b, z, z, w_gm, w_hg, w_out],
        semantics=("arbitrary",),
    )


HALO = 8
FFN_ROWS = 256
FFN_GROUP = 6


def _shift_down(v, prev, n):
    rolled = pltpu.roll(v, n, 0)
    rid = lax.broadcasted_iota(jnp.int32, v.shape, 0)
    out = rolled
    for r in range(n):
        out = jnp.where(rid == r, prev[HALO - n + r : HALO - n + r + 1], out)
    return out


def _shift_up(v, nxt, n):
    tm = v.shape[0]
    rolled = pltpu.roll(v, tm - n, 0)
    rid = lax.broadcasted_iota(jnp.int32, v.shape, 0)
    out = rolled
    for r in range(n):
        out = jnp.where(rid == tm - n + r, nxt[r : r + 1], out)
    return out


def _conv_cols(f):
    return _divisor(f, 256, 128)


def _ffn_fwd(x, gain, w_up, cw, cb, w_down, layer, name, rider=None):
    s, d = x.shape
    f = w_down.shape[0]
    tm = _divisor(s, FFN_ROWS, 16)
    cwid = _conv_cols(f)

    def body(x_ref, g_ref, wu_ref, cw_ref, cb_ref, wd_ref, out_ref, z_ref, h_ref, conv_ref, halo):
        @pl.when(pl.program_id(0) == 0)
        def _():
            halo[...] = jnp.zeros_like(halo)

        xv = x_ref[...]
        r = lax.rsqrt(jnp.mean(xv * xv, axis=-1, keepdims=True) + EPS)
        h = ((xv * r) * g_ref[...]).astype(MM_DTYPE)
        h_ref[...] = h
        acc = xv
        starts = list(range(0, f, cwid))

        def project(c0):
            return [jnp.dot(h, wu_ref[:, base : base + cwid], preferred_element_type=F32) for base in (c0, f + c0)]

        ahead = project(starts[0])
        for n, c0 in enumerate(starts):
            halves = []
            current = ahead
            if n + 1 < len(starts):
                ahead = project(starts[n + 1])
            for base, zc in zip((c0, f + c0), current):
                cols = slice(base, base + cwid)
                z_ref[:, cols] = zc.astype(MM_DTYPE)
                prev = halo[:, cols]
                conv = cb_ref[:, cols] + cw_ref[0:1, cols] * _shift_down(zc, prev, 2)
                conv = conv + cw_ref[1:2, cols] * _shift_down(zc, prev, 1) + cw_ref[2:3, cols] * zc
                halo[:, cols] = zc[tm - HALO : tm]
                conv_ref[:, cols] = conv.astype(MM_DTYPE)
                halves.append(conv)
            gate, val = halves
            act = gate * _sigmoid(gate) * val
            acc = acc + _dot(act, wd_ref[c0 : c0 + cwid, :])
        out_ref[...] = acc

    row = lambda w: pl.BlockSpec((tm, w), lambda i: (i, 0))
    return _hosted_call(
        body,
        rider,
        name=name,
        grid=(s // tm,),
        in_specs=[
            row(d),
            _resident((1, d), lambda i: (0, 0)),
            _resident((d, 2 * f), lambda i: (0, 0)),
            _resident((None, 3, 2 * f), lambda i: (layer, 0, 0)),
            _resident((1, 2 * f), lambda i: (0, 0)),
            _resident((f, d), lambda i: (0, 0)),
        ],
        out_specs=[row(d), row(2 * f), row(d), row(2 * f)],
        out_shape=[SDS((s, d), F32), SDS((s, 2 * f), MM_DTYPE), SDS((s, d), MM_DTYPE), SDS((s, 2 * f), MM_DTYPE)],
        scratch_shapes=[pltpu.VMEM((HALO, 2 * f), F32)],
        args=[x, gain, w_up, cw, cb, w_down],
        semantics=("arbitrary",),
    )


def _ffn_bwd(dx, z2, conv, cw, w_down, w_up, x, gain, layer, name):
    s, d = dx.shape
    f = z2.shape[1] // 2
    tm = _divisor(s, 256, 16)
    cwid = _conv_cols(f)
    n_steps = s // tm

    def body(dx_ref, z_ref, conv_ref, cw_ref, wd_ref, wu_ref, x_ref, g_ref, dz_ref, act_ref, dcw_ref, dx1_ref, dg_ref, nxt):
        @pl.when(pl.program_id(0) == 0)
        def _():
            nxt[...] = jnp.zeros_like(nxt)
            dcw_ref[...] = jnp.zeros_like(dcw_ref)
            dg_ref[...] = jnp.zeros_like(dg_ref)

        dxf = dx_ref[...]
        dxv = dxf.astype(MM_DTYPE)
        starts = list(range(0, f, cwid))
        ahead = _dot_nt(dxv, wd_ref[0:cwid, :])
        dh = jnp.zeros((tm, d), F32)

        def project_back(lo, hi):
            return sum(_dot_nt(dz_ref[:, base + lo : base + hi], wu_ref[:, base + lo : base + hi]) for base in (0, f))

        done = 0
        for n, c0 in enumerate(starts):
            dact = ahead
            if n + 1 < len(starts):
                ahead = _dot_nt(dxv, wd_ref[starts[n + 1] : starts[n + 1] + cwid, :])
            if n - done >= FFN_GROUP:
                dh = dh + project_back(starts[done], c0)
                done = n
            gate = conv_ref[:, c0 : c0 + cwid].astype(F32)
            val = conv_ref[:, f + c0 : f + c0 + cwid].astype(F32)
            sg = _sigmoid(gate)
            silu = gate * sg
            act_ref[:, c0 : c0 + cwid] = (silu * val).astype(MM_DTYPE)
            dconvs = (dact * val * _silu_grad(gate, sg), dact * silu)
            for base, dconv in zip((c0, f + c0), dconvs):
                cols = slice(base, base + cwid)
                zc = z_ref[:, cols].astype(F32)
                nx = nxt[:, cols]
                up1 = _shift_up(dconv, nx, 1)
                up2 = _shift_up(dconv, nx, 2)
                dcw_ref[0:1, cols] += jnp.sum(up2 * zc, axis=0, keepdims=True)
                dcw_ref[1:2, cols] += jnp.sum(up1 * zc, axis=0, keepdims=True)
                dcw_ref[2:3, cols] += jnp.sum(dconv * zc, axis=0, keepdims=True)
                dcw_ref[3:4, cols] += jnp.sum(dconv, axis=0, keepdims=True)
                dz = (cw_ref[2:3, cols] * dconv + cw_ref[1:2, cols] * up1 + cw_ref[0:1, cols] * up2).astype(MM_DTYPE)
                dz_ref[:, cols] = dz
                nxt[:, cols] = dconv[0:HALO]
        dh = dh + project_back(starts[done], f)
        xv = x_ref[...]
        r = lax.rsqrt(jnp.mean(xv * xv, axis=-1, keepdims=True) + EPS)
        xh = xv * r
        dg_ref[...] += jnp.sum(dh * xh, axis=0, keepdims=True)
        dxh = dh * g_ref[...]
        dx1_ref[...] = dxf + r * (dxh - xh * jnp.mean(dxh * xh, axis=-1, keepdims=True))

    rev = lambda i: n_steps - 1 - i
    row = lambda w: pl.BlockSpec((tm, w), lambda i: (rev(i), 0))
    return pl.pallas_call(
        body,
        name=name,
        grid=(n_steps,),
        in_specs=[
            row(d),
            row(2 * f),
            row(2 * f),
            _resident((None, 3, 2 * f), lambda i: (layer, 0, 0)),
            _resident((f, d), lambda i: (0, 0)),
            _resident((d, 2 * f), lambda i: (0, 0)),
            row(d),
            _resident((1, d), lambda i: (0, 0)),
        ],
        out_specs=[
            row(2 * f),
            row(f),
            pl.BlockSpec((HALO, 2 * f), lambda i: (0, 0)),
            row(d),
            pl.BlockSpec((1, d), lambda i: (0, 0)),
        ],
        out_shape=[SDS((s, 2 * f), MM_DTYPE), SDS((s, f), MM_DTYPE), SDS((HALO, 2 * f), F32), SDS((s, d), F32), SDS((1, d), F32)],
        scratch_shapes=[pltpu.VMEM((HALO, 2 * f), F32)],
        compiler_params=_params("arbitrary"),
    )(dx, z2, conv, cw, w_down, w_up, x, gain)


def _loss_head(x, gain, target):
    s, d = x.shape
    tm = _divisor(s, 256, 8)

    def body(x_ref, g_ref, t_ref, loss_ref, dx_ref, dg_ref):
        @pl.when(pl.program_id(0) == 0)
        def _():
            loss_ref[...] = jnp.zeros_like(loss_ref)
            dg_ref[...] = jnp.zeros_like(dg_ref)

        xv = x_ref[...]
        gv = g_ref[...]
        r = lax.rsqrt(jnp.mean(xv * xv, axis=-1, keepdims=True) + EPS)
        xh = xv * r
        err = xh * gv - t_ref[...]
        loss_ref[...] += 0.5 * jnp.sum(jnp.mean(err * err, axis=-1, keepdims=True))
        dout = err * (1.0 / d)
        dg_ref[...] += jnp.sum(dout * xh, axis=0, keepdims=True)
        dxh = dout * gv
        dx_ref[...] = r * (dxh - xh * jnp.mean(dxh * xh, axis=-1, keepdims=True))

    return pl.pallas_call(
        body,
        name="loss_head",
        grid=(s // tm,),
        in_specs=[
            pl.BlockSpec((tm, d), lambda i: (i, 0)),
            _resident((1, d), lambda i: (0, 0)),
            pl.BlockSpec((tm, d), lambda i: (i, 0)),
        ],
        out_specs=[
            pl.BlockSpec((8, 128), lambda i: (0, 0)),
            pl.BlockSpec((tm, d), lambda i: (i, 0)),
            pl.BlockSpec((1, d), lambda i: (0, 0)),
        ],
        out_shape=[SDS((8, 128), F32), SDS((s, d), F32), SDS((1, d), F32)],
        compiler_params=_params("arbitrary"),
    )(x, gain, target)


def _adamw(w, g, m, v, name, rider=None):
    shape = w.shape
    cols = shape[-1]
    rows = max(1, math.prod(shape[:-1]))
    tr = _divisor(rows, max(8, TILE_BYTES // (4 * cols)), 8)
    flat = [t.reshape(rows, cols) for t in (w, g, m, v)]

    def body(w_ref, g_ref, m_ref, v_ref, d_ref, nm_ref, nv_ref):
        gv = g_ref[...]
        m2 = ADAM_B1 * m_ref[...] + (1.0 - ADAM_B1) * gv
        v2 = ADAM_B2 * v_ref[...] + (1.0 - ADAM_B2) * (gv * gv)
        m_hat = m2 / (1.0 - ADAM_B1**ADAM_STEP)
        v_hat = v2 / (1.0 - ADAM_B2**ADAM_STEP)
        d_ref[...] = -ADAM_LR * (m_hat / (jnp.sqrt(v_hat) + ADAM_EPS) + ADAM_WD * w_ref[...])
        nm_ref[...] = m2
        nv_ref[...] = v2

    spec = pl.BlockSpec((tr, cols), lambda i: (i, 0))
    outs, rider_outs = _hosted_call(
        body,
        rider,
        name=name,
        grid=(rows // tr,),
        in_specs=[spec] * 4,
        out_specs=[spec] * 3,
        out_shape=[SDS((rows, cols), F32)] * 3,
        scratch_shapes=[],
        args=flat,
        semantics=("arbitrary",),
    )
    return tuple(t.reshape(shape) for t in outs), rider_outs


def _position():
    return lax.axis_index("x"), lax.axis_index("y"), lax.axis_index("c")


def _other_chips(x, y):
    return [(1 - x, y), (x, 1 - y), (1 - x, 1 - y)]


def _remote(src, dst, send_sem, recv_sem, device):
    return pltpu.make_async_remote_copy(
        src_ref=src, dst_ref=dst, send_sem=send_sem, recv_sem=recv_sem, device_id=device, device_id_type=MESH
    )


def _sub(ref, lead, shape, axis, chip=None, half=None):
    cut = [pl.ds(0, shape[0]), pl.ds(0, shape[1])]
    if chip is not None:
        size = shape[axis] // N_CHIPS
        cut[axis] = pl.ds(chip * size, size)
    if half is not None:
        size = shape[1 - axis] // 2
        cut[1 - axis] = pl.ds(half * size, size)
    return ref.at[(*lead, *cut)]


def _scaled(shape, axis, num, den=1):
    return tuple(d * num // den if i == axis else d for i, d in enumerate(shape))


def _gather_rider(shards, axes, layer, conv_w=None):
    n = len(shards)
    shard2d = [t.shape[1:] for t in shards]
    full2d = [_scaled(s2, ax, N_CHIPS) for s2, ax in zip(shard2d, axes)]
    out_shapes = [SDS(f2, t.dtype) for f2, t in zip(full2d, shards)]
    inputs = list(shards)
    own = 6
    scratch = [pltpu.SemaphoreType.DMA((n, 7)), pltpu.SemaphoreType.DMA((n, 7))]
    if conv_w is not None:
        cshape = conv_w.shape
        inputs.append(conv_w)
        out_shapes.append(SDS(_scaled(cshape, 2, N_CHIPS), conv_w.dtype))
        scratch += [pltpu.SemaphoreType.DMA((4,)), pltpu.SemaphoreType.DMA((4,))]

    def conv_slab(ref, chip):
        return ref.at[:, :, pl.ds((2 * chip[0] + chip[1]) * cshape[2], cshape[2])]

    def own_copy(ins, outs, send_sems, recv_sems, k):
        x, y, c = _position()
        slab = _sub(outs[k], (), full2d[k], axes[k], chip=2 * x + y)
        return _remote(ins[k].at[layer], slab, send_sems.at[k, own], recv_sems.at[k, own], (x, y, 1 - c))

    def start(ins, outs, scr):
        send_sems, recv_sems = scr[:2]
        x, y, c = _position()
        me = 2 * x + y
        for k in range(n):
            own_copy(ins, outs, send_sems, recv_sems, k).start()
            for j, chip in enumerate(_other_chips(x, y)):
                _remote(
                    _sub(ins[k], (layer,), shard2d[k], axes[k], half=c),
                    _sub(outs[k], (), full2d[k], axes[k], chip=me, half=c),
                    send_sems.at[k, j], recv_sems.at[k, j], (*chip, c),
                ).start()
        if conv_w is not None:
            csend, crecv = scr[2:]
            _remote(ins[n], conv_slab(outs[n], (x, y)), csend.at[3], crecv.at[3], (x, y, 1 - c)).start()
            for j, chip in enumerate(_other_chips(x, y)):
                _remote(ins[n], conv_slab(outs[n], (x, y)), csend.at[j], crecv.at[j], (*chip, c)).start()

    def finish(ins, outs, scr):
        send_sems, recv_sems = scr[:2]
        x, y, c = _position()
        me = 2 * x + y
        sibling = (x, y, 1 - c)
        chips = _other_chips(x, y)
        forwards = []
        for k in range(n):
            src = _sub(ins[k], (layer,), shard2d[k], axes[k], half=c)
            for j, chip in enumerate(chips):
                landed = _sub(outs[k], (), full2d[k], axes[k], chip=2 * chip[0] + chip[1], half=c)
                _remote(src, landed, send_sems.at[k, j], recv_sems.at[k, j], (*chip, c)).wait_recv()
                fwd = _remote(landed, landed, send_sems.at[k, 3 + j], recv_sems.at[k, 3 + j], sibling)
                fwd.start()
                forwards.append(fwd)
        for k in range(n):
            src = _sub(ins[k], (layer,), shard2d[k], axes[k], half=c)
            for j, chip in enumerate(chips):
                theirs = _sub(outs[k], (), full2d[k], axes[k], chip=2 * chip[0] + chip[1], half=1 - c)
                _remote(src, theirs, send_sems.at[k, 3 + j], recv_sems.at[k, 3 + j], sibling).wait_recv()
        for fwd in forwards:
            fwd.wait_send()
        for k in range(n):
            src = _sub(ins[k], (layer,), shard2d[k], axes[k], half=c)
            mine = _sub(outs[k], (), full2d[k], axes[k], chip=me, half=c)
            for j, chip in enumerate(chips):
                _remote(src, mine, send_sems.at[k, j], recv_sems.at[k, j], (*chip, c)).wait_send()
            own_copy(ins, outs, send_sems, recv_sems, k).wait()
        if conv_w is not None:
            csend, crecv = scr[2:]
            for j, chip in enumerate(chips):
                cp = _remote(ins[n], conv_slab(outs[n], chip), csend.at[j], crecv.at[j], (*chip, c))
                cp.wait_recv()
                cp.wait_send()
            _remote(ins[n], conv_slab(outs[n], (x, y)), csend.at[3], crecv.at[3], sibling).wait()

    return _Rider(inputs, out_shapes, scratch, start, finish)


def _pair_rider(stacks, axes, layer):
    n = len(stacks)
    shapes = [t.shape[1:] for t in stacks]
    out_shapes = [SDS(_scaled(s2, 1 - ax, 1, 2), F32) for s2, ax in zip(shapes, axes)]
    scratch = [pltpu.SemaphoreType.DMA((n,)), pltpu.SemaphoreType.DMA((n,))]

    def copies(ins, outs, scr):
        x, y, c = _position()
        return [
            _remote(_sub(ins[k], (layer,), shapes[k], axes[k], half=1 - c), outs[k], scr[0].at[k], scr[1].at[k], (x, y, 1 - c))
            for k in range(n)
        ]

    def start(ins, outs, scr):
        for cp in copies(ins, outs, scr):
            cp.start()

    def finish(ins, outs, scr):
        for cp in copies(ins, outs, scr):
            cp.wait()

    return _Rider(stacks, out_shapes, scratch, start, finish)


def _chip_rider(pairs, axes):
    n = len(pairs)
    shapes = [t.shape for t in pairs]
    out_shapes = [SDS((3,) + _scaled(s2, ax, 1, N_CHIPS), t.dtype) for s2, ax, t in zip(shapes, axes, pairs)]
    scratch = [pltpu.SemaphoreType.DMA((n, 3)), pltpu.SemaphoreType.DMA((n, 3))]

    def copies(ins, outs, scr):
        x, y, c = _position()
        return [
            _remote(
                _sub(ins[k], (), shapes[k], axes[k], chip=2 * chip[0] + chip[1]), outs[k].at[j],
                scr[0].at[k, j], scr[1].at[k, j], (*chip, c),
            )
            for k in range(n)
            for j, chip in enumerate(_other_chips(x, y))
        ]

    def start(ins, outs, scr):
        for cp in copies(ins, outs, scr):
            cp.start()

    def finish(ins, outs, scr):
        for cp in copies(ins, outs, scr):
            cp.wait()

    return _Rider(pairs, out_shapes, scratch, start, finish)


def _complete_rider(stacks, axes, layers):
    n = len(stacks)
    shapes = [t.shape[1:] for t in stacks]
    scratch = [pltpu.SemaphoreType.DMA((n,)), pltpu.SemaphoreType.DMA((n,))]

    def start(ins, outs, scr):
        x, y, c = _position()
        for k in range(n):
            mine = _sub(outs[k], (layers[k],), shapes[k], axes[k], half=c)
            _remote(mine, mine, scr[0].at[k], scr[1].at[k], (x, y, 1 - c)).start()

    def finish(ins, outs, scr):
        x, y, c = _position()
        for k in range(n):
            mine = _sub(outs[k], (layers[k],), shapes[k], axes[k], half=c)
            theirs = _sub(outs[k], (layers[k],), shapes[k], axes[k], half=1 - c)
            _remote(mine, theirs, scr[0].at[k], scr[1].at[k], (x, y, 1 - c)).wait_recv()
            _remote(mine, mine, scr[0].at[k], scr[1].at[k], (x, y, 1 - c)).wait_send()

    return _Rider(stacks, [SDS(t.shape, t.dtype) for t in stacks], scratch, start, finish, {k: k for k in range(n)})


def _small_rider(buf):
    rows, cols = buf.shape
    flips = [(fx, fy, fc) for fx in (0, 1) for fy in (0, 1) for fc in (0, 1)][1:]

    def peers():
        x, y, c = _position()
        return [(x + f[0] - 2 * x * f[0], y + f[1] - 2 * y * f[1], c + f[2] - 2 * c * f[2]) for f in flips]

    def start(ins, outs, scr):
        x, y, c = _position()
        mine = outs[0].at[4 * x + 2 * y + c]
        pltpu.make_async_copy(ins[0], mine, scr[2]).start()
        for j, peer in enumerate(peers()):
            _remote(ins[0], mine, scr[0].at[j], scr[1].at[j], peer).start()

    def finish(ins, outs, scr):
        x, y, c = _position()
        mine = outs[0].at[4 * x + 2 * y + c]
        for j, peer in enumerate(peers()):
            _remote(ins[0], outs[0].at[4 * peer[0] + 2 * peer[1] + peer[2]], scr[0].at[j], scr[1].at[j], peer).wait_recv()
        for j, peer in enumerate(peers()):
            _remote(ins[0], mine, scr[0].at[j], scr[1].at[j], peer).wait_send()
        pltpu.make_async_copy(ins[0], mine, scr[2]).wait()

    scratch = [pltpu.SemaphoreType.DMA((7,)), pltpu.SemaphoreType.DMA((7,)), pltpu.SemaphoreType.DMA(())]
    return _Rider([buf], [SDS((8, rows, cols), buf.dtype)], scratch, start, finish)


def _my_core():
    return lax.axis_index("c")


def _my_chip():
    return 2 * lax.axis_index("x") + lax.axis_index("y")


def _pair_sum(stack, layer, recv, axis, name):
    hk, hn = recv.shape
    tk = _divisor(hk, max(16, TILE_BYTES // (4 * hn)), 16)
    per = hk // tk

    def body(g_ref, r_ref, out_ref):
        out_ref[...] = (g_ref[...] + r_ref[...]).astype(out_ref.dtype)

    if axis == 1:
        mine = pl.BlockSpec((None, tk, hn), lambda i: (layer, _my_core() * per + i, 0))
    else:
        mine = pl.BlockSpec((None, tk, hn), lambda i: (layer, i, _my_core()))
    return pl.pallas_call(
        body,
        name=name,
        grid=(per,),
        in_specs=[mine, pl.BlockSpec((tk, hn), lambda i: (i, 0))],
        out_specs=pl.BlockSpec((tk, hn), lambda i: (i, 0)),
        out_shape=SDS((hk, hn), BF16),
        compiler_params=_params("parallel"),
    )(stack, recv)


def _chip_sum(pair, others, axis, stack, layer, name):
    _, sk, sn = others.shape
    tk = _divisor(sk, max(16, TILE_BYTES // (4 * sn)), 16)
    per = sk // tk

    def body(p_ref, o0_ref, o1_ref, o2_ref, *rest):
        out_ref = rest[-1]
        acc = p_ref[...].astype(F32) + o0_ref[...].astype(F32)
        out_ref[...] = (acc + o1_ref[...].astype(F32)) + o2_ref[...].astype(F32)

    if axis == 1:
        own = pl.BlockSpec((tk, sn), lambda i: (i, _my_chip()))
        out = pl.BlockSpec((None, tk, sn), lambda i: (layer, _my_core() * per + i, 0))
        shard = (2 * sk, sn)
    else:
        own = pl.BlockSpec((tk, sn), lambda i: (_my_chip() * per + i, 0))
        out = pl.BlockSpec((None, tk, sn), lambda i: (layer, i, _my_core()))
        shard = (sk, 2 * sn)
    other = lambda j: pl.BlockSpec((None, tk, sn), lambda i: (j, i, 0))
    in_specs = [own, other(0), other(1), other(2)]
    args = [pair, others, others, others]
    aliases = {}
    if stack is not None:
        in_specs.append(ANY)
        args.append(stack)
        aliases = {4: 0}
    return pl.pallas_call(
        body,
        name=name,
        grid=(per,),
        in_specs=in_specs,
        out_specs=out,
        out_shape=SDS((DEPTH,) + shard, F32),
        input_output_aliases=aliases,
        compiler_params=_params("parallel"),
    )(*args)


def _sum_devices(gathered, name):
    _, rows, cols = gathered.shape

    def body(g_ref, out_ref):
        acc = g_ref[0]
        for dev in range(1, 8):
            acc = acc + g_ref[dev]
        out_ref[...] = acc

    return pl.pallas_call(body, name=name, out_shape=SDS((rows, cols), F32))(gathered)


PACK_TILE = 8 * 128


def _pack(arrays):
    parts = []
    for t in arrays:
        flat = t.reshape(-1)
        pad = (-flat.shape[0]) % PACK_TILE
        if pad:
            flat = jnp.concatenate([flat, jnp.zeros((pad,), flat.dtype)])
        parts.append(flat.reshape(-1, 128))
    return jnp.concatenate(parts, axis=0)


def _unpack(buf, shapes):
    out, row = [], 0
    for shp in shapes:
        size = math.prod(shp)
        rows = -(-size // PACK_TILE) * 8
        out.append(buf[row : row + rows].reshape(-1)[:size].reshape(shp))
        row += rows
    return out


BIG = ("w_in", "w_br_gm", "w_br_hg", "w_out", "w_up", "w_down")
BIG_AXIS = (1, 1, 1, 0, 1, 0)
LAYER_SMALL = ("mix_norm", "gm_ln_g", "gm_ln_b", "gm_ws", "gm_bs", "lb", "hg_norm_g", "ffn_norm", "conv_w", "conv_b")
WEIGHTS = ("mix_norm", "w_in", "gm_ln_g", "gm_ln_b", "gm_ws", "gm_bs", "hg_lb_logits", "hg_norm_g", "w_br_gm", "w_br_hg", "w_out",
           "ffn_norm", "w_up", "conv_w", "conv_b", "w_down", "final_norm")


def kernel(x, mix_norm, w_in, gm_ln_g, gm_ln_b, gm_ws, gm_bs, hg_lb_logits, hg_norm_g, w_br_gm, w_br_hg, w_out, ffn_norm, w_up, conv_w, conv_b, w_down, final_norm, loss_target, m_mix_norm, m_w_in, m_gm_ln_g, m_gm_ln_b, m_gm_ws, m_gm_bs, m_hg_lb_logits, m_hg_norm_g, m_w_br_gm, m_w_br_hg, m_w_out, m_ffn_norm, m_w_up, m_conv_w, m_conv_b, m_w_down, m_final_norm, v_mix_norm, v_w_in, v_gm_ln_g, v_gm_ln_b, v_gm_ws, v_gm_bs, v_hg_lb_logits, v_hg_norm_g, v_w_br_gm, v_w_br_hg, v_w_out, v_ffn_norm, v_w_up, v_conv_w, v_conv_b, v_w_down, v_final_norm):
    w = dict(mix_norm=mix_norm, w_in=w_in, gm_ln_g=gm_ln_g, gm_ln_b=gm_ln_b, gm_ws=gm_ws, gm_bs=gm_bs, hg_lb_logits=hg_lb_logits,
             hg_norm_g=hg_norm_g, w_br_gm=w_br_gm, w_br_hg=w_br_hg, w_out=w_out, ffn_norm=ffn_norm, w_up=w_up, conv_w=conv_w,
             conv_b=conv_b, w_down=w_down, final_norm=final_norm)
    m = dict(mix_norm=m_mix_norm, w_in=m_w_in, gm_ln_g=m_gm_ln_g, gm_ln_b=m_gm_ln_b, gm_ws=m_gm_ws, gm_bs=m_gm_bs,
             hg_lb_logits=m_hg_lb_logits, hg_norm_g=m_hg_norm_g, w_br_gm=m_w_br_gm, w_br_hg=m_w_br_hg, w_out=m_w_out,
             ffn_norm=m_ffn_norm, w_up=m_w_up, conv_w=m_conv_w, conv_b=m_conv_b, w_down=m_w_down, final_norm=m_final_norm)
    v = dict(mix_norm=v_mix_norm, w_in=v_w_in, gm_ln_g=v_gm_ln_g, gm_ln_b=v_gm_ln_b, gm_ws=v_gm_ws, gm_bs=v_gm_bs,
             hg_lb_logits=v_hg_lb_logits, hg_norm_g=v_hg_norm_g, w_br_gm=v_w_br_gm, w_br_hg=v_w_br_hg, w_out=v_w_out,
             ffn_norm=v_ffn_norm, w_up=v_w_up, conv_w=v_conv_w, conv_b=v_conv_b, w_down=v_w_down, final_norm=v_final_norm)

    axes = list(BIG_AXIS)
    d = x.shape[2]

    shards = [w[k].astype(MM_DTYPE) for k in BIG]
    w_in_0, conv_full = _run_rider(_gather_rider(shards[:1], axes[:1], 0, conv_w=conv_w), "gather_weights_0")
    full = [dict(w_in=w_in_0)]
    lb = _lower_bounds(hg_lb_logits)
    xs = x[0]
    saved = []
    mixer = 4
    for l in range(DEPTH):
        fw = full[l]
        more = l + 1 < DEPTH
        bst = gm_bs[l].T
        (z, h, a), got = _in_proj(
            xs, mix_norm[l : l + 1], fw["w_in"], gm_ln_g[l : l + 1], gm_ln_b[l : l + 1], gm_ws[l], bst, f"in_proj_{l}",
            _gather_rider(shards[mixer:], axes[mixer:], l),
        )
        fw.update(zip(BIG[mixer:], got))
        rider = _gather_rider(shards[:1], axes[:1], l + 1) if more else None
        if l == 0:
            rider = _join(rider, _gather_rider(shards[1:mixer], axes[1:mixer], 0))
        (o, b, states), got = _hgrn_fwd(z, lb[l : l + 1], hg_norm_g[l : l + 1], f"hgrn_fwd_{l}", rider)
        if more:
            full.append(dict(w_in=got[0]))
        if l == 0:
            fw.update(zip(BIG[1:mixer], got[1:]))
        rider = _gather_rider(shards[1:mixer], axes[1:mixer], l + 1) if more else None
        (x1,), got = _mix_fwd(xs, a, b, z, fw["w_br_gm"], fw["w_br_hg"], fw["w_out"], f"mix_fwd_{l}", rider)
        if more:
            full[l + 1].update(zip(BIG[1:mixer], got))
        (x2, z2, h2, conv), _ = _ffn_fwd(
            x1, ffn_norm[l : l + 1], fw["w_up"], conv_full, conv_b[l : l + 1], fw["w_down"], l, f"ffn_fwd_{l}"
        )
        saved.append((xs, h, z, a, b, o, states, x1, h2, z2, conv, bst))
        xs = x2

    loss_tile, dx, d_final = _loss_head(xs, final_norm.reshape(1, d), loss_target[0])
    loss = lax.psum(loss_tile[0, 0], ("x", "y", "c"))

    big = {k: None for k in BIG}
    grads = [None] * len(BIG)
    per_layer = [None] * DEPTH
    mix_ids, ffn_ids = list(range(mixer)), list(range(mixer, len(BIG)))
    mix_axes, ffn_axes = axes[:mixer], axes[mixer:]
    mix_pairs = None
    mix_summed = None

    def pair_sums(ids, received, layer):
        return [_pair_sum(big[BIG[k]], layer, r, axes[k], f"pair_sum_{BIG[k]}_{layer}") for k, r in zip(ids, received)]

    def chip_sums(ids, pairs, others, layer):
        for k, p, ot in zip(ids, pairs, others):
            grads[k] = _chip_sum(p, ot, axes[k], grads[k], layer, f"chip_sum_{BIG[k]}_{layer}")

    for l in reversed(range(DEPTH)):
        x0, h, z, a, b, o, states, x1, h2, z2, conv, bst = saved[l]
        fw = full[l]
        dz2, act, dconv, dx1, d_ffn = _ffn_bwd(
            dx, z2, conv, conv_full, fw["w_down"], fw["w_up"], x1, ffn_norm[l : l + 1], l, f"ffn_bwd_{l}"
        )
        big["w_down"] = _matmul_tn(act, dx, big["w_down"], l, f"dw_down_{l}")
        big["w_up"] = _matmul_tn(h2, dz2, big["w_up"], l, f"dw_up_{l}")
        rider = _pair_rider([big[BIG[k]] for k in ffn_ids], ffn_axes, l)
        if mix_pairs is not None:
            rider = _join(rider, _chip_rider(mix_pairs, mix_axes))
        (y, dpa, dpb, da, db, dgates), got = _mix_bwd(
            dx1, a, b, z, fw["w_br_gm"], fw["w_br_hg"], fw["w_out"], f"mix_bwd_{l}", rider
        )
        if mix_pairs is not None:
            chip_sums(mix_ids, mix_pairs, got[len(ffn_ids) :], l + 1)
            mix_summed = l + 1
        ffn_pairs = pair_sums(ffn_ids, got[: len(ffn_ids)], l)
        big["w_out"] = _matmul_tn(y, dx1, big["w_out"], l, f"dw_out_{l}")
        big["w_br_gm"] = _matmul_tn(a, dpa, big["w_br_gm"], l, f"dw_br_gm_{l}")
        big["w_br_hg"] = _matmul_tn(b, dpb, big["w_br_hg"], l, f"dw_br_hg_{l}")
        (dz_hg, d_lb, d_ng), others = _hgrn_bwd(
            db, z, o, states, lb[l : l + 1], hg_norm_g[l : l + 1], f"hgrn_bwd_{l}", _chip_rider(ffn_pairs, ffn_axes)
        )
        chip_sums(ffn_ids, ffn_pairs, others, l)
        dz_gm, d_lng, d_lnb, d_ws, d_bst = _gmlp_bwd(da, z, gm_ln_g[l : l + 1], gm_ln_b[l : l + 1], gm_ws[l], bst, f"gmlp_bwd_{l}")
        n_in = fw["w_in"].shape[1]
        big["w_in"] = _matmul_tn(h, dz_gm, big["w_in"], l, f"dw_in_gm_{l}", n_total=n_in, col_off=0)
        big["w_in"] = _matmul_tn(h, dz_hg, big["w_in"], l, f"dw_in_hg_{l}", n_total=n_in, col_off=2 * BR_DIM)
        big["w_in"] = _matmul_tn(h, dgates, big["w_in"], l, f"dw_in_gate_{l}", n_total=n_in, col_off=6 * BR_DIM)
        done_ids = ffn_ids + (mix_ids if mix_summed is not None else [])
        done_layers = [l] * len(ffn_ids) + ([mix_summed] * len(mix_ids) if mix_summed is not None else [])
        rider = _join(
            _pair_rider([big[BIG[k]] for k in mix_ids], mix_axes, l),
            _complete_rider([grads[k] for k in done_ids], [axes[k] for k in done_ids], done_layers),
        )
        if l == 0:
            upper = [_pack([per_layer[j][k] for k in LAYER_SMALL]) for j in range(1, DEPTH)] + [_pack([d_final[0]])]
            rider = _join(rider, _small_rider(jnp.concatenate(upper, axis=0)))
        (dx, d_mix), got = _in_proj_bwd([dz_gm, dz_hg, dgates], fw["w_in"], x0, mix_norm[l : l + 1], dx1, f"in_proj_bwd_{l}", rider)
        for k, g in zip(done_ids, got[len(mix_ids) : len(mix_ids) + len(done_ids)]):
            grads[k] = g
        gathered_upper = got[len(mix_ids) + len(done_ids) :]
        mix_pairs = pair_sums(mix_ids, got[: len(mix_ids)], l)
        per_layer[l] = dict(
            mix_norm=d_mix[0], gm_ln_g=d_lng[0], gm_ln_b=d_lnb[0], gm_ws=d_ws, gm_bs=d_bst.T, lb=d_lb[0], hg_norm_g=d_ng[0],
            ffn_norm=d_ffn[0], conv_w=dconv[0:3], conv_b=dconv[3],
        )

    delta, new_m, new_v = {}, {}, {}
    grad = {BIG[k]: grads[k] for k in ffn_ids}
    rider = _join(_chip_rider(mix_pairs, mix_axes), _small_rider(_pack([per_layer[0][k] for k in LAYER_SMALL])))
    (delta["w_up"], new_m["w_up"], new_v["w_up"]), got = _adamw(w["w_up"], grad["w_up"], m["w_up"], v["w_up"], "adamw_w_up", rider)
    chip_sums(mix_ids, mix_pairs, got[: len(mix_ids)], 0)
    rider = _complete_rider([grads[k] for k in mix_ids], mix_axes, [0] * len(mix_ids))
    (delta["w_down"], new_m["w_down"], new_v["w_down"]), done = _adamw(
        w["w_down"], grad["w_down"], m["w_down"], v["w_down"], "adamw_w_down", rider
    )
    grad.update({BIG[k]: g for k, g in zip(mix_ids, done)})

    layer_shapes = [per_layer[0][k].shape for k in LAYER_SMALL]
    summed_upper = _unpack(_sum_devices(gathered_upper[0], "sum_small_upper"), layer_shapes * (DEPTH - 1) + [d_final[0].shape])
    summed_0 = _unpack(_sum_devices(got[len(mix_ids)], "sum_small_0"), layer_shapes)
    by_layer = [summed_0] + [summed_upper[j * len(LAYER_SMALL) : (j + 1) * len(LAYER_SMALL)] for j in range(DEPTH - 1)]
    small = {k: jnp.stack([by_layer[j][i] for j in range(DEPTH)]) for i, k in enumerate(LAYER_SMALL)}
    for k in LAYER_SMALL:
        if k != "lb":
            grad[k] = small[k]
    grad["hg_lb_logits"] = _lower_bounds_bwd(hg_lb_logits, small["lb"])
    grad["final_norm"] = summed_upper[-1]
    grad["conv_w"] = lax.dynamic_slice_in_dim(grad["conv_w"], _my_chip() * conv_w.shape[2], conv_w.shape[2], axis=2)

    for k in WEIGHTS:
        if k not in delta:
            (delta[k], new_m[k], new_v[k]), _ = _adamw(w[k], grad[k], m[k], v[k], f"adamw_{k}")

    return (loss, dx[None], *[grad[k] for k in WEIGHTS], *[delta[k] for k in WEIGHTS], *[new_m[k] for k in WEIGHTS],
            *[new_v[k] for k in WEIGHTS])
```

```python
import math

import jax
import jax.numpy as jnp
from jax import lax
from jax.experimental import pallas as pl
from jax.experimental.pallas import tpu as pltpu

F32 = jnp.float32
BF16 = jnp.bfloat16
MM_DTYPE = BF16

N_HEADS = 4
HEAD_DIM = 128
BR_DIM = N_HEADS * HEAD_DIM
CHUNK = 64
GM_BLOCK = 128
DEPTH = 4
N_CHIPS = 4
EPS = 1e-6
TINY = 1e-30
LB_MAX = 0.999
ADAM_LR = 0.001
ADAM_B1 = 0.9
ADAM_B2 = 0.999
ADAM_EPS = 1e-08
ADAM_WD = 0.01
ADAM_STEP = 10
INV_SQRT2 = 1.0 / math.sqrt(2.0)
INV_SQRT_2PI = 1.0 / math.sqrt(2.0 * math.pi)

VMEM_LIMIT_BYTES = 56 * 1024 * 1024
TILE_BYTES = 2 * 1024 * 1024
ACC_BYTES = 6 * 1024 * 1024
MESH = pl.DeviceIdType.MESH
SDS = jax.ShapeDtypeStruct
ANY = pl.BlockSpec(memory_space=pl.ANY)


def _params(*sem):
    return pltpu.CompilerParams(dimension_semantics=sem or None, vmem_limit_bytes=VMEM_LIMIT_BYTES)


def _divisor(n, limit, mult):
    best = None
    for d in range(mult, min(n, limit) + 1, mult):
        if n % d == 0:
            best = d
    return best if best is not None else n


def _dot(a, b):
    return jnp.dot(a.astype(MM_DTYPE), b.astype(MM_DTYPE), preferred_element_type=F32)


def _dot_nt(a, b):
    return lax.dot_general(a.astype(MM_DTYPE), b.astype(MM_DTYPE), (((1,), (1,)), ((), ())), preferred_element_type=F32)


def _dot_tn(a, b):
    return lax.dot_general(a.astype(MM_DTYPE), b.astype(MM_DTYPE), (((0,), (0,)), ((), ())), preferred_element_type=F32)


def _sigmoid(x):
    return 1.0 / (1.0 + jnp.exp(-x))


def _gelu(x):
    return 0.5 * x * (1.0 + lax.erf(x * INV_SQRT2))


def _gelu_grad(x):
    return 0.5 * (1.0 + lax.erf(x * INV_SQRT2)) + x * jnp.exp(-0.5 * x * x) * INV_SQRT_2PI


def _silu_grad(x, s):
    return s * (1.0 + x * (1.0 - s))


def _resident(shape, index_map):
    return pl.BlockSpec(shape, index_map, pipeline_mode=pl.Buffered(1))


class _Rider:
    def __init__(self, inputs, out_shapes, scratch, start, finish, aliases=None):
        self.inputs = list(inputs)
        self.out_shapes = list(out_shapes)
        self.scratch = list(scratch)
        self.start = start
        self.finish = finish
        self.aliases = dict(aliases or {})


def _join(a, b):
    if a is None or b is None:
        return a if b is None else b
    na, oa, sa = len(a.inputs), len(a.out_shapes), len(a.scratch)

    def start(i, o, s):
        a.start(i[:na], o[:oa], s[:sa])
        b.start(i[na:], o[oa:], s[sa:])

    def finish(i, o, s):
        a.finish(i[:na], o[:oa], s[:sa])
        b.finish(i[na:], o[oa:], s[sa:])

    aliases = dict(a.aliases)
    aliases.update({na + k: oa + v for k, v in b.aliases.items()})
    return _Rider(a.inputs + b.inputs, a.out_shapes + b.out_shapes, a.scratch + b.scratch, start, finish, aliases)


def _hosted_call(body, rider, *, name, grid, in_specs, out_specs, out_shape, scratch_shapes, args, semantics):
    n_in, n_out, n_scr = len(in_specs), len(out_specs), len(scratch_shapes)
    if rider is None:
        outs = pl.pallas_call(
            body, name=name, grid=grid, in_specs=in_specs, out_specs=out_specs, out_shape=out_shape,
            scratch_shapes=scratch_shapes, compiler_params=_params(*semantics),
        )(*args)
        return list(outs), []
    ri, ro = len(rider.inputs), len(rider.out_shapes)

    def wrapped(*refs):
        p = 0
        hin = refs[p : p + n_in]
        p += n_in
        rin = refs[p : p + ri]
        p += ri
        hout = refs[p : p + n_out]
        p += n_out
        rout = refs[p : p + ro]
        p += ro
        hscr = refs[p : p + n_scr]
        rscr = refs[p + n_scr :]

        @pl.when(pl.program_id(0) == 0)
        def _():
            rider.start(rin, rout, rscr)

        body(*hin, *hout, *hscr)

        @pl.when(pl.program_id(0) == grid[0] - 1)
        def _():
            rider.finish(rin, rout, rscr)

    outs = pl.pallas_call(
        wrapped,
        name=name,
        grid=grid,
        in_specs=list(in_specs) + [ANY] * ri,
        out_specs=list(out_specs) + [ANY] * ro,
        out_shape=list(out_shape) + rider.out_shapes,
        scratch_shapes=list(scratch_shapes) + rider.scratch,
        input_output_aliases={n_in + k: n_out + v for k, v in rider.aliases.items()},
        compiler_params=_params(*semantics),
    )(*args, *rider.inputs)
    return list(outs[:n_out]), list(outs[n_out:])


def _run_rider(rider, name):
    ri, ro = len(rider.inputs), len(rider.out_shapes)

    def body(*refs):
        rin, rout, rscr = refs[:ri], refs[ri : ri + ro], refs[ri + ro :]
        rider.start(rin, rout, rscr)
        rider.finish(rin, rout, rscr)

    return list(
        pl.pallas_call(
            body,
            name=name,
            in_specs=[ANY] * ri,
            out_specs=[ANY] * ro,
            out_shape=rider.out_shapes,
            scratch_shapes=rider.scratch,
            input_output_aliases=rider.aliases,
        )(*rider.inputs)
    )


def _lower_bounds(logits):
    def body(lg_ref, lb_ref):
        lg = lg_ref[...]
        mx = jnp.max(lg, axis=0, keepdims=True)
        e = jnp.exp(lg - mx)
        p = e / jnp.sum(e, axis=0, keepdims=True)
        run = p[0:1]
        rows = [run - p[0:1]]
        for l in range(1, DEPTH):
            run = run + p[l : l + 1]
            rows.append(run - p[0:1])
        raw = jnp.concatenate(rows, axis=0)
        lb_ref[...] = jnp.minimum(jnp.maximum(raw, 0.0), LB_MAX)

    return pl.pallas_call(body, name="lower_bounds", out_shape=SDS(logits.shape, F32))(logits)


def _lower_bounds_bwd(logits, dlb):
    def body(lg_ref, dlb_ref, dlg_ref):
        lg = lg_ref[...]
        mx = jnp.max(lg, axis=0, keepdims=True)
        e = jnp.exp(lg - mx)
        p = e / jnp.sum(e, axis=0, keepdims=True)
        run = p[0:1]
        rows = [run - p[0:1]]
        for l in range(1, DEPTH):
            run = run + p[l : l + 1]
            rows.append(run - p[0:1])
        raw = jnp.concatenate(rows, axis=0)
        lo = jnp.where(raw > 0.0, 1.0, jnp.where(raw == 0.0, 0.5, 0.0))
        clipped = jnp.maximum(raw, 0.0)
        hi = jnp.where(clipped < LB_MAX, 1.0, jnp.where(clipped == LB_MAX, 0.5, 0.0))
        draw = dlb_ref[...] * lo * hi
        total = jnp.sum(draw, axis=0, keepdims=True)
        dps = []
        tail = total
        for j in range(DEPTH):
            dps.append(tail - total if j == 0 else tail)
            tail = tail - draw[j : j + 1]
        dp = jnp.concatenate(dps, axis=0)
        dlg_ref[...] = p * (dp - jnp.sum(p * dp, axis=0, keepdims=True))

    return pl.pallas_call(body, name="lower_bounds_bwd", out_shape=SDS(logits.shape, F32))(logits, dlb)


def _in_proj(x, gain, w, lng, lnb, ws, bst, name, rider=None):
    s, d = x.shape
    n = w.shape[1]
    tm = _divisor(s, 512, GM_BLOCK)

    def body(x_ref, g_ref, w_ref, lng_ref, lnb_ref, ws_ref, bst_ref, z_ref, h_ref, a_ref):
        xv = x_ref[...]
        r = lax.rsqrt(jnp.mean(xv * xv, axis=-1, keepdims=True) + EPS)
        h = ((xv * r) * g_ref[...]).astype(MM_DTYPE)
        h_ref[...] = h
        uv = jnp.dot(h, w_ref[:, : 2 * BR_DIM], preferred_element_type=F32)
        z_ref[:, : 2 * BR_DIM] = uv
        z_ref[:, 2 * BR_DIM :] = jnp.dot(h, w_ref[:, 2 * BR_DIM :], preferred_element_type=F32)
        _gmlp_fwd_tile(uv[:, :BR_DIM], uv[:, BR_DIM:], lng_ref, lnb_ref, ws_ref, bst_ref, a_ref)

    row = lambda wd: pl.BlockSpec((tm, wd), lambda i: (i, 0))
    return _hosted_call(
        body,
        rider,
        name=name,
        grid=(s // tm,),
        in_specs=[row(d), _resident((1, d), lambda i: (0, 0)), _resident((d, n), lambda i: (0, 0))] + _gmlp_param_specs(),
        out_specs=[row(n), row(d), row(BR_DIM)],
        out_shape=[SDS((s, n), F32), SDS((s, d), MM_DTYPE), SDS((s, BR_DIM), MM_DTYPE)],
        scratch_shapes=[],
        args=[x, gain, w, lng, lnb, ws, bst],
        semantics=("arbitrary",),
    )


def _in_proj_bwd(dz_parts, w, x, gain, dres, name, rider=None):
    s, d = x.shape
    n = w.shape[1]
    tm = _divisor(s, 512, 16)
    widths = [p.shape[1] for p in dz_parts]
    offsets = [sum(widths[:k]) for k in range(len(widths))]
    n_parts = len(dz_parts)

    def body(*refs):
        dz_refs = refs[:n_parts]
        w_ref, x_ref, g_ref, dres_ref, dx_ref, dg_ref = refs[n_parts:]

        @pl.when(pl.program_id(0) == 0)
        def _():
            dg_ref[...] = jnp.zeros_like(dg_ref)

        dh = None
        for dz_ref, off, wd in zip(dz_refs, offsets, widths):
            part = _dot_nt(dz_ref[...], w_ref[:, off : off + wd])
            dh = part if dh is None else dh + part
        xv = x_ref[...]
        r = lax.rsqrt(jnp.mean(xv * xv, axis=-1, keepdims=True) + EPS)
        xh = xv * r
        dg_ref[...] += jnp.sum(dh * xh, axis=0, keepdims=True)
        dxh = dh * g_ref[...]
        dx_ref[...] = dres_ref[...] + r * (dxh - xh * jnp.mean(dxh * xh, axis=-1, keepdims=True))

    in_specs = [pl.BlockSpec((tm, wd), lambda i: (i, 0)) for wd in widths] + [
        _resident((d, n), lambda i: (0, 0)),
        pl.BlockSpec((tm, d), lambda i: (i, 0)),
        _resident((1, d), lambda i: (0, 0)),
        pl.BlockSpec((tm, d), lambda i: (i, 0)),
    ]
    return _hosted_call(
        body,
        rider,
        name=name,
        grid=(s // tm,),
        in_specs=in_specs,
        out_specs=[pl.BlockSpec((tm, d), lambda i: (i, 0)), pl.BlockSpec((1, d), lambda i: (0, 0))],
        out_shape=[SDS((s, d), F32), SDS((1, d), F32)],
        scratch_shapes=[],
        args=[*dz_parts, w, x, gain, dres],
        semantics=("arbitrary",),
    )


def _matmul_tn(a, b, stack, layer, name, n_total=None, col_off=0):
    s, k = a.shape
    n = b.shape[1]
    n_total = n if n_total is None else n_total
    tn = _divisor(math.gcd(n, col_off) if col_off else n, max(128, ACC_BYTES // (4 * k)), 128)
    ts = _divisor(s, min(2048, max(512, ACC_BYTES // (2 * k))), 16)
    off = col_off // tn

    def body(a_ref, b_ref, *rest):
        out_ref = rest[-1]

        @pl.when(pl.program_id(1) == 0)
        def _():
            out_ref[...] = jnp.zeros_like(out_ref)

        out_ref[...] += _dot_tn(a_ref[...], b_ref[...])

    in_specs = [pl.BlockSpec((ts, k), lambda j, i: (i, 0)), pl.BlockSpec((ts, tn), lambda j, i: (i, j))]
    args = [a, b]
    aliases = {}
    if stack is not None:
        in_specs.append(ANY)
        args.append(stack)
        aliases = {2: 0}
    return pl.pallas_call(
        body,
        name=name,
        grid=(n // tn, s // ts),
        in_specs=in_specs,
        out_specs=pl.BlockSpec((None, k, tn), lambda j, i: (layer, 0, off + j)),
        out_shape=SDS((DEPTH, k, n_total), F32),
        input_output_aliases=aliases,
        compiler_params=_params("parallel", "arbitrary"),
    )(*args)


def _gm_mask():
    r = lax.broadcasted_iota(jnp.int32, (GM_BLOCK, GM_BLOCK), 0) // CHUNK
    c = lax.broadcasted_iota(jnp.int32, (GM_BLOCK, GM_BLOCK), 1) // CHUNK
    return r >= c


def _gm_normed(v, lng, lnb):
    vg = _gelu(v)
    mu = jnp.mean(vg, axis=-1, keepdims=True)
    xc = vg - mu
    rstd = lax.rsqrt(jnp.mean(xc * xc, axis=-1, keepdims=True) + EPS)
    xh = xc * rstd
    return xh, rstd, xh * lng + lnb


def _gmlp_param_specs():
    return [
        _resident((1, BR_DIM), lambda i: (0, 0)),
        _resident((1, BR_DIM), lambda i: (0, 0)),
        _resident((N_HEADS, GM_BLOCK, GM_BLOCK), lambda i: (0, 0, 0)),
        _resident((GM_BLOCK, N_HEADS), lambda i: (0, 0)),
    ]


def _gmlp_fwd_tile(u, v, lng_ref, lnb_ref, ws_ref, bst_ref, a_ref):
    mask = _gm_mask()
    ug = _gelu(u)
    _, _, vn = _gm_normed(v, lng_ref[...], lnb_ref[...])
    vn = vn.astype(MM_DTYPE)
    for h in range(N_HEADS):
        cs = slice(h * HEAD_DIM, (h + 1) * HEAD_DIM)
        wm = jnp.where(mask, ws_ref[h], 0.0).astype(MM_DTYPE)
        for blk in range(u.shape[0] // GM_BLOCK):
            rs = slice(blk * GM_BLOCK, (blk + 1) * GM_BLOCK)
            mixed = _dot(wm, vn[rs, cs]) + bst_ref[:, h : h + 1]
            a_ref[rs, cs] = (ug[rs, cs] * mixed).astype(MM_DTYPE)


def _gmlp_bwd_tile(da_v, u, v, lng_ref, lnb_ref, ws_ref, bst_ref, dz_ref, dlng_ref, dlnb_ref, dws_ref, dbst_ref):
    mask = _gm_mask()
    ug = _gelu(u)
    lng_v = lng_ref[...]
    xh, rstd, vn = _gm_normed(v, lng_v, lnb_ref[...])
    vnb = vn.astype(MM_DTYPE)
    dmix = da_v * ug
    dmixb = dmix.astype(MM_DTYPE)
    dvn_cols = []
    for h in range(N_HEADS):
        cs = slice(h * HEAD_DIM, (h + 1) * HEAD_DIM)
        wm = jnp.where(mask, ws_ref[h], 0.0).astype(MM_DTYPE)
        dw = jnp.zeros((GM_BLOCK, GM_BLOCK), F32)
        dbias = jnp.zeros((GM_BLOCK, 1), F32)
        dvn_rows = []
        for blk in range(u.shape[0] // GM_BLOCK):
            rs = slice(blk * GM_BLOCK, (blk + 1) * GM_BLOCK)
            mixed = _dot(wm, vnb[rs, cs]) + bst_ref[:, h : h + 1]
            dz_ref[rs, cs] = (da_v[rs, cs] * mixed * _gelu_grad(u[rs, cs])).astype(MM_DTYPE)
            dw = dw + _dot_nt(dmixb[rs, cs], vnb[rs, cs])
            dbias = dbias + jnp.sum(dmix[rs, cs], axis=1, keepdims=True)
            dvn_rows.append(_dot_tn(wm, dmixb[rs, cs]))
        dws_ref[h] += jnp.where(mask, dw, 0.0)
        dbst_ref[:, h : h + 1] += dbias
        dvn_cols.append(jnp.concatenate(dvn_rows, axis=0) if len(dvn_rows) > 1 else dvn_rows[0])
    dvn = jnp.concatenate(dvn_cols, axis=1)
    dlng_ref[...] += jnp.sum(dvn * xh, axis=0, keepdims=True)
    dlnb_ref[...] += jnp.sum(dvn, axis=0, keepdims=True)
    dxh = dvn * lng_v
    dvg = rstd * (dxh - jnp.mean(dxh, axis=-1, keepdims=True) - xh * jnp.mean(dxh * xh, axis=-1, keepdims=True))
    dz_ref[:, BR_DIM:] = (dvg * _gelu_grad(v)).astype(MM_DTYPE)


def _gmlp_bwd(da, z, lng, lnb, ws, bst, name):
    s = z.shape[0]
    tg = _divisor(s, 256, GM_BLOCK)

    def body(da_ref, z_ref, lng_ref, lnb_ref, ws_ref, bst_ref, dz_ref, dlng_ref, dlnb_ref, dws_ref, dbst_ref):
        @pl.when(pl.program_id(0) == 0)
        def _():
            for ref in (dlng_ref, dlnb_ref, dws_ref, dbst_ref):
                ref[...] = jnp.zeros_like(ref)

        _gmlp_bwd_tile(da_ref[...], z_ref[:, :BR_DIM], z_ref[:, BR_DIM:], lng_ref, lnb_ref, ws_ref, bst_ref,
                       dz_ref, dlng_ref, dlnb_ref, dws_ref, dbst_ref)

    fixed = lambda shape: pl.BlockSpec(shape, lambda i: tuple(0 for _ in shape))
    return pl.pallas_call(
        body,
        name=name,
        grid=(s // tg,),
        in_specs=[pl.BlockSpec((tg, BR_DIM), lambda i: (i, 0)), pl.BlockSpec((tg, 2 * BR_DIM), lambda i: (i, 0))] + _gmlp_param_specs(),
        out_specs=[
            pl.BlockSpec((tg, 2 * BR_DIM), lambda i: (i, 0)),
            fixed((1, BR_DIM)),
            fixed((1, BR_DIM)),
            fixed((N_HEADS, GM_BLOCK, GM_BLOCK)),
            fixed((GM_BLOCK, N_HEADS)),
        ],
        out_shape=[
            SDS((s, 2 * BR_DIM), MM_DTYPE),
            SDS((1, BR_DIM), F32),
            SDS((1, BR_DIM), F32),
            SDS((N_HEADS, GM_BLOCK, GM_BLOCK), F32),
            SDS((GM_BLOCK, N_HEADS), F32),
        ],
        compiler_params=_params("arbitrary"),
    )(da, z, lng, lnb, ws, bst)


HG_ROWS = 512
HG_UNROLL = 8
MID = CHUNK // 2 - 1


def _tri(lower):
    r = lax.broadcasted_iota(jnp.int32, (CHUNK, CHUNK), 0)
    c = lax.broadcasted_iota(jnp.int32, (CHUNK, CHUNK), 1)
    return r >= c if lower else c >= r


def _scan_rows(x, reverse=False):
    n = x.shape[0]
    rid = lax.broadcasted_iota(jnp.int32, x.shape, 0)
    k = 1
    while k < n:
        if reverse:
            x = x + jnp.where(rid < n - k, pltpu.roll(x, n - k, 0), 0.0)
        else:
            x = x + jnp.where(rid >= k, pltpu.roll(x, k, 0), 0.0)
        k *= 2
    return x


def _hgrn_fwd_chunk(zqf_ref, zio_ref, ci, lb_ref, ng_v, state, o_ref, b_ref, st_ref):
    low = _tri(True)
    rows = slice(ci * CHUNK, (ci + 1) * CHUNK)
    for h in range(N_HEADS):
        cs = slice(h * HEAD_DIM, (h + 1) * HEAD_DIM)
        cs2 = slice(BR_DIM + h * HEAD_DIM, BR_DIM + (h + 1) * HEAD_DIM)
        zq = zqf_ref[rows, cs]
        zf = zqf_ref[rows, cs2]
        vi = zio_ref[rows, cs]
        zog = zio_ref[rows, cs2]
        lbh = lb_ref[:, cs]
        qf = zq * _sigmoid(zq)
        forget = lbh + (1.0 - lbh) * _sigmoid(zf)
        g = jnp.log(jnp.maximum(forget, TINY))
        k = (1.0 - lbh) * _sigmoid(-zf)
        gc = _scan_rows(g)
        gm = gc[MID : MID + 1]
        gl = gc[CHUNK - 1 : CHUNK]
        a = jnp.where(low, _dot_nt(qf * jnp.exp(gc - gm), k * jnp.exp(gm - gc)), 0.0)
        st = state[h]
        st_ref[ci, h] = st
        o = _dot(a, vi) + _dot_nt(qf * jnp.exp(gc), st)
        state[h] = st * jnp.exp(gl) + _dot_tn(vi, k * jnp.exp(gl - gc))
        r = lax.rsqrt(jnp.mean(o * o, axis=-1, keepdims=True) + EPS)
        o_ref[rows, cs] = o
        b_ref[rows, cs] = (((o * r) * ng_v) * (zog * _sigmoid(zog))).astype(MM_DTYPE)


def _hgrn_fwd(z, lb, ng, name, rider=None):
    s = z.shape[0]
    rows_per = _divisor(s, HG_ROWS, CHUNK)
    n_sub = rows_per // CHUNK

    def body(zqf_ref, zio_ref, lb_ref, ng_ref, o_ref, b_ref, st_ref, state):
        @pl.when(pl.program_id(0) == 0)
        def _():
            state[...] = jnp.zeros_like(state)

        ng_v = ng_ref[...]
        for ci in range(n_sub):
            _hgrn_fwd_chunk(zqf_ref, zio_ref, ci, lb_ref, ng_v, state, o_ref, b_ref, st_ref)

    return _hosted_call(
        body,
        rider,
        name=name,
        grid=(s // rows_per,),
        in_specs=[
            pl.BlockSpec((rows_per, 2 * BR_DIM), lambda i: (i, 1)),
            pl.BlockSpec((rows_per, 2 * BR_DIM), lambda i: (i, 2)),
            _resident((1, BR_DIM), lambda i: (0, 0)),
            _resident((1, HEAD_DIM), lambda i: (0, 0)),
        ],
        out_specs=[
            pl.BlockSpec((rows_per, BR_DIM), lambda i: (i, 0)),
            pl.BlockSpec((rows_per, BR_DIM), lambda i: (i, 0)),
            pl.BlockSpec((n_sub, N_HEADS, HEAD_DIM, HEAD_DIM), lambda i: (i, 0, 0, 0)),
        ],
        out_shape=[
            SDS((s, BR_DIM), F32),
            SDS((s, BR_DIM), MM_DTYPE),
            SDS((s // CHUNK, N_HEADS, HEAD_DIM, HEAD_DIM), F32),
        ],
        scratch_shapes=[pltpu.VMEM((N_HEADS, HEAD_DIM, HEAD_DIM), F32)],
        args=[z, z, lb, ng],
        semantics=("arbitrary",),
    )


def _hgrn_bwd(db, z, o, states, lb, ng, name, rider=None):
    s = z.shape[0]
    rows_per = _divisor(s, HG_ROWS, CHUNK)
    n_sub = rows_per // CHUNK
    n_steps = s // rows_per

    def body(db_ref, zqf_ref, zio_ref, o_ref, st_ref, lb_ref, ng_ref, dz_ref, dlb_ref, dng_ref, dstate):
        @pl.when(pl.program_id(0) == 0)
        def _():
            dstate[...] = jnp.zeros_like(dstate)
            dlb_ref[...] = jnp.zeros_like(dlb_ref)
            dng_ref[...] = jnp.zeros_like(dng_ref)

        low = _tri(True)
        ng_v = ng_ref[...]

        def chunk(cc, carry):
            ci = n_sub - 1 - cc
            rows = pl.ds(pl.multiple_of(ci * CHUNK, CHUNK), CHUNK)
            for h in range(N_HEADS):
                cs = slice(h * HEAD_DIM, (h + 1) * HEAD_DIM)
                cs2 = slice(BR_DIM + h * HEAD_DIM, BR_DIM + (h + 1) * HEAD_DIM)
                zq = zqf_ref[rows, cs]
                zf = zqf_ref[rows, cs2]
                vi = zio_ref[rows, cs]
                zog = zio_ref[rows, cs2]
                lbh = lb_ref[:, cs]
                ov = o_ref[rows, cs]
                dbv = db_ref[rows, cs]
                r = lax.rsqrt(jnp.mean(ov * ov, axis=-1, keepdims=True) + EPS)
                on = ov * r
                sgo = _sigmoid(zog)
                gate = zog * sgo
                dng_ref[...] += jnp.sum(dbv * gate * on, axis=0, keepdims=True)
                don = dbv * gate * ng_v
                dzog = dbv * (on * ng_v) * _silu_grad(zog, sgo)
                do = r * (don - on * jnp.mean(don * on, axis=-1, keepdims=True))
                sq = _sigmoid(zq)
                qf = zq * sq
                sg = _sigmoid(zf)
                sgn = _sigmoid(-zf)
                oml = 1.0 - lbh
                forget = lbh + oml * sg
                fm = jnp.maximum(forget, TINY)
                k = oml * sgn
                gc = _scan_rows(jnp.log(fm))
                gm = gc[MID : MID + 1]
                gl = gc[CHUNK - 1 : CHUNK]
                e_g = jnp.exp(gc)
                e_q = jnp.exp(gc - gm)
                e_k = jnp.exp(gm - gc)
                e_kd = jnp.exp(gl - gc)
                e_gl = jnp.exp(gl)
                qt = qf * e_q
                kt = k * e_k
                a = jnp.where(low, _dot_nt(qt, kt), 0.0)
                st = st_ref[ci, h]
                dst = dstate[h]
                dp = jnp.where(low, _dot_nt(do, vi), 0.0)
                dv = _dot_tn(a, do) + _dot_nt(k * e_kd, dst)
                dq = _dot(dp, kt) * e_q + e_g * _dot(do, st)
                dks = e_kd * _dot(vi, dst)
                dk = _dot_tn(dp, qt) * e_k + dks
                dstate[h] = _dot_tn(do, qf * e_g) + dst * e_gl
                dgc = qf * dq - k * dk
                extra = e_gl * jnp.sum(st * dst, axis=0, keepdims=True) + jnp.sum(k * dks, axis=0, keepdims=True)
                dg = _scan_rows(dgc, reverse=True) + extra
                dforget = jnp.where(forget > TINY, dg / fm, 0.0)
                both = dforget - dk
                dlb_ref[:, cs] += jnp.sum(sgn * both, axis=0, keepdims=True)
                dz_ref[rows, cs] = (dq * _silu_grad(zq, sq)).astype(MM_DTYPE)
                dz_ref[rows, cs2] = (oml * sg * sgn * both).astype(MM_DTYPE)
                dz_ref[rows, 2 * BR_DIM + h * HEAD_DIM : 2 * BR_DIM + (h + 1) * HEAD_DIM] = dv.astype(MM_DTYPE)
                dz_ref[rows, 3 * BR_DIM + h * HEAD_DIM : 3 * BR_DIM + (h + 1) * HEAD_DIM] = dzog.astype(MM_DTYPE)
            return carry

        lax.fori_loop(0, n_sub, chunk, 0, unroll=math.gcd(n_sub, HG_UNROLL))

    rev = lambda i: n_steps - 1 - i
    return _hosted_call(
        body,
        rider,
        name=name,
        grid=(n_steps,),
        in_specs=[
            pl.BlockSpec((rows_per, BR_DIM), lambda i: (rev(i), 0)),
            pl.BlockSpec((rows_per, 2 * BR_DIM), lambda i: (rev(i), 1)),
            pl.BlockSpec((rows_per, 2 * BR_DIM), lambda i: (rev(i), 2)),
            pl.BlockSpec((rows_per, BR_DIM), lambda i: (rev(i), 0)),
            pl.BlockSpec((n_sub, N_HEADS, HEAD_DIM, HEAD_DIM), lambda i: (rev(i), 0, 0, 0)),
            _resident((1, BR_DIM), lambda i: (0, 0)),
            _resident((1, HEAD_DIM), lambda i: (0, 0)),
        ],
        out_specs=[
            pl.BlockSpec((rows_per, 4 * BR_DIM), lambda i: (rev(i), 0)),
            pl.BlockSpec((1, BR_DIM), lambda i: (0, 0)),
            pl.BlockSpec((1, HEAD_DIM), lambda i: (0, 0)),
        ],
        out_shape=[SDS((s, 4 * BR_DIM), MM_DTYPE), SDS((1, BR_DIM), F32), SDS((1, HEAD_DIM), F32)],
        scratch_shapes=[pltpu.VMEM((N_HEADS, HEAD_DIM, HEAD_DIM), F32)],
        args=[db, z, z, o, states, lb, ng],
        semantics=("arbitrary",),
    )


def _mix_fwd(x, a, b, z, w_gm, w_hg, w_out, name, rider=None):
    s, d = x.shape
    tm = _divisor(s, 512, 16)
    g0 = 6 * BR_DIM // d

    def body(x_ref, a_ref, b_ref, ga_ref, gb_ref, wgm_ref, whg_ref, wout_ref, out_ref):
        y = _sigmoid(ga_ref[...]) * _dot(a_ref[...], wgm_ref[...]) + _sigmoid(gb_ref[...]) * _dot(b_ref[...], whg_ref[...])
        out_ref[...] = x_ref[...] + _dot(y, wout_ref[...])

    return _hosted_call(
        body,
        rider,
        name=name,
        grid=(s // tm,),
        in_specs=[
            pl.BlockSpec((tm, d), lambda i: (i, 0)),
            pl.BlockSpec((tm, BR_DIM), lambda i: (i, 0)),
            pl.BlockSpec((tm, BR_DIM), lambda i: (i, 0)),
            pl.BlockSpec((tm, d), lambda i: (i, g0)),
            pl.BlockSpec((tm, d), lambda i: (i, g0 + 1)),
            _resident((BR_DIM, d), lambda i: (0, 0)),
            _resident((BR_DIM, d), lambda i: (0, 0)),
            _resident((d, d), lambda i: (0, 0)),
        ],
        out_specs=[pl.BlockSpec((tm, d), lambda i: (i, 0))],
        out_shape=[SDS((s, d), F32)],
        scratch_shapes=[],
        args=[x, a, b, z, z, w_gm, w_hg, w_out],
        semantics=("arbitrary",),
    )


def _mix_bwd(dx, a, b, z, w_gm, w_hg, w_out, name, rider=None):
    s, d = dx.shape
    tm = _divisor(s, 512, 16)
    g0 = 6 * BR_DIM // d

    def body(dx_ref, a_ref, b_ref, ga_ref, gb_ref, wgm_ref, whg_ref, wout_ref, y_ref, dpa_ref, dpb_ref, da_ref, db_ref, dg_ref):
        dy = _dot_nt(dx_ref[...], wout_ref[...])
        pa = _dot(a_ref[...], wgm_ref[...])
        pb = _dot(b_ref[...], whg_ref[...])
        sa = _sigmoid(ga_ref[...])
        sb = _sigmoid(gb_ref[...])
        y_ref[...] = (sa * pa + sb * pb).astype(MM_DTYPE)
        dpa = (dy * sa).astype(MM_DTYPE)
        dpb = (dy * sb).astype(MM_DTYPE)
        dpa_ref[...] = dpa
        dpb_ref[...] = dpb
        da_ref[...] = _dot_nt(dpa, wgm_ref[...])
        db_ref[...] = _dot_nt(dpb, whg_ref[...])
        dg_ref[:, :d] = (dy * pa * sa * (1.0 - sa)).astype(MM_DTYPE)
        dg_ref[:, d:] = (dy * pb * sb * (1.0 - sb)).astype(MM_DTYPE)

    row = lambda w: pl.BlockSpec((tm, w), lambda i: (i, 0))
    return _hosted_call(
        body,
        rider,
        name=name,
        grid=(s // tm,),
        in_specs=[
            row(d),
            row(BR_DIM),
            row(BR_DIM),
            pl.BlockSpec((tm, d), lambda i: (i, g0)),
            pl.BlockSpec((tm, d), lambda i: (i, g0 + 1)),
            _resident((BR_DIM, d), lambda i: (0, 0)),
            _resident((BR_DIM, d), lambda i: (0, 0)),
            _resident((d, d), lambda i: (0, 0)),
        ],
        out_specs=[row(d), row(d), row(d), row(BR_DIM), row(BR_DIM), row(2 * d)],
        out_shape=[
            SDS((s, d), MM_DTYPE),
            SDS((s, d), MM_DTYPE),
            SDS((s, d), MM_DTYPE),
            SDS((s, BR_DIM), F32),
            SDS((s, BR_DIM), F32),
            SDS((s, 2 * d), MM_DTYPE),
        ],
        scratch_shapes=[],
        args=[dx, a, b, z, z, w_gm, w_hg, w_out],
        semantics=("arbitrary",),
    )


HALO = 8
FFN_ROWS = 256


def _shift_down(v, prev, n):
    rolled = pltpu.roll(v, n, 0)
    rid = lax.broadcasted_iota(jnp.int32, v.shape, 0)
    out = rolled
    for r in range(n):
        out = jnp.where(rid == r, prev[HALO - n + r : HALO - n + r + 1], out)
    return out


def _shift_up(v, nxt, n):
    tm = v.shape[0]
    rolled = pltpu.roll(v, tm - n, 0)
    rid = lax.broadcasted_iota(jnp.int32, v.shape, 0)
    out = rolled
    for r in range(n):
        out = jnp.where(rid == tm - n + r, nxt[r : r + 1], out)
    return out


def _conv_cols(f):
    return _divisor(f, 256, 128)


def _ffn_fwd(x, gain, w_up, cw, cb, w_down, layer, name, rider=None):
    s, d = x.shape
    f = w_down.shape[0]
    tm = _divisor(s, FFN_ROWS, 16)
    cwid = _conv_cols(f)

    def body(x_ref, g_ref, wu_ref, cw_ref, cb_ref, wd_ref, out_ref, z_ref, h_ref, conv_ref, halo):
        @pl.when(pl.program_id(0) == 0)
        def _():
            halo[...] = jnp.zeros_like(halo)

        xv = x_ref[...]
        r = lax.rsqrt(jnp.mean(xv * xv, axis=-1, keepdims=True) + EPS)
        h = ((xv * r) * g_ref[...]).astype(MM_DTYPE)
        h_ref[...] = h
        acc = xv
        starts = list(range(0, f, cwid))

        def project(c0):
            return [jnp.dot(h, wu_ref[:, base : base + cwid], preferred_element_type=F32) for base in (c0, f + c0)]

        ahead = project(starts[0])
        for n, c0 in enumerate(starts):
            halves = []
            current = ahead
            if n + 1 < len(starts):
                ahead = project(starts[n + 1])
            for base, zc in zip((c0, f + c0), current):
                cols = slice(base, base + cwid)
                z_ref[:, cols] = zc.astype(MM_DTYPE)
                prev = halo[:, cols]
                conv = cb_ref[:, cols] + cw_ref[0:1, cols] * _shift_down(zc, prev, 2)
                conv = conv + cw_ref[1:2, cols] * _shift_down(zc, prev, 1) + cw_ref[2:3, cols] * zc
                halo[:, cols] = zc[tm - HALO : tm]
                conv_ref[:, cols] = conv.astype(MM_DTYPE)
                halves.append(conv)
            gate, val = halves
            act = gate * _sigmoid(gate) * val
            acc = acc + _dot(act, wd_ref[c0 : c0 + cwid, :])
        out_ref[...] = acc

    row = lambda w: pl.BlockSpec((tm, w), lambda i: (i, 0))
    return _hosted_call(
        body,
        rider,
        name=name,
        grid=(s // tm,),
        in_specs=[
            row(d),
            _resident((1, d), lambda i: (0, 0)),
            _resident((d, 2 * f), lambda i: (0, 0)),
            _resident((None, 3, 2 * f), lambda i: (layer, 0, 0)),
            _resident((1, 2 * f), lambda i: (0, 0)),
            _resident((f, d), lambda i: (0, 0)),
        ],
        out_specs=[row(d), row(2 * f), row(d), row(2 * f)],
        out_shape=[SDS((s, d), F32), SDS((s, 2 * f), MM_DTYPE), SDS((s, d), MM_DTYPE), SDS((s, 2 * f), MM_DTYPE)],
        scratch_shapes=[pltpu.VMEM((HALO, 2 * f), F32)],
        args=[x, gain, w_up, cw, cb, w_down],
        semantics=("arbitrary",),
    )


def _ffn_bwd(dx, z2, conv, cw, w_down, w_up, x, gain, layer, name):
    s, d = dx.shape
    f = z2.shape[1] // 2
    tm = _divisor(s, 256, 16)
    cwid = _conv_cols(f)
    n_steps = s // tm

    def body(dx_ref, z_ref, conv_ref, cw_ref, wd_ref, wu_ref, x_ref, g_ref, dz_ref, act_ref, dcw_ref, dx1_ref, dg_ref, nxt):
        @pl.when(pl.program_id(0) == 0)
        def _():
            nxt[...] = jnp.zeros_like(nxt)
            dcw_ref[...] = jnp.zeros_like(dcw_ref)
            dg_ref[...] = jnp.zeros_like(dg_ref)

        dxf = dx_ref[...]
        dxv = dxf.astype(MM_DTYPE)
        starts = list(range(0, f, cwid))
        ahead = _dot_nt(dxv, wd_ref[0:cwid, :])
        dh = jnp.zeros((tm, d), F32)
        pending = []
        for n, c0 in enumerate(starts):
            dact = ahead
            if n + 1 < len(starts):
                ahead = _dot_nt(dxv, wd_ref[starts[n + 1] : starts[n + 1] + cwid, :])
            for dz_prev, cols_prev in pending:
                dh = dh + _dot_nt(dz_prev, wu_ref[:, cols_prev])
            pending = []
            gate = conv_ref[:, c0 : c0 + cwid].astype(F32)
            val = conv_ref[:, f + c0 : f + c0 + cwid].astype(F32)
            sg = _sigmoid(gate)
            silu = gate * sg
            act_ref[:, c0 : c0 + cwid] = (silu * val).astype(MM_DTYPE)
            dconvs = (dact * val * _silu_grad(gate, sg), dact * silu)
            for base, dconv in zip((c0, f + c0), dconvs):
                cols = slice(base, base + cwid)
                zc = z_ref[:, cols].astype(F32)
                nx = nxt[:, cols]
                up1 = _shift_up(dconv, nx, 1)
                up2 = _shift_up(dconv, nx, 2)
                dcw_ref[0:1, cols] += jnp.sum(up2 * zc, axis=0, keepdims=True)
                dcw_ref[1:2, cols] += jnp.sum(up1 * zc, axis=0, keepdims=True)
                dcw_ref[2:3, cols] += jnp.sum(dconv * zc, axis=0, keepdims=True)
                dcw_ref[3:4, cols] += jnp.sum(dconv, axis=0, keepdims=True)
                dz = (cw_ref[2:3, cols] * dconv + cw_ref[1:2, cols] * up1 + cw_ref[0:1, cols] * up2).astype(MM_DTYPE)
                dz_ref[:, cols] = dz
                nxt[:, cols] = dconv[0:HALO]
                pending.append((dz, cols))
        for dz_prev, cols_prev in pending:
            dh = dh + _dot_nt(dz_prev, wu_ref[:, cols_prev])
        xv = x_ref[...]
        r = lax.rsqrt(jnp.mean(xv * xv, axis=-1, keepdims=True) + EPS)
        xh = xv * r
        dg_ref[...] += jnp.sum(dh * xh, axis=0, keepdims=True)
        dxh = dh * g_ref[...]
        dx1_ref[...] = dxf + r * (dxh - xh * jnp.mean(dxh * xh, axis=-1, keepdims=True))

    rev = lambda i: n_steps - 1 - i
    row = lambda w: pl.BlockSpec((tm, w), lambda i: (rev(i), 0))
    return pl.pallas_call(
        body,
        name=name,
        grid=(n_steps,),
        in_specs=[
            row(d),
            row(2 * f),
            row(2 * f),
            _resident((None, 3, 2 * f), lambda i: (layer, 0, 0)),
            _resident((f, d), lambda i: (0, 0)),
            _resident((d, 2 * f), lambda i: (0, 0)),
            row(d),
            _resident((1, d), lambda i: (0, 0)),
        ],
        out_specs=[
            row(2 * f),
            row(f),
            pl.BlockSpec((HALO, 2 * f), lambda i: (0, 0)),
            row(d),
            pl.BlockSpec((1, d), lambda i: (0, 0)),
        ],
        out_shape=[SDS((s, 2 * f), MM_DTYPE), SDS((s, f), MM_DTYPE), SDS((HALO, 2 * f), F32), SDS((s, d), F32), SDS((1, d), F32)],
        scratch_shapes=[pltpu.VMEM((HALO, 2 * f), F32)],
        compiler_params=_params("arbitrary"),
    )(dx, z2, conv, cw, w_down, w_up, x, gain)


def _loss_head(x, gain, target):
    s, d = x.shape
    tm = _divisor(s, 256, 8)

    def body(x_ref, g_ref, t_ref, loss_ref, dx_ref, dg_ref):
        @pl.when(pl.program_id(0) == 0)
        def _():
            loss_ref[...] = jnp.zeros_like(loss_ref)
            dg_ref[...] = jnp.zeros_like(dg_ref)

        xv = x_ref[...]
        gv = g_ref[...]
        r = lax.rsqrt(jnp.mean(xv * xv, axis=-1, keepdims=True) + EPS)
        xh = xv * r
        err = xh * gv - t_ref[...]
        loss_ref[...] += 0.5 * jnp.sum(jnp.mean(err * err, axis=-1, keepdims=True))
        dout = err * (1.0 / d)
        dg_ref[...] += jnp.sum(dout * xh, axis=0, keepdims=True)
        dxh = dout * gv
        dx_ref[...] = r * (dxh - xh * jnp.mean(dxh * xh, axis=-1, keepdims=True))

    return pl.pallas_call(
        body,
        name="loss_head",
        grid=(s // tm,),
        in_specs=[
            pl.BlockSpec((tm, d), lambda i: (i, 0)),
            _resident((1, d), lambda i: (0, 0)),
            pl.BlockSpec((tm, d), lambda i: (i, 0)),
        ],
        out_specs=[
            pl.BlockSpec((8, 128), lambda i: (0, 0)),
            pl.BlockSpec((tm, d), lambda i: (i, 0)),
            pl.BlockSpec((1, d), lambda i: (0, 0)),
        ],
        out_shape=[SDS((8, 128), F32), SDS((s, d), F32), SDS((1, d), F32)],
        compiler_params=_params("arbitrary"),
    )(x, gain, target)


def _adamw(w, g, m, v, name):
    shape = w.shape
    cols = shape[-1]
    rows = max(1, math.prod(shape[:-1]))
    tr = _divisor(rows, max(8, TILE_BYTES // (4 * cols)), 8)
    flat = [t.reshape(rows, cols) for t in (w, g, m, v)]

    def body(w_ref, g_ref, m_ref, v_ref, d_ref, nm_ref, nv_ref):
        gv = g_ref[...]
        m2 = ADAM_B1 * m_ref[...] + (1.0 - ADAM_B1) * gv
        v2 = ADAM_B2 * v_ref[...] + (1.0 - ADAM_B2) * (gv * gv)
        m_hat = m2 / (1.0 - ADAM_B1**ADAM_STEP)
        v_hat = v2 / (1.0 - ADAM_B2**ADAM_STEP)
        d_ref[...] = -ADAM_LR * (m_hat / (jnp.sqrt(v_hat) + ADAM_EPS) + ADAM_WD * w_ref[...])
        nm_ref[...] = m2
        nv_ref[...] = v2

    spec = pl.BlockSpec((tr, cols), lambda i: (i, 0))
    outs = pl.pallas_call(
        body,
        name=name,
        grid=(rows // tr,),
        in_specs=[spec] * 4,
        out_specs=[spec] * 3,
        out_shape=[SDS((rows, cols), F32)] * 3,
        compiler_params=_params("parallel"),
    )(*flat)
    return tuple(t.reshape(shape) for t in outs)


def _position():
    return lax.axis_index("x"), lax.axis_index("y"), lax.axis_index("c")


def _other_chips(x, y):
    return [(1 - x, y), (x, 1 - y), (1 - x, 1 - y)]


def _remote(src, dst, send_sem, recv_sem, device):
    return pltpu.make_async_remote_copy(
        src_ref=src, dst_ref=dst, send_sem=send_sem, recv_sem=recv_sem, device_id=device, device_id_type=MESH
    )


def _sub(ref, lead, shape, axis, chip=None, half=None):
    cut = [pl.ds(0, shape[0]), pl.ds(0, shape[1])]
    if chip is not None:
        size = shape[axis] // N_CHIPS
        cut[axis] = pl.ds(chip * size, size)
    if half is not None:
        size = shape[1 - axis] // 2
        cut[1 - axis] = pl.ds(half * size, size)
    return ref.at[(*lead, *cut)]


def _scaled(shape, axis, num, den=1):
    return tuple(d * num // den if i == axis else d for i, d in enumerate(shape))


def _gather_rider(shards, axes, layer, conv_w=None):
    n = len(shards)
    shard2d = [t.shape[1:] for t in shards]
    full2d = [_scaled(s2, ax, N_CHIPS) for s2, ax in zip(shard2d, axes)]
    out_shapes = [SDS(f2, t.dtype) for f2, t in zip(full2d, shards)]
    inputs = list(shards)
    own = 6
    scratch = [pltpu.SemaphoreType.DMA((n, 7)), pltpu.SemaphoreType.DMA((n, 7))]
    if conv_w is not None:
        cshape = conv_w.shape
        inputs.append(conv_w)
        out_shapes.append(SDS(_scaled(cshape, 2, N_CHIPS), conv_w.dtype))
        scratch += [pltpu.SemaphoreType.DMA((4,)), pltpu.SemaphoreType.DMA((4,))]

    def conv_slab(ref, chip):
        return ref.at[:, :, pl.ds((2 * chip[0] + chip[1]) * cshape[2], cshape[2])]

    def own_copy(ins, outs, send_sems, recv_sems, k):
        x, y, c = _position()
        slab = _sub(outs[k], (), full2d[k], axes[k], chip=2 * x + y)
        return _remote(ins[k].at[layer], slab, send_sems.at[k, own], recv_sems.at[k, own], (x, y, 1 - c))

    def start(ins, outs, scr):
        send_sems, recv_sems = scr[:2]
        x, y, c = _position()
        me = 2 * x + y
        for k in range(n):
            own_copy(ins, outs, send_sems, recv_sems, k).start()
            for j, chip in enumerate(_other_chips(x, y)):
                _remote(
                    _sub(ins[k], (layer,), shard2d[k], axes[k], half=c),
                    _sub(outs[k], (), full2d[k], axes[k], chip=me, half=c),
                    send_sems.at[k, j], recv_sems.at[k, j], (*chip, c),
                ).start()
        if conv_w is not None:
            csend, crecv = scr[2:]
            _remote(ins[n], conv_slab(outs[n], (x, y)), csend.at[3], crecv.at[3], (x, y, 1 - c)).start()
            for j, chip in enumerate(_other_chips(x, y)):
                _remote(ins[n], conv_slab(outs[n], (x, y)), csend.at[j], crecv.at[j], (*chip, c)).start()

    def finish(ins, outs, scr):
        send_sems, recv_sems = scr[:2]
        x, y, c = _position()
        me = 2 * x + y
        sibling = (x, y, 1 - c)
        chips = _other_chips(x, y)
        forwards = []
        for k in range(n):
            src = _sub(ins[k], (layer,), shard2d[k], axes[k], half=c)
            for j, chip in enumerate(chips):
                landed = _sub(outs[k], (), full2d[k], axes[k], chip=2 * chip[0] + chip[1], half=c)
                _remote(src, landed, send_sems.at[k, j], recv_sems.at[k, j], (*chip, c)).wait_recv()
                fwd = _remote(landed, landed, send_sems.at[k, 3 + j], recv_sems.at[k, 3 + j], sibling)
                fwd.start()
                forwards.append(fwd)
        for k in range(n):
            src = _sub(ins[k], (layer,), shard2d[k], axes[k], half=c)
            for j, chip in enumerate(chips):
                theirs = _sub(outs[k], (), full2d[k], axes[k], chip=2 * chip[0] + chip[1], half=1 - c)
                _remote(src, theirs, send_sems.at[k, 3 + j], recv_sems.at[k, 3 + j], sibling).wait_recv()
        for fwd in forwards:
            fwd.wait_send()
        for k in range(n):
            src = _sub(ins[k], (layer,), shard2d[k], axes[k], half=c)
            mine = _sub(outs[k], (), full2d[k], axes[k], chip=me, half=c)
            for j, chip in enumerate(chips):
                _remote(src, mine, send_sems.at[k, j], recv_sems.at[k, j], (*chip, c)).wait_send()
            own_copy(ins, outs, send_sems, recv_sems, k).wait()
        if conv_w is not None:
            csend, crecv = scr[2:]
            for j, chip in enumerate(chips):
                cp = _remote(ins[n], conv_slab(outs[n], chip), csend.at[j], crecv.at[j], (*chip, c))
                cp.wait_recv()
                cp.wait_send()
            _remote(ins[n], conv_slab(outs[n], (x, y)), csend.at[3], crecv.at[3], sibling).wait()

    return _Rider(inputs, out_shapes, scratch, start, finish)


def _pair_rider(stacks, axes, layer):
    n = len(stacks)
    shapes = [t.shape[1:] for t in stacks]
    out_shapes = [SDS(_scaled(s2, 1 - ax, 1, 2), F32) for s2, ax in zip(shapes, axes)]
    scratch = [pltpu.SemaphoreType.DMA((n,)), pltpu.SemaphoreType.DMA((n,))]

    def copies(ins, outs, scr):
        x, y, c = _position()
        return [
            _remote(_sub(ins[k], (layer,), shapes[k], axes[k], half=1 - c), outs[k], scr[0].at[k], scr[1].at[k], (x, y, 1 - c))
            for k in range(n)
        ]

    def start(ins, outs, scr):
        for cp in copies(ins, outs, scr):
            cp.start()

    def finish(ins, outs, scr):
        for cp in copies(ins, outs, scr):
            cp.wait()

    return _Rider(stacks, out_shapes, scratch, start, finish)


def _chip_rider(pairs, axes):
    n = len(pairs)
    shapes = [t.shape for t in pairs]
    out_shapes = [SDS((3,) + _scaled(s2, ax, 1, N_CHIPS), t.dtype) for s2, ax, t in zip(shapes, axes, pairs)]
    scratch = [pltpu.SemaphoreType.DMA((n, 3)), pltpu.SemaphoreType.DMA((n, 3))]

    def copies(ins, outs, scr):
        x, y, c = _position()
        return [
            _remote(
                _sub(ins[k], (), shapes[k], axes[k], chip=2 * chip[0] + chip[1]), outs[k].at[j],
                scr[0].at[k, j], scr[1].at[k, j], (*chip, c),
            )
            for k in range(n)
            for j, chip in enumerate(_other_chips(x, y))
        ]

    def start(ins, outs, scr):
        for cp in copies(ins, outs, scr):
            cp.start()

    def finish(ins, outs, scr):
        for cp in copies(ins, outs, scr):
            cp.wait()

    return _Rider(pairs, out_shapes, scratch, start, finish)


def _complete_rider(stacks, axes, layers):
    n = len(stacks)
    shapes = [t.shape[1:] for t in stacks]
    scratch = [pltpu.SemaphoreType.DMA((n,)), pltpu.SemaphoreType.DMA((n,))]

    def start(ins, outs, scr):
        x, y, c = _position()
        for k in range(n):
            mine = _sub(outs[k], (layers[k],), shapes[k], axes[k], half=c)
            _remote(mine, mine, scr[0].at[k], scr[1].at[k], (x, y, 1 - c)).start()

    def finish(ins, outs, scr):
        x, y, c = _position()
        for k in range(n):
            mine = _sub(outs[k], (layers[k],), shapes[k], axes[k], half=c)
            theirs = _sub(outs[k], (layers[k],), shapes[k], axes[k], half=1 - c)
            _remote(mine, theirs, scr[0].at[k], scr[1].at[k], (x, y, 1 - c)).wait_recv()
            _remote(mine, mine, scr[0].at[k], scr[1].at[k], (x, y, 1 - c)).wait_send()

    return _Rider(stacks, [SDS(t.shape, t.dtype) for t in stacks], scratch, start, finish, {k: k for k in range(n)})


def _small_rider(buf):
    rows, cols = buf.shape
    flips = [(fx, fy, fc) for fx in (0, 1) for fy in (0, 1) for fc in (0, 1)][1:]

    def peers():
        x, y, c = _position()
        return [(x + f[0] - 2 * x * f[0], y + f[1] - 2 * y * f[1], c + f[2] - 2 * c * f[2]) for f in flips]

    def start(ins, outs, scr):
        x, y, c = _position()
        mine = outs[0].at[4 * x + 2 * y + c]
        pltpu.make_async_copy(ins[0], mine, scr[2]).start()
        for j, peer in enumerate(peers()):
            _remote(ins[0], mine, scr[0].at[j], scr[1].at[j], peer).start()

    def finish(ins, outs, scr):
        x, y, c = _position()
        mine = outs[0].at[4 * x + 2 * y + c]
        for j, peer in enumerate(peers()):
            _remote(ins[0], outs[0].at[4 * peer[0] + 2 * peer[1] + peer[2]], scr[0].at[j], scr[1].at[j], peer).wait_recv()
        for j, peer in enumerate(peers()):
            _remote(ins[0], mine, scr[0].at[j], scr[1].at[j], peer).wait_send()
        pltpu.make_async_copy(ins[0], mine, scr[2]).wait()

    scratch = [pltpu.SemaphoreType.DMA((7,)), pltpu.SemaphoreType.DMA((7,)), pltpu.SemaphoreType.DMA(())]
    return _Rider([buf], [SDS((8, rows, cols), buf.dtype)], scratch, start, finish)


def _my_core():
    return lax.axis_index("c")


def _my_chip():
    return 2 * lax.axis_index("x") + lax.axis_index("y")


def _pair_sum(stack, layer, recv, axis, name):
    hk, hn = recv.shape
    tk = _divisor(hk, max(16, TILE_BYTES // (4 * hn)), 16)
    per = hk // tk

    def body(g_ref, r_ref, out_ref):
        out_ref[...] = (g_ref[...] + r_ref[...]).astype(out_ref.dtype)

    if axis == 1:
        mine = pl.BlockSpec((None, tk, hn), lambda i: (layer, _my_core() * per + i, 0))
    else:
        mine = pl.BlockSpec((None, tk, hn), lambda i: (layer, i, _my_core()))
    return pl.pallas_call(
        body,
        name=name,
        grid=(per,),
        in_specs=[mine, pl.BlockSpec((tk, hn), lambda i: (i, 0))],
        out_specs=pl.BlockSpec((tk, hn), lambda i: (i, 0)),
        out_shape=SDS((hk, hn), BF16),
        compiler_params=_params("parallel"),
    )(stack, recv)


def _chip_sum(pair, others, axis, stack, layer, name):
    _, sk, sn = others.shape
    tk = _divisor(sk, max(16, TILE_BYTES // (4 * sn)), 16)
    per = sk // tk

    def body(p_ref, o0_ref, o1_ref, o2_ref, *rest):
        out_ref = rest[-1]
        acc = p_ref[...].astype(F32) + o0_ref[...].astype(F32)
        out_ref[...] = (acc + o1_ref[...].astype(F32)) + o2_ref[...].astype(F32)

    if axis == 1:
        own = pl.BlockSpec((tk, sn), lambda i: (i, _my_chip()))
        out = pl.BlockSpec((None, tk, sn), lambda i: (layer, _my_core() * per + i, 0))
        shard = (2 * sk, sn)
    else:
        own = pl.BlockSpec((tk, sn), lambda i: (_my_chip() * per + i, 0))
        out = pl.BlockSpec((None, tk, sn), lambda i: (layer, i, _my_core()))
        shard = (sk, 2 * sn)
    other = lambda j: pl.BlockSpec((None, tk, sn), lambda i: (j, i, 0))
    in_specs = [own, other(0), other(1), other(2)]
    args = [pair, others, others, others]
    aliases = {}
    if stack is not None:
        in_specs.append(ANY)
        args.append(stack)
        aliases = {4: 0}
    return pl.pallas_call(
        body,
        name=name,
        grid=(per,),
        in_specs=in_specs,
        out_specs=out,
        out_shape=SDS((DEPTH,) + shard, F32),
        input_output_aliases=aliases,
        compiler_params=_params("parallel"),
    )(*args)


def _sum_devices(gathered, name):
    _, rows, cols = gathered.shape

    def body(g_ref, out_ref):
        acc = g_ref[0]
        for dev in range(1, 8):
            acc = acc + g_ref[dev]
        out_ref[...] = acc

    return pl.pallas_call(body, name=name, out_shape=SDS((rows, cols), F32))(gathered)


PACK_TILE = 8 * 128


def _pack(arrays):
    parts = []
    for t in arrays:
        flat = t.reshape(-1)
        pad = (-flat.shape[0]) % PACK_TILE
        if pad:
            flat = jnp.concatenate([flat, jnp.zeros((pad,), flat.dtype)])
        parts.append(flat.reshape(-1, 128))
    return jnp.concatenate(parts, axis=0)


def _unpack(buf, shapes):
    out, row = [], 0
    for shp in shapes:
        size = math.prod(shp)
        rows = -(-size // PACK_TILE) * 8
        out.append(buf[row : row + rows].reshape(-1)[:size].reshape(shp))
        row += rows
    return out


BIG = ("w_in", "w_br_gm", "w_br_hg", "w_out", "w_up", "w_down")
BIG_AXIS = (1, 1, 1, 0, 1, 0)
LAYER_SMALL = ("mix_norm", "gm_ln_g", "gm_ln_b", "gm_ws", "gm_bs", "lb", "hg_norm_g", "ffn_norm", "conv_w", "conv_b")
WEIGHTS = ("mix_norm", "w_in", "gm_ln_g", "gm_ln_b", "gm_ws", "gm_bs", "hg_lb_logits", "hg_norm_g", "w_br_gm", "w_br_hg", "w_out",
           "ffn_norm", "w_up", "conv_w", "conv_b", "w_down", "final_norm")


def kernel(x, mix_norm, w_in, gm_ln_g, gm_ln_b, gm_ws, gm_bs, hg_lb_logits, hg_norm_g, w_br_gm, w_br_hg, w_out, ffn_norm, w_up, conv_w, conv_b, w_down, final_norm, loss_target, m_mix_norm, m_w_in, m_gm_ln_g, m_gm_ln_b, m_gm_ws, m_gm_bs, m_hg_lb_logits, m_hg_norm_g, m_w_br_gm, m_w_br_hg, m_w_out, m_ffn_norm, m_w_up, m_conv_w, m_conv_b, m_w_down, m_final_norm, v_mix_norm, v_w_in, v_gm_ln_g, v_gm_ln_b, v_gm_ws, v_gm_bs, v_hg_lb_logits, v_hg_norm_g, v_w_br_gm, v_w_br_hg, v_w_out, v_ffn_norm, v_w_up, v_conv_w, v_conv_b, v_w_down, v_final_norm):
    w = dict(mix_norm=mix_norm, w_in=w_in, gm_ln_g=gm_ln_g, gm_ln_b=gm_ln_b, gm_ws=gm_ws, gm_bs=gm_bs, hg_lb_logits=hg_lb_logits,
             hg_norm_g=hg_norm_g, w_br_gm=w_br_gm, w_br_hg=w_br_hg, w_out=w_out, ffn_norm=ffn_norm, w_up=w_up, conv_w=conv_w,
             conv_b=conv_b, w_down=w_down, final_norm=final_norm)
    m = dict(mix_norm=m_mix_norm, w_in=m_w_in, gm_ln_g=m_gm_ln_g, gm_ln_b=m_gm_ln_b, gm_ws=m_gm_ws, gm_bs=m_gm_bs,
             hg_lb_logits=m_hg_lb_logits, hg_norm_g=m_hg_norm_g, w_br_gm=m_w_br_gm, w_br_hg=m_w_br_hg, w_out=m_w_out,
             ffn_norm=m_ffn_norm, w_up=m_w_up, conv_w=m_conv_w, conv_b=m_conv_b, w_down=m_w_down, final_norm=m_final_norm)
    v = dict(mix_norm=v_mix_norm, w_in=v_w_in, gm_ln_g=v_gm_ln_g, gm_ln_b=v_gm_ln_b, gm_ws=v_gm_ws, gm_bs=v_gm_bs,
             hg_lb_logits=v_hg_lb_logits, hg_norm_g=v_hg_norm_g, w_br_gm=v_w_br_gm, w_br_hg=v_w_br_hg, w_out=v_w_out,
             ffn_norm=v_ffn_norm, w_up=v_w_up, conv_w=v_conv_w, conv_b=v_conv_b, w_down=v_w_down, final_norm=v_final_norm)

    axes = list(BIG_AXIS)
    d = x.shape[2]

    shards = [w[k].astype(MM_DTYPE) for k in BIG]
    w_in_0, conv_full = _run_rider(_gather_rider(shards[:1], axes[:1], 0, conv_w=conv_w), "gather_weights_0")
    full = [dict(w_in=w_in_0)]
    lb = _lower_bounds(hg_lb_logits)
    xs = x[0]
    saved = []
    mixer = 4
    for l in range(DEPTH):
        fw = full[l]
        more = l + 1 < DEPTH
        bst = gm_bs[l].T
        (z, h, a), got = _in_proj(
            xs, mix_norm[l : l + 1], fw["w_in"], gm_ln_g[l : l + 1], gm_ln_b[l : l + 1], gm_ws[l], bst, f"in_proj_{l}",
            _gather_rider(shards[mixer:], axes[mixer:], l),
        )
        fw.update(zip(BIG[mixer:], got))
        rider = _gather_rider(shards[:1], axes[:1], l + 1) if more else None
        if l == 0:
            rider = _join(rider, _gather_rider(shards[1:mixer], axes[1:mixer], 0))
        (o, b, states), got = _hgrn_fwd(z, lb[l : l + 1], hg_norm_g[l : l + 1], f"hgrn_fwd_{l}", rider)
        if more:
            full.append(dict(w_in=got[0]))
        if l == 0:
            fw.update(zip(BIG[1:mixer], got[1:]))
        rider = _gather_rider(shards[1:mixer], axes[1:mixer], l + 1) if more else None
        (x1,), got = _mix_fwd(xs, a, b, z, fw["w_br_gm"], fw["w_br_hg"], fw["w_out"], f"mix_fwd_{l}", rider)
        if more:
            full[l + 1].update(zip(BIG[1:mixer], got))
        (x2, z2, h2, conv), _ = _ffn_fwd(
            x1, ffn_norm[l : l + 1], fw["w_up"], conv_full, conv_b[l : l + 1], fw["w_down"], l, f"ffn_fwd_{l}"
        )
        saved.append((xs, h, z, a, b, o, states, x1, h2, z2, conv, bst))
        xs = x2

    loss_tile, dx, d_final = _loss_head(xs, final_norm.reshape(1, d), loss_target[0])
    loss = lax.psum(loss_tile[0, 0], ("x", "y", "c"))

    big = {k: None for k in BIG}
    grads = [None] * len(BIG)
    per_layer = [None] * DEPTH
    mix_ids, ffn_ids = list(range(mixer)), list(range(mixer, len(BIG)))
    mix_axes, ffn_axes = axes[:mixer], axes[mixer:]
    mix_pairs = None
    mix_summed = None

    def pair_sums(ids, received, layer):
        return [_pair_sum(big[BIG[k]], layer, r, axes[k], f"pair_sum_{BIG[k]}_{layer}") for k, r in zip(ids, received)]

    def chip_sums(ids, pairs, others, layer):
        for k, p, ot in zip(ids, pairs, others):
            grads[k] = _chip_sum(p, ot, axes[k], grads[k], layer, f"chip_sum_{BIG[k]}_{layer}")

    for l in reversed(range(DEPTH)):
        x0, h, z, a, b, o, states, x1, h2, z2, conv, bst = saved[l]
        fw = full[l]
        dz2, act, dconv, dx1, d_ffn = _ffn_bwd(
            dx, z2, conv, conv_full, fw["w_down"], fw["w_up"], x1, ffn_norm[l : l + 1], l, f"ffn_bwd_{l}"
        )
        big["w_down"] = _matmul_tn(act, dx, big["w_down"], l, f"dw_down_{l}")
        big["w_up"] = _matmul_tn(h2, dz2, big["w_up"], l, f"dw_up_{l}")
        rider = _pair_rider([big[BIG[k]] for k in ffn_ids], ffn_axes, l)
        if mix_pairs is not None:
            rider = _join(rider, _chip_rider(mix_pairs, mix_axes))
        (y, dpa, dpb, da, db, dgates), got = _mix_bwd(
            dx1, a, b, z, fw["w_br_gm"], fw["w_br_hg"], fw["w_out"], f"mix_bwd_{l}", rider
        )
        if mix_pairs is not None:
            chip_sums(mix_ids, mix_pairs, got[len(ffn_ids) :], l + 1)
            mix_summed = l + 1
        ffn_pairs = pair_sums(ffn_ids, got[: len(ffn_ids)], l)
        big["w_out"] = _matmul_tn(y, dx1, big["w_out"], l, f"dw_out_{l}")
        big["w_br_gm"] = _matmul_tn(a, dpa, big["w_br_gm"], l, f"dw_br_gm_{l}")
        big["w_br_hg"] = _matmul_tn(b, dpb, big["w_br_hg"], l, f"dw_br_hg_{l}")
        (dz_hg, d_lb, d_ng), others = _hgrn_bwd(
            db, z, o, states, lb[l : l + 1], hg_norm_g[l : l + 1], f"hgrn_bwd_{l}", _chip_rider(ffn_pairs, ffn_axes)
        )
        chip_sums(ffn_ids, ffn_pairs, others, l)
        dz_gm, d_lng, d_lnb, d_ws, d_bst = _gmlp_bwd(da, z, gm_ln_g[l : l + 1], gm_ln_b[l : l + 1], gm_ws[l], bst, f"gmlp_bwd_{l}")
        n_in = fw["w_in"].shape[1]
        big["w_in"] = _matmul_tn(h, dz_gm, big["w_in"], l, f"dw_in_gm_{l}", n_total=n_in, col_off=0)
        big["w_in"] = _matmul_tn(h, dz_hg, big["w_in"], l, f"dw_in_hg_{l}", n_total=n_in, col_off=2 * BR_DIM)
        big["w_in"] = _matmul_tn(h, dgates, big["w_in"], l, f"dw_in_gate_{l}", n_total=n_in, col_off=6 * BR_DIM)
        done_ids = ffn_ids + (mix_ids if mix_summed is not None else [])
        done_layers = [l] * len(ffn_ids) + ([mix_summed] * len(mix_ids) if mix_summed is not None else [])
        rider = _join(
            _pair_rider([big[BIG[k]] for k in mix_ids], mix_axes, l),
            _complete_rider([grads[k] for k in done_ids], [axes[k] for k in done_ids], done_layers),
        )
        if l == 0:
            upper = [_pack([per_layer[j][k] for k in LAYER_SMALL]) for j in range(1, DEPTH)] + [_pack([d_final[0]])]
            rider = _join(rider, _small_rider(jnp.concatenate(upper, axis=0)))
        (dx, d_mix), got = _in_proj_bwd([dz_gm, dz_hg, dgates], fw["w_in"], x0, mix_norm[l : l + 1], dx1, f"in_proj_bwd_{l}", rider)
        for k, g in zip(done_ids, got[len(mix_ids) : len(mix_ids) + len(done_ids)]):
            grads[k] = g
        gathered_upper = got[len(mix_ids) + len(done_ids) :]
        mix_pairs = pair_sums(mix_ids, got[: len(mix_ids)], l)
        per_layer[l] = dict(
            mix_norm=d_mix[0], gm_ln_g=d_lng[0], gm_ln_b=d_lnb[0], gm_ws=d_ws, gm_bs=d_bst.T, lb=d_lb[0], hg_norm_g=d_ng[0],
            ffn_norm=d_ffn[0], conv_w=dconv[0:3], conv_b=dconv[3],
        )

    got = _run_rider(
        _join(_chip_rider(mix_pairs, mix_axes), _small_rider(_pack([per_layer[0][k] for k in LAYER_SMALL]))), "grad_exchange_0"
    )
    chip_sums(mix_ids, mix_pairs, got[: len(mix_ids)], 0)
    done = _run_rider(_complete_rider([grads[k] for k in mix_ids], mix_axes, [0] * len(mix_ids)), "grad_complete_0")
    for k, g in zip(mix_ids, done):
        grads[k] = g
    grad = dict(zip(BIG, grads))

    layer_shapes = [per_layer[0][k].shape for k in LAYER_SMALL]
    summed_upper = _unpack(_sum_devices(gathered_upper[0], "sum_small_upper"), layer_shapes * (DEPTH - 1) + [d_final[0].shape])
    summed_0 = _unpack(_sum_devices(got[len(mix_ids)], "sum_small_0"), layer_shapes)
    by_layer = [summed_0] + [summed_upper[j * len(LAYER_SMALL) : (j + 1) * len(LAYER_SMALL)] for j in range(DEPTH - 1)]
    small = {k: jnp.stack([by_layer[j][i] for j in range(DEPTH)]) for i, k in enumerate(LAYER_SMALL)}
    for k in LAYER_SMALL:
        if k != "lb":
            grad[k] = small[k]
    grad["hg_lb_logits"] = _lower_bounds_bwd(hg_lb_logits, small["lb"])
    grad["final_norm"] = summed_upper[-1]
    grad["conv_w"] = lax.dynamic_slice_in_dim(grad["conv_w"], _my_chip() * conv_w.shape[2], conv_w.shape[2], axis=2)

    delta, new_m, new_v = {}, {}, {}
    for k in WEIGHTS:
        delta[k], new_m[k], new_v[k] = _adamw(w[k], grad[k], m[k], v[k], f"adamw_{k}")

    return (loss, dx[None], *[grad[k] for k in WEIGHTS], *[delta[k] for k in WEIGHTS], *[new_m[k] for k in WEIGHTS],
            *[new_v[k] for k in WEIGHTS])
```

```python
import math

import jax
import jax.numpy as jnp
from jax import lax
from jax.experimental import pallas as pl
from jax.experimental.pallas import tpu as pltpu

F32 = jnp.float32
BF16 = jnp.bfloat16
MM_DTYPE = BF16

N_HEADS = 4
HEAD_DIM = 128
BR_DIM = N_HEADS * HEAD_DIM
CHUNK = 64
GM_BLOCK = 128
DEPTH = 4
N_CHIPS = 4
EPS = 1e-6
TINY = 1e-30
LB_MAX = 0.999
ADAM_LR = 0.001
ADAM_B1 = 0.9
ADAM_B2 = 0.999
ADAM_EPS = 1e-08
ADAM_WD = 0.01
ADAM_STEP = 10
INV_SQRT2 = 1.0 / math.sqrt(2.0)
INV_SQRT_2PI = 1.0 / math.sqrt(2.0 * math.pi)

VMEM_LIMIT_BYTES = 56 * 1024 * 1024
TILE_BYTES = 2 * 1024 * 1024
ACC_BYTES = 6 * 1024 * 1024
MESH = pl.DeviceIdType.MESH
SDS = jax.ShapeDtypeStruct
ANY = pl.BlockSpec(memory_space=pl.ANY)


def _params(*sem):
    return pltpu.CompilerParams(dimension_semantics=sem or None, vmem_limit_bytes=VMEM_LIMIT_BYTES)


def _divisor(n, limit, mult):
    best = None
    for d in range(mult, min(n, limit) + 1, mult):
        if n % d == 0:
            best = d
    return best if best is not None else n


def _dot(a, b):
    return jnp.dot(a.astype(MM_DTYPE), b.astype(MM_DTYPE), preferred_element_type=F32)


def _dot_nt(a, b):
    return lax.dot_general(a.astype(MM_DTYPE), b.astype(MM_DTYPE), (((1,), (1,)), ((), ())), preferred_element_type=F32)


def _dot_tn(a, b):
    return lax.dot_general(a.astype(MM_DTYPE), b.astype(MM_DTYPE), (((0,), (0,)), ((), ())), preferred_element_type=F32)


def _sigmoid(x):
    return 1.0 / (1.0 + jnp.exp(-x))


def _gelu(x):
    return 0.5 * x * (1.0 + lax.erf(x * INV_SQRT2))


def _gelu_grad(x):
    return 0.5 * (1.0 + lax.erf(x * INV_SQRT2)) + x * jnp.exp(-0.5 * x * x) * INV_SQRT_2PI


def _silu_grad(x, s):
    return s * (1.0 + x * (1.0 - s))


def _resident(shape, index_map):
    return pl.BlockSpec(shape, index_map, pipeline_mode=pl.Buffered(1))


class _Rider:
    def __init__(self, inputs, out_shapes, scratch, start, finish, aliases=None):
        self.inputs = list(inputs)
        self.out_shapes = list(out_shapes)
        self.scratch = list(scratch)
        self.start = start
        self.finish = finish
        self.aliases = dict(aliases or {})


def _join(a, b):
    if a is None or b is None:
        return a if b is None else b
    na, oa, sa = len(a.inputs), len(a.out_shapes), len(a.scratch)

    def start(i, o, s):
        a.start(i[:na], o[:oa], s[:sa])
        b.start(i[na:], o[oa:], s[sa:])

    def finish(i, o, s):
        a.finish(i[:na], o[:oa], s[:sa])
        b.finish(i[na:], o[oa:], s[sa:])

    aliases = dict(a.aliases)
    aliases.update({na + k: oa + v for k, v in b.aliases.items()})
    return _Rider(a.inputs + b.inputs, a.out_shapes + b.out_shapes, a.scratch + b.scratch, start, finish, aliases)


def _hosted_call(body, rider, *, name, grid, in_specs, out_specs, out_shape, scratch_shapes, args, semantics):
    n_in, n_out, n_scr = len(in_specs), len(out_specs), len(scratch_shapes)
    if rider is None:
        outs = pl.pallas_call(
            body, name=name, grid=grid, in_specs=in_specs, out_specs=out_specs, out_shape=out_shape,
            scratch_shapes=scratch_shapes, compiler_params=_params(*semantics),
        )(*args)
        return list(outs), []
    ri, ro = len(rider.inputs), len(rider.out_shapes)

    def wrapped(*refs):
        p = 0
        hin = refs[p : p + n_in]
        p += n_in
        rin = refs[p : p + ri]
        p += ri
        hout = refs[p : p + n_out]
        p += n_out
        rout = refs[p : p + ro]
        p += ro
        hscr = refs[p : p + n_scr]
        rscr = refs[p + n_scr :]

        @pl.when(pl.program_id(0) == 0)
        def _():
            rider.start(rin, rout, rscr)

        body(*hin, *hout, *hscr)

        @pl.when(pl.program_id(0) == grid[0] - 1)
        def _():
            rider.finish(rin, rout, rscr)

    outs = pl.pallas_call(
        wrapped,
        name=name,
        grid=grid,
        in_specs=list(in_specs) + [ANY] * ri,
        out_specs=list(out_specs) + [ANY] * ro,
        out_shape=list(out_shape) + rider.out_shapes,
        scratch_shapes=list(scratch_shapes) + rider.scratch,
        input_output_aliases={n_in + k: n_out + v for k, v in rider.aliases.items()},
        compiler_params=_params(*semantics),
    )(*args, *rider.inputs)
    return list(outs[:n_out]), list(outs[n_out:])


def _run_rider(rider, name):
    ri, ro = len(rider.inputs), len(rider.out_shapes)

    def body(*refs):
        rin, rout, rscr = refs[:ri], refs[ri : ri + ro], refs[ri + ro :]
        rider.start(rin, rout, rscr)
        rider.finish(rin, rout, rscr)

    return list(
        pl.pallas_call(
            body,
            name=name,
            in_specs=[ANY] * ri,
            out_specs=[ANY] * ro,
            out_shape=rider.out_shapes,
            scratch_shapes=rider.scratch,
            input_output_aliases=rider.aliases,
        )(*rider.inputs)
    )


def _lower_bounds(logits):
    def body(lg_ref, lb_ref):
        lg = lg_ref[...]
        mx = jnp.max(lg, axis=0, keepdims=True)
        e = jnp.exp(lg - mx)
        p = e / jnp.sum(e, axis=0, keepdims=True)
        run = p[0:1]
        rows = [run - p[0:1]]
        for l in range(1, DEPTH):
            run = run + p[l : l + 1]
            rows.append(run - p[0:1])
        raw = jnp.concatenate(rows, axis=0)
        lb_ref[...] = jnp.minimum(jnp.maximum(raw, 0.0), LB_MAX)

    return pl.pallas_call(body, name="lower_bounds", out_shape=SDS(logits.shape, F32))(logits)


def _lower_bounds_bwd(logits, dlb):
    def body(lg_ref, dlb_ref, dlg_ref):
        lg = lg_ref[...]
        mx = jnp.max(lg, axis=0, keepdims=True)
        e = jnp.exp(lg - mx)
        p = e / jnp.sum(e, axis=0, keepdims=True)
        run = p[0:1]
        rows = [run - p[0:1]]
        for l in range(1, DEPTH):
            run = run + p[l : l + 1]
            rows.append(run - p[0:1])
        raw = jnp.concatenate(rows, axis=0)
        lo = jnp.where(raw > 0.0, 1.0, jnp.where(raw == 0.0, 0.5, 0.0))
        clipped = jnp.maximum(raw, 0.0)
        hi = jnp.where(clipped < LB_MAX, 1.0, jnp.where(clipped == LB_MAX, 0.5, 0.0))
        draw = dlb_ref[...] * lo * hi
        total = jnp.sum(draw, axis=0, keepdims=True)
        dps = []
        tail = total
        for j in range(DEPTH):
            dps.append(tail - total if j == 0 else tail)
            tail = tail - draw[j : j + 1]
        dp = jnp.concatenate(dps, axis=0)
        dlg_ref[...] = p * (dp - jnp.sum(p * dp, axis=0, keepdims=True))

    return pl.pallas_call(body, name="lower_bounds_bwd", out_shape=SDS(logits.shape, F32))(logits, dlb)


def _in_proj(x, gain, w, lng, lnb, ws, bst, name, rider=None):
    s, d = x.shape
    n = w.shape[1]
    tm = _divisor(s, 512, GM_BLOCK)

    def body(x_ref, g_ref, w_ref, lng_ref, lnb_ref, ws_ref, bst_ref, z_ref, h_ref, a_ref):
        xv = x_ref[...]
        r = lax.rsqrt(jnp.mean(xv * xv, axis=-1, keepdims=True) + EPS)
        h = ((xv * r) * g_ref[...]).astype(MM_DTYPE)
        h_ref[...] = h
        uv = jnp.dot(h, w_ref[:, : 2 * BR_DIM], preferred_element_type=F32)
        z_ref[:, : 2 * BR_DIM] = uv
        z_ref[:, 2 * BR_DIM :] = jnp.dot(h, w_ref[:, 2 * BR_DIM :], preferred_element_type=F32)
        _gmlp_fwd_tile(uv[:, :BR_DIM], uv[:, BR_DIM:], lng_ref, lnb_ref, ws_ref, bst_ref, a_ref)

    row = lambda wd: pl.BlockSpec((tm, wd), lambda i: (i, 0))
    return _hosted_call(
        body,
        rider,
        name=name,
        grid=(s // tm,),
        in_specs=[row(d), _resident((1, d), lambda i: (0, 0)), _resident((d, n), lambda i: (0, 0))] + _gmlp_param_specs(),
        out_specs=[row(n), row(d), row(BR_DIM)],
        out_shape=[SDS((s, n), F32), SDS((s, d), MM_DTYPE), SDS((s, BR_DIM), MM_DTYPE)],
        scratch_shapes=[],
        args=[x, gain, w, lng, lnb, ws, bst],
        semantics=("arbitrary",),
    )


def _in_proj_bwd(dz_parts, w, x, gain, dres, name, rider=None):
    s, d = x.shape
    n = w.shape[1]
    tm = _divisor(s, 512, 16)
    widths = [p.shape[1] for p in dz_parts]
    offsets = [sum(widths[:k]) for k in range(len(widths))]
    n_parts = len(dz_parts)

    def body(*refs):
        dz_refs = refs[:n_parts]
        w_ref, x_ref, g_ref, dres_ref, dx_ref, dg_ref = refs[n_parts:]

        @pl.when(pl.program_id(0) == 0)
        def _():
            dg_ref[...] = jnp.zeros_like(dg_ref)

        dh = None
        for dz_ref, off, wd in zip(dz_refs, offsets, widths):
            part = _dot_nt(dz_ref[...], w_ref[:, off : off + wd])
            dh = part if dh is None else dh + part
        xv = x_ref[...]
        r = lax.rsqrt(jnp.mean(xv * xv, axis=-1, keepdims=True) + EPS)
        xh = xv * r
        dg_ref[...] += jnp.sum(dh * xh, axis=0, keepdims=True)
        dxh = dh * g_ref[...]
        dx_ref[...] = dres_ref[...] + r * (dxh - xh * jnp.mean(dxh * xh, axis=-1, keepdims=True))

    in_specs = [pl.BlockSpec((tm, wd), lambda i: (i, 0)) for wd in widths] + [
        _resident((d, n), lambda i: (0, 0)),
        pl.BlockSpec((tm, d), lambda i: (i, 0)),
        _resident((1, d), lambda i: (0, 0)),
        pl.BlockSpec((tm, d), lambda i: (i, 0)),
    ]
    return _hosted_call(
        body,
        rider,
        name=name,
        grid=(s // tm,),
        in_specs=in_specs,
        out_specs=[pl.BlockSpec((tm, d), lambda i: (i, 0)), pl.BlockSpec((1, d), lambda i: (0, 0))],
        out_shape=[SDS((s, d), F32), SDS((1, d), F32)],
        scratch_shapes=[],
        args=[*dz_parts, w, x, gain, dres],
        semantics=("arbitrary",),
    )


def _matmul_tn(a, b, stack, layer, name, n_total=None, col_off=0):
    s, k = a.shape
    n = b.shape[1]
    n_total = n if n_total is None else n_total
    tn = _divisor(math.gcd(n, col_off) if col_off else n, max(128, ACC_BYTES // (4 * k)), 128)
    ts = _divisor(s, min(2048, max(512, ACC_BYTES // (2 * k))), 16)
    off = col_off // tn

    def body(a_ref, b_ref, *rest):
        out_ref = rest[-1]

        @pl.when(pl.program_id(1) == 0)
        def _():
            out_ref[...] = jnp.zeros_like(out_ref)

        out_ref[...] += _dot_tn(a_ref[...], b_ref[...])

    in_specs = [pl.BlockSpec((ts, k), lambda j, i: (i, 0)), pl.BlockSpec((ts, tn), lambda j, i: (i, j))]
    args = [a, b]
    aliases = {}
    if stack is not None:
        in_specs.append(ANY)
        args.append(stack)
        aliases = {2: 0}
    return pl.pallas_call(
        body,
        name=name,
        grid=(n // tn, s // ts),
        in_specs=in_specs,
        out_specs=pl.BlockSpec((None, k, tn), lambda j, i: (layer, 0, off + j)),
        out_shape=SDS((DEPTH, k, n_total), F32),
        input_output_aliases=aliases,
        compiler_params=_params("parallel", "arbitrary"),
    )(*args)


def _gm_mask():
    r = lax.broadcasted_iota(jnp.int32, (GM_BLOCK, GM_BLOCK), 0) // CHUNK
    c = lax.broadcasted_iota(jnp.int32, (GM_BLOCK, GM_BLOCK), 1) // CHUNK
    return r >= c


def _gm_normed(v, lng, lnb):
    vg = _gelu(v)
    mu = jnp.mean(vg, axis=-1, keepdims=True)
    xc = vg - mu
    rstd = lax.rsqrt(jnp.mean(xc * xc, axis=-1, keepdims=True) + EPS)
    xh = xc * rstd
    return xh, rstd, xh * lng + lnb


def _gmlp_param_specs():
    return [
        _resident((1, BR_DIM), lambda i: (0, 0)),
        _resident((1, BR_DIM), lambda i: (0, 0)),
        _resident((N_HEADS, GM_BLOCK, GM_BLOCK), lambda i: (0, 0, 0)),
        _resident((GM_BLOCK, N_HEADS), lambda i: (0, 0)),
    ]


def _gmlp_fwd_tile(u, v, lng_ref, lnb_ref, ws_ref, bst_ref, a_ref):
    mask = _gm_mask()
    ug = _gelu(u)
    _, _, vn = _gm_normed(v, lng_ref[...], lnb_ref[...])
    vn = vn.astype(MM_DTYPE)
    for h in range(N_HEADS):
        cs = slice(h * HEAD_DIM, (h + 1) * HEAD_DIM)
        wm = jnp.where(mask, ws_ref[h], 0.0).astype(MM_DTYPE)
        for blk in range(u.shape[0] // GM_BLOCK):
            rs = slice(blk * GM_BLOCK, (blk + 1) * GM_BLOCK)
            mixed = _dot(wm, vn[rs, cs]) + bst_ref[:, h : h + 1]
            a_ref[rs, cs] = (ug[rs, cs] * mixed).astype(MM_DTYPE)


def _gmlp_bwd_tile(da_v, u, v, lng_ref, lnb_ref, ws_ref, bst_ref, dz_ref, dlng_ref, dlnb_ref, dws_ref, dbst_ref):
    mask = _gm_mask()
    ug = _gelu(u)
    lng_v = lng_ref[...]
    xh, rstd, vn = _gm_normed(v, lng_v, lnb_ref[...])
    vnb = vn.astype(MM_DTYPE)
    dmix = da_v * ug
    dmixb = dmix.astype(MM_DTYPE)
    dvn_cols = []
    for h in range(N_HEADS):
        cs = slice(h * HEAD_DIM, (h + 1) * HEAD_DIM)
        wm = jnp.where(mask, ws_ref[h], 0.0).astype(MM_DTYPE)
        dw = jnp.zeros((GM_BLOCK, GM_BLOCK), F32)
        dbias = jnp.zeros((GM_BLOCK, 1), F32)
        dvn_rows = []
        for blk in range(u.shape[0] // GM_BLOCK):
            rs = slice(blk * GM_BLOCK, (blk + 1) * GM_BLOCK)
            mixed = _dot(wm, vnb[rs, cs]) + bst_ref[:, h : h + 1]
            dz_ref[rs, cs] = (da_v[rs, cs] * mixed * _gelu_grad(u[rs, cs])).astype(MM_DTYPE)
            dw = dw + _dot_nt(dmixb[rs, cs], vnb[rs, cs])
            dbias = dbias + jnp.sum(dmix[rs, cs], axis=1, keepdims=True)
            dvn_rows.append(_dot_tn(wm, dmixb[rs, cs]))
        dws_ref[h] += jnp.where(mask, dw, 0.0)
        dbst_ref[:, h : h + 1] += dbias
        dvn_cols.append(jnp.concatenate(dvn_rows, axis=0) if len(dvn_rows) > 1 else dvn_rows[0])
    dvn = jnp.concatenate(dvn_cols, axis=1)
    dlng_ref[...] += jnp.sum(dvn * xh, axis=0, keepdims=True)
    dlnb_ref[...] += jnp.sum(dvn, axis=0, keepdims=True)
    dxh = dvn * lng_v
    dvg = rstd * (dxh - jnp.mean(dxh, axis=-1, keepdims=True) - xh * jnp.mean(dxh * xh, axis=-1, keepdims=True))
    dz_ref[:, BR_DIM:] = (dvg * _gelu_grad(v)).astype(MM_DTYPE)


HG_ROWS = 512
HG_UNROLL = 8
MID = CHUNK // 2 - 1


def _tri(lower):
    r = lax.broadcasted_iota(jnp.int32, (CHUNK, CHUNK), 0)
    c = lax.broadcasted_iota(jnp.int32, (CHUNK, CHUNK), 1)
    return r >= c if lower else c >= r


def _scan_rows(x, reverse=False):
    n = x.shape[0]
    rid = lax.broadcasted_iota(jnp.int32, x.shape, 0)
    k = 1
    while k < n:
        if reverse:
            x = x + jnp.where(rid < n - k, pltpu.roll(x, n - k, 0), 0.0)
        else:
            x = x + jnp.where(rid >= k, pltpu.roll(x, k, 0), 0.0)
        k *= 2
    return x


def _hgrn_fwd_chunk(zqf_ref, zio_ref, ci, lb_ref, ng_v, state, o_ref, b_ref, st_ref):
    low = _tri(True)
    rows = slice(ci * CHUNK, (ci + 1) * CHUNK)
    for h in range(N_HEADS):
        cs = slice(h * HEAD_DIM, (h + 1) * HEAD_DIM)
        cs2 = slice(BR_DIM + h * HEAD_DIM, BR_DIM + (h + 1) * HEAD_DIM)
        zq = zqf_ref[rows, cs]
        zf = zqf_ref[rows, cs2]
        vi = zio_ref[rows, cs]
        zog = zio_ref[rows, cs2]
        lbh = lb_ref[:, cs]
        qf = zq * _sigmoid(zq)
        forget = lbh + (1.0 - lbh) * _sigmoid(zf)
        g = jnp.log(jnp.maximum(forget, TINY))
        k = (1.0 - lbh) * _sigmoid(-zf)
        gc = _scan_rows(g)
        gm = gc[MID : MID + 1]
        gl = gc[CHUNK - 1 : CHUNK]
        a = jnp.where(low, _dot_nt(qf * jnp.exp(gc - gm), k * jnp.exp(gm - gc)), 0.0)
        st = state[h]
        st_ref[ci, h] = st
        o = _dot(a, vi) + _dot_nt(qf * jnp.exp(gc), st)
        state[h] = st * jnp.exp(gl) + _dot_tn(vi, k * jnp.exp(gl - gc))
        r = lax.rsqrt(jnp.mean(o * o, axis=-1, keepdims=True) + EPS)
        o_ref[rows, cs] = o
        b_ref[rows, cs] = (((o * r) * ng_v) * (zog * _sigmoid(zog))).astype(MM_DTYPE)


def _hgrn_fwd(z, lb, ng, name, rider=None):
    s = z.shape[0]
    rows_per = _divisor(s, HG_ROWS, CHUNK)
    n_sub = rows_per // CHUNK

    def body(zqf_ref, zio_ref, lb_ref, ng_ref, o_ref, b_ref, st_ref, state):
        @pl.when(pl.program_id(0) == 0)
        def _():
            state[...] = jnp.zeros_like(state)

        ng_v = ng_ref[...]
        for ci in range(n_sub):
            _hgrn_fwd_chunk(zqf_ref, zio_ref, ci, lb_ref, ng_v, state, o_ref, b_ref, st_ref)

    return _hosted_call(
        body,
        rider,
        name=name,
        grid=(s // rows_per,),
        in_specs=[
            pl.BlockSpec((rows_per, 2 * BR_DIM), lambda i: (i, 1)),
            pl.BlockSpec((rows_per, 2 * BR_DIM), lambda i: (i, 2)),
            _resident((1, BR_DIM), lambda i: (0, 0)),
            _resident((1, HEAD_DIM), lambda i: (0, 0)),
        ],
        out_specs=[
            pl.BlockSpec((rows_per, BR_DIM), lambda i: (i, 0)),
            pl.BlockSpec((rows_per, BR_DIM), lambda i: (i, 0)),
            pl.BlockSpec((n_sub, N_HEADS, HEAD_DIM, HEAD_DIM), lambda i: (i, 0, 0, 0)),
        ],
        out_shape=[
            SDS((s, BR_DIM), F32),
            SDS((s, BR_DIM), MM_DTYPE),
            SDS((s // CHUNK, N_HEADS, HEAD_DIM, HEAD_DIM), F32),
        ],
        scratch_shapes=[pltpu.VMEM((N_HEADS, HEAD_DIM, HEAD_DIM), F32)],
        args=[z, z, lb, ng],
        semantics=("arbitrary",),
    )


def _hgrn_bwd(db, z, o, states, lb, ng, name, rider=None):
    s = z.shape[0]
    rows_per = _divisor(s, HG_ROWS, CHUNK)
    n_sub = rows_per // CHUNK
    n_steps = s // rows_per

    def body(db_ref, zqf_ref, zio_ref, o_ref, st_ref, lb_ref, ng_ref, dz_ref, dlb_ref, dng_ref, dstate):
        @pl.when(pl.program_id(0) == 0)
        def _():
            dstate[...] = jnp.zeros_like(dstate)
            dlb_ref[...] = jnp.zeros_like(dlb_ref)
            dng_ref[...] = jnp.zeros_like(dng_ref)

        low = _tri(True)
        ng_v = ng_ref[...]

        def chunk(cc, carry):
            ci = n_sub - 1 - cc
            rows = pl.ds(pl.multiple_of(ci * CHUNK, CHUNK), CHUNK)
            for h in range(N_HEADS):
                cs = slice(h * HEAD_DIM, (h + 1) * HEAD_DIM)
                cs2 = slice(BR_DIM + h * HEAD_DIM, BR_DIM + (h + 1) * HEAD_DIM)
                zq = zqf_ref[rows, cs]
                zf = zqf_ref[rows, cs2]
                vi = zio_ref[rows, cs]
                zog = zio_ref[rows, cs2]
                lbh = lb_ref[:, cs]
                ov = o_ref[rows, cs]
                dbv = db_ref[rows, cs]
                r = lax.rsqrt(jnp.mean(ov * ov, axis=-1, keepdims=True) + EPS)
                on = ov * r
                sgo = _sigmoid(zog)
                gate = zog * sgo
                dng_ref[...] += jnp.sum(dbv * gate * on, axis=0, keepdims=True)
                don = dbv * gate * ng_v
                dzog = dbv * (on * ng_v) * _silu_grad(zog, sgo)
                do = r * (don - on * jnp.mean(don * on, axis=-1, keepdims=True))
                sq = _sigmoid(zq)
                qf = zq * sq
                sg = _sigmoid(zf)
                sgn = _sigmoid(-zf)
                oml = 1.0 - lbh
                forget = lbh + oml * sg
                fm = jnp.maximum(forget, TINY)
                k = oml * sgn
                gc = _scan_rows(jnp.log(fm))
                gm = gc[MID : MID + 1]
                gl = gc[CHUNK - 1 : CHUNK]
                e_g = jnp.exp(gc)
                e_q = jnp.exp(gc - gm)
                e_k = jnp.exp(gm - gc)
                e_kd = jnp.exp(gl - gc)
                e_gl = jnp.exp(gl)
                qt = qf * e_q
                kt = k * e_k
                a = jnp.where(low, _dot_nt(qt, kt), 0.0)
                st = st_ref[ci, h]
                dst = dstate[h]
                dp = jnp.where(low, _dot_nt(do, vi), 0.0)
                dv = _dot_tn(a, do) + _dot_nt(k * e_kd, dst)
                dq = _dot(dp, kt) * e_q + e_g * _dot(do, st)
                dks = e_kd * _dot(vi, dst)
                dk = _dot_tn(dp, qt) * e_k + dks
                dstate[h] = _dot_tn(do, qf * e_g) + dst * e_gl
                dgc = qf * dq - k * dk
                extra = e_gl * jnp.sum(st * dst, axis=0, keepdims=True) + jnp.sum(k * dks, axis=0, keepdims=True)
                dg = _scan_rows(dgc, reverse=True) + extra
                dforget = jnp.where(forget > TINY, dg / fm, 0.0)
                both = dforget - dk
                dlb_ref[:, cs] += jnp.sum(sgn * both, axis=0, keepdims=True)
                dz_ref[rows, cs] = (dq * _silu_grad(zq, sq)).astype(MM_DTYPE)
                dz_ref[rows, cs2] = (oml * sg * sgn * both).astype(MM_DTYPE)
                dz_ref[rows, 2 * BR_DIM + h * HEAD_DIM : 2 * BR_DIM + (h + 1) * HEAD_DIM] = dv.astype(MM_DTYPE)
                dz_ref[rows, 3 * BR_DIM + h * HEAD_DIM : 3 * BR_DIM + (h + 1) * HEAD_DIM] = dzog.astype(MM_DTYPE)
            return carry

        lax.fori_loop(0, n_sub, chunk, 0, unroll=math.gcd(n_sub, HG_UNROLL))

    rev = lambda i: n_steps - 1 - i
    return _hosted_call(
        body,
        rider,
        name=name,
        grid=(n_steps,),
        in_specs=[
            pl.BlockSpec((rows_per, BR_DIM), lambda i: (rev(i), 0)),
            pl.BlockSpec((rows_per, 2 * BR_DIM), lambda i: (rev(i), 1)),
            pl.BlockSpec((rows_per, 2 * BR_DIM), lambda i: (rev(i), 2)),
            pl.BlockSpec((rows_per, BR_DIM), lambda i: (rev(i), 0)),
            pl.BlockSpec((n_sub, N_HEADS, HEAD_DIM, HEAD_DIM), lambda i: (rev(i), 0, 0, 0)),
            _resident((1, BR_DIM), lambda i: (0, 0)),
            _resident((1, HEAD_DIM), lambda i: (0, 0)),
        ],
        out_specs=[
            pl.BlockSpec((rows_per, 4 * BR_DIM), lambda i: (rev(i), 0)),
            pl.BlockSpec((1, BR_DIM), lambda i: (0, 0)),
            pl.BlockSpec((1, HEAD_DIM), lambda i: (0, 0)),
        ],
        out_shape=[SDS((s, 4 * BR_DIM), MM_DTYPE), SDS((1, BR_DIM), F32), SDS((1, HEAD_DIM), F32)],
        scratch_shapes=[pltpu.VMEM((N_HEADS, HEAD_DIM, HEAD_DIM), F32)],
        args=[db, z, z, o, states, lb, ng],
        semantics=("arbitrary",),
    )


def _mix_fwd(x, a, b, z, w_gm, w_hg, w_out, name, rider=None):
    s, d = x.shape
    tm = _divisor(s, 512, 16)
    g0 = 6 * BR_DIM // d

    def body(x_ref, a_ref, b_ref, ga_ref, gb_ref, wgm_ref, whg_ref, wout_ref, out_ref):
        y = _sigmoid(ga_ref[...]) * _dot(a_ref[...], wgm_ref[...]) + _sigmoid(gb_ref[...]) * _dot(b_ref[...], whg_ref[...])
        out_ref[...] = x_ref[...] + _dot(y, wout_ref[...])

    return _hosted_call(
        body,
        rider,
        name=name,
        grid=(s // tm,),
        in_specs=[
            pl.BlockSpec((tm, d), lambda i: (i, 0)),
            pl.BlockSpec((tm, BR_DIM), lambda i: (i, 0)),
            pl.BlockSpec((tm, BR_DIM), lambda i: (i, 0)),
            pl.BlockSpec((tm, d), lambda i: (i, g0)),
            pl.BlockSpec((tm, d), lambda i: (i, g0 + 1)),
            _resident((BR_DIM, d), lambda i: (0, 0)),
            _resident((BR_DIM, d), lambda i: (0, 0)),
            _resident((d, d), lambda i: (0, 0)),
        ],
        out_specs=[pl.BlockSpec((tm, d), lambda i: (i, 0))],
        out_shape=[SDS((s, d), F32)],
        scratch_shapes=[],
        args=[x, a, b, z, z, w_gm, w_hg, w_out],
        semantics=("arbitrary",),
    )


def _mix_bwd(dx, a, b, z, w_gm, w_hg, w_out, lng, lnb, ws, bst, name, rider=None):
    s, d = dx.shape
    tm = _divisor(s, 512, GM_BLOCK)
    g0 = 6 * BR_DIM // d

    def body(dx_ref, a_ref, b_ref, uv_ref, ga_ref, gb_ref, wgm_ref, whg_ref, wout_ref, lng_ref, lnb_ref, ws_ref, bst_ref,
             y_ref, dpa_ref, dpb_ref, db_ref, dg_ref, duv_ref, dlng_ref, dlnb_ref, dws_ref, dbst_ref):
        @pl.when(pl.program_id(0) == 0)
        def _():
            for ref in (dlng_ref, dlnb_ref, dws_ref, dbst_ref):
                ref[...] = jnp.zeros_like(ref)

        dy = _dot_nt(dx_ref[...], wout_ref[...])
        pa = _dot(a_ref[...], wgm_ref[...])
        pb = _dot(b_ref[...], whg_ref[...])
        sa = _sigmoid(ga_ref[...])
        sb = _sigmoid(gb_ref[...])
        y_ref[...] = (sa * pa + sb * pb).astype(MM_DTYPE)
        dpa = (dy * sa).astype(MM_DTYPE)
        dpb = (dy * sb).astype(MM_DTYPE)
        dpa_ref[...] = dpa
        dpb_ref[...] = dpb
        da = _dot_nt(dpa, wgm_ref[...])
        db_ref[...] = _dot_nt(dpb, whg_ref[...])
        dg_ref[:, :d] = (dy * pa * sa * (1.0 - sa)).astype(MM_DTYPE)
        dg_ref[:, d:] = (dy * pb * sb * (1.0 - sb)).astype(MM_DTYPE)
        _gmlp_bwd_tile(da, uv_ref[:, :BR_DIM], uv_ref[:, BR_DIM:], lng_ref, lnb_ref, ws_ref, bst_ref,
                       duv_ref, dlng_ref, dlnb_ref, dws_ref, dbst_ref)

    row = lambda w: pl.BlockSpec((tm, w), lambda i: (i, 0))
    fixed = lambda shape: pl.BlockSpec(shape, lambda i: tuple(0 for _ in shape))
    return _hosted_call(
        body,
        rider,
        name=name,
        grid=(s // tm,),
        in_specs=[
            row(d),
            row(BR_DIM),
            row(BR_DIM),
            row(2 * BR_DIM),
            pl.BlockSpec((tm, d), lambda i: (i, g0)),
            pl.BlockSpec((tm, d), lambda i: (i, g0 + 1)),
            _resident((BR_DIM, d), lambda i: (0, 0)),
            _resident((BR_DIM, d), lambda i: (0, 0)),
            _resident((d, d), lambda i: (0, 0)),
        ]
        + _gmlp_param_specs(),
        out_specs=[row(d), row(d), row(d), row(BR_DIM), row(2 * d), row(2 * BR_DIM), fixed((1, BR_DIM)), fixed((1, BR_DIM)),
                   fixed((N_HEADS, GM_BLOCK, GM_BLOCK)), fixed((GM_BLOCK, N_HEADS))],
        out_shape=[
            SDS((s, d), MM_DTYPE),
            SDS((s, d), MM_DTYPE),
            SDS((s, d), MM_DTYPE),
            SDS((s, BR_DIM), F32),
            SDS((s, 2 * d), MM_DTYPE),
            SDS((s, 2 * BR_DIM), MM_DTYPE),
            SDS((1, BR_DIM), F32),
            SDS((1, BR_DIM), F32),
            SDS((N_HEADS, GM_BLOCK, GM_BLOCK), F32),
            SDS((GM_BLOCK, N_HEADS), F32),
        ],
        scratch_shapes=[],
        args=[dx, a, b, z, z, z, w_gm, w_hg, w_out, lng, lnb, ws, bst],
        semantics=("arbitrary",),
    )


HALO = 8
FFN_ROWS = 256


def _shift_down(v, prev, n):
    rolled = pltpu.roll(v, n, 0)
    rid = lax.broadcasted_iota(jnp.int32, v.shape, 0)
    out = rolled
    for r in range(n):
        out = jnp.where(rid == r, prev[HALO - n + r : HALO - n + r + 1], out)
    return out


def _shift_up(v, nxt, n):
    tm = v.shape[0]
    rolled = pltpu.roll(v, tm - n, 0)
    rid = lax.broadcasted_iota(jnp.int32, v.shape, 0)
    out = rolled
    for r in range(n):
        out = jnp.where(rid == tm - n + r, nxt[r : r + 1], out)
    return out


def _conv_cols(f):
    return _divisor(f, 256, 128)


def _ffn_fwd(x, gain, w_up, cw, cb, w_down, layer, name, rider=None):
    s, d = x.shape
    f = w_down.shape[0]
    tm = _divisor(s, FFN_ROWS, 16)
    cwid = _conv_cols(f)

    def body(x_ref, g_ref, wu_ref, cw_ref, cb_ref, wd_ref, out_ref, z_ref, h_ref, conv_ref, halo):
        @pl.when(pl.program_id(0) == 0)
        def _():
            halo[...] = jnp.zeros_like(halo)

        xv = x_ref[...]
        r = lax.rsqrt(jnp.mean(xv * xv, axis=-1, keepdims=True) + EPS)
        h = ((xv * r) * g_ref[...]).astype(MM_DTYPE)
        h_ref[...] = h
        acc = xv
        starts = list(range(0, f, cwid))

        def project(c0):
            return [jnp.dot(h, wu_ref[:, base : base + cwid], preferred_element_type=F32) for base in (c0, f + c0)]

        ahead = project(starts[0])
        for n, c0 in enumerate(starts):
            halves = []
            current = ahead
            if n + 1 < len(starts):
                ahead = project(starts[n + 1])
            for base, zc in zip((c0, f + c0), current):
                cols = slice(base, base + cwid)
                z_ref[:, cols] = zc.astype(MM_DTYPE)
                prev = halo[:, cols]
                conv = cb_ref[:, cols] + cw_ref[0:1, cols] * _shift_down(zc, prev, 2)
                conv = conv + cw_ref[1:2, cols] * _shift_down(zc, prev, 1) + cw_ref[2:3, cols] * zc
                halo[:, cols] = zc[tm - HALO : tm]
                conv_ref[:, cols] = conv.astype(MM_DTYPE)
                halves.append(conv)
            gate, val = halves
            act = gate * _sigmoid(gate) * val
            acc = acc + _dot(act, wd_ref[c0 : c0 + cwid, :])
        out_ref[...] = acc

    row = lambda w: pl.BlockSpec((tm, w), lambda i: (i, 0))
    return _hosted_call(
        body,
        rider,
        name=name,
        grid=(s // tm,),
        in_specs=[
            row(d),
            _resident((1, d), lambda i: (0, 0)),
            _resident((d, 2 * f), lambda i: (0, 0)),
            _resident((None, 3, 2 * f), lambda i: (layer, 0, 0)),
            _resident((1, 2 * f), lambda i: (0, 0)),
            _resident((f, d), lambda i: (0, 0)),
        ],
        out_specs=[row(d), row(2 * f), row(d), row(2 * f)],
        out_shape=[SDS((s, d), F32), SDS((s, 2 * f), MM_DTYPE), SDS((s, d), MM_DTYPE), SDS((s, 2 * f), MM_DTYPE)],
        scratch_shapes=[pltpu.VMEM((HALO, 2 * f), F32)],
        args=[x, gain, w_up, cw, cb, w_down],
        semantics=("arbitrary",),
    )


def _ffn_bwd(dx, z2, conv, cw, w_down, w_up, x, gain, layer, name):
    s, d = dx.shape
    f = z2.shape[1] // 2
    tm = _divisor(s, 256, 16)
    cwid = _conv_cols(f)
    n_steps = s // tm

    def body(dx_ref, z_ref, conv_ref, cw_ref, wd_ref, wu_ref, x_ref, g_ref, dz_ref, act_ref, dcw_ref, dx1_ref, dg_ref, nxt):
        @pl.when(pl.program_id(0) == 0)
        def _():
            nxt[...] = jnp.zeros_like(nxt)
            dcw_ref[...] = jnp.zeros_like(dcw_ref)
            dg_ref[...] = jnp.zeros_like(dg_ref)

        dxf = dx_ref[...]
        dxv = dxf.astype(MM_DTYPE)
        starts = list(range(0, f, cwid))
        ahead = _dot_nt(dxv, wd_ref[0:cwid, :])
        dh = jnp.zeros((tm, d), F32)
        pending = []
        for n, c0 in enumerate(starts):
            dact = ahead
            if n + 1 < len(starts):
                ahead = _dot_nt(dxv, wd_ref[starts[n + 1] : starts[n + 1] + cwid, :])
            for dz_prev, cols_prev in pending:
                dh = dh + _dot_nt(dz_prev, wu_ref[:, cols_prev])
            pending = []
            gate = conv_ref[:, c0 : c0 + cwid].astype(F32)
            val = conv_ref[:, f + c0 : f + c0 + cwid].astype(F32)
            sg = _sigmoid(gate)
            silu = gate * sg
            act_ref[:, c0 : c0 + cwid] = (silu * val).astype(MM_DTYPE)
            dconvs = (dact * val * _silu_grad(gate, sg), dact * silu)
            for base, dconv in zip((c0, f + c0), dconvs):
                cols = slice(base, base + cwid)
                zc = z_ref[:, cols].astype(F32)
                nx = nxt[:, cols]
                up1 = _shift_up(dconv, nx, 1)
                up2 = _shift_up(dconv, nx, 2)
                dcw_ref[0:1, cols] += jnp.sum(up2 * zc, axis=0, keepdims=True)
                dcw_ref[1:2, cols] += jnp.sum(up1 * zc, axis=0, keepdims=True)
                dcw_ref[2:3, cols] += jnp.sum(dconv * zc, axis=0, keepdims=True)
                dcw_ref[3:4, cols] += jnp.sum(dconv, axis=0, keepdims=True)
                dz = (cw_ref[2:3, cols] * dconv + cw_ref[1:2, cols] * up1 + cw_ref[0:1, cols] * up2).astype(MM_DTYPE)
                dz_ref[:, cols] = dz
                nxt[:, cols] = dconv[0:HALO]
                pending.append((dz, cols))
        for dz_prev, cols_prev in pending:
            dh = dh + _dot_nt(dz_prev, wu_ref[:, cols_prev])
        xv = x_ref[...]
        r = lax.rsqrt(jnp.mean(xv * xv, axis=-1, keepdims=True) + EPS)
        xh = xv * r
        dg_ref[...] += jnp.sum(dh * xh, axis=0, keepdims=True)
        dxh = dh * g_ref[...]
        dx1_ref[...] = dxf + r * (dxh - xh * jnp.mean(dxh * xh, axis=-1, keepdims=True))

    rev = lambda i: n_steps - 1 - i
    row = lambda w: pl.BlockSpec((tm, w), lambda i: (rev(i), 0))
    return pl.pallas_call(
        body,
        name=name,
        grid=(n_steps,),
        in_specs=[
            row(d),
            row(2 * f),
            row(2 * f),
            _resident((None, 3, 2 * f), lambda i: (layer, 0, 0)),
            _resident((f, d), lambda i: (0, 0)),
            _resident((d, 2 * f), lambda i: (0, 0)),
            row(d),
            _resident((1, d), lambda i: (0, 0)),
        ],
        out_specs=[
            row(2 * f),
            row(f),
            pl.BlockSpec((HALO, 2 * f), lambda i: (0, 0)),
            row(d),
            pl.BlockSpec((1, d), lambda i: (0, 0)),
        ],
        out_shape=[SDS((s, 2 * f), MM_DTYPE), SDS((s, f), MM_DTYPE), SDS((HALO, 2 * f), F32), SDS((s, d), F32), SDS((1, d), F32)],
        scratch_shapes=[pltpu.VMEM((HALO, 2 * f), F32)],
        compiler_params=_params("arbitrary"),
    )(dx, z2, conv, cw, w_down, w_up, x, gain)


def _loss_head(x, gain, target):
    s, d = x.shape
    tm = _divisor(s, 256, 8)

    def body(x_ref, g_ref, t_ref, loss_ref, dx_ref, dg_ref):
        @pl.when(pl.program_id(0) == 0)
        def _():
            loss_ref[...] = jnp.zeros_like(loss_ref)
            dg_ref[...] = jnp.zeros_like(dg_ref)

        xv = x_ref[...]
        gv = g_ref[...]
        r = lax.rsqrt(jnp.mean(xv * xv, axis=-1, keepdims=True) + EPS)
        xh = xv * r
        err = xh * gv - t_ref[...]
        loss_ref[...] += 0.5 * jnp.sum(jnp.mean(err * err, axis=-1, keepdims=True))
        dout = err * (1.0 / d)
        dg_ref[...] += jnp.sum(dout * xh, axis=0, keepdims=True)
        dxh = dout * gv
        dx_ref[...] = r * (dxh - xh * jnp.mean(dxh * xh, axis=-1, keepdims=True))

    return pl.pallas_call(
        body,
        name="loss_head",
        grid=(s // tm,),
        in_specs=[
            pl.BlockSpec((tm, d), lambda i: (i, 0)),
            _resident((1, d), lambda i: (0, 0)),
            pl.BlockSpec((tm, d), lambda i: (i, 0)),
        ],
        out_specs=[
            pl.BlockSpec((8, 128), lambda i: (0, 0)),
            pl.BlockSpec((tm, d), lambda i: (i, 0)),
            pl.BlockSpec((1, d), lambda i: (0, 0)),
        ],
        out_shape=[SDS((8, 128), F32), SDS((s, d), F32), SDS((1, d), F32)],
        compiler_params=_params("arbitrary"),
    )(x, gain, target)


def _adamw(w, g, m, v, name):
    shape = w.shape
    cols = shape[-1]
    rows = max(1, math.prod(shape[:-1]))
    tr = _divisor(rows, max(8, TILE_BYTES // (4 * cols)), 8)
    flat = [t.reshape(rows, cols) for t in (w, g, m, v)]

    def body(w_ref, g_ref, m_ref, v_ref, d_ref, nm_ref, nv_ref):
        gv = g_ref[...]
        m2 = ADAM_B1 * m_ref[...] + (1.0 - ADAM_B1) * gv
        v2 = ADAM_B2 * v_ref[...] + (1.0 - ADAM_B2) * (gv * gv)
        m_hat = m2 / (1.0 - ADAM_B1**ADAM_STEP)
        v_hat = v2 / (1.0 - ADAM_B2**ADAM_STEP)
        d_ref[...] = -ADAM_LR * (m_hat / (jnp.sqrt(v_hat) + ADAM_EPS) + ADAM_WD * w_ref[...])
        nm_ref[...] = m2
        nv_ref[...] = v2

    spec = pl.BlockSpec((tr, cols), lambda i: (i, 0))
    outs = pl.pallas_call(
        body,
        name=name,
        grid=(rows // tr,),
        in_specs=[spec] * 4,
        out_specs=[spec] * 3,
        out_shape=[SDS((rows, cols), F32)] * 3,
        compiler_params=_params("parallel"),
    )(*flat)
    return tuple(t.reshape(shape) for t in outs)


def _position():
    return lax.axis_index("x"), lax.axis_index("y"), lax.axis_index("c")


def _other_chips(x, y):
    return [(1 - x, y), (x, 1 - y), (1 - x, 1 - y)]


def _remote(src, dst, send_sem, recv_sem, device):
    return pltpu.make_async_remote_copy(
        src_ref=src, dst_ref=dst, send_sem=send_sem, recv_sem=recv_sem, device_id=device, device_id_type=MESH
    )


def _sub(ref, lead, shape, axis, chip=None, half=None):
    cut = [pl.ds(0, shape[0]), pl.ds(0, shape[1])]
    if chip is not None:
        size = shape[axis] // N_CHIPS
        cut[axis] = pl.ds(chip * size, size)
    if half is not None:
        size = shape[1 - axis] // 2
        cut[1 - axis] = pl.ds(half * size, size)
    return ref.at[(*lead, *cut)]


def _scaled(shape, axis, num, den=1):
    return tuple(d * num // den if i == axis else d for i, d in enumerate(shape))


def _gather_rider(shards, axes, layer, conv_w=None):
    n = len(shards)
    shard2d = [t.shape[1:] for t in shards]
    full2d = [_scaled(s2, ax, N_CHIPS) for s2, ax in zip(shard2d, axes)]
    out_shapes = [SDS(f2, t.dtype) for f2, t in zip(full2d, shards)]
    inputs = list(shards)
    own = 6
    scratch = [pltpu.SemaphoreType.DMA((n, 7)), pltpu.SemaphoreType.DMA((n, 7))]
    if conv_w is not None:
        cshape = conv_w.shape
        inputs.append(conv_w)
        out_shapes.append(SDS(_scaled(cshape, 2, N_CHIPS), conv_w.dtype))
        scratch += [pltpu.SemaphoreType.DMA((4,)), pltpu.SemaphoreType.DMA((4,))]

    def conv_slab(ref, chip):
        return ref.at[:, :, pl.ds((2 * chip[0] + chip[1]) * cshape[2], cshape[2])]

    def own_copy(ins, outs, send_sems, recv_sems, k):
        x, y, c = _position()
        slab = _sub(outs[k], (), full2d[k], axes[k], chip=2 * x + y)
        return _remote(ins[k].at[layer], slab, send_sems.at[k, own], recv_sems.at[k, own], (x, y, 1 - c))

    def start(ins, outs, scr):
        send_sems, recv_sems = scr[:2]
        x, y, c = _position()
        me = 2 * x + y
        for k in range(n):
            own_copy(ins, outs, send_sems, recv_sems, k).start()
            for j, chip in enumerate(_other_chips(x, y)):
                _remote(
                    _sub(ins[k], (layer,), shard2d[k], axes[k], half=c),
                    _sub(outs[k], (), full2d[k], axes[k], chip=me, half=c),
                    send_sems.at[k, j], recv_sems.at[k, j], (*chip, c),
                ).start()
        if conv_w is not None:
            csend, crecv = scr[2:]
            _remote(ins[n], conv_slab(outs[n], (x, y)), csend.at[3], crecv.at[3], (x, y, 1 - c)).start()
            for j, chip in enumerate(_other_chips(x, y)):
                _remote(ins[n], conv_slab(outs[n], (x, y)), csend.at[j], crecv.at[j], (*chip, c)).start()

    def finish(ins, outs, scr):
        send_sems, recv_sems = scr[:2]
        x, y, c = _position()
        me = 2 * x + y
        sibling = (x, y, 1 - c)
        chips = _other_chips(x, y)
        forwards = []
        for k in range(n):
            src = _sub(ins[k], (layer,), shard2d[k], axes[k], half=c)
            for j, chip in enumerate(chips):
                landed = _sub(outs[k], (), full2d[k], axes[k], chip=2 * chip[0] + chip[1], half=c)
                _remote(src, landed, send_sems.at[k, j], recv_sems.at[k, j], (*chip, c)).wait_recv()
                fwd = _remote(landed, landed, send_sems.at[k, 3 + j], recv_sems.at[k, 3 + j], sibling)
                fwd.start()
                forwards.append(fwd)
        for k in range(n):
            src = _sub(ins[k], (layer,), shard2d[k], axes[k], half=c)
            for j, chip in enumerate(chips):
                theirs = _sub(outs[k], (), full2d[k], axes[k], chip=2 * chip[0] + chip[1], half=1 - c)
                _remote(src, theirs, send_sems.at[k, 3 + j], recv_sems.at[k, 3 + j], sibling).wait_recv()
        for fwd in forwards:
            fwd.wait_send()
        for k in range(n):
            src = _sub(ins[k], (layer,), shard2d[k], axes[k], half=c)
            mine = _sub(outs[k], (), full2d[k], axes[k], chip=me, half=c)
            for j, chip in enumerate(chips):
                _remote(src, mine, send_sems.at[k, j], recv_sems.at[k, j], (*chip, c)).wait_send()
            own_copy(ins, outs, send_sems, recv_sems, k).wait()
        if conv_w is not None:
            csend, crecv = scr[2:]
            for j, chip in enumerate(chips):
                cp = _remote(ins[n], conv_slab(outs[n], chip), csend.at[j], crecv.at[j], (*chip, c))
                cp.wait_recv()
                cp.wait_send()
            _remote(ins[n], conv_slab(outs[n], (x, y)), csend.at[3], crecv.at[3], sibling).wait()

    return _Rider(inputs, out_shapes, scratch, start, finish)


def _pair_rider(stacks, axes, layer):
    n = len(stacks)
    shapes = [t.shape[1:] for t in stacks]
    out_shapes = [SDS(_scaled(s2, 1 - ax, 1, 2), F32) for s2, ax in zip(shapes, axes)]
    scratch = [pltpu.SemaphoreType.DMA((n,)), pltpu.SemaphoreType.DMA((n,))]

    def copies(ins, outs, scr):
        x, y, c = _position()
        return [
            _remote(_sub(ins[k], (layer,), shapes[k], axes[k], half=1 - c), outs[k], scr[0].at[k], scr[1].at[k], (x, y, 1 - c))
            for k in range(n)
        ]

    def start(ins, outs, scr):
        for cp in copies(ins, outs, scr):
            cp.start()

    def finish(ins, outs, scr):
        for cp in copies(ins, outs, scr):
            cp.wait()

    return _Rider(stacks, out_shapes, scratch, start, finish)


def _chip_rider(pairs, axes):
    n = len(pairs)
    shapes = [t.shape for t in pairs]
    out_shapes = [SDS((3,) + _scaled(s2, ax, 1, N_CHIPS), t.dtype) for s2, ax, t in zip(shapes, axes, pairs)]
    scratch = [pltpu.SemaphoreType.DMA((n, 3)), pltpu.SemaphoreType.DMA((n, 3))]

    def copies(ins, outs, scr):
        x, y, c = _position()
        return [
            _remote(
                _sub(ins[k], (), shapes[k], axes[k], chip=2 * chip[0] + chip[1]), outs[k].at[j],
                scr[0].at[k, j], scr[1].at[k, j], (*chip, c),
            )
            for k in range(n)
            for j, chip in enumerate(_other_chips(x, y))
        ]

    def start(ins, outs, scr):
        for cp in copies(ins, outs, scr):
            cp.start()

    def finish(ins, outs, scr):
        for cp in copies(ins, outs, scr):
            cp.wait()

    return _Rider(pairs, out_shapes, scratch, start, finish)


def _complete_rider(stacks, axes, layers):
    n = len(stacks)
    shapes = [t.shape[1:] for t in stacks]
    scratch = [pltpu.SemaphoreType.DMA((n,)), pltpu.SemaphoreType.DMA((n,))]

    def start(ins, outs, scr):
        x, y, c = _position()
        for k in range(n):
            mine = _sub(outs[k], (layers[k],), shapes[k], axes[k], half=c)
            _remote(mine, mine, scr[0].at[k], scr[1].at[k], (x, y, 1 - c)).start()

    def finish(ins, outs, scr):
        x, y, c = _position()
        for k in range(n):
            mine = _sub(outs[k], (layers[k],), shapes[k], axes[k], half=c)
            theirs = _sub(outs[k], (layers[k],), shapes[k], axes[k], half=1 - c)
            _remote(mine, theirs, scr[0].at[k], scr[1].at[k], (x, y, 1 - c)).wait_recv()
            _remote(mine, mine, scr[0].at[k], scr[1].at[k], (x, y, 1 - c)).wait_send()

    return _Rider(stacks, [SDS(t.shape, t.dtype) for t in stacks], scratch, start, finish, {k: k for k in range(n)})


def _small_rider(buf):
    rows, cols = buf.shape
    flips = [(fx, fy, fc) for fx in (0, 1) for fy in (0, 1) for fc in (0, 1)][1:]

    def peers():
        x, y, c = _position()
        return [(x + f[0] - 2 * x * f[0], y + f[1] - 2 * y * f[1], c + f[2] - 2 * c * f[2]) for f in flips]

    def start(ins, outs, scr):
        x, y, c = _position()
        mine = outs[0].at[4 * x + 2 * y + c]
        pltpu.make_async_copy(ins[0], mine, scr[2]).start()
        for j, peer in enumerate(peers()):
            _remote(ins[0], mine, scr[0].at[j], scr[1].at[j], peer).start()

    def finish(ins, outs, scr):
        x, y, c = _position()
        mine = outs[0].at[4 * x + 2 * y + c]
        for j, peer in enumerate(peers()):
            _remote(ins[0], outs[0].at[4 * peer[0] + 2 * peer[1] + peer[2]], scr[0].at[j], scr[1].at[j], peer).wait_recv()
        for j, peer in enumerate(peers()):
            _remote(ins[0], mine, scr[0].at[j], scr[1].at[j], peer).wait_send()
        pltpu.make_async_copy(ins[0], mine, scr[2]).wait()

    scratch = [pltpu.SemaphoreType.DMA((7,)), pltpu.SemaphoreType.DMA((7,)), pltpu.SemaphoreType.DMA(())]
    return _Rider([buf], [SDS((8, rows, cols), buf.dtype)], scratch, start, finish)


def _my_core():
    return lax.axis_index("c")


def _my_chip():
    return 2 * lax.axis_index("x") + lax.axis_index("y")


def _pair_sum(stack, layer, recv, axis, name):
    hk, hn = recv.shape
    tk = _divisor(hk, max(16, TILE_BYTES // (4 * hn)), 16)
    per = hk // tk

    def body(g_ref, r_ref, out_ref):
        out_ref[...] = (g_ref[...] + r_ref[...]).astype(out_ref.dtype)

    if axis == 1:
        mine = pl.BlockSpec((None, tk, hn), lambda i: (layer, _my_core() * per + i, 0))
    else:
        mine = pl.BlockSpec((None, tk, hn), lambda i: (layer, i, _my_core()))
    return pl.pallas_call(
        body,
        name=name,
        grid=(per,),
        in_specs=[mine, pl.BlockSpec((tk, hn), lambda i: (i, 0))],
        out_specs=pl.BlockSpec((tk, hn), lambda i: (i, 0)),
        out_shape=SDS((hk, hn), BF16),
        compiler_params=_params("parallel"),
    )(stack, recv)


def _chip_sum(pair, others, axis, stack, layer, name):
    _, sk, sn = others.shape
    tk = _divisor(sk, max(16, TILE_BYTES // (4 * sn)), 16)
    per = sk // tk

    def body(p_ref, o0_ref, o1_ref, o2_ref, *rest):
        out_ref = rest[-1]
        acc = p_ref[...].astype(F32) + o0_ref[...].astype(F32)
        out_ref[...] = (acc + o1_ref[...].astype(F32)) + o2_ref[...].astype(F32)

    if axis == 1:
        own = pl.BlockSpec((tk, sn), lambda i: (i, _my_chip()))
        out = pl.BlockSpec((None, tk, sn), lambda i: (layer, _my_core() * per + i, 0))
        shard = (2 * sk, sn)
    else:
        own = pl.BlockSpec((tk, sn), lambda i: (_my_chip() * per + i, 0))
        out = pl.BlockSpec((None, tk, sn), lambda i: (layer, i, _my_core()))
        shard = (sk, 2 * sn)
    other = lambda j: pl.BlockSpec((None, tk, sn), lambda i: (j, i, 0))
    in_specs = [own, other(0), other(1), other(2)]
    args = [pair, others, others, others]
    aliases = {}
    if stack is not None:
        in_specs.append(ANY)
        args.append(stack)
        aliases = {4: 0}
    return pl.pallas_call(
        body,
        name=name,
        grid=(per,),
        in_specs=in_specs,
        out_specs=out,
        out_shape=SDS((DEPTH,) + shard, F32),
        input_output_aliases=aliases,
        compiler_params=_params("parallel"),
    )(*args)


def _sum_devices(gathered, name):
    _, rows, cols = gathered.shape

    def body(g_ref, out_ref):
        acc = g_ref[0]
        for dev in range(1, 8):
            acc = acc + g_ref[dev]
        out_ref[...] = acc

    return pl.pallas_call(body, name=name, out_shape=SDS((rows, cols), F32))(gathered)


PACK_TILE = 8 * 128


def _pack(arrays):
    parts = []
    for t in arrays:
        flat = t.reshape(-1)
        pad = (-flat.shape[0]) % PACK_TILE
        if pad:
            flat = jnp.concatenate([flat, jnp.zeros((pad,), flat.dtype)])
        parts.append(flat.reshape(-1, 128))
    return jnp.concatenate(parts, axis=0)


def _unpack(buf, shapes):
    out, row = [], 0
    for shp in shapes:
        size = math.prod(shp)
        rows = -(-size // PACK_TILE) * 8
        out.append(buf[row : row + rows].reshape(-1)[:size].reshape(shp))
        row += rows
    return out


BIG = ("w_in", "w_br_gm", "w_br_hg", "w_out", "w_up", "w_down")
BIG_AXIS = (1, 1, 1, 0, 1, 0)
LAYER_SMALL = ("mix_norm", "gm_ln_g", "gm_ln_b", "gm_ws", "gm_bs", "lb", "hg_norm_g", "ffn_norm", "conv_w", "conv_b")
WEIGHTS = ("mix_norm", "w_in", "gm_ln_g", "gm_ln_b", "gm_ws", "gm_bs", "hg_lb_logits", "hg_norm_g", "w_br_gm", "w_br_hg", "w_out",
           "ffn_norm", "w_up", "conv_w", "conv_b", "w_down", "final_norm")


def kernel(x, mix_norm, w_in, gm_ln_g, gm_ln_b, gm_ws, gm_bs, hg_lb_logits, hg_norm_g, w_br_gm, w_br_hg, w_out, ffn_norm, w_up, conv_w, conv_b, w_down, final_norm, loss_target, m_mix_norm, m_w_in, m_gm_ln_g, m_gm_ln_b, m_gm_ws, m_gm_bs, m_hg_lb_logits, m_hg_norm_g, m_w_br_gm, m_w_br_hg, m_w_out, m_ffn_norm, m_w_up, m_conv_w, m_conv_b, m_w_down, m_final_norm, v_mix_norm, v_w_in, v_gm_ln_g, v_gm_ln_b, v_gm_ws, v_gm_bs, v_hg_lb_logits, v_hg_norm_g, v_w_br_gm, v_w_br_hg, v_w_out, v_ffn_norm, v_w_up, v_conv_w, v_conv_b, v_w_down, v_final_norm):
    w = dict(mix_norm=mix_norm, w_in=w_in, gm_ln_g=gm_ln_g, gm_ln_b=gm_ln_b, gm_ws=gm_ws, gm_bs=gm_bs, hg_lb_logits=hg_lb_logits,
             hg_norm_g=hg_norm_g, w_br_gm=w_br_gm, w_br_hg=w_br_hg, w_out=w_out, ffn_norm=ffn_norm, w_up=w_up, conv_w=conv_w,
             conv_b=conv_b, w_down=w_down, final_norm=final_norm)
    m = dict(mix_norm=m_mix_norm, w_in=m_w_in, gm_ln_g=m_gm_ln_g, gm_ln_b=m_gm_ln_b, gm_ws=m_gm_ws, gm_bs=m_gm_bs,
             hg_lb_logits=m_hg_lb_logits, hg_norm_g=m_hg_norm_g, w_br_gm=m_w_br_gm, w_br_hg=m_w_br_hg, w_out=m_w_out,
             ffn_norm=m_ffn_norm, w_up=m_w_up, conv_w=m_conv_w, conv_b=m_conv_b, w_down=m_w_down, final_norm=m_final_norm)
    v = dict(mix_norm=v_mix_norm, w_in=v_w_in, gm_ln_g=v_gm_ln_g, gm_ln_b=v_gm_ln_b, gm_ws=v_gm_ws, gm_bs=v_gm_bs,
             hg_lb_logits=v_hg_lb_logits, hg_norm_g=v_hg_norm_g, w_br_gm=v_w_br_gm, w_br_hg=v_w_br_hg, w_out=v_w_out,
             ffn_norm=v_ffn_norm, w_up=v_w_up, conv_w=v_conv_w, conv_b=v_conv_b, w_down=v_w_down, final_norm=v_final_norm)

    axes = list(BIG_AXIS)
    d = x.shape[2]

    shards = [w[k].astype(MM_DTYPE) for k in BIG]
    w_in_0, conv_full = _run_rider(_gather_rider(shards[:1], axes[:1], 0, conv_w=conv_w), "gather_weights_0")
    full = [dict(w_in=w_in_0)]
    lb = _lower_bounds(hg_lb_logits)
    xs = x[0]
    saved = []
    mixer = 4
    for l in range(DEPTH):
        fw = full[l]
        more = l + 1 < DEPTH
        bst = gm_bs[l].T
        (z, h, a), got = _in_proj(
            xs, mix_norm[l : l + 1], fw["w_in"], gm_ln_g[l : l + 1], gm_ln_b[l : l + 1], gm_ws[l], bst, f"in_proj_{l}",
            _gather_rider(shards[mixer:], axes[mixer:], l),
        )
        fw.update(zip(BIG[mixer:], got))
        rider = _gather_rider(shards[:1], axes[:1], l + 1) if more else None
        if l == 0:
            rider = _join(rider, _gather_rider(shards[1:mixer], axes[1:mixer], 0))
        (o, b, states), got = _hgrn_fwd(z, lb[l : l + 1], hg_norm_g[l : l + 1], f"hgrn_fwd_{l}", rider)
        if more:
            full.append(dict(w_in=got[0]))
        if l == 0:
            fw.update(zip(BIG[1:mixer], got[1:]))
        rider = _gather_rider(shards[1:mixer], axes[1:mixer], l + 1) if more else None
        (x1,), got = _mix_fwd(xs, a, b, z, fw["w_br_gm"], fw["w_br_hg"], fw["w_out"], f"mix_fwd_{l}", rider)
        if more:
            full[l + 1].update(zip(BIG[1:mixer], got))
        (x2, z2, h2, conv), _ = _ffn_fwd(
            x1, ffn_norm[l : l + 1], fw["w_up"], conv_full, conv_b[l : l + 1], fw["w_down"], l, f"ffn_fwd_{l}"
        )
        saved.append((xs, h, z, a, b, o, states, x1, h2, z2, conv, bst))
        xs = x2

    loss_tile, dx, d_final = _loss_head(xs, final_norm.reshape(1, d), loss_target[0])
    loss = lax.psum(loss_tile[0, 0], ("x", "y", "c"))

    big = {k: None for k in BIG}
    grads = [None] * len(BIG)
    per_layer = [None] * DEPTH
    mix_ids, ffn_ids = list(range(mixer)), list(range(mixer, len(BIG)))
    mix_axes, ffn_axes = axes[:mixer], axes[mixer:]
    mix_pairs = None
    mix_summed = None

    def pair_sums(ids, received, layer):
        return [_pair_sum(big[BIG[k]], layer, r, axes[k], f"pair_sum_{BIG[k]}_{layer}") for k, r in zip(ids, received)]

    def chip_sums(ids, pairs, others, layer):
        for k, p, ot in zip(ids, pairs, others):
            grads[k] = _chip_sum(p, ot, axes[k], grads[k], layer, f"chip_sum_{BIG[k]}_{layer}")

    for l in reversed(range(DEPTH)):
        x0, h, z, a, b, o, states, x1, h2, z2, conv, bst = saved[l]
        fw = full[l]
        dz2, act, dconv, dx1, d_ffn = _ffn_bwd(
            dx, z2, conv, conv_full, fw["w_down"], fw["w_up"], x1, ffn_norm[l : l + 1], l, f"ffn_bwd_{l}"
        )
        big["w_down"] = _matmul_tn(act, dx, big["w_down"], l, f"dw_down_{l}")
        big["w_up"] = _matmul_tn(h2, dz2, big["w_up"], l, f"dw_up_{l}")
        rider = _pair_rider([big[BIG[k]] for k in ffn_ids], ffn_axes, l)
        if mix_pairs is not None:
            rider = _join(rider, _chip_rider(mix_pairs, mix_axes))
        (y, dpa, dpb, db, dgates, dz_gm, d_lng, d_lnb, d_ws, d_bst), got = _mix_bwd(
            dx1, a, b, z, fw["w_br_gm"], fw["w_br_hg"], fw["w_out"], gm_ln_g[l : l + 1], gm_ln_b[l : l + 1], gm_ws[l], bst,
            f"mix_bwd_{l}", rider,
        )
        if mix_pairs is not None:
            chip_sums(mix_ids, mix_pairs, got[len(ffn_ids) :], l + 1)
            mix_summed = l + 1
        ffn_pairs = pair_sums(ffn_ids, got[: len(ffn_ids)], l)
        big["w_out"] = _matmul_tn(y, dx1, big["w_out"], l, f"dw_out_{l}")
        big["w_br_gm"] = _matmul_tn(a, dpa, big["w_br_gm"], l, f"dw_br_gm_{l}")
        big["w_br_hg"] = _matmul_tn(b, dpb, big["w_br_hg"], l, f"dw_br_hg_{l}")
        (dz_hg, d_lb, d_ng), others = _hgrn_bwd(
            db, z, o, states, lb[l : l + 1], hg_norm_g[l : l + 1], f"hgrn_bwd_{l}", _chip_rider(ffn_pairs, ffn_axes)
        )
        chip_sums(ffn_ids, ffn_pairs, others, l)
        n_in = fw["w_in"].shape[1]
        big["w_in"] = _matmul_tn(h, dz_gm, big["w_in"], l, f"dw_in_gm_{l}", n_total=n_in, col_off=0)
        big["w_in"] = _matmul_tn(h, dz_hg, big["w_in"], l, f"dw_in_hg_{l}", n_total=n_in, col_off=2 * BR_DIM)
        big["w_in"] = _matmul_tn(h, dgates, big["w_in"], l, f"dw_in_gate_{l}", n_total=n_in, col_off=6 * BR_DIM)
        done_ids = ffn_ids + (mix_ids if mix_summed is not None else [])
        done_layers = [l] * len(ffn_ids) + ([mix_summed] * len(mix_ids) if mix_summed is not None else [])
        rider = _join(
            _pair_rider([big[BIG[k]] for k in mix_ids], mix_axes, l),
            _complete_rider([grads[k] for k in done_ids], [axes[k] for k in done_ids], done_layers),
        )
        if l == 0:
            upper = [_pack([per_layer[j][k] for k in LAYER_SMALL]) for j in range(1, DEPTH)] + [_pack([d_final[0]])]
            rider = _join(rider, _small_rider(jnp.concatenate(upper, axis=0)))
        (dx, d_mix), got = _in_proj_bwd([dz_gm, dz_hg, dgates], fw["w_in"], x0, mix_norm[l : l + 1], dx1, f"in_proj_bwd_{l}", rider)
        for k, g in zip(done_ids, got[len(mix_ids) : len(mix_ids) + len(done_ids)]):
            grads[k] = g
        gathered_upper = got[len(mix_ids) + len(done_ids) :]
        mix_pairs = pair_sums(mix_ids, got[: len(mix_ids)], l)
        per_layer[l] = dict(
            mix_norm=d_mix[0], gm_ln_g=d_lng[0], gm_ln_b=d_lnb[0], gm_ws=d_ws, gm_bs=d_bst.T, lb=d_lb[0], hg_norm_g=d_ng[0],
            ffn_norm=d_ffn[0], conv_w=dconv[0:3], conv_b=dconv[3],
        )

    got = _run_rider(
        _join(_chip_rider(mix_pairs, mix_axes), _small_rider(_pack([per_layer[0][k] for k in LAYER_SMALL]))), "grad_exchange_0"
    )
    chip_sums(mix_ids, mix_pairs, got[: len(mix_ids)], 0)
    done = _run_rider(_complete_rider([grads[k] for k in mix_ids], mix_axes, [0] * len(mix_ids)), "grad_complete_0")
    for k, g in zip(mix_ids, done):
        grads[k] = g
    grad = dict(zip(BIG, grads))

    layer_shapes = [per_layer[0][k].shape for k in LAYER_SMALL]
    summed_upper = _unpack(_sum_devices(gathered_upper[0], "sum_small_upper"), layer_shapes * (DEPTH - 1) + [d_final[0].shape])
    summed_0 = _unpack(_sum_devices(got[len(mix_ids)], "sum_small_0"), layer_shapes)
    by_layer = [summed_0] + [summed_upper[j * len(LAYER_SMALL) : (j + 1) * len(LAYER_SMALL)] for j in range(DEPTH - 1)]
    small = {k: jnp.stack([by_layer[j][i] for j in range(DEPTH)]) for i, k in enumerate(LAYER_SMALL)}
    for k in LAYER_SMALL:
        if k != "lb":
            grad[k] = small[k]
    grad["hg_lb_logits"] = _lower_bounds_bwd(hg_lb_logits, small["lb"])
    grad["final_norm"] = summed_upper[-1]
    grad["conv_w"] = lax.dynamic_slice_in_dim(grad["conv_w"], _my_chip() * conv_w.shape[2], conv_w.shape[2], axis=2)

    delta, new_m, new_v = {}, {}, {}
    for k in WEIGHTS:
        delta[k], new_m[k], new_v[k] = _adamw(w[k], grad[k], m[k], v[k], f"adamw_{k}")

    return (loss, dx[None], *[grad[k] for k in WEIGHTS], *[delta[k] for k in WEIGHTS], *[new_m[k] for k in WEIGHTS],
            *[new_v[k] for k in WEIGHTS])
```

```python
import math

import jax
import jax.numpy as jnp
from jax import lax
from jax.experimental import pallas as pl
from jax.experimental.pallas import tpu as pltpu

F32 = jnp.float32
BF16 = jnp.bfloat16
MM_DTYPE = BF16

N_HEADS = 4
HEAD_DIM = 128
BR_DIM = N_HEADS * HEAD_DIM
CHUNK = 64
GM_BLOCK = 128
DEPTH = 4
N_CHIPS = 4
EPS = 1e-6
TINY = 1e-30
LB_MAX = 0.999
ADAM_LR = 0.001
ADAM_B1 = 0.9
ADAM_B2 = 0.999
ADAM_EPS = 1e-08
ADAM_WD = 0.01
ADAM_STEP = 10
INV_SQRT2 = 1.0 / math.sqrt(2.0)
INV_SQRT_2PI = 1.0 / math.sqrt(2.0 * math.pi)

VMEM_LIMIT_BYTES = 56 * 1024 * 1024
TILE_BYTES = 2 * 1024 * 1024
ACC_BYTES = 6 * 1024 * 1024
MESH = pl.DeviceIdType.MESH
SDS = jax.ShapeDtypeStruct
ANY = pl.BlockSpec(memory_space=pl.ANY)


def _params(*sem):
    return pltpu.CompilerParams(dimension_semantics=sem or None, vmem_limit_bytes=VMEM_LIMIT_BYTES)


def _divisor(n, limit, mult):
    best = None
    for d in range(mult, min(n, limit) + 1, mult):
        if n % d == 0:
            best = d
    return best if best is not None else n


def _dot(a, b):
    return jnp.dot(a.astype(MM_DTYPE), b.astype(MM_DTYPE), preferred_element_type=F32)


def _dot_nt(a, b):
    return lax.dot_general(a.astype(MM_DTYPE), b.astype(MM_DTYPE), (((1,), (1,)), ((), ())), preferred_element_type=F32)


def _dot_tn(a, b):
    return lax.dot_general(a.astype(MM_DTYPE), b.astype(MM_DTYPE), (((0,), (0,)), ((), ())), preferred_element_type=F32)


def _sigmoid(x):
    return 1.0 / (1.0 + jnp.exp(-x))


def _gelu(x):
    return 0.5 * x * (1.0 + lax.erf(x * INV_SQRT2))


def _gelu_grad(x):
    return 0.5 * (1.0 + lax.erf(x * INV_SQRT2)) + x * jnp.exp(-0.5 * x * x) * INV_SQRT_2PI


def _silu_grad(x, s):
    return s * (1.0 + x * (1.0 - s))


def _resident(shape, index_map):
    return pl.BlockSpec(shape, index_map, pipeline_mode=pl.Buffered(1))


class _Rider:
    def __init__(self, inputs, out_shapes, scratch, start, finish, aliases=None):
        self.inputs = list(inputs)
        self.out_shapes = list(out_shapes)
        self.scratch = list(scratch)
        self.start = start
        self.finish = finish
        self.aliases = dict(aliases or {})


def _join(a, b):
    if a is None or b is None:
        return a if b is None else b
    na, oa, sa = len(a.inputs), len(a.out_shapes), len(a.scratch)

    def start(i, o, s):
        a.start(i[:na], o[:oa], s[:sa])
        b.start(i[na:], o[oa:], s[sa:])

    def finish(i, o, s):
        a.finish(i[:na], o[:oa], s[:sa])
        b.finish(i[na:], o[oa:], s[sa:])

    aliases = dict(a.aliases)
    aliases.update({na + k: oa + v for k, v in b.aliases.items()})
    return _Rider(a.inputs + b.inputs, a.out_shapes + b.out_shapes, a.scratch + b.scratch, start, finish, aliases)


def _hosted_call(body, rider, *, name, grid, in_specs, out_specs, out_shape, scratch_shapes, args, semantics):
    n_in, n_out, n_scr = len(in_specs), len(out_specs), len(scratch_shapes)
    if rider is None:
        outs = pl.pallas_call(
            body, name=name, grid=grid, in_specs=in_specs, out_specs=out_specs, out_shape=out_shape,
            scratch_shapes=scratch_shapes, compiler_params=_params(*semantics),
        )(*args)
        return list(outs), []
    ri, ro = len(rider.inputs), len(rider.out_shapes)

    def wrapped(*refs):
        p = 0
        hin = refs[p : p + n_in]
        p += n_in
        rin = refs[p : p + ri]
        p += ri
        hout = refs[p : p + n_out]
        p += n_out
        rout = refs[p : p + ro]
        p += ro
        hscr = refs[p : p + n_scr]
        rscr = refs[p + n_scr :]

        @pl.when(pl.program_id(0) == 0)
        def _():
            rider.start(rin, rout, rscr)

        body(*hin, *hout, *hscr)

        @pl.when(pl.program_id(0) == grid[0] - 1)
        def _():
            rider.finish(rin, rout, rscr)

    outs = pl.pallas_call(
        wrapped,
        name=name,
        grid=grid,
        in_specs=list(in_specs) + [ANY] * ri,
        out_specs=list(out_specs) + [ANY] * ro,
        out_shape=list(out_shape) + rider.out_shapes,
        scratch_shapes=list(scratch_shapes) + rider.scratch,
        input_output_aliases={n_in + k: n_out + v for k, v in rider.aliases.items()},
        compiler_params=_params(*semantics),
    )(*args, *rider.inputs)
    return list(outs[:n_out]), list(outs[n_out:])


def _run_rider(rider, name):
    ri, ro = len(rider.inputs), len(rider.out_shapes)

    def body(*refs):
        rin, rout, rscr = refs[:ri], refs[ri : ri + ro], refs[ri + ro :]
        rider.start(rin, rout, rscr)
        rider.finish(rin, rout, rscr)

    return list(
        pl.pallas_call(
            body,
            name=name,
            in_specs=[ANY] * ri,
            out_specs=[ANY] * ro,
            out_shape=rider.out_shapes,
            scratch_shapes=rider.scratch,
            input_output_aliases=rider.aliases,
        )(*rider.inputs)
    )


def _lower_bounds(logits):
    def body(lg_ref, lb_ref):
        lg = lg_ref[...]
        mx = jnp.max(lg, axis=0, keepdims=True)
        e = jnp.exp(lg - mx)
        p = e / jnp.sum(e, axis=0, keepdims=True)
        run = p[0:1]
        rows = [run - p[0:1]]
        for l in range(1, DEPTH):
            run = run + p[l : l + 1]
            rows.append(run - p[0:1])
        raw = jnp.concatenate(rows, axis=0)
        lb_ref[...] = jnp.minimum(jnp.maximum(raw, 0.0), LB_MAX)

    return pl.pallas_call(body, name="lower_bounds", out_shape=SDS(logits.shape, F32))(logits)


def _lower_bounds_bwd(logits, dlb):
    def body(lg_ref, dlb_ref, dlg_ref):
        lg = lg_ref[...]
        mx = jnp.max(lg, axis=0, keepdims=True)
        e = jnp.exp(lg - mx)
        p = e / jnp.sum(e, axis=0, keepdims=True)
        run = p[0:1]
        rows = [run - p[0:1]]
        for l in range(1, DEPTH):
            run = run + p[l : l + 1]
            rows.append(run - p[0:1])
        raw = jnp.concatenate(rows, axis=0)
        lo = jnp.where(raw > 0.0, 1.0, jnp.where(raw == 0.0, 0.5, 0.0))
        clipped = jnp.maximum(raw, 0.0)
        hi = jnp.where(clipped < LB_MAX, 1.0, jnp.where(clipped == LB_MAX, 0.5, 0.0))
        draw = dlb_ref[...] * lo * hi
        total = jnp.sum(draw, axis=0, keepdims=True)
        dps = []
        tail = total
        for j in range(DEPTH):
            dps.append(tail - total if j == 0 else tail)
            tail = tail - draw[j : j + 1]
        dp = jnp.concatenate(dps, axis=0)
        dlg_ref[...] = p * (dp - jnp.sum(p * dp, axis=0, keepdims=True))

    return pl.pallas_call(body, name="lower_bounds_bwd", out_shape=SDS(logits.shape, F32))(logits, dlb)


def _in_proj(x, gain, w, lng, lnb, ws, bst, name, rider=None):
    s, d = x.shape
    n = w.shape[1]
    tm = _divisor(s, 512, GM_BLOCK)

    def body(x_ref, g_ref, w_ref, lng_ref, lnb_ref, ws_ref, bst_ref, z_ref, h_ref, a_ref):
        xv = x_ref[...]
        r = lax.rsqrt(jnp.mean(xv * xv, axis=-1, keepdims=True) + EPS)
        h = ((xv * r) * g_ref[...]).astype(MM_DTYPE)
        h_ref[...] = h
        uv = jnp.dot(h, w_ref[:, : 2 * BR_DIM], preferred_element_type=F32)
        z_ref[:, : 2 * BR_DIM] = uv
        z_ref[:, 2 * BR_DIM :] = jnp.dot(h, w_ref[:, 2 * BR_DIM :], preferred_element_type=F32)
        _gmlp_fwd_tile(uv[:, :BR_DIM], uv[:, BR_DIM:], lng_ref, lnb_ref, ws_ref, bst_ref, a_ref)

    row = lambda wd: pl.BlockSpec((tm, wd), lambda i: (i, 0))
    return _hosted_call(
        body,
        rider,
        name=name,
        grid=(s // tm,),
        in_specs=[row(d), _resident((1, d), lambda i: (0, 0)), _resident((d, n), lambda i: (0, 0))] + _gmlp_param_specs(),
        out_specs=[row(n), row(d), row(BR_DIM)],
        out_shape=[SDS((s, n), F32), SDS((s, d), MM_DTYPE), SDS((s, BR_DIM), MM_DTYPE)],
        scratch_shapes=[],
        args=[x, gain, w, lng, lnb, ws, bst],
        semantics=("arbitrary",),
    )


def _in_proj_bwd(dz_parts, w, x, gain, dres, name, rider=None):
    s, d = x.shape
    n = w.shape[1]
    tm = _divisor(s, 512, 16)
    widths = [p.shape[1] for p in dz_parts]
    offsets = [sum(widths[:k]) for k in range(len(widths))]
    n_parts = len(dz_parts)

    def body(*refs):
        dz_refs = refs[:n_parts]
        w_ref, x_ref, g_ref, dres_ref, dx_ref, dg_ref = refs[n_parts:]

        @pl.when(pl.program_id(0) == 0)
        def _():
            dg_ref[...] = jnp.zeros_like(dg_ref)

        dh = None
        for dz_ref, off, wd in zip(dz_refs, offsets, widths):
            part = _dot_nt(dz_ref[...], w_ref[:, off : off + wd])
            dh = part if dh is None else dh + part
        xv = x_ref[...]
        r = lax.rsqrt(jnp.mean(xv * xv, axis=-1, keepdims=True) + EPS)
        xh = xv * r
        dg_ref[...] += jnp.sum(dh * xh, axis=0, keepdims=True)
        dxh = dh * g_ref[...]
        dx_ref[...] = dres_ref[...] + r * (dxh - xh * jnp.mean(dxh * xh, axis=-1, keepdims=True))

    in_specs = [pl.BlockSpec((tm, wd), lambda i: (i, 0)) for wd in widths] + [
        _resident((d, n), lambda i: (0, 0)),
        pl.BlockSpec((tm, d), lambda i: (i, 0)),
        _resident((1, d), lambda i: (0, 0)),
        pl.BlockSpec((tm, d), lambda i: (i, 0)),
    ]
    return _hosted_call(
        body,
        rider,
        name=name,
        grid=(s // tm,),
        in_specs=in_specs,
        out_specs=[pl.BlockSpec((tm, d), lambda i: (i, 0)), pl.BlockSpec((1, d), lambda i: (0, 0))],
        out_shape=[SDS((s, d), F32), SDS((1, d), F32)],
        scratch_shapes=[],
        args=[*dz_parts, w, x, gain, dres],
        semantics=("arbitrary",),
    )


def _matmul_tn(a, b, stack, layer, name, n_total=None, col_off=0):
    s, k = a.shape
    n = b.shape[1]
    n_total = n if n_total is None else n_total
    tn = _divisor(math.gcd(n, col_off) if col_off else n, max(128, ACC_BYTES // (4 * k)), 128)
    ts = _divisor(s, min(2048, max(512, ACC_BYTES // (2 * k))), 16)
    off = col_off // tn

    def body(a_ref, b_ref, *rest):
        out_ref = rest[-1]

        @pl.when(pl.program_id(1) == 0)
        def _():
            out_ref[...] = jnp.zeros_like(out_ref)

        out_ref[...] += _dot_tn(a_ref[...], b_ref[...])

    in_specs = [pl.BlockSpec((ts, k), lambda j, i: (i, 0)), pl.BlockSpec((ts, tn), lambda j, i: (i, j))]
    args = [a, b]
    aliases = {}
    if stack is not None:
        in_specs.append(ANY)
        args.append(stack)
        aliases = {2: 0}
    return pl.pallas_call(
        body,
        name=name,
        grid=(n // tn, s // ts),
        in_specs=in_specs,
        out_specs=pl.BlockSpec((None, k, tn), lambda j, i: (layer, 0, off + j)),
        out_shape=SDS((DEPTH, k, n_total), F32),
        input_output_aliases=aliases,
        compiler_params=_params("parallel", "arbitrary"),
    )(*args)


def _gm_mask():
    r = lax.broadcasted_iota(jnp.int32, (GM_BLOCK, GM_BLOCK), 0) // CHUNK
    c = lax.broadcasted_iota(jnp.int32, (GM_BLOCK, GM_BLOCK), 1) // CHUNK
    return r >= c


def _gm_normed(v, lng, lnb):
    vg = _gelu(v)
    mu = jnp.mean(vg, axis=-1, keepdims=True)
    xc = vg - mu
    rstd = lax.rsqrt(jnp.mean(xc * xc, axis=-1, keepdims=True) + EPS)
    xh = xc * rstd
    return xh, rstd, xh * lng + lnb


def _gmlp_param_specs():
    return [
        _resident((1, BR_DIM), lambda i: (0, 0)),
        _resident((1, BR_DIM), lambda i: (0, 0)),
        _resident((N_HEADS, GM_BLOCK, GM_BLOCK), lambda i: (0, 0, 0)),
        _resident((GM_BLOCK, N_HEADS), lambda i: (0, 0)),
    ]


def _gmlp_fwd_tile(u, v, lng_ref, lnb_ref, ws_ref, bst_ref, a_ref):
    mask = _gm_mask()
    ug = _gelu(u)
    _, _, vn = _gm_normed(v, lng_ref[...], lnb_ref[...])
    vn = vn.astype(MM_DTYPE)
    for h in range(N_HEADS):
        cs = slice(h * HEAD_DIM, (h + 1) * HEAD_DIM)
        wm = jnp.where(mask, ws_ref[h], 0.0).astype(MM_DTYPE)
        for blk in range(u.shape[0] // GM_BLOCK):
            rs = slice(blk * GM_BLOCK, (blk + 1) * GM_BLOCK)
            mixed = _dot(wm, vn[rs, cs]) + bst_ref[:, h : h + 1]
            a_ref[rs, cs] = (ug[rs, cs] * mixed).astype(MM_DTYPE)


def _gmlp_bwd_tile(da_v, u, v, lng_ref, lnb_ref, ws_ref, bst_ref, dz_ref, dlng_ref, dlnb_ref, dws_ref, dbst_ref):
    mask = _gm_mask()
    ug = _gelu(u)
    lng_v = lng_ref[...]
    xh, rstd, vn = _gm_normed(v, lng_v, lnb_ref[...])
    vnb = vn.astype(MM_DTYPE)
    dmix = da_v * ug
    dmixb = dmix.astype(MM_DTYPE)
    dvn_cols = []
    for h in range(N_HEADS):
        cs = slice(h * HEAD_DIM, (h + 1) * HEAD_DIM)
        wm = jnp.where(mask, ws_ref[h], 0.0).astype(MM_DTYPE)
        dw = jnp.zeros((GM_BLOCK, GM_BLOCK), F32)
        dbias = jnp.zeros((GM_BLOCK, 1), F32)
        dvn_rows = []
        for blk in range(u.shape[0] // GM_BLOCK):
            rs = slice(blk * GM_BLOCK, (blk + 1) * GM_BLOCK)
            mixed = _dot(wm, vnb[rs, cs]) + bst_ref[:, h : h + 1]
            dz_ref[rs, cs] = (da_v[rs, cs] * mixed * _gelu_grad(u[rs, cs])).astype(MM_DTYPE)
            dw = dw + _dot_nt(dmixb[rs, cs], vnb[rs, cs])
            dbias = dbias + jnp.sum(dmix[rs, cs], axis=1, keepdims=True)
            dvn_rows.append(_dot_tn(wm, dmixb[rs, cs]))
        dws_ref[h] += jnp.where(mask, dw, 0.0)
        dbst_ref[:, h : h + 1] += dbias
        dvn_cols.append(jnp.concatenate(dvn_rows, axis=0) if len(dvn_rows) > 1 else dvn_rows[0])
    dvn = jnp.concatenate(dvn_cols, axis=1)
    dlng_ref[...] += jnp.sum(dvn * xh, axis=0, keepdims=True)
    dlnb_ref[...] += jnp.sum(dvn, axis=0, keepdims=True)
    dxh = dvn * lng_v
    dvg = rstd * (dxh - jnp.mean(dxh, axis=-1, keepdims=True) - xh * jnp.mean(dxh * xh, axis=-1, keepdims=True))
    dz_ref[:, BR_DIM:] = (dvg * _gelu_grad(v)).astype(MM_DTYPE)


HG_ROWS = 512
HG_UNROLL = 8
MID = CHUNK // 2 - 1


def _tri(lower):
    r = lax.broadcasted_iota(jnp.int32, (CHUNK, CHUNK), 0)
    c = lax.broadcasted_iota(jnp.int32, (CHUNK, CHUNK), 1)
    return r >= c if lower else c >= r


def _scan_rows(x, reverse=False):
    n = x.shape[0]
    rid = lax.broadcasted_iota(jnp.int32, x.shape, 0)
    k = 1
    while k < n:
        if reverse:
            x = x + jnp.where(rid < n - k, pltpu.roll(x, n - k, 0), 0.0)
        else:
            x = x + jnp.where(rid >= k, pltpu.roll(x, k, 0), 0.0)
        k *= 2
    return x


def _hgrn_fwd_chunk(zqf_ref, zio_ref, ci, lb_ref, ng_v, state, o_ref, b_ref, st_ref):
    low = _tri(True)
    rows = slice(ci * CHUNK, (ci + 1) * CHUNK)
    for h in range(N_HEADS):
        cs = slice(h * HEAD_DIM, (h + 1) * HEAD_DIM)
        cs2 = slice(BR_DIM + h * HEAD_DIM, BR_DIM + (h + 1) * HEAD_DIM)
        zq = zqf_ref[rows, cs]
        zf = zqf_ref[rows, cs2]
        vi = zio_ref[rows, cs]
        zog = zio_ref[rows, cs2]
        lbh = lb_ref[:, cs]
        qf = zq * _sigmoid(zq)
        forget = lbh + (1.0 - lbh) * _sigmoid(zf)
        g = jnp.log(jnp.maximum(forget, TINY))
        k = (1.0 - lbh) * _sigmoid(-zf)
        gc = _scan_rows(g)
        gm = gc[MID : MID + 1]
        gl = gc[CHUNK - 1 : CHUNK]
        a = jnp.where(low, _dot_nt(qf * jnp.exp(gc - gm), k * jnp.exp(gm - gc)), 0.0)
        st = state[h]
        st_ref[ci, h] = st
        o = _dot(a, vi) + _dot_nt(qf * jnp.exp(gc), st)
        state[h] = st * jnp.exp(gl) + _dot_tn(vi, k * jnp.exp(gl - gc))
        r = lax.rsqrt(jnp.mean(o * o, axis=-1, keepdims=True) + EPS)
        o_ref[rows, cs] = o
        b_ref[rows, cs] = (((o * r) * ng_v) * (zog * _sigmoid(zog))).astype(MM_DTYPE)


def _hgrn_fwd(z, lb, ng, name, rider=None):
    s = z.shape[0]
    rows_per = _divisor(s, HG_ROWS, CHUNK)
    n_sub = rows_per // CHUNK

    def body(zqf_ref, zio_ref, lb_ref, ng_ref, o_ref, b_ref, st_ref, state):
        @pl.when(pl.program_id(0) == 0)
        def _():
            state[...] = jnp.zeros_like(state)

        ng_v = ng_ref[...]
        for ci in range(n_sub):
            _hgrn_fwd_chunk(zqf_ref, zio_ref, ci, lb_ref, ng_v, state, o_ref, b_ref, st_ref)

    return _hosted_call(
        body,
        rider,
        name=name,
        grid=(s // rows_per,),
        in_specs=[
            pl.BlockSpec((rows_per, 2 * BR_DIM), lambda i: (i, 1)),
            pl.BlockSpec((rows_per, 2 * BR_DIM), lambda i: (i, 2)),
            _resident((1, BR_DIM), lambda i: (0, 0)),
            _resident((1, HEAD_DIM), lambda i: (0, 0)),
        ],
        out_specs=[
            pl.BlockSpec((rows_per, BR_DIM), lambda i: (i, 0)),
            pl.BlockSpec((rows_per, BR_DIM), lambda i: (i, 0)),
            pl.BlockSpec((n_sub, N_HEADS, HEAD_DIM, HEAD_DIM), lambda i: (i, 0, 0, 0)),
        ],
        out_shape=[
            SDS((s, BR_DIM), F32),
            SDS((s, BR_DIM), MM_DTYPE),
            SDS((s // CHUNK, N_HEADS, HEAD_DIM, HEAD_DIM), F32),
        ],
        scratch_shapes=[pltpu.VMEM((N_HEADS, HEAD_DIM, HEAD_DIM), F32)],
        args=[z, z, lb, ng],
        semantics=("arbitrary",),
    )


def _hgrn_bwd(db, z, o, states, lb, ng, name, rider=None):
    s = z.shape[0]
    rows_per = _divisor(s, HG_ROWS, CHUNK)
    n_sub = rows_per // CHUNK
    n_steps = s // rows_per

    def body(db_ref, zqf_ref, zio_ref, o_ref, st_ref, lb_ref, ng_ref, dz_ref, dlb_ref, dng_ref, dstate):
        @pl.when(pl.program_id(0) == 0)
        def _():
            dstate[...] = jnp.zeros_like(dstate)
            dlb_ref[...] = jnp.zeros_like(dlb_ref)
            dng_ref[...] = jnp.zeros_like(dng_ref)

        low = _tri(True)
        ng_v = ng_ref[...]

        def chunk(cc, carry):
            ci = n_sub - 1 - cc
            rows = pl.ds(pl.multiple_of(ci * CHUNK, CHUNK), CHUNK)
            for h in range(N_HEADS):
                cs = slice(h * HEAD_DIM, (h + 1) * HEAD_DIM)
                cs2 = slice(BR_DIM + h * HEAD_DIM, BR_DIM + (h + 1) * HEAD_DIM)
                zq = zqf_ref[rows, cs]
                zf = zqf_ref[rows, cs2]
                vi = zio_ref[rows, cs]
                zog = zio_ref[rows, cs2]
                lbh = lb_ref[:, cs]
                ov = o_ref[rows, cs]
                dbv = db_ref[rows, cs]
                r = lax.rsqrt(jnp.mean(ov * ov, axis=-1, keepdims=True) + EPS)
                on = ov * r
                sgo = _sigmoid(zog)
                gate = zog * sgo
                dng_ref[...] += jnp.sum(dbv * gate * on, axis=0, keepdims=True)
                don = dbv * gate * ng_v
                dzog = dbv * (on * ng_v) * _silu_grad(zog, sgo)
                do = r * (don - on * jnp.mean(don * on, axis=-1, keepdims=True))
                sq = _sigmoid(zq)
                qf = zq * sq
                sg = _sigmoid(zf)
                sgn = _sigmoid(-zf)
                oml = 1.0 - lbh
                forget = lbh + oml * sg
                fm = jnp.maximum(forget, TINY)
                k = oml * sgn
                gc = _scan_rows(jnp.log(fm))
                gm = gc[MID : MID + 1]
                gl = gc[CHUNK - 1 : CHUNK]
                e_g = jnp.exp(gc)
                e_q = jnp.exp(gc - gm)
                e_k = jnp.exp(gm - gc)
                e_kd = jnp.exp(gl - gc)
                e_gl = jnp.exp(gl)
                qt = qf * e_q
                kt = k * e_k
                a = jnp.where(low, _dot_nt(qt, kt), 0.0)
                st = st_ref[ci, h]
                dst = dstate[h]
                dp = jnp.where(low, _dot_nt(do, vi), 0.0)
                dv = _dot_tn(a, do) + _dot_nt(k * e_kd, dst)
                dq = _dot(dp, kt) * e_q + e_g * _dot(do, st)
                dks = e_kd * _dot(vi, dst)
                dk = _dot_tn(dp, qt) * e_k + dks
                dstate[h] = _dot_tn(do, qf * e_g) + dst * e_gl
                dgc = qf * dq - k * dk
                extra = e_gl * jnp.sum(st * dst, axis=0, keepdims=True) + jnp.sum(k * dks, axis=0, keepdims=True)
                dg = _scan_rows(dgc, reverse=True) + extra
                dforget = jnp.where(forget > TINY, dg / fm, 0.0)
                both = dforget - dk
                dlb_ref[:, cs] += jnp.sum(sgn * both, axis=0, keepdims=True)
                dz_ref[rows, cs] = (dq * _silu_grad(zq, sq)).astype(MM_DTYPE)
                dz_ref[rows, cs2] = (oml * sg * sgn * both).astype(MM_DTYPE)
                dz_ref[rows, 2 * BR_DIM + h * HEAD_DIM : 2 * BR_DIM + (h + 1) * HEAD_DIM] = dv.astype(MM_DTYPE)
                dz_ref[rows, 3 * BR_DIM + h * HEAD_DIM : 3 * BR_DIM + (h + 1) * HEAD_DIM] = dzog.astype(MM_DTYPE)
            return carry

        lax.fori_loop(0, n_sub, chunk, 0, unroll=math.gcd(n_sub, HG_UNROLL))

    rev = lambda i: n_steps - 1 - i
    return _hosted_call(
        body,
        rider,
        name=name,
        grid=(n_steps,),
        in_specs=[
            pl.BlockSpec((rows_per, BR_DIM), lambda i: (rev(i), 0)),
            pl.BlockSpec((rows_per, 2 * BR_DIM), lambda i: (rev(i), 1)),
            pl.BlockSpec((rows_per, 2 * BR_DIM), lambda i: (rev(i), 2)),
            pl.BlockSpec((rows_per, BR_DIM), lambda i: (rev(i), 0)),
            pl.BlockSpec((n_sub, N_HEADS, HEAD_DIM, HEAD_DIM), lambda i: (rev(i), 0, 0, 0)),
            _resident((1, BR_DIM), lambda i: (0, 0)),
            _resident((1, HEAD_DIM), lambda i: (0, 0)),
        ],
        out_specs=[
            pl.BlockSpec((rows_per, 4 * BR_DIM), lambda i: (rev(i), 0)),
            pl.BlockSpec((1, BR_DIM), lambda i: (0, 0)),
            pl.BlockSpec((1, HEAD_DIM), lambda i: (0, 0)),
        ],
        out_shape=[SDS((s, 4 * BR_DIM), MM_DTYPE), SDS((1, BR_DIM), F32), SDS((1, HEAD_DIM), F32)],
        scratch_shapes=[pltpu.VMEM((N_HEADS, HEAD_DIM, HEAD_DIM), F32)],
        args=[db, z, z, o, states, lb, ng],
        semantics=("arbitrary",),
    )


def _mix_fwd(x, a, b, z, w_gm, w_hg, w_out, name, rider=None):
    s, d = x.shape
    tm = _divisor(s, 512, 16)
    g0 = 6 * BR_DIM // d

    def body(x_ref, a_ref, b_ref, ga_ref, gb_ref, wgm_ref, whg_ref, wout_ref, out_ref):
        y = _sigmoid(ga_ref[...]) * _dot(a_ref[...], wgm_ref[...]) + _sigmoid(gb_ref[...]) * _dot(b_ref[...], whg_ref[...])
        out_ref[...] = x_ref[...] + _dot(y, wout_ref[...])

    return _hosted_call(
        body,
        rider,
        name=name,
        grid=(s // tm,),
        in_specs=[
            pl.BlockSpec((tm, d), lambda i: (i, 0)),
            pl.BlockSpec((tm, BR_DIM), lambda i: (i, 0)),
            pl.BlockSpec((tm, BR_DIM), lambda i: (i, 0)),
            pl.BlockSpec((tm, d), lambda i: (i, g0)),
            pl.BlockSpec((tm, d), lambda i: (i, g0 + 1)),
            _resident((BR_DIM, d), lambda i: (0, 0)),
            _resident((BR_DIM, d), lambda i: (0, 0)),
            _resident((d, d), lambda i: (0, 0)),
        ],
        out_specs=[pl.BlockSpec((tm, d), lambda i: (i, 0))],
        out_shape=[SDS((s, d), F32)],
        scratch_shapes=[],
        args=[x, a, b, z, z, w_gm, w_hg, w_out],
        semantics=("arbitrary",),
    )


def _mix_bwd(dx, a, b, z, w_gm, w_hg, w_out, lng, lnb, ws, bst, name, rider=None):
    s, d = dx.shape
    tm = _divisor(s, 512, GM_BLOCK)
    g0 = 6 * BR_DIM // d

    def body(dx_ref, a_ref, b_ref, uv_ref, ga_ref, gb_ref, wgm_ref, whg_ref, wout_ref, lng_ref, lnb_ref, ws_ref, bst_ref,
             y_ref, dpa_ref, dpb_ref, db_ref, dg_ref, duv_ref, dlng_ref, dlnb_ref, dws_ref, dbst_ref):
        @pl.when(pl.program_id(0) == 0)
        def _():
            for ref in (dlng_ref, dlnb_ref, dws_ref, dbst_ref):
                ref[...] = jnp.zeros_like(ref)

        dy = _dot_nt(dx_ref[...], wout_ref[...])
        pa = _dot(a_ref[...], wgm_ref[...])
        pb = _dot(b_ref[...], whg_ref[...])
        sa = _sigmoid(ga_ref[...])
        sb = _sigmoid(gb_ref[...])
        y_ref[...] = (sa * pa + sb * pb).astype(MM_DTYPE)
        dpa = (dy * sa).astype(MM_DTYPE)
        dpb = (dy * sb).astype(MM_DTYPE)
        dpa_ref[...] = dpa
        dpb_ref[...] = dpb
        da = _dot_nt(dpa, wgm_ref[...])
        db_ref[...] = _dot_nt(dpb, whg_ref[...])
        dg_ref[:, :d] = (dy * pa * sa * (1.0 - sa)).astype(MM_DTYPE)
        dg_ref[:, d:] = (dy * pb * sb * (1.0 - sb)).astype(MM_DTYPE)
        _gmlp_bwd_tile(da, uv_ref[:, :BR_DIM], uv_ref[:, BR_DIM:], lng_ref, lnb_ref, ws_ref, bst_ref,
                       duv_ref, dlng_ref, dlnb_ref, dws_ref, dbst_ref)

    row = lambda w: pl.BlockSpec((tm, w), lambda i: (i, 0))
    fixed = lambda shape: pl.BlockSpec(shape, lambda i: tuple(0 for _ in shape))
    return _hosted_call(
        body,
        rider,
        name=name,
        grid=(s // tm,),
        in_specs=[
            row(d),
            row(BR_DIM),
            row(BR_DIM),
            row(2 * BR_DIM),
            pl.BlockSpec((tm, d), lambda i: (i, g0)),
            pl.BlockSpec((tm, d), lambda i: (i, g0 + 1)),
            _resident((BR_DIM, d), lambda i: (0, 0)),
            _resident((BR_DIM, d), lambda i: (0, 0)),
            _resident((d, d), lambda i: (0, 0)),
        ]
        + _gmlp_param_specs(),
        out_specs=[row(d), row(d), row(d), row(BR_DIM), row(2 * d), row(2 * BR_DIM), fixed((1, BR_DIM)), fixed((1, BR_DIM)),
                   fixed((N_HEADS, GM_BLOCK, GM_BLOCK)), fixed((GM_BLOCK, N_HEADS))],
        out_shape=[
            SDS((s, d), MM_DTYPE),
            SDS((s, d), MM_DTYPE),
            SDS((s, d), MM_DTYPE),
            SDS((s, BR_DIM), F32),
            SDS((s, 2 * d), MM_DTYPE),
            SDS((s, 2 * BR_DIM), MM_DTYPE),
            SDS((1, BR_DIM), F32),
            SDS((1, BR_DIM), F32),
            SDS((N_HEADS, GM_BLOCK, GM_BLOCK), F32),
            SDS((GM_BLOCK, N_HEADS), F32),
        ],
        scratch_shapes=[],
        args=[dx, a, b, z, z, z, w_gm, w_hg, w_out, lng, lnb, ws, bst],
        semantics=("arbitrary",),
    )


HALO = 8
FFN_ROWS = 256


def _shift_down(v, prev, n):
    rolled = pltpu.roll(v, n, 0)
    rid = lax.broadcasted_iota(jnp.int32, v.shape, 0)
    out = rolled
    for r in range(n):
        out = jnp.where(rid == r, prev[HALO - n + r : HALO - n + r + 1], out)
    return out


def _shift_up(v, nxt, n):
    tm = v.shape[0]
    rolled = pltpu.roll(v, tm - n, 0)
    rid = lax.broadcasted_iota(jnp.int32, v.shape, 0)
    out = rolled
    for r in range(n):
        out = jnp.where(rid == tm - n + r, nxt[r : r + 1], out)
    return out


def _conv_cols(f):
    return _divisor(f, 256, 128)


def _ffn_fwd(x, gain, w_up, cw, cb, w_down, layer, name, rider=None):
    s, d = x.shape
    f = w_down.shape[0]
    tm = _divisor(s, FFN_ROWS, 16)
    cwid = _conv_cols(f)

    def body(x_ref, g_ref, wu_ref, cw_ref, cb_ref, wd_ref, out_ref, z_ref, h_ref, conv_ref, halo):
        @pl.when(pl.program_id(0) == 0)
        def _():
            halo[...] = jnp.zeros_like(halo)

        xv = x_ref[...]
        r = lax.rsqrt(jnp.mean(xv * xv, axis=-1, keepdims=True) + EPS)
        h = ((xv * r) * g_ref[...]).astype(MM_DTYPE)
        h_ref[...] = h
        acc = xv
        starts = list(range(0, f, cwid))

        def project(c0):
            return [jnp.dot(h, wu_ref[:, base : base + cwid], preferred_element_type=F32) for base in (c0, f + c0)]

        ahead = project(starts[0])
        for n, c0 in enumerate(starts):
            halves = []
            current = ahead
            if n + 1 < len(starts):
                ahead = project(starts[n + 1])
            for base, zc in zip((c0, f + c0), current):
                cols = slice(base, base + cwid)
                z_ref[:, cols] = zc.astype(MM_DTYPE)
                prev = halo[:, cols]
                conv = cb_ref[:, cols] + cw_ref[0:1, cols] * _shift_down(zc, prev, 2)
                conv = conv + cw_ref[1:2, cols] * _shift_down(zc, prev, 1) + cw_ref[2:3, cols] * zc
                halo[:, cols] = zc[tm - HALO : tm]
                conv_ref[:, cols] = conv.astype(MM_DTYPE)
                halves.append(conv)
            gate, val = halves
            act = gate * _sigmoid(gate) * val
            acc = acc + _dot(act, wd_ref[c0 : c0 + cwid, :])
        out_ref[...] = acc

    row = lambda w: pl.BlockSpec((tm, w), lambda i: (i, 0))
    return _hosted_call(
        body,
        rider,
        name=name,
        grid=(s // tm,),
        in_specs=[
            row(d),
            _resident((1, d), lambda i: (0, 0)),
            _resident((d, 2 * f), lambda i: (0, 0)),
            _resident((None, 3, 2 * f), lambda i: (layer, 0, 0)),
            _resident((1, 2 * f), lambda i: (0, 0)),
            _resident((f, d), lambda i: (0, 0)),
        ],
        out_specs=[row(d), row(2 * f), row(d), row(2 * f)],
        out_shape=[SDS((s, d), F32), SDS((s, 2 * f), MM_DTYPE), SDS((s, d), MM_DTYPE), SDS((s, 2 * f), MM_DTYPE)],
        scratch_shapes=[pltpu.VMEM((HALO, 2 * f), F32)],
        args=[x, gain, w_up, cw, cb, w_down],
        semantics=("arbitrary",),
    )


def _ffn_bwd(dx, z2, conv, cw, w_down, w_up, x, gain, layer, name):
    s, d = dx.shape
    f = z2.shape[1] // 2
    tm = _divisor(s, 256, 16)
    cwid = _conv_cols(f)
    n_steps = s // tm

    def body(dx_ref, z_ref, conv_ref, cw_ref, wd_ref, wu_ref, x_ref, g_ref, dz_ref, act_ref, dcw_ref, dx1_ref, dg_ref, nxt):
        @pl.when(pl.program_id(0) == 0)
        def _():
            nxt[...] = jnp.zeros_like(nxt)
            dcw_ref[...] = jnp.zeros_like(dcw_ref)
            dg_ref[...] = jnp.zeros_like(dg_ref)

        dxf = dx_ref[...]
        dxv = dxf.astype(MM_DTYPE)
        starts = list(range(0, f, cwid))
        ahead = _dot_nt(dxv, wd_ref[0:cwid, :])
        dh = jnp.zeros((tm, d), F32)
        pending = []
        for n, c0 in enumerate(starts):
            dact = ahead
            if n + 1 < len(starts):
                ahead = _dot_nt(dxv, wd_ref[starts[n + 1] : starts[n + 1] + cwid, :])
            for dz_prev, cols_prev in pending:
                dh = dh + _dot_nt(dz_prev, wu_ref[:, cols_prev])
            pending = []
            gate = conv_ref[:, c0 : c0 + cwid].astype(F32)
            val = conv_ref[:, f + c0 : f + c0 + cwid].astype(F32)
            sg = _sigmoid(gate)
            silu = gate * sg
            act_ref[:, c0 : c0 + cwid] = (silu * val).astype(MM_DTYPE)
            dconvs = (dact * val * _silu_grad(gate, sg), dact * silu)
            for base, dconv in zip((c0, f + c0), dconvs):
                cols = slice(base, base + cwid)
                zc = z_ref[:, cols].astype(F32)
                nx = nxt[:, cols]
                up1 = _shift_up(dconv, nx, 1)
                up2 = _shift_up(dconv, nx, 2)
                dcw_ref[0:1, cols] += jnp.sum(up2 * zc, axis=0, keepdims=True)
                dcw_ref[1:2, cols] += jnp.sum(up1 * zc, axis=0, keepdims=True)
                dcw_ref[2:3, cols] += jnp.sum(dconv * zc, axis=0, keepdims=True)
                dcw_ref[3:4, cols] += jnp.sum(dconv, axis=0, keepdims=True)
                dz = (cw_ref[2:3, cols] * dconv + cw_ref[1:2, cols] * up1 + cw_ref[0:1, cols] * up2).astype(MM_DTYPE)
                dz_ref[:, cols] = dz
                nxt[:, cols] = dconv[0:HALO]
                pending.append((dz, cols))
        for dz_prev, cols_prev in pending:
            dh = dh + _dot_nt(dz_prev, wu_ref[:, cols_prev])
        xv = x_ref[...]
        r = lax.rsqrt(jnp.mean(xv * xv, axis=-1, keepdims=True) + EPS)
        xh = xv * r
        dg_ref[...] += jnp.sum(dh * xh, axis=0, keepdims=True)
        dxh = dh * g_ref[...]
        dx1_ref[...] = dxf + r * (dxh - xh * jnp.mean(dxh * xh, axis=-1, keepdims=True))

    rev = lambda i: n_steps - 1 - i
    row = lambda w: pl.BlockSpec((tm, w), lambda i: (rev(i), 0))
    return pl.pallas_call(
        body,
        name=name,
        grid=(n_steps,),
        in_specs=[
            row(d),
            row(2 * f),
            row(2 * f),
            _resident((None, 3, 2 * f), lambda i: (layer, 0, 0)),
            _resident((f, d), lambda i: (0, 0)),
            _resident((d, 2 * f), lambda i: (0, 0)),
            row(d),
            _resident((1, d), lambda i: (0, 0)),
        ],
        out_specs=[
            row(2 * f),
            row(f),
            pl.BlockSpec((HALO, 2 * f), lambda i: (0, 0)),
            row(d),
            pl.BlockSpec((1, d), lambda i: (0, 0)),
        ],
        out_shape=[SDS((s, 2 * f), MM_DTYPE), SDS((s, f), MM_DTYPE), SDS((HALO, 2 * f), F32), SDS((s, d), F32), SDS((1, d), F32)],
        scratch_shapes=[pltpu.VMEM((HALO, 2 * f), F32)],
        compiler_params=_params("arbitrary"),
    )(dx, z2, conv, cw, w_down, w_up, x, gain)


def _loss_head(x, gain, target):
    s, d = x.shape
    tm = _divisor(s, 256, 8)

    def body(x_ref, g_ref, t_ref, loss_ref, dx_ref, dg_ref):
        @pl.when(pl.program_id(0) == 0)
        def _():
            loss_ref[...] = jnp.zeros_like(loss_ref)
            dg_ref[...] = jnp.zeros_like(dg_ref)

        xv = x_ref[...]
        gv = g_ref[...]
        r = lax.rsqrt(jnp.mean(xv * xv, axis=-1, keepdims=True) + EPS)
        xh = xv * r
        err = xh * gv - t_ref[...]
        loss_ref[...] += 0.5 * jnp.sum(jnp.mean(err * err, axis=-1, keepdims=True))
        dout = err * (1.0 / d)
        dg_ref[...] += jnp.sum(dout * xh, axis=0, keepdims=True)
        dxh = dout * gv
        dx_ref[...] = r * (dxh - xh * jnp.mean(dxh * xh, axis=-1, keepdims=True))

    return pl.pallas_call(
        body,
        name="loss_head",
        grid=(s // tm,),
        in_specs=[
            pl.BlockSpec((tm, d), lambda i: (i, 0)),
            _resident((1, d), lambda i: (0, 0)),
            pl.BlockSpec((tm, d), lambda i: (i, 0)),
        ],
        out_specs=[
            pl.BlockSpec((8, 128), lambda i: (0, 0)),
            pl.BlockSpec((tm, d), lambda i: (i, 0)),
            pl.BlockSpec((1, d), lambda i: (0, 0)),
        ],
        out_shape=[SDS((8, 128), F32), SDS((s, d), F32), SDS((1, d), F32)],
        compiler_params=_params("arbitrary"),
    )(x, gain, target)


def _adamw(w, g, m, v, name):
    shape = w.shape
    cols = shape[-1]
    rows = max(1, math.prod(shape[:-1]))
    tr = _divisor(rows, max(8, TILE_BYTES // (4 * cols)), 8)
    flat = [t.reshape(rows, cols) for t in (w, g, m, v)]

    def body(w_ref, g_ref, m_ref, v_ref, d_ref, nm_ref, nv_ref):
        gv = g_ref[...]
        m2 = ADAM_B1 * m_ref[...] + (1.0 - ADAM_B1) * gv
        v2 = ADAM_B2 * v_ref[...] + (1.0 - ADAM_B2) * (gv * gv)
        m_hat = m2 / (1.0 - ADAM_B1**ADAM_STEP)
        v_hat = v2 / (1.0 - ADAM_B2**ADAM_STEP)
        d_ref[...] = -ADAM_LR * (m_hat / (jnp.sqrt(v_hat) + ADAM_EPS) + ADAM_WD * w_ref[...])
        nm_ref[...] = m2
        nv_ref[...] = v2

    spec = pl.BlockSpec((tr, cols), lambda i: (i, 0))
    outs = pl.pallas_call(
        body,
        name=name,
        grid=(rows // tr,),
        in_specs=[spec] * 4,
        out_specs=[spec] * 3,
        out_shape=[SDS((rows, cols), F32)] * 3,
        compiler_params=_params("parallel"),
    )(*flat)
    return tuple(t.reshape(shape) for t in outs)


def _position():
    return lax.axis_index("x"), lax.axis_index("y"), lax.axis_index("c")


def _other_chips(x, y):
    return [(1 - x, y), (x, 1 - y), (1 - x, 1 - y)]


def _remote(src, dst, send_sem, recv_sem, device):
    return pltpu.make_async_remote_copy(
        src_ref=src, dst_ref=dst, send_sem=send_sem, recv_sem=recv_sem, device_id=device, device_id_type=MESH
    )


def _sub(ref, lead, shape, axis, chip=None, half=None):
    cut = [pl.ds(0, shape[0]), pl.ds(0, shape[1])]
    if chip is not None:
        size = shape[axis] // N_CHIPS
        cut[axis] = pl.ds(chip * size, size)
    if half is not None:
        size = shape[1 - axis] // 2
        cut[1 - axis] = pl.ds(half * size, size)
    return ref.at[(*lead, *cut)]


def _scaled(shape, axis, num, den=1):
    return tuple(d * num // den if i == axis else d for i, d in enumerate(shape))


def _gather_rider(shards, axes, layer, conv_w=None):
    n = len(shards)
    shard2d = [t.shape[1:] for t in shards]
    full2d = [_scaled(s2, ax, N_CHIPS) for s2, ax in zip(shard2d, axes)]
    out_shapes = [SDS(f2, t.dtype) for f2, t in zip(full2d, shards)]
    inputs = list(shards)
    own = 6
    scratch = [pltpu.SemaphoreType.DMA((n, 7)), pltpu.SemaphoreType.DMA((n, 7))]
    if conv_w is not None:
        cshape = conv_w.shape
        inputs.append(conv_w)
        out_shapes.append(SDS(_scaled(cshape, 2, N_CHIPS), conv_w.dtype))
        scratch += [pltpu.SemaphoreType.DMA((4,)), pltpu.SemaphoreType.DMA((4,))]

    def conv_slab(ref, chip):
        return ref.at[:, :, pl.ds((2 * chip[0] + chip[1]) * cshape[2], cshape[2])]

    def own_copy(ins, outs, send_sems, recv_sems, k):
        x, y, c = _position()
        slab = _sub(outs[k], (), full2d[k], axes[k], chip=2 * x + y)
        return _remote(ins[k].at[layer], slab, send_sems.at[k, own], recv_sems.at[k, own], (x, y, 1 - c))

    def start(ins, outs, scr):
        send_sems, recv_sems = scr[:2]
        x, y, c = _position()
        me = 2 * x + y
        for k in range(n):
            own_copy(ins, outs, send_sems, recv_sems, k).start()
            for j, chip in enumerate(_other_chips(x, y)):
                _remote(
                    _sub(ins[k], (layer,), shard2d[k], axes[k], half=c),
                    _sub(outs[k], (), full2d[k], axes[k], chip=me, half=c),
                    send_sems.at[k, j], recv_sems.at[k, j], (*chip, c),
                ).start()
        if conv_w is not None:
            csend, crecv = scr[2:]
            _remote(ins[n], conv_slab(outs[n], (x, y)), csend.at[3], crecv.at[3], (x, y, 1 - c)).start()
            for j, chip in enumerate(_other_chips(x, y)):
                _remote(ins[n], conv_slab(outs[n], (x, y)), csend.at[j], crecv.at[j], (*chip, c)).start()

    def finish(ins, outs, scr):
        send_sems, recv_sems = scr[:2]
        x, y, c = _position()
        me = 2 * x + y
        sibling = (x, y, 1 - c)
        chips = _other_chips(x, y)
        forwards = []
        for k in range(n):
            src = _sub(ins[k], (layer,), shard2d[k], axes[k], half=c)
            for j, chip in enumerate(chips):
                landed = _sub(outs[k], (), full2d[k], axes[k], chip=2 * chip[0] + chip[1], half=c)
                _remote(src, landed, send_sems.at[k, j], recv_sems.at[k, j], (*chip, c)).wait_recv()
                fwd = _remote(landed, landed, send_sems.at[k, 3 + j], recv_sems.at[k, 3 + j], sibling)
                fwd.start()
                forwards.append(fwd)
        for k in range(n):
            src = _sub(ins[k], (layer,), shard2d[k], axes[k], half=c)
            for j, chip in enumerate(chips):
                theirs = _sub(outs[k], (), full2d[k], axes[k], chip=2 * chip[0] + chip[1], half=1 - c)
                _remote(src, theirs, send_sems.at[k, 3 + j], recv_sems.at[k, 3 + j], sibling).wait_recv()
        for fwd in forwards:
            fwd.wait_send()
        for k in range(n):
            src = _sub(ins[k], (layer,), shard2d[k], axes[k], half=c)
            mine = _sub(outs[k], (), full2d[k], axes[k], chip=me, half=c)
            for j, chip in enumerate(chips):
                _remote(src, mine, send_sems.at[k, j], recv_sems.at[k, j], (*chip, c)).wait_send()
            own_copy(ins, outs, send_sems, recv_sems, k).wait()
        if conv_w is not None:
            csend, crecv = scr[2:]
            for j, chip in enumerate(chips):
                cp = _remote(ins[n], conv_slab(outs[n], chip), csend.at[j], crecv.at[j], (*chip, c))
                cp.wait_recv()
                cp.wait_send()
            _remote(ins[n], conv_slab(outs[n], (x, y)), csend.at[3], crecv.at[3], sibling).wait()

    return _Rider(inputs, out_shapes, scratch, start, finish)


def _pair_rider(stacks, axes, layer):
    n = len(stacks)
    shapes = [t.shape[1:] for t in stacks]
    out_shapes = [SDS(_scaled(s2, 1 - ax, 1, 2), F32) for s2, ax in zip(shapes, axes)]
    scratch = [pltpu.SemaphoreType.DMA((n,)), pltpu.SemaphoreType.DMA((n,))]

    def copies(ins, outs, scr):
        x, y, c = _position()
        return [
            _remote(_sub(ins[k], (layer,), shapes[k], axes[k], half=1 - c), outs[k], scr[0].at[k], scr[1].at[k], (x, y, 1 - c))
            for k in range(n)
        ]

    def start(ins, outs, scr):
        for cp in copies(ins, outs, scr):
            cp.start()

    def finish(ins, outs, scr):
        for cp in copies(ins, outs, scr):
            cp.wait()

    return _Rider(stacks, out_shapes, scratch, start, finish)


def _chip_rider(pairs, axes):
    n = len(pairs)
    shapes = [t.shape for t in pairs]
    out_shapes = [SDS((3,) + _scaled(s2, ax, 1, N_CHIPS), t.dtype) for s2, ax, t in zip(shapes, axes, pairs)]
    scratch = [pltpu.SemaphoreType.DMA((n, 3)), pltpu.SemaphoreType.DMA((n, 3))]

    def copies(ins, outs, scr):
        x, y, c = _position()
        return [
            _remote(
                _sub(ins[k], (), shapes[k], axes[k], chip=2 * chip[0] + chip[1]), outs[k].at[j],
                scr[0].at[k, j], scr[1].at[k, j], (*chip, c),
            )
            for k in range(n)
            for j, chip in enumerate(_other_chips(x, y))
        ]

    def start(ins, outs, scr):
        for cp in copies(ins, outs, scr):
            cp.start()

    def finish(ins, outs, scr):
        for cp in copies(ins, outs, scr):
            cp.wait()

    return _Rider(pairs, out_shapes, scratch, start, finish)


def _complete_rider(stacks, axes, layers):
    n = len(stacks)
    shapes = [t.shape[1:] for t in stacks]
    scratch = [pltpu.SemaphoreType.DMA((n,)), pltpu.SemaphoreType.DMA((n,))]

    def start(ins, outs, scr):
        x, y, c = _position()
        for k in range(n):
            mine = _sub(outs[k], (layers[k],), shapes[k], axes[k], half=c)
            _remote(mine, mine, scr[0].at[k], scr[1].at[k], (x, y, 1 - c)).start()

    def finish(ins, outs, scr):
        x, y, c = _position()
        for k in range(n):
            mine = _sub(outs[k], (layers[k],), shapes[k], axes[k], half=c)
            theirs = _sub(outs[k], (layers[k],), shapes[k], axes[k], half=1 - c)
            _remote(mine, theirs, scr[0].at[k], scr[1].at[k], (x, y, 1 - c)).wait_recv()
            _remote(mine, mine, scr[0].at[k], scr[1].at[k], (x, y, 1 - c)).wait_send()

    return _Rider(stacks, [SDS(t.shape, t.dtype) for t in stacks], scratch, start, finish, {k: k for k in range(n)})


def _small_rider(buf):
    rows, cols = buf.shape
    flips = [(fx, fy, fc) for fx in (0, 1) for fy in (0, 1) for fc in (0, 1)][1:]

    def peers():
        x, y, c = _position()
        return [(x + f[0] - 2 * x * f[0], y + f[1] - 2 * y * f[1], c + f[2] - 2 * c * f[2]) for f in flips]

    def start(ins, outs, scr):
        x, y, c = _position()
        mine = outs[0].at[4 * x + 2 * y + c]
        pltpu.make_async_copy(ins[0], mine, scr[2]).start()
        for j, peer in enumerate(peers()):
            _remote(ins[0], mine, scr[0].at[j], scr[1].at[j], peer).start()

    def finish(ins, outs, scr):
        x, y, c = _position()
        mine = outs[0].at[4 * x + 2 * y + c]
        for j, peer in enumerate(peers()):
            _remote(ins[0], outs[0].at[4 * peer[0] + 2 * peer[1] + peer[2]], scr[0].at[j], scr[1].at[j], peer).wait_recv()
        for j, peer in enumerate(peers()):
            _remote(ins[0], mine, scr[0].at[j], scr[1].at[j], peer).wait_send()
        pltpu.make_async_copy(ins[0], mine, scr[2]).wait()

    scratch = [pltpu.SemaphoreType.DMA((7,)), pltpu.SemaphoreType.DMA((7,)), pltpu.SemaphoreType.DMA(())]
    return _Rider([buf], [SDS((8, rows, cols), buf.dtype)], scratch, start, finish)


def _my_core():
    return lax.axis_index("c")


def _my_chip():
    return 2 * lax.axis_index("x") + lax.axis_index("y")


def _pair_sum(stack, layer, recv, axis, name):
    hk, hn = recv.shape
    tk = _divisor(hk, max(16, TILE_BYTES // (4 * hn)), 16)
    per = hk // tk

    def body(g_ref, r_ref, out_ref):
        out_ref[...] = (g_ref[...] + r_ref[...]).astype(out_ref.dtype)

    if axis == 1:
        mine = pl.BlockSpec((None, tk, hn), lambda i: (layer, _my_core() * per + i, 0))
    else:
        mine = pl.BlockSpec((None, tk, hn), lambda i: (layer, i, _my_core()))
    return pl.pallas_call(
        body,
        name=name,
        grid=(per,),
        in_specs=[mine, pl.BlockSpec((tk, hn), lambda i: (i, 0))],
        out_specs=pl.BlockSpec((tk, hn), lambda i: (i, 0)),
        out_shape=SDS((hk, hn), BF16),
        compiler_params=_params("parallel"),
    )(stack, recv)


def _chip_sum(pair, others, axis, stack, layer, name):
    _, sk, sn = others.shape
    tk = _divisor(sk, max(16, TILE_BYTES // (4 * sn)), 16)
    per = sk // tk

    def body(p_ref, o0_ref, o1_ref, o2_ref, *rest):
        out_ref = rest[-1]
        acc = p_ref[...].astype(F32) + o0_ref[...].astype(F32)
        out_ref[...] = (acc + o1_ref[...].astype(F32)) + o2_ref[...].astype(F32)

    if axis == 1:
        own = pl.BlockSpec((tk, sn), lambda i: (i, _my_chip()))
        out = pl.BlockSpec((None, tk, sn), lambda i: (layer, _my_core() * per + i, 0))
        shard = (2 * sk, sn)
    else:
        own = pl.BlockSpec((tk, sn), lambda i: (_my_chip() * per + i, 0))
        out = pl.BlockSpec((None, tk, sn), lambda i: (layer, i, _my_core()))
        shard = (sk, 2 * sn)
    other = lambda j: pl.BlockSpec((None, tk, sn), lambda i: (j, i, 0))
    in_specs = [own, other(0), other(1), other(2)]
    args = [pair, others, others, others]
    aliases = {}
    if stack is not None:
        in_specs.append(ANY)
        args.append(stack)
        aliases = {4: 0}
    return pl.pallas_call(
        body,
        name=name,
        grid=(per,),
        in_specs=in_specs,
        out_specs=out,
        out_shape=SDS((DEPTH,) + shard, F32),
        input_output_aliases=aliases,
        compiler_params=_params("parallel"),
    )(*args)


def _sum_devices(gathered, name):
    _, rows, cols = gathered.shape

    def body(g_ref, out_ref):
        acc = g_ref[0]
        for dev in range(1, 8):
            acc = acc + g_ref[dev]
        out_ref[...] = acc

    return pl.pallas_call(body, name=name, out_shape=SDS((rows, cols), F32))(gathered)


PACK_TILE = 8 * 128


def _pack(arrays):
    parts = []
    for t in arrays:
        flat = t.reshape(-1)
        pad = (-flat.shape[0]) % PACK_TILE
        if pad:
            flat = jnp.concatenate([flat, jnp.zeros((pad,), flat.dtype)])
        parts.append(flat.reshape(-1, 128))
    return jnp.concatenate(parts, axis=0)


def _unpack(buf, shapes):
    out, row = [], 0
    for shp in shapes:
        size = math.prod(shp)
        rows = -(-size // PACK_TILE) * 8
        out.append(buf[row : row + rows].reshape(-1)[:size].reshape(shp))
        row += rows
    return out


BIG = ("w_in", "w_br_gm", "w_br_hg", "w_out", "w_up", "w_down")
BIG_AXIS = (1, 1, 1, 0, 1, 0)
LAYER_SMALL = ("mix_norm", "gm_ln_g", "gm_ln_b", "gm_ws", "gm_bs", "lb", "hg_norm_g", "ffn_norm", "conv_w", "conv_b")
WEIGHTS = ("mix_norm", "w_in", "gm_ln_g", "gm_ln_b", "gm_ws", "gm_bs", "hg_lb_logits", "hg_norm_g", "w_br_gm", "w_br_hg", "w_out",
           "ffn_norm", "w_up", "conv_w", "conv_b", "w_down", "final_norm")


def kernel(x, mix_norm, w_in, gm_ln_g, gm_ln_b, gm_ws, gm_bs, hg_lb_logits, hg_norm_g, w_br_gm, w_br_hg, w_out, ffn_norm, w_up, conv_w, conv_b, w_down, final_norm, loss_target, m_mix_norm, m_w_in, m_gm_ln_g, m_gm_ln_b, m_gm_ws, m_gm_bs, m_hg_lb_logits, m_hg_norm_g, m_w_br_gm, m_w_br_hg, m_w_out, m_ffn_norm, m_w_up, m_conv_w, m_conv_b, m_w_down, m_final_norm, v_mix_norm, v_w_in, v_gm_ln_g, v_gm_ln_b, v_gm_ws, v_gm_bs, v_hg_lb_logits, v_hg_norm_g, v_w_br_gm, v_w_br_hg, v_w_out, v_ffn_norm, v_w_up, v_conv_w, v_conv_b, v_w_down, v_final_norm):
    w = dict(mix_norm=mix_norm, w_in=w_in, gm_ln_g=gm_ln_g, gm_ln_b=gm_ln_b, gm_ws=gm_ws, gm_bs=gm_bs, hg_lb_logits=hg_lb_logits,
             hg_norm_g=hg_norm_g, w_br_gm=w_br_gm, w_br_hg=w_br_hg, w_out=w_out, ffn_norm=ffn_norm, w_up=w_up, conv_w=conv_w,
             conv_b=conv_b, w_down=w_down, final_norm=final_norm)
    m = dict(mix_norm=m_mix_norm, w_in=m_w_in, gm_ln_g=m_gm_ln_g, gm_ln_b=m_gm_ln_b, gm_ws=m_gm_ws, gm_bs=m_gm_bs,
             hg_lb_logits=m_hg_lb_logits, hg_norm_g=m_hg_norm_g, w_br_gm=m_w_br_gm, w_br_hg=m_w_br_hg, w_out=m_w_out,
             ffn_norm=m_ffn_norm, w_up=m_w_up, conv_w=m_conv_w, conv_b=m_conv_b, w_down=m_w_down, final_norm=m_final_norm)
    v = dict(mix_norm=v_mix_norm, w_in=v_w_in, gm_ln_g=v_gm_ln_g, gm_ln_b=v_gm_ln_b, gm_ws=v_gm_ws, gm_bs=v_gm_bs,
             hg_lb_logits=v_hg_lb_logits, hg_norm_g=v_hg_norm_g, w_br_gm=v_w_br_gm, w_br_hg=v_w_br_hg, w_out=v_w_out,
             ffn_norm=v_ffn_norm, w_up=v_w_up, conv_w=v_conv_w, conv_b=v_conv_b, w_down=v_w_down, final_norm=v_final_norm)

    axes = list(BIG_AXIS)
    d = x.shape[2]

    shards = [w[k].astype(MM_DTYPE) for k in BIG]
    w_in_0, conv_full = _run_rider(_gather_rider(shards[:1], axes[:1], 0, conv_w=conv_w), "gather_weights_0")
    full = [dict(w_in=w_in_0)]
    lb = _lower_bounds(hg_lb_logits)
    xs = x[0]
    saved = []
    mixer = 4
    for l in range(DEPTH):
        fw = full[l]
        more = l + 1 < DEPTH
        bst = gm_bs[l].T
        (z, h, a), got = _in_proj(
            xs, mix_norm[l : l + 1], fw["w_in"], gm_ln_g[l : l + 1], gm_ln_b[l : l + 1], gm_ws[l], bst, f"in_proj_{l}",
            _gather_rider(shards[mixer:], axes[mixer:], l),
        )
        fw.update(zip(BIG[mixer:], got))
        rider = _gather_rider(shards[1:mixer], axes[1:mixer], 0) if l == 0 else None
        (o, b, states), got = _hgrn_fwd(z, lb[l : l + 1], hg_norm_g[l : l + 1], f"hgrn_fwd_{l}", rider)
        if l == 0:
            fw.update(zip(BIG[1:mixer], got))
        (x1,), _ = _mix_fwd(xs, a, b, z, fw["w_br_gm"], fw["w_br_hg"], fw["w_out"], f"mix_fwd_{l}")
        rider = _gather_rider(shards[:mixer], axes[:mixer], l + 1) if more else None
        (x2, z2, h2, conv), got = _ffn_fwd(
            x1, ffn_norm[l : l + 1], fw["w_up"], conv_full, conv_b[l : l + 1], fw["w_down"], l, f"ffn_fwd_{l}", rider
        )
        if more:
            full.append(dict(zip(BIG[:mixer], got)))
        saved.append((xs, h, z, a, b, o, states, x1, h2, z2, conv, bst))
        xs = x2

    loss_tile, dx, d_final = _loss_head(xs, final_norm.reshape(1, d), loss_target[0])
    loss = lax.psum(loss_tile[0, 0], ("x", "y", "c"))

    big = {k: None for k in BIG}
    grads = [None] * len(BIG)
    per_layer = [None] * DEPTH
    mix_ids, ffn_ids = list(range(mixer)), list(range(mixer, len(BIG)))
    mix_axes, ffn_axes = axes[:mixer], axes[mixer:]
    mix_pairs = None
    mix_summed = None

    def pair_sums(ids, received, layer):
        return [_pair_sum(big[BIG[k]], layer, r, axes[k], f"pair_sum_{BIG[k]}_{layer}") for k, r in zip(ids, received)]

    def chip_sums(ids, pairs, others, layer):
        for k, p, ot in zip(ids, pairs, others):
            grads[k] = _chip_sum(p, ot, axes[k], grads[k], layer, f"chip_sum_{BIG[k]}_{layer}")

    for l in reversed(range(DEPTH)):
        x0, h, z, a, b, o, states, x1, h2, z2, conv, bst = saved[l]
        fw = full[l]
        dz2, act, dconv, dx1, d_ffn = _ffn_bwd(
            dx, z2, conv, conv_full, fw["w_down"], fw["w_up"], x1, ffn_norm[l : l + 1], l, f"ffn_bwd_{l}"
        )
        big["w_down"] = _matmul_tn(act, dx, big["w_down"], l, f"dw_down_{l}")
        big["w_up"] = _matmul_tn(h2, dz2, big["w_up"], l, f"dw_up_{l}")
        rider = _pair_rider([big[BIG[k]] for k in ffn_ids], ffn_axes, l)
        if mix_pairs is not None:
            rider = _join(rider, _chip_rider(mix_pairs, mix_axes))
        (y, dpa, dpb, db, dgates, dz_gm, d_lng, d_lnb, d_ws, d_bst), got = _mix_bwd(
            dx1, a, b, z, fw["w_br_gm"], fw["w_br_hg"], fw["w_out"], gm_ln_g[l : l + 1], gm_ln_b[l : l + 1], gm_ws[l], bst,
            f"mix_bwd_{l}", rider,
        )
        if mix_pairs is not None:
            chip_sums(mix_ids, mix_pairs, got[len(ffn_ids) :], l + 1)
            mix_summed = l + 1
        ffn_pairs = pair_sums(ffn_ids, got[: len(ffn_ids)], l)
        big["w_out"] = _matmul_tn(y, dx1, big["w_out"], l, f"dw_out_{l}")
        big["w_br_gm"] = _matmul_tn(a, dpa, big["w_br_gm"], l, f"dw_br_gm_{l}")
        big["w_br_hg"] = _matmul_tn(b, dpb, big["w_br_hg"], l, f"dw_br_hg_{l}")
        (dz_hg, d_lb, d_ng), others = _hgrn_bwd(
            db, z, o, states, lb[l : l + 1], hg_norm_g[l : l + 1], f"hgrn_bwd_{l}", _chip_rider(ffn_pairs, ffn_axes)
        )
        chip_sums(ffn_ids, ffn_pairs, others, l)
        n_in = fw["w_in"].shape[1]
        big["w_in"] = _matmul_tn(h, dz_gm, big["w_in"], l, f"dw_in_gm_{l}", n_total=n_in, col_off=0)
        big["w_in"] = _matmul_tn(h, dz_hg, big["w_in"], l, f"dw_in_hg_{l}", n_total=n_in, col_off=2 * BR_DIM)
        big["w_in"] = _matmul_tn(h, dgates, big["w_in"], l, f"dw_in_gate_{l}", n_total=n_in, col_off=6 * BR_DIM)
        done_ids = ffn_ids + (mix_ids if mix_summed is not None else [])
        done_layers = [l] * len(ffn_ids) + ([mix_summed] * len(mix_ids) if mix_summed is not None else [])
        rider = _join(
            _pair_rider([big[BIG[k]] for k in mix_ids], mix_axes, l),
            _complete_rider([grads[k] for k in done_ids], [axes[k] for k in done_ids], done_layers),
        )
        if l == 0:
            upper = [_pack([per_layer[j][k] for k in LAYER_SMALL]) for j in range(1, DEPTH)] + [_pack([d_final[0]])]
            rider = _join(rider, _small_rider(jnp.concatenate(upper, axis=0)))
        (dx, d_mix), got = _in_proj_bwd([dz_gm, dz_hg, dgates], fw["w_in"], x0, mix_norm[l : l + 1], dx1, f"in_proj_bwd_{l}", rider)
        for k, g in zip(done_ids, got[len(mix_ids) : len(mix_ids) + len(done_ids)]):
            grads[k] = g
        gathered_upper = got[len(mix_ids) + len(done_ids) :]
        mix_pairs = pair_sums(mix_ids, got[: len(mix_ids)], l)
        per_layer[l] = dict(
            mix_norm=d_mix[0], gm_ln_g=d_lng[0], gm_ln_b=d_lnb[0], gm_ws=d_ws, gm_bs=d_bst.T, lb=d_lb[0], hg_norm_g=d_ng[0],
            ffn_norm=d_ffn[0], conv_w=dconv[0:3], conv_b=dconv[3],
        )

    got = _run_rider(
        _join(_chip_rider(mix_pairs, mix_axes), _small_rider(_pack([per_layer[0][k] for k in LAYER_SMALL]))), "grad_exchange_0"
    )
    chip_sums(mix_ids, mix_pairs, got[: len(mix_ids)], 0)
    done = _run_rider(_complete_rider([grads[k] for k in mix_ids], mix_axes, [0] * len(mix_ids)), "grad_complete_0")
    for k, g in zip(mix_ids, done):
        grads[k] = g
    grad = dict(zip(BIG, grads))

    layer_shapes = [per_layer[0][k].shape for k in LAYER_SMALL]
    summed_upper = _unpack(_sum_devices(gathered_upper[0], "sum_small_upper"), layer_shapes * (DEPTH - 1) + [d_final[0].shape])
    summed_0 = _unpack(_sum_devices(got[len(mix_ids)], "sum_small_0"), layer_shapes)
    by_layer = [summed_0] + [summed_upper[j * len(LAYER_SMALL) : (j + 1) * len(LAYER_SMALL)] for j in range(DEPTH - 1)]
    small = {k: jnp.stack([by_layer[j][i] for j in range(DEPTH)]) for i, k in enumerate(LAYER_SMALL)}
    for k in LAYER_SMALL:
        if k != "lb":
            grad[k] = small[k]
    grad["hg_lb_logits"] = _lower_bounds_bwd(hg_lb_logits, small["lb"])
    grad["final_norm"] = summed_upper[-1]
    grad["conv_w"] = lax.dynamic_slice_in_dim(grad["conv_w"], _my_chip() * conv_w.shape[2], conv_w.shape[2], axis=2)

    delta, new_m, new_v = {}, {}, {}
    for k in WEIGHTS:
        delta[k], new_m[k], new_v[k] = _adamw(w[k], grad[k], m[k], v[k], f"adamw_{k}")

    return (loss, dx[None], *[grad[k] for k in WEIGHTS], *[delta[k] for k in WEIGHTS], *[new_m[k] for k in WEIGHTS],
            *[new_v[k] for k in WEIGHTS])
```

```python
import math

import jax
import jax.numpy as jnp
from jax import lax
from jax.experimental import pallas as pl
from jax.experimental.pallas import tpu as pltpu

F32 = jnp.float32
BF16 = jnp.bfloat16
MM_DTYPE = BF16

N_HEADS = 4
HEAD_DIM = 128
BR_DIM = N_HEADS * HEAD_DIM
CHUNK = 64
GM_BLOCK = 128
DEPTH = 4
N_CHIPS = 4
EPS = 1e-6
TINY = 1e-30
LB_MAX = 0.999
ADAM_LR = 0.001
ADAM_B1 = 0.9
ADAM_B2 = 0.999
ADAM_EPS = 1e-08
ADAM_WD = 0.01
ADAM_STEP = 10
INV_SQRT2 = 1.0 / math.sqrt(2.0)
INV_SQRT_2PI = 1.0 / math.sqrt(2.0 * math.pi)

VMEM_LIMIT_BYTES = 56 * 1024 * 1024
TILE_BYTES = 2 * 1024 * 1024
ACC_BYTES = 6 * 1024 * 1024
MESH = pl.DeviceIdType.MESH
SDS = jax.ShapeDtypeStruct
ANY = pl.BlockSpec(memory_space=pl.ANY)


def _params(*sem):
    return pltpu.CompilerParams(dimension_semantics=sem or None, vmem_limit_bytes=VMEM_LIMIT_BYTES)


def _divisor(n, limit, mult):
    best = None
    for d in range(mult, min(n, limit) + 1, mult):
        if n % d == 0:
            best = d
    return best if best is not None else n


def _dot(a, b):
    return jnp.dot(a.astype(MM_DTYPE), b.astype(MM_DTYPE), preferred_element_type=F32)


def _dot_nt(a, b):
    return lax.dot_general(a.astype(MM_DTYPE), b.astype(MM_DTYPE), (((1,), (1,)), ((), ())), preferred_element_type=F32)


def _dot_tn(a, b):
    return lax.dot_general(a.astype(MM_DTYPE), b.astype(MM_DTYPE), (((0,), (0,)), ((), ())), preferred_element_type=F32)


def _sigmoid(x):
    return 1.0 / (1.0 + jnp.exp(-x))


def _gelu(x):
    return 0.5 * x * (1.0 + lax.erf(x * INV_SQRT2))


def _gelu_grad(x):
    return 0.5 * (1.0 + lax.erf(x * INV_SQRT2)) + x * jnp.exp(-0.5 * x * x) * INV_SQRT_2PI


def _silu_grad(x, s):
    return s * (1.0 + x * (1.0 - s))


def _resident(shape, index_map):
    return pl.BlockSpec(shape, index_map, pipeline_mode=pl.Buffered(1))


class _Rider:
    def __init__(self, inputs, out_shapes, scratch, start, finish, aliases=None):
        self.inputs = list(inputs)
        self.out_shapes = list(out_shapes)
        self.scratch = list(scratch)
        self.start = start
        self.finish = finish
        self.aliases = dict(aliases or {})


def _join(a, b):
    if a is None or b is None:
        return a if b is None else b
    na, oa, sa = len(a.inputs), len(a.out_shapes), len(a.scratch)

    def start(i, o, s):
        a.start(i[:na], o[:oa], s[:sa])
        b.start(i[na:], o[oa:], s[sa:])

    def finish(i, o, s):
        a.finish(i[:na], o[:oa], s[:sa])
        b.finish(i[na:], o[oa:], s[sa:])

    aliases = dict(a.aliases)
    aliases.update({na + k: oa + v for k, v in b.aliases.items()})
    return _Rider(a.inputs + b.inputs, a.out_shapes + b.out_shapes, a.scratch + b.scratch, start, finish, aliases)


def _hosted_call(body, rider, *, name, grid, in_specs, out_specs, out_shape, scratch_shapes, args, semantics):
    n_in, n_out, n_scr = len(in_specs), len(out_specs), len(scratch_shapes)
    if rider is None:
        outs = pl.pallas_call(
            body, name=name, grid=grid, in_specs=in_specs, out_specs=out_specs, out_shape=out_shape,
            scratch_shapes=scratch_shapes, compiler_params=_params(*semantics),
        )(*args)
        return list(outs), []
    ri, ro = len(rider.inputs), len(rider.out_shapes)

    def wrapped(*refs):
        p = 0
        hin = refs[p : p + n_in]
        p += n_in
        rin = refs[p : p + ri]
        p += ri
        hout = refs[p : p + n_out]
        p += n_out
        rout = refs[p : p + ro]
        p += ro
        hscr = refs[p : p + n_scr]
        rscr = refs[p + n_scr :]

        @pl.when(pl.program_id(0) == 0)
        def _():
            rider.start(rin, rout, rscr)

        body(*hin, *hout, *hscr)

        @pl.when(pl.program_id(0) == grid[0] - 1)
        def _():
            rider.finish(rin, rout, rscr)

    outs = pl.pallas_call(
        wrapped,
        name=name,
        grid=grid,
        in_specs=list(in_specs) + [ANY] * ri,
        out_specs=list(out_specs) + [ANY] * ro,
        out_shape=list(out_shape) + rider.out_shapes,
        scratch_shapes=list(scratch_shapes) + rider.scratch,
        input_output_aliases={n_in + k: n_out + v for k, v in rider.aliases.items()},
        compiler_params=_params(*semantics),
    )(*args, *rider.inputs)
    return list(outs[:n_out]), list(outs[n_out:])


def _run_rider(rider, name):
    ri, ro = len(rider.inputs), len(rider.out_shapes)

    def body(*refs):
        rin, rout, rscr = refs[:ri], refs[ri : ri + ro], refs[ri + ro :]
        rider.start(rin, rout, rscr)
        rider.finish(rin, rout, rscr)

    return list(
        pl.pallas_call(
            body,
            name=name,
            in_specs=[ANY] * ri,
            out_specs=[ANY] * ro,
            out_shape=rider.out_shapes,
            scratch_shapes=rider.scratch,
            input_output_aliases=rider.aliases,
        )(*rider.inputs)
    )


def _lower_bounds(logits):
    def body(lg_ref, lb_ref):
        lg = lg_ref[...]
        mx = jnp.max(lg, axis=0, keepdims=True)
        e = jnp.exp(lg - mx)
        p = e / jnp.sum(e, axis=0, keepdims=True)
        run = p[0:1]
        rows = [run - p[0:1]]
        for l in range(1, DEPTH):
            run = run + p[l : l + 1]
            rows.append(run - p[0:1])
        raw = jnp.concatenate(rows, axis=0)
        lb_ref[...] = jnp.minimum(jnp.maximum(raw, 0.0), LB_MAX)

    return pl.pallas_call(body, name="lower_bounds", out_shape=SDS(logits.shape, F32))(logits)


def _lower_bounds_bwd(logits, dlb):
    def body(lg_ref, dlb_ref, dlg_ref):
        lg = lg_ref[...]
        mx = jnp.max(lg, axis=0, keepdims=True)
        e = jnp.exp(lg - mx)
        p = e / jnp.sum(e, axis=0, keepdims=True)
        run = p[0:1]
        rows = [run - p[0:1]]
        for l in range(1, DEPTH):
            run = run + p[l : l + 1]
            rows.append(run - p[0:1])
        raw = jnp.concatenate(rows, axis=0)
        lo = jnp.where(raw > 0.0, 1.0, jnp.where(raw == 0.0, 0.5, 0.0))
        clipped = jnp.maximum(raw, 0.0)
        hi = jnp.where(clipped < LB_MAX, 1.0, jnp.where(clipped == LB_MAX, 0.5, 0.0))
        draw = dlb_ref[...] * lo * hi
        total = jnp.sum(draw, axis=0, keepdims=True)
        dps = []
        tail = total
        for j in range(DEPTH):
            dps.append(tail - total if j == 0 else tail)
            tail = tail - draw[j : j + 1]
        dp = jnp.concatenate(dps, axis=0)
        dlg_ref[...] = p * (dp - jnp.sum(p * dp, axis=0, keepdims=True))

    return pl.pallas_call(body, name="lower_bounds_bwd", out_shape=SDS(logits.shape, F32))(logits, dlb)


def _in_proj(x, gain, w, lng, lnb, ws, bst, name, rider=None):
    s, d = x.shape
    n = w.shape[1]
    tm = _divisor(s, 512, GM_BLOCK)

    def body(x_ref, g_ref, w_ref, lng_ref, lnb_ref, ws_ref, bst_ref, z_ref, h_ref, a_ref):
        xv = x_ref[...]
        r = lax.rsqrt(jnp.mean(xv * xv, axis=-1, keepdims=True) + EPS)
        h = ((xv * r) * g_ref[...]).astype(MM_DTYPE)
        h_ref[...] = h
        uv = jnp.dot(h, w_ref[:, : 2 * BR_DIM], preferred_element_type=F32)
        z_ref[:, : 2 * BR_DIM] = uv
        z_ref[:, 2 * BR_DIM :] = jnp.dot(h, w_ref[:, 2 * BR_DIM :], preferred_element_type=F32)
        _gmlp_fwd_tile(uv[:, :BR_DIM], uv[:, BR_DIM:], lng_ref, lnb_ref, ws_ref, bst_ref, a_ref)

    row = lambda wd: pl.BlockSpec((tm, wd), lambda i: (i, 0))
    return _hosted_call(
        body,
        rider,
        name=name,
        grid=(s // tm,),
        in_specs=[row(d), _resident((1, d), lambda i: (0, 0)), _resident((d, n), lambda i: (0, 0))] + _gmlp_param_specs(),
        out_specs=[row(n), row(d), row(BR_DIM)],
        out_shape=[SDS((s, n), F32), SDS((s, d), MM_DTYPE), SDS((s, BR_DIM), MM_DTYPE)],
        scratch_shapes=[],
        args=[x, gain, w, lng, lnb, ws, bst],
        semantics=("arbitrary",),
    )


def _in_proj_bwd(dz_parts, w, x, gain, dres, name, rider=None):
    s, d = x.shape
    n = w.shape[1]
    tm = _divisor(s, 512, 16)
    widths = [p.shape[1] for p in dz_parts]
    offsets = [sum(widths[:k]) for k in range(len(widths))]
    n_parts = len(dz_parts)

    def body(*refs):
        dz_refs = refs[:n_parts]
        w_ref, x_ref, g_ref, dres_ref, dx_ref, dg_ref = refs[n_parts:]

        @pl.when(pl.program_id(0) == 0)
        def _():
            dg_ref[...] = jnp.zeros_like(dg_ref)

        dh = None
        for dz_ref, off, wd in zip(dz_refs, offsets, widths):
            part = _dot_nt(dz_ref[...], w_ref[:, off : off + wd])
            dh = part if dh is None else dh + part
        xv = x_ref[...]
        r = lax.rsqrt(jnp.mean(xv * xv, axis=-1, keepdims=True) + EPS)
        xh = xv * r
        dg_ref[...] += jnp.sum(dh * xh, axis=0, keepdims=True)
        dxh = dh * g_ref[...]
        dx_ref[...] = dres_ref[...] + r * (dxh - xh * jnp.mean(dxh * xh, axis=-1, keepdims=True))

    in_specs = [pl.BlockSpec((tm, wd), lambda i: (i, 0)) for wd in widths] + [
        _resident((d, n), lambda i: (0, 0)),
        pl.BlockSpec((tm, d), lambda i: (i, 0)),
        _resident((1, d), lambda i: (0, 0)),
        pl.BlockSpec((tm, d), lambda i: (i, 0)),
    ]
    return _hosted_call(
        body,
        rider,
        name=name,
        grid=(s // tm,),
        in_specs=in_specs,
        out_specs=[pl.BlockSpec((tm, d), lambda i: (i, 0)), pl.BlockSpec((1, d), lambda i: (0, 0))],
        out_shape=[SDS((s, d), F32), SDS((1, d), F32)],
        scratch_shapes=[],
        args=[*dz_parts, w, x, gain, dres],
        semantics=("arbitrary",),
    )


def _matmul_tn(a, b, stack, layer, name, n_total=None, col_off=0):
    s, k = a.shape
    n = b.shape[1]
    n_total = n if n_total is None else n_total
    tn = _divisor(math.gcd(n, col_off) if col_off else n, max(128, ACC_BYTES // (4 * k)), 128)
    ts = _divisor(s, min(2048, max(512, ACC_BYTES // (2 * k))), 16)
    off = col_off // tn

    def body(a_ref, b_ref, *rest):
        out_ref = rest[-1]

        @pl.when(pl.program_id(1) == 0)
        def _():
            out_ref[...] = jnp.zeros_like(out_ref)

        out_ref[...] += _dot_tn(a_ref[...], b_ref[...])

    in_specs = [pl.BlockSpec((ts, k), lambda j, i: (i, 0)), pl.BlockSpec((ts, tn), lambda j, i: (i, j))]
    args = [a, b]
    aliases = {}
    if stack is not None:
        in_specs.append(ANY)
        args.append(stack)
        aliases = {2: 0}
    return pl.pallas_call(
        body,
        name=name,
        grid=(n // tn, s // ts),
        in_specs=in_specs,
        out_specs=pl.BlockSpec((None, k, tn), lambda j, i: (layer, 0, off + j)),
        out_shape=SDS((DEPTH, k, n_total), F32),
        input_output_aliases=aliases,
        compiler_params=_params("parallel", "arbitrary"),
    )(*args)


def _gm_mask():
    r = lax.broadcasted_iota(jnp.int32, (GM_BLOCK, GM_BLOCK), 0) // CHUNK
    c = lax.broadcasted_iota(jnp.int32, (GM_BLOCK, GM_BLOCK), 1) // CHUNK
    return r >= c


def _gm_normed(v, lng, lnb):
    vg = _gelu(v)
    mu = jnp.mean(vg, axis=-1, keepdims=True)
    xc = vg - mu
    rstd = lax.rsqrt(jnp.mean(xc * xc, axis=-1, keepdims=True) + EPS)
    xh = xc * rstd
    return xh, rstd, xh * lng + lnb


def _gmlp_param_specs():
    return [
        _resident((1, BR_DIM), lambda i: (0, 0)),
        _resident((1, BR_DIM), lambda i: (0, 0)),
        _resident((N_HEADS, GM_BLOCK, GM_BLOCK), lambda i: (0, 0, 0)),
        _resident((GM_BLOCK, N_HEADS), lambda i: (0, 0)),
    ]


def _gmlp_fwd_tile(u, v, lng_ref, lnb_ref, ws_ref, bst_ref, a_ref):
    mask = _gm_mask()
    ug = _gelu(u)
    _, _, vn = _gm_normed(v, lng_ref[...], lnb_ref[...])
    vn = vn.astype(MM_DTYPE)
    for h in range(N_HEADS):
        cs = slice(h * HEAD_DIM, (h + 1) * HEAD_DIM)
        wm = jnp.where(mask, ws_ref[h], 0.0).astype(MM_DTYPE)
        for blk in range(u.shape[0] // GM_BLOCK):
            rs = slice(blk * GM_BLOCK, (blk + 1) * GM_BLOCK)
            mixed = _dot(wm, vn[rs, cs]) + bst_ref[:, h : h + 1]
            a_ref[rs, cs] = (ug[rs, cs] * mixed).astype(MM_DTYPE)


def _gmlp_bwd_tile(da_v, u, v, lng_ref, lnb_ref, ws_ref, bst_ref, dz_ref, dlng_ref, dlnb_ref, dws_ref, dbst_ref):
    mask = _gm_mask()
    ug = _gelu(u)
    lng_v = lng_ref[...]
    xh, rstd, vn = _gm_normed(v, lng_v, lnb_ref[...])
    vnb = vn.astype(MM_DTYPE)
    dmix = da_v * ug
    dmixb = dmix.astype(MM_DTYPE)
    dvn_cols = []
    for h in range(N_HEADS):
        cs = slice(h * HEAD_DIM, (h + 1) * HEAD_DIM)
        wm = jnp.where(mask, ws_ref[h], 0.0).astype(MM_DTYPE)
        dw = jnp.zeros((GM_BLOCK, GM_BLOCK), F32)
        dbias = jnp.zeros((GM_BLOCK, 1), F32)
        dvn_rows = []
        for blk in range(u.shape[0] // GM_BLOCK):
            rs = slice(blk * GM_BLOCK, (blk + 1) * GM_BLOCK)
            mixed = _dot(wm, vnb[rs, cs]) + bst_ref[:, h : h + 1]
            dz_ref[rs, cs] = (da_v[rs, cs] * mixed * _gelu_grad(u[rs, cs])).astype(MM_DTYPE)
            dw = dw + _dot_nt(dmixb[rs, cs], vnb[rs, cs])
            dbias = dbias + jnp.sum(dmix[rs, cs], axis=1, keepdims=True)
            dvn_rows.append(_dot_tn(wm, dmixb[rs, cs]))
        dws_ref[h] += jnp.where(mask, dw, 0.0)
        dbst_ref[:, h : h + 1] += dbias
        dvn_cols.append(jnp.concatenate(dvn_rows, axis=0) if len(dvn_rows) > 1 else dvn_rows[0])
    dvn = jnp.concatenate(dvn_cols, axis=1)
    dlng_ref[...] += jnp.sum(dvn * xh, axis=0, keepdims=True)
    dlnb_ref[...] += jnp.sum(dvn, axis=0, keepdims=True)
    dxh = dvn * lng_v
    dvg = rstd * (dxh - jnp.mean(dxh, axis=-1, keepdims=True) - xh * jnp.mean(dxh * xh, axis=-1, keepdims=True))
    dz_ref[:, BR_DIM:] = (dvg * _gelu_grad(v)).astype(MM_DTYPE)


HG_ROWS = 512
HG_UNROLL = 8
MID = CHUNK // 2 - 1


def _tri(lower):
    r = lax.broadcasted_iota(jnp.int32, (CHUNK, CHUNK), 0)
    c = lax.broadcasted_iota(jnp.int32, (CHUNK, CHUNK), 1)
    return r >= c if lower else c >= r


def _scan_rows(x, reverse=False):
    n = x.shape[0]
    rid = lax.broadcasted_iota(jnp.int32, x.shape, 0)
    k = 1
    while k < n:
        if reverse:
            x = x + jnp.where(rid < n - k, pltpu.roll(x, n - k, 0), 0.0)
        else:
            x = x + jnp.where(rid >= k, pltpu.roll(x, k, 0), 0.0)
        k *= 2
    return x


def _hgrn_fwd_chunk(zqf_ref, zio_ref, ci, lb_ref, ng_v, state, o_ref, b_ref, st_ref):
    low = _tri(True)
    rows = slice(ci * CHUNK, (ci + 1) * CHUNK)
    for h in range(N_HEADS):
        cs = slice(h * HEAD_DIM, (h + 1) * HEAD_DIM)
        cs2 = slice(BR_DIM + h * HEAD_DIM, BR_DIM + (h + 1) * HEAD_DIM)
        zq = zqf_ref[rows, cs]
        zf = zqf_ref[rows, cs2]
        vi = zio_ref[rows, cs]
        zog = zio_ref[rows, cs2]
        lbh = lb_ref[:, cs]
        qf = zq * _sigmoid(zq)
        forget = lbh + (1.0 - lbh) * _sigmoid(zf)
        g = jnp.log(jnp.maximum(forget, TINY))
        k = (1.0 - lbh) * _sigmoid(-zf)
        gc = _scan_rows(g)
        gm = gc[MID : MID + 1]
        gl = gc[CHUNK - 1 : CHUNK]
        a = jnp.where(low, _dot_nt(qf * jnp.exp(gc - gm), k * jnp.exp(gm - gc)), 0.0)
        st = state[h]
        st_ref[ci, h] = st
        o = _dot(a, vi) + _dot_nt(qf * jnp.exp(gc), st)
        state[h] = st * jnp.exp(gl) + _dot_tn(vi, k * jnp.exp(gl - gc))
        r = lax.rsqrt(jnp.mean(o * o, axis=-1, keepdims=True) + EPS)
        o_ref[rows, cs] = o
        b_ref[rows, cs] = (((o * r) * ng_v) * (zog * _sigmoid(zog))).astype(MM_DTYPE)


def _hgrn_fwd(z, lb, ng, name, rider=None):
    s = z.shape[0]
    rows_per = _divisor(s, HG_ROWS, CHUNK)
    n_sub = rows_per // CHUNK

    def body(zqf_ref, zio_ref, lb_ref, ng_ref, o_ref, b_ref, st_ref, state):
        @pl.when(pl.program_id(0) == 0)
        def _():
            state[...] = jnp.zeros_like(state)

        ng_v = ng_ref[...]
        for ci in range(n_sub):
            _hgrn_fwd_chunk(zqf_ref, zio_ref, ci, lb_ref, ng_v, state, o_ref, b_ref, st_ref)

    return _hosted_call(
        body,
        rider,
        name=name,
        grid=(s // rows_per,),
        in_specs=[
            pl.BlockSpec((rows_per, 2 * BR_DIM), lambda i: (i, 1)),
            pl.BlockSpec((rows_per, 2 * BR_DIM), lambda i: (i, 2)),
            _resident((1, BR_DIM), lambda i: (0, 0)),
            _resident((1, HEAD_DIM), lambda i: (0, 0)),
        ],
        out_specs=[
            pl.BlockSpec((rows_per, BR_DIM), lambda i: (i, 0)),
            pl.BlockSpec((rows_per, BR_DIM), lambda i: (i, 0)),
            pl.BlockSpec((n_sub, N_HEADS, HEAD_DIM, HEAD_DIM), lambda i: (i, 0, 0, 0)),
        ],
        out_shape=[
            SDS((s, BR_DIM), F32),
            SDS((s, BR_DIM), MM_DTYPE),
            SDS((s // CHUNK, N_HEADS, HEAD_DIM, HEAD_DIM), F32),
        ],
        scratch_shapes=[pltpu.VMEM((N_HEADS, HEAD_DIM, HEAD_DIM), F32)],
        args=[z, z, lb, ng],
        semantics=("arbitrary",),
    )


def _hgrn_bwd(db, z, o, states, lb, ng, name, rider=None):
    s = z.shape[0]
    rows_per = _divisor(s, HG_ROWS, CHUNK)
    n_sub = rows_per // CHUNK
    n_steps = s // rows_per

    def body(db_ref, zqf_ref, zio_ref, o_ref, st_ref, lb_ref, ng_ref, dz_ref, dlb_ref, dng_ref, dstate):
        @pl.when(pl.program_id(0) == 0)
        def _():
            dstate[...] = jnp.zeros_like(dstate)
            dlb_ref[...] = jnp.zeros_like(dlb_ref)
            dng_ref[...] = jnp.zeros_like(dng_ref)

        low = _tri(True)
        ng_v = ng_ref[...]

        def chunk(cc, carry):
            ci = n_sub - 1 - cc
            rows = pl.ds(pl.multiple_of(ci * CHUNK, CHUNK), CHUNK)
            for h in range(N_HEADS):
                cs = slice(h * HEAD_DIM, (h + 1) * HEAD_DIM)
                cs2 = slice(BR_DIM + h * HEAD_DIM, BR_DIM + (h + 1) * HEAD_DIM)
                zq = zqf_ref[rows, cs]
                zf = zqf_ref[rows, cs2]
                vi = zio_ref[rows, cs]
                zog = zio_ref[rows, cs2]
                lbh = lb_ref[:, cs]
                ov = o_ref[rows, cs]
                dbv = db_ref[rows, cs]
                r = lax.rsqrt(jnp.mean(ov * ov, axis=-1, keepdims=True) + EPS)
                on = ov * r
                sgo = _sigmoid(zog)
                gate = zog * sgo
                dng_ref[...] += jnp.sum(dbv * gate * on, axis=0, keepdims=True)
                don = dbv * gate * ng_v
                dzog = dbv * (on * ng_v) * _silu_grad(zog, sgo)
                do = r * (don - on * jnp.mean(don * on, axis=-1, keepdims=True))
                sq = _sigmoid(zq)
                qf = zq * sq
                sg = _sigmoid(zf)
                sgn = _sigmoid(-zf)
                oml = 1.0 - lbh
                forget = lbh + oml * sg
                fm = jnp.maximum(forget, TINY)
                k = oml * sgn
                gc = _scan_rows(jnp.log(fm))
                gm = gc[MID : MID + 1]
                gl = gc[CHUNK - 1 : CHUNK]
                e_g = jnp.exp(gc)
                e_q = jnp.exp(gc - gm)
                e_k = jnp.exp(gm - gc)
                e_kd = jnp.exp(gl - gc)
                e_gl = jnp.exp(gl)
                qt = qf * e_q
                kt = k * e_k
                a = jnp.where(low, _dot_nt(qt, kt), 0.0)
                st = st_ref[ci, h]
                dst = dstate[h]
                dp = jnp.where(low, _dot_nt(do, vi), 0.0)
                dv = _dot_tn(a, do) + _dot_nt(k * e_kd, dst)
                dq = _dot(dp, kt) * e_q + e_g * _dot(do, st)
                dks = e_kd * _dot(vi, dst)
                dk = _dot_tn(dp, qt) * e_k + dks
                dstate[h] = _dot_tn(do, qf * e_g) + dst * e_gl
                dgc = qf * dq - k * dk
                extra = e_gl * jnp.sum(st * dst, axis=0, keepdims=True) + jnp.sum(k * dks, axis=0, keepdims=True)
                dg = _scan_rows(dgc, reverse=True) + extra
                dforget = jnp.where(forget > TINY, dg / fm, 0.0)
                both = dforget - dk
                dlb_ref[:, cs] += jnp.sum(sgn * both, axis=0, keepdims=True)
                dz_ref[rows, cs] = (dq * _silu_grad(zq, sq)).astype(MM_DTYPE)
                dz_ref[rows, cs2] = (oml * sg * sgn * both).astype(MM_DTYPE)
                dz_ref[rows, 2 * BR_DIM + h * HEAD_DIM : 2 * BR_DIM + (h + 1) * HEAD_DIM] = dv.astype(MM_DTYPE)
                dz_ref[rows, 3 * BR_DIM + h * HEAD_DIM : 3 * BR_DIM + (h + 1) * HEAD_DIM] = dzog.astype(MM_DTYPE)
            return carry

        lax.fori_loop(0, n_sub, chunk, 0, unroll=math.gcd(n_sub, HG_UNROLL))

    rev = lambda i: n_steps - 1 - i
    return _hosted_call(
        body,
        rider,
        name=name,
        grid=(n_steps,),
        in_specs=[
            pl.BlockSpec((rows_per, BR_DIM), lambda i: (rev(i), 0)),
            pl.BlockSpec((rows_per, 2 * BR_DIM), lambda i: (rev(i), 1)),
            pl.BlockSpec((rows_per, 2 * BR_DIM), lambda i: (rev(i), 2)),
            pl.BlockSpec((rows_per, BR_DIM), lambda i: (rev(i), 0)),
            pl.BlockSpec((n_sub, N_HEADS, HEAD_DIM, HEAD_DIM), lambda i: (rev(i), 0, 0, 0)),
            _resident((1, BR_DIM), lambda i: (0, 0)),
            _resident((1, HEAD_DIM), lambda i: (0, 0)),
        ],
        out_specs=[
            pl.BlockSpec((rows_per, 4 * BR_DIM), lambda i: (rev(i), 0)),
            pl.BlockSpec((1, BR_DIM), lambda i: (0, 0)),
            pl.BlockSpec((1, HEAD_DIM), lambda i: (0, 0)),
        ],
        out_shape=[SDS((s, 4 * BR_DIM), MM_DTYPE), SDS((1, BR_DIM), F32), SDS((1, HEAD_DIM), F32)],
        scratch_shapes=[pltpu.VMEM((N_HEADS, HEAD_DIM, HEAD_DIM), F32)],
        args=[db, z, z, o, states, lb, ng],
        semantics=("arbitrary",),
    )


def _mix_fwd(x, a, b, z, w_gm, w_hg, w_out, name, rider=None):
    s, d = x.shape
    tm = _divisor(s, 512, 16)
    g0 = 6 * BR_DIM // d

    def body(x_ref, a_ref, b_ref, ga_ref, gb_ref, wgm_ref, whg_ref, wout_ref, out_ref):
        y = _sigmoid(ga_ref[...]) * _dot(a_ref[...], wgm_ref[...]) + _sigmoid(gb_ref[...]) * _dot(b_ref[...], whg_ref[...])
        out_ref[...] = x_ref[...] + _dot(y, wout_ref[...])

    return _hosted_call(
        body,
        rider,
        name=name,
        grid=(s // tm,),
        in_specs=[
            pl.BlockSpec((tm, d), lambda i: (i, 0)),
            pl.BlockSpec((tm, BR_DIM), lambda i: (i, 0)),
            pl.BlockSpec((tm, BR_DIM), lambda i: (i, 0)),
            pl.BlockSpec((tm, d), lambda i: (i, g0)),
            pl.BlockSpec((tm, d), lambda i: (i, g0 + 1)),
            _resident((BR_DIM, d), lambda i: (0, 0)),
            _resident((BR_DIM, d), lambda i: (0, 0)),
            _resident((d, d), lambda i: (0, 0)),
        ],
        out_specs=[pl.BlockSpec((tm, d), lambda i: (i, 0))],
        out_shape=[SDS((s, d), F32)],
        scratch_shapes=[],
        args=[x, a, b, z, z, w_gm, w_hg, w_out],
        semantics=("arbitrary",),
    )


def _mix_bwd(dx, a, b, z, w_gm, w_hg, w_out, lng, lnb, ws, bst, name, rider=None):
    s, d = dx.shape
    tm = _divisor(s, 512, GM_BLOCK)
    g0 = 6 * BR_DIM // d

    def body(dx_ref, a_ref, b_ref, uv_ref, ga_ref, gb_ref, wgm_ref, whg_ref, wout_ref, lng_ref, lnb_ref, ws_ref, bst_ref,
             y_ref, dpa_ref, dpb_ref, db_ref, dg_ref, duv_ref, dlng_ref, dlnb_ref, dws_ref, dbst_ref):
        @pl.when(pl.program_id(0) == 0)
        def _():
            for ref in (dlng_ref, dlnb_ref, dws_ref, dbst_ref):
                ref[...] = jnp.zeros_like(ref)

        dy = _dot_nt(dx_ref[...], wout_ref[...])
        pa = _dot(a_ref[...], wgm_ref[...])
        pb = _dot(b_ref[...], whg_ref[...])
        sa = _sigmoid(ga_ref[...])
        sb = _sigmoid(gb_ref[...])
        y_ref[...] = (sa * pa + sb * pb).astype(MM_DTYPE)
        dpa = (dy * sa).astype(MM_DTYPE)
        dpb = (dy * sb).astype(MM_DTYPE)
        dpa_ref[...] = dpa
        dpb_ref[...] = dpb
        da = _dot_nt(dpa, wgm_ref[...])
        db_ref[...] = _dot_nt(dpb, whg_ref[...])
        dg_ref[:, :d] = (dy * pa * sa * (1.0 - sa)).astype(MM_DTYPE)
        dg_ref[:, d:] = (dy * pb * sb * (1.0 - sb)).astype(MM_DTYPE)
        _gmlp_bwd_tile(da, uv_ref[:, :BR_DIM], uv_ref[:, BR_DIM:], lng_ref, lnb_ref, ws_ref, bst_ref,
                       duv_ref, dlng_ref, dlnb_ref, dws_ref, dbst_ref)

    row = lambda w: pl.BlockSpec((tm, w), lambda i: (i, 0))
    fixed = lambda shape: pl.BlockSpec(shape, lambda i: tuple(0 for _ in shape))
    return _hosted_call(
        body,
        rider,
        name=name,
        grid=(s // tm,),
        in_specs=[
            row(d),
            row(BR_DIM),
            row(BR_DIM),
            row(2 * BR_DIM),
            pl.BlockSpec((tm, d), lambda i: (i, g0)),
            pl.BlockSpec((tm, d), lambda i: (i, g0 + 1)),
            _resident((BR_DIM, d), lambda i: (0, 0)),
            _resident((BR_DIM, d), lambda i: (0, 0)),
            _resident((d, d), lambda i: (0, 0)),
        ]
        + _gmlp_param_specs(),
        out_specs=[row(d), row(d), row(d), row(BR_DIM), row(2 * d), row(2 * BR_DIM), fixed((1, BR_DIM)), fixed((1, BR_DIM)),
                   fixed((N_HEADS, GM_BLOCK, GM_BLOCK)), fixed((GM_BLOCK, N_HEADS))],
        out_shape=[
            SDS((s, d), MM_DTYPE),
            SDS((s, d), MM_DTYPE),
            SDS((s, d), MM_DTYPE),
            SDS((s, BR_DIM), F32),
            SDS((s, 2 * d), MM_DTYPE),
            SDS((s, 2 * BR_DIM), MM_DTYPE),
            SDS((1, BR_DIM), F32),
            SDS((1, BR_DIM), F32),
            SDS((N_HEADS, GM_BLOCK, GM_BLOCK), F32),
            SDS((GM_BLOCK, N_HEADS), F32),
        ],
        scratch_shapes=[],
        args=[dx, a, b, z, z, z, w_gm, w_hg, w_out, lng, lnb, ws, bst],
        semantics=("arbitrary",),
    )


HALO = 8
FFN_ROWS = 256


def _shift_down(v, prev, n):
    rolled = pltpu.roll(v, n, 0)
    rid = lax.broadcasted_iota(jnp.int32, v.shape, 0)
    out = rolled
    for r in range(n):
        out = jnp.where(rid == r, prev[HALO - n + r : HALO - n + r + 1], out)
    return out


def _shift_up(v, nxt, n):
    tm = v.shape[0]
    rolled = pltpu.roll(v, tm - n, 0)
    rid = lax.broadcasted_iota(jnp.int32, v.shape, 0)
    out = rolled
    for r in range(n):
        out = jnp.where(rid == tm - n + r, nxt[r : r + 1], out)
    return out


def _conv_cols(f):
    return _divisor(f, 256, 128)


def _ffn_fwd(x, gain, w_up, cw, cb, w_down, layer, name, rider=None):
    s, d = x.shape
    f = w_down.shape[0]
    tm = _divisor(s, FFN_ROWS, 16)
    cwid = _conv_cols(f)

    def body(x_ref, g_ref, wu_ref, cw_ref, cb_ref, wd_ref, out_ref, z_ref, h_ref, conv_ref, halo):
        @pl.when(pl.program_id(0) == 0)
        def _():
            halo[...] = jnp.zeros_like(halo)

        xv = x_ref[...]
        r = lax.rsqrt(jnp.mean(xv * xv, axis=-1, keepdims=True) + EPS)
        h = ((xv * r) * g_ref[...]).astype(MM_DTYPE)
        h_ref[...] = h
        acc = xv
        starts = list(range(0, f, cwid))

        def project(c0):
            return [jnp.dot(h, wu_ref[:, base : base + cwid], preferred_element_type=F32) for base in (c0, f + c0)]

        ahead = project(starts[0])
        for n, c0 in enumerate(starts):
            halves = []
            current = ahead
            if n + 1 < len(starts):
                ahead = project(starts[n + 1])
            for base, zc in zip((c0, f + c0), current):
                cols = slice(base, base + cwid)
                z_ref[:, cols] = zc.astype(MM_DTYPE)
                prev = halo[:, cols]
                conv = cb_ref[:, cols] + cw_ref[0:1, cols] * _shift_down(zc, prev, 2)
                conv = conv + cw_ref[1:2, cols] * _shift_down(zc, prev, 1) + cw_ref[2:3, cols] * zc
                halo[:, cols] = zc[tm - HALO : tm]
                conv_ref[:, cols] = conv.astype(MM_DTYPE)
                halves.append(conv)
            gate, val = halves
            act = gate * _sigmoid(gate) * val
            acc = acc + _dot(act, wd_ref[c0 : c0 + cwid, :])
        out_ref[...] = acc

    row = lambda w: pl.BlockSpec((tm, w), lambda i: (i, 0))
    return _hosted_call(
        body,
        rider,
        name=name,
        grid=(s // tm,),
        in_specs=[
            row(d),
            _resident((1, d), lambda i: (0, 0)),
            _resident((d, 2 * f), lambda i: (0, 0)),
            _resident((None, 3, 2 * f), lambda i: (layer, 0, 0)),
            _resident((1, 2 * f), lambda i: (0, 0)),
            _resident((f, d), lambda i: (0, 0)),
        ],
        out_specs=[row(d), row(2 * f), row(d), row(2 * f)],
        out_shape=[SDS((s, d), F32), SDS((s, 2 * f), MM_DTYPE), SDS((s, d), MM_DTYPE), SDS((s, 2 * f), MM_DTYPE)],
        scratch_shapes=[pltpu.VMEM((HALO, 2 * f), F32)],
        args=[x, gain, w_up, cw, cb, w_down],
        semantics=("arbitrary",),
    )


def _ffn_bwd(dx, z2, conv, cw, w_down, w_up, x, gain, layer, name):
    s, d = dx.shape
    f = z2.shape[1] // 2
    tm = _divisor(s, 256, 16)
    cwid = _conv_cols(f)
    n_steps = s // tm

    def body(dx_ref, z_ref, conv_ref, cw_ref, wd_ref, wu_ref, x_ref, g_ref, dz_ref, act_ref, dcw_ref, dx1_ref, dg_ref, nxt):
        @pl.when(pl.program_id(0) == 0)
        def _():
            nxt[...] = jnp.zeros_like(nxt)
            dcw_ref[...] = jnp.zeros_like(dcw_ref)
            dg_ref[...] = jnp.zeros_like(dg_ref)

        dxf = dx_ref[...]
        dxv = dxf.astype(MM_DTYPE)
        starts = list(range(0, f, cwid))
        ahead = _dot_nt(dxv, wd_ref[0:cwid, :])
        dh = jnp.zeros((tm, d), F32)
        pending = []
        for n, c0 in enumerate(starts):
            dact = ahead
            if n + 1 < len(starts):
                ahead = _dot_nt(dxv, wd_ref[starts[n + 1] : starts[n + 1] + cwid, :])
            for dz_prev, cols_prev in pending:
                dh = dh + _dot_nt(dz_prev, wu_ref[:, cols_prev])
            pending = []
            gate = conv_ref[:, c0 : c0 + cwid].astype(F32)
            val = conv_ref[:, f + c0 : f + c0 + cwid].astype(F32)
            sg = _sigmoid(gate)
            silu = gate * sg
            act_ref[:, c0 : c0 + cwid] = (silu * val).astype(MM_DTYPE)
            dconvs = (dact * val * _silu_grad(gate, sg), dact * silu)
            for base, dconv in zip((c0, f + c0), dconvs):
                cols = slice(base, base + cwid)
                zc = z_ref[:, cols].astype(F32)
                nx = nxt[:, cols]
                up1 = _shift_up(dconv, nx, 1)
                up2 = _shift_up(dconv, nx, 2)
                dcw_ref[0:1, cols] += jnp.sum(up2 * zc, axis=0, keepdims=True)
                dcw_ref[1:2, cols] += jnp.sum(up1 * zc, axis=0, keepdims=True)
                dcw_ref[2:3, cols] += jnp.sum(dconv * zc, axis=0, keepdims=True)
                dcw_ref[3:4, cols] += jnp.sum(dconv, axis=0, keepdims=True)
                dz = (cw_ref[2:3, cols] * dconv + cw_ref[1:2, cols] * up1 + cw_ref[0:1, cols] * up2).astype(MM_DTYPE)
                dz_ref[:, cols] = dz
                nxt[:, cols] = dconv[0:HALO]
                pending.append((dz, cols))
        for dz_prev, cols_prev in pending:
            dh = dh + _dot_nt(dz_prev, wu_ref[:, cols_prev])
        xv = x_ref[...]
        r = lax.rsqrt(jnp.mean(xv * xv, axis=-1, keepdims=True) + EPS)
        xh = xv * r
        dg_ref[...] += jnp.sum(dh * xh, axis=0, keepdims=True)
        dxh = dh * g_ref[...]
        dx1_ref[...] = dxf + r * (dxh - xh * jnp.mean(dxh * xh, axis=-1, keepdims=True))

    rev = lambda i: n_steps - 1 - i
    row = lambda w: pl.BlockSpec((tm, w), lambda i: (rev(i), 0))
    return pl.pallas_call(
        body,
        name=name,
        grid=(n_steps,),
        in_specs=[
            row(d),
            row(2 * f),
            row(2 * f),
            _resident((None, 3, 2 * f), lambda i: (layer, 0, 0)),
            _resident((f, d), lambda i: (0, 0)),
            _resident((d, 2 * f), lambda i: (0, 0)),
            row(d),
            _resident((1, d), lambda i: (0, 0)),
        ],
        out_specs=[
            row(2 * f),
            row(f),
            pl.BlockSpec((HALO, 2 * f), lambda i: (0, 0)),
            row(d),
            pl.BlockSpec((1, d), lambda i: (0, 0)),
        ],
        out_shape=[SDS((s, 2 * f), MM_DTYPE), SDS((s, f), MM_DTYPE), SDS((HALO, 2 * f), F32), SDS((s, d), F32), SDS((1, d), F32)],
        scratch_shapes=[pltpu.VMEM((HALO, 2 * f), F32)],
        compiler_params=_params("arbitrary"),
    )(dx, z2, conv, cw, w_down, w_up, x, gain)


def _loss_head(x, gain, target):
    s, d = x.shape
    tm = _divisor(s, 256, 8)

    def body(x_ref, g_ref, t_ref, loss_ref, dx_ref, dg_ref):
        @pl.when(pl.program_id(0) == 0)
        def _():
            loss_ref[...] = jnp.zeros_like(loss_ref)
            dg_ref[...] = jnp.zeros_like(dg_ref)

        xv = x_ref[...]
        gv = g_ref[...]
        r = lax.rsqrt(jnp.mean(xv * xv, axis=-1, keepdims=True) + EPS)
        xh = xv * r
        err = xh * gv - t_ref[...]
        loss_ref[...] += 0.5 * jnp.sum(jnp.mean(err * err, axis=-1, keepdims=True))
        dout = err * (1.0 / d)
        dg_ref[...] += jnp.sum(dout * xh, axis=0, keepdims=True)
        dxh = dout * gv
        dx_ref[...] = r * (dxh - xh * jnp.mean(dxh * xh, axis=-1, keepdims=True))

    return pl.pallas_call(
        body,
        name="loss_head",
        grid=(s // tm,),
        in_specs=[
            pl.BlockSpec((tm, d), lambda i: (i, 0)),
            _resident((1, d), lambda i: (0, 0)),
            pl.BlockSpec((tm, d), lambda i: (i, 0)),
        ],
        out_specs=[
            pl.BlockSpec((8, 128), lambda i: (0, 0)),
            pl.BlockSpec((tm, d), lambda i: (i, 0)),
            pl.BlockSpec((1, d), lambda i: (0, 0)),
        ],
        out_shape=[SDS((8, 128), F32), SDS((s, d), F32), SDS((1, d), F32)],
        compiler_params=_params("arbitrary"),
    )(x, gain, target)


def _adamw(w, g, m, v, name):
    shape = w.shape
    cols = shape[-1]
    rows = max(1, math.prod(shape[:-1]))
    tr = _divisor(rows, max(8, TILE_BYTES // (4 * cols)), 8)
    flat = [t.reshape(rows, cols) for t in (w, g, m, v)]

    def body(w_ref, g_ref, m_ref, v_ref, d_ref, nm_ref, nv_ref):
        gv = g_ref[...]
        m2 = ADAM_B1 * m_ref[...] + (1.0 - ADAM_B1) * gv
        v2 = ADAM_B2 * v_ref[...] + (1.0 - ADAM_B2) * (gv * gv)
        m_hat = m2 / (1.0 - ADAM_B1**ADAM_STEP)
        v_hat = v2 / (1.0 - ADAM_B2**ADAM_STEP)
        d_ref[...] = -ADAM_LR * (m_hat / (jnp.sqrt(v_hat) + ADAM_EPS) + ADAM_WD * w_ref[...])
        nm_ref[...] = m2
        nv_ref[...] = v2

    spec = pl.BlockSpec((tr, cols), lambda i: (i, 0))
    outs = pl.pallas_call(
        body,
        name=name,
        grid=(rows // tr,),
        in_specs=[spec] * 4,
        out_specs=[spec] * 3,
        out_shape=[SDS((rows, cols), F32)] * 3,
        compiler_params=_params("parallel"),
    )(*flat)
    return tuple(t.reshape(shape) for t in outs)


def _position():
    return lax.axis_index("x"), lax.axis_index("y"), lax.axis_index("c")


def _other_chips(x, y):
    return [(1 - x, y), (x, 1 - y), (1 - x, 1 - y)]


def _remote(src, dst, send_sem, recv_sem, device):
    return pltpu.make_async_remote_copy(
        src_ref=src, dst_ref=dst, send_sem=send_sem, recv_sem=recv_sem, device_id=device, device_id_type=MESH
    )


def _sub(ref, lead, shape, axis, chip=None, half=None):
    cut = [pl.ds(0, shape[0]), pl.ds(0, shape[1])]
    if chip is not None:
        size = shape[axis] // N_CHIPS
        cut[axis] = pl.ds(chip * size, size)
    if half is not None:
        size = shape[1 - axis] // 2
        cut[1 - axis] = pl.ds(half * size, size)
    return ref.at[(*lead, *cut)]


def _scaled(shape, axis, num, den=1):
    return tuple(d * num // den if i == axis else d for i, d in enumerate(shape))


def _gather_rider(shards, axes, layer, conv_w=None):
    n = len(shards)
    shard2d = [t.shape[1:] for t in shards]
    full2d = [_scaled(s2, ax, N_CHIPS) for s2, ax in zip(shard2d, axes)]
    out_shapes = [SDS(f2, t.dtype) for f2, t in zip(full2d, shards)]
    inputs = list(shards)
    own = 6
    scratch = [pltpu.SemaphoreType.DMA((n, 7)), pltpu.SemaphoreType.DMA((n, 7))]
    if conv_w is not None:
        cshape = conv_w.shape
        inputs.append(conv_w)
        out_shapes.append(SDS(_scaled(cshape, 2, N_CHIPS), conv_w.dtype))
        scratch += [pltpu.SemaphoreType.DMA((4,)), pltpu.SemaphoreType.DMA((4,))]

    def conv_slab(ref, chip):
        return ref.at[:, :, pl.ds((2 * chip[0] + chip[1]) * cshape[2], cshape[2])]

    def own_copy(ins, outs, send_sems, recv_sems, k):
        x, y, c = _position()
        slab = _sub(outs[k], (), full2d[k], axes[k], chip=2 * x + y)
        return _remote(ins[k].at[layer], slab, send_sems.at[k, own], recv_sems.at[k, own], (x, y, 1 - c))

    def start(ins, outs, scr):
        send_sems, recv_sems = scr[:2]
        x, y, c = _position()
        me = 2 * x + y
        for k in range(n):
            own_copy(ins, outs, send_sems, recv_sems, k).start()
            for j, chip in enumerate(_other_chips(x, y)):
                _remote(
                    _sub(ins[k], (layer,), shard2d[k], axes[k], half=c),
                    _sub(outs[k], (), full2d[k], axes[k], chip=me, half=c),
                    send_sems.at[k, j], recv_sems.at[k, j], (*chip, c),
                ).start()
        if conv_w is not None:
            csend, crecv = scr[2:]
            _remote(ins[n], conv_slab(outs[n], (x, y)), csend.at[3], crecv.at[3], (x, y, 1 - c)).start()
            for j, chip in enumerate(_other_chips(x, y)):
                _remote(ins[n], conv_slab(outs[n], (x, y)), csend.at[j], crecv.at[j], (*chip, c)).start()

    def finish(ins, outs, scr):
        send_sems, recv_sems = scr[:2]
        x, y, c = _position()
        me = 2 * x + y
        sibling = (x, y, 1 - c)
        chips = _other_chips(x, y)
        forwards = []
        for k in range(n):
            src = _sub(ins[k], (layer,), shard2d[k], axes[k], half=c)
            for j, chip in enumerate(chips):
                landed = _sub(outs[k], (), full2d[k], axes[k], chip=2 * chip[0] + chip[1], half=c)
                _remote(src, landed, send_sems.at[k, j], recv_sems.at[k, j], (*chip, c)).wait_recv()
                fwd = _remote(landed, landed, send_sems.at[k, 3 + j], recv_sems.at[k, 3 + j], sibling)
                fwd.start()
                forwards.append(fwd)
        for k in range(n):
            src = _sub(ins[k], (layer,), shard2d[k], axes[k], half=c)
            for j, chip in enumerate(chips):
                theirs = _sub(outs[k], (), full2d[k], axes[k], chip=2 * chip[0] + chip[1], half=1 - c)
                _remote(src, theirs, send_sems.at[k, 3 + j], recv_sems.at[k, 3 + j], sibling).wait_recv()
        for fwd in forwards:
            fwd.wait_send()
        for k in range(n):
            src = _sub(ins[k], (layer,), shard2d[k], axes[k], half=c)
            mine = _sub(outs[k], (), full2d[k], axes[k], chip=me, half=c)
            for j, chip in enumerate(chips):
                _remote(src, mine, send_sems.at[k, j], recv_sems.at[k, j], (*chip, c)).wait_send()
            own_copy(ins, outs, send_sems, recv_sems, k).wait()
        if conv_w is not None:
            csend, crecv = scr[2:]
            for j, chip in enumerate(chips):
                cp = _remote(ins[n], conv_slab(outs[n], chip), csend.at[j], crecv.at[j], (*chip, c))
                cp.wait_recv()
                cp.wait_send()
            _remote(ins[n], conv_slab(outs[n], (x, y)), csend.at[3], crecv.at[3], sibling).wait()

    return _Rider(inputs, out_shapes, scratch, start, finish)


def _pair_rider(stacks, axes, layer):
    n = len(stacks)
    shapes = [t.shape[1:] for t in stacks]
    out_shapes = [SDS(_scaled(s2, 1 - ax, 1, 2), F32) for s2, ax in zip(shapes, axes)]
    scratch = [pltpu.SemaphoreType.DMA((n,)), pltpu.SemaphoreType.DMA((n,))]

    def copies(ins, outs, scr):
        x, y, c = _position()
        return [
            _remote(_sub(ins[k], (layer,), shapes[k], axes[k], half=1 - c), outs[k], scr[0].at[k], scr[1].at[k], (x, y, 1 - c))
            for k in range(n)
        ]

    def start(ins, outs, scr):
        for cp in copies(ins, outs, scr):
            cp.start()

    def finish(ins, outs, scr):
        for cp in copies(ins, outs, scr):
            cp.wait()

    return _Rider(stacks, out_shapes, scratch, start, finish)


def _chip_rider(pairs, axes):
    n = len(pairs)
    shapes = [t.shape for t in pairs]
    out_shapes = [SDS((3,) + _scaled(s2, ax, 1, N_CHIPS), t.dtype) for s2, ax, t in zip(shapes, axes, pairs)]
    scratch = [pltpu.SemaphoreType.DMA((n, 3)), pltpu.SemaphoreType.DMA((n, 3))]

    def copies(ins, outs, scr):
        x, y, c = _position()
        return [
            _remote(
                _sub(ins[k], (), shapes[k], axes[k], chip=2 * chip[0] + chip[1]), outs[k].at[j],
                scr[0].at[k, j], scr[1].at[k, j], (*chip, c),
            )
            for k in range(n)
            for j, chip in enumerate(_other_chips(x, y))
        ]

    def start(ins, outs, scr):
        for cp in copies(ins, outs, scr):
            cp.start()

    def finish(ins, outs, scr):
        for cp in copies(ins, outs, scr):
            cp.wait()

    return _Rider(pairs, out_shapes, scratch, start, finish)


def _complete_rider(stacks, axes, layers):
    n = len(stacks)
    shapes = [t.shape[1:] for t in stacks]
    scratch = [pltpu.SemaphoreType.DMA((n,)), pltpu.SemaphoreType.DMA((n,))]

    def start(ins, outs, scr):
        x, y, c = _position()
        for k in range(n):
            mine = _sub(outs[k], (layers[k],), shapes[k], axes[k], half=c)
            _remote(mine, mine, scr[0].at[k], scr[1].at[k], (x, y, 1 - c)).start()

    def finish(ins, outs, scr):
        x, y, c = _position()
        for k in range(n):
            mine = _sub(outs[k], (layers[k],), shapes[k], axes[k], half=c)
            theirs = _sub(outs[k], (layers[k],), shapes[k], axes[k], half=1 - c)
            _remote(mine, theirs, scr[0].at[k], scr[1].at[k], (x, y, 1 - c)).wait_recv()
            _remote(mine, mine, scr[0].at[k], scr[1].at[k], (x, y, 1 - c)).wait_send()

    return _Rider(stacks, [SDS(t.shape, t.dtype) for t in stacks], scratch, start, finish, {k: k for k in range(n)})


def _small_rider(buf):
    rows, cols = buf.shape
    flips = [(fx, fy, fc) for fx in (0, 1) for fy in (0, 1) for fc in (0, 1)][1:]

    def peers():
        x, y, c = _position()
        return [(x + f[0] - 2 * x * f[0], y + f[1] - 2 * y * f[1], c + f[2] - 2 * c * f[2]) for f in flips]

    def start(ins, outs, scr):
        x, y, c = _position()
        mine = outs[0].at[4 * x + 2 * y + c]
        pltpu.make_async_copy(ins[0], mine, scr[2]).start()
        for j, peer in enumerate(peers()):
            _remote(ins[0], mine, scr[0].at[j], scr[1].at[j], peer).start()

    def finish(ins, outs, scr):
        x, y, c = _position()
        mine = outs[0].at[4 * x + 2 * y + c]
        for j, peer in enumerate(peers()):
            _remote(ins[0], outs[0].at[4 * peer[0] + 2 * peer[1] + peer[2]], scr[0].at[j], scr[1].at[j], peer).wait_recv()
        for j, peer in enumerate(peers()):
            _remote(ins[0], mine, scr[0].at[j], scr[1].at[j], peer).wait_send()
        pltpu.make_async_copy(ins[0], mine, scr[2]).wait()

    scratch = [pltpu.SemaphoreType.DMA((7,)), pltpu.SemaphoreType.DMA((7,)), pltpu.SemaphoreType.DMA(())]
    return _Rider([buf], [SDS((8, rows, cols), buf.dtype)], scratch, start, finish)


def _my_core():
    return lax.axis_index("c")


def _my_chip():
    return 2 * lax.axis_index("x") + lax.axis_index("y")


def _pair_sum(stack, layer, recv, axis, name):
    hk, hn = recv.shape
    tk = _divisor(hk, max(16, TILE_BYTES // (4 * hn)), 16)
    per = hk // tk

    def body(g_ref, r_ref, out_ref):
        out_ref[...] = (g_ref[...] + r_ref[...]).astype(out_ref.dtype)

    if axis == 1:
        mine = pl.BlockSpec((None, tk, hn), lambda i: (layer, _my_core() * per + i, 0))
    else:
        mine = pl.BlockSpec((None, tk, hn), lambda i: (layer, i, _my_core()))
    return pl.pallas_call(
        body,
        name=name,
        grid=(per,),
        in_specs=[mine, pl.BlockSpec((tk, hn), lambda i: (i, 0))],
        out_specs=pl.BlockSpec((tk, hn), lambda i: (i, 0)),
        out_shape=SDS((hk, hn), BF16),
        compiler_params=_params("parallel"),
    )(stack, recv)


def _chip_sum(pair, others, axis, stack, layer, name):
    _, sk, sn = others.shape
    tk = _divisor(sk, max(16, TILE_BYTES // (4 * sn)), 16)
    per = sk // tk

    def body(p_ref, o0_ref, o1_ref, o2_ref, *rest):
        out_ref = rest[-1]
        acc = p_ref[...].astype(F32) + o0_ref[...].astype(F32)
        out_ref[...] = (acc + o1_ref[...].astype(F32)) + o2_ref[...].astype(F32)

    if axis == 1:
        own = pl.BlockSpec((tk, sn), lambda i: (i, _my_chip()))
        out = pl.BlockSpec((None, tk, sn), lambda i: (layer, _my_core() * per + i, 0))
        shard = (2 * sk, sn)
    else:
        own = pl.BlockSpec((tk, sn), lambda i: (_my_chip() * per + i, 0))
        out = pl.BlockSpec((None, tk, sn), lambda i: (layer, i, _my_core()))
        shard = (sk, 2 * sn)
    other = lambda j: pl.BlockSpec((None, tk, sn), lambda i: (j, i, 0))
    in_specs = [own, other(0), other(1), other(2)]
    args = [pair, others, others, others]
    aliases = {}
    if stack is not None:
        in_specs.append(ANY)
        args.append(stack)
        aliases = {4: 0}
    return pl.pallas_call(
        body,
        name=name,
        grid=(per,),
        in_specs=in_specs,
        out_specs=out,
        out_shape=SDS((DEPTH,) + shard, F32),
        input_output_aliases=aliases,
        compiler_params=_params("parallel"),
    )(*args)


def _sum_devices(gathered, name):
    _, rows, cols = gathered.shape

    def body(g_ref, out_ref):
        acc = g_ref[0]
        for dev in range(1, 8):
            acc = acc + g_ref[dev]
        out_ref[...] = acc

    return pl.pallas_call(body, name=name, out_shape=SDS((rows, cols), F32))(gathered)


PACK_TILE = 8 * 128


def _pack(arrays):
    parts = []
    for t in arrays:
        flat = t.reshape(-1)
        pad = (-flat.shape[0]) % PACK_TILE
        if pad:
            flat = jnp.concatenate([flat, jnp.zeros((pad,), flat.dtype)])
        parts.append(flat.reshape(-1, 128))
    return jnp.concatenate(parts, axis=0)


def _unpack(buf, shapes):
    out, row = [], 0
    for shp in shapes:
        size = math.prod(shp)
        rows = -(-size // PACK_TILE) * 8
        out.append(buf[row : row + rows].reshape(-1)[:size].reshape(shp))
        row += rows
    return out


BIG = ("w_in", "w_br_gm", "w_br_hg", "w_out", "w_up", "w_down")
BIG_AXIS = (1, 1, 1, 0, 1, 0)
LAYER_SMALL = ("mix_norm", "gm_ln_g", "gm_ln_b", "gm_ws", "gm_bs", "lb", "hg_norm_g", "ffn_norm", "conv_w", "conv_b")
WEIGHTS = ("mix_norm", "w_in", "gm_ln_g", "gm_ln_b", "gm_ws", "gm_bs", "hg_lb_logits", "hg_norm_g", "w_br_gm", "w_br_hg", "w_out",
           "ffn_norm", "w_up", "conv_w", "conv_b", "w_down", "final_norm")


def kernel(x, mix_norm, w_in, gm_ln_g, gm_ln_b, gm_ws, gm_bs, hg_lb_logits, hg_norm_g, w_br_gm, w_br_hg, w_out, ffn_norm, w_up, conv_w, conv_b, w_down, final_norm, loss_target, m_mix_norm, m_w_in, m_gm_ln_g, m_gm_ln_b, m_gm_ws, m_gm_bs, m_hg_lb_logits, m_hg_norm_g, m_w_br_gm, m_w_br_hg, m_w_out, m_ffn_norm, m_w_up, m_conv_w, m_conv_b, m_w_down, m_final_norm, v_mix_norm, v_w_in, v_gm_ln_g, v_gm_ln_b, v_gm_ws, v_gm_bs, v_hg_lb_logits, v_hg_norm_g, v_w_br_gm, v_w_br_hg, v_w_out, v_ffn_norm, v_w_up, v_conv_w, v_conv_b, v_w_down, v_final_norm):
    w = dict(mix_norm=mix_norm, w_in=w_in, gm_ln_g=gm_ln_g, gm_ln_b=gm_ln_b, gm_ws=gm_ws, gm_bs=gm_bs, hg_lb_logits=hg_lb_logits,
             hg_norm_g=hg_norm_g, w_br_gm=w_br_gm, w_br_hg=w_br_hg, w_out=w_out, ffn_norm=ffn_norm, w_up=w_up, conv_w=conv_w,
             conv_b=conv_b, w_down=w_down, final_norm=final_norm)
    m = dict(mix_norm=m_mix_norm, w_in=m_w_in, gm_ln_g=m_gm_ln_g, gm_ln_b=m_gm_ln_b, gm_ws=m_gm_ws, gm_bs=m_gm_bs,
             hg_lb_logits=m_hg_lb_logits, hg_norm_g=m_hg_norm_g, w_br_gm=m_w_br_gm, w_br_hg=m_w_br_hg, w_out=m_w_out,
             ffn_norm=m_ffn_norm, w_up=m_w_up, conv_w=m_conv_w, conv_b=m_conv_b, w_down=m_w_down, final_norm=m_final_norm)
    v = dict(mix_norm=v_mix_norm, w_in=v_w_in, gm_ln_g=v_gm_ln_g, gm_ln_b=v_gm_ln_b, gm_ws=v_gm_ws, gm_bs=v_gm_bs,
             hg_lb_logits=v_hg_lb_logits, hg_norm_g=v_hg_norm_g, w_br_gm=v_w_br_gm, w_br_hg=v_w_br_hg, w_out=v_w_out,
             ffn_norm=v_ffn_norm, w_up=v_w_up, conv_w=v_conv_w, conv_b=v_conv_b, w_down=v_w_down, final_norm=v_final_norm)

    axes = list(BIG_AXIS)
    d = x.shape[2]

    shards = [w[k].astype(MM_DTYPE) for k in BIG]
    w_in_0, conv_full = _run_rider(_gather_rider(shards[:1], axes[:1], 0, conv_w=conv_w), "gather_weights_0")
    full = [dict(w_in=w_in_0)]
    lb = _lower_bounds(hg_lb_logits)
    xs = x[0]
    saved = []
    mixer = 4
    for l in range(DEPTH):
        fw = full[l]
        more = l + 1 < DEPTH
        bst = gm_bs[l].T
        rider = _gather_rider(shards[mixer:], axes[mixer:], 0) if l == 0 else None
        (z, h, a), got = _in_proj(
            xs, mix_norm[l : l + 1], fw["w_in"], gm_ln_g[l : l + 1], gm_ln_b[l : l + 1], gm_ws[l], bst, f"in_proj_{l}", rider
        )
        if l == 0:
            fw.update(zip(BIG[mixer:], got))
        rider = _gather_rider(shards[1:mixer], axes[1:mixer], 0) if l == 0 else None
        (o, b, states), got = _hgrn_fwd(z, lb[l : l + 1], hg_norm_g[l : l + 1], f"hgrn_fwd_{l}", rider)
        if l == 0:
            fw.update(zip(BIG[1:mixer], got))
        (x1,), _ = _mix_fwd(xs, a, b, z, fw["w_br_gm"], fw["w_br_hg"], fw["w_out"], f"mix_fwd_{l}")
        rider = _gather_rider(shards, axes, l + 1) if more else None
        (x2, z2, h2, conv), got = _ffn_fwd(
            x1, ffn_norm[l : l + 1], fw["w_up"], conv_full, conv_b[l : l + 1], fw["w_down"], l, f"ffn_fwd_{l}", rider
        )
        if more:
            full.append(dict(zip(BIG, got)))
        saved.append((xs, h, z, a, b, o, states, x1, h2, z2, conv, bst))
        xs = x2

    loss_tile, dx, d_final = _loss_head(xs, final_norm.reshape(1, d), loss_target[0])
    loss = lax.psum(loss_tile[0, 0], ("x", "y", "c"))

    big = {k: None for k in BIG}
    grads = [None] * len(BIG)
    per_layer = [None] * DEPTH
    mix_ids, ffn_ids = list(range(mixer)), list(range(mixer, len(BIG)))
    mix_axes, ffn_axes = axes[:mixer], axes[mixer:]
    mix_pairs = None
    mix_summed = None

    def pair_sums(ids, received, layer):
        return [_pair_sum(big[BIG[k]], layer, r, axes[k], f"pair_sum_{BIG[k]}_{layer}") for k, r in zip(ids, received)]

    def chip_sums(ids, pairs, others, layer):
        for k, p, ot in zip(ids, pairs, others):
            grads[k] = _chip_sum(p, ot, axes[k], grads[k], layer, f"chip_sum_{BIG[k]}_{layer}")

    for l in reversed(range(DEPTH)):
        x0, h, z, a, b, o, states, x1, h2, z2, conv, bst = saved[l]
        fw = full[l]
        dz2, act, dconv, dx1, d_ffn = _ffn_bwd(
            dx, z2, conv, conv_full, fw["w_down"], fw["w_up"], x1, ffn_norm[l : l + 1], l, f"ffn_bwd_{l}"
        )
        big["w_down"] = _matmul_tn(act, dx, big["w_down"], l, f"dw_down_{l}")
        big["w_up"] = _matmul_tn(h2, dz2, big["w_up"], l, f"dw_up_{l}")
        rider = _pair_rider([big[BIG[k]] for k in ffn_ids], ffn_axes, l)
        if mix_pairs is not None:
            rider = _join(rider, _chip_rider(mix_pairs, mix_axes))
        (y, dpa, dpb, db, dgates, dz_gm, d_lng, d_lnb, d_ws, d_bst), got = _mix_bwd(
            dx1, a, b, z, fw["w_br_gm"], fw["w_br_hg"], fw["w_out"], gm_ln_g[l : l + 1], gm_ln_b[l : l + 1], gm_ws[l], bst,
            f"mix_bwd_{l}", rider,
        )
        if mix_pairs is not None:
            chip_sums(mix_ids, mix_pairs, got[len(ffn_ids) :], l + 1)
            mix_summed = l + 1
        ffn_pairs = pair_sums(ffn_ids, got[: len(ffn_ids)], l)
        big["w_out"] = _matmul_tn(y, dx1, big["w_out"], l, f"dw_out_{l}")
        big["w_br_gm"] = _matmul_tn(a, dpa, big["w_br_gm"], l, f"dw_br_gm_{l}")
        big["w_br_hg"] = _matmul_tn(b, dpb, big["w_br_hg"], l, f"dw_br_hg_{l}")
        (dz_hg, d_lb, d_ng), others = _hgrn_bwd(
            db, z, o, states, lb[l : l + 1], hg_norm_g[l : l + 1], f"hgrn_bwd_{l}", _chip_rider(ffn_pairs, ffn_axes)
        )
        chip_sums(ffn_ids, ffn_pairs, others, l)
        n_in = fw["w_in"].shape[1]
        big["w_in"] = _matmul_tn(h, dz_gm, big["w_in"], l, f"dw_in_gm_{l}", n_total=n_in, col_off=0)
        big["w_in"] = _matmul_tn(h, dz_hg, big["w_in"], l, f"dw_in_hg_{l}", n_total=n_in, col_off=2 * BR_DIM)
        big["w_in"] = _matmul_tn(h, dgates, big["w_in"], l, f"dw_in_gate_{l}", n_total=n_in, col_off=6 * BR_DIM)
        done_ids = ffn_ids + (mix_ids if mix_summed is not None else [])
        done_layers = [l] * len(ffn_ids) + ([mix_summed] * len(mix_ids) if mix_summed is not None else [])
        rider = _join(
            _pair_rider([big[BIG[k]] for k in mix_ids], mix_axes, l),
            _complete_rider([grads[k] for k in done_ids], [axes[k] for k in done_ids], done_layers),
        )
        if l == 0:
            upper = [_pack([per_layer[j][k] for k in LAYER_SMALL]) for j in range(1, DEPTH)] + [_pack([d_final[0]])]
            rider = _join(rider, _small_rider(jnp.concatenate(upper, axis=0)))
        (dx, d_mix), got = _in_proj_bwd([dz_gm, dz_hg, dgates], fw["w_in"], x0, mix_norm[l : l + 1], dx1, f"in_proj_bwd_{l}", rider)
        for k, g in zip(done_ids, got[len(mix_ids) : len(mix_ids) + len(done_ids)]):
            grads[k] = g
        gathered_upper = got[len(mix_ids) + len(done_ids) :]
        mix_pairs = pair_sums(mix_ids, got[: len(mix_ids)], l)
        per_layer[l] = dict(
            mix_norm=d_mix[0], gm_ln_g=d_lng[0], gm_ln_b=d_lnb[0], gm_ws=d_ws, gm_bs=d_bst.T, lb=d_lb[0], hg_norm_g=d_ng[0],
            ffn_norm=d_ffn[0], conv_w=dconv[0:3], conv_b=dconv[3],
        )

    got = _run_rider(
        _join(_chip_rider(mix_pairs, mix_axes), _small_rider(_pack([per_layer[0][k] for k in LAYER_SMALL]))), "grad_exchange_0"
    )
    chip_sums(mix_ids, mix_pairs, got[: len(mix_ids)], 0)
    done = _run_rider(_complete_rider([grads[k] for k in mix_ids], mix_axes, [0] * len(mix_ids)), "grad_complete_0")
    for k, g in zip(mix_ids, done):
        grads[k] = g
    grad = dict(zip(BIG, grads))

    layer_shapes = [per_layer[0][k].shape for k in LAYER_SMALL]
    summed_upper = _unpack(_sum_devices(gathered_upper[0], "sum_small_upper"), layer_shapes * (DEPTH - 1) + [d_final[0].shape])
    summed_0 = _unpack(_sum_devices(got[len(mix_ids)], "sum_small_0"), layer_shapes)
    by_layer = [summed_0] + [summed_upper[j * len(LAYER_SMALL) : (j + 1) * len(LAYER_SMALL)] for j in range(DEPTH - 1)]
    small = {k: jnp.stack([by_layer[j][i] for j in range(DEPTH)]) for i, k in enumerate(LAYER_SMALL)}
    for k in LAYER_SMALL:
        if k != "lb":
            grad[k] = small[k]
    grad["hg_lb_logits"] = _lower_bounds_bwd(hg_lb_logits, small["lb"])
    grad["final_norm"] = summed_upper[-1]
    grad["conv_w"] = lax.dynamic_slice_in_dim(grad["conv_w"], _my_chip() * conv_w.shape[2], conv_w.shape[2], axis=2)

    delta, new_m, new_v = {}, {}, {}
    for k in WEIGHTS:
        delta[k], new_m[k], new_v[k] = _adamw(w[k], grad[k], m[k], v[k], f"adamw_{k}")

    return (loss, dx[None], *[grad[k] for k in WEIGHTS], *[delta[k] for k in WEIGHTS], *[new_m[k] for k in WEIGHTS],
            *[new_v[k] for k in WEIGHTS])
```

```python
import math

import jax
import jax.numpy as jnp
from jax import lax
from jax.experimental import pallas as pl
from jax.experimental.pallas import tpu as pltpu

F32 = jnp.float32
BF16 = jnp.bfloat16
MM_DTYPE = BF16

N_HEADS = 4
HEAD_DIM = 128
BR_DIM = N_HEADS * HEAD_DIM
CHUNK = 64
GM_BLOCK = 128
DEPTH = 4
N_CHIPS = 4
EPS = 1e-6
TINY = 1e-30
LB_MAX = 0.999
ADAM_LR = 0.001
ADAM_B1 = 0.9
ADAM_B2 = 0.999
ADAM_EPS = 1e-08
ADAM_WD = 0.01
ADAM_STEP = 10
INV_SQRT2 = 1.0 / math.sqrt(2.0)
INV_SQRT_2PI = 1.0 / math.sqrt(2.0 * math.pi)

VMEM_LIMIT_BYTES = 56 * 1024 * 1024
TILE_BYTES = 2 * 1024 * 1024
ACC_BYTES = 6 * 1024 * 1024
MESH = pl.DeviceIdType.MESH
SDS = jax.ShapeDtypeStruct
ANY = pl.BlockSpec(memory_space=pl.ANY)


def _params(*sem):
    return pltpu.CompilerParams(dimension_semantics=sem or None, vmem_limit_bytes=VMEM_LIMIT_BYTES)


def _divisor(n, limit, mult):
    best = None
    for d in range(mult, min(n, limit) + 1, mult):
        if n % d == 0:
            best = d
    return best if best is not None else n


def _dot(a, b):
    return jnp.dot(a.astype(MM_DTYPE), b.astype(MM_DTYPE), preferred_element_type=F32)


def _dot_nt(a, b):
    return lax.dot_general(a.astype(MM_DTYPE), b.astype(MM_DTYPE), (((1,), (1,)), ((), ())), preferred_element_type=F32)


def _dot_tn(a, b):
    return lax.dot_general(a.astype(MM_DTYPE), b.astype(MM_DTYPE), (((0,), (0,)), ((), ())), preferred_element_type=F32)


def _sigmoid(x):
    return 1.0 / (1.0 + jnp.exp(-x))


def _gelu(x):
    return 0.5 * x * (1.0 + lax.erf(x * INV_SQRT2))


def _gelu_grad(x):
    return 0.5 * (1.0 + lax.erf(x * INV_SQRT2)) + x * jnp.exp(-0.5 * x * x) * INV_SQRT_2PI


def _silu_grad(x, s):
    return s * (1.0 + x * (1.0 - s))


def _resident(shape, index_map):
    return pl.BlockSpec(shape, index_map, pipeline_mode=pl.Buffered(1))


class _Rider:
    def __init__(self, inputs, out_shapes, scratch, start, finish, aliases=None):
        self.inputs = list(inputs)
        self.out_shapes = list(out_shapes)
        self.scratch = list(scratch)
        self.start = start
        self.finish = finish
        self.aliases = dict(aliases or {})


def _join(a, b):
    if a is None or b is None:
        return a if b is None else b
    na, oa, sa = len(a.inputs), len(a.out_shapes), len(a.scratch)

    def start(i, o, s):
        a.start(i[:na], o[:oa], s[:sa])
        b.start(i[na:], o[oa:], s[sa:])

    def finish(i, o, s):
        a.finish(i[:na], o[:oa], s[:sa])
        b.finish(i[na:], o[oa:], s[sa:])

    aliases = dict(a.aliases)
    aliases.update({na + k: oa + v for k, v in b.aliases.items()})
    return _Rider(a.inputs + b.inputs, a.out_shapes + b.out_shapes, a.scratch + b.scratch, start, finish, aliases)


def _hosted_call(body, rider, *, name, grid, in_specs, out_specs, out_shape, scratch_shapes, args, semantics):
    n_in, n_out, n_scr = len(in_specs), len(out_specs), len(scratch_shapes)
    if rider is None:
        outs = pl.pallas_call(
            body, name=name, grid=grid, in_specs=in_specs, out_specs=out_specs, out_shape=out_shape,
            scratch_shapes=scratch_shapes, compiler_params=_params(*semantics),
        )(*args)
        return list(outs), []
    ri, ro = len(rider.inputs), len(rider.out_shapes)

    def wrapped(*refs):
        p = 0
        hin = refs[p : p + n_in]
        p += n_in
        rin = refs[p : p + ri]
        p += ri
        hout = refs[p : p + n_out]
        p += n_out
        rout = refs[p : p + ro]
        p += ro
        hscr = refs[p : p + n_scr]
        rscr = refs[p + n_scr :]

        @pl.when(pl.program_id(0) == 0)
        def _():
            rider.start(rin, rout, rscr)

        body(*hin, *hout, *hscr)

        @pl.when(pl.program_id(0) == grid[0] - 1)
        def _():
            rider.finish(rin, rout, rscr)

    outs = pl.pallas_call(
        wrapped,
        name=name,
        grid=grid,
        in_specs=list(in_specs) + [ANY] * ri,
        out_specs=list(out_specs) + [ANY] * ro,
        out_shape=list(out_shape) + rider.out_shapes,
        scratch_shapes=list(scratch_shapes) + rider.scratch,
        input_output_aliases={n_in + k: n_out + v for k, v in rider.aliases.items()},
        compiler_params=_params(*semantics),
    )(*args, *rider.inputs)
    return list(outs[:n_out]), list(outs[n_out:])


def _run_rider(rider, name):
    ri, ro = len(rider.inputs), len(rider.out_shapes)

    def body(*refs):
        rin, rout, rscr = refs[:ri], refs[ri : ri + ro], refs[ri + ro :]
        rider.start(rin, rout, rscr)
        rider.finish(rin, rout, rscr)

    return list(
        pl.pallas_call(
            body,
            name=name,
            in_specs=[ANY] * ri,
            out_specs=[ANY] * ro,
            out_shape=rider.out_shapes,
            scratch_shapes=rider.scratch,
            input_output_aliases=rider.aliases,
        )(*rider.inputs)
    )


def _lower_bounds(logits):
    def body(lg_ref, lb_ref):
        lg = lg_ref[...]
        mx = jnp.max(lg, axis=0, keepdims=True)
        e = jnp.exp(lg - mx)
        p = e / jnp.sum(e, axis=0, keepdims=True)
        run = p[0:1]
        rows = [run - p[0:1]]
        for l in range(1, DEPTH):
            run = run + p[l : l + 1]
            rows.append(run - p[0:1])
        raw = jnp.concatenate(rows, axis=0)
        lb_ref[...] = jnp.minimum(jnp.maximum(raw, 0.0), LB_MAX)

    return pl.pallas_call(body, name="lower_bounds", out_shape=SDS(logits.shape, F32))(logits)


def _lower_bounds_bwd(logits, dlb):
    def body(lg_ref, dlb_ref, dlg_ref):
        lg = lg_ref[...]
        mx = jnp.max(lg, axis=0, keepdims=True)
        e = jnp.exp(lg - mx)
        p = e / jnp.sum(e, axis=0, keepdims=True)
        run = p[0:1]
        rows = [run - p[0:1]]
        for l in range(1, DEPTH):
            run = run + p[l : l + 1]
            rows.append(run - p[0:1])
        raw = jnp.concatenate(rows, axis=0)
        lo = jnp.where(raw > 0.0, 1.0, jnp.where(raw == 0.0, 0.5, 0.0))
        clipped = jnp.maximum(raw, 0.0)
        hi = jnp.where(clipped < LB_MAX, 1.0, jnp.where(clipped == LB_MAX, 0.5, 0.0))
        draw = dlb_ref[...] * lo * hi
        total = jnp.sum(draw, axis=0, keepdims=True)
        dps = []
        tail = total
        for j in range(DEPTH):
            dps.append(tail - total if j == 0 else tail)
            tail = tail - draw[j : j + 1]
        dp = jnp.concatenate(dps, axis=0)
        dlg_ref[...] = p * (dp - jnp.sum(p * dp, axis=0, keepdims=True))

    return pl.pallas_call(body, name="lower_bounds_bwd", out_shape=SDS(logits.shape, F32))(logits, dlb)


def _in_proj(x, gain, w, lng, lnb, ws, bst, name, rider=None):
    s, d = x.shape
    n = w.shape[1]
    tm = _divisor(s, 512, GM_BLOCK)

    def body(x_ref, g_ref, w_ref, lng_ref, lnb_ref, ws_ref, bst_ref, z_ref, h_ref, a_ref):
        xv = x_ref[...]
        r = lax.rsqrt(jnp.mean(xv * xv, axis=-1, keepdims=True) + EPS)
        h = ((xv * r) * g_ref[...]).astype(MM_DTYPE)
        h_ref[...] = h
        uv = jnp.dot(h, w_ref[:, : 2 * BR_DIM], preferred_element_type=F32)
        z_ref[:, : 2 * BR_DIM] = uv
        z_ref[:, 2 * BR_DIM :] = jnp.dot(h, w_ref[:, 2 * BR_DIM :], preferred_element_type=F32)
        _gmlp_fwd_tile(uv[:, :BR_DIM], uv[:, BR_DIM:], lng_ref, lnb_ref, ws_ref, bst_ref, a_ref)

    row = lambda wd: pl.BlockSpec((tm, wd), lambda i: (i, 0))
    return _hosted_call(
        body,
        rider,
        name=name,
        grid=(s // tm,),
        in_specs=[row(d), _resident((1, d), lambda i: (0, 0)), _resident((d, n), lambda i: (0, 0))] + _gmlp_param_specs(),
        out_specs=[row(n), row(d), row(BR_DIM)],
        out_shape=[SDS((s, n), F32), SDS((s, d), MM_DTYPE), SDS((s, BR_DIM), MM_DTYPE)],
        scratch_shapes=[],
        args=[x, gain, w, lng, lnb, ws, bst],
        semantics=("arbitrary",),
    )


def _in_proj_bwd(dz_parts, w, x, gain, dres, name, rider=None):
    s, d = x.shape
    n = w.shape[1]
    tm = _divisor(s, 512, 16)
    widths = [p.shape[1] for p in dz_parts]
    offsets = [sum(widths[:k]) for k in range(len(widths))]
    n_parts = len(dz_parts)

    def body(*refs):
        dz_refs = refs[:n_parts]
        w_ref, x_ref, g_ref, dres_ref, dx_ref, dg_ref = refs[n_parts:]

        @pl.when(pl.program_id(0) == 0)
        def _():
            dg_ref[...] = jnp.zeros_like(dg_ref)

        dh = None
        for dz_ref, off, wd in zip(dz_refs, offsets, widths):
            part = _dot_nt(dz_ref[...], w_ref[:, off : off + wd])
            dh = part if dh is None else dh + part
        xv = x_ref[...]
        r = lax.rsqrt(jnp.mean(xv * xv, axis=-1, keepdims=True) + EPS)
        xh = xv * r
        dg_ref[...] += jnp.sum(dh * xh, axis=0, keepdims=True)
        dxh = dh * g_ref[...]
        dx_ref[...] = dres_ref[...] + r * (dxh - xh * jnp.mean(dxh * xh, axis=-1, keepdims=True))

    in_specs = [pl.BlockSpec((tm, wd), lambda i: (i, 0)) for wd in widths] + [
        _resident((d, n), lambda i: (0, 0)),
        pl.BlockSpec((tm, d), lambda i: (i, 0)),
        _resident((1, d), lambda i: (0, 0)),
        pl.BlockSpec((tm, d), lambda i: (i, 0)),
    ]
    return _hosted_call(
        body,
        rider,
        name=name,
        grid=(s // tm,),
        in_specs=in_specs,
        out_specs=[pl.BlockSpec((tm, d), lambda i: (i, 0)), pl.BlockSpec((1, d), lambda i: (0, 0))],
        out_shape=[SDS((s, d), F32), SDS((1, d), F32)],
        scratch_shapes=[],
        args=[*dz_parts, w, x, gain, dres],
        semantics=("arbitrary",),
    )


def _matmul_tn(a, b, stack, layer, name, n_total=None, col_off=0):
    s, k = a.shape
    n = b.shape[1]
    n_total = n if n_total is None else n_total
    tn = _divisor(math.gcd(n, col_off) if col_off else n, max(128, ACC_BYTES // (4 * k)), 128)
    ts = _divisor(s, min(2048 if n > tn else 1024, max(512, ACC_BYTES // (2 * k))), 16)
    off = col_off // tn

    def body(a_ref, b_ref, *rest):
        out_ref = rest[-1]

        @pl.when(pl.program_id(1) == 0)
        def _():
            out_ref[...] = jnp.zeros_like(out_ref)

        out_ref[...] += _dot_tn(a_ref[...], b_ref[...])

    in_specs = [pl.BlockSpec((ts, k), lambda j, i: (i, 0)), pl.BlockSpec((ts, tn), lambda j, i: (i, j))]
    args = [a, b]
    aliases = {}
    if stack is not None:
        in_specs.append(ANY)
        args.append(stack)
        aliases = {2: 0}
    return pl.pallas_call(
        body,
        name=name,
        grid=(n // tn, s // ts),
        in_specs=in_specs,
        out_specs=pl.BlockSpec((None, k, tn), lambda j, i: (layer, 0, off + j)),
        out_shape=SDS((DEPTH, k, n_total), F32),
        input_output_aliases=aliases,
        compiler_params=_params("parallel", "arbitrary"),
    )(*args)


def _gm_mask():
    r = lax.broadcasted_iota(jnp.int32, (GM_BLOCK, GM_BLOCK), 0) // CHUNK
    c = lax.broadcasted_iota(jnp.int32, (GM_BLOCK, GM_BLOCK), 1) // CHUNK
    return r >= c


def _gm_normed(v, lng, lnb):
    vg = _gelu(v)
    mu = jnp.mean(vg, axis=-1, keepdims=True)
    xc = vg - mu
    rstd = lax.rsqrt(jnp.mean(xc * xc, axis=-1, keepdims=True) + EPS)
    xh = xc * rstd
    return xh, rstd, xh * lng + lnb


def _gmlp_param_specs():
    return [
        _resident((1, BR_DIM), lambda i: (0, 0)),
        _resident((1, BR_DIM), lambda i: (0, 0)),
        _resident((N_HEADS, GM_BLOCK, GM_BLOCK), lambda i: (0, 0, 0)),
        _resident((GM_BLOCK, N_HEADS), lambda i: (0, 0)),
    ]


def _gmlp_fwd_tile(u, v, lng_ref, lnb_ref, ws_ref, bst_ref, a_ref):
    mask = _gm_mask()
    ug = _gelu(u)
    _, _, vn = _gm_normed(v, lng_ref[...], lnb_ref[...])
    vn = vn.astype(MM_DTYPE)
    for h in range(N_HEADS):
        cs = slice(h * HEAD_DIM, (h + 1) * HEAD_DIM)
        wm = jnp.where(mask, ws_ref[h], 0.0).astype(MM_DTYPE)
        for blk in range(u.shape[0] // GM_BLOCK):
            rs = slice(blk * GM_BLOCK, (blk + 1) * GM_BLOCK)
            mixed = _dot(wm, vn[rs, cs]) + bst_ref[:, h : h + 1]
            a_ref[rs, cs] = (ug[rs, cs] * mixed).astype(MM_DTYPE)


def _gmlp_bwd_tile(da_v, u, v, lng_ref, lnb_ref, ws_ref, bst_ref, dz_ref, dlng_ref, dlnb_ref, dws_ref, dbst_ref):
    mask = _gm_mask()
    ug = _gelu(u)
    lng_v = lng_ref[...]
    xh, rstd, vn = _gm_normed(v, lng_v, lnb_ref[...])
    vnb = vn.astype(MM_DTYPE)
    dmix = da_v * ug
    dmixb = dmix.astype(MM_DTYPE)
    dvn_cols = []
    for h in range(N_HEADS):
        cs = slice(h * HEAD_DIM, (h + 1) * HEAD_DIM)
        wm = jnp.where(mask, ws_ref[h], 0.0).astype(MM_DTYPE)
        dw = jnp.zeros((GM_BLOCK, GM_BLOCK), F32)
        dbias = jnp.zeros((GM_BLOCK, 1), F32)
        dvn_rows = []
        for blk in range(u.shape[0] // GM_BLOCK):
            rs = slice(blk * GM_BLOCK, (blk + 1) * GM_BLOCK)
            mixed = _dot(wm, vnb[rs, cs]) + bst_ref[:, h : h + 1]
            dz_ref[rs, cs] = (da_v[rs, cs] * mixed * _gelu_grad(u[rs, cs])).astype(MM_DTYPE)
            dw = dw + _dot_nt(dmixb[rs, cs], vnb[rs, cs])
            dbias = dbias + jnp.sum(dmix[rs, cs], axis=1, keepdims=True)
            dvn_rows.append(_dot_tn(wm, dmixb[rs, cs]))
        dws_ref[h] += jnp.where(mask, dw, 0.0)
        dbst_ref[:, h : h + 1] += dbias
        dvn_cols.append(jnp.concatenate(dvn_rows, axis=0) if len(dvn_rows) > 1 else dvn_rows[0])
    dvn = jnp.concatenate(dvn_cols, axis=1)
    dlng_ref[...] += jnp.sum(dvn * xh, axis=0, keepdims=True)
    dlnb_ref[...] += jnp.sum(dvn, axis=0, keepdims=True)
    dxh = dvn * lng_v
    dvg = rstd * (dxh - jnp.mean(dxh, axis=-1, keepdims=True) - xh * jnp.mean(dxh * xh, axis=-1, keepdims=True))
    dz_ref[:, BR_DIM:] = (dvg * _gelu_grad(v)).astype(MM_DTYPE)


HG_ROWS = 512
HG_UNROLL = 8
MID = CHUNK // 2 - 1


def _tri(lower):
    r = lax.broadcasted_iota(jnp.int32, (CHUNK, CHUNK), 0)
    c = lax.broadcasted_iota(jnp.int32, (CHUNK, CHUNK), 1)
    return r >= c if lower else c >= r


def _scan_rows(x, reverse=False):
    n = x.shape[0]
    rid = lax.broadcasted_iota(jnp.int32, x.shape, 0)
    k = 1
    while k < n:
        if reverse:
            x = x + jnp.where(rid < n - k, pltpu.roll(x, n - k, 0), 0.0)
        else:
            x = x + jnp.where(rid >= k, pltpu.roll(x, k, 0), 0.0)
        k *= 2
    return x


def _hgrn_fwd_chunk(zqf_ref, zio_ref, ci, lb_ref, ng_v, state, o_ref, b_ref, st_ref):
    low = _tri(True)
    rows = slice(ci * CHUNK, (ci + 1) * CHUNK)
    for h in range(N_HEADS):
        cs = slice(h * HEAD_DIM, (h + 1) * HEAD_DIM)
        cs2 = slice(BR_DIM + h * HEAD_DIM, BR_DIM + (h + 1) * HEAD_DIM)
        zq = zqf_ref[rows, cs]
        zf = zqf_ref[rows, cs2]
        vi = zio_ref[rows, cs]
        zog = zio_ref[rows, cs2]
        lbh = lb_ref[:, cs]
        qf = zq * _sigmoid(zq)
        forget = lbh + (1.0 - lbh) * _sigmoid(zf)
        g = jnp.log(jnp.maximum(forget, TINY))
        k = (1.0 - lbh) * _sigmoid(-zf)
        gc = _scan_rows(g)
        gm = gc[MID : MID + 1]
        gl = gc[CHUNK - 1 : CHUNK]
        a = jnp.where(low, _dot_nt(qf * jnp.exp(gc - gm), k * jnp.exp(gm - gc)), 0.0)
        st = state[h]
        st_ref[ci, h] = st
        o = _dot(a, vi) + _dot_nt(qf * jnp.exp(gc), st)
        state[h] = st * jnp.exp(gl) + _dot_tn(vi, k * jnp.exp(gl - gc))
        r = lax.rsqrt(jnp.mean(o * o, axis=-1, keepdims=True) + EPS)
        o_ref[rows, cs] = o
        b_ref[rows, cs] = (((o * r) * ng_v) * (zog * _sigmoid(zog))).astype(MM_DTYPE)


def _hgrn_fwd(z, lb, ng, name, rider=None):
    s = z.shape[0]
    rows_per = _divisor(s, HG_ROWS, CHUNK)
    n_sub = rows_per // CHUNK

    def body(zqf_ref, zio_ref, lb_ref, ng_ref, o_ref, b_ref, st_ref, state):
        @pl.when(pl.program_id(0) == 0)
        def _():
            state[...] = jnp.zeros_like(state)

        ng_v = ng_ref[...]
        for ci in range(n_sub):
            _hgrn_fwd_chunk(zqf_ref, zio_ref, ci, lb_ref, ng_v, state, o_ref, b_ref, st_ref)

    return _hosted_call(
        body,
        rider,
        name=name,
        grid=(s // rows_per,),
        in_specs=[
            pl.BlockSpec((rows_per, 2 * BR_DIM), lambda i: (i, 1)),
            pl.BlockSpec((rows_per, 2 * BR_DIM), lambda i: (i, 2)),
            _resident((1, BR_DIM), lambda i: (0, 0)),
            _resident((1, HEAD_DIM), lambda i: (0, 0)),
        ],
        out_specs=[
            pl.BlockSpec((rows_per, BR_DIM), lambda i: (i, 0)),
            pl.BlockSpec((rows_per, BR_DIM), lambda i: (i, 0)),
            pl.BlockSpec((n_sub, N_HEADS, HEAD_DIM, HEAD_DIM), lambda i: (i, 0, 0, 0)),
        ],
        out_shape=[
            SDS((s, BR_DIM), F32),
            SDS((s, BR_DIM), MM_DTYPE),
            SDS((s // CHUNK, N_HEADS, HEAD_DIM, HEAD_DIM), F32),
        ],
        scratch_shapes=[pltpu.VMEM((N_HEADS, HEAD_DIM, HEAD_DIM), F32)],
        args=[z, z, lb, ng],
        semantics=("arbitrary",),
    )


def _hgrn_bwd(db, z, o, states, lb, ng, name, rider=None):
    s = z.shape[0]
    rows_per = _divisor(s, HG_ROWS, CHUNK)
    n_sub = rows_per // CHUNK
    n_steps = s // rows_per

    def body(db_ref, zqf_ref, zio_ref, o_ref, st_ref, lb_ref, ng_ref, dz_ref, dlb_ref, dng_ref, dstate):
        @pl.when(pl.program_id(0) == 0)
        def _():
            dstate[...] = jnp.zeros_like(dstate)
            dlb_ref[...] = jnp.zeros_like(dlb_ref)
            dng_ref[...] = jnp.zeros_like(dng_ref)

        low = _tri(True)
        ng_v = ng_ref[...]

        def chunk(cc, carry):
            ci = n_sub - 1 - cc
            rows = pl.ds(pl.multiple_of(ci * CHUNK, CHUNK), CHUNK)
            for h in range(N_HEADS):
                cs = slice(h * HEAD_DIM, (h + 1) * HEAD_DIM)
                cs2 = slice(BR_DIM + h * HEAD_DIM, BR_DIM + (h + 1) * HEAD_DIM)
                zq = zqf_ref[rows, cs]
                zf = zqf_ref[rows, cs2]
                vi = zio_ref[rows, cs]
                zog = zio_ref[rows, cs2]
                lbh = lb_ref[:, cs]
                ov = o_ref[rows, cs]
                dbv = db_ref[rows, cs]
                r = lax.rsqrt(jnp.mean(ov * ov, axis=-1, keepdims=True) + EPS)
                on = ov * r
                sgo = _sigmoid(zog)
                gate = zog * sgo
                dng_ref[...] += jnp.sum(dbv * gate * on, axis=0, keepdims=True)
                don = dbv * gate * ng_v
                dzog = dbv * (on * ng_v) * _silu_grad(zog, sgo)
                do = r * (don - on * jnp.mean(don * on, axis=-1, keepdims=True))
                sq = _sigmoid(zq)
                qf = zq * sq
                sg = _sigmoid(zf)
                sgn = _sigmoid(-zf)
                oml = 1.0 - lbh
                forget = lbh + oml * sg
                fm = jnp.maximum(forget, TINY)
                k = oml * sgn
                gc = _scan_rows(jnp.log(fm))
                gm = gc[MID : MID + 1]
                gl = gc[CHUNK - 1 : CHUNK]
                e_g = jnp.exp(gc)
                e_q = jnp.exp(gc - gm)
                e_k = jnp.exp(gm - gc)
                e_kd = jnp.exp(gl - gc)
                e_gl = jnp.exp(gl)
                qt = qf * e_q
                kt = k * e_k
                a = jnp.where(low, _dot_nt(qt, kt), 0.0)
                st = st_ref[ci, h]
                dst = dstate[h]
                dp = jnp.where(low, _dot_nt(do, vi), 0.0)
                dv = _dot_tn(a, do) + _dot_nt(k * e_kd, dst)
                dq = _dot(dp, kt) * e_q + e_g * _dot(do, st)
                dks = e_kd * _dot(vi, dst)
                dk = _dot_tn(dp, qt) * e_k + dks
                dstate[h] = _dot_tn(do, qf * e_g) + dst * e_gl
                dgc = qf * dq - k * dk
                extra = e_gl * jnp.sum(st * dst, axis=0, keepdims=True) + jnp.sum(k * dks, axis=0, keepdims=True)
                dg = _scan_rows(dgc, reverse=True) + extra
                dforget = jnp.where(forget > TINY, dg / fm, 0.0)
                both = dforget - dk
                dlb_ref[:, cs] += jnp.sum(sgn * both, axis=0, keepdims=True)
                dz_ref[rows, cs] = (dq * _silu_grad(zq, sq)).astype(MM_DTYPE)
                dz_ref[rows, cs2] = (oml * sg * sgn * both).astype(MM_DTYPE)
                dz_ref[rows, 2 * BR_DIM + h * HEAD_DIM : 2 * BR_DIM + (h + 1) * HEAD_DIM] = dv.astype(MM_DTYPE)
                dz_ref[rows, 3 * BR_DIM + h * HEAD_DIM : 3 * BR_DIM + (h + 1) * HEAD_DIM] = dzog.astype(MM_DTYPE)
            return carry

        lax.fori_loop(0, n_sub, chunk, 0, unroll=math.gcd(n_sub, HG_UNROLL))

    rev = lambda i: n_steps - 1 - i
    return _hosted_call(
        body,
        rider,
        name=name,
        grid=(n_steps,),
        in_specs=[
            pl.BlockSpec((rows_per, BR_DIM), lambda i: (rev(i), 0)),
            pl.BlockSpec((rows_per, 2 * BR_DIM), lambda i: (rev(i), 1)),
            pl.BlockSpec((rows_per, 2 * BR_DIM), lambda i: (rev(i), 2)),
            pl.BlockSpec((rows_per, BR_DIM), lambda i: (rev(i), 0)),
            pl.BlockSpec((n_sub, N_HEADS, HEAD_DIM, HEAD_DIM), lambda i: (rev(i), 0, 0, 0)),
            _resident((1, BR_DIM), lambda i: (0, 0)),
            _resident((1, HEAD_DIM), lambda i: (0, 0)),
        ],
        out_specs=[
            pl.BlockSpec((rows_per, 4 * BR_DIM), lambda i: (rev(i), 0)),
            pl.BlockSpec((1, BR_DIM), lambda i: (0, 0)),
            pl.BlockSpec((1, HEAD_DIM), lambda i: (0, 0)),
        ],
        out_shape=[SDS((s, 4 * BR_DIM), MM_DTYPE), SDS((1, BR_DIM), F32), SDS((1, HEAD_DIM), F32)],
        scratch_shapes=[pltpu.VMEM((N_HEADS, HEAD_DIM, HEAD_DIM), F32)],
        args=[db, z, z, o, states, lb, ng],
        semantics=("arbitrary",),
    )


def _mix_fwd(x, a, b, z, w_gm, w_hg, w_out, name, rider=None):
    s, d = x.shape
    tm = _divisor(s, 512, 16)
    g0 = 6 * BR_DIM // d

    def body(x_ref, a_ref, b_ref, ga_ref, gb_ref, wgm_ref, whg_ref, wout_ref, out_ref):
        y = _sigmoid(ga_ref[...]) * _dot(a_ref[...], wgm_ref[...]) + _sigmoid(gb_ref[...]) * _dot(b_ref[...], whg_ref[...])
        out_ref[...] = x_ref[...] + _dot(y, wout_ref[...])

    return _hosted_call(
        body,
        rider,
        name=name,
        grid=(s // tm,),
        in_specs=[
            pl.BlockSpec((tm, d), lambda i: (i, 0)),
            pl.BlockSpec((tm, BR_DIM), lambda i: (i, 0)),
            pl.BlockSpec((tm, BR_DIM), lambda i: (i, 0)),
            pl.BlockSpec((tm, d), lambda i: (i, g0)),
            pl.BlockSpec((tm, d), lambda i: (i, g0 + 1)),
            _resident((BR_DIM, d), lambda i: (0, 0)),
            _resident((BR_DIM, d), lambda i: (0, 0)),
            _resident((d, d), lambda i: (0, 0)),
        ],
        out_specs=[pl.BlockSpec((tm, d), lambda i: (i, 0))],
        out_shape=[SDS((s, d), F32)],
        scratch_shapes=[],
        args=[x, a, b, z, z, w_gm, w_hg, w_out],
        semantics=("arbitrary",),
    )


def _mix_bwd(dx, a, b, z, w_gm, w_hg, w_out, lng, lnb, ws, bst, name, rider=None):
    s, d = dx.shape
    tm = _divisor(s, 512, GM_BLOCK)
    g0 = 6 * BR_DIM // d

    def body(dx_ref, a_ref, b_ref, uv_ref, ga_ref, gb_ref, wgm_ref, whg_ref, wout_ref, lng_ref, lnb_ref, ws_ref, bst_ref,
             y_ref, dpa_ref, dpb_ref, db_ref, dg_ref, duv_ref, dlng_ref, dlnb_ref, dws_ref, dbst_ref):
        @pl.when(pl.program_id(0) == 0)
        def _():
            for ref in (dlng_ref, dlnb_ref, dws_ref, dbst_ref):
                ref[...] = jnp.zeros_like(ref)

        dy = _dot_nt(dx_ref[...], wout_ref[...])
        pa = _dot(a_ref[...], wgm_ref[...])
        pb = _dot(b_ref[...], whg_ref[...])
        sa = _sigmoid(ga_ref[...])
        sb = _sigmoid(gb_ref[...])
        y_ref[...] = (sa * pa + sb * pb).astype(MM_DTYPE)
        dpa = (dy * sa).astype(MM_DTYPE)
        dpb = (dy * sb).astype(MM_DTYPE)
        dpa_ref[...] = dpa
        dpb_ref[...] = dpb
        da = _dot_nt(dpa, wgm_ref[...])
        db_ref[...] = _dot_nt(dpb, whg_ref[...])
        dg_ref[:, :d] = (dy * pa * sa * (1.0 - sa)).astype(MM_DTYPE)
        dg_ref[:, d:] = (dy * pb * sb * (1.0 - sb)).astype(MM_DTYPE)
        _gmlp_bwd_tile(da, uv_ref[:, :BR_DIM], uv_ref[:, BR_DIM:], lng_ref, lnb_ref, ws_ref, bst_ref,
                       duv_ref, dlng_ref, dlnb_ref, dws_ref, dbst_ref)

    row = lambda w: pl.BlockSpec((tm, w), lambda i: (i, 0))
    fixed = lambda shape: pl.BlockSpec(shape, lambda i: tuple(0 for _ in shape))
    return _hosted_call(
        body,
        rider,
        name=name,
        grid=(s // tm,),
        in_specs=[
            row(d),
            row(BR_DIM),
            row(BR_DIM),
            row(2 * BR_DIM),
            pl.BlockSpec((tm, d), lambda i: (i, g0)),
            pl.BlockSpec((tm, d), lambda i: (i, g0 + 1)),
            _resident((BR_DIM, d), lambda i: (0, 0)),
            _resident((BR_DIM, d), lambda i: (0, 0)),
            _resident((d, d), lambda i: (0, 0)),
        ]
        + _gmlp_param_specs(),
        out_specs=[row(d), row(d), row(d), row(BR_DIM), row(2 * d), row(2 * BR_DIM), fixed((1, BR_DIM)), fixed((1, BR_DIM)),
                   fixed((N_HEADS, GM_BLOCK, GM_BLOCK)), fixed((GM_BLOCK, N_HEADS))],
        out_shape=[
            SDS((s, d), MM_DTYPE),
            SDS((s, d), MM_DTYPE),
            SDS((s, d), MM_DTYPE),
            SDS((s, BR_DIM), F32),
            SDS((s, 2 * d), MM_DTYPE),
            SDS((s, 2 * BR_DIM), MM_DTYPE),
            SDS((1, BR_DIM), F32),
            SDS((1, BR_DIM), F32),
            SDS((N_HEADS, GM_BLOCK, GM_BLOCK), F32),
            SDS((GM_BLOCK, N_HEADS), F32),
        ],
        scratch_shapes=[],
        args=[dx, a, b, z, z, z, w_gm, w_hg, w_out, lng, lnb, ws, bst],
        semantics=("arbitrary",),
    )


HALO = 8
FFN_ROWS = 256


def _shift_down(v, prev, n):
    rolled = pltpu.roll(v, n, 0)
    rid = lax.broadcasted_iota(jnp.int32, v.shape, 0)
    out = rolled
    for r in range(n):
        out = jnp.where(rid == r, prev[HALO - n + r : HALO - n + r + 1], out)
    return out


def _shift_up(v, nxt, n):
    tm = v.shape[0]
    rolled = pltpu.roll(v, tm - n, 0)
    rid = lax.broadcasted_iota(jnp.int32, v.shape, 0)
    out = rolled
    for r in range(n):
        out = jnp.where(rid == tm - n + r, nxt[r : r + 1], out)
    return out


def _conv_cols(f):
    return _divisor(f, 256, 128)


def _ffn_fwd(x, gain, w_up, cw, cb, w_down, layer, name, rider=None):
    s, d = x.shape
    f = w_down.shape[0]
    tm = _divisor(s, FFN_ROWS, 16)
    cwid = _conv_cols(f)

    def body(x_ref, g_ref, wu_ref, cw_ref, cb_ref, wd_ref, out_ref, z_ref, h_ref, conv_ref, halo):
        @pl.when(pl.program_id(0) == 0)
        def _():
            halo[...] = jnp.zeros_like(halo)

        xv = x_ref[...]
        r = lax.rsqrt(jnp.mean(xv * xv, axis=-1, keepdims=True) + EPS)
        h = ((xv * r) * g_ref[...]).astype(MM_DTYPE)
        h_ref[...] = h
        acc = xv
        starts = list(range(0, f, cwid))

        def project(c0):
            return [jnp.dot(h, wu_ref[:, base : base + cwid], preferred_element_type=F32) for base in (c0, f + c0)]

        ahead = project(starts[0])
        for n, c0 in enumerate(starts):
            halves = []
            current = ahead
            if n + 1 < len(starts):
                ahead = project(starts[n + 1])
            for base, zc in zip((c0, f + c0), current):
                cols = slice(base, base + cwid)
                z_ref[:, cols] = zc.astype(MM_DTYPE)
                prev = halo[:, cols]
                conv = cb_ref[:, cols] + cw_ref[0:1, cols] * _shift_down(zc, prev, 2)
                conv = conv + cw_ref[1:2, cols] * _shift_down(zc, prev, 1) + cw_ref[2:3, cols] * zc
                halo[:, cols] = zc[tm - HALO : tm]
                conv_ref[:, cols] = conv.astype(MM_DTYPE)
                halves.append(conv)
            gate, val = halves
            act = gate * _sigmoid(gate) * val
            acc = acc + _dot(act, wd_ref[c0 : c0 + cwid, :])
        out_ref[...] = acc

    row = lambda w: pl.BlockSpec((tm, w), lambda i: (i, 0))
    return _hosted_call(
        body,
        rider,
        name=name,
        grid=(s // tm,),
        in_specs=[
            row(d),
            _resident((1, d), lambda i: (0, 0)),
            _resident((d, 2 * f), lambda i: (0, 0)),
            _resident((None, 3, 2 * f), lambda i: (layer, 0, 0)),
            _resident((1, 2 * f), lambda i: (0, 0)),
            _resident((f, d), lambda i: (0, 0)),
        ],
        out_specs=[row(d), row(2 * f), row(d), row(2 * f)],
        out_shape=[SDS((s, d), F32), SDS((s, 2 * f), MM_DTYPE), SDS((s, d), MM_DTYPE), SDS((s, 2 * f), MM_DTYPE)],
        scratch_shapes=[pltpu.VMEM((HALO, 2 * f), F32)],
        args=[x, gain, w_up, cw, cb, w_down],
        semantics=("arbitrary",),
    )


def _ffn_bwd(dx, z2, conv, cw, w_down, w_up, x, gain, layer, name):
    s, d = dx.shape
    f = z2.shape[1] // 2
    tm = _divisor(s, 256, 16)
    cwid = _conv_cols(f)
    n_steps = s // tm

    def body(dx_ref, z_ref, conv_ref, cw_ref, wd_ref, wu_ref, x_ref, g_ref, dz_ref, act_ref, dcw_ref, dx1_ref, dg_ref, nxt):
        @pl.when(pl.program_id(0) == 0)
        def _():
            nxt[...] = jnp.zeros_like(nxt)
            dcw_ref[...] = jnp.zeros_like(dcw_ref)
            dg_ref[...] = jnp.zeros_like(dg_ref)

        dxf = dx_ref[...]
        dxv = dxf.astype(MM_DTYPE)
        starts = list(range(0, f, cwid))
        ahead = _dot_nt(dxv, wd_ref[0:cwid, :])
        dh = jnp.zeros((tm, d), F32)
        pending = []
        for n, c0 in enumerate(starts):
            dact = ahead
            if n + 1 < len(starts):
                ahead = _dot_nt(dxv, wd_ref[starts[n + 1] : starts[n + 1] + cwid, :])
            for dz_prev, cols_prev in pending:
                dh = dh + _dot_nt(dz_prev, wu_ref[:, cols_prev])
            pending = []
            gate = conv_ref[:, c0 : c0 + cwid].astype(F32)
            val = conv_ref[:, f + c0 : f + c0 + cwid].astype(F32)
            sg = _sigmoid(gate)
            silu = gate * sg
            act_ref[:, c0 : c0 + cwid] = (silu * val).astype(MM_DTYPE)
            dconvs = (dact * val * _silu_grad(gate, sg), dact * silu)
            for base, dconv in zip((c0, f + c0), dconvs):
                cols = slice(base, base + cwid)
                zc = z_ref[:, cols].astype(F32)
                nx = nxt[:, cols]
                up1 = _shift_up(dconv, nx, 1)
                up2 = _shift_up(dconv, nx, 2)
                dcw_ref[0:1, cols] += jnp.sum(up2 * zc, axis=0, keepdims=True)
                dcw_ref[1:2, cols] += jnp.sum(up1 * zc, axis=0, keepdims=True)
                dcw_ref[2:3, cols] += jnp.sum(dconv * zc, axis=0, keepdims=True)
                dcw_ref[3:4, cols] += jnp.sum(dconv, axis=0, keepdims=True)
                dz = (cw_ref[2:3, cols] * dconv + cw_ref[1:2, cols] * up1 + cw_ref[0:1, cols] * up2).astype(MM_DTYPE)
                dz_ref[:, cols] = dz
                nxt[:, cols] = dconv[0:HALO]
                pending.append((dz, cols))
        for dz_prev, cols_prev in pending:
            dh = dh + _dot_nt(dz_prev, wu_ref[:, cols_prev])
        xv = x_ref[...]
        r = lax.rsqrt(jnp.mean(xv * xv, axis=-1, keepdims=True) + EPS)
        xh = xv * r
        dg_ref[...] += jnp.sum(dh * xh, axis=0, keepdims=True)
        dxh = dh * g_ref[...]
        dx1_ref[...] = dxf + r * (dxh - xh * jnp.mean(dxh * xh, axis=-1, keepdims=True))

    rev = lambda i: n_steps - 1 - i
    row = lambda w: pl.BlockSpec((tm, w), lambda i: (rev(i), 0))
    return pl.pallas_call(
        body,
        name=name,
        grid=(n_steps,),
        in_specs=[
            row(d),
            row(2 * f),
            row(2 * f),
            _resident((None, 3, 2 * f), lambda i: (layer, 0, 0)),
            _resident((f, d), lambda i: (0, 0)),
            _resident((d, 2 * f), lambda i: (0, 0)),
            row(d),
            _resident((1, d), lambda i: (0, 0)),
        ],
        out_specs=[
            row(2 * f),
            row(f),
            pl.BlockSpec((HALO, 2 * f), lambda i: (0, 0)),
            row(d),
            pl.BlockSpec((1, d), lambda i: (0, 0)),
        ],
        out_shape=[SDS((s, 2 * f), MM_DTYPE), SDS((s, f), MM_DTYPE), SDS((HALO, 2 * f), F32), SDS((s, d), F32), SDS((1, d), F32)],
        scratch_shapes=[pltpu.VMEM((HALO, 2 * f), F32)],
        compiler_params=_params("arbitrary"),
    )(dx, z2, conv, cw, w_down, w_up, x, gain)


def _loss_head(x, gain, target):
    s, d = x.shape
    tm = _divisor(s, 256, 8)

    def body(x_ref, g_ref, t_ref, loss_ref, dx_ref, dg_ref):
        @pl.when(pl.program_id(0) == 0)
        def _():
            loss_ref[...] = jnp.zeros_like(loss_ref)
            dg_ref[...] = jnp.zeros_like(dg_ref)

        xv = x_ref[...]
        gv = g_ref[...]
        r = lax.rsqrt(jnp.mean(xv * xv, axis=-1, keepdims=True) + EPS)
        xh = xv * r
        err = xh * gv - t_ref[...]
        loss_ref[...] += 0.5 * jnp.sum(jnp.mean(err * err, axis=-1, keepdims=True))
        dout = err * (1.0 / d)
        dg_ref[...] += jnp.sum(dout * xh, axis=0, keepdims=True)
        dxh = dout * gv
        dx_ref[...] = r * (dxh - xh * jnp.mean(dxh * xh, axis=-1, keepdims=True))

    return pl.pallas_call(
        body,
        name="loss_head",
        grid=(s // tm,),
        in_specs=[
            pl.BlockSpec((tm, d), lambda i: (i, 0)),
            _resident((1, d), lambda i: (0, 0)),
            pl.BlockSpec((tm, d), lambda i: (i, 0)),
        ],
        out_specs=[
            pl.BlockSpec((8, 128), lambda i: (0, 0)),
            pl.BlockSpec((tm, d), lambda i: (i, 0)),
            pl.BlockSpec((1, d), lambda i: (0, 0)),
        ],
        out_shape=[SDS((8, 128), F32), SDS((s, d), F32), SDS((1, d), F32)],
        compiler_params=_params("arbitrary"),
    )(x, gain, target)


def _adamw(w, g, m, v, name):
    shape = w.shape
    cols = shape[-1]
    rows = max(1, math.prod(shape[:-1]))
    tr = _divisor(rows, max(8, TILE_BYTES // (4 * cols)), 8)
    flat = [t.reshape(rows, cols) for t in (w, g, m, v)]

    def body(w_ref, g_ref, m_ref, v_ref, d_ref, nm_ref, nv_ref):
        gv = g_ref[...]
        m2 = ADAM_B1 * m_ref[...] + (1.0 - ADAM_B1) * gv
        v2 = ADAM_B2 * v_ref[...] + (1.0 - ADAM_B2) * (gv * gv)
        m_hat = m2 / (1.0 - ADAM_B1**ADAM_STEP)
        v_hat = v2 / (1.0 - ADAM_B2**ADAM_STEP)
        d_ref[...] = -ADAM_LR * (m_hat / (jnp.sqrt(v_hat) + ADAM_EPS) + ADAM_WD * w_ref[...])
        nm_ref[...] = m2
        nv_ref[...] = v2

    spec = pl.BlockSpec((tr, cols), lambda i: (i, 0))
    outs = pl.pallas_call(
        body,
        name=name,
        grid=(rows // tr,),
        in_specs=[spec] * 4,
        out_specs=[spec] * 3,
        out_shape=[SDS((rows, cols), F32)] * 3,
        compiler_params=_params("parallel"),
    )(*flat)
    return tuple(t.reshape(shape) for t in outs)


def _position():
    return lax.axis_index("x"), lax.axis_index("y"), lax.axis_index("c")


def _other_chips(x, y):
    return [(1 - x, y), (x, 1 - y), (1 - x, 1 - y)]


def _remote(src, dst, send_sem, recv_sem, device):
    return pltpu.make_async_remote_copy(
        src_ref=src, dst_ref=dst, send_sem=send_sem, recv_sem=recv_sem, device_id=device, device_id_type=MESH
    )


def _sub(ref, lead, shape, axis, chip=None, half=None):
    cut = [pl.ds(0, shape[0]), pl.ds(0, shape[1])]
    if chip is not None:
        size = shape[axis] // N_CHIPS
        cut[axis] = pl.ds(chip * size, size)
    if half is not None:
        size = shape[1 - axis] // 2
        cut[1 - axis] = pl.ds(half * size, size)
    return ref.at[(*lead, *cut)]


def _scaled(shape, axis, num, den=1):
    return tuple(d * num // den if i == axis else d for i, d in enumerate(shape))


def _gather_rider(shards, axes, layer, conv_w=None):
    n = len(shards)
    shard2d = [t.shape[1:] for t in shards]
    full2d = [_scaled(s2, ax, N_CHIPS) for s2, ax in zip(shard2d, axes)]
    out_shapes = [SDS(f2, t.dtype) for f2, t in zip(full2d, shards)]
    inputs = list(shards)
    own = 6
    scratch = [pltpu.SemaphoreType.DMA((n, 7)), pltpu.SemaphoreType.DMA((n, 7))]
    if conv_w is not None:
        cshape = conv_w.shape
        inputs.append(conv_w)
        out_shapes.append(SDS(_scaled(cshape, 2, N_CHIPS), conv_w.dtype))
        scratch += [pltpu.SemaphoreType.DMA((4,)), pltpu.SemaphoreType.DMA((4,))]

    def conv_slab(ref, chip):
        return ref.at[:, :, pl.ds((2 * chip[0] + chip[1]) * cshape[2], cshape[2])]

    def own_copy(ins, outs, send_sems, recv_sems, k):
        x, y, c = _position()
        slab = _sub(outs[k], (), full2d[k], axes[k], chip=2 * x + y)
        return _remote(ins[k].at[layer], slab, send_sems.at[k, own], recv_sems.at[k, own], (x, y, 1 - c))

    def start(ins, outs, scr):
        send_sems, recv_sems = scr[:2]
        x, y, c = _position()
        me = 2 * x + y
        for k in range(n):
            own_copy(ins, outs, send_sems, recv_sems, k).start()
            for j, chip in enumerate(_other_chips(x, y)):
                _remote(
                    _sub(ins[k], (layer,), shard2d[k], axes[k], half=c),
                    _sub(outs[k], (), full2d[k], axes[k], chip=me, half=c),
                    send_sems.at[k, j], recv_sems.at[k, j], (*chip, c),
                ).start()
        if conv_w is not None:
            csend, crecv = scr[2:]
            _remote(ins[n], conv_slab(outs[n], (x, y)), csend.at[3], crecv.at[3], (x, y, 1 - c)).start()
            for j, chip in enumerate(_other_chips(x, y)):
                _remote(ins[n], conv_slab(outs[n], (x, y)), csend.at[j], crecv.at[j], (*chip, c)).start()

    def finish(ins, outs, scr):
        send_sems, recv_sems = scr[:2]
        x, y, c = _position()
        me = 2 * x + y
        sibling = (x, y, 1 - c)
        chips = _other_chips(x, y)
        forwards = []
        for k in range(n):
            src = _sub(ins[k], (layer,), shard2d[k], axes[k], half=c)
            for j, chip in enumerate(chips):
                landed = _sub(outs[k], (), full2d[k], axes[k], chip=2 * chip[0] + chip[1], half=c)
                _remote(src, landed, send_sems.at[k, j], recv_sems.at[k, j], (*chip, c)).wait_recv()
                fwd = _remote(landed, landed, send_sems.at[k, 3 + j], recv_sems.at[k, 3 + j], sibling)
                fwd.start()
                forwards.append(fwd)
        for k in range(n):
            src = _sub(ins[k], (layer,), shard2d[k], axes[k], half=c)
            for j, chip in enumerate(chips):
                theirs = _sub(outs[k], (), full2d[k], axes[k], chip=2 * chip[0] + chip[1], half=1 - c)
                _remote(src, theirs, send_sems.at[k, 3 + j], recv_sems.at[k, 3 + j], sibling).wait_recv()
        for fwd in forwards:
            fwd.wait_send()
        for k in range(n):
            src = _sub(ins[k], (layer,), shard2d[k], axes[k], half=c)
            mine = _sub(outs[k], (), full2d[k], axes[k], chip=me, half=c)
            for j, chip in enumerate(chips):
                _remote(src, mine, send_sems.at[k, j], recv_sems.at[k, j], (*chip, c)).wait_send()
            own_copy(ins, outs, send_sems, recv_sems, k).wait()
        if conv_w is not None:
            csend, crecv = scr[2:]
            for j, chip in enumerate(chips):
                cp = _remote(ins[n], conv_slab(outs[n], chip), csend.at[j], crecv.at[j], (*chip, c))
                cp.wait_recv()
                cp.wait_send()
            _remote(ins[n], conv_slab(outs[n], (x, y)), csend.at[3], crecv.at[3], sibling).wait()

    return _Rider(inputs, out_shapes, scratch, start, finish)


def _pair_rider(stacks, axes, layer):
    n = len(stacks)
    shapes = [t.shape[1:] for t in stacks]
    out_shapes = [SDS(_scaled(s2, 1 - ax, 1, 2), F32) for s2, ax in zip(shapes, axes)]
    scratch = [pltpu.SemaphoreType.DMA((n,)), pltpu.SemaphoreType.DMA((n,))]

    def copies(ins, outs, scr):
        x, y, c = _position()
        return [
            _remote(_sub(ins[k], (layer,), shapes[k], axes[k], half=1 - c), outs[k], scr[0].at[k], scr[1].at[k], (x, y, 1 - c))
            for k in range(n)
        ]

    def start(ins, outs, scr):
        for cp in copies(ins, outs, scr):
            cp.start()

    def finish(ins, outs, scr):
        for cp in copies(ins, outs, scr):
            cp.wait()

    return _Rider(stacks, out_shapes, scratch, start, finish)


def _chip_rider(pairs, axes):
    n = len(pairs)
    shapes = [t.shape for t in pairs]
    out_shapes = [SDS((3,) + _scaled(s2, ax, 1, N_CHIPS), t.dtype) for s2, ax, t in zip(shapes, axes, pairs)]
    scratch = [pltpu.SemaphoreType.DMA((n, 3)), pltpu.SemaphoreType.DMA((n, 3))]

    def copies(ins, outs, scr):
        x, y, c = _position()
        return [
            _remote(
                _sub(ins[k], (), shapes[k], axes[k], chip=2 * chip[0] + chip[1]), outs[k].at[j],
                scr[0].at[k, j], scr[1].at[k, j], (*chip, c),
            )
            for k in range(n)
            for j, chip in enumerate(_other_chips(x, y))
        ]

    def start(ins, outs, scr):
        for cp in copies(ins, outs, scr):
            cp.start()

    def finish(ins, outs, scr):
        for cp in copies(ins, outs, scr):
            cp.wait()

    return _Rider(pairs, out_shapes, scratch, start, finish)


def _complete_rider(stacks, axes, layers):
    n = len(stacks)
    shapes = [t.shape[1:] for t in stacks]
    scratch = [pltpu.SemaphoreType.DMA((n,)), pltpu.SemaphoreType.DMA((n,))]

    def start(ins, outs, scr):
        x, y, c = _position()
        for k in range(n):
            mine = _sub(outs[k], (layers[k],), shapes[k], axes[k], half=c)
            _remote(mine, mine, scr[0].at[k], scr[1].at[k], (x, y, 1 - c)).start()

    def finish(ins, outs, scr):
        x, y, c = _position()
        for k in range(n):
            mine = _sub(outs[k], (layers[k],), shapes[k], axes[k], half=c)
            theirs = _sub(outs[k], (layers[k],), shapes[k], axes[k], half=1 - c)
            _remote(mine, theirs, scr[0].at[k], scr[1].at[k], (x, y, 1 - c)).wait_recv()
            _remote(mine, mine, scr[0].at[k], scr[1].at[k], (x, y, 1 - c)).wait_send()

    return _Rider(stacks, [SDS(t.shape, t.dtype) for t in stacks], scratch, start, finish, {k: k for k in range(n)})


def _small_rider(buf):
    rows, cols = buf.shape
    flips = [(fx, fy, fc) for fx in (0, 1) for fy in (0, 1) for fc in (0, 1)][1:]

    def peers():
        x, y, c = _position()
        return [(x + f[0] - 2 * x * f[0], y + f[1] - 2 * y * f[1], c + f[2] - 2 * c * f[2]) for f in flips]

    def start(ins, outs, scr):
        x, y, c = _position()
        mine = outs[0].at[4 * x + 2 * y + c]
        pltpu.make_async_copy(ins[0], mine, scr[2]).start()
        for j, peer in enumerate(peers()):
            _remote(ins[0], mine, scr[0].at[j], scr[1].at[j], peer).start()

    def finish(ins, outs, scr):
        x, y, c = _position()
        mine = outs[0].at[4 * x + 2 * y + c]
        for j, peer in enumerate(peers()):
            _remote(ins[0], outs[0].at[4 * peer[0] + 2 * peer[1] + peer[2]], scr[0].at[j], scr[1].at[j], peer).wait_recv()
        for j, peer in enumerate(peers()):
            _remote(ins[0], mine, scr[0].at[j], scr[1].at[j], peer).wait_send()
        pltpu.make_async_copy(ins[0], mine, scr[2]).wait()

    scratch = [pltpu.SemaphoreType.DMA((7,)), pltpu.SemaphoreType.DMA((7,)), pltpu.SemaphoreType.DMA(())]
    return _Rider([buf], [SDS((8, rows, cols), buf.dtype)], scratch, start, finish)


def _my_core():
    return lax.axis_index("c")


def _my_chip():
    return 2 * lax.axis_index("x") + lax.axis_index("y")


def _pair_sum(stack, layer, recv, axis, name):
    hk, hn = recv.shape
    tk = _divisor(hk, max(16, TILE_BYTES // (4 * hn)), 16)
    per = hk // tk

    def body(g_ref, r_ref, out_ref):
        out_ref[...] = (g_ref[...] + r_ref[...]).astype(out_ref.dtype)

    if axis == 1:
        mine = pl.BlockSpec((None, tk, hn), lambda i: (layer, _my_core() * per + i, 0))
    else:
        mine = pl.BlockSpec((None, tk, hn), lambda i: (layer, i, _my_core()))
    return pl.pallas_call(
        body,
        name=name,
        grid=(per,),
        in_specs=[mine, pl.BlockSpec((tk, hn), lambda i: (i, 0))],
        out_specs=pl.BlockSpec((tk, hn), lambda i: (i, 0)),
        out_shape=SDS((hk, hn), BF16),
        compiler_params=_params("parallel"),
    )(stack, recv)


def _chip_sum(pair, others, axis, stack, layer, name):
    _, sk, sn = others.shape
    tk = _divisor(sk, max(16, TILE_BYTES // (4 * sn)), 16)
    per = sk // tk

    def body(p_ref, o0_ref, o1_ref, o2_ref, *rest):
        out_ref = rest[-1]
        acc = p_ref[...].astype(F32) + o0_ref[...].astype(F32)
        out_ref[...] = (acc + o1_ref[...].astype(F32)) + o2_ref[...].astype(F32)

    if axis == 1:
        own = pl.BlockSpec((tk, sn), lambda i: (i, _my_chip()))
        out = pl.BlockSpec((None, tk, sn), lambda i: (layer, _my_core() * per + i, 0))
        shard = (2 * sk, sn)
    else:
        own = pl.BlockSpec((tk, sn), lambda i: (_my_chip() * per + i, 0))
        out = pl.BlockSpec((None, tk, sn), lambda i: (layer, i, _my_core()))
        shard = (sk, 2 * sn)
    other = lambda j: pl.BlockSpec((None, tk, sn), lambda i: (j, i, 0))
    in_specs = [own, other(0), other(1), other(2)]
    args = [pair, others, others, others]
    aliases = {}
    if stack is not None:
        in_specs.append(ANY)
        args.append(stack)
        aliases = {4: 0}
    return pl.pallas_call(
        body,
        name=name,
        grid=(per,),
        in_specs=in_specs,
        out_specs=out,
        out_shape=SDS((DEPTH,) + shard, F32),
        input_output_aliases=aliases,
        compiler_params=_params("parallel"),
    )(*args)


def _sum_devices(gathered, name):
    _, rows, cols = gathered.shape

    def body(g_ref, out_ref):
        acc = g_ref[0]
        for dev in range(1, 8):
            acc = acc + g_ref[dev]
        out_ref[...] = acc

    return pl.pallas_call(body, name=name, out_shape=SDS((rows, cols), F32))(gathered)


PACK_TILE = 8 * 128


def _pack(arrays):
    parts = []
    for t in arrays:
        flat = t.reshape(-1)
        pad = (-flat.shape[0]) % PACK_TILE
        if pad:
            flat = jnp.concatenate([flat, jnp.zeros((pad,), flat.dtype)])
        parts.append(flat.reshape(-1, 128))
    return jnp.concatenate(parts, axis=0)


def _unpack(buf, shapes):
    out, row = [], 0
    for shp in shapes:
        size = math.prod(shp)
        rows = -(-size // PACK_TILE) * 8
        out.append(buf[row : row + rows].reshape(-1)[:size].reshape(shp))
        row += rows
    return out


BIG = ("w_in", "w_br_gm", "w_br_hg", "w_out", "w_up", "w_down")
BIG_AXIS = (1, 1, 1, 0, 1, 0)
LAYER_SMALL = ("mix_norm", "gm_ln_g", "gm_ln_b", "gm_ws", "gm_bs", "lb", "hg_norm_g", "ffn_norm", "conv_w", "conv_b")
WEIGHTS = ("mix_norm", "w_in", "gm_ln_g", "gm_ln_b", "gm_ws", "gm_bs", "hg_lb_logits", "hg_norm_g", "w_br_gm", "w_br_hg", "w_out",
           "ffn_norm", "w_up", "conv_w", "conv_b", "w_down", "final_norm")


def kernel(x, mix_norm, w_in, gm_ln_g, gm_ln_b, gm_ws, gm_bs, hg_lb_logits, hg_norm_g, w_br_gm, w_br_hg, w_out, ffn_norm, w_up, conv_w, conv_b, w_down, final_norm, loss_target, m_mix_norm, m_w_in, m_gm_ln_g, m_gm_ln_b, m_gm_ws, m_gm_bs, m_hg_lb_logits, m_hg_norm_g, m_w_br_gm, m_w_br_hg, m_w_out, m_ffn_norm, m_w_up, m_conv_w, m_conv_b, m_w_down, m_final_norm, v_mix_norm, v_w_in, v_gm_ln_g, v_gm_ln_b, v_gm_ws, v_gm_bs, v_hg_lb_logits, v_hg_norm_g, v_w_br_gm, v_w_br_hg, v_w_out, v_ffn_norm, v_w_up, v_conv_w, v_conv_b, v_w_down, v_final_norm):
    w = dict(mix_norm=mix_norm, w_in=w_in, gm_ln_g=gm_ln_g, gm_ln_b=gm_ln_b, gm_ws=gm_ws, gm_bs=gm_bs, hg_lb_logits=hg_lb_logits,
             hg_norm_g=hg_norm_g, w_br_gm=w_br_gm, w_br_hg=w_br_hg, w_out=w_out, ffn_norm=ffn_norm, w_up=w_up, conv_w=conv_w,
             conv_b=conv_b, w_down=w_down, final_norm=final_norm)
    m = dict(mix_norm=m_mix_norm, w_in=m_w_in, gm_ln_g=m_gm_ln_g, gm_ln_b=m_gm_ln_b, gm_ws=m_gm_ws, gm_bs=m_gm_bs,
             hg_lb_logits=m_hg_lb_logits, hg_norm_g=m_hg_norm_g, w_br_gm=m_w_br_gm, w_br_hg=m_w_br_hg, w_out=m_w_out,
             ffn_norm=m_ffn_norm, w_up=m_w_up, conv_w=m_conv_w, conv_b=m_conv_b, w_down=m_w_down, final_norm=m_final_norm)
    v = dict(mix_norm=v_mix_norm, w_in=v_w_in, gm_ln_g=v_gm_ln_g, gm_ln_b=v_gm_ln_b, gm_ws=v_gm_ws, gm_bs=v_gm_bs,
             hg_lb_logits=v_hg_lb_logits, hg_norm_g=v_hg_norm_g, w_br_gm=v_w_br_gm, w_br_hg=v_w_br_hg, w_out=v_w_out,
             ffn_norm=v_ffn_norm, w_up=v_w_up, conv_w=v_conv_w, conv_b=v_conv_b, w_down=v_w_down, final_norm=v_final_norm)

    axes = list(BIG_AXIS)
    d = x.shape[2]

    shards = [w[k].astype(MM_DTYPE) for k in BIG]
    w_in_0, conv_full = _run_rider(_gather_rider(shards[:1], axes[:1], 0, conv_w=conv_w), "gather_weights_0")
    full = [dict(w_in=w_in_0)]
    lb = _lower_bounds(hg_lb_logits)
    xs = x[0]
    saved = []
    mixer = 4
    for l in range(DEPTH):
        fw = full[l]
        more = l + 1 < DEPTH
        bst = gm_bs[l].T
        rider = _gather_rider(shards[mixer:], axes[mixer:], 0) if l == 0 else None
        (z, h, a), got = _in_proj(
            xs, mix_norm[l : l + 1], fw["w_in"], gm_ln_g[l : l + 1], gm_ln_b[l : l + 1], gm_ws[l], bst, f"in_proj_{l}", rider
        )
        if l == 0:
            fw.update(zip(BIG[mixer:], got))
        rider = _gather_rider(shards[1:mixer], axes[1:mixer], 0) if l == 0 else None
        (o, b, states), got = _hgrn_fwd(z, lb[l : l + 1], hg_norm_g[l : l + 1], f"hgrn_fwd_{l}", rider)
        if l == 0:
            fw.update(zip(BIG[1:mixer], got))
        (x1,), _ = _mix_fwd(xs, a, b, z, fw["w_br_gm"], fw["w_br_hg"], fw["w_out"], f"mix_fwd_{l}")
        rider = _gather_rider(shards, axes, l + 1) if more else None
        (x2, z2, h2, conv), got = _ffn_fwd(
            x1, ffn_norm[l : l + 1], fw["w_up"], conv_full, conv_b[l : l + 1], fw["w_down"], l, f"ffn_fwd_{l}", rider
        )
        if more:
            full.append(dict(zip(BIG, got)))
        saved.append((xs, h, z, a, b, o, states, x1, h2, z2, conv, bst))
        xs = x2

    loss_tile, dx, d_final = _loss_head(xs, final_norm.reshape(1, d), loss_target[0])
    loss = lax.psum(loss_tile[0, 0], ("x", "y", "c"))

    big = {k: None for k in BIG}
    grads = [None] * len(BIG)
    per_layer = [None] * DEPTH
    mix_ids, ffn_ids = list(range(mixer)), list(range(mixer, len(BIG)))
    mix_axes, ffn_axes = axes[:mixer], axes[mixer:]
    mix_pairs = None
    mix_summed = None

    def pair_sums(ids, received, layer):
        return [_pair_sum(big[BIG[k]], layer, r, axes[k], f"pair_sum_{BIG[k]}_{layer}") for k, r in zip(ids, received)]

    def chip_sums(ids, pairs, others, layer):
        for k, p, ot in zip(ids, pairs, others):
            grads[k] = _chip_sum(p, ot, axes[k], grads[k], layer, f"chip_sum_{BIG[k]}_{layer}")

    for l in reversed(range(DEPTH)):
        x0, h, z, a, b, o, states, x1, h2, z2, conv, bst = saved[l]
        fw = full[l]
        dz2, act, dconv, dx1, d_ffn = _ffn_bwd(
            dx, z2, conv, conv_full, fw["w_down"], fw["w_up"], x1, ffn_norm[l : l + 1], l, f"ffn_bwd_{l}"
        )
        big["w_down"] = _matmul_tn(act, dx, big["w_down"], l, f"dw_down_{l}")
        big["w_up"] = _matmul_tn(h2, dz2, big["w_up"], l, f"dw_up_{l}")
        rider = _pair_rider([big[BIG[k]] for k in ffn_ids], ffn_axes, l)
        if mix_pairs is not None:
            rider = _join(rider, _chip_rider(mix_pairs, mix_axes))
        (y, dpa, dpb, db, dgates, dz_gm, d_lng, d_lnb, d_ws, d_bst), got = _mix_bwd(
            dx1, a, b, z, fw["w_br_gm"], fw["w_br_hg"], fw["w_out"], gm_ln_g[l : l + 1], gm_ln_b[l : l + 1], gm_ws[l], bst,
            f"mix_bwd_{l}", rider,
        )
        if mix_pairs is not None:
            chip_sums(mix_ids, mix_pairs, got[len(ffn_ids) :], l + 1)
            mix_summed = l + 1
        ffn_pairs = pair_sums(ffn_ids, got[: len(ffn_ids)], l)
        big["w_out"] = _matmul_tn(y, dx1, big["w_out"], l, f"dw_out_{l}")
        big["w_br_gm"] = _matmul_tn(a, dpa, big["w_br_gm"], l, f"dw_br_gm_{l}")
        big["w_br_hg"] = _matmul_tn(b, dpb, big["w_br_hg"], l, f"dw_br_hg_{l}")
        (dz_hg, d_lb, d_ng), others = _hgrn_bwd(
            db, z, o, states, lb[l : l + 1], hg_norm_g[l : l + 1], f"hgrn_bwd_{l}", _chip_rider(ffn_pairs, ffn_axes)
        )
        chip_sums(ffn_ids, ffn_pairs, others, l)
        n_in = fw["w_in"].shape[1]
        big["w_in"] = _matmul_tn(h, dz_gm, big["w_in"], l, f"dw_in_gm_{l}", n_total=n_in, col_off=0)
        big["w_in"] = _matmul_tn(h, dz_hg, big["w_in"], l, f"dw_in_hg_{l}", n_total=n_in, col_off=2 * BR_DIM)
        big["w_in"] = _matmul_tn(h, dgates, big["w_in"], l, f"dw_in_gate_{l}", n_total=n_in, col_off=6 * BR_DIM)
        done_ids = ffn_ids + (mix_ids if mix_summed is not None else [])
        done_layers = [l] * len(ffn_ids) + ([mix_summed] * len(mix_ids) if mix_summed is not None else [])
        rider = _join(
            _pair_rider([big[BIG[k]] for k in mix_ids], mix_axes, l),
            _complete_rider([grads[k] for k in done_ids], [axes[k] for k in done_ids], done_layers),
        )
        if l == 0:
            upper = [_pack([per_layer[j][k] for k in LAYER_SMALL]) for j in range(1, DEPTH)] + [_pack([d_final[0]])]
            rider = _join(rider, _small_rider(jnp.concatenate(upper, axis=0)))
        (dx, d_mix), got = _in_proj_bwd([dz_gm, dz_hg, dgates], fw["w_in"], x0, mix_norm[l : l + 1], dx1, f"in_proj_bwd_{l}", rider)
        for k, g in zip(done_ids, got[len(mix_ids) : len(mix_ids) + len(done_ids)]):
            grads[k] = g
        gathered_upper = got[len(mix_ids) + len(done_ids) :]
        mix_pairs = pair_sums(mix_ids, got[: len(mix_ids)], l)
        per_layer[l] = dict(
            mix_norm=d_mix[0], gm_ln_g=d_lng[0], gm_ln_b=d_lnb[0], gm_ws=d_ws, gm_bs=d_bst.T, lb=d_lb[0], hg_norm_g=d_ng[0],
            ffn_norm=d_ffn[0], conv_w=dconv[0:3], conv_b=dconv[3],
        )

    got = _run_rider(
        _join(_chip_rider(mix_pairs, mix_axes), _small_rider(_pack([per_layer[0][k] for k in LAYER_SMALL]))), "grad_exchange_0"
    )
    chip_sums(mix_ids, mix_pairs, got[: len(mix_ids)], 0)
    done = _run_rider(_complete_rider([grads[k] for k in mix_ids], mix_axes, [0] * len(mix_ids)), "grad_complete_0")
    for k, g in zip(mix_ids, done):
        grads[k] = g
    grad = dict(zip(BIG, grads))

    layer_shapes = [per_layer[0][k].shape for k in LAYER_SMALL]
    summed_upper = _unpack(_sum_devices(gathered_upper[0], "sum_small_upper"), layer_shapes * (DEPTH - 1) + [d_final[0].shape])
    summed_0 = _unpack(_sum_devices(got[len(mix_ids)], "sum_small_0"), layer_shapes)
    by_layer = [summed_0] + [summed_upper[j * len(LAYER_SMALL) : (j + 1) * len(LAYER_SMALL)] for j in range(DEPTH - 1)]
    small = {k: jnp.stack([by_layer[j][i] for j in range(DEPTH)]) for i, k in enumerate(LAYER_SMALL)}
    for k in LAYER_SMALL:
        if k != "lb":
            grad[k] = small[k]
    grad["hg_lb_logits"] = _lower_bounds_bwd(hg_lb_logits, small["lb"])
    grad["final_norm"] = summed_upper[-1]
    grad["conv_w"] = lax.dynamic_slice_in_dim(grad["conv_w"], _my_chip() * conv_w.shape[2], conv_w.shape[2], axis=2)

    delta, new_m, new_v = {}, {}, {}
    for k in WEIGHTS:
        delta[k], new_m[k], new_v[k] = _adamw(w[k], grad[k], m[k], v[k], f"adamw_{k}")

    return (loss, dx[None], *[grad[k] for k in WEIGHTS], *[delta[k] for k in WEIGHTS], *[new_m[k] for k in WEIGHTS],
            *[new_v[k] for k in WEIGHTS])
```

```python
import math

import jax
import jax.numpy as jnp
from jax import lax
from jax.experimental import pallas as pl
from jax.experimental.pallas import tpu as pltpu

F32 = jnp.float32
BF16 = jnp.bfloat16
MM_DTYPE = BF16

N_HEADS = 4
HEAD_DIM = 128
BR_DIM = N_HEADS * HEAD_DIM
CHUNK = 64
GM_BLOCK = 128
DEPTH = 4
N_CHIPS = 4
EPS = 1e-6
TINY = 1e-30
LB_MAX = 0.999
ADAM_LR = 0.001
ADAM_B1 = 0.9
ADAM_B2 = 0.999
ADAM_EPS = 1e-08
ADAM_WD = 0.01
ADAM_STEP = 10
INV_SQRT2 = 1.0 / math.sqrt(2.0)
INV_SQRT_2PI = 1.0 / math.sqrt(2.0 * math.pi)

VMEM_LIMIT_BYTES = 56 * 1024 * 1024
TILE_BYTES = 2 * 1024 * 1024
ACC_BYTES = 6 * 1024 * 1024
MESH = pl.DeviceIdType.MESH
SDS = jax.ShapeDtypeStruct
ANY = pl.BlockSpec(memory_space=pl.ANY)


def _params(*sem):
    return pltpu.CompilerParams(dimension_semantics=sem or None, vmem_limit_bytes=VMEM_LIMIT_BYTES)


def _divisor(n, limit, mult):
    best = None
    for d in range(mult, min(n, limit) + 1, mult):
        if n % d == 0:
            best = d
    return best if best is not None else n


def _dot(a, b):
    return jnp.dot(a.astype(MM_DTYPE), b.astype(MM_DTYPE), preferred_element_type=F32)


def _dot_nt(a, b):
    return lax.dot_general(a.astype(MM_DTYPE), b.astype(MM_DTYPE), (((1,), (1,)), ((), ())), preferred_element_type=F32)


def _dot_tn(a, b):
    return lax.dot_general(a.astype(MM_DTYPE), b.astype(MM_DTYPE), (((0,), (0,)), ((), ())), preferred_element_type=F32)


def _sigmoid(x):
    return 1.0 / (1.0 + jnp.exp(-x))


def _gelu(x):
    return 0.5 * x * (1.0 + lax.erf(x * INV_SQRT2))


def _gelu_grad(x):
    return 0.5 * (1.0 + lax.erf(x * INV_SQRT2)) + x * jnp.exp(-0.5 * x * x) * INV_SQRT_2PI


def _silu_grad(x, s):
    return s * (1.0 + x * (1.0 - s))


def _resident(shape, index_map):
    return pl.BlockSpec(shape, index_map, pipeline_mode=pl.Buffered(1))


class _Rider:
    def __init__(self, inputs, out_shapes, scratch, start, finish, aliases=None):
        self.inputs = list(inputs)
        self.out_shapes = list(out_shapes)
        self.scratch = list(scratch)
        self.start = start
        self.finish = finish
        self.aliases = dict(aliases or {})


def _join(a, b):
    if a is None or b is None:
        return a if b is None else b
    na, oa, sa = len(a.inputs), len(a.out_shapes), len(a.scratch)

    def start(i, o, s):
        a.start(i[:na], o[:oa], s[:sa])
        b.start(i[na:], o[oa:], s[sa:])

    def finish(i, o, s):
        a.finish(i[:na], o[:oa], s[:sa])
        b.finish(i[na:], o[oa:], s[sa:])

    aliases = dict(a.aliases)
    aliases.update({na + k: oa + v for k, v in b.aliases.items()})
    return _Rider(a.inputs + b.inputs, a.out_shapes + b.out_shapes, a.scratch + b.scratch, start, finish, aliases)


def _hosted_call(body, rider, *, name, grid, in_specs, out_specs, out_shape, scratch_shapes, args, semantics):
    n_in, n_out, n_scr = len(in_specs), len(out_specs), len(scratch_shapes)
    if rider is None:
        outs = pl.pallas_call(
            body, name=name, grid=grid, in_specs=in_specs, out_specs=out_specs, out_shape=out_shape,
            scratch_shapes=scratch_shapes, compiler_params=_params(*semantics),
        )(*args)
        return list(outs), []
    ri, ro = len(rider.inputs), len(rider.out_shapes)

    def wrapped(*refs):
        p = 0
        hin = refs[p : p + n_in]
        p += n_in
        rin = refs[p : p + ri]
        p += ri
        hout = refs[p : p + n_out]
        p += n_out
        rout = refs[p : p + ro]
        p += ro
        hscr = refs[p : p + n_scr]
        rscr = refs[p + n_scr :]

        @pl.when(pl.program_id(0) == 0)
        def _():
            rider.start(rin, rout, rscr)

        body(*hin, *hout, *hscr)

        @pl.when(pl.program_id(0) == grid[0] - 1)
        def _():
            rider.finish(rin, rout, rscr)

    outs = pl.pallas_call(
        wrapped,
        name=name,
        grid=grid,
        in_specs=list(in_specs) + [ANY] * ri,
        out_specs=list(out_specs) + [ANY] * ro,
        out_shape=list(out_shape) + rider.out_shapes,
        scratch_shapes=list(scratch_shapes) + rider.scratch,
        input_output_aliases={n_in + k: n_out + v for k, v in rider.aliases.items()},
        compiler_params=_params(*semantics),
    )(*args, *rider.inputs)
    return list(outs[:n_out]), list(outs[n_out:])


def _run_rider(rider, name):
    ri, ro = len(rider.inputs), len(rider.out_shapes)

    def body(*refs):
        rin, rout, rscr = refs[:ri], refs[ri : ri + ro], refs[ri + ro :]
        rider.start(rin, rout, rscr)
        rider.finish(rin, rout, rscr)

    return list(
        pl.pallas_call(
            body,
            name=name,
            in_specs=[ANY] * ri,
            out_specs=[ANY] * ro,
            out_shape=rider.out_shapes,
            scratch_shapes=rider.scratch,
            input_output_aliases=rider.aliases,
        )(*rider.inputs)
    )


def _lower_bounds(logits):
    def body(lg_ref, lb_ref):
        lg = lg_ref[...]
        mx = jnp.max(lg, axis=0, keepdims=True)
        e = jnp.exp(lg - mx)
        p = e / jnp.sum(e, axis=0, keepdims=True)
        run = p[0:1]
        rows = [run - p[0:1]]
        for l in range(1, DEPTH):
            run = run + p[l : l + 1]
            rows.append(run - p[0:1])
        raw = jnp.concatenate(rows, axis=0)
        lb_ref[...] = jnp.minimum(jnp.maximum(raw, 0.0), LB_MAX)

    return pl.pallas_call(body, name="lower_bounds", out_shape=SDS(logits.shape, F32))(logits)


def _lower_bounds_bwd(logits, dlb):
    def body(lg_ref, dlb_ref, dlg_ref):
        lg = lg_ref[...]
        mx = jnp.max(lg, axis=0, keepdims=True)
        e = jnp.exp(lg - mx)
        p = e / jnp.sum(e, axis=0, keepdims=True)
        run = p[0:1]
        rows = [run - p[0:1]]
        for l in range(1, DEPTH):
            run = run + p[l : l + 1]
            rows.append(run - p[0:1])
        raw = jnp.concatenate(rows, axis=0)
        lo = jnp.where(raw > 0.0, 1.0, jnp.where(raw == 0.0, 0.5, 0.0))
        clipped = jnp.maximum(raw, 0.0)
        hi = jnp.where(clipped < LB_MAX, 1.0, jnp.where(clipped == LB_MAX, 0.5, 0.0))
        draw = dlb_ref[...] * lo * hi
        total = jnp.sum(draw, axis=0, keepdims=True)
        dps = []
        tail = total
        for j in range(DEPTH):
            dps.append(tail - total if j == 0 else tail)
            tail = tail - draw[j : j + 1]
        dp = jnp.concatenate(dps, axis=0)
        dlg_ref[...] = p * (dp - jnp.sum(p * dp, axis=0, keepdims=True))

    return pl.pallas_call(body, name="lower_bounds_bwd", out_shape=SDS(logits.shape, F32))(logits, dlb)


def _in_proj(x, gain, w, lng, lnb, ws, bst, name, rider=None):
    s, d = x.shape
    n = w.shape[1]
    tm = _divisor(s, 512, GM_BLOCK)

    def body(x_ref, g_ref, w_ref, lng_ref, lnb_ref, ws_ref, bst_ref, z_ref, h_ref, a_ref):
        xv = x_ref[...]
        r = lax.rsqrt(jnp.mean(xv * xv, axis=-1, keepdims=True) + EPS)
        h = ((xv * r) * g_ref[...]).astype(MM_DTYPE)
        h_ref[...] = h
        uv = jnp.dot(h, w_ref[:, : 2 * BR_DIM], preferred_element_type=F32)
        z_ref[:, : 2 * BR_DIM] = uv
        z_ref[:, 2 * BR_DIM :] = jnp.dot(h, w_ref[:, 2 * BR_DIM :], preferred_element_type=F32)
        _gmlp_fwd_tile(uv[:, :BR_DIM], uv[:, BR_DIM:], lng_ref, lnb_ref, ws_ref, bst_ref, a_ref)

    row = lambda wd: pl.BlockSpec((tm, wd), lambda i: (i, 0))
    return _hosted_call(
        body,
        rider,
        name=name,
        grid=(s // tm,),
        in_specs=[row(d), _resident((1, d), lambda i: (0, 0)), _resident((d, n), lambda i: (0, 0))] + _gmlp_param_specs(),
        out_specs=[row(n), row(d), row(BR_DIM)],
        out_shape=[SDS((s, n), F32), SDS((s, d), MM_DTYPE), SDS((s, BR_DIM), MM_DTYPE)],
        scratch_shapes=[],
        args=[x, gain, w, lng, lnb, ws, bst],
        semantics=("arbitrary",),
    )


def _in_proj_bwd(dz_parts, w, x, gain, dres, name, rider=None):
    s, d = x.shape
    n = w.shape[1]
    tm = _divisor(s, 512, 16)
    widths = [p.shape[1] for p in dz_parts]
    offsets = [sum(widths[:k]) for k in range(len(widths))]
    n_parts = len(dz_parts)

    def body(*refs):
        dz_refs = refs[:n_parts]
        w_ref, x_ref, g_ref, dres_ref, dx_ref, dg_ref = refs[n_parts:]

        @pl.when(pl.program_id(0) == 0)
        def _():
            dg_ref[...] = jnp.zeros_like(dg_ref)

        dh = None
        for dz_ref, off, wd in zip(dz_refs, offsets, widths):
            part = _dot_nt(dz_ref[...], w_ref[:, off : off + wd])
            dh = part if dh is None else dh + part
        xv = x_ref[...]
        r = lax.rsqrt(jnp.mean(xv * xv, axis=-1, keepdims=True) + EPS)
        xh = xv * r
        dg_ref[...] += jnp.sum(dh * xh, axis=0, keepdims=True)
        dxh = dh * g_ref[...]
        dx_ref[...] = dres_ref[...] + r * (dxh - xh * jnp.mean(dxh * xh, axis=-1, keepdims=True))

    in_specs = [pl.BlockSpec((tm, wd), lambda i: (i, 0)) for wd in widths] + [
        _resident((d, n), lambda i: (0, 0)),
        pl.BlockSpec((tm, d), lambda i: (i, 0)),
        _resident((1, d), lambda i: (0, 0)),
        pl.BlockSpec((tm, d), lambda i: (i, 0)),
    ]
    return _hosted_call(
        body,
        rider,
        name=name,
        grid=(s // tm,),
        in_specs=in_specs,
        out_specs=[pl.BlockSpec((tm, d), lambda i: (i, 0)), pl.BlockSpec((1, d), lambda i: (0, 0))],
        out_shape=[SDS((s, d), F32), SDS((1, d), F32)],
        scratch_shapes=[],
        args=[*dz_parts, w, x, gain, dres],
        semantics=("arbitrary",),
    )


def _matmul_tn(a, b, stack, layer, name, n_total=None, col_off=0):
    s, k = a.shape
    n = b.shape[1]
    n_total = n if n_total is None else n_total
    tn = _divisor(math.gcd(n, col_off) if col_off else n, max(128, ACC_BYTES // (4 * k)), 128)
    ts = _divisor(s, min(2048, max(512, ACC_BYTES // (2 * k))), 16)
    off = col_off // tn

    def body(a_ref, b_ref, *rest):
        out_ref = rest[-1]

        @pl.when(pl.program_id(1) == 0)
        def _():
            out_ref[...] = jnp.zeros_like(out_ref)

        out_ref[...] += _dot_tn(a_ref[...], b_ref[...])

    in_specs = [pl.BlockSpec((ts, k), lambda j, i: (i, 0)), pl.BlockSpec((ts, tn), lambda j, i: (i, j))]
    args = [a, b]
    aliases = {}
    if stack is not None:
        in_specs.append(ANY)
        args.append(stack)
        aliases = {2: 0}
    return pl.pallas_call(
        body,
        name=name,
        grid=(n // tn, s // ts),
        in_specs=in_specs,
        out_specs=pl.BlockSpec((None, k, tn), lambda j, i: (layer, 0, off + j)),
        out_shape=SDS((DEPTH, k, n_total), F32),
        input_output_aliases=aliases,
        compiler_params=_params("parallel", "arbitrary"),
    )(*args)


def _gm_mask():
    r = lax.broadcasted_iota(jnp.int32, (GM_BLOCK, GM_BLOCK), 0) // CHUNK
    c = lax.broadcasted_iota(jnp.int32, (GM_BLOCK, GM_BLOCK), 1) // CHUNK
    return r >= c


def _gm_normed(v, lng, lnb):
    vg = _gelu(v)
    mu = jnp.mean(vg, axis=-1, keepdims=True)
    xc = vg - mu
    rstd = lax.rsqrt(jnp.mean(xc * xc, axis=-1, keepdims=True) + EPS)
    xh = xc * rstd
    return xh, rstd, xh * lng + lnb


def _gmlp_param_specs():
    return [
        _resident((1, BR_DIM), lambda i: (0, 0)),
        _resident((1, BR_DIM), lambda i: (0, 0)),
        _resident((N_HEADS, GM_BLOCK, GM_BLOCK), lambda i: (0, 0, 0)),
        _resident((GM_BLOCK, N_HEADS), lambda i: (0, 0)),
    ]


def _gmlp_fwd_tile(u, v, lng_ref, lnb_ref, ws_ref, bst_ref, a_ref):
    mask = _gm_mask()
    ug = _gelu(u)
    _, _, vn = _gm_normed(v, lng_ref[...], lnb_ref[...])
    vn = vn.astype(MM_DTYPE)
    for h in range(N_HEADS):
        cs = slice(h * HEAD_DIM, (h + 1) * HEAD_DIM)
        wm = jnp.where(mask, ws_ref[h], 0.0).astype(MM_DTYPE)
        for blk in range(u.shape[0] // GM_BLOCK):
            rs = slice(blk * GM_BLOCK, (blk + 1) * GM_BLOCK)
            mixed = _dot(wm, vn[rs, cs]) + bst_ref[:, h : h + 1]
            a_ref[rs, cs] = (ug[rs, cs] * mixed).astype(MM_DTYPE)


def _gmlp_bwd_tile(da_v, u, v, lng_ref, lnb_ref, ws_ref, bst_ref, dz_ref, dlng_ref, dlnb_ref, dws_ref, dbst_ref):
    mask = _gm_mask()
    ug = _gelu(u)
    lng_v = lng_ref[...]
    xh, rstd, vn = _gm_normed(v, lng_v, lnb_ref[...])
    vnb = vn.astype(MM_DTYPE)
    dmix = da_v * ug
    dmixb = dmix.astype(MM_DTYPE)
    dvn_cols = []
    for h in range(N_HEADS):
        cs = slice(h * HEAD_DIM, (h + 1) * HEAD_DIM)
        wm = jnp.where(mask, ws_ref[h], 0.0).astype(MM_DTYPE)
        dw = jnp.zeros((GM_BLOCK, GM_BLOCK), F32)
        dbias = jnp.zeros((GM_BLOCK, 1), F32)
        dvn_rows = []
        for blk in range(u.shape[0] // GM_BLOCK):
            rs = slice(blk * GM_BLOCK, (blk + 1) * GM_BLOCK)
            mixed = _dot(wm, vnb[rs, cs]) + bst_ref[:, h : h + 1]
            dz_ref[rs, cs] = (da_v[rs, cs] * mixed * _gelu_grad(u[rs, cs])).astype(MM_DTYPE)
            dw = dw + _dot_nt(dmixb[rs, cs], vnb[rs, cs])
            dbias = dbias + jnp.sum(dmix[rs, cs], axis=1, keepdims=True)
            dvn_rows.append(_dot_tn(wm, dmixb[rs, cs]))
        dws_ref[h] += jnp.where(mask, dw, 0.0)
        dbst_ref[:, h : h + 1] += dbias
        dvn_cols.append(jnp.concatenate(dvn_rows, axis=0) if len(dvn_rows) > 1 else dvn_rows[0])
    dvn = jnp.concatenate(dvn_cols, axis=1)
    dlng_ref[...] += jnp.sum(dvn * xh, axis=0, keepdims=True)
    dlnb_ref[...] += jnp.sum(dvn, axis=0, keepdims=True)
    dxh = dvn * lng_v
    dvg = rstd * (dxh - jnp.mean(dxh, axis=-1, keepdims=True) - xh * jnp.mean(dxh * xh, axis=-1, keepdims=True))
    dz_ref[:, BR_DIM:] = (dvg * _gelu_grad(v)).astype(MM_DTYPE)


HG_ROWS = 512
HG_UNROLL = 8
MID = CHUNK // 2 - 1


def _tri(lower):
    r = lax.broadcasted_iota(jnp.int32, (CHUNK, CHUNK), 0)
    c = lax.broadcasted_iota(jnp.int32, (CHUNK, CHUNK), 1)
    return r >= c if lower else c >= r


def _scan_rows(x, reverse=False):
    n = x.shape[0]
    rid = lax.broadcasted_iota(jnp.int32, x.shape, 0)
    k = 1
    while k < n:
        if reverse:
            x = x + jnp.where(rid < n - k, pltpu.roll(x, n - k, 0), 0.0)
        else:
            x = x + jnp.where(rid >= k, pltpu.roll(x, k, 0), 0.0)
        k *= 2
    return x


def _hgrn_fwd_chunk(zqf_ref, zio_ref, ci, lb_ref, ng_v, state, o_ref, b_ref, st_ref):
    low = _tri(True)
    rows = slice(ci * CHUNK, (ci + 1) * CHUNK)
    for h in range(N_HEADS):
        cs = slice(h * HEAD_DIM, (h + 1) * HEAD_DIM)
        cs2 = slice(BR_DIM + h * HEAD_DIM, BR_DIM + (h + 1) * HEAD_DIM)
        zq = zqf_ref[rows, cs]
        zf = zqf_ref[rows, cs2]
        vi = zio_ref[rows, cs]
        zog = zio_ref[rows, cs2]
        lbh = lb_ref[:, cs]
        qf = zq * _sigmoid(zq)
        forget = lbh + (1.0 - lbh) * _sigmoid(zf)
        g = jnp.log(jnp.maximum(forget, TINY))
        k = (1.0 - lbh) * _sigmoid(-zf)
        gc = _scan_rows(g)
        gm = gc[MID : MID + 1]
        gl = gc[CHUNK - 1 : CHUNK]
        qt = qf * jnp.exp(gc - gm)
        kt = k * jnp.exp(gm - gc)
        e_gm = jnp.exp(gm)
        e_lm = jnp.exp(gl - gm)
        a = jnp.where(low, _dot_nt(qt, kt), 0.0)
        st = state[h]
        st_ref[ci, h] = st
        o = _dot(a, vi) + _dot_nt(qt * e_gm, st)
        state[h] = st * (e_gm * e_lm) + _dot_tn(vi, kt * e_lm)
        r = lax.rsqrt(jnp.mean(o * o, axis=-1, keepdims=True) + EPS)
        o_ref[rows, cs] = o
        b_ref[rows, cs] = (((o * r) * ng_v) * (zog * _sigmoid(zog))).astype(MM_DTYPE)


def _hgrn_fwd(z, lb, ng, name, rider=None):
    s = z.shape[0]
    rows_per = _divisor(s, HG_ROWS, CHUNK)
    n_sub = rows_per // CHUNK

    def body(zqf_ref, zio_ref, lb_ref, ng_ref, o_ref, b_ref, st_ref, state):
        @pl.when(pl.program_id(0) == 0)
        def _():
            state[...] = jnp.zeros_like(state)

        ng_v = ng_ref[...]
        for ci in range(n_sub):
            _hgrn_fwd_chunk(zqf_ref, zio_ref, ci, lb_ref, ng_v, state, o_ref, b_ref, st_ref)

    return _hosted_call(
        body,
        rider,
        name=name,
        grid=(s // rows_per,),
        in_specs=[
            pl.BlockSpec((rows_per, 2 * BR_DIM), lambda i: (i, 1)),
            pl.BlockSpec((rows_per, 2 * BR_DIM), lambda i: (i, 2)),
            _resident((1, BR_DIM), lambda i: (0, 0)),
            _resident((1, HEAD_DIM), lambda i: (0, 0)),
        ],
        out_specs=[
            pl.BlockSpec((rows_per, BR_DIM), lambda i: (i, 0)),
            pl.BlockSpec((rows_per, BR_DIM), lambda i: (i, 0)),
            pl.BlockSpec((n_sub, N_HEADS, HEAD_DIM, HEAD_DIM), lambda i: (i, 0, 0, 0)),
        ],
        out_shape=[
            SDS((s, BR_DIM), F32),
            SDS((s, BR_DIM), MM_DTYPE),
            SDS((s // CHUNK, N_HEADS, HEAD_DIM, HEAD_DIM), F32),
        ],
        scratch_shapes=[pltpu.VMEM((N_HEADS, HEAD_DIM, HEAD_DIM), F32)],
        args=[z, z, lb, ng],
        semantics=("arbitrary",),
    )


def _hgrn_bwd(db, z, o, states, lb, ng, name, rider=None):
    s = z.shape[0]
    rows_per = _divisor(s, HG_ROWS, CHUNK)
    n_sub = rows_per // CHUNK
    n_steps = s // rows_per

    def body(db_ref, zqf_ref, zio_ref, o_ref, st_ref, lb_ref, ng_ref, dz_ref, dlb_ref, dng_ref, dstate):
        @pl.when(pl.program_id(0) == 0)
        def _():
            dstate[...] = jnp.zeros_like(dstate)
            dlb_ref[...] = jnp.zeros_like(dlb_ref)
            dng_ref[...] = jnp.zeros_like(dng_ref)

        low = _tri(True)
        ng_v = ng_ref[...]

        def chunk(cc, carry):
            ci = n_sub - 1 - cc
            rows = pl.ds(pl.multiple_of(ci * CHUNK, CHUNK), CHUNK)
            for h in range(N_HEADS):
                cs = slice(h * HEAD_DIM, (h + 1) * HEAD_DIM)
                cs2 = slice(BR_DIM + h * HEAD_DIM, BR_DIM + (h + 1) * HEAD_DIM)
                zq = zqf_ref[rows, cs]
                zf = zqf_ref[rows, cs2]
                vi = zio_ref[rows, cs]
                zog = zio_ref[rows, cs2]
                lbh = lb_ref[:, cs]
                ov = o_ref[rows, cs]
                dbv = db_ref[rows, cs]
                r = lax.rsqrt(jnp.mean(ov * ov, axis=-1, keepdims=True) + EPS)
                on = ov * r
                sgo = _sigmoid(zog)
                gate = zog * sgo
                dng_ref[...] += jnp.sum(dbv * gate * on, axis=0, keepdims=True)
                don = dbv * gate * ng_v
                dzog = dbv * (on * ng_v) * _silu_grad(zog, sgo)
                do = r * (don - on * jnp.mean(don * on, axis=-1, keepdims=True))
                sq = _sigmoid(zq)
                qf = zq * sq
                sg = _sigmoid(zf)
                sgn = _sigmoid(-zf)
                oml = 1.0 - lbh
                forget = lbh + oml * sg
                fm = jnp.maximum(forget, TINY)
                k = oml * sgn
                gc = _scan_rows(jnp.log(fm))
                gm = gc[MID : MID + 1]
                gl = gc[CHUNK - 1 : CHUNK]
                e_g = jnp.exp(gc)
                e_q = jnp.exp(gc - gm)
                e_k = jnp.exp(gm - gc)
                e_kd = jnp.exp(gl - gc)
                e_gl = jnp.exp(gl)
                qt = qf * e_q
                kt = k * e_k
                a = jnp.where(low, _dot_nt(qt, kt), 0.0)
                st = st_ref[ci, h]
                dst = dstate[h]
                dp = jnp.where(low, _dot_nt(do, vi), 0.0)
                dv = _dot_tn(a, do) + _dot_nt(k * e_kd, dst)
                dq = _dot(dp, kt) * e_q + e_g * _dot(do, st)
                dks = e_kd * _dot(vi, dst)
                dk = _dot_tn(dp, qt) * e_k + dks
                dstate[h] = _dot_tn(do, qf * e_g) + dst * e_gl
                dgc = qf * dq - k * dk
                extra = e_gl * jnp.sum(st * dst, axis=0, keepdims=True) + jnp.sum(k * dks, axis=0, keepdims=True)
                dg = _scan_rows(dgc, reverse=True) + extra
                dforget = jnp.where(forget > TINY, dg / fm, 0.0)
                both = dforget - dk
                dlb_ref[:, cs] += jnp.sum(sgn * both, axis=0, keepdims=True)
                dz_ref[rows, cs] = (dq * _silu_grad(zq, sq)).astype(MM_DTYPE)
                dz_ref[rows, cs2] = (oml * sg * sgn * both).astype(MM_DTYPE)
                dz_ref[rows, 2 * BR_DIM + h * HEAD_DIM : 2 * BR_DIM + (h + 1) * HEAD_DIM] = dv.astype(MM_DTYPE)
                dz_ref[rows, 3 * BR_DIM + h * HEAD_DIM : 3 * BR_DIM + (h + 1) * HEAD_DIM] = dzog.astype(MM_DTYPE)
            return carry

        lax.fori_loop(0, n_sub, chunk, 0, unroll=math.gcd(n_sub, HG_UNROLL))

    rev = lambda i: n_steps - 1 - i
    return _hosted_call(
        body,
        rider,
        name=name,
        grid=(n_steps,),
        in_specs=[
            pl.BlockSpec((rows_per, BR_DIM), lambda i: (rev(i), 0)),
            pl.BlockSpec((rows_per, 2 * BR_DIM), lambda i: (rev(i), 1)),
            pl.BlockSpec((rows_per, 2 * BR_DIM), lambda i: (rev(i), 2)),
            pl.BlockSpec((rows_per, BR_DIM), lambda i: (rev(i), 0)),
            pl.BlockSpec((n_sub, N_HEADS, HEAD_DIM, HEAD_DIM), lambda i: (rev(i), 0, 0, 0)),
            _resident((1, BR_DIM), lambda i: (0, 0)),
            _resident((1, HEAD_DIM), lambda i: (0, 0)),
        ],
        out_specs=[
            pl.BlockSpec((rows_per, 4 * BR_DIM), lambda i: (rev(i), 0)),
            pl.BlockSpec((1, BR_DIM), lambda i: (0, 0)),
            pl.BlockSpec((1, HEAD_DIM), lambda i: (0, 0)),
        ],
        out_shape=[SDS((s, 4 * BR_DIM), MM_DTYPE), SDS((1, BR_DIM), F32), SDS((1, HEAD_DIM), F32)],
        scratch_shapes=[pltpu.VMEM((N_HEADS, HEAD_DIM, HEAD_DIM), F32)],
        args=[db, z, z, o, states, lb, ng],
        semantics=("arbitrary",),
    )


def _mix_fwd(x, a, b, z, w_gm, w_hg, w_out, name, rider=None):
    s, d = x.shape
    tm = _divisor(s, 512, 16)
    g0 = 6 * BR_DIM // d

    def body(x_ref, a_ref, b_ref, ga_ref, gb_ref, wgm_ref, whg_ref, wout_ref, out_ref):
        y = _sigmoid(ga_ref[...]) * _dot(a_ref[...], wgm_ref[...]) + _sigmoid(gb_ref[...]) * _dot(b_ref[...], whg_ref[...])
        out_ref[...] = x_ref[...] + _dot(y, wout_ref[...])

    return _hosted_call(
        body,
        rider,
        name=name,
        grid=(s // tm,),
        in_specs=[
            pl.BlockSpec((tm, d), lambda i: (i, 0)),
            pl.BlockSpec((tm, BR_DIM), lambda i: (i, 0)),
            pl.BlockSpec((tm, BR_DIM), lambda i: (i, 0)),
            pl.BlockSpec((tm, d), lambda i: (i, g0)),
            pl.BlockSpec((tm, d), lambda i: (i, g0 + 1)),
            _resident((BR_DIM, d), lambda i: (0, 0)),
            _resident((BR_DIM, d), lambda i: (0, 0)),
            _resident((d, d), lambda i: (0, 0)),
        ],
        out_specs=[pl.BlockSpec((tm, d), lambda i: (i, 0))],
        out_shape=[SDS((s, d), F32)],
        scratch_shapes=[],
        args=[x, a, b, z, z, w_gm, w_hg, w_out],
        semantics=("arbitrary",),
    )


def _mix_bwd(dx, a, b, z, w_gm, w_hg, w_out, lng, lnb, ws, bst, name, rider=None):
    s, d = dx.shape
    tm = _divisor(s, 512, GM_BLOCK)
    g0 = 6 * BR_DIM // d

    def body(dx_ref, a_ref, b_ref, uv_ref, ga_ref, gb_ref, wgm_ref, whg_ref, wout_ref, lng_ref, lnb_ref, ws_ref, bst_ref,
             y_ref, dpa_ref, dpb_ref, db_ref, dg_ref, duv_ref, dlng_ref, dlnb_ref, dws_ref, dbst_ref):
        @pl.when(pl.program_id(0) == 0)
        def _():
            for ref in (dlng_ref, dlnb_ref, dws_ref, dbst_ref):
                ref[...] = jnp.zeros_like(ref)

        dy = _dot_nt(dx_ref[...], wout_ref[...])
        pa = _dot(a_ref[...], wgm_ref[...])
        pb = _dot(b_ref[...], whg_ref[...])
        sa = _sigmoid(ga_ref[...])
        sb = _sigmoid(gb_ref[...])
        y_ref[...] = (sa * pa + sb * pb).astype(MM_DTYPE)
        dpa = (dy * sa).astype(MM_DTYPE)
        dpb = (dy * sb).astype(MM_DTYPE)
        dpa_ref[...] = dpa
        dpb_ref[...] = dpb
        da = _dot_nt(dpa, wgm_ref[...])
        db_ref[...] = _dot_nt(dpb, whg_ref[...])
        dg_ref[:, :d] = (dy * pa * sa * (1.0 - sa)).astype(MM_DTYPE)
        dg_ref[:, d:] = (dy * pb * sb * (1.0 - sb)).astype(MM_DTYPE)
        _gmlp_bwd_tile(da, uv_ref[:, :BR_DIM], uv_ref[:, BR_DIM:], lng_ref, lnb_ref, ws_ref, bst_ref,
                       duv_ref, dlng_ref, dlnb_ref, dws_ref, dbst_ref)

    row = lambda w: pl.BlockSpec((tm, w), lambda i: (i, 0))
    fixed = lambda shape: pl.BlockSpec(shape, lambda i: tuple(0 for _ in shape))
    return _hosted_call(
        body,
        rider,
        name=name,
        grid=(s // tm,),
        in_specs=[
            row(d),
            row(BR_DIM),
            row(BR_DIM),
            row(2 * BR_DIM),
            pl.BlockSpec((tm, d), lambda i: (i, g0)),
            pl.BlockSpec((tm, d), lambda i: (i, g0 + 1)),
            _resident((BR_DIM, d), lambda i: (0, 0)),
            _resident((BR_DIM, d), lambda i: (0, 0)),
            _resident((d, d), lambda i: (0, 0)),
        ]
        + _gmlp_param_specs(),
        out_specs=[row(d), row(d), row(d), row(BR_DIM), row(2 * d), row(2 * BR_DIM), fixed((1, BR_DIM)), fixed((1, BR_DIM)),
                   fixed((N_HEADS, GM_BLOCK, GM_BLOCK)), fixed((GM_BLOCK, N_HEADS))],
        out_shape=[
            SDS((s, d), MM_DTYPE),
            SDS((s, d), MM_DTYPE),
            SDS((s, d), MM_DTYPE),
            SDS((s, BR_DIM), F32),
            SDS((s, 2 * d), MM_DTYPE),
            SDS((s, 2 * BR_DIM), MM_DTYPE),
            SDS((1, BR_DIM), F32),
            SDS((1, BR_DIM), F32),
            SDS((N_HEADS, GM_BLOCK, GM_BLOCK), F32),
            SDS((GM_BLOCK, N_HEADS), F32),
        ],
        scratch_shapes=[],
        args=[dx, a, b, z, z, z, w_gm, w_hg, w_out, lng, lnb, ws, bst],
        semantics=("arbitrary",),
    )


HALO = 8
FFN_ROWS = 256


def _shift_down(v, prev, n):
    rolled = pltpu.roll(v, n, 0)
    rid = lax.broadcasted_iota(jnp.int32, v.shape, 0)
    out = rolled
    for r in range(n):
        out = jnp.where(rid == r, prev[HALO - n + r : HALO - n + r + 1], out)
    return out


def _shift_up(v, nxt, n):
    tm = v.shape[0]
    rolled = pltpu.roll(v, tm - n, 0)
    rid = lax.broadcasted_iota(jnp.int32, v.shape, 0)
    out = rolled
    for r in range(n):
        out = jnp.where(rid == tm - n + r, nxt[r : r + 1], out)
    return out


def _conv_cols(f):
    return _divisor(f, 256, 128)


def _ffn_fwd(x, gain, w_up, cw, cb, w_down, layer, name, rider=None):
    s, d = x.shape
    f = w_down.shape[0]
    tm = _divisor(s, FFN_ROWS, 16)
    cwid = _conv_cols(f)

    def body(x_ref, g_ref, wu_ref, cw_ref, cb_ref, wd_ref, out_ref, z_ref, h_ref, conv_ref, halo):
        @pl.when(pl.program_id(0) == 0)
        def _():
            halo[...] = jnp.zeros_like(halo)

        xv = x_ref[...]
        r = lax.rsqrt(jnp.mean(xv * xv, axis=-1, keepdims=True) + EPS)
        h = ((xv * r) * g_ref[...]).astype(MM_DTYPE)
        h_ref[...] = h
        acc = xv
        starts = list(range(0, f, cwid))

        def project(c0):
            return [jnp.dot(h, wu_ref[:, base : base + cwid], preferred_element_type=F32) for base in (c0, f + c0)]

        ahead = project(starts[0])
        for n, c0 in enumerate(starts):
            halves = []
            current = ahead
            if n + 1 < len(starts):
                ahead = project(starts[n + 1])
            for base, zc in zip((c0, f + c0), current):
                cols = slice(base, base + cwid)
                z_ref[:, cols] = zc.astype(MM_DTYPE)
                prev = halo[:, cols]
                conv = cb_ref[:, cols] + cw_ref[0:1, cols] * _shift_down(zc, prev, 2)
                conv = conv + cw_ref[1:2, cols] * _shift_down(zc, prev, 1) + cw_ref[2:3, cols] * zc
                halo[:, cols] = zc[tm - HALO : tm]
                conv_ref[:, cols] = conv.astype(MM_DTYPE)
                halves.append(conv)
            gate, val = halves
            act = gate * _sigmoid(gate) * val
            acc = acc + _dot(act, wd_ref[c0 : c0 + cwid, :])
        out_ref[...] = acc

    row = lambda w: pl.BlockSpec((tm, w), lambda i: (i, 0))
    return _hosted_call(
        body,
        rider,
        name=name,
        grid=(s // tm,),
        in_specs=[
            row(d),
            _resident((1, d), lambda i: (0, 0)),
            _resident((d, 2 * f), lambda i: (0, 0)),
            _resident((None, 3, 2 * f), lambda i: (layer, 0, 0)),
            _resident((1, 2 * f), lambda i: (0, 0)),
            _resident((f, d), lambda i: (0, 0)),
        ],
        out_specs=[row(d), row(2 * f), row(d), row(2 * f)],
        out_shape=[SDS((s, d), F32), SDS((s, 2 * f), MM_DTYPE), SDS((s, d), MM_DTYPE), SDS((s, 2 * f), MM_DTYPE)],
        scratch_shapes=[pltpu.VMEM((HALO, 2 * f), F32)],
        args=[x, gain, w_up, cw, cb, w_down],
        semantics=("arbitrary",),
    )


def _ffn_bwd(dx, z2, conv, cw, w_down, w_up, x, gain, layer, name):
    s, d = dx.shape
    f = z2.shape[1] // 2
    tm = _divisor(s, 256, 16)
    cwid = _conv_cols(f)
    n_steps = s // tm

    def body(dx_ref, z_ref, conv_ref, cw_ref, wd_ref, wu_ref, x_ref, g_ref, dz_ref, act_ref, dcw_ref, dx1_ref, dg_ref, nxt):
        @pl.when(pl.program_id(0) == 0)
        def _():
            nxt[...] = jnp.zeros_like(nxt)
            dcw_ref[...] = jnp.zeros_like(dcw_ref)
            dg_ref[...] = jnp.zeros_like(dg_ref)

        dxf = dx_ref[...]
        dxv = dxf.astype(MM_DTYPE)
        starts = list(range(0, f, cwid))
        ahead = _dot_nt(dxv, wd_ref[0:cwid, :])
        dh = jnp.zeros((tm, d), F32)
        pending = []
        for n, c0 in enumerate(starts):
            dact = ahead
            if n + 1 < len(starts):
                ahead = _dot_nt(dxv, wd_ref[starts[n + 1] : starts[n + 1] + cwid, :])
            for dz_prev, cols_prev in pending:
                dh = dh + _dot_nt(dz_prev, wu_ref[:, cols_prev])
            pending = []
            gate = conv_ref[:, c0 : c0 + cwid].astype(F32)
            val = conv_ref[:, f + c0 : f + c0 + cwid].astype(F32)
            sg = _sigmoid(gate)
            silu = gate * sg
            act_ref[:, c0 : c0 + cwid] = (silu * val).astype(MM_DTYPE)
            dconvs = (dact * val * _silu_grad(gate, sg), dact * silu)
            for base, dconv in zip((c0, f + c0), dconvs):
                cols = slice(base, base + cwid)
                zc = z_ref[:, cols].astype(F32)
                nx = nxt[:, cols]
                up1 = _shift_up(dconv, nx, 1)
                up2 = _shift_up(dconv, nx, 2)
                dcw_ref[0:1, cols] += jnp.sum(up2 * zc, axis=0, keepdims=True)
                dcw_ref[1:2, cols] += jnp.sum(up1 * zc, axis=0, keepdims=True)
                dcw_ref[2:3, cols] += jnp.sum(dconv * zc, axis=0, keepdims=True)
                dcw_ref[3:4, cols] += jnp.sum(dconv, axis=0, keepdims=True)
                dz = (cw_ref[2:3, cols] * dconv + cw_ref[1:2, cols] * up1 + cw_ref[0:1, cols] * up2).astype(MM_DTYPE)
                dz_ref[:, cols] = dz
                nxt[:, cols] = dconv[0:HALO]
                pending.append((dz, cols))
        for dz_prev, cols_prev in pending:
            dh = dh + _dot_nt(dz_prev, wu_ref[:, cols_prev])
        xv = x_ref[...]
        r = lax.rsqrt(jnp.mean(xv * xv, axis=-1, keepdims=True) + EPS)
        xh = xv * r
        dg_ref[...] += jnp.sum(dh * xh, axis=0, keepdims=True)
        dxh = dh * g_ref[...]
        dx1_ref[...] = dxf + r * (dxh - xh * jnp.mean(dxh * xh, axis=-1, keepdims=True))

    rev = lambda i: n_steps - 1 - i
    row = lambda w: pl.BlockSpec((tm, w), lambda i: (rev(i), 0))
    return pl.pallas_call(
        body,
        name=name,
        grid=(n_steps,),
        in_specs=[
            row(d),
            row(2 * f),
            row(2 * f),
            _resident((None, 3, 2 * f), lambda i: (layer, 0, 0)),
            _resident((f, d), lambda i: (0, 0)),
            _resident((d, 2 * f), lambda i: (0, 0)),
            row(d),
            _resident((1, d), lambda i: (0, 0)),
        ],
        out_specs=[
            row(2 * f),
            row(f),
            pl.BlockSpec((HALO, 2 * f), lambda i: (0, 0)),
            row(d),
            pl.BlockSpec((1, d), lambda i: (0, 0)),
        ],
        out_shape=[SDS((s, 2 * f), MM_DTYPE), SDS((s, f), MM_DTYPE), SDS((HALO, 2 * f), F32), SDS((s, d), F32), SDS((1, d), F32)],
        scratch_shapes=[pltpu.VMEM((HALO, 2 * f), F32)],
        compiler_params=_params("arbitrary"),
    )(dx, z2, conv, cw, w_down, w_up, x, gain)


def _loss_head(x, gain, target):
    s, d = x.shape
    tm = _divisor(s, 256, 8)

    def body(x_ref, g_ref, t_ref, loss_ref, dx_ref, dg_ref):
        @pl.when(pl.program_id(0) == 0)
        def _():
            loss_ref[...] = jnp.zeros_like(loss_ref)
            dg_ref[...] = jnp.zeros_like(dg_ref)

        xv = x_ref[...]
        gv = g_ref[...]
        r = lax.rsqrt(jnp.mean(xv * xv, axis=-1, keepdims=True) + EPS)
        xh = xv * r
        err = xh * gv - t_ref[...]
        loss_ref[...] += 0.5 * jnp.sum(jnp.mean(err * err, axis=-1, keepdims=True))
        dout = err * (1.0 / d)
        dg_ref[...] += jnp.sum(dout * xh, axis=0, keepdims=True)
        dxh = dout * gv
        dx_ref[...] = r * (dxh - xh * jnp.mean(dxh * xh, axis=-1, keepdims=True))

    return pl.pallas_call(
        body,
        name="loss_head",
        grid=(s // tm,),
        in_specs=[
            pl.BlockSpec((tm, d), lambda i: (i, 0)),
            _resident((1, d), lambda i: (0, 0)),
            pl.BlockSpec((tm, d), lambda i: (i, 0)),
        ],
        out_specs=[
            pl.BlockSpec((8, 128), lambda i: (0, 0)),
            pl.BlockSpec((tm, d), lambda i: (i, 0)),
            pl.BlockSpec((1, d), lambda i: (0, 0)),
        ],
        out_shape=[SDS((8, 128), F32), SDS((s, d), F32), SDS((1, d), F32)],
        compiler_params=_params("arbitrary"),
    )(x, gain, target)


def _adamw(w, g, m, v, name):
    shape = w.shape
    cols = shape[-1]
    rows = max(1, math.prod(shape[:-1]))
    tr = _divisor(rows, max(8, TILE_BYTES // (4 * cols)), 8)
    flat = [t.reshape(rows, cols) for t in (w, g, m, v)]

    def body(w_ref, g_ref, m_ref, v_ref, d_ref, nm_ref, nv_ref):
        gv = g_ref[...]
        m2 = ADAM_B1 * m_ref[...] + (1.0 - ADAM_B1) * gv
        v2 = ADAM_B2 * v_ref[...] + (1.0 - ADAM_B2) * (gv * gv)
        m_hat = m2 / (1.0 - ADAM_B1**ADAM_STEP)
        v_hat = v2 / (1.0 - ADAM_B2**ADAM_STEP)
        d_ref[...] = -ADAM_LR * (m_hat / (jnp.sqrt(v_hat) + ADAM_EPS) + ADAM_WD * w_ref[...])
        nm_ref[...] = m2
        nv_ref[...] = v2

    spec = pl.BlockSpec((tr, cols), lambda i: (i, 0))
    outs = pl.pallas_call(
        body,
        name=name,
        grid=(rows // tr,),
        in_specs=[spec] * 4,
        out_specs=[spec] * 3,
        out_shape=[SDS((rows, cols), F32)] * 3,
        compiler_params=_params("parallel"),
    )(*flat)
    return tuple(t.reshape(shape) for t in outs)


def _position():
    return lax.axis_index("x"), lax.axis_index("y"), lax.axis_index("c")


def _other_chips(x, y):
    return [(1 - x, y), (x, 1 - y), (1 - x, 1 - y)]


def _remote(src, dst, send_sem, recv_sem, device):
    return pltpu.make_async_remote_copy(
        src_ref=src, dst_ref=dst, send_sem=send_sem, recv_sem=recv_sem, device_id=device, device_id_type=MESH
    )


def _sub(ref, lead, shape, axis, chip=None, half=None):
    cut = [pl.ds(0, shape[0]), pl.ds(0, shape[1])]
    if chip is not None:
        size = shape[axis] // N_CHIPS
        cut[axis] = pl.ds(chip * size, size)
    if half is not None:
        size = shape[1 - axis] // 2
        cut[1 - axis] = pl.ds(half * size, size)
    return ref.at[(*lead, *cut)]


def _scaled(shape, axis, num, den=1):
    return tuple(d * num // den if i == axis else d for i, d in enumerate(shape))


def _gather_rider(shards, axes, layer, conv_w=None):
    n = len(shards)
    shard2d = [t.shape[1:] for t in shards]
    full2d = [_scaled(s2, ax, N_CHIPS) for s2, ax in zip(shard2d, axes)]
    out_shapes = [SDS(f2, t.dtype) for f2, t in zip(full2d, shards)]
    inputs = list(shards)
    own = 6
    scratch = [pltpu.SemaphoreType.DMA((n, 7)), pltpu.SemaphoreType.DMA((n, 7))]
    if conv_w is not None:
        cshape = conv_w.shape
        inputs.append(conv_w)
        out_shapes.append(SDS(_scaled(cshape, 2, N_CHIPS), conv_w.dtype))
        scratch += [pltpu.SemaphoreType.DMA((4,)), pltpu.SemaphoreType.DMA((4,))]

    def conv_slab(ref, chip):
        return ref.at[:, :, pl.ds((2 * chip[0] + chip[1]) * cshape[2], cshape[2])]

    def own_copy(ins, outs, send_sems, recv_sems, k):
        x, y, c = _position()
        slab = _sub(outs[k], (), full2d[k], axes[k], chip=2 * x + y)
        return _remote(ins[k].at[layer], slab, send_sems.at[k, own], recv_sems.at[k, own], (x, y, 1 - c))

    def start(ins, outs, scr):
        send_sems, recv_sems = scr[:2]
        x, y, c = _position()
        me = 2 * x + y
        for k in range(n):
            own_copy(ins, outs, send_sems, recv_sems, k).start()
            for j, chip in enumerate(_other_chips(x, y)):
                _remote(
                    _sub(ins[k], (layer,), shard2d[k], axes[k], half=c),
                    _sub(outs[k], (), full2d[k], axes[k], chip=me, half=c),
                    send_sems.at[k, j], recv_sems.at[k, j], (*chip, c),
                ).start()
        if conv_w is not None:
            csend, crecv = scr[2:]
            _remote(ins[n], conv_slab(outs[n], (x, y)), csend.at[3], crecv.at[3], (x, y, 1 - c)).start()
            for j, chip in enumerate(_other_chips(x, y)):
                _remote(ins[n], conv_slab(outs[n], (x, y)), csend.at[j], crecv.at[j], (*chip, c)).start()

    def finish(ins, outs, scr):
        send_sems, recv_sems = scr[:2]
        x, y, c = _position()
        me = 2 * x + y
        sibling = (x, y, 1 - c)
        chips = _other_chips(x, y)
        forwards = []
        for k in range(n):
            src = _sub(ins[k], (layer,), shard2d[k], axes[k], half=c)
            for j, chip in enumerate(chips):
                landed = _sub(outs[k], (), full2d[k], axes[k], chip=2 * chip[0] + chip[1], half=c)
                _remote(src, landed, send_sems.at[k, j], recv_sems.at[k, j], (*chip, c)).wait_recv()
                fwd = _remote(landed, landed, send_sems.at[k, 3 + j], recv_sems.at[k, 3 + j], sibling)
                fwd.start()
                forwards.append(fwd)
        for k in range(n):
            src = _sub(ins[k], (layer,), shard2d[k], axes[k], half=c)
            for j, chip in enumerate(chips):
                theirs = _sub(outs[k], (), full2d[k], axes[k], chip=2 * chip[0] + chip[1], half=1 - c)
                _remote(src, theirs, send_sems.at[k, 3 + j], recv_sems.at[k, 3 + j], sibling).wait_recv()
        for fwd in forwards:
            fwd.wait_send()
        for k in range(n):
            src = _sub(ins[k], (layer,), shard2d[k], axes[k], half=c)
            mine = _sub(outs[k], (), full2d[k], axes[k], chip=me, half=c)
            for j, chip in enumerate(chips):
                _remote(src, mine, send_sems.at[k, j], recv_sems.at[k, j], (*chip, c)).wait_send()
            own_copy(ins, outs, send_sems, recv_sems, k).wait()
        if conv_w is not None:
            csend, crecv = scr[2:]
            for j, chip in enumerate(chips):
                cp = _remote(ins[n], conv_slab(outs[n], chip), csend.at[j], crecv.at[j], (*chip, c))
                cp.wait_recv()
                cp.wait_send()
            _remote(ins[n], conv_slab(outs[n], (x, y)), csend.at[3], crecv.at[3], sibling).wait()

    return _Rider(inputs, out_shapes, scratch, start, finish)


def _pair_rider(stacks, axes, layer):
    n = len(stacks)
    shapes = [t.shape[1:] for t in stacks]
    out_shapes = [SDS(_scaled(s2, 1 - ax, 1, 2), F32) for s2, ax in zip(shapes, axes)]
    scratch = [pltpu.SemaphoreType.DMA((n,)), pltpu.SemaphoreType.DMA((n,))]

    def copies(ins, outs, scr):
        x, y, c = _position()
        return [
            _remote(_sub(ins[k], (layer,), shapes[k], axes[k], half=1 - c), outs[k], scr[0].at[k], scr[1].at[k], (x, y, 1 - c))
            for k in range(n)
        ]

    def start(ins, outs, scr):
        for cp in copies(ins, outs, scr):
            cp.start()

    def finish(ins, outs, scr):
        for cp in copies(ins, outs, scr):
            cp.wait()

    return _Rider(stacks, out_shapes, scratch, start, finish)


def _chip_rider(pairs, axes):
    n = len(pairs)
    shapes = [t.shape for t in pairs]
    out_shapes = [SDS((3,) + _scaled(s2, ax, 1, N_CHIPS), t.dtype) for s2, ax, t in zip(shapes, axes, pairs)]
    scratch = [pltpu.SemaphoreType.DMA((n, 3)), pltpu.SemaphoreType.DMA((n, 3))]

    def copies(ins, outs, scr):
        x, y, c = _position()
        return [
            _remote(
                _sub(ins[k], (), shapes[k], axes[k], chip=2 * chip[0] + chip[1]), outs[k].at[j],
                scr[0].at[k, j], scr[1].at[k, j], (*chip, c),
            )
            for k in range(n)
            for j, chip in enumerate(_other_chips(x, y))
        ]

    def start(ins, outs, scr):
        for cp in copies(ins, outs, scr):
            cp.start()

    def finish(ins, outs, scr):
        for cp in copies(ins, outs, scr):
            cp.wait()

    return _Rider(pairs, out_shapes, scratch, start, finish)


def _complete_rider(stacks, axes, layers):
    n = len(stacks)
    shapes = [t.shape[1:] for t in stacks]
    scratch = [pltpu.SemaphoreType.DMA((n,)), pltpu.SemaphoreType.DMA((n,))]

    def start(ins, outs, scr):
        x, y, c = _position()
        for k in range(n):
            mine = _sub(outs[k], (layers[k],), shapes[k], axes[k], half=c)
            _remote(mine, mine, scr[0].at[k], scr[1].at[k], (x, y, 1 - c)).start()

    def finish(ins, outs, scr):
        x, y, c = _position()
        for k in range(n):
            mine = _sub(outs[k], (layers[k],), shapes[k], axes[k], half=c)
            theirs = _sub(outs[k], (layers[k],), shapes[k], axes[k], half=1 - c)
            _remote(mine, theirs, scr[0].at[k], scr[1].at[k], (x, y, 1 - c)).wait_recv()
            _remote(mine, mine, scr[0].at[k], scr[1].at[k], (x, y, 1 - c)).wait_send()

    return _Rider(stacks, [SDS(t.shape, t.dtype) for t in stacks], scratch, start, finish, {k: k for k in range(n)})


def _small_rider(buf):
    rows, cols = buf.shape
    flips = [(fx, fy, fc) for fx in (0, 1) for fy in (0, 1) for fc in (0, 1)][1:]

    def peers():
        x, y, c = _position()
        return [(x + f[0] - 2 * x * f[0], y + f[1] - 2 * y * f[1], c + f[2] - 2 * c * f[2]) for f in flips]

    def start(ins, outs, scr):
        x, y, c = _position()
        mine = outs[0].at[4 * x + 2 * y + c]
        pltpu.make_async_copy(ins[0], mine, scr[2]).start()
        for j, peer in enumerate(peers()):
            _remote(ins[0], mine, scr[0].at[j], scr[1].at[j], peer).start()

    def finish(ins, outs, scr):
        x, y, c = _position()
        mine = outs[0].at[4 * x + 2 * y + c]
        for j, peer in enumerate(peers()):
            _remote(ins[0], outs[0].at[4 * peer[0] + 2 * peer[1] + peer[2]], scr[0].at[j], scr[1].at[j], peer).wait_recv()
        for j, peer in enumerate(peers()):
            _remote(ins[0], mine, scr[0].at[j], scr[1].at[j], peer).wait_send()
        pltpu.make_async_copy(ins[0], mine, scr[2]).wait()

    scratch = [pltpu.SemaphoreType.DMA((7,)), pltpu.SemaphoreType.DMA((7,)), pltpu.SemaphoreType.DMA(())]
    return _Rider([buf], [SDS((8, rows, cols), buf.dtype)], scratch, start, finish)


def _my_core():
    return lax.axis_index("c")


def _my_chip():
    return 2 * lax.axis_index("x") + lax.axis_index("y")


def _pair_sum(stack, layer, recv, axis, name):
    hk, hn = recv.shape
    tk = _divisor(hk, max(16, TILE_BYTES // (4 * hn)), 16)
    per = hk // tk

    def body(g_ref, r_ref, out_ref):
        out_ref[...] = (g_ref[...] + r_ref[...]).astype(out_ref.dtype)

    if axis == 1:
        mine = pl.BlockSpec((None, tk, hn), lambda i: (layer, _my_core() * per + i, 0))
    else:
        mine = pl.BlockSpec((None, tk, hn), lambda i: (layer, i, _my_core()))
    return pl.pallas_call(
        body,
        name=name,
        grid=(per,),
        in_specs=[mine, pl.BlockSpec((tk, hn), lambda i: (i, 0))],
        out_specs=pl.BlockSpec((tk, hn), lambda i: (i, 0)),
        out_shape=SDS((hk, hn), BF16),
        compiler_params=_params("parallel"),
    )(stack, recv)


def _chip_sum(pair, others, axis, stack, layer, name):
    _, sk, sn = others.shape
    tk = _divisor(sk, max(16, TILE_BYTES // (4 * sn)), 16)
    per = sk // tk

    def body(p_ref, o0_ref, o1_ref, o2_ref, *rest):
        out_ref = rest[-1]
        acc = p_ref[...].astype(F32) + o0_ref[...].astype(F32)
        out_ref[...] = (acc + o1_ref[...].astype(F32)) + o2_ref[...].astype(F32)

    if axis == 1:
        own = pl.BlockSpec((tk, sn), lambda i: (i, _my_chip()))
        out = pl.BlockSpec((None, tk, sn), lambda i: (layer, _my_core() * per + i, 0))
        shard = (2 * sk, sn)
    else:
        own = pl.BlockSpec((tk, sn), lambda i: (_my_chip() * per + i, 0))
        out = pl.BlockSpec((None, tk, sn), lambda i: (layer, i, _my_core()))
        shard = (sk, 2 * sn)
    other = lambda j: pl.BlockSpec((None, tk, sn), lambda i: (j, i, 0))
    in_specs = [own, other(0), other(1), other(2)]
    args = [pair, others, others, others]
    aliases = {}
    if stack is not None:
        in_specs.append(ANY)
        args.append(stack)
        aliases = {4: 0}
    return pl.pallas_call(
        body,
        name=name,
        grid=(per,),
        in_specs=in_specs,
        out_specs=out,
        out_shape=SDS((DEPTH,) + shard, F32),
        input_output_aliases=aliases,
        compiler_params=_params("parallel"),
    )(*args)


def _sum_devices(gathered, name):
    _, rows, cols = gathered.shape

    def body(g_ref, out_ref):
        acc = g_ref[0]
        for dev in range(1, 8):
            acc = acc + g_ref[dev]
        out_ref[...] = acc

    return pl.pallas_call(body, name=name, out_shape=SDS((rows, cols), F32))(gathered)


PACK_TILE = 8 * 128


def _pack(arrays):
    parts = []
    for t in arrays:
        flat = t.reshape(-1)
        pad = (-flat.shape[0]) % PACK_TILE
        if pad:
            flat = jnp.concatenate([flat, jnp.zeros((pad,), flat.dtype)])
        parts.append(flat.reshape(-1, 128))
    return jnp.concatenate(parts, axis=0)


def _unpack(buf, shapes):
    out, row = [], 0
    for shp in shapes:
        size = math.prod(shp)
        rows = -(-size // PACK_TILE) * 8
        out.append(buf[row : row + rows].reshape(-1)[:size].reshape(shp))
        row += rows
    return out


BIG = ("w_in", "w_br_gm", "w_br_hg", "w_out", "w_up", "w_down")
BIG_AXIS = (1, 1, 1, 0, 1, 0)
LAYER_SMALL = ("mix_norm", "gm_ln_g", "gm_ln_b", "gm_ws", "gm_bs", "lb", "hg_norm_g", "ffn_norm", "conv_w", "conv_b")
WEIGHTS = ("mix_norm", "w_in", "gm_ln_g", "gm_ln_b", "gm_ws", "gm_bs", "hg_lb_logits", "hg_norm_g", "w_br_gm", "w_br_hg", "w_out",
           "ffn_norm", "w_up", "conv_w", "conv_b", "w_down", "final_norm")


def kernel(x, mix_norm, w_in, gm_ln_g, gm_ln_b, gm_ws, gm_bs, hg_lb_logits, hg_norm_g, w_br_gm, w_br_hg, w_out, ffn_norm, w_up, conv_w, conv_b, w_down, final_norm, loss_target, m_mix_norm, m_w_in, m_gm_ln_g, m_gm_ln_b, m_gm_ws, m_gm_bs, m_hg_lb_logits, m_hg_norm_g, m_w_br_gm, m_w_br_hg, m_w_out, m_ffn_norm, m_w_up, m_conv_w, m_conv_b, m_w_down, m_final_norm, v_mix_norm, v_w_in, v_gm_ln_g, v_gm_ln_b, v_gm_ws, v_gm_bs, v_hg_lb_logits, v_hg_norm_g, v_w_br_gm, v_w_br_hg, v_w_out, v_ffn_norm, v_w_up, v_conv_w, v_conv_b, v_w_down, v_final_norm):
    w = dict(mix_norm=mix_norm, w_in=w_in, gm_ln_g=gm_ln_g, gm_ln_b=gm_ln_b, gm_ws=gm_ws, gm_bs=gm_bs, hg_lb_logits=hg_lb_logits,
             hg_norm_g=hg_norm_g, w_br_gm=w_br_gm, w_br_hg=w_br_hg, w_out=w_out, ffn_norm=ffn_norm, w_up=w_up, conv_w=conv_w,
             conv_b=conv_b, w_down=w_down, final_norm=final_norm)
    m = dict(mix_norm=m_mix_norm, w_in=m_w_in, gm_ln_g=m_gm_ln_g, gm_ln_b=m_gm_ln_b, gm_ws=m_gm_ws, gm_bs=m_gm_bs,
             hg_lb_logits=m_hg_lb_logits, hg_norm_g=m_hg_norm_g, w_br_gm=m_w_br_gm, w_br_hg=m_w_br_hg, w_out=m_w_out,
             ffn_norm=m_ffn_norm, w_up=m_w_up, conv_w=m_conv_w, conv_b=m_conv_b, w_down=m_w_down, final_norm=m_final_norm)
    v = dict(mix_norm=v_mix_norm, w_in=v_w_in, gm_ln_g=v_gm_ln_g, gm_ln_b=v_gm_ln_b, gm_ws=v_gm_ws, gm_bs=v_gm_bs,
             hg_lb_logits=v_hg_lb_logits, hg_norm_g=v_hg_norm_g, w_br_gm=v_w_br_gm, w_br_hg=v_w_br_hg, w_out=v_w_out,
             ffn_norm=v_ffn_norm, w_up=v_w_up, conv_w=v_conv_w, conv_b=v_conv_b, w_down=v_w_down, final_norm=v_final_norm)

    axes = list(BIG_AXIS)
    d = x.shape[2]

    shards = [w[k].astype(MM_DTYPE) for k in BIG]
    w_in_0, conv_full = _run_rider(_gather_rider(shards[:1], axes[:1], 0, conv_w=conv_w), "gather_weights_0")
    full = [dict(w_in=w_in_0)]
    lb = _lower_bounds(hg_lb_logits)
    xs = x[0]
    saved = []
    mixer = 4
    for l in range(DEPTH):
        fw = full[l]
        more = l + 1 < DEPTH
        bst = gm_bs[l].T
        rider = _gather_rider(shards[mixer:], axes[mixer:], 0) if l == 0 else None
        (z, h, a), got = _in_proj(
            xs, mix_norm[l : l + 1], fw["w_in"], gm_ln_g[l : l + 1], gm_ln_b[l : l + 1], gm_ws[l], bst, f"in_proj_{l}", rider
        )
        if l == 0:
            fw.update(zip(BIG[mixer:], got))
        rider = _gather_rider(shards[1:mixer], axes[1:mixer], 0) if l == 0 else None
        (o, b, states), got = _hgrn_fwd(z, lb[l : l + 1], hg_norm_g[l : l + 1], f"hgrn_fwd_{l}", rider)
        if l == 0:
            fw.update(zip(BIG[1:mixer], got))
        (x1,), _ = _mix_fwd(xs, a, b, z, fw["w_br_gm"], fw["w_br_hg"], fw["w_out"], f"mix_fwd_{l}")
        rider = _gather_rider(shards, axes, l + 1) if more else None
        (x2, z2, h2, conv), got = _ffn_fwd(
            x1, ffn_norm[l : l + 1], fw["w_up"], conv_full, conv_b[l : l + 1], fw["w_down"], l, f"ffn_fwd_{l}", rider
        )
        if more:
            full.append(dict(zip(BIG, got)))
        saved.append((xs, h, z, a, b, o, states, x1, h2, z2, conv, bst))
        xs = x2

    loss_tile, dx, d_final = _loss_head(xs, final_norm.reshape(1, d), loss_target[0])
    loss = lax.psum(loss_tile[0, 0], ("x", "y", "c"))

    big = {k: None for k in BIG}
    grads = [None] * len(BIG)
    per_layer = [None] * DEPTH
    mix_ids, ffn_ids = list(range(mixer)), list(range(mixer, len(BIG)))
    mix_axes, ffn_axes = axes[:mixer], axes[mixer:]
    mix_pairs = None
    mix_summed = None

    def pair_sums(ids, received, layer):
        return [_pair_sum(big[BIG[k]], layer, r, axes[k], f"pair_sum_{BIG[k]}_{layer}") for k, r in zip(ids, received)]

    def chip_sums(ids, pairs, others, layer):
        for k, p, ot in zip(ids, pairs, others):
            grads[k] = _chip_sum(p, ot, axes[k], grads[k], layer, f"chip_sum_{BIG[k]}_{layer}")

    for l in reversed(range(DEPTH)):
        x0, h, z, a, b, o, states, x1, h2, z2, conv, bst = saved[l]
        fw = full[l]
        dz2, act, dconv, dx1, d_ffn = _ffn_bwd(
            dx, z2, conv, conv_full, fw["w_down"], fw["w_up"], x1, ffn_norm[l : l + 1], l, f"ffn_bwd_{l}"
        )
        big["w_down"] = _matmul_tn(act, dx, big["w_down"], l, f"dw_down_{l}")
        big["w_up"] = _matmul_tn(h2, dz2, big["w_up"], l, f"dw_up_{l}")
        rider = _pair_rider([big[BIG[k]] for k in ffn_ids], ffn_axes, l)
        if mix_pairs is not None:
            rider = _join(rider, _chip_rider(mix_pairs, mix_axes))
        (y, dpa, dpb, db, dgates, dz_gm, d_lng, d_lnb, d_ws, d_bst), got = _mix_bwd(
            dx1, a, b, z, fw["w_br_gm"], fw["w_br_hg"], fw["w_out"], gm_ln_g[l : l + 1], gm_ln_b[l : l + 1], gm_ws[l], bst,
            f"mix_bwd_{l}", rider,
        )
        if mix_pairs is not None:
            chip_sums(mix_ids, mix_pairs, got[len(ffn_ids) :], l + 1)
            mix_summed = l + 1
        ffn_pairs = pair_sums(ffn_ids, got[: len(ffn_ids)], l)
        big["w_out"] = _matmul_tn(y, dx1, big["w_out"], l, f"dw_out_{l}")
        big["w_br_gm"] = _matmul_tn(a, dpa, big["w_br_gm"], l, f"dw_br_gm_{l}")
        big["w_br_hg"] = _matmul_tn(b, dpb, big["w_br_hg"], l, f"dw_br_hg_{l}")
        (dz_hg, d_lb, d_ng), others = _hgrn_bwd(
            db, z, o, states, lb[l : l + 1], hg_norm_g[l : l + 1], f"hgrn_bwd_{l}", _chip_rider(ffn_pairs, ffn_axes)
        )
        chip_sums(ffn_ids, ffn_pairs, others, l)
        n_in = fw["w_in"].shape[1]
        big["w_in"] = _matmul_tn(h, dz_gm, big["w_in"], l, f"dw_in_gm_{l}", n_total=n_in, col_off=0)
        big["w_in"] = _matmul_tn(h, dz_hg, big["w_in"], l, f"dw_in_hg_{l}", n_total=n_in, col_off=2 * BR_DIM)
        big["w_in"] = _matmul_tn(h, dgates, big["w_in"], l, f"dw_in_gate_{l}", n_total=n_in, col_off=6 * BR_DIM)
        done_ids = ffn_ids + (mix_ids if mix_summed is not None else [])
        done_layers = [l] * len(ffn_ids) + ([mix_summed] * len(mix_ids) if mix_summed is not None else [])
        rider = _join(
            _pair_rider([big[BIG[k]] for k in mix_ids], mix_axes, l),
            _complete_rider([grads[k] for k in done_ids], [axes[k] for k in done_ids], done_layers),
        )
        if l == 0:
            upper = [_pack([per_layer[j][k] for k in LAYER_SMALL]) for j in range(1, DEPTH)] + [_pack([d_final[0]])]
            rider = _join(rider, _small_rider(jnp.concatenate(upper, axis=0)))
        (dx, d_mix), got = _in_proj_bwd([dz_gm, dz_hg, dgates], fw["w_in"], x0, mix_norm[l : l + 1], dx1, f"in_proj_bwd_{l}", rider)
        for k, g in zip(done_ids, got[len(mix_ids) : len(mix_ids) + len(done_ids)]):
            grads[k] = g
        gathered_upper = got[len(mix_ids) + len(done_ids) :]
        mix_pairs = pair_sums(mix_ids, got[: len(mix_ids)], l)
        per_layer[l] = dict(
            mix_norm=d_mix[0], gm_ln_g=d_lng[0], gm_ln_b=d_lnb[0], gm_ws=d_ws, gm_bs=d_bst.T, lb=d_lb[0], hg_norm_g=d_ng[0],
            ffn_norm=d_ffn[0], conv_w=dconv[0:3], conv_b=dconv[3],
        )

    got = _run_rider(
        _join(_chip_rider(mix_pairs, mix_axes), _small_rider(_pack([per_layer[0][k] for k in LAYER_SMALL]))), "grad_exchange_0"
    )
    chip_sums(mix_ids, mix_pairs, got[: len(mix_ids)], 0)
    done = _run_rider(_complete_rider([grads[k] for k in mix_ids], mix_axes, [0] * len(mix_ids)), "grad_complete_0")
    for k, g in zip(mix_ids, done):
        grads[k] = g
    grad = dict(zip(BIG, grads))

    layer_shapes = [per_layer[0][k].shape for k in LAYER_SMALL]
    summed_upper = _unpack(_sum_devices(gathered_upper[0], "sum_small_upper"), layer_shapes * (DEPTH - 1) + [d_final[0].shape])
    summed_0 = _unpack(_sum_devices(got[len(mix_ids)], "sum_small_0"), layer_shapes)
    by_layer = [summed_0] + [summed_upper[j * len(LAYER_SMALL) : (j + 1) * len(LAYER_SMALL)] for j in range(DEPTH - 1)]
    small = {k: jnp.stack([by_layer[j][i] for j in range(DEPTH)]) for i, k in enumerate(LAYER_SMALL)}
    for k in LAYER_SMALL:
        if k != "lb":
            grad[k] = small[k]
    grad["hg_lb_logits"] = _lower_bounds_bwd(hg_lb_logits, small["lb"])
    grad["final_norm"] = summed_upper[-1]
    grad["conv_w"] = lax.dynamic_slice_in_dim(grad["conv_w"], _my_chip() * conv_w.shape[2], conv_w.shape[2], axis=2)

    delta, new_m, new_v = {}, {}, {}
    for k in WEIGHTS:
        delta[k], new_m[k], new_v[k] = _adamw(w[k], grad[k], m[k], v[k], f"adamw_{k}")

    return (loss, dx[None], *[grad[k] for k in WEIGHTS], *[delta[k] for k in WEIGHTS], *[new_m[k] for k in WEIGHTS],
            *[new_v[k] for k in WEIGHTS])
```
